```python
import numpy as np
import jax
import jax.numpy as jnp
from jax import lax


D_MODEL = 1024
BATCH = 8
SEQ = 2048
DEPTH = 2

ROPE_THETA = 10000.0
NORM_EPS = 1e-6
NEG_INF = -1e30
FORCED_SCORE = 1e4

MLSTM_HEADS = 4
MLSTM_DH = D_MODEL // 8
MLSTM_W = MLSTM_HEADS * MLSTM_DH
MLSTM_CHUNK = 64
MLSTM_CONV = 4
MLSTM_GATE_CAP = 15.0
MOBA_HEADS = 4
MOBA_DH = D_MODEL // 8
MOBA_W = MOBA_HEADS * MOBA_DH
MOBA_BLOCK = 256
MOBA_TOPK = 3
MOBA_QCHUNK = 16
EVEN_SPLITS = (2 * MLSTM_W, 3 * MLSTM_W, 4 * MLSTM_W, 4 * MLSTM_W + 2 * MLSTM_HEADS,
               4 * MLSTM_W + 2 * MLSTM_HEADS + MOBA_W, 4 * MLSTM_W + 2 * MLSTM_HEADS + 2 * MOBA_W)
EVEN_IN = 4 * MLSTM_W + 2 * MLSTM_HEADS + 3 * MOBA_W

NSA_HEADS = 8
NSA_GROUPS = 2
NSA_REP = NSA_HEADS // NSA_GROUPS
NSA_DH = D_MODEL // 8
NSA_W = NSA_HEADS * NSA_DH
NSA_KV_W = NSA_GROUPS * NSA_DH
CMP_LEN = 32
CMP_STRIDE = 16
CMP_HIDDEN = 256
SEL_BLOCK = 64
SEL_TOPN = 8
WINDOW = 512
NSA_QCHUNK = 32
ODD_SPLITS = (NSA_W, NSA_W + NSA_KV_W, NSA_W + 2 * NSA_KV_W, NSA_W + 3 * NSA_KV_W,
              NSA_W + 4 * NSA_KV_W, NSA_W + 5 * NSA_KV_W, NSA_W + 6 * NSA_KV_W)
ODD_IN = NSA_W + 6 * NSA_KV_W + 3 * NSA_HEADS

MOE_GROUPS = 4
MOE_EXPERTS_PER_GROUP = 4
MOE_EXPERTS = MOE_GROUPS * MOE_EXPERTS_PER_GROUP
MOE_TOPK = 2
MOE_HIDDEN = D_MODEL // 4

kernel_name = 'hybrid_mlstm_moba_nsa_hmoe'


def rms_norm(x, g):
    xf = x.astype(jnp.float32)
    y = xf * lax.rsqrt(jnp.mean(xf * xf, axis=-1, keepdims=True) + NORM_EPS)
    return (y * g.astype(jnp.float32)).astype(x.dtype)


def rotary(x):
    S, Dh = x.shape[-2], x.shape[-1]
    half = Dh // 2
    inv = ROPE_THETA ** (-(jnp.arange(half, dtype=jnp.float32) / half))
    ang = jnp.arange(S, dtype=jnp.float32)[:, None] * inv[None, :]
    cos, sin = jnp.cos(ang), jnp.sin(ang)
    xf = x.astype(jnp.float32)
    x1, x2 = xf[..., :half], xf[..., half:]
    return jnp.concatenate([x1 * cos - x2 * sin, x2 * cos + x1 * sin], axis=-1).astype(x.dtype)


def masked_softmax(s, mask):
    s = jnp.where(mask, s.astype(jnp.float32), NEG_INF)
    e = jnp.where(mask, jnp.exp(s - jnp.max(s, axis=-1, keepdims=True)), 0.0)
    return e / jnp.maximum(jnp.sum(e, axis=-1, keepdims=True), 1e-30)


def split_heads(t, n, d):
    B, S, _ = t.shape
    return t.reshape(B, S, n, d).transpose(0, 2, 1, 3)


def merge_heads(t):
    B, n, S, d = t.shape
    return t.transpose(0, 2, 1, 3).reshape(B, S, n * d)


def causal_depthwise_conv(x, w):
    K, C = w.shape
    return lax.conv_general_dilated(x, w[:, None, :].astype(x.dtype), window_strides=(1,),
                                    padding=[(K - 1, 0)], dimension_numbers=('NWC', 'WIO', 'NWC'),
                                    feature_group_count=C)


def mlstm_chunkwise(q, k, v, log_i, log_f):
    B, H, S, Dh = q.shape
    L = MLSTM_CHUNK
    NC = S // L
    f32 = jnp.float32
    q = q.astype(f32).reshape(B, H, NC, L, Dh) * (Dh ** -0.5)
    k = k.astype(f32).reshape(B, H, NC, L, Dh)
    v = v.astype(f32).reshape(B, H, NC, L, Dh)
    log_i = log_i.astype(f32).reshape(B, H, NC, L)
    b = jnp.cumsum(log_f.astype(f32).reshape(B, H, NC, L), axis=-1)
    b_last = b[..., -1]
    g = b_last[..., None] - b + log_i
    g_max = jnp.max(g, axis=-1)
    w = jnp.exp(g - g_max[..., None])
    kv = jnp.einsum('bhcs,bhcse,bhcsd->bhced', w, v, k)
    ksum = jnp.einsum('bhcs,bhcsd->bhcd', w, k)

    def step(carry, xs):
        C, n, m = carry
        kv_c, ks_c, gm_c, bl_c = xs
        m_new = jnp.maximum(bl_c + m, gm_c)
        a = jnp.exp(bl_c + m - m_new)
        c = jnp.exp(gm_c - m_new)
        C_new = a[..., None, None] * C + c[..., None, None] * kv_c
        n_new = a[..., None] * n + c[..., None] * ks_c
        return (C_new, n_new, m_new), (C, n, m)

    init = (jnp.zeros((B, H, Dh, Dh), f32), jnp.zeros((B, H, Dh), f32), jnp.zeros((B, H), f32))
    xs = (jnp.moveaxis(kv, 2, 0), jnp.moveaxis(ksum, 2, 0), jnp.moveaxis(g_max, 2, 0), jnp.moveaxis(b_last, 2, 0))
    _, (C_in, n_in, m_in) = lax.scan(step, init, xs)
    C_in = jnp.moveaxis(C_in, 0, 2)
    n_in = jnp.moveaxis(n_in, 0, 2)
    m_in = jnp.moveaxis(m_in, 0, 2)
    causal = jnp.tril(jnp.ones((L, L), dtype=bool))
    dmat = jnp.where(causal, b[..., :, None] - b[..., None, :] + log_i[..., None, :], NEG_INF)
    inter = b + m_in[..., None]
    m_t = jnp.maximum(inter, jnp.max(dmat, axis=-1))
    s = jnp.einsum('bhctd,bhcsd->bhcts', q, k) * jnp.exp(dmat - m_t[..., None])
    w_inter = jnp.exp(inter - m_t)
    num = jnp.einsum('bhcts,bhcse->bhcte', s, v) + w_inter[..., None] * jnp.einsum('bhced,bhctd->bhcte', C_in, q)
    den = jnp.sum(s, axis=-1) + w_inter * jnp.einsum('bhcd,bhctd->bhct', n_in, q)
    h = num / jnp.maximum(jnp.abs(den), jnp.exp(-m_t))[..., None]
    return h.reshape(B, H, S, Dh)


def moba_attention(q, k, v):
    B, H, S, Dh = q.shape
    BS = MOBA_BLOCK
    NB = -(-S // BS)
    pad = NB * BS - S
    kp = jnp.pad(k, ((0, 0), (0, 0), (0, pad), (0, 0)))
    vp = jnp.pad(v, ((0, 0), (0, 0), (0, pad), (0, 0)))
    kb = kp.reshape(B, H, NB, BS, Dh)
    vb = vp.reshape(B, H, NB, BS, Dh)
    counts = np.minimum(BS, S - np.arange(NB) * BS).astype(np.float32)
    k_mean = jnp.sum(kb.astype(jnp.float32), axis=3) / counts[:, None]
    gate = jnp.einsum('bhsd,bhnd->bhsn', q.astype(jnp.float32), k_mean)
    pos = jnp.arange(S)
    past = jnp.arange(NB)[None, :] < (pos // BS)[:, None]
    gate = jnp.where(past, gate, NEG_INF)
    k_sel = min(MOBA_TOPK, NB)
    top_val, top_idx = lax.top_k(gate, k_sel)
    top_ok = top_val > 0.5 * NEG_INF
    scale = Dh ** -0.5
    Q = MOBA_QCHUNK
    bi = jnp.arange(B)[:, None, None, None]
    hi = jnp.arange(H)[None, :, None, None]

    def chunk(c):
        t0 = c * Q
        tq = t0 + jnp.arange(Q)
        qc = lax.dynamic_slice_in_dim(q, t0, Q, axis=2)
        idx = lax.dynamic_slice_in_dim(top_idx, t0, Q, axis=2)
        ok = lax.dynamic_slice_in_dim(top_ok, t0, Q, axis=2)
        kg = kb[bi, hi, idx]
        vg = vb[bi, hi, idx].reshape(B, H, Q, k_sel * BS, Dh)
        s_sel = jnp.einsum('bhqd,bhqnpd->bhqnp', qc, kg).reshape(B, H, Q, k_sel * BS)
        m_sel = jnp.repeat(ok, BS, axis=-1)
        blk0 = (t0 // BS) * BS
        k_own = lax.dynamic_slice_in_dim(kp, blk0, BS, axis=2)
        v_own = lax.dynamic_slice_in_dim(vp, blk0, BS, axis=2)
        s_own = jnp.einsum('bhqd,bhpd->bhqp', qc, k_own)
        m_own = (blk0 + jnp.arange(BS))[None, :] <= tq[:, None]
        s_all = jnp.concatenate([s_sel, s_own], axis=-1).astype(jnp.float32) * scale
        m_all = jnp.concatenate([m_sel, jnp.broadcast_to(m_own, (B, H, Q, BS))], axis=-1)
        p = masked_softmax(s_all, m_all).astype(v.dtype)
        return (jnp.einsum('bhqk,bhqkd->bhqd', p[..., :k_sel * BS], vg)
                + jnp.einsum('bhqp,bhpd->bhqd', p[..., k_sel * BS:], v_own))

    outs = lax.map(chunk, jnp.arange(S // Q))
    return outs.transpose(1, 2, 0, 3, 4).reshape(B, H, S, Dh)


def even_mixer(h, w_in, conv_w, gate_b, head_gain, w_out):
    B, S, _ = h.shape
    z = h @ w_in
    qk_m, v_m, o_m, if_m, q_b, k_b, v_b = jnp.split(z, EVEN_SPLITS, axis=-1)
    qk_m = jax.nn.silu(causal_depthwise_conv(qk_m, conv_w))
    q_m, k_m = jnp.split(qk_m, 2, axis=-1)
    pre = if_m.astype(jnp.float32) + gate_b.astype(jnp.float32)
    pre = MLSTM_GATE_CAP * jnp.tanh(pre / MLSTM_GATE_CAP)
    log_i = pre[..., :MLSTM_HEADS].transpose(0, 2, 1)
    log_f = jax.nn.log_sigmoid(pre[..., MLSTM_HEADS:]).transpose(0, 2, 1)
    h_t = mlstm_chunkwise(split_heads(q_m, MLSTM_HEADS, MLSTM_DH), split_heads(k_m, MLSTM_HEADS, MLSTM_DH),
                          split_heads(v_m, MLSTM_HEADS, MLSTM_DH), log_i, log_f)
    h_t = (h_t * lax.rsqrt(jnp.mean(h_t * h_t, axis=-1, keepdims=True) + NORM_EPS)
           * head_gain.astype(jnp.float32).reshape(MLSTM_HEADS, 1, MLSTM_DH))
    h_m = (merge_heads(h_t) * jax.nn.sigmoid(o_m.astype(jnp.float32))).astype(h.dtype)
    o_b = moba_attention(rotary(split_heads(q_b, MOBA_HEADS, MOBA_DH)), rotary(split_heads(k_b, MOBA_HEADS, MOBA_DH)),
                         split_heads(v_b, MOBA_HEADS, MOBA_DH))
    y = jnp.concatenate([h_m, merge_heads(o_b).astype(h.dtype)], axis=-1)
    return y @ w_out


def odd_mixer(h, w_in, cmp_pos, w_ck1, w_ck2, w_cv1, w_cv2, w_out):
    B, S, _ = h.shape
    dt = h.dtype
    G, R, Dh = NSA_GROUPS, NSA_REP, NSA_DH
    scale = Dh ** -0.5
    z = h @ w_in
    q, k_c, v_c, k_s, v_s, k_w, v_w, gate = jnp.split(z, ODD_SPLITS, axis=-1)
    q = q.reshape(B, S, G, R, Dh).transpose(0, 2, 3, 1, 4)
    k_c, v_c, k_s, v_s, k_w, v_w = [split_heads(t, G, Dh) for t in (k_c, v_c, k_s, v_s, k_w, v_w)]
    n_cmp = (S - CMP_LEN) // CMP_STRIDE + 1
    cmp_start = np.arange(n_cmp) * CMP_STRIDE
    cmp_end = cmp_start + CMP_LEN - 1
    blk_idx = cmp_start[:, None] + np.arange(CMP_LEN)[None, :]

    def compress(t, pe, w1, w2):
        tb = (t[:, :, blk_idx] + pe).reshape(B, G, n_cmp, CMP_LEN * Dh)
        return jax.nn.gelu(tb @ w1) @ w2

    k_cmp = compress(k_c, cmp_pos[0], w_ck1, w_ck2)
    v_cmp = compress(v_c, cmp_pos[1], w_cv1, w_cv2)
    pos = jnp.arange(S)
    cmp_ok = cmp_end[None, :] <= pos[:, None]
    p_cmp = masked_softmax(jnp.einsum('bgrsd,bgnd->bgrsn', q, k_cmp) * scale, cmp_ok)
    o_cmp = jnp.einsum('bgrsn,bgnd->bgrsd', p_cmp.astype(dt), v_cmp)
    n_blk = S // SEL_BLOCK
    blk_lo = np.arange(n_blk) * SEL_BLOCK
    overlap = ((cmp_start[:, None] <= blk_lo[None, :] + SEL_BLOCK - 1)
               & (cmp_end[:, None] >= blk_lo[None, :])).astype(np.float32)
    imp = jnp.einsum('bgrsn,nj->bgsj', p_cmp, overlap)
    cur = (pos // SEL_BLOCK)[:, None]
    jb = jnp.arange(n_blk)[None, :]
    forced = (jb == 0) | (jb == cur) | (jb == cur - 1)
    imp = jnp.where(jb <= cur, jnp.where(forced, FORCED_SCORE, imp), NEG_INF)
    n_sel = min(SEL_TOPN, n_blk)
    sel_val, sel_idx = lax.top_k(imp, n_sel)
    sel_ok = sel_val > 0.5 * NEG_INF
    q_r = rotary(q)
    k_sb = rotary(k_s).reshape(B, G, n_blk, SEL_BLOCK, Dh)
    v_sb = v_s.reshape(B, G, n_blk, SEL_BLOCK, Dh)
    k_wp = jnp.pad(rotary(k_w), ((0, 0), (0, 0), (WINDOW, 0), (0, 0)))
    v_wp = jnp.pad(v_w, ((0, 0), (0, 0), (WINDOW, 0), (0, 0)))
    Q = NSA_QCHUNK
    bi = jnp.arange(B)[:, None, None, None]
    gi = jnp.arange(G)[None, :, None, None]

    def chunk(c):
        t0 = c * Q
        tq = t0 + jnp.arange(Q)
        qc = lax.dynamic_slice_in_dim(q_r, t0, Q, axis=3)
        idx = lax.dynamic_slice_in_dim(sel_idx, t0, Q, axis=2)
        ok = lax.dynamic_slice_in_dim(sel_ok, t0, Q, axis=2)
        kg = k_sb[bi, gi, idx].reshape(B, G, Q, n_sel * SEL_BLOCK, Dh)
        vg = v_sb[bi, gi, idx].reshape(B, G, Q, n_sel * SEL_BLOCK, Dh)
        tk = (idx[..., None] * SEL_BLOCK + jnp.arange(SEL_BLOCK)).reshape(B, G, Q, n_sel * SEL_BLOCK)
        m_sel = jnp.repeat(ok, SEL_BLOCK, axis=-1) & (tk <= tq[:, None])
        p_s = masked_softmax(jnp.einsum('bgrqd,bgqkd->bgrqk', qc, kg) * scale, m_sel[:, :, None])
        o_s = jnp.einsum('bgrqk,bgqkd->bgrqd', p_s.astype(dt), vg)
        k_win = lax.dynamic_slice_in_dim(k_wp, t0, WINDOW + Q, axis=2)
        v_win = lax.dynamic_slice_in_dim(v_wp, t0, WINDOW + Q, axis=2)
        tw = t0 - WINDOW + jnp.arange(WINDOW + Q)
        gap = tq[:, None] - tw[None, :]
        m_win = (tw[None, :] >= 0) & (gap >= 0) & (gap < WINDOW)
        p_w = masked_softmax(jnp.einsum('bgrqd,bgkd->bgrqk', qc, k_win) * scale, m_win)
        o_w = jnp.einsum('bgrqk,bgkd->bgrqd', p_w.astype(dt), v_win)
        return o_s, o_w

    o_sel, o_win = lax.map(chunk, jnp.arange(S // Q))
    o_sel = o_sel.transpose(1, 2, 3, 0, 4, 5).reshape(B, G, R, S, Dh)
    o_win = o_win.transpose(1, 2, 3, 0, 4, 5).reshape(B, G, R, S, Dh)
    g = jax.nn.sigmoid(gate.astype(jnp.float32)).reshape(B, S, G, R, 3).transpose(0, 2, 3, 1, 4).astype(dt)
    o = g[..., 0:1] * o_cmp + g[..., 1:2] * o_sel + g[..., 2:3] * o_win
    o = o.transpose(0, 3, 1, 2, 4).reshape(B, S, NSA_W)
    return o @ w_out


def hier_moe(x, w_g, b_g, w_e, b_e, w1, w3, w2):
    B, S, D = x.shape
    T = B * S
    xt = x.reshape(T, D)
    xf = xt.astype(jnp.float32)
    g_logits = xf @ w_g.astype(jnp.float32) + b_g.astype(jnp.float32)
    g_prob = jax.nn.softmax(g_logits, axis=-1)
    g_top = jnp.argmax(g_logits, axis=-1)
    g_w = jnp.take_along_axis(g_prob, g_top[:, None], axis=-1)
    e_logits = (xf @ w_e.astype(jnp.float32) + b_e.astype(jnp.float32)).reshape(T, MOE_GROUPS, MOE_EXPERTS_PER_GROUP)
    e_in = jnp.take_along_axis(e_logits, g_top[:, None, None], axis=1)[:, 0]
    top_v, top_i = lax.top_k(e_in, MOE_TOPK)
    w_top = jax.nn.softmax(top_v, axis=-1) * g_w
    eid = g_top[:, None] * MOE_EXPERTS_PER_GROUP + top_i
    combine = jnp.einsum('tk,tke->te', w_top, jax.nn.one_hot(eid, MOE_EXPERTS, dtype=jnp.float32))
    out = jnp.zeros((T, D), jnp.float32)
    for e in range(MOE_EXPERTS):
        hdn = jax.nn.silu(xt @ w1[e]) * (xt @ w3[e])
        out = out + combine[:, e:e + 1] * (hdn @ w2[e]).astype(jnp.float32)
    return out.astype(x.dtype).reshape(B, S, D)


def setup_inputs(seed: int = 0) -> dict:
    key = jax.random.key(seed)
    keys = iter(jax.random.split(key, 64))
    D = D_MODEL

    def nrm(shape, scale):
        return jax.random.normal(next(keys), shape, jnp.float32) * scale

    def gain(n):
        return 1.0 + nrm((n,), 0.02)

    inp = {}
    inp['x'] = nrm((BATCH, SEQ, D), 1.0)
    inp['mix_norm_0'] = gain(D)
    inp['w_in_0'] = nrm((D, EVEN_IN), D ** -0.5)
    inp['mlstm_conv_0'] = nrm((MLSTM_CONV, 2 * MLSTM_W), MLSTM_CONV ** -0.5)
    inp['mlstm_gate_b_0'] = jnp.concatenate([nrm((MLSTM_HEADS,), 0.1),
                                             jnp.linspace(3.0, 6.0, MLSTM_HEADS, dtype=jnp.float32) + nrm((MLSTM_HEADS,), 0.1)])
    inp['mlstm_head_norm_0'] = gain(MLSTM_W)
    inp['w_out_0'] = nrm((MLSTM_W + MOBA_W, D), (MLSTM_W + MOBA_W) ** -0.5)

    def add_moe(i):
        inp[f'ffn_norm_{i}'] = gain(D)
        inp[f'router_group_{i}'] = nrm((D, MOE_GROUPS), D ** -0.5)
        inp[f'router_group_b_{i}'] = nrm((MOE_GROUPS,), 0.01)
        inp[f'router_expert_{i}'] = nrm((D, MOE_EXPERTS), D ** -0.5)
        inp[f'router_expert_b_{i}'] = nrm((MOE_EXPERTS,), 0.01)
        inp[f'moe_w1_{i}'] = nrm((MOE_EXPERTS, D, MOE_HIDDEN), D ** -0.5)
        inp[f'moe_w3_{i}'] = nrm((MOE_EXPERTS, D, MOE_HIDDEN), D ** -0.5)
        inp[f'moe_w2_{i}'] = nrm((MOE_EXPERTS, MOE_HIDDEN, D), MOE_HIDDEN ** -0.5)

    add_moe(0)
    inp['mix_norm_1'] = gain(D)
    inp['w_in_1'] = nrm((D, ODD_IN), D ** -0.5)
    inp['nsa_cmp_pos_1'] = nrm((2, CMP_LEN, NSA_DH), 0.1)
    inp['nsa_cmp_k1_1'] = nrm((CMP_LEN * NSA_DH, CMP_HIDDEN), (CMP_LEN * NSA_DH) ** -0.5)
    inp['nsa_cmp_k2_1'] = nrm((CMP_HIDDEN, NSA_DH), CMP_HIDDEN ** -0.5)
    inp['nsa_cmp_v1_1'] = nrm((CMP_LEN * NSA_DH, CMP_HIDDEN), (CMP_LEN * NSA_DH) ** -0.5)
    inp['nsa_cmp_v2_1'] = nrm((CMP_HIDDEN, NSA_DH), CMP_HIDDEN ** -0.5)
    inp['w_out_1'] = nrm((NSA_W, D), NSA_W ** -0.5)
    add_moe(1)
    inp['final_norm'] = gain(D)
    return inp


def reference(x, mix_norm_0, w_in_0, mlstm_conv_0, mlstm_gate_b_0, mlstm_head_norm_0, w_out_0,
              ffn_norm_0, router_group_0, router_group_b_0, router_expert_0, router_expert_b_0,
              moe_w1_0, moe_w3_0, moe_w2_0,
              mix_norm_1, w_in_1, nsa_cmp_pos_1, nsa_cmp_k1_1, nsa_cmp_k2_1, nsa_cmp_v1_1, nsa_cmp_v2_1, w_out_1,
              ffn_norm_1, router_group_1, router_group_b_1, router_expert_1, router_expert_b_1,
              moe_w1_1, moe_w3_1, moe_w2_1,
              final_norm):
    mix_norms = (mix_norm_0, mix_norm_1)
    mixers = ((even_mixer, (w_in_0, mlstm_conv_0, mlstm_gate_b_0, mlstm_head_norm_0, w_out_0)),
              (odd_mixer, (w_in_1, nsa_cmp_pos_1, nsa_cmp_k1_1, nsa_cmp_k2_1, nsa_cmp_v1_1, nsa_cmp_v2_1, w_out_1)))
    ffn_norms = (ffn_norm_0, ffn_norm_1)
    moes = ((router_group_0, router_group_b_0, router_expert_0, router_expert_b_0, moe_w1_0, moe_w3_0, moe_w2_0),
            (router_group_1, router_group_b_1, router_expert_1, router_expert_b_1, moe_w1_1, moe_w3_1, moe_w2_1))
    h = x
    for layer in range(DEPTH):
        mixer, params = mixers[layer]
        h = h + mixer(rms_norm(h, mix_norms[layer]), *params)
        h = h + hier_moe(rms_norm(h, ffn_norms[layer]), *moes[layer])
    return rms_norm(h, final_norm)
```

```python
import functools

import numpy as np
import jax
import jax.numpy as jnp
from jax import lax
from jax.experimental import pallas as pl
from jax.experimental.pallas import tpu as pltpu

F32 = jnp.float32
BF16 = jnp.bfloat16
HIGHEST = lax.Precision.HIGHEST

LANES = 128
D_MODEL = 1024
HEAD_DIM = 128
ROPE_THETA = 10000.0
NORM_EPS = 1e-6
NEG_INF = -1e30
FORCED_SCORE = 1e4

MLSTM_HEADS = 4
MLSTM_W = MLSTM_HEADS * HEAD_DIM
MLSTM_CHUNK = 64
MLSTM_CONV = 4
MLSTM_GATE_CAP = 15.0
MOBA_HEADS = 4
MOBA_W = MOBA_HEADS * HEAD_DIM
MOBA_BLOCK = 256
MOBA_TOPK = 3

NSA_HEADS = 8
NSA_GROUPS = 2
NSA_REP = NSA_HEADS // NSA_GROUPS
NSA_W = NSA_HEADS * HEAD_DIM
NSA_KV_W = NSA_GROUPS * HEAD_DIM
CMP_LEN = 32
CMP_STRIDE = 16
CMP_HIDDEN = 256
SEL_BLOCK = 64
SEL_TOPN = 8
WINDOW = 512
NSA_TQ = 256

MOE_GROUPS = 4
MOE_EPG = 4
MOE_EXPERTS = MOE_GROUPS * MOE_EPG
MOE_HIDDEN = D_MODEL // 4

EVEN_N = 4 * MLSTM_W + 3 * MOBA_W + LANES
EVEN_IF_BLK = (4 * MLSTM_W + 3 * MOBA_W) // LANES
ODD_GATE_BLK = (NSA_W + 6 * NSA_KV_W) // LANES
ODD_N = NSA_W + 6 * NSA_KV_W + NSA_GROUPS * LANES

VMEM_LIMIT = 56 * 1024 * 1024

_NT = (((1,), (1,)), ((), ()))
_TN = (((0,), (0,)), ((), ()))


def _cparams(sem):
    return pltpu.CompilerParams(dimension_semantics=sem, vmem_limit_bytes=VMEM_LIMIT)


def _rms(x, g):
    return x * lax.rsqrt(jnp.mean(x * x, axis=-1, keepdims=True) + NORM_EPS) * g


def _sigmoid(x):
    return 1.0 / (1.0 + jnp.exp(-x))


def _log_sigmoid(x):
    return -(jnp.maximum(-x, 0.0) + jnp.log1p(jnp.exp(-jnp.abs(x))))


def _rotate(x, cos2, sin2):
    return x * cos2 + pltpu.roll(x, HEAD_DIM // 2, axis=1) * sin2


def _rank_before(v, n_valid):
    lane = lax.broadcasted_iota(jnp.int32, v.shape, 1)
    rank = jnp.zeros(v.shape, F32)
    for m in range(n_valid):
        vm = v[:, m:m + 1]
        ahead = (vm > v) | ((vm == v) & (lane > m))
        rank = rank + jnp.where(ahead, 1.0, 0.0)
    return rank


def _norm_matmul_body(*refs, n_chunk, with_t):
    if with_t:
        x_ref, g_ref, w_ref, wt_ref, o_ref, ot_ref = refs
    else:
        x_ref, g_ref, w_ref, o_ref = refs
    yb = _rms(x_ref[...], g_ref[...]).astype(BF16)
    n = w_ref.shape[1]
    for c0 in range(0, n, n_chunk):
        c1 = min(n, c0 + n_chunk)
        o_ref[:, c0:c1] = jnp.dot(yb, w_ref[:, c0:c1], preferred_element_type=F32)
    if with_t:
        ot_ref[...] = lax.dot_general(wt_ref[...], yb, _NT, preferred_element_type=F32)


def _norm_matmul(x2d, gain, w, wt=None, tm=512):
    t, d = x2d.shape
    n = w.shape[1]
    with_t = wt is not None
    in_specs = [pl.BlockSpec((tm, d), lambda i: (i, 0)),
                pl.BlockSpec((1, d), lambda i: (0, 0)),
                pl.BlockSpec((d, n), lambda i: (0, 0))]
    out_specs = [pl.BlockSpec((tm, n), lambda i: (i, 0))]
    out_shape = [jax.ShapeDtypeStruct((t, n), F32)]
    args = [x2d, gain.reshape(1, d), w]
    if with_t:
        r = wt.shape[0]
        in_specs.append(pl.BlockSpec((r, d), lambda i: (0, 0)))
        out_specs.append(pl.BlockSpec((r, tm), lambda i: (0, i)))
        out_shape.append(jax.ShapeDtypeStruct((r, t), F32))
        args.append(wt)
    outs = pl.pallas_call(
        functools.partial(_norm_matmul_body, n_chunk=512, with_t=with_t),
        grid=(t // tm,), in_specs=in_specs, out_specs=out_specs, out_shape=out_shape,
        compiler_params=_cparams(("parallel",)), name="norm_in_proj")(*args)
    return outs if with_t else outs[0]


def _proj_residual_body(*refs, n_in):
    res_ref = refs[0]
    a_refs = refs[1:1 + n_in]
    w_refs = refs[1 + n_in:1 + 2 * n_in]
    o_ref = refs[1 + 2 * n_in]
    acc = res_ref[...]
    for a_ref, w_ref in zip(a_refs, w_refs):
        acc = acc + jnp.dot(a_ref[...].astype(BF16), w_ref[...], preferred_element_type=F32)
    o_ref[...] = acc


def _proj_residual(res2d, acts, ws, tm=512):
    t, d = res2d.shape
    n_in = len(acts)
    in_specs = [pl.BlockSpec((tm, d), lambda i: (i, 0))]
    in_specs += [pl.BlockSpec((tm, a.shape[1]), lambda i: (i, 0)) for a in acts]
    in_specs += [pl.BlockSpec(w.shape, lambda i: (0, 0)) for w in ws]
    return pl.pallas_call(
        functools.partial(_proj_residual_body, n_in=n_in),
        grid=(t // tm,), in_specs=in_specs,
        out_specs=pl.BlockSpec((tm, d), lambda i: (i, 0)),
        out_shape=jax.ShapeDtypeStruct((t, d), F32),
        compiler_params=_cparams(("parallel",)), name="out_proj_residual")(res2d, *acts, *ws)


def _chunk_cumsum(x, axis):
    idx = lax.broadcasted_iota(jnp.int32, x.shape, axis) % MLSTM_CHUNK
    d = 1
    while d < MLSTM_CHUNK:
        x = x + jnp.where(idx >= d, pltpu.roll(x, d, axis=axis), 0.0)
        d *= 2
    return x


def _mlstm_body(q_ref, k_ref, v_ref, og_ref, if_ref, gt_ref, cwq_ref, cwk_ref, bcol_ref, brow_ref,
                gain_ref, out_ref, qs, ks, bcs, lics, brs, lirs):
    h = pl.program_id(1)
    s_len = q_ref.shape[1]
    n_chunks = s_len // MLSTM_CHUNK
    L = MLSTM_CHUNK
    row = lax.broadcasted_iota(jnp.int32, (s_len, HEAD_DIM), 0)

    def conv_silu(x, w):
        acc = x * w[MLSTM_CONV - 1:MLSTM_CONV, :]
        for d in range(1, MLSTM_CONV):
            shifted = jnp.where(row >= d, pltpu.roll(x, d, axis=0), 0.0)
            acc = acc + shifted * w[MLSTM_CONV - 1 - d:MLSTM_CONV - d, :]
        return acc * _sigmoid(acc)

    qs[...] = conv_silu(q_ref[0], cwq_ref[...]) * (HEAD_DIM ** -0.5)
    ks[...] = conv_silu(k_ref[0], cwk_ref[...])

    pre = if_ref[0] + bcol_ref[...]
    pre = MLSTM_GATE_CAP * jnp.tanh(pre / MLSTM_GATE_CAP)
    lane = lax.broadcasted_iota(jnp.int32, pre.shape, 1)
    b_all = _chunk_cumsum(_log_sigmoid(pre), 0)
    lics[...] = jnp.sum(jnp.where(lane == h, pre, 0.0), axis=1, keepdims=True)
    bcs[...] = jnp.sum(jnp.where(lane == h + MLSTM_HEADS, b_all, 0.0), axis=1, keepdims=True)

    pr = gt_ref[...] + brow_ref[...]
    pr = MLSTM_GATE_CAP * jnp.tanh(pr / MLSTM_GATE_CAP)
    sub = lax.broadcasted_iota(jnp.int32, pr.shape, 0)
    b_rows = _chunk_cumsum(_log_sigmoid(pr), 1)
    li_row = jnp.sum(jnp.where(sub == h, pr, 0.0), axis=0, keepdims=True)
    b_row = jnp.sum(jnp.where(sub == h + MLSTM_HEADS, b_rows, 0.0), axis=0, keepdims=True)
    for c in range(n_chunks):
        brs[c] = b_row[:, c * L:(c + 1) * L]
        lirs[c] = li_row[:, c * L:(c + 1) * L]

    tri = (lax.broadcasted_iota(jnp.int32, (L, L), 1) <= lax.broadcasted_iota(jnp.int32, (L, L), 0))
    gain = gain_ref[...]

    def chunk(c, carry):
        c_state, n_state, m_state = carry
        r0 = pl.multiple_of(c * L, L)
        q = qs[pl.ds(r0, L), :]
        k = ks[pl.ds(r0, L), :]
        v = v_ref[0, pl.ds(r0, L), :]
        b_col = bcs[pl.ds(r0, L), :]
        li_col = lics[pl.ds(r0, L), :]
        b_r = brs[c]
        li_r = lirs[c]
        qb = q.astype(BF16)
        kb = k.astype(BF16)
        b_last = b_col[L - 1:L, :]
        g = b_last - b_col + li_col
        g_max = jnp.max(g, axis=0, keepdims=True)
        w = jnp.exp(g - g_max)
        kv = lax.dot_general((w * v).astype(BF16), kb, _TN, preferred_element_type=F32)
        ksum = jnp.sum(w * k, axis=0, keepdims=True)
        dmat = jnp.where(tri, b_col - b_r + li_r, NEG_INF)
        inter = b_col + m_state
        m_t = jnp.maximum(inter, jnp.max(dmat, axis=1, keepdims=True))
        s = lax.dot_general(qb, kb, _NT, preferred_element_type=F32) * jnp.exp(dmat - m_t)
        w_inter = jnp.exp(inter - m_t)
        num = (jnp.dot(s.astype(BF16), v.astype(BF16), preferred_element_type=F32)
               + w_inter * lax.dot_general(qb, c_state.astype(BF16), _NT, preferred_element_type=F32))
        den = jnp.sum(s, axis=1, keepdims=True) + w_inter * jnp.sum(q * n_state, axis=1, keepdims=True)
        ht = num / jnp.maximum(jnp.abs(den), jnp.exp(-m_t))
        ht = ht * lax.rsqrt(jnp.mean(ht * ht, axis=1, keepdims=True) + NORM_EPS) * gain
        out_ref[0, pl.ds(r0, L), :] = ht * _sigmoid(og_ref[0, pl.ds(r0, L), :])
        m_new = jnp.maximum(b_last + m_state, g_max)
        a = jnp.exp(b_last + m_state - m_new)
        cc = jnp.exp(g_max - m_new)
        return a * c_state + cc * kv, a * n_state + cc * ksum, m_new

    init = (jnp.zeros((HEAD_DIM, HEAD_DIM), F32), jnp.zeros((1, HEAD_DIM), F32), jnp.zeros((1, 1), F32))
    lax.fori_loop(0, n_chunks, chunk, init)


def _mlstm(z3, gt, conv_w, gate_b, head_gain):
    b, s_len, _ = z3.shape
    n_chunks = s_len // MLSTM_CHUNK
    gr = gt.shape[0]
    bcol = jnp.zeros((1, LANES), F32).at[0, :2 * MLSTM_HEADS].set(gate_b)
    brow = jnp.zeros((gr, 1), F32).at[:2 * MLSTM_HEADS, 0].set(gate_b)

    def col(off):
        return pl.BlockSpec((1, s_len, HEAD_DIM), lambda bi, hi: (bi, 0, off + hi))

    in_specs = [col(0), col(MLSTM_HEADS), col(2 * MLSTM_HEADS), col(3 * MLSTM_HEADS),
                pl.BlockSpec((1, s_len, LANES), lambda bi, hi: (bi, 0, EVEN_IF_BLK)),
                pl.BlockSpec((gr, s_len), lambda bi, hi: (0, bi)),
                pl.BlockSpec((MLSTM_CONV, HEAD_DIM), lambda bi, hi: (0, hi)),
                pl.BlockSpec((MLSTM_CONV, HEAD_DIM), lambda bi, hi: (0, MLSTM_HEADS + hi)),
                pl.BlockSpec((1, LANES), lambda bi, hi: (0, 0)),
                pl.BlockSpec((gr, 1), lambda bi, hi: (0, 0)),
                pl.BlockSpec((1, HEAD_DIM), lambda bi, hi: (0, hi))]
    scratch = [pltpu.VMEM((s_len, HEAD_DIM), F32), pltpu.VMEM((s_len, HEAD_DIM), F32),
               pltpu.VMEM((s_len, 1), F32), pltpu.VMEM((s_len, 1), F32),
               pltpu.VMEM((n_chunks, 1, MLSTM_CHUNK), F32), pltpu.VMEM((n_chunks, 1, MLSTM_CHUNK), F32)]
    return pl.pallas_call(
        _mlstm_body, grid=(b, MLSTM_HEADS), in_specs=in_specs,
        out_specs=pl.BlockSpec((1, s_len, HEAD_DIM), lambda bi, hi: (bi, 0, hi)),
        out_shape=jax.ShapeDtypeStruct((b, s_len, MLSTM_W), F32),
        scratch_shapes=scratch, compiler_params=_cparams(("parallel", "parallel")),
        name="mlstm")(z3, z3, z3, z3, z3, gt, conv_w, conv_w, bcol, brow, head_gain.reshape(1, MLSTM_W))


def _moba_body(q_ref, k_ref, v_ref, cos_ref, sin_ref, o_ref, kr_scr, vb_scr, km_scr):
    i = pl.program_id(2)
    s_len = k_ref.shape[1]
    bs = MOBA_BLOCK
    nb = s_len // bs

    @pl.when(i == 0)
    def _():
        kr = _rotate(k_ref[0], cos_ref[...], sin_ref[...])
        kr_scr[...] = kr.astype(BF16)
        vb_scr[...] = v_ref[0].astype(BF16)
        rows = [jnp.sum(kr[n * bs:(n + 1) * bs, :], axis=0, keepdims=True) / float(bs) for n in range(nb)]
        rows.append(jnp.zeros((LANES - nb, HEAD_DIM), F32))
        km_scr[...] = jnp.concatenate(rows, axis=0)

    t0 = pl.multiple_of(i * bs, bs)
    qr = _rotate(q_ref[0], cos_ref[pl.ds(t0, bs), :], sin_ref[pl.ds(t0, bs), :])
    gate = lax.dot_general(qr, km_scr[...], _NT, precision=HIGHEST, preferred_element_type=F32)
    lane = lax.broadcasted_iota(jnp.int32, gate.shape, 1)
    val = jnp.where(lane < i, gate, NEG_INF)
    sel = (_rank_before(val, nb) < MOBA_TOPK) & (val > 0.5 * NEG_INF)
    sel_f = jnp.where(sel, 1.0, 0.0)

    s = lax.dot_general(qr.astype(BF16), kr_scr[...], _NT, preferred_element_type=F32) * (HEAD_DIM ** -0.5)
    q_pos = t0 + lax.broadcasted_iota(jnp.int32, (bs, bs), 0)
    col = lax.broadcasted_iota(jnp.int32, (bs, bs), 1)
    pieces = []
    for n in range(nb):
        picked = jnp.broadcast_to(sel_f[:, n:n + 1] > 0.5, (bs, bs))
        key_pos = n * bs + col
        pieces.append(picked | ((key_pos <= q_pos) & (key_pos >= t0)))
    mask = jnp.concatenate(pieces, axis=1)
    s = jnp.where(mask, s, NEG_INF)
    e = jnp.where(mask, jnp.exp(s - jnp.max(s, axis=1, keepdims=True)), 0.0)
    denom = jnp.maximum(jnp.sum(e, axis=1, keepdims=True), 1e-30)
    o_ref[0] = jnp.dot(e.astype(BF16), vb_scr[...], preferred_element_type=F32) / denom


def _moba(z3, cos2, sin2):
    b, s_len, _ = z3.shape
    nb = s_len // MOBA_BLOCK
    q_off = 4 * MLSTM_W // LANES

    def kv(off):
        return pl.BlockSpec((1, s_len, HEAD_DIM), lambda bi, hi, i: (bi, 0, off + hi))

    in_specs = [pl.BlockSpec((1, MOBA_BLOCK, HEAD_DIM), lambda bi, hi, i: (bi, i, q_off + hi)),
                kv(q_off + MOBA_HEADS), kv(q_off + 2 * MOBA_HEADS),
                pl.BlockSpec((s_len, HEAD_DIM), lambda bi, hi, i: (0, 0)),
                pl.BlockSpec((s_len, HEAD_DIM), lambda bi, hi, i: (0, 0))]
    scratch = [pltpu.VMEM((s_len, HEAD_DIM), BF16), pltpu.VMEM((s_len, HEAD_DIM), BF16),
               pltpu.VMEM((LANES, HEAD_DIM), F32)]
    return pl.pallas_call(
        _moba_body, grid=(b, MOBA_HEADS, nb), in_specs=in_specs,
        out_specs=pl.BlockSpec((1, MOBA_BLOCK, HEAD_DIM), lambda bi, hi, i: (bi, i, hi)),
        out_shape=jax.ShapeDtypeStruct((b, s_len, MOBA_W), F32),
        scratch_shapes=scratch, compiler_params=_cparams(("parallel", "parallel", "arbitrary")),
        name="moba")(z3, z3, z3, cos2, sin2)


def _gelu_tanh(x):
    return x * (0.5 * (1.0 + jnp.tanh(np.sqrt(2.0 / np.pi) * (x + 0.044715 * (x * x * x)))))


def _nsa_compress_body(kc_ref, vc_ref, pe_ref, wk1_ref, wk2_ref, wv1_ref, wv2_ref, ko_ref, vo_ref):
    n_rows = kc_ref.shape[1] // CMP_STRIDE
    halves = CMP_LEN // CMP_STRIDE
    assert halves == 2

    def compress(x_ref, pe, w1_ref, w2_ref):
        ya = jnp.zeros((n_rows, CMP_HIDDEN), F32)
        yb = jnp.zeros((n_rows, CMP_HIDDEN), F32)
        for l in range(CMP_STRIDE):
            r = x_ref[0, pl.ds(l, n_rows, stride=CMP_STRIDE), :]
            la, lb = l, CMP_STRIDE + l
            ya = ya + jnp.dot((r + pe[la:la + 1, :]).astype(BF16), w1_ref[la * HEAD_DIM:(la + 1) * HEAD_DIM, :],
                              preferred_element_type=F32)
            yb = yb + jnp.dot((r + pe[lb:lb + 1, :]).astype(BF16), w1_ref[lb * HEAD_DIM:(lb + 1) * HEAD_DIM, :],
                              preferred_element_type=F32)
        pre = ya + pltpu.roll(yb, n_rows - 1, axis=0)
        return jnp.dot(_gelu_tanh(pre).astype(BF16), w2_ref[...], preferred_element_type=F32)

    ko_ref[0, 0] = compress(kc_ref, pe_ref[0], wk1_ref, wk2_ref)
    vo_ref[0, 0] = compress(vc_ref, pe_ref[1], wv1_ref, wv2_ref)


def _nsa_compress(z3, cmp_pos, wk1, wk2, wv1, wv2):
    b, s_len, _ = z3.shape
    n_rows = s_len // CMP_STRIDE
    kc_off = NSA_W // LANES
    vc_off = kc_off + NSA_GROUPS
    const2 = lambda bi, gi: (0, 0)
    in_specs = [pl.BlockSpec((1, s_len, HEAD_DIM), lambda bi, gi: (bi, 0, kc_off + gi)),
                pl.BlockSpec((1, s_len, HEAD_DIM), lambda bi, gi: (bi, 0, vc_off + gi)),
                pl.BlockSpec(cmp_pos.shape, lambda bi, gi: (0, 0, 0)),
                pl.BlockSpec(wk1.shape, const2), pl.BlockSpec(wk2.shape, const2),
                pl.BlockSpec(wv1.shape, const2), pl.BlockSpec(wv2.shape, const2)]
    out_spec = pl.BlockSpec((1, 1, n_rows, HEAD_DIM), lambda bi, gi: (bi, gi, 0, 0))
    out_sds = jax.ShapeDtypeStruct((b, NSA_GROUPS, n_rows, HEAD_DIM), F32)
    return pl.pallas_call(
        _nsa_compress_body, grid=(b, NSA_GROUPS), in_specs=in_specs,
        out_specs=[out_spec, out_spec], out_shape=[out_sds, out_sds],
        compiler_params=_cparams(("parallel", "parallel")), name="nsa_compress")(
            z3, z3, cmp_pos, wk1, wk2, wv1, wv2)


def _nsa_body(q_ref, ks_ref, vs_ref, kw_ref, vw_ref, kc_ref, vc_ref, gate_ref, cos_ref, sin_ref, ov_ref,
              o_ref, krs, vsb, krw, vwb):
    i = pl.program_id(2)
    tq = NSA_TQ
    tk = NSA_TQ
    scale = HEAD_DIM ** -0.5
    n_sel_blk = ks_ref.shape[1] // SEL_BLOCK
    n_cmp = (ks_ref.shape[1] - CMP_LEN) // CMP_STRIDE + 1
    shift = SEL_BLOCK.bit_length() - 1
    assert 1 << shift == SEL_BLOCK and n_sel_blk <= LANES

    @pl.when(i == 0)
    def _():
        krs[...] = _rotate(ks_ref[0], cos_ref[...], sin_ref[...]).astype(BF16)
        krw[...] = _rotate(kw_ref[0], cos_ref[...], sin_ref[...]).astype(BF16)
        vsb[...] = vs_ref[0].astype(BF16)
        vwb[...] = vw_ref[0].astype(BF16)

    t0 = pl.multiple_of(i * tq, tq)
    pos = t0 + lax.broadcasted_iota(jnp.int32, (tq, 1), 0)
    lane = lax.broadcasted_iota(jnp.int32, (tq, LANES), 1)
    cos_q = cos_ref[pl.ds(t0, tq), :]
    sin_q = sin_ref[pl.ds(t0, tq), :]

    kc = kc_ref[0, 0].astype(BF16)
    vc = vc_ref[0, 0].astype(BF16)
    cmp_ok = (lane * CMP_STRIDE + (CMP_LEN - 1) <= pos) & (lane < n_cmp)
    q_heads, qr_heads, o_cmp = [], [], []
    p_sum = jnp.zeros((tq, LANES), F32)
    for r in range(NSA_REP):
        q = q_ref[0, :, r * HEAD_DIM:(r + 1) * HEAD_DIM]
        qr_heads.append(_rotate(q, cos_q, sin_q).astype(BF16))
        s = lax.dot_general(q.astype(BF16), kc, _NT, preferred_element_type=F32) * scale
        s = jnp.where(cmp_ok, s, NEG_INF)
        e = jnp.where(cmp_ok, jnp.exp(s - jnp.max(s, axis=1, keepdims=True)), 0.0)
        p = e / jnp.maximum(jnp.sum(e, axis=1, keepdims=True), 1e-30)
        o_cmp.append(jnp.dot(p.astype(BF16), vc, preferred_element_type=F32))
        p_sum = p_sum + p
    imp = jnp.dot(p_sum, ov_ref[...], precision=HIGHEST, preferred_element_type=F32)

    cur = pos >> shift
    forced = (lane == 0) | (lane == cur) | (lane == cur - 1)
    val = jnp.where(lane <= cur, jnp.where(forced, FORCED_SCORE, imp), NEG_INF)
    sel = (_rank_before(val, n_sel_blk) < SEL_TOPN) & (val > 0.5 * NEG_INF)
    sel_b = jnp.where(sel, 1.0, 0.0).astype(BF16)

    def sel_tile(kt, carry):
        k0 = pl.multiple_of(kt * tk, tk)
        k_t = krs[pl.ds(k0, tk), :]
        v_t = vsb[pl.ds(k0, tk), :]
        blk_of_key = (k0 + lax.broadcasted_iota(jnp.int32, (LANES, tk), 1)) >> shift
        expand = jnp.where(blk_of_key == lax.broadcasted_iota(jnp.int32, (LANES, tk), 0), 1.0, 0.0).astype(BF16)
        picked = jnp.dot(sel_b, expand, preferred_element_type=F32)
        key_pos = k0 + lax.broadcasted_iota(jnp.int32, (tq, tk), 1)
        mask = (picked > 0.5) & (key_pos <= pos)
        out = []
        for r in range(NSA_REP):
            m_prev, l_prev, acc = carry[3 * r:3 * r + 3]
            s = lax.dot_general(qr_heads[r], k_t, _NT, preferred_element_type=F32) * scale
            s = jnp.where(mask, s, NEG_INF)
            m_new = jnp.maximum(m_prev, jnp.max(s, axis=1, keepdims=True))
            alpha = jnp.exp(m_prev - m_new)
            e = jnp.where(mask, jnp.exp(s - m_new), 0.0)
            out += [m_new, alpha * l_prev + jnp.sum(e, axis=1, keepdims=True),
                    alpha * acc + jnp.dot(e.astype(BF16), v_t, preferred_element_type=F32)]
        return tuple(out)

    init = (jnp.full((tq, 1), NEG_INF, F32), jnp.zeros((tq, 1), F32), jnp.zeros((tq, HEAD_DIM), F32)) * NSA_REP
    fin = lax.fori_loop(0, i + 1, sel_tile, init)

    span = WINDOW + tq
    start = pl.multiple_of(jnp.maximum(t0 - WINDOW, 0), tq)
    k_w = krw[pl.ds(start, span), :]
    v_w = vwb[pl.ds(start, span), :]
    gap = pos - (start + lax.broadcasted_iota(jnp.int32, (tq, span), 1))
    m_win = (gap >= 0) & (gap < WINDOW)

    gates = _sigmoid(gate_ref[0])
    for r in range(NSA_REP):
        s = lax.dot_general(qr_heads[r], k_w, _NT, preferred_element_type=F32) * scale
        s = jnp.where(m_win, s, NEG_INF)
        e = jnp.where(m_win, jnp.exp(s - jnp.max(s, axis=1, keepdims=True)), 0.0)
        o_win = (jnp.dot(e.astype(BF16), v_w, preferred_element_type=F32)
                 / jnp.maximum(jnp.sum(e, axis=1, keepdims=True), 1e-30))
        o_sel = fin[3 * r + 2] / jnp.maximum(fin[3 * r + 1], 1e-30)
        g0, g1, g2 = (gates[:, 3 * r + c:3 * r + c + 1] for c in range(3))
        o_ref[0, :, r * HEAD_DIM:(r + 1) * HEAD_DIM] = g0 * o_cmp[r] + g1 * o_sel + g2 * o_win


def _nsa_attention(z3, k_cmp, v_cmp, cos2, sin2):
    b, s_len, _ = z3.shape
    assert s_len >= WINDOW + NSA_TQ and s_len % NSA_TQ == 0
    n_cmp = (s_len - CMP_LEN) // CMP_STRIDE + 1
    n_blk = s_len // SEL_BLOCK
    n_rows = k_cmp.shape[2]
    assert n_rows == LANES
    cmp_start = np.arange(n_rows) * CMP_STRIDE
    cmp_end = cmp_start + CMP_LEN - 1
    blk_lo = np.arange(LANES) * SEL_BLOCK
    overlap = ((cmp_start[:, None] <= blk_lo[None, :] + SEL_BLOCK - 1) & (cmp_end[:, None] >= blk_lo[None, :])
               & (np.arange(n_rows)[:, None] < n_cmp) & (np.arange(LANES)[None, :] < n_blk)).astype(np.float32)
    base = NSA_W // LANES

    def kv(off):
        return pl.BlockSpec((1, s_len, HEAD_DIM), lambda bi, gi, i: (bi, 0, base + off * NSA_GROUPS + gi))

    cmp_spec = pl.BlockSpec((1, 1, n_rows, HEAD_DIM), lambda bi, gi, i: (bi, gi, 0, 0))
    const2 = lambda bi, gi, i: (0, 0)
    in_specs = [pl.BlockSpec((1, NSA_TQ, NSA_REP * HEAD_DIM), lambda bi, gi, i: (bi, i, gi)),
                kv(2), kv(3), kv(4), kv(5), cmp_spec, cmp_spec,
                pl.BlockSpec((1, NSA_TQ, LANES), lambda bi, gi, i: (bi, i, ODD_GATE_BLK + gi)),
                pl.BlockSpec((s_len, HEAD_DIM), const2), pl.BlockSpec((s_len, HEAD_DIM), const2),
                pl.BlockSpec((n_rows, LANES), const2)]
    scratch = [pltpu.VMEM((s_len, HEAD_DIM), BF16) for _ in range(4)]
    return pl.pallas_call(
        _nsa_body, grid=(b, NSA_GROUPS, s_len // NSA_TQ), in_specs=in_specs,
        out_specs=pl.BlockSpec((1, NSA_TQ, NSA_REP * HEAD_DIM), lambda bi, gi, i: (bi, i, gi)),
        out_shape=jax.ShapeDtypeStruct((b, s_len, NSA_W), F32),
        scratch_shapes=scratch, compiler_params=_cparams(("parallel", "parallel", "arbitrary")),
        name="nsa_attention")(z3, z3, z3, z3, z3, k_cmp, v_cmp, z3, cos2, sin2, jnp.asarray(overlap))


def _moe_body(x_ref, g_ref, wr_ref, br_ref, w1_ref, w3_ref, w2_ref, fg_ref, o_ref, xn_scr, cw_scr, acc_scr,
              *, final_norm):
    j = pl.program_id(1)
    tm = x_ref.shape[0]
    lane = lax.broadcasted_iota(jnp.int32, (tm, LANES), 1).astype(F32)
    neg = float("-inf")

    @pl.when(j == 0)
    def _():
        xn = _rms(x_ref[...], g_ref[...])
        xn_scr[...] = xn.astype(BF16)
        logits = jnp.dot(xn, wr_ref[...], precision=HIGHEST, preferred_element_type=F32) + br_ref[...]
        is_g = lane < MOE_GROUPS
        gl = jnp.where(is_g, logits, neg)
        g_max = jnp.max(gl, axis=1, keepdims=True)
        g_w = 1.0 / jnp.sum(jnp.where(is_g, jnp.exp(gl - g_max), 0.0), axis=1, keepdims=True)
        g_top = jnp.min(jnp.where(gl == g_max, lane, float(LANES)), axis=1, keepdims=True)
        lo = MOE_GROUPS + MOE_EPG * g_top
        el = jnp.where((lane >= lo) & (lane < lo + MOE_EPG), logits, neg)
        v1 = jnp.max(el, axis=1, keepdims=True)
        i1 = jnp.min(jnp.where(el == v1, lane, float(LANES)), axis=1, keepdims=True)
        el2 = jnp.where(lane == i1, neg, el)
        v2 = jnp.max(el2, axis=1, keepdims=True)
        i2 = jnp.min(jnp.where(el2 == v2, lane, float(LANES)), axis=1, keepdims=True)
        e2 = jnp.exp(v2 - v1)
        den = 1.0 + e2
        cw_scr[...] = (jnp.where(lane == i1, g_w / den, 0.0) + jnp.where(lane == i2, g_w * e2 / den, 0.0))
        acc_scr[...] = jnp.zeros_like(acc_scr)

    xn = xn_scr[...]
    cw = cw_scr[...]
    acc = jnp.zeros(acc_scr.shape, F32)
    for r in range(MOE_EPG):
        h1 = jnp.dot(xn, w1_ref[r], preferred_element_type=F32)
        h3 = jnp.dot(xn, w3_ref[r], preferred_element_type=F32)
        e_lane = (MOE_GROUPS + MOE_EPG * j + r).astype(F32)
        col = jnp.sum(jnp.where(lane == e_lane, cw, 0.0), axis=1, keepdims=True)
        hd = (h1 * _sigmoid(h1)) * h3 * col
        acc = acc + jnp.dot(hd.astype(BF16), w2_ref[r], preferred_element_type=F32)
    acc_scr[...] += acc

    @pl.when(j == pl.num_programs(1) - 1)
    def _():
        out = x_ref[...] + acc_scr[...]
        if final_norm:
            out = _rms(out, fg_ref[...])
        o_ref[...] = out


def _moe(x2d, gain, w_g, b_g, w_e, b_e, w1, w3, w2, final_gain, final_norm, tm=1024):
    t, d = x2d.shape
    wr = jnp.zeros((d, LANES), F32).at[:, :MOE_GROUPS].set(w_g).at[:, MOE_GROUPS:MOE_GROUPS + MOE_EXPERTS].set(w_e)
    br = jnp.zeros((1, LANES), F32).at[0, :MOE_GROUPS].set(b_g).at[0, MOE_GROUPS:MOE_GROUPS + MOE_EXPERTS].set(b_e)
    in_specs = [pl.BlockSpec((tm, d), lambda i, j: (i, 0)),
                pl.BlockSpec((1, d), lambda i, j: (0, 0)),
                pl.BlockSpec((d, LANES), lambda i, j: (0, 0)),
                pl.BlockSpec((1, LANES), lambda i, j: (0, 0)),
                pl.BlockSpec((MOE_EPG, d, MOE_HIDDEN), lambda i, j: (j, 0, 0)),
                pl.BlockSpec((MOE_EPG, d, MOE_HIDDEN), lambda i, j: (j, 0, 0)),
                pl.BlockSpec((MOE_EPG, MOE_HIDDEN, d), lambda i, j: (j, 0, 0)),
                pl.BlockSpec((1, d), lambda i, j: (0, 0))]
    scratch = [pltpu.VMEM((tm, d), BF16), pltpu.VMEM((tm, LANES), F32), pltpu.VMEM((tm, d), F32)]
    return pl.pallas_call(
        functools.partial(_moe_body, final_norm=final_norm),
        grid=(t // tm, MOE_GROUPS), in_specs=in_specs,
        out_specs=pl.BlockSpec((tm, d), lambda i, j: (i, 0)),
        out_shape=jax.ShapeDtypeStruct((t, d), F32),
        scratch_shapes=scratch, compiler_params=_cparams(("parallel", "arbitrary")),
        name="hier_moe")(x2d, gain.reshape(1, d), wr, br, w1.astype(BF16), w3.astype(BF16), w2.astype(BF16),
                         final_gain.reshape(1, d))


def _rope_tables(s_len):
    half = HEAD_DIM // 2
    inv = ROPE_THETA ** (-(jnp.arange(half, dtype=F32) / half))
    ang = jnp.arange(s_len, dtype=F32)[:, None] * inv[None, :]
    cos, sin = jnp.cos(ang), jnp.sin(ang)
    return jnp.concatenate([cos, cos], axis=-1), jnp.concatenate([-sin, sin], axis=-1)


def _even_weights(w_in):
    n_if = 2 * MLSTM_HEADS
    a = 4 * MLSTM_W
    w_main = jnp.concatenate([w_in[:, :a], w_in[:, a + n_if:],
                              w_in[:, a:a + n_if], jnp.zeros((w_in.shape[0], LANES - n_if), w_in.dtype)], axis=1)
    w_if_t = jnp.concatenate([w_in[:, a:a + n_if].T, jnp.zeros((16 - n_if, w_in.shape[0]), w_in.dtype)], axis=0)
    return w_main.astype(BF16), w_if_t.astype(BF16)


def _odd_weights(w_in):
    a = NSA_W + 6 * NSA_KV_W
    per_group = NSA_REP * 3
    pad = jnp.zeros((w_in.shape[0], LANES - per_group), w_in.dtype)
    cols = [w_in[:, :a]]
    for g in range(NSA_GROUPS):
        cols += [w_in[:, a + g * per_group:a + (g + 1) * per_group], pad]
    return jnp.concatenate(cols, axis=1).astype(BF16)


def kernel(x, mix_norm_0, w_in_0, mlstm_conv_0, mlstm_gate_b_0, mlstm_head_norm_0, w_out_0, ffn_norm_0, router_group_0, router_group_b_0, router_expert_0, router_expert_b_0, moe_w1_0, moe_w3_0, moe_w2_0, mix_norm_1, w_in_1, nsa_cmp_pos_1, nsa_cmp_k1_1, nsa_cmp_k2_1, nsa_cmp_v1_1, nsa_cmp_v2_1, w_out_1, ffn_norm_1, router_group_1, router_group_b_1, router_expert_1, router_expert_b_1, moe_w1_1, moe_w3_1, moe_w2_1, final_norm):
    b, s_len, d = x.shape
    t = b * s_len
    cos2, sin2 = _rope_tables(s_len)
    x2d = x.reshape(t, d)

    w_main, w_if_t = _even_weights(w_in_0)
    z0, gt = _norm_matmul(x2d, mix_norm_0, w_main, w_if_t)
    z0 = z0.reshape(b, s_len, EVEN_N)
    h_m = _mlstm(z0, gt, mlstm_conv_0, mlstm_gate_b_0, mlstm_head_norm_0)
    o_b = _moba(z0, cos2, sin2)
    w_out = w_out_0.astype(BF16)
    h = _proj_residual(x2d, [h_m.reshape(t, MLSTM_W), o_b.reshape(t, MOBA_W)], [w_out[:MLSTM_W], w_out[MLSTM_W:]])
    h = _moe(h, ffn_norm_0, router_group_0, router_group_b_0, router_expert_0, router_expert_b_0,
             moe_w1_0, moe_w3_0, moe_w2_0, final_norm, False)

    z1 = _norm_matmul(h, mix_norm_1, _odd_weights(w_in_1)).reshape(b, s_len, ODD_N)
    k_cmp, v_cmp = _nsa_compress(z1, nsa_cmp_pos_1, nsa_cmp_k1_1.astype(BF16), nsa_cmp_k2_1.astype(BF16),
                                 nsa_cmp_v1_1.astype(BF16), nsa_cmp_v2_1.astype(BF16))
    o = _nsa_attention(z1, k_cmp, v_cmp, cos2, sin2)
    h = _proj_residual(h, [o.reshape(t, NSA_W)], [w_out_1.astype(BF16)])
    h = _moe(h, ffn_norm_1, router_group_1, router_group_b_1, router_expert_1, router_expert_b_1,
             moe_w1_1, moe_w3_1, moe_w2_1, final_norm, True)
    return h.reshape(b, s_len, d)
```

```python
import functools

import numpy as np
import jax
import jax.numpy as jnp
from jax import lax
from jax.experimental import pallas as pl
from jax.experimental.pallas import tpu as pltpu

F32 = jnp.float32
BF16 = jnp.bfloat16
HIGHEST = lax.Precision.HIGHEST

LANES = 128
BF16_SUBLANES = 16
D_MODEL = 1024
HEAD_DIM = 128
ROPE_THETA = 10000.0
NORM_EPS = 1e-6
NEG_INF = -1e30
FORCED_SCORE = 1e4

MLSTM_HEADS = 4
MLSTM_W = MLSTM_HEADS * HEAD_DIM
MLSTM_CHUNK = 64
MLSTM_CONV = 4
MLSTM_GATE_CAP = 15.0
MOBA_HEADS = 4
MOBA_W = MOBA_HEADS * HEAD_DIM
MOBA_BLOCK = 256
MOBA_TOPK = 3

NSA_HEADS = 8
NSA_GROUPS = 2
NSA_REP = NSA_HEADS // NSA_GROUPS
NSA_W = NSA_HEADS * HEAD_DIM
NSA_KV_W = NSA_GROUPS * HEAD_DIM
CMP_LEN = 32
CMP_STRIDE = 16
CMP_HIDDEN = 256
SEL_BLOCK = 64
SEL_TOPN = 8
WINDOW = 512
NSA_TQ = 256

MOE_GROUPS = 4
MOE_EPG = 4
MOE_EXPERTS = MOE_GROUPS * MOE_EPG
MOE_HIDDEN = D_MODEL // 4

EVEN_N = 4 * MLSTM_W + 3 * MOBA_W + LANES
EVEN_IF_BLK = (4 * MLSTM_W + 3 * MOBA_W) // LANES
ODD_GATE_BLK = (NSA_W + 6 * NSA_KV_W) // LANES
ODD_N = NSA_W + 6 * NSA_KV_W + NSA_GROUPS * LANES

VMEM_LIMIT = 56 * 1024 * 1024

_NT = (((1,), (1,)), ((), ()))
_TN = (((0,), (0,)), ((), ()))


def _cparams(sem):
    return pltpu.CompilerParams(dimension_semantics=sem, vmem_limit_bytes=VMEM_LIMIT)


def _rms(x, g):
    return x * lax.rsqrt(jnp.mean(x * x, axis=-1, keepdims=True) + NORM_EPS) * g


def _sigmoid(x):
    return 1.0 / (1.0 + jnp.exp(-x))


def _log_sigmoid(x):
    return -(jnp.maximum(-x, 0.0) + jnp.log1p(jnp.exp(-jnp.abs(x))))


def _rotate(x, cos2, sin2):
    return x * cos2 + pltpu.roll(x, HEAD_DIM // 2, axis=1) * sin2


def _rank_before(v, n_valid, axis):
    idx = lax.broadcasted_iota(jnp.int32, v.shape, axis)
    rank = jnp.zeros(v.shape, F32)
    for m in range(n_valid):
        vm = v[m:m + 1, :] if axis == 0 else v[:, m:m + 1]
        tie = jnp.where(idx > m, 1.0, 0.0)
        rank = rank + jnp.where(vm > v, 1.0, jnp.where(vm == v, tie, 0.0))
    return rank


def _norm_matmul_body(*refs, n_chunk, with_t):
    if with_t:
        x_ref, g_ref, w_ref, wt_ref, o_ref, ot_ref = refs
    else:
        x_ref, g_ref, w_ref, o_ref = refs
    yb = _rms(x_ref[...], g_ref[...]).astype(BF16)
    n = w_ref.shape[1]
    for c0 in range(0, n, n_chunk):
        c1 = min(n, c0 + n_chunk)
        o_ref[:, c0:c1] = jnp.dot(yb, w_ref[:, c0:c1], preferred_element_type=F32)
    if with_t:
        ot_ref[...] = lax.dot_general(wt_ref[...], yb, _NT, preferred_element_type=F32)


def _norm_matmul(x2d, gain, w, wt=None, tm=512):
    t, d = x2d.shape
    n = w.shape[1]
    with_t = wt is not None
    in_specs = [pl.BlockSpec((tm, d), lambda i: (i, 0)),
                pl.BlockSpec((1, d), lambda i: (0, 0)),
                pl.BlockSpec((d, n), lambda i: (0, 0))]
    out_specs = [pl.BlockSpec((tm, n), lambda i: (i, 0))]
    out_shape = [jax.ShapeDtypeStruct((t, n), F32)]
    args = [x2d, gain.reshape(1, d), w]
    if with_t:
        r = wt.shape[0]
        in_specs.append(pl.BlockSpec((r, d), lambda i: (0, 0)))
        out_specs.append(pl.BlockSpec((r, tm), lambda i: (0, i)))
        out_shape.append(jax.ShapeDtypeStruct((r, t), F32))
        args.append(wt)
    outs = pl.pallas_call(
        functools.partial(_norm_matmul_body, n_chunk=512, with_t=with_t),
        grid=(t // tm,), in_specs=in_specs, out_specs=out_specs, out_shape=out_shape,
        compiler_params=_cparams(("parallel",)), name="norm_in_proj")(*args)
    return outs if with_t else outs[0]


def _proj_residual_body(*refs, n_in):
    res_ref = refs[0]
    a_refs = refs[1:1 + n_in]
    w_refs = refs[1 + n_in:1 + 2 * n_in]
    o_ref = refs[1 + 2 * n_in]
    acc = res_ref[...]
    for a_ref, w_ref in zip(a_refs, w_refs):
        acc = acc + jnp.dot(a_ref[...].astype(BF16), w_ref[...], preferred_element_type=F32)
    o_ref[...] = acc


def _proj_residual(res2d, acts, ws, tm=512):
    t, d = res2d.shape
    n_in = len(acts)
    in_specs = [pl.BlockSpec((tm, d), lambda i: (i, 0))]
    in_specs += [pl.BlockSpec((tm, a.shape[1]), lambda i: (i, 0)) for a in acts]
    in_specs += [pl.BlockSpec(w.shape, lambda i: (0, 0)) for w in ws]
    return pl.pallas_call(
        functools.partial(_proj_residual_body, n_in=n_in),
        grid=(t // tm,), in_specs=in_specs,
        out_specs=pl.BlockSpec((tm, d), lambda i: (i, 0)),
        out_shape=jax.ShapeDtypeStruct((t, d), F32),
        compiler_params=_cparams(("parallel",)), name="out_proj_residual")(res2d, *acts, *ws)


def _chunk_cumsum(x, axis):
    idx = lax.broadcasted_iota(jnp.int32, x.shape, axis) % MLSTM_CHUNK
    d = 1
    while d < MLSTM_CHUNK:
        x = x + jnp.where(idx >= d, pltpu.roll(x, d, axis=axis), 0.0)
        d *= 2
    return x


def _mlstm_body(q_ref, k_ref, v_ref, og_ref, if_ref, gt_ref, cwq_ref, cwk_ref, bcol_ref, brow_ref,
                gain_ref, out_ref, qs, ks, bcs, lics, brs, lirs):
    h = pl.program_id(1)
    s_len = q_ref.shape[1]
    n_chunks = s_len // MLSTM_CHUNK
    L = MLSTM_CHUNK
    row = lax.broadcasted_iota(jnp.int32, (s_len, HEAD_DIM), 0)

    def conv_silu(x, w):
        acc = x * w[MLSTM_CONV - 1:MLSTM_CONV, :]
        for d in range(1, MLSTM_CONV):
            shifted = jnp.where(row >= d, pltpu.roll(x, d, axis=0), 0.0)
            acc = acc + shifted * w[MLSTM_CONV - 1 - d:MLSTM_CONV - d, :]
        return acc * _sigmoid(acc)

    qs[...] = conv_silu(q_ref[0], cwq_ref[...]) * (HEAD_DIM ** -0.5)
    ks[...] = conv_silu(k_ref[0], cwk_ref[...])

    pre = if_ref[0] + bcol_ref[...]
    pre = MLSTM_GATE_CAP * jnp.tanh(pre / MLSTM_GATE_CAP)
    lane = lax.broadcasted_iota(jnp.int32, pre.shape, 1)
    b_all = _chunk_cumsum(_log_sigmoid(pre), 0)
    lics[...] = jnp.sum(jnp.where(lane == h, pre, 0.0), axis=1, keepdims=True)
    bcs[...] = jnp.sum(jnp.where(lane == h + MLSTM_HEADS, b_all, 0.0), axis=1, keepdims=True)

    pr = gt_ref[...] + brow_ref[...]
    pr = MLSTM_GATE_CAP * jnp.tanh(pr / MLSTM_GATE_CAP)
    sub = lax.broadcasted_iota(jnp.int32, pr.shape, 0)
    b_rows = _chunk_cumsum(_log_sigmoid(pr), 1)
    li_row = jnp.sum(jnp.where(sub == h, pr, 0.0), axis=0, keepdims=True)
    b_row = jnp.sum(jnp.where(sub == h + MLSTM_HEADS, b_rows, 0.0), axis=0, keepdims=True)
    for c in range(n_chunks):
        brs[c] = b_row[:, c * L:(c + 1) * L]
        lirs[c] = li_row[:, c * L:(c + 1) * L]

    tri = (lax.broadcasted_iota(jnp.int32, (L, L), 1) <= lax.broadcasted_iota(jnp.int32, (L, L), 0))
    gain = gain_ref[...]

    def chunk(c, carry):
        c_state, n_state, m_state = carry
        r0 = pl.multiple_of(c * L, L)
        q = qs[pl.ds(r0, L), :]
        k = ks[pl.ds(r0, L), :]
        v = v_ref[0, pl.ds(r0, L), :]
        b_col = bcs[pl.ds(r0, L), :]
        li_col = lics[pl.ds(r0, L), :]
        b_r = brs[c]
        li_r = lirs[c]
        qb = q.astype(BF16)
        kb = k.astype(BF16)
        b_last = b_col[L - 1:L, :]
        g = b_last - b_col + li_col
        g_max = jnp.max(g, axis=0, keepdims=True)
        w = jnp.exp(g - g_max)
        kv = lax.dot_general((w * v).astype(BF16), kb, _TN, preferred_element_type=F32)
        ksum = jnp.sum(w * k, axis=0, keepdims=True)
        dmat = jnp.where(tri, b_col - b_r + li_r, NEG_INF)
        inter = b_col + m_state
        m_t = jnp.maximum(inter, jnp.max(dmat, axis=1, keepdims=True))
        s = lax.dot_general(qb, kb, _NT, preferred_element_type=F32) * jnp.exp(dmat - m_t)
        w_inter = jnp.exp(inter - m_t)
        num = (jnp.dot(s.astype(BF16), v.astype(BF16), preferred_element_type=F32)
               + w_inter * lax.dot_general(qb, c_state.astype(BF16), _NT, preferred_element_type=F32))
        den = jnp.sum(s, axis=1, keepdims=True) + w_inter * jnp.sum(q * n_state, axis=1, keepdims=True)
        ht = num / jnp.maximum(jnp.abs(den), jnp.exp(-m_t))
        ht = ht * lax.rsqrt(jnp.mean(ht * ht, axis=1, keepdims=True) + NORM_EPS) * gain
        out_ref[0, pl.ds(r0, L), :] = ht * _sigmoid(og_ref[0, pl.ds(r0, L), :])
        m_new = jnp.maximum(b_last + m_state, g_max)
        a = jnp.exp(b_last + m_state - m_new)
        cc = jnp.exp(g_max - m_new)
        return a * c_state + cc * kv, a * n_state + cc * ksum, m_new

    init = (jnp.zeros((HEAD_DIM, HEAD_DIM), F32), jnp.zeros((1, HEAD_DIM), F32), jnp.zeros((1, 1), F32))
    lax.fori_loop(0, n_chunks, chunk, init, unroll=8)


def _mlstm(z3, gt, conv_w, gate_b, head_gain):
    b, s_len, _ = z3.shape
    n_chunks = s_len // MLSTM_CHUNK
    gr = gt.shape[0]
    bcol = jnp.zeros((1, LANES), F32).at[0, :2 * MLSTM_HEADS].set(gate_b)
    brow = jnp.zeros((gr, 1), F32).at[:2 * MLSTM_HEADS, 0].set(gate_b)

    def col(off):
        return pl.BlockSpec((1, s_len, HEAD_DIM), lambda bi, hi: (bi, 0, off + hi))

    in_specs = [col(0), col(MLSTM_HEADS), col(2 * MLSTM_HEADS), col(3 * MLSTM_HEADS),
                pl.BlockSpec((1, s_len, LANES), lambda bi, hi: (bi, 0, EVEN_IF_BLK)),
                pl.BlockSpec((gr, s_len), lambda bi, hi: (0, bi)),
                pl.BlockSpec((MLSTM_CONV, HEAD_DIM), lambda bi, hi: (0, hi)),
                pl.BlockSpec((MLSTM_CONV, HEAD_DIM), lambda bi, hi: (0, MLSTM_HEADS + hi)),
                pl.BlockSpec((1, LANES), lambda bi, hi: (0, 0)),
                pl.BlockSpec((gr, 1), lambda bi, hi: (0, 0)),
                pl.BlockSpec((1, HEAD_DIM), lambda bi, hi: (0, hi))]
    scratch = [pltpu.VMEM((s_len, HEAD_DIM), F32), pltpu.VMEM((s_len, HEAD_DIM), F32),
               pltpu.VMEM((s_len, 1), F32), pltpu.VMEM((s_len, 1), F32),
               pltpu.VMEM((n_chunks, 1, MLSTM_CHUNK), F32), pltpu.VMEM((n_chunks, 1, MLSTM_CHUNK), F32)]
    return pl.pallas_call(
        _mlstm_body, grid=(b, MLSTM_HEADS), in_specs=in_specs,
        out_specs=pl.BlockSpec((1, s_len, HEAD_DIM), lambda bi, hi: (bi, 0, hi)),
        out_shape=jax.ShapeDtypeStruct((b, s_len, MLSTM_W), F32),
        scratch_shapes=scratch, compiler_params=_cparams(("parallel", "parallel")),
        name="mlstm")(z3, z3, z3, z3, z3, gt, conv_w, conv_w, bcol, brow, head_gain.reshape(1, MLSTM_W))


def _moba_body(q_ref, k_ref, v_ref, cos_ref, sin_ref, o_ref, kr_scr, vb_scr, km_scr):
    i = pl.program_id(1)
    s_len = k_ref.shape[1]
    bs = MOBA_BLOCK
    nb = s_len // bs
    heads = [slice(h * HEAD_DIM, (h + 1) * HEAD_DIM) for h in range(MOBA_HEADS)]

    @pl.when(i == 0)
    def _():
        vb_scr[...] = v_ref[0].astype(BF16)
        for h, cols in enumerate(heads):
            kr = _rotate(k_ref[0, :, cols], cos_ref[...], sin_ref[...])
            kr_scr[:, cols] = kr.astype(BF16)
            rows = [jnp.sum(kr[n * bs:(n + 1) * bs, :], axis=0, keepdims=True) / float(bs) for n in range(nb)]
            rows.append(jnp.zeros((km_scr.shape[1] - nb, HEAD_DIM), F32))
            km_scr[h] = jnp.concatenate(rows, axis=0)

    t0 = pl.multiple_of(i * bs, bs)
    cos_q = cos_ref[pl.ds(t0, bs), :]
    sin_q = sin_ref[pl.ds(t0, bs), :]
    n_gate = km_scr.shape[1]
    blk = lax.broadcasted_iota(jnp.int32, (n_gate, bs), 0)
    eye = jnp.where(lax.broadcasted_iota(jnp.int32, (n_gate, LANES), 0)
                    == lax.broadcasted_iota(jnp.int32, (n_gate, LANES), 1), 1.0, 0.0).astype(BF16)
    causal = lax.broadcasted_iota(jnp.int32, (bs, bs), 1) <= lax.broadcasted_iota(jnp.int32, (bs, bs), 0)

    qbs, sel_bs, init = [], [], []
    for h, cols in enumerate(heads):
        qr = _rotate(q_ref[0, :, cols], cos_q, sin_q)
        gate_t = lax.dot_general(km_scr[h], qr, _NT, precision=HIGHEST, preferred_element_type=F32)
        val = jnp.where(blk < i, gate_t, NEG_INF)
        sel_t = jnp.where(_rank_before(val, nb, 0) < MOBA_TOPK, jnp.where(val > 0.5 * NEG_INF, 1.0, 0.0), 0.0)
        sel_bs.append(lax.dot_general(sel_t.astype(BF16), eye, _TN, preferred_element_type=F32).astype(BF16))
        qb = (qr * (HEAD_DIM ** -0.5)).astype(BF16)
        qbs.append(qb)
        s = lax.dot_general(qb, kr_scr[pl.ds(t0, bs), cols], _NT, preferred_element_type=F32)
        s = jnp.where(causal, s, NEG_INF)
        m0 = jnp.max(s, axis=1, keepdims=True)
        e = jnp.exp(s - m0)
        init += [m0, jnp.sum(e, axis=1, keepdims=True),
                 jnp.dot(e.astype(BF16), vb_scr[pl.ds(t0, bs), cols], preferred_element_type=F32)]

    def past_block(j, carry):
        k0 = pl.multiple_of(j * bs, bs)
        expand = jnp.where(lax.broadcasted_iota(jnp.int32, (LANES, bs), 0) == j, 1.0, 0.0).astype(BF16)
        out = []
        for h, cols in enumerate(heads):
            m_prev, l_prev, acc = carry[3 * h:3 * h + 3]
            picked = jnp.dot(sel_bs[h], expand, preferred_element_type=F32)
            s = lax.dot_general(qbs[h], kr_scr[pl.ds(k0, bs), cols], _NT, preferred_element_type=F32)
            s = jnp.where(picked > 0.5, s, NEG_INF)
            m_new = jnp.maximum(m_prev, jnp.max(s, axis=1, keepdims=True))
            alpha = jnp.exp(m_prev - m_new)
            e = jnp.exp(s - m_new)
            out += [m_new, alpha * l_prev + jnp.sum(e, axis=1, keepdims=True),
                    alpha * acc + jnp.dot(e.astype(BF16), vb_scr[pl.ds(k0, bs), cols], preferred_element_type=F32)]
        return tuple(out)

    fin = lax.fori_loop(0, i, past_block, tuple(init))
    for h, cols in enumerate(heads):
        o_ref[0, :, cols] = fin[3 * h + 2] / jnp.maximum(fin[3 * h + 1], 1e-30)


def _moba(z3, cos2, sin2):
    b, s_len, _ = z3.shape
    nb = s_len // MOBA_BLOCK
    q_off = 4 * MLSTM_W // MOBA_W

    def kv(off):
        return pl.BlockSpec((1, s_len, MOBA_W), lambda bi, i: (bi, 0, off))

    in_specs = [pl.BlockSpec((1, MOBA_BLOCK, MOBA_W), lambda bi, i: (bi, i, q_off)),
                kv(q_off + 1), kv(q_off + 2),
                pl.BlockSpec((s_len, HEAD_DIM), lambda bi, i: (0, 0)),
                pl.BlockSpec((s_len, HEAD_DIM), lambda bi, i: (0, 0))]
    scratch = [pltpu.VMEM((s_len, MOBA_W), BF16), pltpu.VMEM((s_len, MOBA_W), BF16),
               pltpu.VMEM((MOBA_HEADS, BF16_SUBLANES, HEAD_DIM), F32)]
    return pl.pallas_call(
        _moba_body, grid=(b, nb), in_specs=in_specs,
        out_specs=pl.BlockSpec((1, MOBA_BLOCK, MOBA_W), lambda bi, i: (bi, i, 0)),
        out_shape=jax.ShapeDtypeStruct((b, s_len, MOBA_W), F32),
        scratch_shapes=scratch, compiler_params=_cparams(("parallel", "arbitrary")),
        name="moba")(z3, z3, z3, cos2, sin2)


def _gelu_tanh(x):
    return x * (0.5 * (1.0 + jnp.tanh(np.sqrt(2.0 / np.pi) * (x + 0.044715 * (x * x * x)))))


def _nsa_compress_body(kc_ref, vc_ref, pe_ref, wk1_ref, wk2_ref, wv1_ref, wv2_ref, ko_ref, vo_ref):
    n_rows = kc_ref.shape[1] // CMP_STRIDE
    halves = CMP_LEN // CMP_STRIDE
    assert halves == 2

    def compress(x_ref, pe, w1_ref, w2_ref):
        ya = jnp.zeros((n_rows, CMP_HIDDEN), F32)
        yb = jnp.zeros((n_rows, CMP_HIDDEN), F32)
        for l in range(CMP_STRIDE):
            r = x_ref[0, pl.ds(l, n_rows, stride=CMP_STRIDE), :]
            la, lb = l, CMP_STRIDE + l
            ya = ya + jnp.dot((r + pe[la:la + 1, :]).astype(BF16), w1_ref[la * HEAD_DIM:(la + 1) * HEAD_DIM, :],
                              preferred_element_type=F32)
            yb = yb + jnp.dot((r + pe[lb:lb + 1, :]).astype(BF16), w1_ref[lb * HEAD_DIM:(lb + 1) * HEAD_DIM, :],
                              preferred_element_type=F32)
        pre = ya + pltpu.roll(yb, n_rows - 1, axis=0)
        return jnp.dot(_gelu_tanh(pre).astype(BF16), w2_ref[...], preferred_element_type=F32)

    ko_ref[0, 0] = compress(kc_ref, pe_ref[0], wk1_ref, wk2_ref)
    vo_ref[0, 0] = compress(vc_ref, pe_ref[1], wv1_ref, wv2_ref)


def _nsa_compress(z3, cmp_pos, wk1, wk2, wv1, wv2):
    b, s_len, _ = z3.shape
    n_rows = s_len // CMP_STRIDE
    kc_off = NSA_W // LANES
    vc_off = kc_off + NSA_GROUPS
    const2 = lambda bi, gi: (0, 0)
    in_specs = [pl.BlockSpec((1, s_len, HEAD_DIM), lambda bi, gi: (bi, 0, kc_off + gi)),
                pl.BlockSpec((1, s_len, HEAD_DIM), lambda bi, gi: (bi, 0, vc_off + gi)),
                pl.BlockSpec(cmp_pos.shape, lambda bi, gi: (0, 0, 0)),
                pl.BlockSpec(wk1.shape, const2), pl.BlockSpec(wk2.shape, const2),
                pl.BlockSpec(wv1.shape, const2), pl.BlockSpec(wv2.shape, const2)]
    out_spec = pl.BlockSpec((1, 1, n_rows, HEAD_DIM), lambda bi, gi: (bi, gi, 0, 0))
    out_sds = jax.ShapeDtypeStruct((b, NSA_GROUPS, n_rows, HEAD_DIM), F32)
    return pl.pallas_call(
        _nsa_compress_body, grid=(b, NSA_GROUPS), in_specs=in_specs,
        out_specs=[out_spec, out_spec], out_shape=[out_sds, out_sds],
        compiler_params=_cparams(("parallel", "parallel")), name="nsa_compress")(
            z3, z3, cmp_pos, wk1, wk2, wv1, wv2)


def _nsa_body(q_ref, ks_ref, vs_ref, kw_ref, vw_ref, kc_ref, vc_ref, gate_ref, cos_ref, sin_ref, ov_ref,
              o_ref, krs, vsb, krw, vwb):
    i = pl.program_id(2)
    tq = NSA_TQ
    tk = NSA_TQ
    scale = HEAD_DIM ** -0.5
    n_sel_blk = ks_ref.shape[1] // SEL_BLOCK
    n_cmp = (ks_ref.shape[1] - CMP_LEN) // CMP_STRIDE + 1
    shift = SEL_BLOCK.bit_length() - 1
    assert 1 << shift == SEL_BLOCK and n_sel_blk <= LANES

    @pl.when(i == 0)
    def _():
        krs[...] = _rotate(ks_ref[0], cos_ref[...], sin_ref[...]).astype(BF16)
        krw[...] = _rotate(kw_ref[0], cos_ref[...], sin_ref[...]).astype(BF16)
        vsb[...] = vs_ref[0].astype(BF16)
        vwb[...] = vw_ref[0].astype(BF16)

    t0 = pl.multiple_of(i * tq, tq)
    pos = t0 + lax.broadcasted_iota(jnp.int32, (tq, 1), 0)
    pos_row = t0 + lax.broadcasted_iota(jnp.int32, (1, tq), 1)
    cos_q = cos_ref[pl.ds(t0, tq), :]
    sin_q = sin_ref[pl.ds(t0, tq), :]

    kc = kc_ref[0, 0].astype(BF16)
    vc = vc_ref[0, 0].astype(BF16)
    n_col = lax.broadcasted_iota(jnp.int32, (LANES, 1), 0)
    cmp_end = jnp.where(n_col < n_cmp, n_col * CMP_STRIDE + (CMP_LEN - 1), jnp.iinfo(jnp.int32).max)
    cmp_ok = cmp_end <= pos_row
    qr_heads, o_cmp = [], []
    p_sum = jnp.zeros((LANES, tq), F32)
    for r in range(NSA_REP):
        q = q_ref[0, :, r * HEAD_DIM:(r + 1) * HEAD_DIM] * scale
        qr_heads.append(_rotate(q, cos_q, sin_q).astype(BF16))
        s = lax.dot_general(kc, q.astype(BF16), _NT, preferred_element_type=F32)
        s = jnp.where(cmp_ok, s, NEG_INF)
        e = jnp.where(cmp_ok, jnp.exp(s - jnp.max(s, axis=0, keepdims=True)), 0.0)
        p = e / jnp.maximum(jnp.sum(e, axis=0, keepdims=True), 1e-30)
        o_cmp.append(lax.dot_general(p.astype(BF16), vc, _TN, preferred_element_type=F32))
        p_sum = p_sum + p
    imp = jnp.dot(ov_ref[...], p_sum, precision=HIGHEST, preferred_element_type=F32)

    behind = (pos_row >> shift) - lax.broadcasted_iota(jnp.int32, imp.shape, 0)
    val = jnp.where(behind == 0, FORCED_SCORE, jnp.where(behind == 1, FORCED_SCORE, imp))
    val = jnp.where(lax.broadcasted_iota(jnp.int32, imp.shape, 0) == 0, FORCED_SCORE, val)
    val = jnp.where(behind >= 0, val, NEG_INF)
    sel_t = jnp.where(_rank_before(val, n_sel_blk, 0) < SEL_TOPN, jnp.where(val > 0.5 * NEG_INF, 1.0, 0.0), 0.0)
    eye = jnp.where(lax.broadcasted_iota(jnp.int32, (n_sel_blk, LANES), 0)
                    == lax.broadcasted_iota(jnp.int32, (n_sel_blk, LANES), 1), 1.0, 0.0).astype(BF16)
    sel_b = lax.dot_general(sel_t.astype(BF16), eye, _TN, preferred_element_type=F32).astype(BF16)

    def sel_tile(k0, carry, diagonal):
        k_t = krs[pl.ds(k0, tk), :]
        v_t = vsb[pl.ds(k0, tk), :]
        blk_of_key = (k0 + lax.broadcasted_iota(jnp.int32, (LANES, tk), 1)) >> shift
        expand = jnp.where(blk_of_key == lax.broadcasted_iota(jnp.int32, (LANES, tk), 0), 1.0, 0.0).astype(BF16)
        picked = jnp.dot(sel_b, expand, preferred_element_type=F32)
        bias = jnp.where(picked > 0.5, 0.0, NEG_INF)
        if diagonal:
            bias = jnp.where(lax.broadcasted_iota(jnp.int32, (tq, tk), 1)
                             <= lax.broadcasted_iota(jnp.int32, (tq, tk), 0), bias, NEG_INF)
        out = []
        for r in range(NSA_REP):
            s = lax.dot_general(qr_heads[r], k_t, _NT, preferred_element_type=F32) + bias
            if carry is None:
                m_new = jnp.max(s, axis=1, keepdims=True)
                e = jnp.exp(s - m_new)
                out += [m_new, jnp.sum(e, axis=1, keepdims=True),
                        jnp.dot(e.astype(BF16), v_t, preferred_element_type=F32)]
            else:
                m_prev, l_prev, acc = carry[3 * r:3 * r + 3]
                m_new = jnp.maximum(m_prev, jnp.max(s, axis=1, keepdims=True))
                alpha = jnp.exp(m_prev - m_new)
                e = jnp.exp(s - m_new)
                out += [m_new, alpha * l_prev + jnp.sum(e, axis=1, keepdims=True),
                        alpha * acc + jnp.dot(e.astype(BF16), v_t, preferred_element_type=F32)]
        return tuple(out)

    init = sel_tile(t0, None, True)
    fin = lax.fori_loop(0, i, lambda kt, c: sel_tile(pl.multiple_of(kt * tk, tk), c, False), init)

    span = WINDOW + tq
    start = pl.multiple_of(jnp.maximum(t0 - WINDOW, 0), tq)
    k_w = krw[pl.ds(start, span), :]
    v_w = vwb[pl.ds(start, span), :]
    gap = pos - (start + lax.broadcasted_iota(jnp.int32, (tq, span), 1))
    win_bias = jnp.where(gap >= 0, jnp.where(gap < WINDOW, 0.0, NEG_INF), NEG_INF)

    gates = _sigmoid(gate_ref[0])
    for r in range(NSA_REP):
        s = lax.dot_general(qr_heads[r], k_w, _NT, preferred_element_type=F32) + win_bias
        e = jnp.exp(s - jnp.max(s, axis=1, keepdims=True))
        o_win = (jnp.dot(e.astype(BF16), v_w, preferred_element_type=F32)
                 / jnp.maximum(jnp.sum(e, axis=1, keepdims=True), 1e-30))
        o_sel = fin[3 * r + 2] / jnp.maximum(fin[3 * r + 1], 1e-30)
        g0, g1, g2 = (gates[:, 3 * r + c:3 * r + c + 1] for c in range(3))
        o_ref[0, :, r * HEAD_DIM:(r + 1) * HEAD_DIM] = g0 * o_cmp[r] + g1 * o_sel + g2 * o_win


def _nsa_attention(z3, k_cmp, v_cmp, cos2, sin2):
    b, s_len, _ = z3.shape
    assert s_len >= WINDOW + NSA_TQ and s_len % NSA_TQ == 0
    n_cmp = (s_len - CMP_LEN) // CMP_STRIDE + 1
    n_blk = s_len // SEL_BLOCK
    n_rows = k_cmp.shape[2]
    assert n_rows == LANES
    cmp_start = np.arange(n_rows) * CMP_STRIDE
    cmp_end = cmp_start + CMP_LEN - 1
    blk_lo = np.arange(LANES) * SEL_BLOCK
    overlap = ((cmp_start[:, None] <= blk_lo[None, :] + SEL_BLOCK - 1) & (cmp_end[:, None] >= blk_lo[None, :])
               & (np.arange(n_rows)[:, None] < n_cmp)).astype(np.float32)
    overlap_t = np.ascontiguousarray(overlap.T[:n_blk])
    base = NSA_W // LANES

    def kv(off):
        return pl.BlockSpec((1, s_len, HEAD_DIM), lambda bi, gi, i: (bi, 0, base + off * NSA_GROUPS + gi))

    cmp_spec = pl.BlockSpec((1, 1, n_rows, HEAD_DIM), lambda bi, gi, i: (bi, gi, 0, 0))
    const2 = lambda bi, gi, i: (0, 0)
    in_specs = [pl.BlockSpec((1, NSA_TQ, NSA_REP * HEAD_DIM), lambda bi, gi, i: (bi, i, gi)),
                kv(2), kv(3), kv(4), kv(5), cmp_spec, cmp_spec,
                pl.BlockSpec((1, NSA_TQ, LANES), lambda bi, gi, i: (bi, i, ODD_GATE_BLK + gi)),
                pl.BlockSpec((s_len, HEAD_DIM), const2), pl.BlockSpec((s_len, HEAD_DIM), const2),
                pl.BlockSpec((n_blk, n_rows), const2)]
    scratch = [pltpu.VMEM((s_len, HEAD_DIM), BF16) for _ in range(4)]
    return pl.pallas_call(
        _nsa_body, grid=(b, NSA_GROUPS, s_len // NSA_TQ), in_specs=in_specs,
        out_specs=pl.BlockSpec((1, NSA_TQ, NSA_REP * HEAD_DIM), lambda bi, gi, i: (bi, i, gi)),
        out_shape=jax.ShapeDtypeStruct((b, s_len, NSA_W), F32),
        scratch_shapes=scratch, compiler_params=_cparams(("parallel", "parallel", "arbitrary")),
        name="nsa_attention")(z3, z3, z3, z3, z3, k_cmp, v_cmp, z3, cos2, sin2, jnp.asarray(overlap_t))


def _moe_body(x_ref, g_ref, wr_ref, br_ref, w1_ref, w3_ref, w2_ref, fg_ref, o_ref, xn_scr, cw_scr, acc_scr,
              *, final_norm):
    j = pl.program_id(1)
    tm = x_ref.shape[0]
    lane = lax.broadcasted_iota(jnp.int32, (tm, LANES), 1).astype(F32)
    neg = float("-inf")

    @pl.when(j == 0)
    def _():
        xn = _rms(x_ref[...], g_ref[...])
        xn_scr[...] = xn.astype(BF16)
        logits = jnp.dot(xn, wr_ref[...], precision=HIGHEST, preferred_element_type=F32) + br_ref[...]
        is_g = lane < MOE_GROUPS
        gl = jnp.where(is_g, logits, neg)
        g_max = jnp.max(gl, axis=1, keepdims=True)
        g_w = 1.0 / jnp.sum(jnp.where(is_g, jnp.exp(gl - g_max), 0.0), axis=1, keepdims=True)
        g_top = jnp.min(jnp.where(gl == g_max, lane, float(LANES)), axis=1, keepdims=True)
        lo = MOE_GROUPS + MOE_EPG * g_top
        el = jnp.where((lane >= lo) & (lane < lo + MOE_EPG), logits, neg)
        v1 = jnp.max(el, axis=1, keepdims=True)
        i1 = jnp.min(jnp.where(el == v1, lane, float(LANES)), axis=1, keepdims=True)
        el2 = jnp.where(lane == i1, neg, el)
        v2 = jnp.max(el2, axis=1, keepdims=True)
        i2 = jnp.min(jnp.where(el2 == v2, lane, float(LANES)), axis=1, keepdims=True)
        e2 = jnp.exp(v2 - v1)
        den = 1.0 + e2
        cw_scr[...] = (jnp.where(lane == i1, g_w / den, 0.0) + jnp.where(lane == i2, g_w * e2 / den, 0.0))
        acc_scr[...] = jnp.zeros_like(acc_scr)

    xn = xn_scr[...]
    cw = cw_scr[...]
    acc = jnp.zeros(acc_scr.shape, F32)
    for r in range(MOE_EPG):
        h1 = jnp.dot(xn, w1_ref[r], preferred_element_type=F32)
        h3 = jnp.dot(xn, w3_ref[r], preferred_element_type=F32)
        e_lane = (MOE_GROUPS + MOE_EPG * j + r).astype(F32)
        col = jnp.sum(jnp.where(lane == e_lane, cw, 0.0), axis=1, keepdims=True)
        hd = (h1 * _sigmoid(h1)) * h3 * col
        acc = acc + jnp.dot(hd.astype(BF16), w2_ref[r], preferred_element_type=F32)
    acc_scr[...] += acc

    @pl.when(j == pl.num_programs(1) - 1)
    def _():
        out = x_ref[...] + acc_scr[...]
        if final_norm:
            out = _rms(out, fg_ref[...])
        o_ref[...] = out


def _moe(x2d, gain, w_g, b_g, w_e, b_e, w1, w3, w2, final_gain, final_norm, tm=1024):
    t, d = x2d.shape
    wr = jnp.zeros((d, LANES), F32).at[:, :MOE_GROUPS].set(w_g).at[:, MOE_GROUPS:MOE_GROUPS + MOE_EXPERTS].set(w_e)
    br = jnp.zeros((1, LANES), F32).at[0, :MOE_GROUPS].set(b_g).at[0, MOE_GROUPS:MOE_GROUPS + MOE_EXPERTS].set(b_e)
    in_specs = [pl.BlockSpec((tm, d), lambda i, j: (i, 0)),
                pl.BlockSpec((1, d), lambda i, j: (0, 0)),
                pl.BlockSpec((d, LANES), lambda i, j: (0, 0)),
                pl.BlockSpec((1, LANES), lambda i, j: (0, 0)),
                pl.BlockSpec((MOE_EPG, d, MOE_HIDDEN), lambda i, j: (j, 0, 0)),
                pl.BlockSpec((MOE_EPG, d, MOE_HIDDEN), lambda i, j: (j, 0, 0)),
                pl.BlockSpec((MOE_EPG, MOE_HIDDEN, d), lambda i, j: (j, 0, 0)),
                pl.BlockSpec((1, d), lambda i, j: (0, 0))]
    scratch = [pltpu.VMEM((tm, d), BF16), pltpu.VMEM((tm, LANES), F32), pltpu.VMEM((tm, d), F32)]
    return pl.pallas_call(
        functools.partial(_moe_body, final_norm=final_norm),
        grid=(t // tm, MOE_GROUPS), in_specs=in_specs,
        out_specs=pl.BlockSpec((tm, d), lambda i, j: (i, 0)),
        out_shape=jax.ShapeDtypeStruct((t, d), F32),
        scratch_shapes=scratch, compiler_params=_cparams(("parallel", "arbitrary")),
        name="hier_moe")(x2d, gain.reshape(1, d), wr, br, w1.astype(BF16), w3.astype(BF16), w2.astype(BF16),
                         final_gain.reshape(1, d))


def _rope_tables(s_len):
    half = HEAD_DIM // 2
    inv = ROPE_THETA ** (-(jnp.arange(half, dtype=F32) / half))
    ang = jnp.arange(s_len, dtype=F32)[:, None] * inv[None, :]
    cos, sin = jnp.cos(ang), jnp.sin(ang)
    return jnp.concatenate([cos, cos], axis=-1), jnp.concatenate([-sin, sin], axis=-1)


def _even_weights(w_in):
    n_if = 2 * MLSTM_HEADS
    a = 4 * MLSTM_W
    w_main = jnp.concatenate([w_in[:, :a], w_in[:, a + n_if:],
                              w_in[:, a:a + n_if], jnp.zeros((w_in.shape[0], LANES - n_if), w_in.dtype)], axis=1)
    w_if_t = jnp.concatenate([w_in[:, a:a + n_if].T, jnp.zeros((16 - n_if, w_in.shape[0]), w_in.dtype)], axis=0)
    return w_main.astype(BF16), w_if_t.astype(BF16)


def _odd_weights(w_in):
    a = NSA_W + 6 * NSA_KV_W
    per_group = NSA_REP * 3
    pad = jnp.zeros((w_in.shape[0], LANES - per_group), w_in.dtype)
    cols = [w_in[:, :a]]
    for g in range(NSA_GROUPS):
        cols += [w_in[:, a + g * per_group:a + (g + 1) * per_group], pad]
    return jnp.concatenate(cols, axis=1).astype(BF16)


def kernel(x, mix_norm_0, w_in_0, mlstm_conv_0, mlstm_gate_b_0, mlstm_head_norm_0, w_out_0, ffn_norm_0, router_group_0, router_group_b_0, router_expert_0, router_expert_b_0, moe_w1_0, moe_w3_0, moe_w2_0, mix_norm_1, w_in_1, nsa_cmp_pos_1, nsa_cmp_k1_1, nsa_cmp_k2_1, nsa_cmp_v1_1, nsa_cmp_v2_1, w_out_1, ffn_norm_1, router_group_1, router_group_b_1, router_expert_1, router_expert_b_1, moe_w1_1, moe_w3_1, moe_w2_1, final_norm):
    b, s_len, d = x.shape
    t = b * s_len
    cos2, sin2 = _rope_tables(s_len)
    x2d = x.reshape(t, d)

    w_main, w_if_t = _even_weights(w_in_0)
    z0, gt = _norm_matmul(x2d, mix_norm_0, w_main, w_if_t)
    z0 = z0.reshape(b, s_len, EVEN_N)
    h_m = _mlstm(z0, gt, mlstm_conv_0, mlstm_gate_b_0, mlstm_head_norm_0)
    o_b = _moba(z0, cos2, sin2)
    w_out = w_out_0.astype(BF16)
    h = _proj_residual(x2d, [h_m.reshape(t, MLSTM_W), o_b.reshape(t, MOBA_W)], [w_out[:MLSTM_W], w_out[MLSTM_W:]])
    h = _moe(h, ffn_norm_0, router_group_0, router_group_b_0, router_expert_0, router_expert_b_0,
             moe_w1_0, moe_w3_0, moe_w2_0, final_norm, False)

    z1 = _norm_matmul(h, mix_norm_1, _odd_weights(w_in_1)).reshape(b, s_len, ODD_N)
    k_cmp, v_cmp = _nsa_compress(z1, nsa_cmp_pos_1, nsa_cmp_k1_1.astype(BF16), nsa_cmp_k2_1.astype(BF16),
                                 nsa_cmp_v1_1.astype(BF16), nsa_cmp_v2_1.astype(BF16))
    o = _nsa_attention(z1, k_cmp, v_cmp, cos2, sin2)
    h = _proj_residual(h, [o.reshape(t, NSA_W)], [w_out_1.astype(BF16)])
    h = _moe(h, ffn_norm_1, router_group_1, router_group_b_1, router_expert_1, router_expert_b_1,
             moe_w1_1, moe_w3_1, moe_w2_1, final_norm, True)
    return h.reshape(b, s_len, d)
```

```python
import functools

import numpy as np
import jax
import jax.numpy as jnp
from jax import lax
from jax.experimental import pallas as pl
from jax.experimental.pallas import tpu as pltpu

F32 = jnp.float32
BF16 = jnp.bfloat16
HIGHEST = lax.Precision.HIGHEST

LANES = 128
BF16_SUBLANES = 16
D_MODEL = 1024
HEAD_DIM = 128
ROPE_THETA = 10000.0
NORM_EPS = 1e-6
NEG_INF = -1e30
FORCED_SCORE = 1e4

MLSTM_HEADS = 4
MLSTM_W = MLSTM_HEADS * HEAD_DIM
MLSTM_CHUNK = 64
MLSTM_CONV = 4
MLSTM_GATE_CAP = 15.0
MOBA_HEADS = 4
MOBA_W = MOBA_HEADS * HEAD_DIM
MOBA_BLOCK = 256
MOBA_TOPK = 3

NSA_HEADS = 8
NSA_GROUPS = 2
NSA_REP = NSA_HEADS // NSA_GROUPS
NSA_W = NSA_HEADS * HEAD_DIM
NSA_KV_W = NSA_GROUPS * HEAD_DIM
CMP_LEN = 32
CMP_STRIDE = 16
CMP_HIDDEN = 256
SEL_BLOCK = 64
SEL_TOPN = 8
WINDOW = 512
NSA_TQ = 256

MOE_GROUPS = 4
MOE_EPG = 4
MOE_EXPERTS = MOE_GROUPS * MOE_EPG
MOE_HIDDEN = D_MODEL // 4

EVEN_N = 4 * MLSTM_W + 3 * MOBA_W + LANES
EVEN_IF_BLK = (4 * MLSTM_W + 3 * MOBA_W) // LANES
ODD_GATE_BLK = (NSA_W + 6 * NSA_KV_W) // LANES
ODD_N = NSA_W + 6 * NSA_KV_W + NSA_GROUPS * LANES

VMEM_LIMIT = 56 * 1024 * 1024

_NT = (((1,), (1,)), ((), ()))
_TN = (((0,), (0,)), ((), ()))


def _cparams(sem):
    return pltpu.CompilerParams(dimension_semantics=sem, vmem_limit_bytes=VMEM_LIMIT)


def _rms(x, g):
    return x * lax.rsqrt(jnp.mean(x * x, axis=-1, keepdims=True) + NORM_EPS) * g


def _sigmoid(x):
    return 1.0 / (1.0 + jnp.exp(-x))


def _log_sigmoid(x):
    return -(jnp.maximum(-x, 0.0) + jnp.log1p(jnp.exp(-jnp.abs(x))))


def _rotate(x, cos2, sin2):
    return x * cos2 + pltpu.roll(x, HEAD_DIM // 2, axis=1) * sin2


def _rank_before(v, n_valid, axis):
    idx = lax.broadcasted_iota(jnp.int32, v.shape, axis)
    rank = jnp.zeros(v.shape, F32)
    for m in range(n_valid):
        vm = v[m:m + 1, :] if axis == 0 else v[:, m:m + 1]
        tie = jnp.where(idx > m, 1.0, 0.0)
        rank = rank + jnp.where(vm > v, 1.0, jnp.where(vm == v, tie, 0.0))
    return rank


def _norm_matmul_body(*refs, n_chunk, with_t):
    if with_t:
        x_ref, g_ref, w_ref, wt_ref, o_ref, ot_ref = refs
    else:
        x_ref, g_ref, w_ref, o_ref = refs
    yb = _rms(x_ref[...], g_ref[...]).astype(BF16)
    n = w_ref.shape[1]
    for c0 in range(0, n, n_chunk):
        c1 = min(n, c0 + n_chunk)
        o_ref[:, c0:c1] = jnp.dot(yb, w_ref[:, c0:c1], preferred_element_type=F32)
    if with_t:
        ot_ref[...] = lax.dot_general(wt_ref[...], yb, _NT, preferred_element_type=F32)


def _norm_matmul(x2d, gain, w, wt=None, tm=512):
    t, d = x2d.shape
    n = w.shape[1]
    with_t = wt is not None
    in_specs = [pl.BlockSpec((tm, d), lambda i: (i, 0)),
                pl.BlockSpec((1, d), lambda i: (0, 0)),
                pl.BlockSpec((d, n), lambda i: (0, 0))]
    out_specs = [pl.BlockSpec((tm, n), lambda i: (i, 0))]
    out_shape = [jax.ShapeDtypeStruct((t, n), F32)]
    args = [x2d, gain.reshape(1, d), w]
    if with_t:
        r = wt.shape[0]
        in_specs.append(pl.BlockSpec((r, d), lambda i: (0, 0)))
        out_specs.append(pl.BlockSpec((r, tm), lambda i: (0, i)))
        out_shape.append(jax.ShapeDtypeStruct((r, t), F32))
        args.append(wt)
    outs = pl.pallas_call(
        functools.partial(_norm_matmul_body, n_chunk=512, with_t=with_t),
        grid=(t // tm,), in_specs=in_specs, out_specs=out_specs, out_shape=out_shape,
        compiler_params=_cparams(("parallel",)), name="norm_in_proj")(*args)
    return outs if with_t else outs[0]


def _chunk_cumsum(x, axis):
    idx = lax.broadcasted_iota(jnp.int32, x.shape, axis) % MLSTM_CHUNK
    d = 1
    while d < MLSTM_CHUNK:
        x = x + jnp.where(idx >= d, pltpu.roll(x, d, axis=axis), 0.0)
        d *= 2
    return x


def _mlstm_body(q_ref, k_ref, v_ref, og_ref, if_ref, gt_ref, cwq_ref, cwk_ref, bcol_ref, brow_ref,
                gain_ref, out_ref, qs, ks, bcs, lics, brs, lirs):
    h = pl.program_id(1)
    s_len = q_ref.shape[1]
    n_chunks = s_len // MLSTM_CHUNK
    L = MLSTM_CHUNK
    row = lax.broadcasted_iota(jnp.int32, (s_len, HEAD_DIM), 0)

    def conv_silu(x, w):
        acc = x * w[MLSTM_CONV - 1:MLSTM_CONV, :]
        for d in range(1, MLSTM_CONV):
            shifted = jnp.where(row >= d, pltpu.roll(x, d, axis=0), 0.0)
            acc = acc + shifted * w[MLSTM_CONV - 1 - d:MLSTM_CONV - d, :]
        return acc * _sigmoid(acc)

    qs[...] = conv_silu(q_ref[0], cwq_ref[...]) * (HEAD_DIM ** -0.5)
    ks[...] = conv_silu(k_ref[0], cwk_ref[...])

    pre = if_ref[0] + bcol_ref[...]
    pre = MLSTM_GATE_CAP * jnp.tanh(pre / MLSTM_GATE_CAP)
    lane = lax.broadcasted_iota(jnp.int32, pre.shape, 1)
    b_all = _chunk_cumsum(_log_sigmoid(pre), 0)
    lics[...] = jnp.sum(jnp.where(lane == h, pre, 0.0), axis=1, keepdims=True)
    bcs[...] = jnp.sum(jnp.where(lane == h + MLSTM_HEADS, b_all, 0.0), axis=1, keepdims=True)

    pr = gt_ref[...] + brow_ref[...]
    pr = MLSTM_GATE_CAP * jnp.tanh(pr / MLSTM_GATE_CAP)
    sub = lax.broadcasted_iota(jnp.int32, pr.shape, 0)
    b_rows = _chunk_cumsum(_log_sigmoid(pr), 1)
    li_row = jnp.sum(jnp.where(sub == h, pr, 0.0), axis=0, keepdims=True)
    b_row = jnp.sum(jnp.where(sub == h + MLSTM_HEADS, b_rows, 0.0), axis=0, keepdims=True)
    for c in range(n_chunks):
        brs[c] = b_row[:, c * L:(c + 1) * L]
        lirs[c] = li_row[:, c * L:(c + 1) * L]

    tri = (lax.broadcasted_iota(jnp.int32, (L, L), 1) <= lax.broadcasted_iota(jnp.int32, (L, L), 0))
    gain = gain_ref[...]

    def chunk(c, carry):
        c_state, n_state, m_state = carry
        r0 = pl.multiple_of(c * L, L)
        q = qs[pl.ds(r0, L), :]
        k = ks[pl.ds(r0, L), :]
        v = v_ref[0, pl.ds(r0, L), :]
        b_col = bcs[pl.ds(r0, L), :]
        li_col = lics[pl.ds(r0, L), :]
        b_r = brs[c]
        li_r = lirs[c]
        qb = q.astype(BF16)
        kb = k.astype(BF16)
        b_last = b_col[L - 1:L, :]
        g = b_last - b_col + li_col
        g_max = jnp.max(g, axis=0, keepdims=True)
        w = jnp.exp(g - g_max)
        kv = lax.dot_general((w * v).astype(BF16), kb, _TN, preferred_element_type=F32)
        ksum = jnp.sum(w * k, axis=0, keepdims=True)
        dmat = jnp.where(tri, b_col - b_r + li_r, NEG_INF)
        inter = b_col + m_state
        m_t = jnp.maximum(inter, jnp.max(dmat, axis=1, keepdims=True))
        s = lax.dot_general(qb, kb, _NT, preferred_element_type=F32) * jnp.exp(dmat - m_t)
        w_inter = jnp.exp(inter - m_t)
        num = (jnp.dot(s.astype(BF16), v.astype(BF16), preferred_element_type=F32)
               + w_inter * lax.dot_general(qb, c_state.astype(BF16), _NT, preferred_element_type=F32))
        den = jnp.sum(s, axis=1, keepdims=True) + w_inter * jnp.sum(q * n_state, axis=1, keepdims=True)
        ht = num / jnp.maximum(jnp.abs(den), jnp.exp(-m_t))
        ht = ht * lax.rsqrt(jnp.mean(ht * ht, axis=1, keepdims=True) + NORM_EPS) * gain
        out_ref[0, pl.ds(r0, L), :] = ht * _sigmoid(og_ref[0, pl.ds(r0, L), :])
        m_new = jnp.maximum(b_last + m_state, g_max)
        a = jnp.exp(b_last + m_state - m_new)
        cc = jnp.exp(g_max - m_new)
        return a * c_state + cc * kv, a * n_state + cc * ksum, m_new

    init = (jnp.zeros((HEAD_DIM, HEAD_DIM), F32), jnp.zeros((1, HEAD_DIM), F32), jnp.zeros((1, 1), F32))
    lax.fori_loop(0, n_chunks, chunk, init, unroll=8)


def _mlstm(z3, gt, conv_w, gate_b, head_gain):
    b, s_len, _ = z3.shape
    n_chunks = s_len // MLSTM_CHUNK
    gr = gt.shape[0]
    bcol = jnp.zeros((1, LANES), F32).at[0, :2 * MLSTM_HEADS].set(gate_b)
    brow = jnp.zeros((gr, 1), F32).at[:2 * MLSTM_HEADS, 0].set(gate_b)

    def col(off):
        return pl.BlockSpec((1, s_len, HEAD_DIM), lambda bi, hi: (bi, 0, off + hi))

    in_specs = [col(0), col(MLSTM_HEADS), col(2 * MLSTM_HEADS), col(3 * MLSTM_HEADS),
                pl.BlockSpec((1, s_len, LANES), lambda bi, hi: (bi, 0, EVEN_IF_BLK)),
                pl.BlockSpec((gr, s_len), lambda bi, hi: (0, bi)),
                pl.BlockSpec((MLSTM_CONV, HEAD_DIM), lambda bi, hi: (0, hi)),
                pl.BlockSpec((MLSTM_CONV, HEAD_DIM), lambda bi, hi: (0, MLSTM_HEADS + hi)),
                pl.BlockSpec((1, LANES), lambda bi, hi: (0, 0)),
                pl.BlockSpec((gr, 1), lambda bi, hi: (0, 0)),
                pl.BlockSpec((1, HEAD_DIM), lambda bi, hi: (0, hi))]
    scratch = [pltpu.VMEM((s_len, HEAD_DIM), F32), pltpu.VMEM((s_len, HEAD_DIM), F32),
               pltpu.VMEM((s_len, 1), F32), pltpu.VMEM((s_len, 1), F32),
               pltpu.VMEM((n_chunks, 1, MLSTM_CHUNK), F32), pltpu.VMEM((n_chunks, 1, MLSTM_CHUNK), F32)]
    return pl.pallas_call(
        _mlstm_body, grid=(b, MLSTM_HEADS), in_specs=in_specs,
        out_specs=pl.BlockSpec((1, s_len, HEAD_DIM), lambda bi, hi: (bi, 0, hi)),
        out_shape=jax.ShapeDtypeStruct((b, s_len, MLSTM_W), F32),
        scratch_shapes=scratch, compiler_params=_cparams(("parallel", "parallel")),
        name="mlstm")(z3, z3, z3, z3, z3, gt, conv_w, conv_w, bcol, brow, head_gain.reshape(1, MLSTM_W))


def _moba_body(q_ref, k_ref, v_ref, cos_ref, sin_ref, o_ref, kr_scr, vb_scr, km_scr):
    i = pl.program_id(1)
    s_len = k_ref.shape[1]
    bs = MOBA_BLOCK
    nb = s_len // bs
    heads = [slice(h * HEAD_DIM, (h + 1) * HEAD_DIM) for h in range(MOBA_HEADS)]

    @pl.when(i == 0)
    def _():
        vb_scr[...] = v_ref[0].astype(BF16)
        for h, cols in enumerate(heads):
            kr = _rotate(k_ref[0, :, cols], cos_ref[...], sin_ref[...])
            kr_scr[:, cols] = kr.astype(BF16)
            rows = [jnp.sum(kr[n * bs:(n + 1) * bs, :], axis=0, keepdims=True) / float(bs) for n in range(nb)]
            rows.append(jnp.zeros((km_scr.shape[1] - nb, HEAD_DIM), F32))
            km_scr[h] = jnp.concatenate(rows, axis=0)

    t0 = pl.multiple_of(i * bs, bs)
    cos_q = cos_ref[pl.ds(t0, bs), :]
    sin_q = sin_ref[pl.ds(t0, bs), :]
    n_gate = km_scr.shape[1]
    blk = lax.broadcasted_iota(jnp.int32, (n_gate, bs), 0)
    eye = jnp.where(lax.broadcasted_iota(jnp.int32, (n_gate, LANES), 0)
                    == lax.broadcasted_iota(jnp.int32, (n_gate, LANES), 1), 1.0, 0.0).astype(BF16)
    causal = lax.broadcasted_iota(jnp.int32, (bs, bs), 1) <= lax.broadcasted_iota(jnp.int32, (bs, bs), 0)

    qbs, sel_bs, init = [], [], []
    for h, cols in enumerate(heads):
        qr = _rotate(q_ref[0, :, cols], cos_q, sin_q)
        gate_t = lax.dot_general(km_scr[h], qr, _NT, precision=HIGHEST, preferred_element_type=F32)
        val = jnp.where(blk < i, gate_t, NEG_INF)
        sel_t = jnp.where(_rank_before(val, nb, 0) < MOBA_TOPK, jnp.where(val > 0.5 * NEG_INF, 1.0, 0.0), 0.0)
        sel_bs.append(lax.dot_general(sel_t.astype(BF16), eye, _TN, preferred_element_type=F32).astype(BF16))
        qb = (qr * (HEAD_DIM ** -0.5)).astype(BF16)
        qbs.append(qb)
        s = lax.dot_general(qb, kr_scr[pl.ds(t0, bs), cols], _NT, preferred_element_type=F32)
        s = jnp.where(causal, s, NEG_INF)
        m0 = jnp.max(s, axis=1, keepdims=True)
        e = jnp.exp(s - m0)
        init += [m0, jnp.sum(e, axis=1, keepdims=True),
                 jnp.dot(e.astype(BF16), vb_scr[pl.ds(t0, bs), cols], preferred_element_type=F32)]

    def past_block(j, carry):
        k0 = pl.multiple_of(j * bs, bs)
        expand = jnp.where(lax.broadcasted_iota(jnp.int32, (LANES, bs), 0) == j, 1.0, 0.0).astype(BF16)
        out = []
        for h, cols in enumerate(heads):
            m_prev, l_prev, acc = carry[3 * h:3 * h + 3]
            picked = jnp.dot(sel_bs[h], expand, preferred_element_type=F32)
            s = lax.dot_general(qbs[h], kr_scr[pl.ds(k0, bs), cols], _NT, preferred_element_type=F32)
            s = jnp.where(picked > 0.5, s, NEG_INF)
            m_new = jnp.maximum(m_prev, jnp.max(s, axis=1, keepdims=True))
            alpha = jnp.exp(m_prev - m_new)
            e = jnp.exp(s - m_new)
            out += [m_new, alpha * l_prev + jnp.sum(e, axis=1, keepdims=True),
                    alpha * acc + jnp.dot(e.astype(BF16), vb_scr[pl.ds(k0, bs), cols], preferred_element_type=F32)]
        return tuple(out)

    fin = lax.fori_loop(0, i, past_block, tuple(init))
    for h, cols in enumerate(heads):
        o_ref[0, :, cols] = fin[3 * h + 2] / jnp.maximum(fin[3 * h + 1], 1e-30)


def _moba(z3, cos2, sin2):
    b, s_len, _ = z3.shape
    nb = s_len // MOBA_BLOCK
    q_off = 4 * MLSTM_W // MOBA_W

    def kv(off):
        return pl.BlockSpec((1, s_len, MOBA_W), lambda bi, i: (bi, 0, off))

    in_specs = [pl.BlockSpec((1, MOBA_BLOCK, MOBA_W), lambda bi, i: (bi, i, q_off)),
                kv(q_off + 1), kv(q_off + 2),
                pl.BlockSpec((s_len, HEAD_DIM), lambda bi, i: (0, 0)),
                pl.BlockSpec((s_len, HEAD_DIM), lambda bi, i: (0, 0))]
    scratch = [pltpu.VMEM((s_len, MOBA_W), BF16), pltpu.VMEM((s_len, MOBA_W), BF16),
               pltpu.VMEM((MOBA_HEADS, BF16_SUBLANES, HEAD_DIM), F32)]
    return pl.pallas_call(
        _moba_body, grid=(b, nb), in_specs=in_specs,
        out_specs=pl.BlockSpec((1, MOBA_BLOCK, MOBA_W), lambda bi, i: (bi, i, 0)),
        out_shape=jax.ShapeDtypeStruct((b, s_len, MOBA_W), F32),
        scratch_shapes=scratch, compiler_params=_cparams(("parallel", "arbitrary")),
        name="moba")(z3, z3, z3, cos2, sin2)


def _gelu_tanh(x):
    return x * (0.5 * (1.0 + jnp.tanh(np.sqrt(2.0 / np.pi) * (x + 0.044715 * (x * x * x)))))


def _nsa_compress_body(kc_ref, vc_ref, pe_ref, wk1_ref, wk2_ref, wv1_ref, wv2_ref, ko_ref, vo_ref):
    n_rows = kc_ref.shape[1] // CMP_STRIDE
    halves = CMP_LEN // CMP_STRIDE
    assert halves == 2

    def compress(x_ref, pe, w1_ref, w2_ref):
        ya = jnp.zeros((n_rows, CMP_HIDDEN), F32)
        yb = jnp.zeros((n_rows, CMP_HIDDEN), F32)
        for l in range(CMP_STRIDE):
            r = x_ref[0, pl.ds(l, n_rows, stride=CMP_STRIDE), :]
            la, lb = l, CMP_STRIDE + l
            ya = ya + jnp.dot((r + pe[la:la + 1, :]).astype(BF16), w1_ref[la * HEAD_DIM:(la + 1) * HEAD_DIM, :],
                              preferred_element_type=F32)
            yb = yb + jnp.dot((r + pe[lb:lb + 1, :]).astype(BF16), w1_ref[lb * HEAD_DIM:(lb + 1) * HEAD_DIM, :],
                              preferred_element_type=F32)
        pre = ya + pltpu.roll(yb, n_rows - 1, axis=0)
        return jnp.dot(_gelu_tanh(pre).astype(BF16), w2_ref[...], preferred_element_type=F32)

    ko_ref[0, 0] = compress(kc_ref, pe_ref[0], wk1_ref, wk2_ref)
    vo_ref[0, 0] = compress(vc_ref, pe_ref[1], wv1_ref, wv2_ref)


def _nsa_compress(z3, cmp_pos, wk1, wk2, wv1, wv2):
    b, s_len, _ = z3.shape
    n_rows = s_len // CMP_STRIDE
    kc_off = NSA_W // LANES
    vc_off = kc_off + NSA_GROUPS
    const2 = lambda bi, gi: (0, 0)
    in_specs = [pl.BlockSpec((1, s_len, HEAD_DIM), lambda bi, gi: (bi, 0, kc_off + gi)),
                pl.BlockSpec((1, s_len, HEAD_DIM), lambda bi, gi: (bi, 0, vc_off + gi)),
                pl.BlockSpec(cmp_pos.shape, lambda bi, gi: (0, 0, 0)),
                pl.BlockSpec(wk1.shape, const2), pl.BlockSpec(wk2.shape, const2),
                pl.BlockSpec(wv1.shape, const2), pl.BlockSpec(wv2.shape, const2)]
    out_spec = pl.BlockSpec((1, 1, n_rows, HEAD_DIM), lambda bi, gi: (bi, gi, 0, 0))
    out_sds = jax.ShapeDtypeStruct((b, NSA_GROUPS, n_rows, HEAD_DIM), F32)
    return pl.pallas_call(
        _nsa_compress_body, grid=(b, NSA_GROUPS), in_specs=in_specs,
        out_specs=[out_spec, out_spec], out_shape=[out_sds, out_sds],
        compiler_params=_cparams(("parallel", "parallel")), name="nsa_compress")(
            z3, z3, cmp_pos, wk1, wk2, wv1, wv2)


def _nsa_body(q_ref, ks_ref, vs_ref, kw_ref, vw_ref, kc_ref, vc_ref, gate_ref, cos_ref, sin_ref, ov_ref,
              o_ref, krs, vsb, krw, vwb):
    i = pl.program_id(2)
    tq = NSA_TQ
    tk = NSA_TQ
    scale = HEAD_DIM ** -0.5
    n_sel_blk = ks_ref.shape[1] // SEL_BLOCK
    n_cmp = (ks_ref.shape[1] - CMP_LEN) // CMP_STRIDE + 1
    shift = SEL_BLOCK.bit_length() - 1
    assert 1 << shift == SEL_BLOCK and n_sel_blk <= LANES

    @pl.when(i == 0)
    def _():
        krs[...] = _rotate(ks_ref[0], cos_ref[...], sin_ref[...]).astype(BF16)
        krw[...] = _rotate(kw_ref[0], cos_ref[...], sin_ref[...]).astype(BF16)
        vsb[...] = vs_ref[0].astype(BF16)
        vwb[...] = vw_ref[0].astype(BF16)

    t0 = pl.multiple_of(i * tq, tq)
    pos = t0 + lax.broadcasted_iota(jnp.int32, (tq, 1), 0)
    pos_row = t0 + lax.broadcasted_iota(jnp.int32, (1, tq), 1)
    cos_q = cos_ref[pl.ds(t0, tq), :]
    sin_q = sin_ref[pl.ds(t0, tq), :]

    kc = kc_ref[0, 0].astype(BF16)
    vc = vc_ref[0, 0].astype(BF16)
    n_col = lax.broadcasted_iota(jnp.int32, (LANES, 1), 0)
    cmp_end = jnp.where(n_col < n_cmp, n_col * CMP_STRIDE + (CMP_LEN - 1), jnp.iinfo(jnp.int32).max)
    cmp_ok = cmp_end <= pos_row
    qr_heads, o_cmp = [], []
    p_sum = jnp.zeros((LANES, tq), F32)
    for r in range(NSA_REP):
        q = q_ref[0, :, r * HEAD_DIM:(r + 1) * HEAD_DIM] * scale
        qr_heads.append(_rotate(q, cos_q, sin_q).astype(BF16))
        s = lax.dot_general(kc, q.astype(BF16), _NT, preferred_element_type=F32)
        s = jnp.where(cmp_ok, s, NEG_INF)
        e = jnp.where(cmp_ok, jnp.exp(s - jnp.max(s, axis=0, keepdims=True)), 0.0)
        p = e / jnp.maximum(jnp.sum(e, axis=0, keepdims=True), 1e-30)
        o_cmp.append(lax.dot_general(p.astype(BF16), vc, _TN, preferred_element_type=F32))
        p_sum = p_sum + p
    imp = jnp.dot(ov_ref[...], p_sum, precision=HIGHEST, preferred_element_type=F32)

    behind = (pos_row >> shift) - lax.broadcasted_iota(jnp.int32, imp.shape, 0)
    val = jnp.where(behind == 0, FORCED_SCORE, jnp.where(behind == 1, FORCED_SCORE, imp))
    val = jnp.where(lax.broadcasted_iota(jnp.int32, imp.shape, 0) == 0, FORCED_SCORE, val)
    val = jnp.where(behind >= 0, val, NEG_INF)
    sel_t = jnp.where(_rank_before(val, n_sel_blk, 0) < SEL_TOPN, jnp.where(val > 0.5 * NEG_INF, 1.0, 0.0), 0.0)
    eye = jnp.where(lax.broadcasted_iota(jnp.int32, (n_sel_blk, LANES), 0)
                    == lax.broadcasted_iota(jnp.int32, (n_sel_blk, LANES), 1), 1.0, 0.0).astype(BF16)
    sel_b = lax.dot_general(sel_t.astype(BF16), eye, _TN, preferred_element_type=F32).astype(BF16)

    def sel_tile(k0, carry, diagonal):
        k_t = krs[pl.ds(k0, tk), :]
        v_t = vsb[pl.ds(k0, tk), :]
        blk_of_key = (k0 + lax.broadcasted_iota(jnp.int32, (LANES, tk), 1)) >> shift
        expand = jnp.where(blk_of_key == lax.broadcasted_iota(jnp.int32, (LANES, tk), 0), 1.0, 0.0).astype(BF16)
        picked = jnp.dot(sel_b, expand, preferred_element_type=F32)
        bias = jnp.where(picked > 0.5, 0.0, NEG_INF)
        if diagonal:
            bias = jnp.where(lax.broadcasted_iota(jnp.int32, (tq, tk), 1)
                             <= lax.broadcasted_iota(jnp.int32, (tq, tk), 0), bias, NEG_INF)
        out = []
        for r in range(NSA_REP):
            s = lax.dot_general(qr_heads[r], k_t, _NT, preferred_element_type=F32) + bias
            if carry is None:
                m_new = jnp.max(s, axis=1, keepdims=True)
                e = jnp.exp(s - m_new)
                out += [m_new, jnp.sum(e, axis=1, keepdims=True),
                        jnp.dot(e.astype(BF16), v_t, preferred_element_type=F32)]
            else:
                m_prev, l_prev, acc = carry[3 * r:3 * r + 3]
                m_new = jnp.maximum(m_prev, jnp.max(s, axis=1, keepdims=True))
                alpha = jnp.exp(m_prev - m_new)
                e = jnp.exp(s - m_new)
                out += [m_new, alpha * l_prev + jnp.sum(e, axis=1, keepdims=True),
                        alpha * acc + jnp.dot(e.astype(BF16), v_t, preferred_element_type=F32)]
        return tuple(out)

    init = sel_tile(t0, None, True)
    fin = lax.fori_loop(0, i, lambda kt, c: sel_tile(pl.multiple_of(kt * tk, tk), c, False), init)

    span = WINDOW + tq
    start = pl.multiple_of(jnp.maximum(t0 - WINDOW, 0), tq)
    k_w = krw[pl.ds(start, span), :]
    v_w = vwb[pl.ds(start, span), :]
    gap = pos - (start + lax.broadcasted_iota(jnp.int32, (tq, span), 1))
    win_bias = jnp.where(gap >= 0, jnp.where(gap < WINDOW, 0.0, NEG_INF), NEG_INF)

    gates = _sigmoid(gate_ref[0])
    for r in range(NSA_REP):
        s = lax.dot_general(qr_heads[r], k_w, _NT, preferred_element_type=F32) + win_bias
        e = jnp.exp(s - jnp.max(s, axis=1, keepdims=True))
        o_win = (jnp.dot(e.astype(BF16), v_w, preferred_element_type=F32)
                 / jnp.maximum(jnp.sum(e, axis=1, keepdims=True), 1e-30))
        o_sel = fin[3 * r + 2] / jnp.maximum(fin[3 * r + 1], 1e-30)
        g0, g1, g2 = (gates[:, 3 * r + c:3 * r + c + 1] for c in range(3))
        o_ref[0, :, r * HEAD_DIM:(r + 1) * HEAD_DIM] = g0 * o_cmp[r] + g1 * o_sel + g2 * o_win


def _nsa_attention(z3, k_cmp, v_cmp, cos2, sin2):
    b, s_len, _ = z3.shape
    assert s_len >= WINDOW + NSA_TQ and s_len % NSA_TQ == 0
    n_cmp = (s_len - CMP_LEN) // CMP_STRIDE + 1
    n_blk = s_len // SEL_BLOCK
    n_rows = k_cmp.shape[2]
    assert n_rows == LANES
    cmp_start = np.arange(n_rows) * CMP_STRIDE
    cmp_end = cmp_start + CMP_LEN - 1
    blk_lo = np.arange(LANES) * SEL_BLOCK
    overlap = ((cmp_start[:, None] <= blk_lo[None, :] + SEL_BLOCK - 1) & (cmp_end[:, None] >= blk_lo[None, :])
               & (np.arange(n_rows)[:, None] < n_cmp)).astype(np.float32)
    overlap_t = np.ascontiguousarray(overlap.T[:n_blk])
    base = NSA_W // LANES

    def kv(off):
        return pl.BlockSpec((1, s_len, HEAD_DIM), lambda bi, gi, i: (bi, 0, base + off * NSA_GROUPS + gi))

    cmp_spec = pl.BlockSpec((1, 1, n_rows, HEAD_DIM), lambda bi, gi, i: (bi, gi, 0, 0))
    const2 = lambda bi, gi, i: (0, 0)
    in_specs = [pl.BlockSpec((1, NSA_TQ, NSA_REP * HEAD_DIM), lambda bi, gi, i: (bi, i, gi)),
                kv(2), kv(3), kv(4), kv(5), cmp_spec, cmp_spec,
                pl.BlockSpec((1, NSA_TQ, LANES), lambda bi, gi, i: (bi, i, ODD_GATE_BLK + gi)),
                pl.BlockSpec((s_len, HEAD_DIM), const2), pl.BlockSpec((s_len, HEAD_DIM), const2),
                pl.BlockSpec((n_blk, n_rows), const2)]
    scratch = [pltpu.VMEM((s_len, HEAD_DIM), BF16) for _ in range(4)]
    return pl.pallas_call(
        _nsa_body, grid=(b, NSA_GROUPS, s_len // NSA_TQ), in_specs=in_specs,
        out_specs=pl.BlockSpec((1, NSA_TQ, NSA_REP * HEAD_DIM), lambda bi, gi, i: (bi, i, gi)),
        out_shape=jax.ShapeDtypeStruct((b, s_len, NSA_W), F32),
        scratch_shapes=scratch, compiler_params=_cparams(("parallel", "parallel", "arbitrary")),
        name="nsa_attention")(z3, z3, z3, z3, z3, k_cmp, v_cmp, z3, cos2, sin2, jnp.asarray(overlap_t))


def _split_bf16(x):
    hi = x.astype(BF16)
    return hi, (x - hi.astype(F32)).astype(BF16)


def _proj_moe_body(*refs, n_act, final_norm):
    res_ref = refs[0]
    a_refs = refs[1:1 + n_act]
    wo_refs = refs[1 + n_act:1 + 2 * n_act]
    (g_ref, wrh_ref, wrl_ref, br_ref, w1_ref, w3_ref, w2_ref, fg_ref, o_ref,
     xn_scr, cw_scr, hd_scr) = refs[1 + 2 * n_act:]
    j = pl.program_id(1)
    tm = res_ref.shape[0]
    lane = lax.broadcasted_iota(jnp.int32, (tm, LANES), 1).astype(F32)
    neg = float("-inf")

    @pl.when(j == 0)
    def _():
        h = res_ref[...]
        for a_ref, wo_ref in zip(a_refs, wo_refs):
            h = h + jnp.dot(a_ref[...].astype(BF16), wo_ref[...], preferred_element_type=F32)
        o_ref[...] = h
        xn = _rms(h, g_ref[...])
        xh, xl = _split_bf16(xn)
        xn_scr[...] = xh
        logits = (jnp.dot(xh, wrh_ref[...], preferred_element_type=F32)
                  + (jnp.dot(xl, wrh_ref[...], preferred_element_type=F32)
                     + jnp.dot(xh, wrl_ref[...], preferred_element_type=F32))) + br_ref[...]
        is_g = lane < MOE_GROUPS
        gl = jnp.where(is_g, logits, neg)
        g_max = jnp.max(gl, axis=1, keepdims=True)
        g_w = 1.0 / jnp.sum(jnp.where(is_g, jnp.exp(gl - g_max), 0.0), axis=1, keepdims=True)
        g_top = jnp.min(jnp.where(gl == g_max, lane, float(LANES)), axis=1, keepdims=True)
        lo = MOE_GROUPS + MOE_EPG * g_top
        el = jnp.where(lane >= lo, jnp.where(lane < lo + MOE_EPG, logits, neg), neg)
        v1 = jnp.max(el, axis=1, keepdims=True)
        i1 = jnp.min(jnp.where(el == v1, lane, float(LANES)), axis=1, keepdims=True)
        el2 = jnp.where(lane == i1, neg, el)
        v2 = jnp.max(el2, axis=1, keepdims=True)
        i2 = jnp.min(jnp.where(el2 == v2, lane, float(LANES)), axis=1, keepdims=True)
        e2 = jnp.exp(v2 - v1)
        den = 1.0 + e2
        cw_scr[...] = (jnp.where(lane == i1, g_w / den, 0.0) + jnp.where(lane == i2, g_w * e2 / den, 0.0))

    xn = xn_scr[...]
    cw = cw_scr[...]
    for r in range(MOE_EPG):
        h1 = jnp.dot(xn, w1_ref[r], preferred_element_type=F32)
        h3 = jnp.dot(xn, w3_ref[r], preferred_element_type=F32)
        e_lane = (MOE_GROUPS + MOE_EPG * j + r).astype(F32)
        col = jnp.sum(jnp.where(lane == e_lane, cw, 0.0), axis=1, keepdims=True)
        hd_scr[:, r * MOE_HIDDEN:(r + 1) * MOE_HIDDEN] = ((h1 * _sigmoid(h1)) * h3 * col).astype(BF16)
    o_ref[...] += jnp.dot(hd_scr[...], w2_ref[...], preferred_element_type=F32)

    if final_norm:
        @pl.when(j == pl.num_programs(1) - 1)
        def _():
            o_ref[...] = _rms(o_ref[...], fg_ref[...])


def _proj_moe(res2d, acts, w_outs, gain, w_g, b_g, w_e, b_e, w1, w3, w2, final_gain, final_norm, tm=1024):
    t, d = res2d.shape
    n_act = len(acts)
    wr = jnp.zeros((d, LANES), F32).at[:, :MOE_GROUPS].set(w_g).at[:, MOE_GROUPS:MOE_GROUPS + MOE_EXPERTS].set(w_e)
    wr_hi, wr_lo = _split_bf16(wr)
    br = jnp.zeros((1, LANES), F32).at[0, :MOE_GROUPS].set(b_g).at[0, MOE_GROUPS:MOE_GROUPS + MOE_EXPERTS].set(b_e)
    tile = lambda i, j: (i, 0)
    const = lambda i, j: (0, 0)
    group_hidden = MOE_EPG * MOE_HIDDEN
    in_specs = [pl.BlockSpec((tm, d), tile)]
    in_specs += [pl.BlockSpec((tm, a.shape[1]), tile) for a in acts]
    in_specs += [pl.BlockSpec(w.shape, const) for w in w_outs]
    in_specs += [pl.BlockSpec((1, d), const),
                 pl.BlockSpec((d, LANES), const), pl.BlockSpec((d, LANES), const),
                 pl.BlockSpec((1, LANES), const),
                 pl.BlockSpec((MOE_EPG, d, MOE_HIDDEN), lambda i, j: (j, 0, 0)),
                 pl.BlockSpec((MOE_EPG, d, MOE_HIDDEN), lambda i, j: (j, 0, 0)),
                 pl.BlockSpec((group_hidden, d), lambda i, j: (j, 0)),
                 pl.BlockSpec((1, d), const)]
    scratch = [pltpu.VMEM((tm, d), BF16), pltpu.VMEM((tm, LANES), F32), pltpu.VMEM((tm, group_hidden), BF16)]
    return pl.pallas_call(
        functools.partial(_proj_moe_body, n_act=n_act, final_norm=final_norm),
        grid=(t // tm, MOE_GROUPS), in_specs=in_specs,
        out_specs=pl.BlockSpec((tm, d), tile),
        out_shape=jax.ShapeDtypeStruct((t, d), F32),
        scratch_shapes=scratch, compiler_params=_cparams(("parallel", "arbitrary")),
        name="out_proj_moe")(res2d, *acts, *w_outs, gain.reshape(1, d), wr_hi, wr_lo, br,
                             w1.astype(BF16), w3.astype(BF16), w2.astype(BF16).reshape(MOE_EXPERTS * MOE_HIDDEN, d),
                             final_gain.reshape(1, d))


def _rope_tables(s_len):
    half = HEAD_DIM // 2
    inv = ROPE_THETA ** (-(jnp.arange(half, dtype=F32) / half))
    ang = jnp.arange(s_len, dtype=F32)[:, None] * inv[None, :]
    cos, sin = jnp.cos(ang), jnp.sin(ang)
    return jnp.concatenate([cos, cos], axis=-1), jnp.concatenate([-sin, sin], axis=-1)


def _even_weights(w_in):
    n_if = 2 * MLSTM_HEADS
    a = 4 * MLSTM_W
    w_main = jnp.concatenate([w_in[:, :a], w_in[:, a + n_if:],
                              w_in[:, a:a + n_if], jnp.zeros((w_in.shape[0], LANES - n_if), w_in.dtype)], axis=1)
    w_if_t = jnp.concatenate([w_in[:, a:a + n_if].T, jnp.zeros((16 - n_if, w_in.shape[0]), w_in.dtype)], axis=0)
    return w_main.astype(BF16), w_if_t.astype(BF16)


def _odd_weights(w_in):
    a = NSA_W + 6 * NSA_KV_W
    per_group = NSA_REP * 3
    pad = jnp.zeros((w_in.shape[0], LANES - per_group), w_in.dtype)
    cols = [w_in[:, :a]]
    for g in range(NSA_GROUPS):
        cols += [w_in[:, a + g * per_group:a + (g + 1) * per_group], pad]
    return jnp.concatenate(cols, axis=1).astype(BF16)


def kernel(x, mix_norm_0, w_in_0, mlstm_conv_0, mlstm_gate_b_0, mlstm_head_norm_0, w_out_0, ffn_norm_0, router_group_0, router_group_b_0, router_expert_0, router_expert_b_0, moe_w1_0, moe_w3_0, moe_w2_0, mix_norm_1, w_in_1, nsa_cmp_pos_1, nsa_cmp_k1_1, nsa_cmp_k2_1, nsa_cmp_v1_1, nsa_cmp_v2_1, w_out_1, ffn_norm_1, router_group_1, router_group_b_1, router_expert_1, router_expert_b_1, moe_w1_1, moe_w3_1, moe_w2_1, final_norm):
    b, s_len, d = x.shape
    t = b * s_len
    cos2, sin2 = _rope_tables(s_len)
    x2d = x.reshape(t, d)

    w_main, w_if_t = _even_weights(w_in_0)
    z0, gt = _norm_matmul(x2d, mix_norm_0, w_main, w_if_t)
    z0 = z0.reshape(b, s_len, EVEN_N)
    h_m = _mlstm(z0, gt, mlstm_conv_0, mlstm_gate_b_0, mlstm_head_norm_0)
    o_b = _moba(z0, cos2, sin2)
    w_out = w_out_0.astype(BF16)
    h = _proj_moe(x2d, [h_m.reshape(t, MLSTM_W), o_b.reshape(t, MOBA_W)], [w_out[:MLSTM_W], w_out[MLSTM_W:]],
                  ffn_norm_0, router_group_0, router_group_b_0, router_expert_0, router_expert_b_0,
                  moe_w1_0, moe_w3_0, moe_w2_0, final_norm, False)

    z1 = _norm_matmul(h, mix_norm_1, _odd_weights(w_in_1)).reshape(b, s_len, ODD_N)
    k_cmp, v_cmp = _nsa_compress(z1, nsa_cmp_pos_1, nsa_cmp_k1_1.astype(BF16), nsa_cmp_k2_1.astype(BF16),
                                 nsa_cmp_v1_1.astype(BF16), nsa_cmp_v2_1.astype(BF16))
    o = _nsa_attention(z1, k_cmp, v_cmp, cos2, sin2)
    h = _proj_moe(h, [o.reshape(t, NSA_W)], [w_out_1.astype(BF16)],
                  ffn_norm_1, router_group_1, router_group_b_1, router_expert_1, router_expert_b_1,
                  moe_w1_1, moe_w3_1, moe_w2_1, final_norm, True)
    return h.reshape(b, s_len, d)
```

```python
import functools

import numpy as np
import jax
import jax.numpy as jnp
from jax import lax
from jax.experimental import pallas as pl
from jax.experimental.pallas import tpu as pltpu

F32 = jnp.float32
BF16 = jnp.bfloat16
HIGHEST = lax.Precision.HIGHEST

LANES = 128
BF16_SUBLANES = 16
D_MODEL = 1024
HEAD_DIM = 128
ROPE_THETA = 10000.0
NORM_EPS = 1e-6
NEG_INF = -1e30
FORCED_SCORE = 1e4

MLSTM_HEADS = 4
MLSTM_W = MLSTM_HEADS * HEAD_DIM
MLSTM_CHUNK = 64
MLSTM_CONV = 4
MLSTM_GATE_CAP = 15.0
MOBA_HEADS = 4
MOBA_W = MOBA_HEADS * HEAD_DIM
MOBA_BLOCK = 256
MOBA_TOPK = 3

NSA_HEADS = 8
NSA_GROUPS = 2
NSA_REP = NSA_HEADS // NSA_GROUPS
NSA_W = NSA_HEADS * HEAD_DIM
NSA_KV_W = NSA_GROUPS * HEAD_DIM
CMP_LEN = 32
CMP_STRIDE = 16
CMP_HIDDEN = 256
SEL_BLOCK = 64
SEL_TOPN = 8
WINDOW = 512
NSA_TQ = 256

MOE_GROUPS = 4
MOE_EPG = 4
MOE_EXPERTS = MOE_GROUPS * MOE_EPG
MOE_HIDDEN = D_MODEL // 4

EVEN_N = 4 * MLSTM_W + 3 * MOBA_W + LANES
EVEN_IF_BLK = (4 * MLSTM_W + 3 * MOBA_W) // LANES
ODD_GATE_BLK = (NSA_W + 6 * NSA_KV_W) // LANES
ODD_N = NSA_W + 6 * NSA_KV_W + NSA_GROUPS * LANES

VMEM_LIMIT = 56 * 1024 * 1024

_NT = (((1,), (1,)), ((), ()))
_TN = (((0,), (0,)), ((), ()))


def _cparams(sem):
    return pltpu.CompilerParams(dimension_semantics=sem, vmem_limit_bytes=VMEM_LIMIT)


def _rms(x, g):
    return x * lax.rsqrt(jnp.mean(x * x, axis=-1, keepdims=True) + NORM_EPS) * g


def _sigmoid(x):
    return 1.0 / (1.0 + jnp.exp(-x))


def _log_sigmoid(x):
    return -(jnp.maximum(-x, 0.0) + jnp.log1p(jnp.exp(-jnp.abs(x))))


def _rotate(x, cos2, sin2):
    return x * cos2 + pltpu.roll(x, HEAD_DIM // 2, axis=1) * sin2


def _flash_steps(qs, k_ts, vt_ts, biases, carries):
    n = len(qs)
    ss = [lax.dot_general(k_ts[u], qs[u], _NT, preferred_element_type=F32) + biases[u] for u in range(n)]
    ms, ls, es, alphas = [], [], [], []
    for u in range(n):
        m_new = jnp.max(ss[u], axis=0, keepdims=True)
        if carries is not None:
            m_new = jnp.maximum(carries[3 * u], m_new)
            alphas.append(jnp.exp(carries[3 * u] - m_new))
        e = jnp.exp(ss[u] - m_new)
        l_new = jnp.sum(e, axis=0, keepdims=True)
        if carries is not None:
            l_new = alphas[u] * carries[3 * u + 1] + l_new
        ms.append(m_new)
        ls.append(l_new)
        es.append(e.astype(BF16))
    out = []
    for u in range(n):
        pv = jnp.dot(vt_ts[u], es[u], preferred_element_type=F32)
        if carries is not None:
            pv = alphas[u] * carries[3 * u + 2] + pv
        out += [ms[u], ls[u], pv]
    return out


def _store_transposed(dst_ref, row0, x):
    n_rows, n_cols = x.shape
    for r in range(0, n_rows, LANES):
        dst_ref[row0:row0 + n_cols, r:r + LANES] = x[r:r + LANES, :].T.astype(dst_ref.dtype)


def _rank_before(v, n_valid, axis):
    idx = lax.broadcasted_iota(jnp.int32, v.shape, axis)
    rank = jnp.zeros(v.shape, F32)
    for m in range(n_valid):
        vm = v[m:m + 1, :] if axis == 0 else v[:, m:m + 1]
        tie = jnp.where(idx > m, 1.0, 0.0)
        rank = rank + jnp.where(vm > v, 1.0, jnp.where(vm == v, tie, 0.0))
    return rank


def _norm_matmul_body(*refs, n_chunk, with_t):
    if with_t:
        x_ref, g_ref, w_ref, wt_ref, o_ref, ot_ref = refs
    else:
        x_ref, g_ref, w_ref, o_ref = refs
    yb = _rms(x_ref[...], g_ref[...]).astype(BF16)
    n = w_ref.shape[1]
    for c0 in range(0, n, n_chunk):
        c1 = min(n, c0 + n_chunk)
        o_ref[:, c0:c1] = jnp.dot(yb, w_ref[:, c0:c1], preferred_element_type=F32)
    if with_t:
        ot_ref[...] = lax.dot_general(wt_ref[...], yb, _NT, preferred_element_type=F32)


def _norm_matmul(x2d, gain, w, wt=None, tm=512):
    t, d = x2d.shape
    n = w.shape[1]
    with_t = wt is not None
    in_specs = [pl.BlockSpec((tm, d), lambda i: (i, 0)),
                pl.BlockSpec((1, d), lambda i: (0, 0)),
                pl.BlockSpec((d, n), lambda i: (0, 0))]
    out_specs = [pl.BlockSpec((tm, n), lambda i: (i, 0))]
    out_shape = [jax.ShapeDtypeStruct((t, n), F32)]
    args = [x2d, gain.reshape(1, d), w]
    if with_t:
        r = wt.shape[0]
        in_specs.append(pl.BlockSpec((r, d), lambda i: (0, 0)))
        out_specs.append(pl.BlockSpec((r, tm), lambda i: (0, i)))
        out_shape.append(jax.ShapeDtypeStruct((r, t), F32))
        args.append(wt)
    outs = pl.pallas_call(
        functools.partial(_norm_matmul_body, n_chunk=512, with_t=with_t),
        grid=(t // tm,), in_specs=in_specs, out_specs=out_specs, out_shape=out_shape,
        compiler_params=_cparams(("parallel",)), name="norm_in_proj")(*args)
    return outs if with_t else outs[0]


def _chunk_cumsum(x, axis):
    idx = lax.broadcasted_iota(jnp.int32, x.shape, axis) % MLSTM_CHUNK
    d = 1
    while d < MLSTM_CHUNK:
        x = x + jnp.where(idx >= d, pltpu.roll(x, d, axis=axis), 0.0)
        d *= 2
    return x


def _mlstm_body(q_ref, k_ref, v_ref, og_ref, if_ref, gt_ref, cwq_ref, cwk_ref, bcol_ref, brow_ref,
                gain_ref, out_ref, qs, ks, bcs, lics, brs, lirs):
    h = pl.program_id(1)
    s_len = q_ref.shape[1]
    n_chunks = s_len // MLSTM_CHUNK
    L = MLSTM_CHUNK
    row = lax.broadcasted_iota(jnp.int32, (s_len, HEAD_DIM), 0)

    def conv_silu(x, w):
        acc = x * w[MLSTM_CONV - 1:MLSTM_CONV, :]
        for d in range(1, MLSTM_CONV):
            shifted = jnp.where(row >= d, pltpu.roll(x, d, axis=0), 0.0)
            acc = acc + shifted * w[MLSTM_CONV - 1 - d:MLSTM_CONV - d, :]
        return acc * _sigmoid(acc)

    qs[...] = conv_silu(q_ref[0], cwq_ref[...]) * (HEAD_DIM ** -0.5)
    ks[...] = conv_silu(k_ref[0], cwk_ref[...])

    pre = if_ref[0] + bcol_ref[...]
    pre = MLSTM_GATE_CAP * jnp.tanh(pre / MLSTM_GATE_CAP)
    lane = lax.broadcasted_iota(jnp.int32, pre.shape, 1)
    b_all = _chunk_cumsum(_log_sigmoid(pre), 0)
    lics[...] = jnp.sum(jnp.where(lane == h, pre, 0.0), axis=1, keepdims=True)
    bcs[...] = jnp.sum(jnp.where(lane == h + MLSTM_HEADS, b_all, 0.0), axis=1, keepdims=True)

    pr = gt_ref[...] + brow_ref[...]
    pr = MLSTM_GATE_CAP * jnp.tanh(pr / MLSTM_GATE_CAP)
    sub = lax.broadcasted_iota(jnp.int32, pr.shape, 0)
    b_rows = _chunk_cumsum(_log_sigmoid(pr), 1)
    li_row = jnp.sum(jnp.where(sub == h, pr, 0.0), axis=0, keepdims=True)
    b_row = jnp.sum(jnp.where(sub == h + MLSTM_HEADS, b_rows, 0.0), axis=0, keepdims=True)
    for c in range(n_chunks):
        brs[c] = b_row[:, c * L:(c + 1) * L]
        lirs[c] = li_row[:, c * L:(c + 1) * L]

    tri = (lax.broadcasted_iota(jnp.int32, (L, L), 1) <= lax.broadcasted_iota(jnp.int32, (L, L), 0))
    gain = gain_ref[...]

    def chunk(c, carry):
        c_state, n_state, m_state = carry
        r0 = pl.multiple_of(c * L, L)
        q = qs[pl.ds(r0, L), :]
        k = ks[pl.ds(r0, L), :]
        v = v_ref[0, pl.ds(r0, L), :]
        b_col = bcs[pl.ds(r0, L), :]
        li_col = lics[pl.ds(r0, L), :]
        b_r = brs[c]
        li_r = lirs[c]
        qb = q.astype(BF16)
        kb = k.astype(BF16)
        b_last = b_col[L - 1:L, :]
        g = b_last - b_col + li_col
        g_max = jnp.max(g, axis=0, keepdims=True)
        w = jnp.exp(g - g_max)
        kv = lax.dot_general((w * v).astype(BF16), kb, _TN, preferred_element_type=F32)
        ksum = jnp.sum(w * k, axis=0, keepdims=True)
        dmat = jnp.where(tri, b_col - b_r + li_r, NEG_INF)
        inter = b_col + m_state
        m_t = jnp.maximum(inter, jnp.max(dmat, axis=1, keepdims=True))
        s = lax.dot_general(qb, kb, _NT, preferred_element_type=F32) * jnp.exp(dmat - m_t)
        w_inter = jnp.exp(inter - m_t)
        num = (jnp.dot(s.astype(BF16), v.astype(BF16), preferred_element_type=F32)
               + w_inter * lax.dot_general(qb, c_state.astype(BF16), _NT, preferred_element_type=F32))
        den = jnp.sum(s, axis=1, keepdims=True) + w_inter * jnp.sum(q * n_state, axis=1, keepdims=True)
        ht = num / jnp.maximum(jnp.abs(den), jnp.exp(-m_t))
        ht = ht * lax.rsqrt(jnp.mean(ht * ht, axis=1, keepdims=True) + NORM_EPS) * gain
        out_ref[0, pl.ds(r0, L), :] = ht * _sigmoid(og_ref[0, pl.ds(r0, L), :])
        m_new = jnp.maximum(b_last + m_state, g_max)
        a = jnp.exp(b_last + m_state - m_new)
        cc = jnp.exp(g_max - m_new)
        return a * c_state + cc * kv, a * n_state + cc * ksum, m_new

    init = (jnp.zeros((HEAD_DIM, HEAD_DIM), F32), jnp.zeros((1, HEAD_DIM), F32), jnp.zeros((1, 1), F32))
    lax.fori_loop(0, n_chunks, chunk, init, unroll=8)


def _mlstm(z3, gt, conv_w, gate_b, head_gain):
    b, s_len, _ = z3.shape
    n_chunks = s_len // MLSTM_CHUNK
    gr = gt.shape[0]
    bcol = jnp.zeros((1, LANES), F32).at[0, :2 * MLSTM_HEADS].set(gate_b)
    brow = jnp.zeros((gr, 1), F32).at[:2 * MLSTM_HEADS, 0].set(gate_b)

    def col(off):
        return pl.BlockSpec((1, s_len, HEAD_DIM), lambda bi, hi: (bi, 0, off + hi))

    in_specs = [col(0), col(MLSTM_HEADS), col(2 * MLSTM_HEADS), col(3 * MLSTM_HEADS),
                pl.BlockSpec((1, s_len, LANES), lambda bi, hi: (bi, 0, EVEN_IF_BLK)),
                pl.BlockSpec((gr, s_len), lambda bi, hi: (0, bi)),
                pl.BlockSpec((MLSTM_CONV, HEAD_DIM), lambda bi, hi: (0, hi)),
                pl.BlockSpec((MLSTM_CONV, HEAD_DIM), lambda bi, hi: (0, MLSTM_HEADS + hi)),
                pl.BlockSpec((1, LANES), lambda bi, hi: (0, 0)),
                pl.BlockSpec((gr, 1), lambda bi, hi: (0, 0)),
                pl.BlockSpec((1, HEAD_DIM), lambda bi, hi: (0, hi))]
    scratch = [pltpu.VMEM((s_len, HEAD_DIM), F32), pltpu.VMEM((s_len, HEAD_DIM), F32),
               pltpu.VMEM((s_len, 1), F32), pltpu.VMEM((s_len, 1), F32),
               pltpu.VMEM((n_chunks, 1, MLSTM_CHUNK), F32), pltpu.VMEM((n_chunks, 1, MLSTM_CHUNK), F32)]
    return pl.pallas_call(
        _mlstm_body, grid=(b, MLSTM_HEADS), in_specs=in_specs,
        out_specs=pl.BlockSpec((1, s_len, HEAD_DIM), lambda bi, hi: (bi, 0, hi)),
        out_shape=jax.ShapeDtypeStruct((b, s_len, MLSTM_W), F32),
        scratch_shapes=scratch, compiler_params=_cparams(("parallel", "parallel")),
        name="mlstm")(z3, z3, z3, z3, z3, gt, conv_w, conv_w, bcol, brow, head_gain.reshape(1, MLSTM_W))


def _moba_body(q_ref, k_ref, v_ref, cos_ref, sin_ref, o_ref, kr_scr, vt_scr, km_scr, sel_scr):
    i = pl.program_id(1)
    s_len = k_ref.shape[1]
    bs = MOBA_BLOCK
    nb = s_len // bs
    heads = [slice(h * HEAD_DIM, (h + 1) * HEAD_DIM) for h in range(MOBA_HEADS)]

    @pl.when(i == 0)
    def _():
        for h, cols in enumerate(heads):
            _store_transposed(vt_scr, h * HEAD_DIM, v_ref[0, :, cols])
            kr = _rotate(k_ref[0, :, cols], cos_ref[...], sin_ref[...])
            kr_scr[:, cols] = kr.astype(BF16)
            rows = [jnp.sum(kr[n * bs:(n + 1) * bs, :], axis=0, keepdims=True) / float(bs) for n in range(nb)]
            rows.append(jnp.zeros((km_scr.shape[1] - nb, HEAD_DIM), F32))
            km_scr[h] = jnp.concatenate(rows, axis=0)

    t0 = pl.multiple_of(i * bs, bs)
    cos_q = cos_ref[pl.ds(t0, bs), :]
    sin_q = sin_ref[pl.ds(t0, bs), :]
    blk = lax.broadcasted_iota(jnp.int32, (km_scr.shape[1], bs), 0)
    causal_bias = jnp.where(lax.broadcasted_iota(jnp.int32, (bs, bs), 0)
                            <= lax.broadcasted_iota(jnp.int32, (bs, bs), 1), 0.0, NEG_INF)

    qbs = []
    for h, cols in enumerate(heads):
        qr = _rotate(q_ref[0, :, cols], cos_q, sin_q)
        gate_t = lax.dot_general(km_scr[h], qr, _NT, precision=HIGHEST, preferred_element_type=F32)
        val = jnp.where(blk < i, gate_t, NEG_INF)
        picked = jnp.where(val > 0.5 * NEG_INF, _rank_before(val, nb, 0), float(nb)) < MOBA_TOPK
        sel_scr[h] = jnp.where(picked, 0.0, NEG_INF)
        qbs.append((qr * (HEAD_DIM ** -0.5)).astype(BF16))

    def key_tiles(k0):
        return ([kr_scr[pl.ds(k0, bs), cols] for cols in heads], [vt_scr[cols, pl.ds(k0, bs)] for cols in heads])

    init = _flash_steps(qbs, *key_tiles(t0), [causal_bias] * MOBA_HEADS, None)

    def past_block(j, carry):
        biases = [sel_scr[h, pl.ds(j, 1), :] for h in range(MOBA_HEADS)]
        return tuple(_flash_steps(qbs, *key_tiles(pl.multiple_of(j * bs, bs)), biases, carry))

    fin = lax.fori_loop(0, i, past_block, tuple(init))
    for h, cols in enumerate(heads):
        o_t = fin[3 * h + 2] / jnp.maximum(fin[3 * h + 1], 1e-30)
        o_ref[0, :, cols] = o_t.T


def _moba(z3, cos2, sin2):
    b, s_len, _ = z3.shape
    nb = s_len // MOBA_BLOCK
    q_off = 4 * MLSTM_W // MOBA_W

    def kv(off):
        return pl.BlockSpec((1, s_len, MOBA_W), lambda bi, i: (bi, 0, off))

    in_specs = [pl.BlockSpec((1, MOBA_BLOCK, MOBA_W), lambda bi, i: (bi, i, q_off)),
                kv(q_off + 1), kv(q_off + 2),
                pl.BlockSpec((s_len, HEAD_DIM), lambda bi, i: (0, 0)),
                pl.BlockSpec((s_len, HEAD_DIM), lambda bi, i: (0, 0))]
    scratch = [pltpu.VMEM((s_len, MOBA_W), BF16),
               pltpu.VMEM((MOBA_W, s_len), BF16),
               pltpu.VMEM((MOBA_HEADS, BF16_SUBLANES, HEAD_DIM), F32),
               pltpu.VMEM((MOBA_HEADS, BF16_SUBLANES, MOBA_BLOCK), F32)]
    return pl.pallas_call(
        _moba_body, grid=(b, nb), in_specs=in_specs,
        out_specs=pl.BlockSpec((1, MOBA_BLOCK, MOBA_W), lambda bi, i: (bi, i, 0)),
        out_shape=jax.ShapeDtypeStruct((b, s_len, MOBA_W), F32),
        scratch_shapes=scratch, compiler_params=_cparams(("parallel", "arbitrary")),
        name="moba")(z3, z3, z3, cos2, sin2)


def _gelu_tanh(x):
    return x * (0.5 * (1.0 + jnp.tanh(np.sqrt(2.0 / np.pi) * (x + 0.044715 * (x * x * x)))))


def _nsa_compress_body(kc_ref, vc_ref, pe_ref, wk1_ref, wk2_ref, wv1_ref, wv2_ref, ko_ref, vo_ref):
    n_rows = kc_ref.shape[1] // CMP_STRIDE
    halves = CMP_LEN // CMP_STRIDE
    assert halves == 2

    def compress(x_ref, pe, w1_ref, w2_ref):
        ya = jnp.zeros((n_rows, CMP_HIDDEN), F32)
        yb = jnp.zeros((n_rows, CMP_HIDDEN), F32)
        for l in range(CMP_STRIDE):
            r = x_ref[0, pl.ds(l, n_rows, stride=CMP_STRIDE), :]
            la, lb = l, CMP_STRIDE + l
            ya = ya + jnp.dot((r + pe[la:la + 1, :]).astype(BF16), w1_ref[la * HEAD_DIM:(la + 1) * HEAD_DIM, :],
                              preferred_element_type=F32)
            yb = yb + jnp.dot((r + pe[lb:lb + 1, :]).astype(BF16), w1_ref[lb * HEAD_DIM:(lb + 1) * HEAD_DIM, :],
                              preferred_element_type=F32)
        pre = ya + pltpu.roll(yb, n_rows - 1, axis=0)
        return jnp.dot(_gelu_tanh(pre).astype(BF16), w2_ref[...], preferred_element_type=F32)

    ko_ref[0, 0] = compress(kc_ref, pe_ref[0], wk1_ref, wk2_ref)
    vo_ref[0, 0] = compress(vc_ref, pe_ref[1], wv1_ref, wv2_ref)


def _nsa_compress(z3, cmp_pos, wk1, wk2, wv1, wv2):
    b, s_len, _ = z3.shape
    n_rows = s_len // CMP_STRIDE
    kc_off = NSA_W // LANES
    vc_off = kc_off + NSA_GROUPS
    const2 = lambda bi, gi: (0, 0)
    in_specs = [pl.BlockSpec((1, s_len, HEAD_DIM), lambda bi, gi: (bi, 0, kc_off + gi)),
                pl.BlockSpec((1, s_len, HEAD_DIM), lambda bi, gi: (bi, 0, vc_off + gi)),
                pl.BlockSpec(cmp_pos.shape, lambda bi, gi: (0, 0, 0)),
                pl.BlockSpec(wk1.shape, const2), pl.BlockSpec(wk2.shape, const2),
                pl.BlockSpec(wv1.shape, const2), pl.BlockSpec(wv2.shape, const2)]
    out_spec = pl.BlockSpec((1, 1, n_rows, HEAD_DIM), lambda bi, gi: (bi, gi, 0, 0))
    out_sds = jax.ShapeDtypeStruct((b, NSA_GROUPS, n_rows, HEAD_DIM), F32)
    return pl.pallas_call(
        _nsa_compress_body, grid=(b, NSA_GROUPS), in_specs=in_specs,
        out_specs=[out_spec, out_spec], out_shape=[out_sds, out_sds],
        compiler_params=_cparams(("parallel", "parallel")), name="nsa_compress")(
            z3, z3, cmp_pos, wk1, wk2, wv1, wv2)


def _nsa_body(q_ref, ks_ref, vs_ref, kw_ref, vw_ref, kc_ref, vc_ref, gate_ref, cos_ref, sin_ref, ov_ref,
              o_ref, krs, vst, krw, vwt, sel_scr):
    i = pl.program_id(2)
    tq = NSA_TQ
    tk = NSA_TQ
    scale = HEAD_DIM ** -0.5
    n_sel_blk = ks_ref.shape[1] // SEL_BLOCK
    n_cmp = (ks_ref.shape[1] - CMP_LEN) // CMP_STRIDE + 1
    shift = SEL_BLOCK.bit_length() - 1
    assert 1 << shift == SEL_BLOCK and n_sel_blk <= LANES
    blk_per_tile = tk // SEL_BLOCK

    @pl.when(i == 0)
    def _():
        krs[...] = _rotate(ks_ref[0], cos_ref[...], sin_ref[...]).astype(BF16)
        krw[...] = _rotate(kw_ref[0], cos_ref[...], sin_ref[...]).astype(BF16)
        _store_transposed(vst, 0, vs_ref[0])
        _store_transposed(vwt, 0, vw_ref[0])

    t0 = pl.multiple_of(i * tq, tq)
    pos_row = t0 + lax.broadcasted_iota(jnp.int32, (1, tq), 1)
    cos_q = cos_ref[pl.ds(t0, tq), :]
    sin_q = sin_ref[pl.ds(t0, tq), :]

    kc = kc_ref[0, 0].astype(BF16)
    vc = vc_ref[0, 0].astype(BF16)
    n_col = lax.broadcasted_iota(jnp.int32, (LANES, 1), 0)
    cmp_end = jnp.where(n_col < n_cmp, n_col * CMP_STRIDE + (CMP_LEN - 1), jnp.iinfo(jnp.int32).max)
    cmp_ok = cmp_end <= pos_row
    qr_heads, o_cmp = [], []
    p_sum = jnp.zeros((LANES, tq), F32)
    for r in range(NSA_REP):
        q = q_ref[0, :, r * HEAD_DIM:(r + 1) * HEAD_DIM] * scale
        qr_heads.append(_rotate(q, cos_q, sin_q).astype(BF16))
        s = lax.dot_general(kc, q.astype(BF16), _NT, preferred_element_type=F32)
        s = jnp.where(cmp_ok, s, NEG_INF)
        e = jnp.where(cmp_ok, jnp.exp(s - jnp.max(s, axis=0, keepdims=True)), 0.0)
        p = e / jnp.maximum(jnp.sum(e, axis=0, keepdims=True), 1e-30)
        o_cmp.append(lax.dot_general(p.astype(BF16), vc, _TN, preferred_element_type=F32))
        p_sum = p_sum + p
    imp = jnp.dot(ov_ref[...], p_sum, precision=HIGHEST, preferred_element_type=F32)

    behind = (pos_row >> shift) - lax.broadcasted_iota(jnp.int32, imp.shape, 0)
    val = jnp.where(behind == 0, FORCED_SCORE, jnp.where(behind == 1, FORCED_SCORE, imp))
    val = jnp.where(lax.broadcasted_iota(jnp.int32, imp.shape, 0) == 0, FORCED_SCORE, val)
    val = jnp.where(behind >= 0, val, NEG_INF)
    picked = jnp.where(val > 0.5 * NEG_INF, _rank_before(val, n_sel_blk, 0), float(n_sel_blk)) < SEL_TOPN
    sel_scr[...] = jnp.where(picked, 0.0, NEG_INF)

    key_i = lax.broadcasted_iota(jnp.int32, (tk, tq), 0)
    qry_i = lax.broadcasted_iota(jnp.int32, (tk, tq), 1)
    causal_bias = jnp.where(key_i <= qry_i, 0.0, NEG_INF)
    n = NSA_REP

    def sel_tile(k0, carry, diagonal):
        blk0 = pl.multiple_of(k0 >> shift, blk_per_tile)
        bias = jnp.concatenate([jnp.broadcast_to(sel_scr[pl.ds(blk0 + b, 1), :], (SEL_BLOCK, tq))
                                for b in range(blk_per_tile)], axis=0)
        if diagonal:
            bias = bias + causal_bias
        return tuple(_flash_steps(qr_heads, [krs[pl.ds(k0, tk), :]] * n, [vst[:, pl.ds(k0, tk)]] * n, [bias] * n,
                                  carry))

    init = sel_tile(t0, None, True)
    sel_fin = lax.fori_loop(0, i, lambda kt, c: sel_tile(pl.multiple_of(kt * tk, tk), c, False), init)

    assert WINDOW == 2 * tk
    start_1 = pl.multiple_of(jnp.maximum(t0 - tk, 0), tk)
    start_2 = pl.multiple_of(jnp.maximum(t0 - 2 * tk, 0), tk)
    bias_1 = jnp.where(i >= 1, 0.0, NEG_INF)
    bias_2 = jnp.where(key_i > qry_i, 0.0, NEG_INF) + jnp.where(i >= 2, 0.0, NEG_INF)

    def win_tile(k0, bias, carry):
        return _flash_steps(qr_heads, [krw[pl.ds(k0, tk), :]] * n, [vwt[:, pl.ds(k0, tk)]] * n, [bias] * n, carry)

    win_fin = win_tile(t0, causal_bias, None)
    win_fin = win_tile(start_1, bias_1, win_fin)
    win_fin = win_tile(start_2, bias_2, win_fin)

    gates = _sigmoid(gate_ref[0])
    gates_t = gates.T
    for r in range(NSA_REP):
        g_sel = gates_t[3 * r + 1:3 * r + 2, :] / jnp.maximum(sel_fin[3 * r + 1], 1e-30)
        g_win = gates_t[3 * r + 2:3 * r + 3, :] / jnp.maximum(win_fin[3 * r + 1], 1e-30)
        o_t = g_sel * sel_fin[3 * r + 2] + g_win * win_fin[3 * r + 2]
        o_ref[0, :, r * HEAD_DIM:(r + 1) * HEAD_DIM] = gates[:, 3 * r:3 * r + 1] * o_cmp[r] + o_t.T


def _nsa_attention(z3, k_cmp, v_cmp, cos2, sin2):
    b, s_len, _ = z3.shape
    assert s_len >= WINDOW + NSA_TQ and s_len % NSA_TQ == 0
    n_cmp = (s_len - CMP_LEN) // CMP_STRIDE + 1
    n_blk = s_len // SEL_BLOCK
    n_rows = k_cmp.shape[2]
    assert n_rows == LANES
    cmp_start = np.arange(n_rows) * CMP_STRIDE
    cmp_end = cmp_start + CMP_LEN - 1
    blk_lo = np.arange(LANES) * SEL_BLOCK
    overlap = ((cmp_start[:, None] <= blk_lo[None, :] + SEL_BLOCK - 1) & (cmp_end[:, None] >= blk_lo[None, :])
               & (np.arange(n_rows)[:, None] < n_cmp)).astype(np.float32)
    overlap_t = np.ascontiguousarray(overlap.T[:n_blk])
    base = NSA_W // LANES

    def kv(off):
        return pl.BlockSpec((1, s_len, HEAD_DIM), lambda bi, gi, i: (bi, 0, base + off * NSA_GROUPS + gi))

    cmp_spec = pl.BlockSpec((1, 1, n_rows, HEAD_DIM), lambda bi, gi, i: (bi, gi, 0, 0))
    const2 = lambda bi, gi, i: (0, 0)
    in_specs = [pl.BlockSpec((1, NSA_TQ, NSA_REP * HEAD_DIM), lambda bi, gi, i: (bi, i, gi)),
                kv(2), kv(3), kv(4), kv(5), cmp_spec, cmp_spec,
                pl.BlockSpec((1, NSA_TQ, LANES), lambda bi, gi, i: (bi, i, ODD_GATE_BLK + gi)),
                pl.BlockSpec((s_len, HEAD_DIM), const2), pl.BlockSpec((s_len, HEAD_DIM), const2),
                pl.BlockSpec((n_blk, n_rows), const2)]
    scratch = [pltpu.VMEM((s_len, HEAD_DIM), BF16), pltpu.VMEM((HEAD_DIM, s_len), BF16),
               pltpu.VMEM((s_len, HEAD_DIM), BF16), pltpu.VMEM((HEAD_DIM, s_len), BF16),
               pltpu.VMEM((n_blk, NSA_TQ), F32)]
    return pl.pallas_call(
        _nsa_body, grid=(b, NSA_GROUPS, s_len // NSA_TQ), in_specs=in_specs,
        out_specs=pl.BlockSpec((1, NSA_TQ, NSA_REP * HEAD_DIM), lambda bi, gi, i: (bi, i, gi)),
        out_shape=jax.ShapeDtypeStruct((b, s_len, NSA_W), F32),
        scratch_shapes=scratch, compiler_params=_cparams(("parallel", "parallel", "arbitrary")),
        name="nsa_attention")(z3, z3, z3, z3, z3, k_cmp, v_cmp, z3, cos2, sin2, jnp.asarray(overlap_t))


def _split_bf16(x):
    hi = x.astype(BF16)
    return hi, (x - hi.astype(F32)).astype(BF16)


def _proj_moe_body(*refs, n_act, final_norm):
    res_ref = refs[0]
    a_refs = refs[1:1 + n_act]
    wo_refs = refs[1 + n_act:1 + 2 * n_act]
    (g_ref, wrh_ref, wrl_ref, br_ref, w1_ref, w3_ref, w2_ref, fg_ref, o_ref,
     xn_scr, cw_scr, hd_scr) = refs[1 + 2 * n_act:]
    j = pl.program_id(1)
    tm = res_ref.shape[0]
    lane = lax.broadcasted_iota(jnp.int32, (tm, LANES), 1).astype(F32)
    neg = float("-inf")

    @pl.when(j == 0)
    def _():
        h = res_ref[...]
        for a_ref, wo_ref in zip(a_refs, wo_refs):
            h = h + jnp.dot(a_ref[...].astype(BF16), wo_ref[...], preferred_element_type=F32)
        o_ref[...] = h
        xn = _rms(h, g_ref[...])
        xh, xl = _split_bf16(xn)
        xn_scr[...] = xh
        logits = (jnp.dot(xh, wrh_ref[...], preferred_element_type=F32)
                  + (jnp.dot(xl, wrh_ref[...], preferred_element_type=F32)
                     + jnp.dot(xh, wrl_ref[...], preferred_element_type=F32))) + br_ref[...]
        is_g = lane < MOE_GROUPS
        gl = jnp.where(is_g, logits, neg)
        g_max = jnp.max(gl, axis=1, keepdims=True)
        g_w = 1.0 / jnp.sum(jnp.where(is_g, jnp.exp(gl - g_max), 0.0), axis=1, keepdims=True)
        g_top = jnp.min(jnp.where(gl == g_max, lane, float(LANES)), axis=1, keepdims=True)
        lo = MOE_GROUPS + MOE_EPG * g_top
        el = jnp.where(lane >= lo, jnp.where(lane < lo + MOE_EPG, logits, neg), neg)
        v1 = jnp.max(el, axis=1, keepdims=True)
        i1 = jnp.min(jnp.where(el == v1, lane, float(LANES)), axis=1, keepdims=True)
        el2 = jnp.where(lane == i1, neg, el)
        v2 = jnp.max(el2, axis=1, keepdims=True)
        i2 = jnp.min(jnp.where(el2 == v2, lane, float(LANES)), axis=1, keepdims=True)
        e2 = jnp.exp(v2 - v1)
        den = 1.0 + e2
        cw_scr[...] = (jnp.where(lane == i1, g_w / den, 0.0) + jnp.where(lane == i2, g_w * e2 / den, 0.0))

    xn = xn_scr[...]
    cw = cw_scr[...]
    for r in range(MOE_EPG):
        h1 = jnp.dot(xn, w1_ref[r], preferred_element_type=F32)
        h3 = jnp.dot(xn, w3_ref[r], preferred_element_type=F32)
        e_lane = (MOE_GROUPS + MOE_EPG * j + r).astype(F32)
        col = jnp.sum(jnp.where(lane == e_lane, cw, 0.0), axis=1, keepdims=True)
        hd_scr[:, r * MOE_HIDDEN:(r + 1) * MOE_HIDDEN] = ((h1 * _sigmoid(h1)) * h3 * col).astype(BF16)
    o_ref[...] += jnp.dot(hd_scr[...], w2_ref[...], preferred_element_type=F32)

    if final_norm:
        @pl.when(j == pl.num_programs(1) - 1)
        def _():
            o_ref[...] = _rms(o_ref[...], fg_ref[...])


def _proj_moe(res2d, acts, w_outs, gain, w_g, b_g, w_e, b_e, w1, w3, w2, final_gain, final_norm, tm=1024):
    t, d = res2d.shape
    n_act = len(acts)
    wr = jnp.zeros((d, LANES), F32).at[:, :MOE_GROUPS].set(w_g).at[:, MOE_GROUPS:MOE_GROUPS + MOE_EXPERTS].set(w_e)
    wr_hi, wr_lo = _split_bf16(wr)
    br = jnp.zeros((1, LANES), F32).at[0, :MOE_GROUPS].set(b_g).at[0, MOE_GROUPS:MOE_GROUPS + MOE_EXPERTS].set(b_e)
    tile = lambda i, j: (i, 0)
    const = lambda i, j: (0, 0)
    group_hidden = MOE_EPG * MOE_HIDDEN
    in_specs = [pl.BlockSpec((tm, d), tile)]
    in_specs += [pl.BlockSpec((tm, a.shape[1]), tile) for a in acts]
    in_specs += [pl.BlockSpec(w.shape, const) for w in w_outs]
    in_specs += [pl.BlockSpec((1, d), const),
                 pl.BlockSpec((d, LANES), const), pl.BlockSpec((d, LANES), const),
                 pl.BlockSpec((1, LANES), const),
                 pl.BlockSpec((MOE_EPG, d, MOE_HIDDEN), lambda i, j: (j, 0, 0)),
                 pl.BlockSpec((MOE_EPG, d, MOE_HIDDEN), lambda i, j: (j, 0, 0)),
                 pl.BlockSpec((group_hidden, d), lambda i, j: (j, 0)),
                 pl.BlockSpec((1, d), const)]
    scratch = [pltpu.VMEM((tm, d), BF16), pltpu.VMEM((tm, LANES), F32), pltpu.VMEM((tm, group_hidden), BF16)]
    return pl.pallas_call(
        functools.partial(_proj_moe_body, n_act=n_act, final_norm=final_norm),
        grid=(t // tm, MOE_GROUPS), in_specs=in_specs,
        out_specs=pl.BlockSpec((tm, d), tile),
        out_shape=jax.ShapeDtypeStruct((t, d), F32),
        scratch_shapes=scratch, compiler_params=_cparams(("parallel", "arbitrary")),
        name="out_proj_moe")(res2d, *acts, *w_outs, gain.reshape(1, d), wr_hi, wr_lo, br,
                             w1.astype(BF16), w3.astype(BF16), w2.astype(BF16).reshape(MOE_EXPERTS * MOE_HIDDEN, d),
                             final_gain.reshape(1, d))


def _rope_tables(s_len):
    half = HEAD_DIM // 2
    inv = ROPE_THETA ** (-(jnp.arange(half, dtype=F32) / half))
    ang = jnp.arange(s_len, dtype=F32)[:, None] * inv[None, :]
    cos, sin = jnp.cos(ang), jnp.sin(ang)
    return jnp.concatenate([cos, cos], axis=-1), jnp.concatenate([-sin, sin], axis=-1)


def _even_weights(w_in):
    n_if = 2 * MLSTM_HEADS
    a = 4 * MLSTM_W
    w_main = jnp.concatenate([w_in[:, :a], w_in[:, a + n_if:],
                              w_in[:, a:a + n_if], jnp.zeros((w_in.shape[0], LANES - n_if), w_in.dtype)], axis=1)
    w_if_t = jnp.concatenate([w_in[:, a:a + n_if].T, jnp.zeros((16 - n_if, w_in.shape[0]), w_in.dtype)], axis=0)
    return w_main.astype(BF16), w_if_t.astype(BF16)


def _odd_weights(w_in):
    a = NSA_W + 6 * NSA_KV_W
    per_group = NSA_REP * 3
    pad = jnp.zeros((w_in.shape[0], LANES - per_group), w_in.dtype)
    cols = [w_in[:, :a]]
    for g in range(NSA_GROUPS):
        cols += [w_in[:, a + g * per_group:a + (g + 1) * per_group], pad]
    return jnp.concatenate(cols, axis=1).astype(BF16)


def kernel(x, mix_norm_0, w_in_0, mlstm_conv_0, mlstm_gate_b_0, mlstm_head_norm_0, w_out_0, ffn_norm_0, router_group_0, router_group_b_0, router_expert_0, router_expert_b_0, moe_w1_0, moe_w3_0, moe_w2_0, mix_norm_1, w_in_1, nsa_cmp_pos_1, nsa_cmp_k1_1, nsa_cmp_k2_1, nsa_cmp_v1_1, nsa_cmp_v2_1, w_out_1, ffn_norm_1, router_group_1, router_group_b_1, router_expert_1, router_expert_b_1, moe_w1_1, moe_w3_1, moe_w2_1, final_norm):
    b, s_len, d = x.shape
    t = b * s_len
    cos2, sin2 = _rope_tables(s_len)
    x2d = x.reshape(t, d)

    w_main, w_if_t = _even_weights(w_in_0)
    z0, gt = _norm_matmul(x2d, mix_norm_0, w_main, w_if_t)
    z0 = z0.reshape(b, s_len, EVEN_N)
    h_m = _mlstm(z0, gt, mlstm_conv_0, mlstm_gate_b_0, mlstm_head_norm_0)
    o_b = _moba(z0, cos2, sin2)
    w_out = w_out_0.astype(BF16)
    h = _proj_moe(x2d, [h_m.reshape(t, MLSTM_W), o_b.reshape(t, MOBA_W)], [w_out[:MLSTM_W], w_out[MLSTM_W:]],
                  ffn_norm_0, router_group_0, router_group_b_0, router_expert_0, router_expert_b_0,
                  moe_w1_0, moe_w3_0, moe_w2_0, final_norm, False)

    z1 = _norm_matmul(h, mix_norm_1, _odd_weights(w_in_1)).reshape(b, s_len, ODD_N)
    k_cmp, v_cmp = _nsa_compress(z1, nsa_cmp_pos_1, nsa_cmp_k1_1.astype(BF16), nsa_cmp_k2_1.astype(BF16),
                                 nsa_cmp_v1_1.astype(BF16), nsa_cmp_v2_1.astype(BF16))
    o = _nsa_attention(z1, k_cmp, v_cmp, cos2, sin2)
    h = _proj_moe(h, [o.reshape(t, NSA_W)], [w_out_1.astype(BF16)],
                  ffn_norm_1, router_group_1, router_group_b_1, router_expert_1, router_expert_b_1,
                  moe_w1_1, moe_w3_1, moe_w2_1, final_norm, True)
    return h.reshape(b, s_len, d)
```

```python
import functools

import numpy as np
import jax
import jax.numpy as jnp
from jax import lax
from jax.experimental import pallas as pl
from jax.experimental.pallas import tpu as pltpu

F32 = jnp.float32
BF16 = jnp.bfloat16
HIGHEST = lax.Precision.HIGHEST

LANES = 128
BF16_SUBLANES = 16
D_MODEL = 1024
HEAD_DIM = 128
ROPE_THETA = 10000.0
NORM_EPS = 1e-6
NEG_INF = -1e30
FORCED_SCORE = 1e4

MLSTM_HEADS = 4
MLSTM_W = MLSTM_HEADS * HEAD_DIM
MLSTM_CHUNK = 64
MLSTM_CONV = 4
MLSTM_GATE_CAP = 15.0
MOBA_HEADS = 4
MOBA_W = MOBA_HEADS * HEAD_DIM
MOBA_BLOCK = 256
MOBA_TOPK = 3

NSA_HEADS = 8
NSA_GROUPS = 2
NSA_REP = NSA_HEADS // NSA_GROUPS
NSA_W = NSA_HEADS * HEAD_DIM
NSA_KV_W = NSA_GROUPS * HEAD_DIM
CMP_LEN = 32
CMP_STRIDE = 16
CMP_HIDDEN = 256
SEL_BLOCK = 64
SEL_TOPN = 8
WINDOW = 512
NSA_TQ = 256

MOE_GROUPS = 4
MOE_EPG = 4
MOE_EXPERTS = MOE_GROUPS * MOE_EPG
MOE_HIDDEN = D_MODEL // 4

EVEN_N = 4 * MLSTM_W + 3 * MOBA_W + LANES
EVEN_IF_BLK = (4 * MLSTM_W + 3 * MOBA_W) // LANES
ODD_GATE_BLK = (NSA_W + 6 * NSA_KV_W) // LANES
ODD_N = NSA_W + 6 * NSA_KV_W + NSA_GROUPS * LANES

VMEM_LIMIT = 56 * 1024 * 1024
VT_ROWS = HEAD_DIM + BF16_SUBLANES
LOG2_E = float(np.log2(np.e))

_NT = (((1,), (1,)), ((), ()))
_TN = (((0,), (0,)), ((), ()))


def _cparams(sem):
    return pltpu.CompilerParams(dimension_semantics=sem, vmem_limit_bytes=VMEM_LIMIT)


def _rms(x, g):
    return x * lax.rsqrt(jnp.mean(x * x, axis=-1, keepdims=True) + NORM_EPS) * g


def _sigmoid(x):
    return 1.0 / (1.0 + jnp.exp(-x))


def _log_sigmoid(x):
    return -(jnp.maximum(-x, 0.0) + jnp.log1p(jnp.exp(-jnp.abs(x))))


def _rotate(x, cos2, sin2):
    return x * cos2 + pltpu.roll(x, HEAD_DIM // 2, axis=1) * sin2


def _flash_steps(qs, k_ts, vt_ts, biases, carries):
    n = len(qs)
    ss = [lax.dot_general(k_ts[u], qs[u], _NT, preferred_element_type=F32) + biases[u] for u in range(n)]
    ms, es, alphas = [], [], []
    for u in range(n):
        m_new = jnp.max(ss[u], axis=0, keepdims=True)
        if carries is not None:
            m_new = jnp.maximum(carries[2 * u], m_new)
            alphas.append(jnp.exp2(carries[2 * u] - m_new))
        ms.append(m_new)
        es.append(jnp.exp2(ss[u] - m_new).astype(BF16))
    out = []
    for u in range(n):
        pv = jnp.dot(vt_ts[u], es[u], preferred_element_type=F32)
        if carries is not None:
            pv = alphas[u] * carries[2 * u + 1] + pv
        out += [ms[u], pv]
    return out


def _flash_output(acc):
    return acc[:HEAD_DIM] / jnp.maximum(acc[HEAD_DIM:HEAD_DIM + 1], 1e-30)


def _store_transposed(dst_ref, x):
    n_rows, n_cols = x.shape
    for r in range(0, n_rows, LANES):
        dst_ref[:n_cols, r:r + LANES] = x[r:r + LANES, :].T.astype(dst_ref.dtype)
    dst_ref[n_cols:, :] = jnp.ones((dst_ref.shape[0] - n_cols, n_rows), dst_ref.dtype)


def _rank_before(v, n_valid, axis):
    idx = lax.broadcasted_iota(jnp.int32, v.shape, axis)
    rank = jnp.zeros(v.shape, F32)
    for m in range(n_valid):
        vm = v[m:m + 1, :] if axis == 0 else v[:, m:m + 1]
        tie = jnp.where(idx > m, 1.0, 0.0)
        rank = rank + jnp.where(vm > v, 1.0, jnp.where(vm == v, tie, 0.0))
    return rank


def _norm_matmul_body(*refs, n_chunk, with_t):
    if with_t:
        x_ref, g_ref, w_ref, wt_ref, o_ref, ot_ref = refs
    else:
        x_ref, g_ref, w_ref, o_ref = refs
    yb = _rms(x_ref[...], g_ref[...]).astype(BF16)
    n = w_ref.shape[1]
    for c0 in range(0, n, n_chunk):
        c1 = min(n, c0 + n_chunk)
        o_ref[:, c0:c1] = jnp.dot(yb, w_ref[:, c0:c1], preferred_element_type=F32)
    if with_t:
        ot_ref[...] = lax.dot_general(wt_ref[...], yb, _NT, preferred_element_type=F32)


def _norm_matmul(x2d, gain, w, wt=None, tm=512):
    t, d = x2d.shape
    n = w.shape[1]
    with_t = wt is not None
    in_specs = [pl.BlockSpec((tm, d), lambda i: (i, 0)),
                pl.BlockSpec((1, d), lambda i: (0, 0)),
                pl.BlockSpec((d, n), lambda i: (0, 0))]
    out_specs = [pl.BlockSpec((tm, n), lambda i: (i, 0))]
    out_shape = [jax.ShapeDtypeStruct((t, n), F32)]
    args = [x2d, gain.reshape(1, d), w]
    if with_t:
        r = wt.shape[0]
        in_specs.append(pl.BlockSpec((r, d), lambda i: (0, 0)))
        out_specs.append(pl.BlockSpec((r, tm), lambda i: (0, i)))
        out_shape.append(jax.ShapeDtypeStruct((r, t), F32))
        args.append(wt)
    outs = pl.pallas_call(
        functools.partial(_norm_matmul_body, n_chunk=512, with_t=with_t),
        grid=(t // tm,), in_specs=in_specs, out_specs=out_specs, out_shape=out_shape,
        compiler_params=_cparams(("parallel",)), name="norm_in_proj")(*args)
    return outs if with_t else outs[0]


def _chunk_cumsum(x, axis):
    idx = lax.broadcasted_iota(jnp.int32, x.shape, axis) % MLSTM_CHUNK
    d = 1
    while d < MLSTM_CHUNK:
        x = x + jnp.where(idx >= d, pltpu.roll(x, d, axis=axis), 0.0)
        d *= 2
    return x


def _mlstm_body(q_ref, k_ref, v_ref, og_ref, if_ref, gt_ref, cwq_ref, cwk_ref, bcol_ref, brow_ref,
                gain_ref, out_ref, qs, ks, bcs, lics, brs, lirs):
    h = pl.program_id(1)
    s_len = q_ref.shape[1]
    n_chunks = s_len // MLSTM_CHUNK
    L = MLSTM_CHUNK
    row = lax.broadcasted_iota(jnp.int32, (s_len, HEAD_DIM), 0)

    def conv_silu(x, w):
        acc = x * w[MLSTM_CONV - 1:MLSTM_CONV, :]
        for d in range(1, MLSTM_CONV):
            shifted = jnp.where(row >= d, pltpu.roll(x, d, axis=0), 0.0)
            acc = acc + shifted * w[MLSTM_CONV - 1 - d:MLSTM_CONV - d, :]
        return acc * _sigmoid(acc)

    qs[...] = conv_silu(q_ref[0], cwq_ref[...]) * (HEAD_DIM ** -0.5)
    ks[...] = conv_silu(k_ref[0], cwk_ref[...])

    pre = if_ref[0] + bcol_ref[...]
    pre = MLSTM_GATE_CAP * jnp.tanh(pre / MLSTM_GATE_CAP)
    lane = lax.broadcasted_iota(jnp.int32, pre.shape, 1)
    b_all = _chunk_cumsum(_log_sigmoid(pre), 0)
    lics[...] = jnp.sum(jnp.where(lane == h, pre, 0.0), axis=1, keepdims=True)
    bcs[...] = jnp.sum(jnp.where(lane == h + MLSTM_HEADS, b_all, 0.0), axis=1, keepdims=True)

    pr = gt_ref[...] + brow_ref[...]
    pr = MLSTM_GATE_CAP * jnp.tanh(pr / MLSTM_GATE_CAP)
    sub = lax.broadcasted_iota(jnp.int32, pr.shape, 0)
    b_rows = _chunk_cumsum(_log_sigmoid(pr), 1)
    li_row = jnp.sum(jnp.where(sub == h, pr, 0.0), axis=0, keepdims=True)
    b_row = jnp.sum(jnp.where(sub == h + MLSTM_HEADS, b_rows, 0.0), axis=0, keepdims=True)
    for c in range(n_chunks):
        brs[c] = b_row[:, c * L:(c + 1) * L]
        lirs[c] = li_row[:, c * L:(c + 1) * L]

    tri = (lax.broadcasted_iota(jnp.int32, (L, L), 1) <= lax.broadcasted_iota(jnp.int32, (L, L), 0))
    gain = gain_ref[...]

    def chunk(c, carry):
        c_state, n_state, m_state = carry
        r0 = pl.multiple_of(c * L, L)
        q = qs[pl.ds(r0, L), :]
        k = ks[pl.ds(r0, L), :]
        v = v_ref[0, pl.ds(r0, L), :]
        b_col = bcs[pl.ds(r0, L), :]
        li_col = lics[pl.ds(r0, L), :]
        b_r = brs[c]
        li_r = lirs[c]
        qb = q.astype(BF16)
        kb = k.astype(BF16)
        b_last = b_col[L - 1:L, :]
        g = b_last - b_col + li_col
        g_max = jnp.max(g, axis=0, keepdims=True)
        w = jnp.exp(g - g_max)
        kv = lax.dot_general((w * v).astype(BF16), kb, _TN, preferred_element_type=F32)
        ksum = jnp.sum(w * k, axis=0, keepdims=True)
        dmat = jnp.where(tri, b_col - b_r + li_r, NEG_INF)
        inter = b_col + m_state
        m_t = jnp.maximum(inter, jnp.max(dmat, axis=1, keepdims=True))
        s = lax.dot_general(qb, kb, _NT, preferred_element_type=F32) * jnp.exp(dmat - m_t)
        w_inter = jnp.exp(inter - m_t)
        num = (jnp.dot(s.astype(BF16), v.astype(BF16), preferred_element_type=F32)
               + w_inter * lax.dot_general(qb, c_state.astype(BF16), _NT, preferred_element_type=F32))
        den = jnp.sum(s, axis=1, keepdims=True) + w_inter * jnp.sum(q * n_state, axis=1, keepdims=True)
        ht = num / jnp.maximum(jnp.abs(den), jnp.exp(-m_t))
        ht = ht * lax.rsqrt(jnp.mean(ht * ht, axis=1, keepdims=True) + NORM_EPS) * gain
        out_ref[0, pl.ds(r0, L), :] = ht * _sigmoid(og_ref[0, pl.ds(r0, L), :])
        m_new = jnp.maximum(b_last + m_state, g_max)
        a = jnp.exp(b_last + m_state - m_new)
        cc = jnp.exp(g_max - m_new)
        return a * c_state + cc * kv, a * n_state + cc * ksum, m_new

    init = (jnp.zeros((HEAD_DIM, HEAD_DIM), F32), jnp.zeros((1, HEAD_DIM), F32), jnp.zeros((1, 1), F32))
    lax.fori_loop(0, n_chunks, chunk, init, unroll=8)


def _mlstm(z3, gt, conv_w, gate_b, head_gain):
    b, s_len, _ = z3.shape
    n_chunks = s_len // MLSTM_CHUNK
    gr = gt.shape[0]
    bcol = jnp.zeros((1, LANES), F32).at[0, :2 * MLSTM_HEADS].set(gate_b)
    brow = jnp.zeros((gr, 1), F32).at[:2 * MLSTM_HEADS, 0].set(gate_b)

    def col(off):
        return pl.BlockSpec((1, s_len, HEAD_DIM), lambda bi, hi: (bi, 0, off + hi))

    in_specs = [col(0), col(MLSTM_HEADS), col(2 * MLSTM_HEADS), col(3 * MLSTM_HEADS),
                pl.BlockSpec((1, s_len, LANES), lambda bi, hi: (bi, 0, EVEN_IF_BLK)),
                pl.BlockSpec((gr, s_len), lambda bi, hi: (0, bi)),
                pl.BlockSpec((MLSTM_CONV, HEAD_DIM), lambda bi, hi: (0, hi)),
                pl.BlockSpec((MLSTM_CONV, HEAD_DIM), lambda bi, hi: (0, MLSTM_HEADS + hi)),
                pl.BlockSpec((1, LANES), lambda bi, hi: (0, 0)),
                pl.BlockSpec((gr, 1), lambda bi, hi: (0, 0)),
                pl.BlockSpec((1, HEAD_DIM), lambda bi, hi: (0, hi))]
    scratch = [pltpu.VMEM((s_len, HEAD_DIM), F32), pltpu.VMEM((s_len, HEAD_DIM), F32),
               pltpu.VMEM((s_len, 1), F32), pltpu.VMEM((s_len, 1), F32),
               pltpu.VMEM((n_chunks, 1, MLSTM_CHUNK), F32), pltpu.VMEM((n_chunks, 1, MLSTM_CHUNK), F32)]
    return pl.pallas_call(
        _mlstm_body, grid=(b, MLSTM_HEADS), in_specs=in_specs,
        out_specs=pl.BlockSpec((1, s_len, HEAD_DIM), lambda bi, hi: (bi, 0, hi)),
        out_shape=jax.ShapeDtypeStruct((b, s_len, MLSTM_W), F32),
        scratch_shapes=scratch, compiler_params=_cparams(("parallel", "parallel")),
        name="mlstm")(z3, z3, z3, z3, z3, gt, conv_w, conv_w, bcol, brow, head_gain.reshape(1, MLSTM_W))


def _moba_body(q_ref, k_ref, v_ref, cos_ref, sin_ref, o_ref, kr_scr, vt_scr, km_scr, sel_scr):
    i = pl.program_id(1)
    s_len = k_ref.shape[1]
    bs = MOBA_BLOCK
    nb = s_len // bs
    heads = [slice(h * HEAD_DIM, (h + 1) * HEAD_DIM) for h in range(MOBA_HEADS)]

    @pl.when(i == 0)
    def _():
        for h, cols in enumerate(heads):
            _store_transposed(vt_scr.at[h], v_ref[0, :, cols])
            kr = _rotate(k_ref[0, :, cols], cos_ref[...], sin_ref[...])
            kr_scr[:, cols] = kr.astype(BF16)
            rows = [jnp.sum(kr[n * bs:(n + 1) * bs, :], axis=0, keepdims=True) / float(bs) for n in range(nb)]
            rows.append(jnp.zeros((km_scr.shape[1] - nb, HEAD_DIM), F32))
            km_scr[h] = jnp.concatenate(rows, axis=0)

    t0 = pl.multiple_of(i * bs, bs)
    cos_q = cos_ref[pl.ds(t0, bs), :]
    sin_q = sin_ref[pl.ds(t0, bs), :]
    blk = lax.broadcasted_iota(jnp.int32, (km_scr.shape[1], bs), 0)
    causal_bias = jnp.where(lax.broadcasted_iota(jnp.int32, (bs, bs), 0)
                            <= lax.broadcasted_iota(jnp.int32, (bs, bs), 1), 0.0, NEG_INF)

    qbs = []
    for h, cols in enumerate(heads):
        qr = _rotate(q_ref[0, :, cols], cos_q, sin_q)
        gate_t = lax.dot_general(km_scr[h], qr, _NT, precision=HIGHEST, preferred_element_type=F32)
        val = jnp.where(blk < i, gate_t, NEG_INF)
        picked = jnp.where(val > 0.5 * NEG_INF, _rank_before(val, nb, 0), float(nb)) < MOBA_TOPK
        sel_scr[h] = jnp.where(picked, 0.0, NEG_INF)
        qbs.append((qr * (HEAD_DIM ** -0.5 * LOG2_E)).astype(BF16))

    def key_tiles(k0, n_blocks):
        return ([kr_scr[pl.ds(k0, n_blocks * bs), cols] for cols in heads],
                [vt_scr[h, :, pl.ds(k0, n_blocks * bs)] for h in range(MOBA_HEADS)])

    init = tuple(_flash_steps(qbs, *key_tiles(t0, 1), [causal_bias] * MOBA_HEADS, None))

    def past_blocks(j, n_blocks, carry):
        biases = [jnp.concatenate([jnp.broadcast_to(sel_scr[h, pl.ds(j + d, 1), :], (bs, bs))
                                   for d in range(n_blocks)], axis=0) for h in range(MOBA_HEADS)]
        k0 = j * bs if isinstance(j, int) else pl.multiple_of(j * bs, bs)
        return tuple(_flash_steps(qbs, *key_tiles(k0, n_blocks), biases, carry))

    odd = i % 2
    carry = lax.cond(odd == 1, lambda c: past_blocks(0, 1, c), lambda c: c, init)
    fin = lax.fori_loop(0, i // 2, lambda p, c: past_blocks(odd + 2 * p, 2, c), carry)
    for h, cols in enumerate(heads):
        o_ref[0, :, cols] = _flash_output(fin[2 * h + 1]).T


def _moba(z3, cos2, sin2):
    b, s_len, _ = z3.shape
    nb = s_len // MOBA_BLOCK
    q_off = 4 * MLSTM_W // MOBA_W

    def kv(off):
        return pl.BlockSpec((1, s_len, MOBA_W), lambda bi, i: (bi, 0, off))

    in_specs = [pl.BlockSpec((1, MOBA_BLOCK, MOBA_W), lambda bi, i: (bi, i, q_off)),
                kv(q_off + 1), kv(q_off + 2),
                pl.BlockSpec((s_len, HEAD_DIM), lambda bi, i: (0, 0)),
                pl.BlockSpec((s_len, HEAD_DIM), lambda bi, i: (0, 0))]
    scratch = [pltpu.VMEM((s_len, MOBA_W), BF16),
               pltpu.VMEM((MOBA_HEADS, VT_ROWS, s_len), BF16),
               pltpu.VMEM((MOBA_HEADS, BF16_SUBLANES, HEAD_DIM), F32),
               pltpu.VMEM((MOBA_HEADS, BF16_SUBLANES, MOBA_BLOCK), F32)]
    return pl.pallas_call(
        _moba_body, grid=(b, nb), in_specs=in_specs,
        out_specs=pl.BlockSpec((1, MOBA_BLOCK, MOBA_W), lambda bi, i: (bi, i, 0)),
        out_shape=jax.ShapeDtypeStruct((b, s_len, MOBA_W), F32),
        scratch_shapes=scratch, compiler_params=_cparams(("parallel", "arbitrary")),
        name="moba")(z3, z3, z3, cos2, sin2)


def _gelu_tanh(x):
    return x * (0.5 * (1.0 + jnp.tanh(np.sqrt(2.0 / np.pi) * (x + 0.044715 * (x * x * x)))))


def _nsa_compress_body(kc_ref, vc_ref, pe_ref, wk1_ref, wk2_ref, wv1_ref, wv2_ref, ko_ref, vo_ref):
    n_rows = kc_ref.shape[1] // CMP_STRIDE
    halves = CMP_LEN // CMP_STRIDE
    assert halves == 2

    def compress(x_ref, pe, w1_ref, w2_ref):
        ya = jnp.zeros((n_rows, CMP_HIDDEN), F32)
        yb = jnp.zeros((n_rows, CMP_HIDDEN), F32)
        for l in range(CMP_STRIDE):
            r = x_ref[0, pl.ds(l, n_rows, stride=CMP_STRIDE), :]
            la, lb = l, CMP_STRIDE + l
            ya = ya + jnp.dot((r + pe[la:la + 1, :]).astype(BF16), w1_ref[la * HEAD_DIM:(la + 1) * HEAD_DIM, :],
                              preferred_element_type=F32)
            yb = yb + jnp.dot((r + pe[lb:lb + 1, :]).astype(BF16), w1_ref[lb * HEAD_DIM:(lb + 1) * HEAD_DIM, :],
                              preferred_element_type=F32)
        pre = ya + pltpu.roll(yb, n_rows - 1, axis=0)
        return jnp.dot(_gelu_tanh(pre).astype(BF16), w2_ref[...], preferred_element_type=F32)

    ko_ref[0, 0] = compress(kc_ref, pe_ref[0], wk1_ref, wk2_ref)
    vo_ref[0, 0] = compress(vc_ref, pe_ref[1], wv1_ref, wv2_ref)


def _nsa_compress(z3, cmp_pos, wk1, wk2, wv1, wv2):
    b, s_len, _ = z3.shape
    n_rows = s_len // CMP_STRIDE
    kc_off = NSA_W // LANES
    vc_off = kc_off + NSA_GROUPS
    const2 = lambda bi, gi: (0, 0)
    in_specs = [pl.BlockSpec((1, s_len, HEAD_DIM), lambda bi, gi: (bi, 0, kc_off + gi)),
                pl.BlockSpec((1, s_len, HEAD_DIM), lambda bi, gi: (bi, 0, vc_off + gi)),
                pl.BlockSpec(cmp_pos.shape, lambda bi, gi: (0, 0, 0)),
                pl.BlockSpec(wk1.shape, const2), pl.BlockSpec(wk2.shape, const2),
                pl.BlockSpec(wv1.shape, const2), pl.BlockSpec(wv2.shape, const2)]
    out_spec = pl.BlockSpec((1, 1, n_rows, HEAD_DIM), lambda bi, gi: (bi, gi, 0, 0))
    out_sds = jax.ShapeDtypeStruct((b, NSA_GROUPS, n_rows, HEAD_DIM), F32)
    return pl.pallas_call(
        _nsa_compress_body, grid=(b, NSA_GROUPS), in_specs=in_specs,
        out_specs=[out_spec, out_spec], out_shape=[out_sds, out_sds],
        compiler_params=_cparams(("parallel", "parallel")), name="nsa_compress")(
            z3, z3, cmp_pos, wk1, wk2, wv1, wv2)


def _nsa_body(q_ref, ks_ref, vs_ref, kw_ref, vw_ref, kc_ref, vc_ref, gate_ref, cos_ref, sin_ref, ov_ref,
              o_ref, krs, vst, krw, vwt, sel_scr):
    i = pl.program_id(2)
    tq = NSA_TQ
    tk = NSA_TQ
    scale = HEAD_DIM ** -0.5
    n_sel_blk = ks_ref.shape[1] // SEL_BLOCK
    n_cmp = (ks_ref.shape[1] - CMP_LEN) // CMP_STRIDE + 1
    shift = SEL_BLOCK.bit_length() - 1
    assert 1 << shift == SEL_BLOCK and n_sel_blk <= LANES
    blk_per_tile = tk // SEL_BLOCK

    @pl.when(i == 0)
    def _():
        krs[...] = _rotate(ks_ref[0], cos_ref[...], sin_ref[...]).astype(BF16)
        krw[...] = _rotate(kw_ref[0], cos_ref[...], sin_ref[...]).astype(BF16)
        _store_transposed(vst, vs_ref[0])
        _store_transposed(vwt, vw_ref[0])

    t0 = pl.multiple_of(i * tq, tq)
    pos_row = t0 + lax.broadcasted_iota(jnp.int32, (1, tq), 1)
    cos_q = cos_ref[pl.ds(t0, tq), :]
    sin_q = sin_ref[pl.ds(t0, tq), :]

    kc = kc_ref[0, 0].astype(BF16)
    vc = vc_ref[0, 0].astype(BF16)
    n_col = lax.broadcasted_iota(jnp.int32, (LANES, 1), 0)
    cmp_end = jnp.where(n_col < n_cmp, n_col * CMP_STRIDE + (CMP_LEN - 1), jnp.iinfo(jnp.int32).max)
    cmp_ok = cmp_end <= pos_row
    qr_heads, o_cmp = [], []
    p_sum = jnp.zeros((LANES, tq), F32)
    for r in range(NSA_REP):
        q = q_ref[0, :, r * HEAD_DIM:(r + 1) * HEAD_DIM] * scale
        qr_heads.append(_rotate(q * LOG2_E, cos_q, sin_q).astype(BF16))
        s = lax.dot_general(kc, q.astype(BF16), _NT, preferred_element_type=F32)
        s = jnp.where(cmp_ok, s, NEG_INF)
        e = jnp.where(cmp_ok, jnp.exp(s - jnp.max(s, axis=0, keepdims=True)), 0.0)
        p = e / jnp.maximum(jnp.sum(e, axis=0, keepdims=True), 1e-30)
        o_cmp.append(lax.dot_general(p.astype(BF16), vc, _TN, preferred_element_type=F32))
        p_sum = p_sum + p
    imp = jnp.dot(ov_ref[...], p_sum, precision=HIGHEST, preferred_element_type=F32)

    behind = (pos_row >> shift) - lax.broadcasted_iota(jnp.int32, imp.shape, 0)
    val = jnp.where(behind == 0, FORCED_SCORE, jnp.where(behind == 1, FORCED_SCORE, imp))
    val = jnp.where(lax.broadcasted_iota(jnp.int32, imp.shape, 0) == 0, FORCED_SCORE, val)
    val = jnp.where(behind >= 0, val, NEG_INF)
    picked = jnp.where(val > 0.5 * NEG_INF, _rank_before(val, n_sel_blk, 0), float(n_sel_blk)) < SEL_TOPN
    sel_scr[...] = jnp.where(picked, 0.0, NEG_INF)

    n = NSA_REP

    def sel_bias(k0, n_tiles):
        blk0 = (k0 >> shift) if isinstance(k0, int) else pl.multiple_of(k0 >> shift, blk_per_tile)
        return jnp.concatenate([jnp.broadcast_to(sel_scr[pl.ds(blk0 + b, 1), :], (SEL_BLOCK, tq))
                                for b in range(n_tiles * blk_per_tile)], axis=0)

    def sel_tiles(k0, n_tiles, carry):
        return tuple(_flash_steps(qr_heads, [krs[pl.ds(k0, n_tiles * tk), :]] * n,
                                  [vst[:, pl.ds(k0, n_tiles * tk)]] * n, [sel_bias(k0, n_tiles)] * n, carry))

    causal_bias = jnp.where(lax.broadcasted_iota(jnp.int32, (tk, tq), 0)
                            <= lax.broadcasted_iota(jnp.int32, (tk, tq), 1), 0.0, NEG_INF)
    span = WINDOW + tq
    start = pl.multiple_of(jnp.maximum(t0 - WINDOW, 0), tq)
    gap = ((t0 - start) + lax.broadcasted_iota(jnp.int32, (span, tq), 1)
           - lax.broadcasted_iota(jnp.int32, (span, tq), 0))
    win_bias = jnp.where(gap >= 0, jnp.where(gap < WINDOW, 0.0, NEG_INF), NEG_INF)
    group = _flash_steps(qr_heads * 2,
                         [krs[pl.ds(t0, tk), :]] * n + [krw[pl.ds(start, span), :]] * n,
                         [vst[:, pl.ds(t0, tk)]] * n + [vwt[:, pl.ds(start, span)]] * n,
                         [sel_bias(t0, 1) + causal_bias] * n + [win_bias] * n, None)
    win_fin = group[2 * n:]

    odd = i % 2
    carry = lax.cond(odd == 1, lambda c: sel_tiles(0, 1, c), lambda c: c, tuple(group[:2 * n]))
    sel_fin = lax.fori_loop(0, i // 2, lambda p, c: sel_tiles(pl.multiple_of((odd + 2 * p) * tk, tk), 2, c), carry)

    gates = _sigmoid(gate_ref[0])
    gates_t = gates.T
    for r in range(NSA_REP):
        o_t = (gates_t[3 * r + 1:3 * r + 2, :] * _flash_output(sel_fin[2 * r + 1])
               + gates_t[3 * r + 2:3 * r + 3, :] * _flash_output(win_fin[2 * r + 1]))
        o_ref[0, :, r * HEAD_DIM:(r + 1) * HEAD_DIM] = gates[:, 3 * r:3 * r + 1] * o_cmp[r] + o_t.T


def _nsa_attention(z3, k_cmp, v_cmp, cos2, sin2):
    b, s_len, _ = z3.shape
    assert s_len >= WINDOW + NSA_TQ and s_len % NSA_TQ == 0
    n_cmp = (s_len - CMP_LEN) // CMP_STRIDE + 1
    n_blk = s_len // SEL_BLOCK
    n_rows = k_cmp.shape[2]
    assert n_rows == LANES
    cmp_start = np.arange(n_rows) * CMP_STRIDE
    cmp_end = cmp_start + CMP_LEN - 1
    blk_lo = np.arange(LANES) * SEL_BLOCK
    overlap = ((cmp_start[:, None] <= blk_lo[None, :] + SEL_BLOCK - 1) & (cmp_end[:, None] >= blk_lo[None, :])
               & (np.arange(n_rows)[:, None] < n_cmp)).astype(np.float32)
    overlap_t = np.ascontiguousarray(overlap.T[:n_blk])
    base = NSA_W // LANES

    def kv(off):
        return pl.BlockSpec((1, s_len, HEAD_DIM), lambda bi, gi, i: (bi, 0, base + off * NSA_GROUPS + gi))

    cmp_spec = pl.BlockSpec((1, 1, n_rows, HEAD_DIM), lambda bi, gi, i: (bi, gi, 0, 0))
    const2 = lambda bi, gi, i: (0, 0)
    in_specs = [pl.BlockSpec((1, NSA_TQ, NSA_REP * HEAD_DIM), lambda bi, gi, i: (bi, i, gi)),
                kv(2), kv(3), kv(4), kv(5), cmp_spec, cmp_spec,
                pl.BlockSpec((1, NSA_TQ, LANES), lambda bi, gi, i: (bi, i, ODD_GATE_BLK + gi)),
                pl.BlockSpec((s_len, HEAD_DIM), const2), pl.BlockSpec((s_len, HEAD_DIM), const2),
                pl.BlockSpec((n_blk, n_rows), const2)]
    scratch = [pltpu.VMEM((s_len, HEAD_DIM), BF16), pltpu.VMEM((VT_ROWS, s_len), BF16),
               pltpu.VMEM((s_len, HEAD_DIM), BF16), pltpu.VMEM((VT_ROWS, s_len), BF16),
               pltpu.VMEM((n_blk, NSA_TQ), F32)]
    return pl.pallas_call(
        _nsa_body, grid=(b, NSA_GROUPS, s_len // NSA_TQ), in_specs=in_specs,
        out_specs=pl.BlockSpec((1, NSA_TQ, NSA_REP * HEAD_DIM), lambda bi, gi, i: (bi, i, gi)),
        out_shape=jax.ShapeDtypeStruct((b, s_len, NSA_W), F32),
        scratch_shapes=scratch, compiler_params=_cparams(("parallel", "parallel", "arbitrary")),
        name="nsa_attention")(z3, z3, z3, z3, z3, k_cmp, v_cmp, z3, cos2, sin2, jnp.asarray(overlap_t))


def _split_bf16(x):
    hi = x.astype(BF16)
    return hi, (x - hi.astype(F32)).astype(BF16)


def _proj_moe_body(*refs, n_act, final_norm):
    res_ref = refs[0]
    a_refs = refs[1:1 + n_act]
    wo_refs = refs[1 + n_act:1 + 2 * n_act]
    (g_ref, wrh_ref, wrl_ref, br_ref, w1_ref, w3_ref, w2_ref, fg_ref, o_ref,
     xn_scr, cw_scr, hd_scr) = refs[1 + 2 * n_act:]
    j = pl.program_id(1)
    tm = res_ref.shape[0]
    lane = lax.broadcasted_iota(jnp.int32, (tm, LANES), 1).astype(F32)
    neg = float("-inf")

    @pl.when(j == 0)
    def _():
        h = res_ref[...]
        for a_ref, wo_ref in zip(a_refs, wo_refs):
            h = h + jnp.dot(a_ref[...].astype(BF16), wo_ref[...], preferred_element_type=F32)
        o_ref[...] = h
        xn = _rms(h, g_ref[...])
        xh, xl = _split_bf16(xn)
        xn_scr[...] = xh
        logits = (jnp.dot(xh, wrh_ref[...], preferred_element_type=F32)
                  + (jnp.dot(xl, wrh_ref[...], preferred_element_type=F32)
                     + jnp.dot(xh, wrl_ref[...], preferred_element_type=F32))) + br_ref[...]
        is_g = lane < MOE_GROUPS
        gl = jnp.where(is_g, logits, neg)
        g_max = jnp.max(gl, axis=1, keepdims=True)
        g_w = 1.0 / jnp.sum(jnp.where(is_g, jnp.exp(gl - g_max), 0.0), axis=1, keepdims=True)
        g_top = jnp.min(jnp.where(gl == g_max, lane, float(LANES)), axis=1, keepdims=True)
        lo = MOE_GROUPS + MOE_EPG * g_top
        el = jnp.where(lane >= lo, jnp.where(lane < lo + MOE_EPG, logits, neg), neg)
        v1 = jnp.max(el, axis=1, keepdims=True)
        i1 = jnp.min(jnp.where(el == v1, lane, float(LANES)), axis=1, keepdims=True)
        el2 = jnp.where(lane == i1, neg, el)
        v2 = jnp.max(el2, axis=1, keepdims=True)
        i2 = jnp.min(jnp.where(el2 == v2, lane, float(LANES)), axis=1, keepdims=True)
        e2 = jnp.exp(v2 - v1)
        den = 1.0 + e2
        cw_scr[...] = (jnp.where(lane == i1, g_w / den, 0.0) + jnp.where(lane == i2, g_w * e2 / den, 0.0))

    xn = xn_scr[...]
    cw = cw_scr[...]
    for r in range(MOE_EPG):
        h1 = jnp.dot(xn, w1_ref[r], preferred_element_type=F32)
        h3 = jnp.dot(xn, w3_ref[r], preferred_element_type=F32)
        e_lane = (MOE_GROUPS + MOE_EPG * j + r).astype(F32)
        col = jnp.sum(jnp.where(lane == e_lane, cw, 0.0), axis=1, keepdims=True)
        hd_scr[:, r * MOE_HIDDEN:(r + 1) * MOE_HIDDEN] = ((h1 * _sigmoid(h1)) * h3 * col).astype(BF16)
    o_ref[...] += jnp.dot(hd_scr[...], w2_ref[...], preferred_element_type=F32)

    if final_norm:
        @pl.when(j == pl.num_programs(1) - 1)
        def _():
            o_ref[...] = _rms(o_ref[...], fg_ref[...])


def _proj_moe(res2d, acts, w_outs, gain, w_g, b_g, w_e, b_e, w1, w3, w2, final_gain, final_norm, tm=1024):
    t, d = res2d.shape
    n_act = len(acts)
    wr = jnp.zeros((d, LANES), F32).at[:, :MOE_GROUPS].set(w_g).at[:, MOE_GROUPS:MOE_GROUPS + MOE_EXPERTS].set(w_e)
    wr_hi, wr_lo = _split_bf16(wr)
    br = jnp.zeros((1, LANES), F32).at[0, :MOE_GROUPS].set(b_g).at[0, MOE_GROUPS:MOE_GROUPS + MOE_EXPERTS].set(b_e)
    tile = lambda i, j: (i, 0)
    const = lambda i, j: (0, 0)
    group_hidden = MOE_EPG * MOE_HIDDEN
    in_specs = [pl.BlockSpec((tm, d), tile)]
    in_specs += [pl.BlockSpec((tm, a.shape[1]), tile) for a in acts]
    in_specs += [pl.BlockSpec(w.shape, const) for w in w_outs]
    in_specs += [pl.BlockSpec((1, d), const),
                 pl.BlockSpec((d, LANES), const), pl.BlockSpec((d, LANES), const),
                 pl.BlockSpec((1, LANES), const),
                 pl.BlockSpec((MOE_EPG, d, MOE_HIDDEN), lambda i, j: (j, 0, 0)),
                 pl.BlockSpec((MOE_EPG, d, MOE_HIDDEN), lambda i, j: (j, 0, 0)),
                 pl.BlockSpec((group_hidden, d), lambda i, j: (j, 0)),
                 pl.BlockSpec((1, d), const)]
    scratch = [pltpu.VMEM((tm, d), BF16), pltpu.VMEM((tm, LANES), F32), pltpu.VMEM((tm, group_hidden), BF16)]
    return pl.pallas_call(
        functools.partial(_proj_moe_body, n_act=n_act, final_norm=final_norm),
        grid=(t // tm, MOE_GROUPS), in_specs=in_specs,
        out_specs=pl.BlockSpec((tm, d), tile),
        out_shape=jax.ShapeDtypeStruct((t, d), F32),
        scratch_shapes=scratch, compiler_params=_cparams(("parallel", "arbitrary")),
        name="out_proj_moe")(res2d, *acts, *w_outs, gain.reshape(1, d), wr_hi, wr_lo, br,
                             w1.astype(BF16), w3.astype(BF16), w2.astype(BF16).reshape(MOE_EXPERTS * MOE_HIDDEN, d),
                             final_gain.reshape(1, d))


def _rope_tables(s_len):
    half = HEAD_DIM // 2
    inv = ROPE_THETA ** (-(jnp.arange(half, dtype=F32) / half))
    ang = jnp.arange(s_len, dtype=F32)[:, None] * inv[None, :]
    cos, sin = jnp.cos(ang), jnp.sin(ang)
    return jnp.concatenate([cos, cos], axis=-1), jnp.concatenate([-sin, sin], axis=-1)


def _even_weights(w_in):
    n_if = 2 * MLSTM_HEADS
    a = 4 * MLSTM_W
    w_main = jnp.concatenate([w_in[:, :a], w_in[:, a + n_if:],
                              w_in[:, a:a + n_if], jnp.zeros((w_in.shape[0], LANES - n_if), w_in.dtype)], axis=1)
    w_if_t = jnp.concatenate([w_in[:, a:a + n_if].T, jnp.zeros((16 - n_if, w_in.shape[0]), w_in.dtype)], axis=0)
    return w_main.astype(BF16), w_if_t.astype(BF16)


def _odd_weights(w_in):
    a = NSA_W + 6 * NSA_KV_W
    per_group = NSA_REP * 3
    pad = jnp.zeros((w_in.shape[0], LANES - per_group), w_in.dtype)
    cols = [w_in[:, :a]]
    for g in range(NSA_GROUPS):
        cols += [w_in[:, a + g * per_group:a + (g + 1) * per_group], pad]
    return jnp.concatenate(cols, axis=1).astype(BF16)


def kernel(x, mix_norm_0, w_in_0, mlstm_conv_0, mlstm_gate_b_0, mlstm_head_norm_0, w_out_0, ffn_norm_0, router_group_0, router_group_b_0, router_expert_0, router_expert_b_0, moe_w1_0, moe_w3_0, moe_w2_0, mix_norm_1, w_in_1, nsa_cmp_pos_1, nsa_cmp_k1_1, nsa_cmp_k2_1, nsa_cmp_v1_1, nsa_cmp_v2_1, w_out_1, ffn_norm_1, router_group_1, router_group_b_1, router_expert_1, router_expert_b_1, moe_w1_1, moe_w3_1, moe_w2_1, final_norm):
    b, s_len, d = x.shape
    t = b * s_len
    cos2, sin2 = _rope_tables(s_len)
    x2d = x.reshape(t, d)

    w_main, w_if_t = _even_weights(w_in_0)
    z0, gt = _norm_matmul(x2d, mix_norm_0, w_main, w_if_t)
    z0 = z0.reshape(b, s_len, EVEN_N)
    h_m = _mlstm(z0, gt, mlstm_conv_0, mlstm_gate_b_0, mlstm_head_norm_0)
    o_b = _moba(z0, cos2, sin2)
    w_out = w_out_0.astype(BF16)
    h = _proj_moe(x2d, [h_m.reshape(t, MLSTM_W), o_b.reshape(t, MOBA_W)], [w_out[:MLSTM_W], w_out[MLSTM_W:]],
                  ffn_norm_0, router_group_0, router_group_b_0, router_expert_0, router_expert_b_0,
                  moe_w1_0, moe_w3_0, moe_w2_0, final_norm, False)

    z1 = _norm_matmul(h, mix_norm_1, _odd_weights(w_in_1)).reshape(b, s_len, ODD_N)
    k_cmp, v_cmp = _nsa_compress(z1, nsa_cmp_pos_1, nsa_cmp_k1_1.astype(BF16), nsa_cmp_k2_1.astype(BF16),
                                 nsa_cmp_v1_1.astype(BF16), nsa_cmp_v2_1.astype(BF16))
    o = _nsa_attention(z1, k_cmp, v_cmp, cos2, sin2)
    h = _proj_moe(h, [o.reshape(t, NSA_W)], [w_out_1.astype(BF16)],
                  ffn_norm_1, router_group_1, router_group_b_1, router_expert_1, router_expert_b_1,
                  moe_w1_1, moe_w3_1, moe_w2_1, final_norm, True)
    return h.reshape(b, s_len, d)
```

```python
import functools

import numpy as np
import jax
import jax.numpy as jnp
from jax import lax
from jax.experimental import pallas as pl
from jax.experimental.pallas import tpu as pltpu

F32 = jnp.float32
BF16 = jnp.bfloat16
HIGHEST = lax.Precision.HIGHEST

LANES = 128
BF16_SUBLANES = 16
D_MODEL = 1024
HEAD_DIM = 128
ROPE_THETA = 10000.0
NORM_EPS = 1e-6
NEG_INF = -1e30
FORCED_SCORE = 1e4

MLSTM_HEADS = 4
MLSTM_W = MLSTM_HEADS * HEAD_DIM
MLSTM_CHUNK = 64
MLSTM_CONV = 4
MLSTM_GATE_CAP = 15.0
MOBA_HEADS = 4
MOBA_W = MOBA_HEADS * HEAD_DIM
MOBA_BLOCK = 256
MOBA_TOPK = 3

NSA_HEADS = 8
NSA_GROUPS = 2
NSA_REP = NSA_HEADS // NSA_GROUPS
NSA_W = NSA_HEADS * HEAD_DIM
NSA_KV_W = NSA_GROUPS * HEAD_DIM
CMP_LEN = 32
CMP_STRIDE = 16
CMP_HIDDEN = 256
SEL_BLOCK = 64
SEL_TOPN = 8
WINDOW = 512
NSA_TQ = 256

MOE_GROUPS = 4
MOE_EPG = 4
MOE_EXPERTS = MOE_GROUPS * MOE_EPG
MOE_HIDDEN = D_MODEL // 4

EVEN_N = 4 * MLSTM_W + 3 * MOBA_W + LANES
EVEN_IF_BLK = (4 * MLSTM_W + 3 * MOBA_W) // LANES
ODD_GATE_BLK = (NSA_W + 6 * NSA_KV_W) // LANES
ODD_N = NSA_W + 6 * NSA_KV_W + NSA_GROUPS * LANES

VMEM_LIMIT = 56 * 1024 * 1024
VT_ROWS = HEAD_DIM + BF16_SUBLANES
LOG2_E = float(np.log2(np.e))

_NT = (((1,), (1,)), ((), ()))
_TN = (((0,), (0,)), ((), ()))


def _cparams(sem):
    return pltpu.CompilerParams(dimension_semantics=sem, vmem_limit_bytes=VMEM_LIMIT)


def _rms(x, g):
    return x * lax.rsqrt(jnp.mean(x * x, axis=-1, keepdims=True) + NORM_EPS) * g


def _sigmoid(x):
    return 0.5 * jnp.tanh(0.5 * x) + 0.5


def _log_sigmoid(x):
    return -(jnp.maximum(-x, 0.0) + jnp.log1p(jnp.exp(-jnp.abs(x))))


def _rotate(x, cos2, sin2):
    return x * cos2 + pltpu.roll(x, HEAD_DIM // 2, axis=1) * sin2


def _flash_steps(qs, k_ts, vt_ts, biases, carries):
    n = len(qs)
    ss = [lax.dot_general(k_ts[u], qs[u], _NT, preferred_element_type=F32) + biases[u] for u in range(n)]
    ms, es, alphas = [], [], []
    for u in range(n):
        m_new = jnp.max(ss[u], axis=0, keepdims=True)
        if carries is not None:
            m_new = jnp.maximum(carries[2 * u], m_new)
            alphas.append(jnp.exp2(carries[2 * u] - m_new))
        ms.append(m_new)
        es.append(jnp.exp2(ss[u] - m_new).astype(BF16))
    out = []
    for u in range(n):
        pv = jnp.dot(vt_ts[u], es[u], preferred_element_type=F32)
        if carries is not None:
            pv = alphas[u] * carries[2 * u + 1] + pv
        out += [ms[u], pv]
    return out


def _flash_output(acc):
    return acc[:HEAD_DIM] / jnp.maximum(acc[HEAD_DIM:HEAD_DIM + 1], 1e-30)


def _store_transposed(dst_ref, x):
    n_rows, n_cols = x.shape
    for r in range(0, n_rows, LANES):
        dst_ref[:n_cols, r:r + LANES] = x[r:r + LANES, :].T.astype(dst_ref.dtype)
    dst_ref[n_cols:, :] = jnp.ones((dst_ref.shape[0] - n_cols, n_rows), dst_ref.dtype)


def _rank_before(v, n_valid, axis):
    idx = lax.broadcasted_iota(jnp.int32, v.shape, axis)
    rank = jnp.zeros(v.shape, F32)
    for m in range(n_valid):
        vm = v[m:m + 1, :] if axis == 0 else v[:, m:m + 1]
        tie = jnp.where(idx > m, 1.0, 0.0)
        rank = rank + jnp.where(vm > v, 1.0, jnp.where(vm == v, tie, 0.0))
    return rank


def _norm_matmul_body(*refs, n_chunk, with_t):
    if with_t:
        x_ref, g_ref, w_ref, wt_ref, o_ref, ot_ref = refs
    else:
        x_ref, g_ref, w_ref, o_ref = refs
    yb = _rms(x_ref[...], g_ref[...]).astype(BF16)
    n = w_ref.shape[1]
    for c0 in range(0, n, n_chunk):
        c1 = min(n, c0 + n_chunk)
        o_ref[:, c0:c1] = jnp.dot(yb, w_ref[:, c0:c1], preferred_element_type=F32)
    if with_t:
        ot_ref[...] = lax.dot_general(wt_ref[...], yb, _NT, preferred_element_type=F32)


def _norm_matmul(x2d, gain, w, wt=None, tm=512):
    t, d = x2d.shape
    n = w.shape[1]
    with_t = wt is not None
    in_specs = [pl.BlockSpec((tm, d), lambda i: (i, 0)),
                pl.BlockSpec((1, d), lambda i: (0, 0)),
                pl.BlockSpec((d, n), lambda i: (0, 0))]
    out_specs = [pl.BlockSpec((tm, n), lambda i: (i, 0))]
    out_shape = [jax.ShapeDtypeStruct((t, n), F32)]
    args = [x2d, gain.reshape(1, d), w]
    if with_t:
        r = wt.shape[0]
        in_specs.append(pl.BlockSpec((r, d), lambda i: (0, 0)))
        out_specs.append(pl.BlockSpec((r, tm), lambda i: (0, i)))
        out_shape.append(jax.ShapeDtypeStruct((r, t), F32))
        args.append(wt)
    outs = pl.pallas_call(
        functools.partial(_norm_matmul_body, n_chunk=512, with_t=with_t),
        grid=(t // tm,), in_specs=in_specs, out_specs=out_specs, out_shape=out_shape,
        compiler_params=_cparams(("parallel",)), name="norm_in_proj")(*args)
    return outs if with_t else outs[0]


def _chunk_cumsum(x, axis):
    idx = lax.broadcasted_iota(jnp.int32, x.shape, axis) % MLSTM_CHUNK
    d = 1
    while d < MLSTM_CHUNK:
        x = x + jnp.where(idx >= d, pltpu.roll(x, d, axis=axis), 0.0)
        d *= 2
    return x


def _mlstm_body(q_ref, k_ref, v_ref, og_ref, if_ref, gt_ref, cwq_ref, cwk_ref, bcol_ref, brow_ref,
                gain_ref, out_ref, ks, qts, vts, hts, css, brs, lirs, gcol):
    h = pl.program_id(1)
    s_len = q_ref.shape[1]
    n_chunks = s_len // MLSTM_CHUNK
    L = MLSTM_CHUNK
    per_slab = LANES // L
    row = lax.broadcasted_iota(jnp.int32, (s_len, HEAD_DIM), 0)

    def conv_silu(x, w):
        acc = x * w[MLSTM_CONV - 1:MLSTM_CONV, :]
        for d in range(1, MLSTM_CONV):
            shifted = jnp.where(row >= d, pltpu.roll(x, d, axis=0), 0.0)
            acc = acc + shifted * w[MLSTM_CONV - 1 - d:MLSTM_CONV - d, :]
        return acc * _sigmoid(acc)

    def store_chunks_transposed(dst, x):
        for p in range(s_len // LANES):
            slab_t = x[p * LANES:(p + 1) * LANES, :].T
            for j in range(per_slab):
                dst[per_slab * p + j] = slab_t[:, j * L:(j + 1) * L].astype(dst.dtype)

    store_chunks_transposed(qts, conv_silu(q_ref[0], cwq_ref[...]) * (HEAD_DIM ** -0.5))
    store_chunks_transposed(vts, v_ref[0])
    ks[...] = conv_silu(k_ref[0], cwk_ref[...]).astype(BF16)

    lane = lax.broadcasted_iota(jnp.int32, gcol.shape, 1)

    @pl.when(h == 0)
    def _():
        pre = if_ref[0] + bcol_ref[...]
        pre = MLSTM_GATE_CAP * jnp.tanh(pre / MLSTM_GATE_CAP)
        gcol[...] = jnp.where(lane < MLSTM_HEADS, pre, _chunk_cumsum(_log_sigmoid(pre), 0))

    g_all = gcol[...]
    css[...] = jnp.sum(jnp.where(lane == h, g_all, jnp.where(lane == h + MLSTM_HEADS, -g_all, 0.0)),
                       axis=1, keepdims=True)

    pr = gt_ref[...] + brow_ref[...]
    pr = MLSTM_GATE_CAP * jnp.tanh(pr / MLSTM_GATE_CAP)
    sub = lax.broadcasted_iota(jnp.int32, pr.shape, 0)
    b_rows = _chunk_cumsum(_log_sigmoid(pr), 1)
    li_row = jnp.sum(jnp.where(sub == h, pr, 0.0), axis=0, keepdims=True)
    b_row = jnp.sum(jnp.where(sub == h + MLSTM_HEADS, b_rows, 0.0), axis=0, keepdims=True)
    for c in range(n_chunks):
        brs[c] = b_row[:, c * L:(c + 1) * L]
        lirs[c] = li_row[:, c * L:(c + 1) * L]

    tri = lax.broadcasted_iota(jnp.int32, (L, L), 0) <= lax.broadcasted_iota(jnp.int32, (L, L), 1)
    gain = jnp.broadcast_to(gain_ref[...], (HEAD_DIM, L))

    group_size = 8
    assert n_chunks % group_size == 0

    def group(gi, carry):
        c_state, n_state, m_state = carry
        cs = [gi * group_size + j for j in range(group_size)]
        r0s = [pl.multiple_of(c * L, L) for c in cs]
        ks_ = [ks[pl.ds(r0, L), :] for r0 in r0s]
        q_ts = [qts[c] for c in cs]
        v_ts = [vts[c] for c in cs]
        b_rs = [brs[c] for c in cs]
        kvs, ksums, kqs, g_maxs, b_lasts = [], [], [], [], []
        for j, c in enumerate(cs):
            b_last = b_rs[j][:, L - 1:L]
            g = b_last - b_rs[j] + lirs[c]
            g_max = jnp.max(g, axis=1, keepdims=True)
            w = jnp.exp(g - g_max)
            kvs.append(jnp.dot((v_ts[j] * w).astype(BF16), ks_[j], preferred_element_type=F32))
            ksums.append(jnp.dot(jnp.broadcast_to(w, (BF16_SUBLANES, L)).astype(BF16), ks_[j],
                                 preferred_element_type=F32)[:1])
            kqs.append(jnp.dot(ks_[j], q_ts[j], preferred_element_type=F32))
            g_maxs.append(g_max)
            b_lasts.append(b_last)
        c_ins, n_ins, m_ins = [], [], []
        for j in range(group_size):
            c_ins.append(c_state.astype(BF16))
            n_ins.append(jnp.broadcast_to(n_state, (BF16_SUBLANES, HEAD_DIM)).astype(BF16))
            m_ins.append(m_state)
            m_new = jnp.maximum(b_lasts[j] + m_state, g_maxs[j])
            a = jnp.exp(b_lasts[j] + m_state - m_new)
            cc = jnp.exp(g_maxs[j] - m_new)
            c_state, n_state, m_state = a * c_state + cc * kvs[j], a * n_state + cc * ksums[j], m_new
        c_qs = [jnp.dot(c_ins[j], q_ts[j], preferred_element_type=F32) for j in range(group_size)]
        n_qs = [jnp.dot(n_ins[j], q_ts[j], preferred_element_type=F32)[:1] for j in range(group_size)]
        ss, m_ts, w_inters = [], [], []
        for j in range(group_size):
            dmat = jnp.where(tri, b_rs[j] + css[pl.ds(r0s[j], L), :], NEG_INF)
            inter = b_rs[j] + m_ins[j]
            m_t = jnp.maximum(inter, jnp.max(dmat, axis=0, keepdims=True))
            ss.append(kqs[j] * jnp.exp(dmat - m_t))
            m_ts.append(m_t)
            w_inters.append(jnp.exp(inter - m_t))
        svs = [jnp.dot(v_ts[j].astype(BF16), ss[j].astype(BF16), preferred_element_type=F32)
               for j in range(group_size)]
        for j, c in enumerate(cs):
            num = svs[j] + w_inters[j] * c_qs[j]
            den = jnp.sum(ss[j], axis=0, keepdims=True) + w_inters[j] * n_qs[j]
            ht = num / jnp.maximum(jnp.abs(den), jnp.exp(-m_ts[j]))
            hts[c] = ht * lax.rsqrt(jnp.mean(ht * ht, axis=0, keepdims=True) + NORM_EPS) * gain
        return c_state, n_state, m_state

    init = (jnp.zeros((HEAD_DIM, HEAD_DIM), F32), jnp.zeros((1, HEAD_DIM), F32), jnp.zeros((1, 1), F32))
    lax.fori_loop(0, n_chunks // group_size, group, init)

    for p in range(s_len // LANES):
        slab_t = jnp.concatenate([hts[per_slab * p + j] for j in range(per_slab)], axis=1)
        rows = slice(p * LANES, (p + 1) * LANES)
        out_ref[0, rows, :] = slab_t.T * _sigmoid(og_ref[0, rows, :])


def _mlstm(z3, gt, conv_w, gate_b, head_gain):
    b, s_len, _ = z3.shape
    n_chunks = s_len // MLSTM_CHUNK
    gr = gt.shape[0]
    bcol = jnp.zeros((1, LANES), F32).at[0, :2 * MLSTM_HEADS].set(gate_b)
    brow = jnp.zeros((gr, 1), F32).at[:2 * MLSTM_HEADS, 0].set(gate_b)

    def col(off):
        return pl.BlockSpec((1, s_len, HEAD_DIM), lambda bi, hi: (bi, 0, off + hi))

    in_specs = [col(0), col(MLSTM_HEADS), col(2 * MLSTM_HEADS), col(3 * MLSTM_HEADS),
                pl.BlockSpec((1, s_len, LANES), lambda bi, hi: (bi, 0, EVEN_IF_BLK)),
                pl.BlockSpec((gr, s_len), lambda bi, hi: (0, bi)),
                pl.BlockSpec((MLSTM_CONV, HEAD_DIM), lambda bi, hi: (0, hi)),
                pl.BlockSpec((MLSTM_CONV, HEAD_DIM), lambda bi, hi: (0, MLSTM_HEADS + hi)),
                pl.BlockSpec((1, LANES), lambda bi, hi: (0, 0)),
                pl.BlockSpec((gr, 1), lambda bi, hi: (0, 0)),
                pl.BlockSpec((HEAD_DIM, 1), lambda bi, hi: (hi, 0))]
    chunk_t = (n_chunks, HEAD_DIM, MLSTM_CHUNK)
    scratch = [pltpu.VMEM((s_len, HEAD_DIM), BF16),
               pltpu.VMEM(chunk_t, BF16), pltpu.VMEM(chunk_t, F32), pltpu.VMEM(chunk_t, F32),
               pltpu.VMEM((s_len, 1), F32),
               pltpu.VMEM((n_chunks, 1, MLSTM_CHUNK), F32), pltpu.VMEM((n_chunks, 1, MLSTM_CHUNK), F32),
               pltpu.VMEM((s_len, LANES), F32)]
    return pl.pallas_call(
        _mlstm_body, grid=(b, MLSTM_HEADS), in_specs=in_specs,
        out_specs=pl.BlockSpec((1, s_len, HEAD_DIM), lambda bi, hi: (bi, 0, hi)),
        out_shape=jax.ShapeDtypeStruct((b, s_len, MLSTM_W), F32),
        scratch_shapes=scratch, compiler_params=_cparams(("parallel", "arbitrary")),
        name="mlstm")(z3, z3, z3, z3, z3, gt, conv_w, conv_w, bcol, brow, head_gain.reshape(MLSTM_W, 1))


def _moba_body(q_ref, k_ref, v_ref, cos_ref, sin_ref, o_ref, kr_scr, vt_scr, km_scr, sel_scr):
    i = pl.program_id(1)
    s_len = k_ref.shape[1]
    bs = MOBA_BLOCK
    nb = s_len // bs
    heads = [slice(h * HEAD_DIM, (h + 1) * HEAD_DIM) for h in range(MOBA_HEADS)]

    @pl.when(i == 0)
    def _():
        for h, cols in enumerate(heads):
            _store_transposed(vt_scr.at[h], v_ref[0, :, cols])
            kr = _rotate(k_ref[0, :, cols], cos_ref[...], sin_ref[...])
            kr_scr[:, cols] = kr.astype(BF16)
            rows = [jnp.sum(kr[n * bs:(n + 1) * bs, :], axis=0, keepdims=True) / float(bs) for n in range(nb)]
            rows.append(jnp.zeros((km_scr.shape[1] - nb, HEAD_DIM), F32))
            km_scr[h] = jnp.concatenate(rows, axis=0)

    t0 = pl.multiple_of(i * bs, bs)
    cos_q = cos_ref[pl.ds(t0, bs), :]
    sin_q = sin_ref[pl.ds(t0, bs), :]
    blk = lax.broadcasted_iota(jnp.int32, (km_scr.shape[1], bs), 0)
    causal_bias = jnp.where(lax.broadcasted_iota(jnp.int32, (bs, bs), 0)
                            <= lax.broadcasted_iota(jnp.int32, (bs, bs), 1), 0.0, NEG_INF)

    qbs = []
    for h, cols in enumerate(heads):
        qr = _rotate(q_ref[0, :, cols], cos_q, sin_q)
        gate_t = lax.dot_general(km_scr[h], qr, _NT, precision=HIGHEST, preferred_element_type=F32)
        val = jnp.where(blk < i, gate_t, NEG_INF)
        picked = jnp.where(val > 0.5 * NEG_INF, _rank_before(val, nb, 0), float(nb)) < MOBA_TOPK
        sel_scr[h] = jnp.where(picked, 0.0, NEG_INF)
        qbs.append((qr * (HEAD_DIM ** -0.5 * LOG2_E)).astype(BF16))

    def key_tiles(k0, n_blocks):
        return ([kr_scr[pl.ds(k0, n_blocks * bs), cols] for cols in heads],
                [vt_scr[h, :, pl.ds(k0, n_blocks * bs)] for h in range(MOBA_HEADS)])

    init = tuple(_flash_steps(qbs, *key_tiles(t0, 1), [causal_bias] * MOBA_HEADS, None))

    def past_blocks(j, n_blocks, carry):
        biases = [jnp.concatenate([jnp.broadcast_to(sel_scr[h, pl.ds(j + d, 1), :], (bs, bs))
                                   for d in range(n_blocks)], axis=0) for h in range(MOBA_HEADS)]
        k0 = j * bs if isinstance(j, int) else pl.multiple_of(j * bs, bs)
        return tuple(_flash_steps(qbs, *key_tiles(k0, n_blocks), biases, carry))

    odd = i % 2
    carry = lax.cond(odd == 1, lambda c: past_blocks(0, 1, c), lambda c: c, init)
    fin = lax.fori_loop(0, i // 2, lambda p, c: past_blocks(odd + 2 * p, 2, c), carry)
    for h, cols in enumerate(heads):
        o_ref[0, :, cols] = _flash_output(fin[2 * h + 1]).T


def _moba(z3, cos2, sin2):
    b, s_len, _ = z3.shape
    nb = s_len // MOBA_BLOCK
    q_off = 4 * MLSTM_W // MOBA_W

    def kv(off):
        return pl.BlockSpec((1, s_len, MOBA_W), lambda bi, i: (bi, 0, off))

    in_specs = [pl.BlockSpec((1, MOBA_BLOCK, MOBA_W), lambda bi, i: (bi, i, q_off)),
                kv(q_off + 1), kv(q_off + 2),
                pl.BlockSpec((s_len, HEAD_DIM), lambda bi, i: (0, 0)),
                pl.BlockSpec((s_len, HEAD_DIM), lambda bi, i: (0, 0))]
    scratch = [pltpu.VMEM((s_len, MOBA_W), BF16),
               pltpu.VMEM((MOBA_HEADS, VT_ROWS, s_len), BF16),
               pltpu.VMEM((MOBA_HEADS, BF16_SUBLANES, HEAD_DIM), F32),
               pltpu.VMEM((MOBA_HEADS, BF16_SUBLANES, MOBA_BLOCK), F32)]
    return pl.pallas_call(
        _moba_body, grid=(b, nb), in_specs=in_specs,
        out_specs=pl.BlockSpec((1, MOBA_BLOCK, MOBA_W), lambda bi, i: (bi, i, 0)),
        out_shape=jax.ShapeDtypeStruct((b, s_len, MOBA_W), F32),
        scratch_shapes=scratch, compiler_params=_cparams(("parallel", "arbitrary")),
        name="moba")(z3, z3, z3, cos2, sin2)


def _gelu_tanh(x):
    return x * (0.5 * (1.0 + jnp.tanh(np.sqrt(2.0 / np.pi) * (x + 0.044715 * (x * x * x)))))


def _nsa_compress_body(kc_ref, vc_ref, pe_ref, wk1_ref, wk2_ref, wv1_ref, wv2_ref, ko_ref, vo_ref):
    n_rows = kc_ref.shape[1] // CMP_STRIDE
    halves = CMP_LEN // CMP_STRIDE
    assert halves == 2

    def compress(x_ref, pe, w1_ref, w2_ref):
        ya = jnp.zeros((n_rows, CMP_HIDDEN), F32)
        yb = jnp.zeros((n_rows, CMP_HIDDEN), F32)
        for l in range(CMP_STRIDE):
            r = x_ref[0, pl.ds(l, n_rows, stride=CMP_STRIDE), :]
            la, lb = l, CMP_STRIDE + l
            ya = ya + jnp.dot((r + pe[la:la + 1, :]).astype(BF16), w1_ref[la * HEAD_DIM:(la + 1) * HEAD_DIM, :],
                              preferred_element_type=F32)
            yb = yb + jnp.dot((r + pe[lb:lb + 1, :]).astype(BF16), w1_ref[lb * HEAD_DIM:(lb + 1) * HEAD_DIM, :],
                              preferred_element_type=F32)
        pre = ya + pltpu.roll(yb, n_rows - 1, axis=0)
        return jnp.dot(_gelu_tanh(pre).astype(BF16), w2_ref[...], preferred_element_type=F32)

    ko_ref[0, 0] = compress(kc_ref, pe_ref[0], wk1_ref, wk2_ref)
    vo_ref[0, 0] = compress(vc_ref, pe_ref[1], wv1_ref, wv2_ref)


def _nsa_compress(z3, cmp_pos, wk1, wk2, wv1, wv2):
    b, s_len, _ = z3.shape
    n_rows = s_len // CMP_STRIDE
    kc_off = NSA_W // LANES
    vc_off = kc_off + NSA_GROUPS
    const2 = lambda bi, gi: (0, 0)
    in_specs = [pl.BlockSpec((1, s_len, HEAD_DIM), lambda bi, gi: (bi, 0, kc_off + gi)),
                pl.BlockSpec((1, s_len, HEAD_DIM), lambda bi, gi: (bi, 0, vc_off + gi)),
                pl.BlockSpec(cmp_pos.shape, lambda bi, gi: (0, 0, 0)),
                pl.BlockSpec(wk1.shape, const2), pl.BlockSpec(wk2.shape, const2),
                pl.BlockSpec(wv1.shape, const2), pl.BlockSpec(wv2.shape, const2)]
    out_spec = pl.BlockSpec((1, 1, n_rows, HEAD_DIM), lambda bi, gi: (bi, gi, 0, 0))
    out_sds = jax.ShapeDtypeStruct((b, NSA_GROUPS, n_rows, HEAD_DIM), F32)
    return pl.pallas_call(
        _nsa_compress_body, grid=(b, NSA_GROUPS), in_specs=in_specs,
        out_specs=[out_spec, out_spec], out_shape=[out_sds, out_sds],
        compiler_params=_cparams(("parallel", "parallel")), name="nsa_compress")(
            z3, z3, cmp_pos, wk1, wk2, wv1, wv2)


def _nsa_body(q_ref, ks_ref, vs_ref, kw_ref, vw_ref, kc_ref, vc_ref, gate_ref, cos_ref, sin_ref, ov_ref,
              o_ref, krs, vst, krw, vwt, sel_scr):
    i = pl.program_id(2)
    tq = NSA_TQ
    tk = NSA_TQ
    scale = HEAD_DIM ** -0.5
    n_sel_blk = ks_ref.shape[1] // SEL_BLOCK
    n_cmp = (ks_ref.shape[1] - CMP_LEN) // CMP_STRIDE + 1
    shift = SEL_BLOCK.bit_length() - 1
    assert 1 << shift == SEL_BLOCK and n_sel_blk <= LANES
    blk_per_tile = tk // SEL_BLOCK

    @pl.when(i == 0)
    def _():
        krs[...] = _rotate(ks_ref[0], cos_ref[...], sin_ref[...]).astype(BF16)
        krw[...] = _rotate(kw_ref[0], cos_ref[...], sin_ref[...]).astype(BF16)
        _store_transposed(vst, vs_ref[0])
        _store_transposed(vwt, vw_ref[0])

    t0 = pl.multiple_of(i * tq, tq)
    pos_row = t0 + lax.broadcasted_iota(jnp.int32, (1, tq), 1)
    cos_q = cos_ref[pl.ds(t0, tq), :]
    sin_q = sin_ref[pl.ds(t0, tq), :]

    kc = kc_ref[0, 0].astype(BF16)
    vc = vc_ref[0, 0].astype(BF16)
    n_col = lax.broadcasted_iota(jnp.int32, (LANES, 1), 0)
    cmp_end = jnp.where(n_col < n_cmp, n_col * CMP_STRIDE + (CMP_LEN - 1), jnp.iinfo(jnp.int32).max)
    cmp_ok = cmp_end <= pos_row
    qr_heads, o_cmp = [], []
    p_sum = jnp.zeros((LANES, tq), F32)
    for r in range(NSA_REP):
        q = q_ref[0, :, r * HEAD_DIM:(r + 1) * HEAD_DIM] * scale
        qr_heads.append(_rotate(q * LOG2_E, cos_q, sin_q).astype(BF16))
        s = lax.dot_general(kc, q.astype(BF16), _NT, preferred_element_type=F32)
        s = jnp.where(cmp_ok, s, NEG_INF)
        e = jnp.where(cmp_ok, jnp.exp(s - jnp.max(s, axis=0, keepdims=True)), 0.0)
        p = e / jnp.maximum(jnp.sum(e, axis=0, keepdims=True), 1e-30)
        o_cmp.append(lax.dot_general(p.astype(BF16), vc, _TN, preferred_element_type=F32))
        p_sum = p_sum + p
    imp = jnp.dot(ov_ref[...], p_sum, precision=HIGHEST, preferred_element_type=F32)

    behind = (pos_row >> shift) - lax.broadcasted_iota(jnp.int32, imp.shape, 0)
    val = jnp.where(behind == 0, FORCED_SCORE, jnp.where(behind == 1, FORCED_SCORE, imp))
    val = jnp.where(lax.broadcasted_iota(jnp.int32, imp.shape, 0) == 0, FORCED_SCORE, val)
    val = jnp.where(behind >= 0, val, NEG_INF)
    picked = jnp.where(val > 0.5 * NEG_INF, _rank_before(val, n_sel_blk, 0), float(n_sel_blk)) < SEL_TOPN
    sel_scr[...] = jnp.where(picked, 0.0, NEG_INF)

    n = NSA_REP

    def sel_bias(k0, n_tiles):
        blk0 = (k0 >> shift) if isinstance(k0, int) else pl.multiple_of(k0 >> shift, blk_per_tile)
        return jnp.concatenate([jnp.broadcast_to(sel_scr[pl.ds(blk0 + b, 1), :], (SEL_BLOCK, tq))
                                for b in range(n_tiles * blk_per_tile)], axis=0)

    def sel_tiles(k0, n_tiles, carry):
        return tuple(_flash_steps(qr_heads, [krs[pl.ds(k0, n_tiles * tk), :]] * n,
                                  [vst[:, pl.ds(k0, n_tiles * tk)]] * n, [sel_bias(k0, n_tiles)] * n, carry))

    causal_bias = jnp.where(lax.broadcasted_iota(jnp.int32, (tk, tq), 0)
                            <= lax.broadcasted_iota(jnp.int32, (tk, tq), 1), 0.0, NEG_INF)
    span = WINDOW + tq
    start = pl.multiple_of(jnp.maximum(t0 - WINDOW, 0), tq)
    gap = ((t0 - start) + lax.broadcasted_iota(jnp.int32, (span, tq), 1)
           - lax.broadcasted_iota(jnp.int32, (span, tq), 0))
    win_bias = jnp.where(gap >= 0, jnp.where(gap < WINDOW, 0.0, NEG_INF), NEG_INF)
    group = _flash_steps(qr_heads * 2,
                         [krs[pl.ds(t0, tk), :]] * n + [krw[pl.ds(start, span), :]] * n,
                         [vst[:, pl.ds(t0, tk)]] * n + [vwt[:, pl.ds(start, span)]] * n,
                         [sel_bias(t0, 1) + causal_bias] * n + [win_bias] * n, None)
    win_fin = group[2 * n:]

    odd = i % 2
    carry = lax.cond(odd == 1, lambda c: sel_tiles(0, 1, c), lambda c: c, tuple(group[:2 * n]))
    sel_fin = lax.fori_loop(0, i // 2, lambda p, c: sel_tiles(pl.multiple_of((odd + 2 * p) * tk, tk), 2, c), carry)

    gates = _sigmoid(gate_ref[0])
    gates_t = gates.T
    for r in range(NSA_REP):
        o_t = (gates_t[3 * r + 1:3 * r + 2, :] * _flash_output(sel_fin[2 * r + 1])
               + gates_t[3 * r + 2:3 * r + 3, :] * _flash_output(win_fin[2 * r + 1]))
        o_ref[0, :, r * HEAD_DIM:(r + 1) * HEAD_DIM] = gates[:, 3 * r:3 * r + 1] * o_cmp[r] + o_t.T


def _nsa_attention(z3, k_cmp, v_cmp, cos2, sin2):
    b, s_len, _ = z3.shape
    assert s_len >= WINDOW + NSA_TQ and s_len % NSA_TQ == 0
    n_cmp = (s_len - CMP_LEN) // CMP_STRIDE + 1
    n_blk = s_len // SEL_BLOCK
    n_rows = k_cmp.shape[2]
    assert n_rows == LANES
    cmp_start = np.arange(n_rows) * CMP_STRIDE
    cmp_end = cmp_start + CMP_LEN - 1
    blk_lo = np.arange(LANES) * SEL_BLOCK
    overlap = ((cmp_start[:, None] <= blk_lo[None, :] + SEL_BLOCK - 1) & (cmp_end[:, None] >= blk_lo[None, :])
               & (np.arange(n_rows)[:, None] < n_cmp)).astype(np.float32)
    overlap_t = np.ascontiguousarray(overlap.T[:n_blk])
    base = NSA_W // LANES

    def kv(off):
        return pl.BlockSpec((1, s_len, HEAD_DIM), lambda bi, gi, i: (bi, 0, base + off * NSA_GROUPS + gi))

    cmp_spec = pl.BlockSpec((1, 1, n_rows, HEAD_DIM), lambda bi, gi, i: (bi, gi, 0, 0))
    const2 = lambda bi, gi, i: (0, 0)
    in_specs = [pl.BlockSpec((1, NSA_TQ, NSA_REP * HEAD_DIM), lambda bi, gi, i: (bi, i, gi)),
                kv(2), kv(3), kv(4), kv(5), cmp_spec, cmp_spec,
                pl.BlockSpec((1, NSA_TQ, LANES), lambda bi, gi, i: (bi, i, ODD_GATE_BLK + gi)),
                pl.BlockSpec((s_len, HEAD_DIM), const2), pl.BlockSpec((s_len, HEAD_DIM), const2),
                pl.BlockSpec((n_blk, n_rows), const2)]
    scratch = [pltpu.VMEM((s_len, HEAD_DIM), BF16), pltpu.VMEM((VT_ROWS, s_len), BF16),
               pltpu.VMEM((s_len, HEAD_DIM), BF16), pltpu.VMEM((VT_ROWS, s_len), BF16),
               pltpu.VMEM((n_blk, NSA_TQ), F32)]
    return pl.pallas_call(
        _nsa_body, grid=(b, NSA_GROUPS, s_len // NSA_TQ), in_specs=in_specs,
        out_specs=pl.BlockSpec((1, NSA_TQ, NSA_REP * HEAD_DIM), lambda bi, gi, i: (bi, i, gi)),
        out_shape=jax.ShapeDtypeStruct((b, s_len, NSA_W), F32),
        scratch_shapes=scratch, compiler_params=_cparams(("parallel", "parallel", "arbitrary")),
        name="nsa_attention")(z3, z3, z3, z3, z3, k_cmp, v_cmp, z3, cos2, sin2, jnp.asarray(overlap_t))


def _split_bf16(x):
    hi = x.astype(BF16)
    return hi, (x - hi.astype(F32)).astype(BF16)


def _proj_moe_body(*refs, n_act, final_norm):
    res_ref = refs[0]
    a_refs = refs[1:1 + n_act]
    wo_refs = refs[1 + n_act:1 + 2 * n_act]
    (g_ref, wrh_ref, wrl_ref, br_ref, w1_ref, w3_ref, w2_ref, fg_ref, o_ref,
     xn_scr, cw_scr, hd_scr) = refs[1 + 2 * n_act:]
    j = pl.program_id(1)
    tm = res_ref.shape[0]
    lane = lax.broadcasted_iota(jnp.int32, (tm, LANES), 1).astype(F32)
    neg = float("-inf")

    @pl.when(j == 0)
    def _():
        h = res_ref[...]
        for a_ref, wo_ref in zip(a_refs, wo_refs):
            h = h + jnp.dot(a_ref[...].astype(BF16), wo_ref[...], preferred_element_type=F32)
        o_ref[...] = h
        xn = _rms(h, g_ref[...])
        xh, xl = _split_bf16(xn)
        xn_scr[...] = xh
        logits = (jnp.dot(xh, wrh_ref[...], preferred_element_type=F32)
                  + (jnp.dot(xl, wrh_ref[...], preferred_element_type=F32)
                     + jnp.dot(xh, wrl_ref[...], preferred_element_type=F32))) + br_ref[...]
        is_g = lane < MOE_GROUPS
        gl = jnp.where(is_g, logits, neg)
        g_max = jnp.max(gl, axis=1, keepdims=True)
        g_w = 1.0 / jnp.sum(jnp.where(is_g, jnp.exp(gl - g_max), 0.0), axis=1, keepdims=True)
        g_top = jnp.min(jnp.where(gl == g_max, lane, float(LANES)), axis=1, keepdims=True)
        lo = MOE_GROUPS + MOE_EPG * g_top
        el = jnp.where(lane >= lo, jnp.where(lane < lo + MOE_EPG, logits, neg), neg)
        v1 = jnp.max(el, axis=1, keepdims=True)
        i1 = jnp.min(jnp.where(el == v1, lane, float(LANES)), axis=1, keepdims=True)
        el2 = jnp.where(lane == i1, neg, el)
        v2 = jnp.max(el2, axis=1, keepdims=True)
        i2 = jnp.min(jnp.where(el2 == v2, lane, float(LANES)), axis=1, keepdims=True)
        e2 = jnp.exp(v2 - v1)
        den = 1.0 + e2
        cw_scr[...] = (jnp.where(lane == i1, g_w / den, 0.0) + jnp.where(lane == i2, g_w * e2 / den, 0.0))

    xn = xn_scr[...]
    cw = cw_scr[...]
    for r in range(MOE_EPG):
        h1 = jnp.dot(xn, w1_ref[r], preferred_element_type=F32)
        h3 = jnp.dot(xn, w3_ref[r], preferred_element_type=F32)
        e_lane = (MOE_GROUPS + MOE_EPG * j + r).astype(F32)
        col = jnp.sum(jnp.where(lane == e_lane, cw, 0.0), axis=1, keepdims=True)
        hd_scr[:, r * MOE_HIDDEN:(r + 1) * MOE_HIDDEN] = ((h1 * _sigmoid(h1)) * h3 * col).astype(BF16)
    o_ref[...] += jnp.dot(hd_scr[...], w2_ref[...], preferred_element_type=F32)

    if final_norm:
        @pl.when(j == pl.num_programs(1) - 1)
        def _():
            o_ref[...] = _rms(o_ref[...], fg_ref[...])


def _proj_moe(res2d, acts, w_outs, gain, w_g, b_g, w_e, b_e, w1, w3, w2, final_gain, final_norm, tm=1024):
    t, d = res2d.shape
    n_act = len(acts)
    wr = jnp.zeros((d, LANES), F32).at[:, :MOE_GROUPS].set(w_g).at[:, MOE_GROUPS:MOE_GROUPS + MOE_EXPERTS].set(w_e)
    wr_hi, wr_lo = _split_bf16(wr)
    br = jnp.zeros((1, LANES), F32).at[0, :MOE_GROUPS].set(b_g).at[0, MOE_GROUPS:MOE_GROUPS + MOE_EXPERTS].set(b_e)
    tile = lambda i, j: (i, 0)
    const = lambda i, j: (0, 0)
    group_hidden = MOE_EPG * MOE_HIDDEN
    in_specs = [pl.BlockSpec((tm, d), tile)]
    in_specs += [pl.BlockSpec((tm, a.shape[1]), tile) for a in acts]
    in_specs += [pl.BlockSpec(w.shape, const) for w in w_outs]
    in_specs += [pl.BlockSpec((1, d), const),
                 pl.BlockSpec((d, LANES), const), pl.BlockSpec((d, LANES), const),
                 pl.BlockSpec((1, LANES), const),
                 pl.BlockSpec((MOE_EPG, d, MOE_HIDDEN), lambda i, j: (j, 0, 0)),
                 pl.BlockSpec((MOE_EPG, d, MOE_HIDDEN), lambda i, j: (j, 0, 0)),
                 pl.BlockSpec((group_hidden, d), lambda i, j: (j, 0)),
                 pl.BlockSpec((1, d), const)]
    scratch = [pltpu.VMEM((tm, d), BF16), pltpu.VMEM((tm, LANES), F32), pltpu.VMEM((tm, group_hidden), BF16)]
    return pl.pallas_call(
        functools.partial(_proj_moe_body, n_act=n_act, final_norm=final_norm),
        grid=(t // tm, MOE_GROUPS), in_specs=in_specs,
        out_specs=pl.BlockSpec((tm, d), tile),
        out_shape=jax.ShapeDtypeStruct((t, d), F32),
        scratch_shapes=scratch, compiler_params=_cparams(("parallel", "arbitrary")),
        name="out_proj_moe")(res2d, *acts, *w_outs, gain.reshape(1, d), wr_hi, wr_lo, br,
                             w1.astype(BF16), w3.astype(BF16), w2.astype(BF16).reshape(MOE_EXPERTS * MOE_HIDDEN, d),
                             final_gain.reshape(1, d))


def _rope_tables(s_len):
    half = HEAD_DIM // 2
    inv = ROPE_THETA ** (-(jnp.arange(half, dtype=F32) / half))
    ang = jnp.arange(s_len, dtype=F32)[:, None] * inv[None, :]
    cos, sin = jnp.cos(ang), jnp.sin(ang)
    return jnp.concatenate([cos, cos], axis=-1), jnp.concatenate([-sin, sin], axis=-1)


def _even_weights(w_in):
    n_if = 2 * MLSTM_HEADS
    a = 4 * MLSTM_W
    w_main = jnp.concatenate([w_in[:, :a], w_in[:, a + n_if:],
                              w_in[:, a:a + n_if], jnp.zeros((w_in.shape[0], LANES - n_if), w_in.dtype)], axis=1)
    w_if_t = jnp.concatenate([w_in[:, a:a + n_if].T, jnp.zeros((16 - n_if, w_in.shape[0]), w_in.dtype)], axis=0)
    return w_main.astype(BF16), w_if_t.astype(BF16)


def _odd_weights(w_in):
    a = NSA_W + 6 * NSA_KV_W
    per_group = NSA_REP * 3
    pad = jnp.zeros((w_in.shape[0], LANES - per_group), w_in.dtype)
    cols = [w_in[:, :a]]
    for g in range(NSA_GROUPS):
        cols += [w_in[:, a + g * per_group:a + (g + 1) * per_group], pad]
    return jnp.concatenate(cols, axis=1).astype(BF16)


def kernel(x, mix_norm_0, w_in_0, mlstm_conv_0, mlstm_gate_b_0, mlstm_head_norm_0, w_out_0, ffn_norm_0, router_group_0, router_group_b_0, router_expert_0, router_expert_b_0, moe_w1_0, moe_w3_0, moe_w2_0, mix_norm_1, w_in_1, nsa_cmp_pos_1, nsa_cmp_k1_1, nsa_cmp_k2_1, nsa_cmp_v1_1, nsa_cmp_v2_1, w_out_1, ffn_norm_1, router_group_1, router_group_b_1, router_expert_1, router_expert_b_1, moe_w1_1, moe_w3_1, moe_w2_1, final_norm):
    b, s_len, d = x.shape
    t = b * s_len
    cos2, sin2 = _rope_tables(s_len)
    x2d = x.reshape(t, d)

    w_main, w_if_t = _even_weights(w_in_0)
    z0, gt = _norm_matmul(x2d, mix_norm_0, w_main, w_if_t)
    z0 = z0.reshape(b, s_len, EVEN_N)
    h_m = _mlstm(z0, gt, mlstm_conv_0, mlstm_gate_b_0, mlstm_head_norm_0)
    o_b = _moba(z0, cos2, sin2)
    w_out = w_out_0.astype(BF16)
    h = _proj_moe(x2d, [h_m.reshape(t, MLSTM_W), o_b.reshape(t, MOBA_W)], [w_out[:MLSTM_W], w_out[MLSTM_W:]],
                  ffn_norm_0, router_group_0, router_group_b_0, router_expert_0, router_expert_b_0,
                  moe_w1_0, moe_w3_0, moe_w2_0, final_norm, False)

    z1 = _norm_matmul(h, mix_norm_1, _odd_weights(w_in_1)).reshape(b, s_len, ODD_N)
    k_cmp, v_cmp = _nsa_compress(z1, nsa_cmp_pos_1, nsa_cmp_k1_1.astype(BF16), nsa_cmp_k2_1.astype(BF16),
                                 nsa_cmp_v1_1.astype(BF16), nsa_cmp_v2_1.astype(BF16))
    o = _nsa_attention(z1, k_cmp, v_cmp, cos2, sin2)
    h = _proj_moe(h, [o.reshape(t, NSA_W)], [w_out_1.astype(BF16)],
                  ffn_norm_1, router_group_1, router_group_b_1, router_expert_1, router_expert_b_1,
                  moe_w1_1, moe_w3_1, moe_w2_1, final_norm, True)
    return h.reshape(b, s_len, d)
```

```python
import functools

import numpy as np
import jax
import jax.numpy as jnp
from jax import lax
from jax.experimental import pallas as pl
from jax.experimental.pallas import tpu as pltpu

F32 = jnp.float32
BF16 = jnp.bfloat16
HIGHEST = lax.Precision.HIGHEST

LANES = 128
BF16_SUBLANES = 16
D_MODEL = 1024
HEAD_DIM = 128
ROPE_THETA = 10000.0
NORM_EPS = 1e-6
NEG_INF = -1e30
FORCED_SCORE = 1e4

MLSTM_HEADS = 4
MLSTM_W = MLSTM_HEADS * HEAD_DIM
MLSTM_CHUNK = 64
MLSTM_CONV = 4
MLSTM_GATE_CAP = 15.0
MOBA_HEADS = 4
MOBA_W = MOBA_HEADS * HEAD_DIM
MOBA_BLOCK = 256
MOBA_TOPK = 3

NSA_HEADS = 8
NSA_GROUPS = 2
NSA_REP = NSA_HEADS // NSA_GROUPS
NSA_W = NSA_HEADS * HEAD_DIM
NSA_KV_W = NSA_GROUPS * HEAD_DIM
CMP_LEN = 32
CMP_STRIDE = 16
CMP_HIDDEN = 256
SEL_BLOCK = 64
SEL_TOPN = 8
WINDOW = 512
NSA_TQ = 256

MOE_GROUPS = 4
MOE_EPG = 4
MOE_EXPERTS = MOE_GROUPS * MOE_EPG
MOE_HIDDEN = D_MODEL // 4

EVEN_N = 4 * MLSTM_W + 3 * MOBA_W + LANES
EVEN_IF_BLK = (4 * MLSTM_W + 3 * MOBA_W) // LANES
ODD_GATE_BLK = (NSA_W + 6 * NSA_KV_W) // LANES
ODD_N = NSA_W + 6 * NSA_KV_W + NSA_GROUPS * LANES

VMEM_LIMIT = 56 * 1024 * 1024
VT_ROWS = HEAD_DIM + BF16_SUBLANES
LOG2_E = float(np.log2(np.e))

_NT = (((1,), (1,)), ((), ()))
_TN = (((0,), (0,)), ((), ()))


def _cparams(sem):
    return pltpu.CompilerParams(dimension_semantics=sem, vmem_limit_bytes=VMEM_LIMIT)


def _rms(x, g):
    return x * lax.rsqrt(jnp.mean(x * x, axis=-1, keepdims=True) + NORM_EPS) * g


def _sigmoid(x):
    return 0.5 * jnp.tanh(0.5 * x) + 0.5


def _log_sigmoid(x):
    return -(jnp.maximum(-x, 0.0) + jnp.log1p(jnp.exp(-jnp.abs(x))))


def _rotate(x, cos2, sin2):
    return x * cos2 + pltpu.roll(x, HEAD_DIM // 2, axis=1) * sin2


def _flash_steps(qs, k_ts, vt_ts, biases, carries):
    n = len(qs)
    ss = [lax.dot_general(k_ts[u], qs[u], _NT, preferred_element_type=F32) + biases[u] for u in range(n)]
    ms, es, alphas, pvs = [], [], [], []

    def value_matmul(u):
        pv = jnp.dot(vt_ts[u], es[u], preferred_element_type=F32)
        if carries is not None:
            pv = alphas[u] * carries[2 * u + 1] + pv
        pvs.append(pv)

    for u in range(n):
        m_new = jnp.max(ss[u], axis=0, keepdims=True)
        if carries is not None:
            m_new = jnp.maximum(carries[2 * u], m_new)
            alphas.append(jnp.exp2(carries[2 * u] - m_new))
        ms.append(m_new)
        es.append(jnp.exp2(ss[u] - m_new).astype(BF16))
        if u > 0:
            value_matmul(u - 1)
    value_matmul(n - 1)
    out = []
    for u in range(n):
        out += [ms[u], pvs[u]]
    return out


def _flash_output(acc):
    return acc[:HEAD_DIM] / jnp.maximum(acc[HEAD_DIM:HEAD_DIM + 1], 1e-30)


def _store_transposed(dst_ref, x):
    n_rows, n_cols = x.shape
    for r in range(0, n_rows, LANES):
        dst_ref[:n_cols, r:r + LANES] = x[r:r + LANES, :].T.astype(dst_ref.dtype)
    dst_ref[n_cols:, :] = jnp.ones((dst_ref.shape[0] - n_cols, n_rows), dst_ref.dtype)


def _rank_before(v, n_valid, axis):
    idx = lax.broadcasted_iota(jnp.int32, v.shape, axis)
    rank = jnp.zeros(v.shape, F32)
    for m in range(n_valid):
        vm = v[m:m + 1, :] if axis == 0 else v[:, m:m + 1]
        tie = jnp.where(idx > m, 1.0, 0.0)
        rank = rank + jnp.where(vm > v, 1.0, jnp.where(vm == v, tie, 0.0))
    return rank


def _norm_matmul_body(*refs, n_chunk, with_t):
    if with_t:
        x_ref, g_ref, w_ref, wt_ref, o_ref, ot_ref = refs
    else:
        x_ref, g_ref, w_ref, o_ref = refs
    yb = _rms(x_ref[...], g_ref[...]).astype(BF16)
    n = w_ref.shape[1]
    for c0 in range(0, n, n_chunk):
        c1 = min(n, c0 + n_chunk)
        o_ref[:, c0:c1] = jnp.dot(yb, w_ref[:, c0:c1], preferred_element_type=F32)
    if with_t:
        ot_ref[...] = lax.dot_general(wt_ref[...], yb, _NT, preferred_element_type=F32)


def _norm_matmul(x2d, gain, w, wt=None, tm=512):
    t, d = x2d.shape
    n = w.shape[1]
    with_t = wt is not None
    in_specs = [pl.BlockSpec((tm, d), lambda i: (i, 0)),
                pl.BlockSpec((1, d), lambda i: (0, 0)),
                pl.BlockSpec((d, n), lambda i: (0, 0))]
    out_specs = [pl.BlockSpec((tm, n), lambda i: (i, 0))]
    out_shape = [jax.ShapeDtypeStruct((t, n), F32)]
    args = [x2d, gain.reshape(1, d), w]
    if with_t:
        r = wt.shape[0]
        in_specs.append(pl.BlockSpec((r, d), lambda i: (0, 0)))
        out_specs.append(pl.BlockSpec((r, tm), lambda i: (0, i)))
        out_shape.append(jax.ShapeDtypeStruct((r, t), F32))
        args.append(wt)
    outs = pl.pallas_call(
        functools.partial(_norm_matmul_body, n_chunk=512, with_t=with_t),
        grid=(t // tm,), in_specs=in_specs, out_specs=out_specs, out_shape=out_shape,
        compiler_params=_cparams(("parallel",)), name="norm_in_proj")(*args)
    return outs if with_t else outs[0]


def _chunk_cumsum(x, axis):
    idx = lax.broadcasted_iota(jnp.int32, x.shape, axis) % MLSTM_CHUNK
    d = 1
    while d < MLSTM_CHUNK:
        x = x + jnp.where(idx >= d, pltpu.roll(x, d, axis=axis), 0.0)
        d *= 2
    return x


def _mlstm_body(q_ref, k_ref, v_ref, og_ref, if_ref, gt_ref, cwq_ref, cwk_ref, bcol_ref, brow_ref,
                gain_ref, out_ref, ks, qts, vts, hts, css, brs, lirs, gcol):
    h = pl.program_id(1)
    s_len = q_ref.shape[1]
    n_chunks = s_len // MLSTM_CHUNK
    L = MLSTM_CHUNK
    per_slab = LANES // L
    row = lax.broadcasted_iota(jnp.int32, (s_len, HEAD_DIM), 0)

    def conv_silu(x, w):
        acc = x * w[MLSTM_CONV - 1:MLSTM_CONV, :]
        for d in range(1, MLSTM_CONV):
            shifted = jnp.where(row >= d, pltpu.roll(x, d, axis=0), 0.0)
            acc = acc + shifted * w[MLSTM_CONV - 1 - d:MLSTM_CONV - d, :]
        return acc * _sigmoid(acc)

    def store_chunks_transposed(dst, x):
        for p in range(s_len // LANES):
            slab_t = x[p * LANES:(p + 1) * LANES, :].T
            for j in range(per_slab):
                dst[per_slab * p + j] = slab_t[:, j * L:(j + 1) * L].astype(dst.dtype)

    store_chunks_transposed(qts, conv_silu(q_ref[0], cwq_ref[...]) * (HEAD_DIM ** -0.5))
    store_chunks_transposed(vts, v_ref[0])
    ks[...] = conv_silu(k_ref[0], cwk_ref[...]).astype(BF16)

    lane = lax.broadcasted_iota(jnp.int32, gcol.shape, 1)

    @pl.when(h == 0)
    def _():
        pre = if_ref[0] + bcol_ref[...]
        pre = MLSTM_GATE_CAP * jnp.tanh(pre / MLSTM_GATE_CAP)
        gcol[...] = jnp.where(lane < MLSTM_HEADS, pre, _chunk_cumsum(_log_sigmoid(pre), 0))

    g_all = gcol[...]
    css[...] = jnp.sum(jnp.where(lane == h, g_all, jnp.where(lane == h + MLSTM_HEADS, -g_all, 0.0)),
                       axis=1, keepdims=True)

    pr = gt_ref[...] + brow_ref[...]
    pr = MLSTM_GATE_CAP * jnp.tanh(pr / MLSTM_GATE_CAP)
    sub = lax.broadcasted_iota(jnp.int32, pr.shape, 0)
    b_rows = _chunk_cumsum(_log_sigmoid(pr), 1)
    li_row = jnp.sum(jnp.where(sub == h, pr, 0.0), axis=0, keepdims=True)
    b_row = jnp.sum(jnp.where(sub == h + MLSTM_HEADS, b_rows, 0.0), axis=0, keepdims=True)
    for c in range(n_chunks):
        brs[c] = b_row[:, c * L:(c + 1) * L]
        lirs[c] = li_row[:, c * L:(c + 1) * L]

    tri = lax.broadcasted_iota(jnp.int32, (L, L), 0) <= lax.broadcasted_iota(jnp.int32, (L, L), 1)
    gain = jnp.broadcast_to(gain_ref[...], (HEAD_DIM, L))

    group_size = 8
    assert n_chunks % group_size == 0

    def group(gi, carry):
        c_state, n_state, m_state = carry
        cs = [gi * group_size + j for j in range(group_size)]
        r0s = [pl.multiple_of(c * L, L) for c in cs]
        ks_ = [ks[pl.ds(r0, L), :] for r0 in r0s]
        q_ts = [qts[c] for c in cs]
        v_ts = [vts[c] for c in cs]
        b_rs = [brs[c] for c in cs]
        kvs, ksums, kqs, g_maxs, b_lasts = [], [], [], [], []
        for j, c in enumerate(cs):
            b_last = b_rs[j][:, L - 1:L]
            g = b_last - b_rs[j] + lirs[c]
            g_max = jnp.max(g, axis=1, keepdims=True)
            w = jnp.exp(g - g_max)
            kvs.append(jnp.dot((v_ts[j] * w).astype(BF16), ks_[j], preferred_element_type=F32))
            ksums.append(jnp.dot(jnp.broadcast_to(w, (BF16_SUBLANES, L)).astype(BF16), ks_[j],
                                 preferred_element_type=F32)[:1])
            kqs.append(jnp.dot(ks_[j], q_ts[j], preferred_element_type=F32))
            g_maxs.append(g_max)
            b_lasts.append(b_last)
        c_ins, n_ins, m_ins = [], [], []
        for j in range(group_size):
            c_ins.append(c_state.astype(BF16))
            n_ins.append(jnp.broadcast_to(n_state, (BF16_SUBLANES, HEAD_DIM)).astype(BF16))
            m_ins.append(m_state)
            m_new = jnp.maximum(b_lasts[j] + m_state, g_maxs[j])
            a = jnp.exp(b_lasts[j] + m_state - m_new)
            cc = jnp.exp(g_maxs[j] - m_new)
            c_state, n_state, m_state = a * c_state + cc * kvs[j], a * n_state + cc * ksums[j], m_new
        c_qs = [jnp.dot(c_ins[j], q_ts[j], preferred_element_type=F32) for j in range(group_size)]
        n_qs = [jnp.dot(n_ins[j], q_ts[j], preferred_element_type=F32)[:1] for j in range(group_size)]
        ss, m_ts, w_inters = [], [], []
        for j in range(group_size):
            dmat = jnp.where(tri, b_rs[j] + css[pl.ds(r0s[j], L), :], NEG_INF)
            inter = b_rs[j] + m_ins[j]
            m_t = jnp.maximum(inter, jnp.max(dmat, axis=0, keepdims=True))
            ss.append(kqs[j] * jnp.exp(dmat - m_t))
            m_ts.append(m_t)
            w_inters.append(jnp.exp(inter - m_t))
        svs = [jnp.dot(v_ts[j].astype(BF16), ss[j].astype(BF16), preferred_element_type=F32)
               for j in range(group_size)]
        for j, c in enumerate(cs):
            num = svs[j] + w_inters[j] * c_qs[j]
            den = jnp.sum(ss[j], axis=0, keepdims=True) + w_inters[j] * n_qs[j]
            ht = num / jnp.maximum(jnp.abs(den), jnp.exp(-m_ts[j]))
            hts[c] = ht * lax.rsqrt(jnp.mean(ht * ht, axis=0, keepdims=True) + NORM_EPS) * gain
        return c_state, n_state, m_state

    init = (jnp.zeros((HEAD_DIM, HEAD_DIM), F32), jnp.zeros((1, HEAD_DIM), F32), jnp.zeros((1, 1), F32))
    lax.fori_loop(0, n_chunks // group_size, group, init)

    for p in range(s_len // LANES):
        slab_t = jnp.concatenate([hts[per_slab * p + j] for j in range(per_slab)], axis=1)
        rows = slice(p * LANES, (p + 1) * LANES)
        out_ref[0, rows, :] = (slab_t.T * _sigmoid(og_ref[0, rows, :])).astype(out_ref.dtype)


def _mlstm(z3, gt, conv_w, gate_b, head_gain):
    b, s_len, _ = z3.shape
    n_chunks = s_len // MLSTM_CHUNK
    gr = gt.shape[0]
    bcol = jnp.zeros((1, LANES), F32).at[0, :2 * MLSTM_HEADS].set(gate_b)
    brow = jnp.zeros((gr, 1), F32).at[:2 * MLSTM_HEADS, 0].set(gate_b)

    def col(off):
        return pl.BlockSpec((1, s_len, HEAD_DIM), lambda bi, hi: (bi, 0, off + hi))

    in_specs = [col(0), col(MLSTM_HEADS), col(2 * MLSTM_HEADS), col(3 * MLSTM_HEADS),
                pl.BlockSpec((1, s_len, LANES), lambda bi, hi: (bi, 0, EVEN_IF_BLK)),
                pl.BlockSpec((gr, s_len), lambda bi, hi: (0, bi)),
                pl.BlockSpec((MLSTM_CONV, HEAD_DIM), lambda bi, hi: (0, hi)),
                pl.BlockSpec((MLSTM_CONV, HEAD_DIM), lambda bi, hi: (0, MLSTM_HEADS + hi)),
                pl.BlockSpec((1, LANES), lambda bi, hi: (0, 0)),
                pl.BlockSpec((gr, 1), lambda bi, hi: (0, 0)),
                pl.BlockSpec((HEAD_DIM, 1), lambda bi, hi: (hi, 0))]
    chunk_t = (n_chunks, HEAD_DIM, MLSTM_CHUNK)
    scratch = [pltpu.VMEM((s_len, HEAD_DIM), BF16),
               pltpu.VMEM(chunk_t, BF16), pltpu.VMEM(chunk_t, F32), pltpu.VMEM(chunk_t, F32),
               pltpu.VMEM((s_len, 1), F32),
               pltpu.VMEM((n_chunks, 1, MLSTM_CHUNK), F32), pltpu.VMEM((n_chunks, 1, MLSTM_CHUNK), F32),
               pltpu.VMEM((s_len, LANES), F32)]
    return pl.pallas_call(
        _mlstm_body, grid=(b, MLSTM_HEADS), in_specs=in_specs,
        out_specs=pl.BlockSpec((1, s_len, HEAD_DIM), lambda bi, hi: (bi, 0, hi)),
        out_shape=jax.ShapeDtypeStruct((b, s_len, MLSTM_W), BF16),
        scratch_shapes=scratch, compiler_params=_cparams(("parallel", "arbitrary")),
        name="mlstm")(z3, z3, z3, z3, z3, gt, conv_w, conv_w, bcol, brow, head_gain.reshape(MLSTM_W, 1))


def _moba_body(q_ref, k_ref, v_ref, cos_ref, sin_ref, o_ref, kr_scr, vt_scr, km_scr, sel_scr):
    i = pl.program_id(1)
    s_len = k_ref.shape[1]
    bs = MOBA_BLOCK
    nb = s_len // bs
    heads = [slice(h * HEAD_DIM, (h + 1) * HEAD_DIM) for h in range(MOBA_HEADS)]

    @pl.when(i == 0)
    def _():
        for h, cols in enumerate(heads):
            _store_transposed(vt_scr.at[h], v_ref[0, :, cols])
            kr = _rotate(k_ref[0, :, cols], cos_ref[...], sin_ref[...])
            kr_scr[:, cols] = kr.astype(BF16)
            rows = [jnp.sum(kr[n * bs:(n + 1) * bs, :], axis=0, keepdims=True) / float(bs) for n in range(nb)]
            rows.append(jnp.zeros((km_scr.shape[1] - nb, HEAD_DIM), F32))
            km_scr[h] = jnp.concatenate(rows, axis=0)

    t0 = pl.multiple_of(i * bs, bs)
    cos_q = cos_ref[pl.ds(t0, bs), :]
    sin_q = sin_ref[pl.ds(t0, bs), :]
    blk = lax.broadcasted_iota(jnp.int32, (km_scr.shape[1], bs), 0)
    causal_bias = jnp.where(lax.broadcasted_iota(jnp.int32, (bs, bs), 0)
                            <= lax.broadcasted_iota(jnp.int32, (bs, bs), 1), 0.0, NEG_INF)

    qbs = []
    for h, cols in enumerate(heads):
        qr = _rotate(q_ref[0, :, cols], cos_q, sin_q)
        gate_t = lax.dot_general(km_scr[h], qr, _NT, precision=HIGHEST, preferred_element_type=F32)
        val = jnp.where(blk < i, gate_t, NEG_INF)
        picked = jnp.where(val > 0.5 * NEG_INF, _rank_before(val, nb, 0), float(nb)) < MOBA_TOPK
        sel_scr[h] = jnp.where(picked, 0.0, NEG_INF)
        qbs.append((qr * (HEAD_DIM ** -0.5 * LOG2_E)).astype(BF16))

    def key_tiles(k0, n_blocks):
        return ([kr_scr[pl.ds(k0, n_blocks * bs), cols] for cols in heads],
                [vt_scr[h, :, pl.ds(k0, n_blocks * bs)] for h in range(MOBA_HEADS)])

    init = tuple(_flash_steps(qbs, *key_tiles(t0, 1), [causal_bias] * MOBA_HEADS, None))

    def past_blocks(j, n_blocks, carry):
        biases = [jnp.concatenate([jnp.broadcast_to(sel_scr[h, pl.ds(j + d, 1), :], (bs, bs))
                                   for d in range(n_blocks)], axis=0) for h in range(MOBA_HEADS)]
        k0 = j * bs if isinstance(j, int) else pl.multiple_of(j * bs, bs)
        return tuple(_flash_steps(qbs, *key_tiles(k0, n_blocks), biases, carry))

    odd = i % 2
    carry = lax.cond(odd == 1, lambda c: past_blocks(0, 1, c), lambda c: c, init)
    fin = lax.fori_loop(0, i // 2, lambda p, c: past_blocks(odd + 2 * p, 2, c), carry)
    for h, cols in enumerate(heads):
        o_ref[0, :, cols] = _flash_output(fin[2 * h + 1]).T.astype(o_ref.dtype)


def _moba(z3, cos2, sin2):
    b, s_len, _ = z3.shape
    nb = s_len // MOBA_BLOCK
    q_off = 4 * MLSTM_W // MOBA_W

    def kv(off):
        return pl.BlockSpec((1, s_len, MOBA_W), lambda bi, i: (bi, 0, off))

    in_specs = [pl.BlockSpec((1, MOBA_BLOCK, MOBA_W), lambda bi, i: (bi, i, q_off)),
                kv(q_off + 1), kv(q_off + 2),
                pl.BlockSpec((s_len, HEAD_DIM), lambda bi, i: (0, 0)),
                pl.BlockSpec((s_len, HEAD_DIM), lambda bi, i: (0, 0))]
    scratch = [pltpu.VMEM((s_len, MOBA_W), BF16),
               pltpu.VMEM((MOBA_HEADS, VT_ROWS, s_len), BF16),
               pltpu.VMEM((MOBA_HEADS, BF16_SUBLANES, HEAD_DIM), F32),
               pltpu.VMEM((MOBA_HEADS, BF16_SUBLANES, MOBA_BLOCK), F32)]
    return pl.pallas_call(
        _moba_body, grid=(b, nb), in_specs=in_specs,
        out_specs=pl.BlockSpec((1, MOBA_BLOCK, MOBA_W), lambda bi, i: (bi, i, 0)),
        out_shape=jax.ShapeDtypeStruct((b, s_len, MOBA_W), BF16),
        scratch_shapes=scratch, compiler_params=_cparams(("parallel", "arbitrary")),
        name="moba")(z3, z3, z3, cos2, sin2)


def _gelu_tanh(x):
    return x * (0.5 * (1.0 + jnp.tanh(np.sqrt(2.0 / np.pi) * (x + 0.044715 * (x * x * x)))))


def _nsa_compress_body(kc_ref, vc_ref, pe_ref, wk1_ref, wk2_ref, wv1_ref, wv2_ref, ko_ref, vo_ref):
    n_rows = kc_ref.shape[1] // CMP_STRIDE
    halves = CMP_LEN // CMP_STRIDE
    assert halves == 2

    def compress(x_ref, pe, w1_ref, w2_ref):
        ya = jnp.zeros((n_rows, CMP_HIDDEN), F32)
        yb = jnp.zeros((n_rows, CMP_HIDDEN), F32)
        for l in range(CMP_STRIDE):
            r = x_ref[0, pl.ds(l, n_rows, stride=CMP_STRIDE), :]
            la, lb = l, CMP_STRIDE + l
            ya = ya + jnp.dot((r + pe[la:la + 1, :]).astype(BF16), w1_ref[la * HEAD_DIM:(la + 1) * HEAD_DIM, :],
                              preferred_element_type=F32)
            yb = yb + jnp.dot((r + pe[lb:lb + 1, :]).astype(BF16), w1_ref[lb * HEAD_DIM:(lb + 1) * HEAD_DIM, :],
                              preferred_element_type=F32)
        pre = ya + pltpu.roll(yb, n_rows - 1, axis=0)
        return jnp.dot(_gelu_tanh(pre).astype(BF16), w2_ref[...], preferred_element_type=F32)

    ko_ref[0, 0] = compress(kc_ref, pe_ref[0], wk1_ref, wk2_ref)
    vo_ref[0, 0] = compress(vc_ref, pe_ref[1], wv1_ref, wv2_ref)


def _nsa_compress(z3, cmp_pos, wk1, wk2, wv1, wv2):
    b, s_len, _ = z3.shape
    n_rows = s_len // CMP_STRIDE
    kc_off = NSA_W // LANES
    vc_off = kc_off + NSA_GROUPS
    const2 = lambda bi, gi: (0, 0)
    in_specs = [pl.BlockSpec((1, s_len, HEAD_DIM), lambda bi, gi: (bi, 0, kc_off + gi)),
                pl.BlockSpec((1, s_len, HEAD_DIM), lambda bi, gi: (bi, 0, vc_off + gi)),
                pl.BlockSpec(cmp_pos.shape, lambda bi, gi: (0, 0, 0)),
                pl.BlockSpec(wk1.shape, const2), pl.BlockSpec(wk2.shape, const2),
                pl.BlockSpec(wv1.shape, const2), pl.BlockSpec(wv2.shape, const2)]
    out_spec = pl.BlockSpec((1, 1, n_rows, HEAD_DIM), lambda bi, gi: (bi, gi, 0, 0))
    out_sds = jax.ShapeDtypeStruct((b, NSA_GROUPS, n_rows, HEAD_DIM), F32)
    return pl.pallas_call(
        _nsa_compress_body, grid=(b, NSA_GROUPS), in_specs=in_specs,
        out_specs=[out_spec, out_spec], out_shape=[out_sds, out_sds],
        compiler_params=_cparams(("parallel", "parallel")), name="nsa_compress")(
            z3, z3, cmp_pos, wk1, wk2, wv1, wv2)


def _nsa_body(q_ref, ks_ref, vs_ref, kw_ref, vw_ref, kc_ref, vc_ref, gate_ref, cos_ref, sin_ref, ov_ref,
              o_ref, krs, vst, krw, vwt, sel_scr):
    i = pl.program_id(2)
    tq = NSA_TQ
    tk = NSA_TQ
    scale = HEAD_DIM ** -0.5
    n_sel_blk = ks_ref.shape[1] // SEL_BLOCK
    n_cmp = (ks_ref.shape[1] - CMP_LEN) // CMP_STRIDE + 1
    shift = SEL_BLOCK.bit_length() - 1
    assert 1 << shift == SEL_BLOCK and n_sel_blk <= LANES
    blk_per_tile = tk // SEL_BLOCK

    @pl.when(i == 0)
    def _():
        krs[...] = _rotate(ks_ref[0], cos_ref[...], sin_ref[...]).astype(BF16)
        krw[...] = _rotate(kw_ref[0], cos_ref[...], sin_ref[...]).astype(BF16)
        _store_transposed(vst, vs_ref[0])
        _store_transposed(vwt, vw_ref[0])

    t0 = pl.multiple_of(i * tq, tq)
    pos_row = t0 + lax.broadcasted_iota(jnp.int32, (1, tq), 1)
    cos_q = cos_ref[pl.ds(t0, tq), :]
    sin_q = sin_ref[pl.ds(t0, tq), :]

    kc = kc_ref[0, 0].astype(BF16)
    vc = vc_ref[0, 0].astype(BF16)
    n_col = lax.broadcasted_iota(jnp.int32, (LANES, 1), 0)
    cmp_end = jnp.where(n_col < n_cmp, n_col * CMP_STRIDE + (CMP_LEN - 1), jnp.iinfo(jnp.int32).max)
    cmp_ok = cmp_end <= pos_row
    qr_heads, o_cmp = [], []
    p_sum = jnp.zeros((LANES, tq), F32)
    for r in range(NSA_REP):
        q = q_ref[0, :, r * HEAD_DIM:(r + 1) * HEAD_DIM] * scale
        qr_heads.append(_rotate(q * LOG2_E, cos_q, sin_q).astype(BF16))
        s = lax.dot_general(kc, q.astype(BF16), _NT, preferred_element_type=F32)
        s = jnp.where(cmp_ok, s, NEG_INF)
        e = jnp.where(cmp_ok, jnp.exp(s - jnp.max(s, axis=0, keepdims=True)), 0.0)
        p = e / jnp.maximum(jnp.sum(e, axis=0, keepdims=True), 1e-30)
        o_cmp.append(lax.dot_general(p.astype(BF16), vc, _TN, preferred_element_type=F32))
        p_sum = p_sum + p
    imp = jnp.dot(ov_ref[...], p_sum, precision=HIGHEST, preferred_element_type=F32)

    behind = (pos_row >> shift) - lax.broadcasted_iota(jnp.int32, imp.shape, 0)
    val = jnp.where(behind == 0, FORCED_SCORE, jnp.where(behind == 1, FORCED_SCORE, imp))
    val = jnp.where(lax.broadcasted_iota(jnp.int32, imp.shape, 0) == 0, FORCED_SCORE, val)
    val = jnp.where(behind >= 0, val, NEG_INF)
    picked = jnp.where(val > 0.5 * NEG_INF, _rank_before(val, n_sel_blk, 0), float(n_sel_blk)) < SEL_TOPN
    sel_scr[...] = jnp.where(picked, 0.0, NEG_INF)

    n = NSA_REP

    def sel_bias(k0, n_tiles):
        blk0 = (k0 >> shift) if isinstance(k0, int) else pl.multiple_of(k0 >> shift, blk_per_tile)
        return jnp.concatenate([jnp.broadcast_to(sel_scr[pl.ds(blk0 + b, 1), :], (SEL_BLOCK, tq))
                                for b in range(n_tiles * blk_per_tile)], axis=0)

    def sel_tiles(k0, n_tiles, carry):
        return tuple(_flash_steps(qr_heads, [krs[pl.ds(k0, n_tiles * tk), :]] * n,
                                  [vst[:, pl.ds(k0, n_tiles * tk)]] * n, [sel_bias(k0, n_tiles)] * n, carry))

    causal_bias = jnp.where(lax.broadcasted_iota(jnp.int32, (tk, tq), 0)
                            <= lax.broadcasted_iota(jnp.int32, (tk, tq), 1), 0.0, NEG_INF)
    span = WINDOW + tq
    start = pl.multiple_of(jnp.maximum(t0 - WINDOW, 0), tq)
    gap = ((t0 - start) + lax.broadcasted_iota(jnp.int32, (span, tq), 1)
           - lax.broadcasted_iota(jnp.int32, (span, tq), 0))
    win_bias = jnp.where(gap >= 0, jnp.where(gap < WINDOW, 0.0, NEG_INF), NEG_INF)
    group = _flash_steps(qr_heads * 2,
                         [krs[pl.ds(t0, tk), :]] * n + [krw[pl.ds(start, span), :]] * n,
                         [vst[:, pl.ds(t0, tk)]] * n + [vwt[:, pl.ds(start, span)]] * n,
                         [sel_bias(t0, 1) + causal_bias] * n + [win_bias] * n, None)
    win_fin = group[2 * n:]

    odd = i % 2
    carry = lax.cond(odd == 1, lambda c: sel_tiles(0, 1, c), lambda c: c, tuple(group[:2 * n]))
    sel_fin = lax.fori_loop(0, i // 2, lambda p, c: sel_tiles(pl.multiple_of((odd + 2 * p) * tk, tk), 2, c), carry)

    gates = _sigmoid(gate_ref[0])
    gates_t = gates.T
    for r in range(NSA_REP):
        o_t = (gates_t[3 * r + 1:3 * r + 2, :] * _flash_output(sel_fin[2 * r + 1])
               + gates_t[3 * r + 2:3 * r + 3, :] * _flash_output(win_fin[2 * r + 1]))
        o_ref[0, :, r * HEAD_DIM:(r + 1) * HEAD_DIM] = (gates[:, 3 * r:3 * r + 1] * o_cmp[r] + o_t.T).astype(o_ref.dtype)


def _nsa_attention(z3, k_cmp, v_cmp, cos2, sin2):
    b, s_len, _ = z3.shape
    assert s_len >= WINDOW + NSA_TQ and s_len % NSA_TQ == 0
    n_cmp = (s_len - CMP_LEN) // CMP_STRIDE + 1
    n_blk = s_len // SEL_BLOCK
    n_rows = k_cmp.shape[2]
    assert n_rows == LANES
    cmp_start = np.arange(n_rows) * CMP_STRIDE
    cmp_end = cmp_start + CMP_LEN - 1
    blk_lo = np.arange(LANES) * SEL_BLOCK
    overlap = ((cmp_start[:, None] <= blk_lo[None, :] + SEL_BLOCK - 1) & (cmp_end[:, None] >= blk_lo[None, :])
               & (np.arange(n_rows)[:, None] < n_cmp)).astype(np.float32)
    overlap_t = np.ascontiguousarray(overlap.T[:n_blk])
    base = NSA_W // LANES

    def kv(off):
        return pl.BlockSpec((1, s_len, HEAD_DIM), lambda bi, gi, i: (bi, 0, base + off * NSA_GROUPS + gi))

    cmp_spec = pl.BlockSpec((1, 1, n_rows, HEAD_DIM), lambda bi, gi, i: (bi, gi, 0, 0))
    const2 = lambda bi, gi, i: (0, 0)
    in_specs = [pl.BlockSpec((1, NSA_TQ, NSA_REP * HEAD_DIM), lambda bi, gi, i: (bi, i, gi)),
                kv(2), kv(3), kv(4), kv(5), cmp_spec, cmp_spec,
                pl.BlockSpec((1, NSA_TQ, LANES), lambda bi, gi, i: (bi, i, ODD_GATE_BLK + gi)),
                pl.BlockSpec((s_len, HEAD_DIM), const2), pl.BlockSpec((s_len, HEAD_DIM), const2),
                pl.BlockSpec((n_blk, n_rows), const2)]
    scratch = [pltpu.VMEM((s_len, HEAD_DIM), BF16), pltpu.VMEM((VT_ROWS, s_len), BF16),
               pltpu.VMEM((s_len, HEAD_DIM), BF16), pltpu.VMEM((VT_ROWS, s_len), BF16),
               pltpu.VMEM((n_blk, NSA_TQ), F32)]
    return pl.pallas_call(
        _nsa_body, grid=(b, NSA_GROUPS, s_len // NSA_TQ), in_specs=in_specs,
        out_specs=pl.BlockSpec((1, NSA_TQ, NSA_REP * HEAD_DIM), lambda bi, gi, i: (bi, i, gi)),
        out_shape=jax.ShapeDtypeStruct((b, s_len, NSA_W), BF16),
        scratch_shapes=scratch, compiler_params=_cparams(("parallel", "parallel", "arbitrary")),
        name="nsa_attention")(z3, z3, z3, z3, z3, k_cmp, v_cmp, z3, cos2, sin2, jnp.asarray(overlap_t))


def _split_bf16(x):
    hi = x.astype(BF16)
    return hi, (x - hi.astype(F32)).astype(BF16)


def _proj_moe_body(*refs, n_act, final_norm):
    res_ref = refs[0]
    a_refs = refs[1:1 + n_act]
    wo_refs = refs[1 + n_act:1 + 2 * n_act]
    (g_ref, wr_ref, br_ref, w1_ref, w3_ref, w2_ref, fg_ref, o_ref,
     xn_scr, cw_scr, hd_scr) = refs[1 + 2 * n_act:]
    j = pl.program_id(1)
    tm = res_ref.shape[0]
    lane = lax.broadcasted_iota(jnp.int32, (tm, LANES), 1).astype(F32)
    neg = float("-inf")

    @pl.when(j == 0)
    def _():
        h = res_ref[...]
        for a_ref, wo_ref in zip(a_refs, wo_refs):
            h = h + jnp.dot(a_ref[...], wo_ref[...], preferred_element_type=F32)
        o_ref[...] = h
        xn = _rms(h, g_ref[...])
        xh, xl = _split_bf16(xn)
        xn_scr[...] = xh
        hi_terms = jnp.dot(xh, wr_ref[...], preferred_element_type=F32)
        logits = (hi_terms[:, :LANES] + (jnp.dot(xl, wr_ref[:, :LANES], preferred_element_type=F32)
                                         + hi_terms[:, LANES:])) + br_ref[...]
        is_g = lane < MOE_GROUPS
        gl = jnp.where(is_g, logits, neg)
        g_max = jnp.max(gl, axis=1, keepdims=True)
        g_w = 1.0 / jnp.sum(jnp.where(is_g, jnp.exp(gl - g_max), 0.0), axis=1, keepdims=True)
        g_top = jnp.min(jnp.where(gl == g_max, lane, float(LANES)), axis=1, keepdims=True)
        lo = MOE_GROUPS + MOE_EPG * g_top
        el = jnp.where(lane >= lo, jnp.where(lane < lo + MOE_EPG, logits, neg), neg)
        v1 = jnp.max(el, axis=1, keepdims=True)
        i1 = jnp.min(jnp.where(el == v1, lane, float(LANES)), axis=1, keepdims=True)
        el2 = jnp.where(lane == i1, neg, el)
        v2 = jnp.max(el2, axis=1, keepdims=True)
        i2 = jnp.min(jnp.where(el2 == v2, lane, float(LANES)), axis=1, keepdims=True)
        e2 = jnp.exp(v2 - v1)
        den = 1.0 + e2
        cw_scr[...] = (jnp.where(lane == i1, g_w / den, 0.0) + jnp.where(lane == i2, g_w * e2 / den, 0.0))

    xn = xn_scr[...]
    cw = cw_scr[...]
    for r in range(MOE_EPG):
        h1 = jnp.dot(xn, w1_ref[r], preferred_element_type=F32)
        h3 = jnp.dot(xn, w3_ref[r], preferred_element_type=F32)
        e_lane = (MOE_GROUPS + MOE_EPG * j + r).astype(F32)
        col = jnp.sum(jnp.where(lane == e_lane, cw, 0.0), axis=1, keepdims=True)
        hd_scr[:, r * MOE_HIDDEN:(r + 1) * MOE_HIDDEN] = ((h1 * _sigmoid(h1)) * h3 * col).astype(BF16)
    o_ref[...] += jnp.dot(hd_scr[...], w2_ref[...], preferred_element_type=F32)

    if final_norm:
        @pl.when(j == pl.num_programs(1) - 1)
        def _():
            o_ref[...] = _rms(o_ref[...], fg_ref[...])


def _proj_moe(res2d, acts, w_outs, gain, w_g, b_g, w_e, b_e, w1, w3, w2, final_gain, final_norm, tm=1024):
    t, d = res2d.shape
    n_act = len(acts)
    wr = jnp.zeros((d, LANES), F32).at[:, :MOE_GROUPS].set(w_g).at[:, MOE_GROUPS:MOE_GROUPS + MOE_EXPERTS].set(w_e)
    wr_cat = jnp.concatenate(_split_bf16(wr), axis=1)
    br = jnp.zeros((1, LANES), F32).at[0, :MOE_GROUPS].set(b_g).at[0, MOE_GROUPS:MOE_GROUPS + MOE_EXPERTS].set(b_e)
    tile = lambda i, j: (i, 0)
    const = lambda i, j: (0, 0)
    group_hidden = MOE_EPG * MOE_HIDDEN
    in_specs = [pl.BlockSpec((tm, d), tile)]
    in_specs += [pl.BlockSpec((tm, a.shape[1]), tile) for a in acts]
    in_specs += [pl.BlockSpec(w.shape, const) for w in w_outs]
    in_specs += [pl.BlockSpec((1, d), const),
                 pl.BlockSpec((d, 2 * LANES), const),
                 pl.BlockSpec((1, LANES), const),
                 pl.BlockSpec((MOE_EPG, d, MOE_HIDDEN), lambda i, j: (j, 0, 0)),
                 pl.BlockSpec((MOE_EPG, d, MOE_HIDDEN), lambda i, j: (j, 0, 0)),
                 pl.BlockSpec((group_hidden, d), lambda i, j: (j, 0)),
                 pl.BlockSpec((1, d), const)]
    scratch = [pltpu.VMEM((tm, d), BF16), pltpu.VMEM((tm, LANES), F32), pltpu.VMEM((tm, group_hidden), BF16)]
    return pl.pallas_call(
        functools.partial(_proj_moe_body, n_act=n_act, final_norm=final_norm),
        grid=(t // tm, MOE_GROUPS), in_specs=in_specs,
        out_specs=pl.BlockSpec((tm, d), tile),
        out_shape=jax.ShapeDtypeStruct((t, d), F32),
        scratch_shapes=scratch, compiler_params=_cparams(("parallel", "arbitrary")),
        name="out_proj_moe")(res2d, *acts, *w_outs, gain.reshape(1, d), wr_cat, br,
                             w1.astype(BF16), w3.astype(BF16), w2.astype(BF16).reshape(MOE_EXPERTS * MOE_HIDDEN, d),
                             final_gain.reshape(1, d))


def _rope_tables(s_len):
    half = HEAD_DIM // 2
    inv = ROPE_THETA ** (-(jnp.arange(half, dtype=F32) / half))
    ang = jnp.arange(s_len, dtype=F32)[:, None] * inv[None, :]
    cos, sin = jnp.cos(ang), jnp.sin(ang)
    return jnp.concatenate([cos, cos], axis=-1), jnp.concatenate([-sin, sin], axis=-1)


def _even_weights(w_in):
    n_if = 2 * MLSTM_HEADS
    a = 4 * MLSTM_W
    w_main = jnp.concatenate([w_in[:, :a], w_in[:, a + n_if:],
                              w_in[:, a:a + n_if], jnp.zeros((w_in.shape[0], LANES - n_if), w_in.dtype)], axis=1)
    w_if_t = jnp.concatenate([w_in[:, a:a + n_if].T, jnp.zeros((16 - n_if, w_in.shape[0]), w_in.dtype)], axis=0)
    return w_main.astype(BF16), w_if_t.astype(BF16)


def _odd_weights(w_in):
    a = NSA_W + 6 * NSA_KV_W
    per_group = NSA_REP * 3
    pad = jnp.zeros((w_in.shape[0], LANES - per_group), w_in.dtype)
    cols = [w_in[:, :a]]
    for g in range(NSA_GROUPS):
        cols += [w_in[:, a + g * per_group:a + (g + 1) * per_group], pad]
    return jnp.concatenate(cols, axis=1).astype(BF16)


def kernel(x, mix_norm_0, w_in_0, mlstm_conv_0, mlstm_gate_b_0, mlstm_head_norm_0, w_out_0, ffn_norm_0, router_group_0, router_group_b_0, router_expert_0, router_expert_b_0, moe_w1_0, moe_w3_0, moe_w2_0, mix_norm_1, w_in_1, nsa_cmp_pos_1, nsa_cmp_k1_1, nsa_cmp_k2_1, nsa_cmp_v1_1, nsa_cmp_v2_1, w_out_1, ffn_norm_1, router_group_1, router_group_b_1, router_expert_1, router_expert_b_1, moe_w1_1, moe_w3_1, moe_w2_1, final_norm):
    b, s_len, d = x.shape
    t = b * s_len
    cos2, sin2 = _rope_tables(s_len)
    x2d = x.reshape(t, d)

    w_main, w_if_t = _even_weights(w_in_0)
    z0, gt = _norm_matmul(x2d, mix_norm_0, w_main, w_if_t)
    z0 = z0.reshape(b, s_len, EVEN_N)
    h_m = _mlstm(z0, gt, mlstm_conv_0, mlstm_gate_b_0, mlstm_head_norm_0)
    o_b = _moba(z0, cos2, sin2)
    w_out = w_out_0.astype(BF16)
    h = _proj_moe(x2d, [h_m.reshape(t, MLSTM_W), o_b.reshape(t, MOBA_W)], [w_out[:MLSTM_W], w_out[MLSTM_W:]],
                  ffn_norm_0, router_group_0, router_group_b_0, router_expert_0, router_expert_b_0,
                  moe_w1_0, moe_w3_0, moe_w2_0, final_norm, False)

    z1 = _norm_matmul(h, mix_norm_1, _odd_weights(w_in_1)).reshape(b, s_len, ODD_N)
    k_cmp, v_cmp = _nsa_compress(z1, nsa_cmp_pos_1, nsa_cmp_k1_1.astype(BF16), nsa_cmp_k2_1.astype(BF16),
                                 nsa_cmp_v1_1.astype(BF16), nsa_cmp_v2_1.astype(BF16))
    o = _nsa_attention(z1, k_cmp, v_cmp, cos2, sin2)
    h = _proj_moe(h, [o.reshape(t, NSA_W)], [w_out_1.astype(BF16)],
                  ffn_norm_1, router_group_1, router_group_b_1, router_expert_1, router_expert_b_1,
                  moe_w1_1, moe_w3_1, moe_w2_1, final_norm, True)
    return h.reshape(b, s_len, d)
```

```python
import functools

import numpy as np
import jax
import jax.numpy as jnp
from jax import lax
from jax.experimental import pallas as pl
from jax.experimental.pallas import tpu as pltpu

F32 = jnp.float32
BF16 = jnp.bfloat16
HIGHEST = lax.Precision.HIGHEST

LANES = 128
BF16_SUBLANES = 16
D_MODEL = 1024
HEAD_DIM = 128
ROPE_THETA = 10000.0
NORM_EPS = 1e-6
NEG_INF = -1e30
FORCED_SCORE = 1e4

MLSTM_HEADS = 4
MLSTM_W = MLSTM_HEADS * HEAD_DIM
MLSTM_CHUNK = 64
MLSTM_CONV = 4
MLSTM_GATE_CAP = 15.0
MOBA_HEADS = 4
MOBA_W = MOBA_HEADS * HEAD_DIM
MOBA_BLOCK = 256
MOBA_TOPK = 3

NSA_HEADS = 8
NSA_GROUPS = 2
NSA_REP = NSA_HEADS // NSA_GROUPS
NSA_W = NSA_HEADS * HEAD_DIM
NSA_KV_W = NSA_GROUPS * HEAD_DIM
CMP_LEN = 32
CMP_STRIDE = 16
CMP_HIDDEN = 256
SEL_BLOCK = 64
SEL_TOPN = 8
WINDOW = 512
NSA_TQ = 256

MOE_GROUPS = 4
MOE_EPG = 4
MOE_EXPERTS = MOE_GROUPS * MOE_EPG
MOE_HIDDEN = D_MODEL // 4

EVEN_N = 4 * MLSTM_W + 3 * MOBA_W + LANES
EVEN_IF_BLK = (4 * MLSTM_W + 3 * MOBA_W) // LANES
ODD_GATE_BLK = (NSA_W + 6 * NSA_KV_W) // LANES
ODD_N = NSA_W + 6 * NSA_KV_W + NSA_GROUPS * LANES

VMEM_LIMIT = 56 * 1024 * 1024
VT_ROWS = HEAD_DIM + BF16_SUBLANES
LOG2_E = float(np.log2(np.e))

_NT = (((1,), (1,)), ((), ()))
_TN = (((0,), (0,)), ((), ()))


def _cparams(sem):
    return pltpu.CompilerParams(dimension_semantics=sem, vmem_limit_bytes=VMEM_LIMIT)


def _rms(x, g):
    return x * lax.rsqrt(jnp.mean(x * x, axis=-1, keepdims=True) + NORM_EPS) * g


def _sigmoid(x):
    return 0.5 * jnp.tanh(0.5 * x) + 0.5


def _log_sigmoid(x):
    return -(jnp.maximum(-x, 0.0) + jnp.log1p(jnp.exp(-jnp.abs(x))))


def _rotate(x, cos2, sin2):
    return x * cos2 + pltpu.roll(x, HEAD_DIM // 2, axis=1) * sin2


def _flash_steps(qs, k_ts, vt_ts, biases, carries):
    n = len(qs)
    ss = [lax.dot_general(k_ts[u], qs[u], _NT, preferred_element_type=F32) + biases[u] for u in range(n)]
    ms, es, alphas, pvs = [], [], [], []

    def value_matmul(u):
        pv = jnp.dot(vt_ts[u], es[u], preferred_element_type=F32)
        if carries is not None:
            pv = alphas[u] * carries[2 * u + 1] + pv
        pvs.append(pv)

    for u in range(n):
        m_new = jnp.max(ss[u], axis=0, keepdims=True)
        if carries is not None:
            m_new = jnp.maximum(carries[2 * u], m_new)
            alphas.append(jnp.exp2(carries[2 * u] - m_new))
        ms.append(m_new)
        es.append(jnp.exp2(ss[u] - m_new).astype(BF16))
        if u > 0:
            value_matmul(u - 1)
    value_matmul(n - 1)
    out = []
    for u in range(n):
        out += [ms[u], pvs[u]]
    return out


def _flash_output(acc):
    return acc[:HEAD_DIM] / jnp.maximum(acc[HEAD_DIM:HEAD_DIM + 1], 1e-30)


def _store_transposed(dst_ref, x):
    n_rows, n_cols = x.shape
    for r in range(0, n_rows, LANES):
        dst_ref[:n_cols, r:r + LANES] = x[r:r + LANES, :].T.astype(dst_ref.dtype)
    dst_ref[n_cols:, :] = jnp.ones((dst_ref.shape[0] - n_cols, n_rows), dst_ref.dtype)


def _rank_before(v, n_valid, axis):
    idx = lax.broadcasted_iota(jnp.int32, v.shape, axis)
    rank = jnp.zeros(v.shape, F32)
    for m in range(n_valid):
        vm = v[m:m + 1, :] if axis == 0 else v[:, m:m + 1]
        tie = jnp.where(idx > m, 1.0, 0.0)
        rank = rank + jnp.where(vm > v, 1.0, jnp.where(vm == v, tie, 0.0))
    return rank


def _norm_matmul_body(*refs, n_chunk, with_t):
    if with_t:
        x_ref, g_ref, w_ref, wt_ref, o_ref, ot_ref = refs
    else:
        x_ref, g_ref, w_ref, o_ref = refs
    yb = _rms(x_ref[...], g_ref[...]).astype(BF16)
    n = w_ref.shape[1]
    for c0 in range(0, n, n_chunk):
        c1 = min(n, c0 + n_chunk)
        o_ref[:, c0:c1] = jnp.dot(yb, w_ref[:, c0:c1], preferred_element_type=F32)
    if with_t:
        ot_ref[...] = lax.dot_general(wt_ref[...], yb, _NT, preferred_element_type=F32)


def _norm_matmul(x2d, gain, w, wt=None, tm=512):
    t, d = x2d.shape
    n = w.shape[1]
    with_t = wt is not None
    in_specs = [pl.BlockSpec((tm, d), lambda i: (i, 0)),
                pl.BlockSpec((1, d), lambda i: (0, 0)),
                pl.BlockSpec((d, n), lambda i: (0, 0))]
    out_specs = [pl.BlockSpec((tm, n), lambda i: (i, 0))]
    out_shape = [jax.ShapeDtypeStruct((t, n), F32)]
    args = [x2d, gain.reshape(1, d), w]
    if with_t:
        r = wt.shape[0]
        in_specs.append(pl.BlockSpec((r, d), lambda i: (0, 0)))
        out_specs.append(pl.BlockSpec((r, tm), lambda i: (0, i)))
        out_shape.append(jax.ShapeDtypeStruct((r, t), F32))
        args.append(wt)
    outs = pl.pallas_call(
        functools.partial(_norm_matmul_body, n_chunk=512, with_t=with_t),
        grid=(t // tm,), in_specs=in_specs, out_specs=out_specs, out_shape=out_shape,
        compiler_params=_cparams(("parallel",)), name="norm_in_proj")(*args)
    return outs if with_t else outs[0]


def _chunk_cumsum(x, axis):
    idx = lax.broadcasted_iota(jnp.int32, x.shape, axis) % MLSTM_CHUNK
    d = 1
    while d < MLSTM_CHUNK:
        x = x + jnp.where(idx >= d, pltpu.roll(x, d, axis=axis), 0.0)
        d *= 2
    return x


def _mlstm_body(q_ref, k_ref, v_ref, og_ref, if_ref, gt_ref, cwq_ref, cwk_ref, bcol_ref, brow_ref,
                gain_ref, out_ref, ks, qts, vts, hts, css, brs, lirs, gcol):
    h = pl.program_id(1)
    s_len = q_ref.shape[1]
    n_chunks = s_len // MLSTM_CHUNK
    L = MLSTM_CHUNK
    per_slab = LANES // L
    row = lax.broadcasted_iota(jnp.int32, (s_len, HEAD_DIM), 0)

    def conv_silu(x, w):
        acc = x * w[MLSTM_CONV - 1:MLSTM_CONV, :]
        for d in range(1, MLSTM_CONV):
            shifted = jnp.where(row >= d, pltpu.roll(x, d, axis=0), 0.0)
            acc = acc + shifted * w[MLSTM_CONV - 1 - d:MLSTM_CONV - d, :]
        return acc * _sigmoid(acc)

    def store_chunks_transposed(dst, x):
        for p in range(s_len // LANES):
            slab_t = x[p * LANES:(p + 1) * LANES, :].T
            for j in range(per_slab):
                dst[per_slab * p + j] = slab_t[:, j * L:(j + 1) * L].astype(dst.dtype)

    store_chunks_transposed(qts, conv_silu(q_ref[0], cwq_ref[...]) * (HEAD_DIM ** -0.5))
    store_chunks_transposed(vts, v_ref[0])
    ks[...] = conv_silu(k_ref[0], cwk_ref[...]).astype(BF16)

    lane = lax.broadcasted_iota(jnp.int32, gcol.shape, 1)

    @pl.when(h == 0)
    def _():
        pre = if_ref[0] + bcol_ref[...]
        pre = MLSTM_GATE_CAP * jnp.tanh(pre / MLSTM_GATE_CAP)
        gcol[...] = jnp.where(lane < MLSTM_HEADS, pre, _chunk_cumsum(_log_sigmoid(pre), 0))

    g_all = gcol[...]
    css[...] = jnp.sum(jnp.where(lane == h, g_all, jnp.where(lane == h + MLSTM_HEADS, -g_all, 0.0)),
                       axis=1, keepdims=True)

    pr = gt_ref[...] + brow_ref[...]
    pr = MLSTM_GATE_CAP * jnp.tanh(pr / MLSTM_GATE_CAP)
    sub = lax.broadcasted_iota(jnp.int32, pr.shape, 0)
    b_rows = _chunk_cumsum(_log_sigmoid(pr), 1)
    li_row = jnp.sum(jnp.where(sub == h, pr, 0.0), axis=0, keepdims=True)
    b_row = jnp.sum(jnp.where(sub == h + MLSTM_HEADS, b_rows, 0.0), axis=0, keepdims=True)
    for c in range(n_chunks):
        brs[c] = b_row[:, c * L:(c + 1) * L]
        lirs[c] = li_row[:, c * L:(c + 1) * L]

    tri = lax.broadcasted_iota(jnp.int32, (L, L), 0) <= lax.broadcasted_iota(jnp.int32, (L, L), 1)
    gain = jnp.broadcast_to(gain_ref[...], (HEAD_DIM, L))

    group_size = 8
    assert n_chunks % group_size == 0

    def group(gi, carry):
        c_state, n_state, m_state = carry
        cs = [gi * group_size + j for j in range(group_size)]
        r0s = [pl.multiple_of(c * L, L) for c in cs]
        ks_ = [ks[pl.ds(r0, L), :] for r0 in r0s]
        q_ts = [qts[c] for c in cs]
        v_ts = [vts[c] for c in cs]
        b_rs = [brs[c] for c in cs]
        kvs, ksums, kqs, g_maxs, b_lasts = [], [], [], [], []
        for j, c in enumerate(cs):
            b_last = b_rs[j][:, L - 1:L]
            g = b_last - b_rs[j] + lirs[c]
            g_max = jnp.max(g, axis=1, keepdims=True)
            w = jnp.exp(g - g_max)
            kvs.append(jnp.dot((v_ts[j] * w).astype(BF16), ks_[j], preferred_element_type=F32))
            ksums.append(jnp.dot(jnp.broadcast_to(w, (BF16_SUBLANES, L)).astype(BF16), ks_[j],
                                 preferred_element_type=F32)[:1])
            kqs.append(jnp.dot(ks_[j], q_ts[j], preferred_element_type=F32))
            g_maxs.append(g_max)
            b_lasts.append(b_last)
        c_ins, n_ins, m_ins = [], [], []
        for j in range(group_size):
            c_ins.append(c_state.astype(BF16))
            n_ins.append(jnp.broadcast_to(n_state, (BF16_SUBLANES, HEAD_DIM)).astype(BF16))
            m_ins.append(m_state)
            m_new = jnp.maximum(b_lasts[j] + m_state, g_maxs[j])
            a = jnp.exp(b_lasts[j] + m_state - m_new)
            cc = jnp.exp(g_maxs[j] - m_new)
            c_state, n_state, m_state = a * c_state + cc * kvs[j], a * n_state + cc * ksums[j], m_new
        c_qs = [jnp.dot(c_ins[j], q_ts[j], preferred_element_type=F32) for j in range(group_size)]
        n_qs = [jnp.dot(n_ins[j], q_ts[j], preferred_element_type=F32)[:1] for j in range(group_size)]
        ss, m_ts, w_inters = [], [], []
        for j in range(group_size):
            dmat = jnp.where(tri, b_rs[j] + css[pl.ds(r0s[j], L), :], NEG_INF)
            inter = b_rs[j] + m_ins[j]
            m_t = jnp.maximum(inter, jnp.max(dmat, axis=0, keepdims=True))
            ss.append(kqs[j] * jnp.exp(dmat - m_t))
            m_ts.append(m_t)
            w_inters.append(jnp.exp(inter - m_t))
        svs = [jnp.dot(v_ts[j].astype(BF16), ss[j].astype(BF16), preferred_element_type=F32)
               for j in range(group_size)]
        for j, c in enumerate(cs):
            num = svs[j] + w_inters[j] * c_qs[j]
            den = jnp.sum(ss[j], axis=0, keepdims=True) + w_inters[j] * n_qs[j]
            ht = num / jnp.maximum(jnp.abs(den), jnp.exp(-m_ts[j]))
            hts[c] = ht * lax.rsqrt(jnp.mean(ht * ht, axis=0, keepdims=True) + NORM_EPS) * gain
        return c_state, n_state, m_state

    init = (jnp.zeros((HEAD_DIM, HEAD_DIM), F32), jnp.zeros((1, HEAD_DIM), F32), jnp.zeros((1, 1), F32))
    lax.fori_loop(0, n_chunks // group_size, group, init)

    for p in range(s_len // LANES):
        slab_t = jnp.concatenate([hts[per_slab * p + j] for j in range(per_slab)], axis=1)
        rows = slice(p * LANES, (p + 1) * LANES)
        out_ref[0, rows, :] = (slab_t.T * _sigmoid(og_ref[0, rows, :])).astype(out_ref.dtype)


def _mlstm(z3, gt, conv_w, gate_b, head_gain):
    b, s_len, _ = z3.shape
    n_chunks = s_len // MLSTM_CHUNK
    gr = gt.shape[0]
    bcol = jnp.zeros((1, LANES), F32).at[0, :2 * MLSTM_HEADS].set(gate_b)
    brow = jnp.zeros((gr, 1), F32).at[:2 * MLSTM_HEADS, 0].set(gate_b)

    def col(off):
        return pl.BlockSpec((1, s_len, HEAD_DIM), lambda bi, hi: (bi, 0, off + hi))

    in_specs = [col(0), col(MLSTM_HEADS), col(2 * MLSTM_HEADS), col(3 * MLSTM_HEADS),
                pl.BlockSpec((1, s_len, LANES), lambda bi, hi: (bi, 0, EVEN_IF_BLK)),
                pl.BlockSpec((gr, s_len), lambda bi, hi: (0, bi)),
                pl.BlockSpec((MLSTM_CONV, HEAD_DIM), lambda bi, hi: (0, hi)),
                pl.BlockSpec((MLSTM_CONV, HEAD_DIM), lambda bi, hi: (0, MLSTM_HEADS + hi)),
                pl.BlockSpec((1, LANES), lambda bi, hi: (0, 0)),
                pl.BlockSpec((gr, 1), lambda bi, hi: (0, 0)),
                pl.BlockSpec((HEAD_DIM, 1), lambda bi, hi: (hi, 0))]
    chunk_t = (n_chunks, HEAD_DIM, MLSTM_CHUNK)
    scratch = [pltpu.VMEM((s_len, HEAD_DIM), BF16),
               pltpu.VMEM(chunk_t, BF16), pltpu.VMEM(chunk_t, F32), pltpu.VMEM(chunk_t, F32),
               pltpu.VMEM((s_len, 1), F32),
               pltpu.VMEM((n_chunks, 1, MLSTM_CHUNK), F32), pltpu.VMEM((n_chunks, 1, MLSTM_CHUNK), F32),
               pltpu.VMEM((s_len, LANES), F32)]
    return pl.pallas_call(
        _mlstm_body, grid=(b, MLSTM_HEADS), in_specs=in_specs,
        out_specs=pl.BlockSpec((1, s_len, HEAD_DIM), lambda bi, hi: (bi, 0, hi)),
        out_shape=jax.ShapeDtypeStruct((b, s_len, MLSTM_W), BF16),
        scratch_shapes=scratch, compiler_params=_cparams(("parallel", "arbitrary")),
        name="mlstm")(z3, z3, z3, z3, z3, gt, conv_w, conv_w, bcol, brow, head_gain.reshape(MLSTM_W, 1))


def _moba_body(q_ref, k_ref, v_ref, cos_ref, sin_ref, o_ref, kr_scr, vt_scr, km_scr, sel_scr):
    i = pl.program_id(1)
    s_len = k_ref.shape[1]
    bs = MOBA_BLOCK
    nb = s_len // bs
    heads = [slice(h * HEAD_DIM, (h + 1) * HEAD_DIM) for h in range(MOBA_HEADS)]

    @pl.when(i == 0)
    def _():
        for h, cols in enumerate(heads):
            _store_transposed(vt_scr.at[h], v_ref[0, :, cols])
            kr = _rotate(k_ref[0, :, cols], cos_ref[...], sin_ref[...])
            kr_scr[:, cols] = kr.astype(BF16)
            rows = [jnp.sum(kr[n * bs:(n + 1) * bs, :], axis=0, keepdims=True) / float(bs) for n in range(nb)]
            rows.append(jnp.zeros((km_scr.shape[1] - nb, HEAD_DIM), F32))
            km_scr[h] = jnp.concatenate(rows, axis=0)

    t0 = pl.multiple_of(i * bs, bs)
    cos_q = cos_ref[pl.ds(t0, bs), :]
    sin_q = sin_ref[pl.ds(t0, bs), :]
    blk = lax.broadcasted_iota(jnp.int32, (km_scr.shape[1], bs), 0)
    causal_bias = jnp.where(lax.broadcasted_iota(jnp.int32, (bs, bs), 0)
                            <= lax.broadcasted_iota(jnp.int32, (bs, bs), 1), 0.0, NEG_INF)

    qbs = []
    for h, cols in enumerate(heads):
        qr = _rotate(q_ref[0, :, cols], cos_q, sin_q)
        gate_t = lax.dot_general(km_scr[h], qr, _NT, precision=HIGHEST, preferred_element_type=F32)
        val = jnp.where(blk < i, gate_t, NEG_INF)
        picked = jnp.where(val > 0.5 * NEG_INF, _rank_before(val, nb, 0), float(nb)) < MOBA_TOPK
        sel_scr[h] = jnp.where(picked, 0.0, NEG_INF)
        qbs.append((qr * (HEAD_DIM ** -0.5 * LOG2_E)).astype(BF16))

    def key_tiles(k0, n_blocks):
        return ([kr_scr[pl.ds(k0, n_blocks * bs), cols] for cols in heads],
                [vt_scr[h, :, pl.ds(k0, n_blocks * bs)] for h in range(MOBA_HEADS)])

    init = tuple(_flash_steps(qbs, *key_tiles(t0, 1), [causal_bias] * MOBA_HEADS, None))

    def past_blocks(j, n_blocks, carry):
        biases = [jnp.concatenate([jnp.broadcast_to(sel_scr[h, pl.ds(j + d, 1), :], (bs, bs))
                                   for d in range(n_blocks)], axis=0) for h in range(MOBA_HEADS)]
        k0 = j * bs if isinstance(j, int) else pl.multiple_of(j * bs, bs)
        return tuple(_flash_steps(qbs, *key_tiles(k0, n_blocks), biases, carry))

    odd = i % 2
    carry = lax.cond(odd == 1, lambda c: past_blocks(0, 1, c), lambda c: c, init)
    fin = lax.fori_loop(0, i // 2, lambda p, c: past_blocks(odd + 2 * p, 2, c), carry)
    for h, cols in enumerate(heads):
        o_ref[0, :, cols] = _flash_output(fin[2 * h + 1]).T.astype(o_ref.dtype)


def _moba(z3, cos2, sin2):
    b, s_len, _ = z3.shape
    nb = s_len // MOBA_BLOCK
    q_off = 4 * MLSTM_W // MOBA_W

    def kv(off):
        return pl.BlockSpec((1, s_len, MOBA_W), lambda bi, i: (bi, 0, off))

    in_specs = [pl.BlockSpec((1, MOBA_BLOCK, MOBA_W), lambda bi, i: (bi, i, q_off)),
                kv(q_off + 1), kv(q_off + 2),
                pl.BlockSpec((s_len, HEAD_DIM), lambda bi, i: (0, 0)),
                pl.BlockSpec((s_len, HEAD_DIM), lambda bi, i: (0, 0))]
    scratch = [pltpu.VMEM((s_len, MOBA_W), BF16),
               pltpu.VMEM((MOBA_HEADS, VT_ROWS, s_len), BF16),
               pltpu.VMEM((MOBA_HEADS, BF16_SUBLANES, HEAD_DIM), F32),
               pltpu.VMEM((MOBA_HEADS, BF16_SUBLANES, MOBA_BLOCK), F32)]
    return pl.pallas_call(
        _moba_body, grid=(b, nb), in_specs=in_specs,
        out_specs=pl.BlockSpec((1, MOBA_BLOCK, MOBA_W), lambda bi, i: (bi, i, 0)),
        out_shape=jax.ShapeDtypeStruct((b, s_len, MOBA_W), BF16),
        scratch_shapes=scratch, compiler_params=_cparams(("parallel", "arbitrary")),
        name="moba")(z3, z3, z3, cos2, sin2)


def _gelu_tanh(x):
    return x * (0.5 * (1.0 + jnp.tanh(np.sqrt(2.0 / np.pi) * (x + 0.044715 * (x * x * x)))))


def _nsa_compress_body(kc_ref, vc_ref, pe_ref, wk1_ref, wk2_ref, wv1_ref, wv2_ref, ko_ref, vo_ref):
    n_rows = kc_ref.shape[1] // CMP_STRIDE
    halves = CMP_LEN // CMP_STRIDE
    assert halves == 2

    def compress(x_ref, pe, w1_ref, w2_ref):
        ya = jnp.zeros((n_rows, CMP_HIDDEN), F32)
        yb = jnp.zeros((n_rows, CMP_HIDDEN), F32)
        for l in range(CMP_STRIDE):
            r = x_ref[0, pl.ds(l, n_rows, stride=CMP_STRIDE), :]
            la, lb = l, CMP_STRIDE + l
            ya = ya + jnp.dot((r + pe[la:la + 1, :]).astype(BF16), w1_ref[la * HEAD_DIM:(la + 1) * HEAD_DIM, :],
                              preferred_element_type=F32)
            yb = yb + jnp.dot((r + pe[lb:lb + 1, :]).astype(BF16), w1_ref[lb * HEAD_DIM:(lb + 1) * HEAD_DIM, :],
                              preferred_element_type=F32)
        pre = ya + pltpu.roll(yb, n_rows - 1, axis=0)
        return jnp.dot(_gelu_tanh(pre).astype(BF16), w2_ref[...], preferred_element_type=F32)

    ko_ref[0, 0] = compress(kc_ref, pe_ref[0], wk1_ref, wk2_ref)
    vo_ref[0, 0] = compress(vc_ref, pe_ref[1], wv1_ref, wv2_ref)


def _nsa_compress(z3, cmp_pos, wk1, wk2, wv1, wv2):
    b, s_len, _ = z3.shape
    n_rows = s_len // CMP_STRIDE
    kc_off = NSA_W // LANES
    vc_off = kc_off + NSA_GROUPS
    const2 = lambda bi, gi: (0, 0)
    in_specs = [pl.BlockSpec((1, s_len, HEAD_DIM), lambda bi, gi: (bi, 0, kc_off + gi)),
                pl.BlockSpec((1, s_len, HEAD_DIM), lambda bi, gi: (bi, 0, vc_off + gi)),
                pl.BlockSpec(cmp_pos.shape, lambda bi, gi: (0, 0, 0)),
                pl.BlockSpec(wk1.shape, const2), pl.BlockSpec(wk2.shape, const2),
                pl.BlockSpec(wv1.shape, const2), pl.BlockSpec(wv2.shape, const2)]
    out_spec = pl.BlockSpec((1, 1, n_rows, HEAD_DIM), lambda bi, gi: (bi, gi, 0, 0))
    out_sds = jax.ShapeDtypeStruct((b, NSA_GROUPS, n_rows, HEAD_DIM), F32)
    return pl.pallas_call(
        _nsa_compress_body, grid=(b, NSA_GROUPS), in_specs=in_specs,
        out_specs=[out_spec, out_spec], out_shape=[out_sds, out_sds],
        compiler_params=_cparams(("parallel", "parallel")), name="nsa_compress")(
            z3, z3, cmp_pos, wk1, wk2, wv1, wv2)


def _nsa_body(q_ref, ks_ref, vs_ref, kw_ref, vw_ref, kc_ref, vc_ref, gate_ref, cos_ref, sin_ref, ov_ref,
              o_ref, krs, vst, krw, vwt, sel_scr):
    i = pl.program_id(1)
    tq = NSA_TQ
    tk = NSA_TQ
    scale = HEAD_DIM ** -0.5
    n_sel_blk = ks_ref.shape[1] // SEL_BLOCK
    n_cmp = (ks_ref.shape[1] - CMP_LEN) // CMP_STRIDE + 1
    shift = SEL_BLOCK.bit_length() - 1
    assert 1 << shift == SEL_BLOCK and n_sel_blk <= LANES
    blk_per_tile = tk // SEL_BLOCK
    groups = [slice(g * HEAD_DIM, (g + 1) * HEAD_DIM) for g in range(NSA_GROUPS)]

    @pl.when(i == 0)
    def _():
        for g, cols in enumerate(groups):
            krs[:, cols] = _rotate(ks_ref[0, :, cols], cos_ref[...], sin_ref[...]).astype(BF16)
            krw[:, cols] = _rotate(kw_ref[0, :, cols], cos_ref[...], sin_ref[...]).astype(BF16)
            _store_transposed(vst.at[g], vs_ref[0, :, cols])
            _store_transposed(vwt.at[g], vw_ref[0, :, cols])

    t0 = pl.multiple_of(i * tq, tq)
    pos_row = t0 + lax.broadcasted_iota(jnp.int32, (1, tq), 1)
    cos_q = cos_ref[pl.ds(t0, tq), :]
    sin_q = sin_ref[pl.ds(t0, tq), :]

    n_col = lax.broadcasted_iota(jnp.int32, (LANES, 1), 0)
    cmp_end = jnp.where(n_col < n_cmp, n_col * CMP_STRIDE + (CMP_LEN - 1), jnp.iinfo(jnp.int32).max)
    cmp_ok = cmp_end <= pos_row
    blk_i = lax.broadcasted_iota(jnp.int32, (n_sel_blk, tq), 0)
    behind = (pos_row >> shift) - blk_i
    units = [(g, r) for g in range(NSA_GROUPS) for r in range(NSA_REP)]
    qr_heads, o_cmp = [], []
    for g in range(NSA_GROUPS):
        kc = kc_ref[0, g].astype(BF16)
        vc = vc_ref[0, g].astype(BF16)
        p_sum = jnp.zeros((LANES, tq), F32)
        for r in range(NSA_REP):
            head = (g * NSA_REP + r) * HEAD_DIM
            q = q_ref[0, :, head:head + HEAD_DIM] * scale
            qr_heads.append(_rotate(q * LOG2_E, cos_q, sin_q).astype(BF16))
            s = lax.dot_general(kc, q.astype(BF16), _NT, preferred_element_type=F32)
            s = jnp.where(cmp_ok, s, NEG_INF)
            e = jnp.where(cmp_ok, jnp.exp(s - jnp.max(s, axis=0, keepdims=True)), 0.0)
            p = e / jnp.maximum(jnp.sum(e, axis=0, keepdims=True), 1e-30)
            o_cmp.append(lax.dot_general(p.astype(BF16), vc, _TN, preferred_element_type=F32))
            p_sum = p_sum + p
        imp = jnp.dot(ov_ref[...], p_sum, precision=HIGHEST, preferred_element_type=F32)
        val = jnp.where(behind == 0, FORCED_SCORE, jnp.where(behind == 1, FORCED_SCORE, imp))
        val = jnp.where(blk_i == 0, FORCED_SCORE, val)
        val = jnp.where(behind >= 0, val, NEG_INF)
        picked = jnp.where(val > 0.5 * NEG_INF, _rank_before(val, n_sel_blk, 0), float(n_sel_blk)) < SEL_TOPN
        sel_scr[g] = jnp.where(picked, 0.0, NEG_INF)

    def sel_bias(g, k0, n_tiles):
        blk0 = (k0 >> shift) if isinstance(k0, int) else pl.multiple_of(k0 >> shift, blk_per_tile)
        return jnp.concatenate([jnp.broadcast_to(sel_scr[g, pl.ds(blk0 + b, 1), :], (SEL_BLOCK, tq))
                                for b in range(n_tiles * blk_per_tile)], axis=0)

    def sel_tiles(k0, n_tiles, carry):
        biases = [sel_bias(g, k0, n_tiles) for g in range(NSA_GROUPS)]
        return tuple(_flash_steps(qr_heads, [krs[pl.ds(k0, n_tiles * tk), groups[g]] for g, _ in units],
                                  [vst[g, :, pl.ds(k0, n_tiles * tk)] for g, _ in units],
                                  [biases[g] for g, _ in units], carry))

    causal_bias = jnp.where(lax.broadcasted_iota(jnp.int32, (tk, tq), 0)
                            <= lax.broadcasted_iota(jnp.int32, (tk, tq), 1), 0.0, NEG_INF)
    span = WINDOW + tq
    start = pl.multiple_of(jnp.maximum(t0 - WINDOW, 0), tq)
    gap = ((t0 - start) + lax.broadcasted_iota(jnp.int32, (span, tq), 1)
           - lax.broadcasted_iota(jnp.int32, (span, tq), 0))
    win_bias = jnp.where(gap >= 0, jnp.where(gap < WINDOW, 0.0, NEG_INF), NEG_INF)
    diag_biases = [sel_bias(g, t0, 1) + causal_bias for g in range(NSA_GROUPS)]
    n = len(units)
    group = _flash_steps(qr_heads * 2,
                         [krs[pl.ds(t0, tk), groups[g]] for g, _ in units]
                         + [krw[pl.ds(start, span), groups[g]] for g, _ in units],
                         [vst[g, :, pl.ds(t0, tk)] for g, _ in units]
                         + [vwt[g, :, pl.ds(start, span)] for g, _ in units],
                         [diag_biases[g] for g, _ in units] + [win_bias] * n, None)
    win_fin = group[2 * n:]

    odd = i % 2
    carry = lax.cond(odd == 1, lambda c: sel_tiles(0, 1, c), lambda c: c, tuple(group[:2 * n]))
    sel_fin = lax.fori_loop(0, i // 2, lambda p, c: sel_tiles(pl.multiple_of((odd + 2 * p) * tk, tk), 2, c), carry)

    gates = _sigmoid(gate_ref[0])
    gates_t = [gates[:, cols].T for cols in groups]
    for u, (g, r) in enumerate(units):
        o_t = (gates_t[g][3 * r + 1:3 * r + 2, :] * _flash_output(sel_fin[2 * u + 1])
               + gates_t[g][3 * r + 2:3 * r + 3, :] * _flash_output(win_fin[2 * u + 1]))
        g_cmp = gates[:, g * LANES + 3 * r:g * LANES + 3 * r + 1]
        o_ref[0, :, u * HEAD_DIM:(u + 1) * HEAD_DIM] = (g_cmp * o_cmp[u] + o_t.T).astype(o_ref.dtype)


def _nsa_attention(z3, k_cmp, v_cmp, cos2, sin2):
    b, s_len, _ = z3.shape
    assert s_len >= WINDOW + NSA_TQ and s_len % NSA_TQ == 0
    n_cmp = (s_len - CMP_LEN) // CMP_STRIDE + 1
    n_blk = s_len // SEL_BLOCK
    n_rows = k_cmp.shape[2]
    assert n_rows == LANES
    cmp_start = np.arange(n_rows) * CMP_STRIDE
    cmp_end = cmp_start + CMP_LEN - 1
    blk_lo = np.arange(LANES) * SEL_BLOCK
    overlap = ((cmp_start[:, None] <= blk_lo[None, :] + SEL_BLOCK - 1) & (cmp_end[:, None] >= blk_lo[None, :])
               & (np.arange(n_rows)[:, None] < n_cmp)).astype(np.float32)
    overlap_t = np.ascontiguousarray(overlap.T[:n_blk])
    base = NSA_W // NSA_KV_W

    def kv(off):
        return pl.BlockSpec((1, s_len, NSA_KV_W), lambda bi, i: (bi, 0, base + off))

    cmp_spec = pl.BlockSpec((1, NSA_GROUPS, n_rows, HEAD_DIM), lambda bi, i: (bi, 0, 0, 0))
    const2 = lambda bi, i: (0, 0)
    gate_w = NSA_GROUPS * LANES
    in_specs = [pl.BlockSpec((1, NSA_TQ, NSA_W), lambda bi, i: (bi, i, 0)),
                kv(2), kv(3), kv(4), kv(5), cmp_spec, cmp_spec,
                pl.BlockSpec((1, NSA_TQ, gate_w), lambda bi, i: (bi, i, ODD_GATE_BLK * LANES // gate_w)),
                pl.BlockSpec((s_len, HEAD_DIM), const2), pl.BlockSpec((s_len, HEAD_DIM), const2),
                pl.BlockSpec((n_blk, n_rows), const2)]
    vt_shape = (NSA_GROUPS, VT_ROWS, s_len)
    scratch = [pltpu.VMEM((s_len, NSA_KV_W), BF16), pltpu.VMEM(vt_shape, BF16),
               pltpu.VMEM((s_len, NSA_KV_W), BF16), pltpu.VMEM(vt_shape, BF16),
               pltpu.VMEM((NSA_GROUPS, n_blk, NSA_TQ), F32)]
    assert (ODD_GATE_BLK * LANES) % gate_w == 0
    return pl.pallas_call(
        _nsa_body, grid=(b, s_len // NSA_TQ), in_specs=in_specs,
        out_specs=pl.BlockSpec((1, NSA_TQ, NSA_W), lambda bi, i: (bi, i, 0)),
        out_shape=jax.ShapeDtypeStruct((b, s_len, NSA_W), BF16),
        scratch_shapes=scratch, compiler_params=_cparams(("parallel", "arbitrary")),
        name="nsa_attention")(z3, z3, z3, z3, z3, k_cmp, v_cmp, z3, cos2, sin2, jnp.asarray(overlap_t))


def _split_bf16(x):
    hi = x.astype(BF16)
    return hi, (x - hi.astype(F32)).astype(BF16)


def _proj_moe_body(*refs, n_act, final_norm):
    res_ref = refs[0]
    a_refs = refs[1:1 + n_act]
    wo_refs = refs[1 + n_act:1 + 2 * n_act]
    (g_ref, wr_ref, br_ref, w1_ref, w3_ref, w2_ref, fg_ref, o_ref,
     xn_scr, cw_scr, hd_scr) = refs[1 + 2 * n_act:]
    j = pl.program_id(1)
    tm = res_ref.shape[0]
    lane = lax.broadcasted_iota(jnp.int32, (tm, LANES), 1).astype(F32)
    neg = float("-inf")

    @pl.when(j == 0)
    def _():
        h = res_ref[...]
        for a_ref, wo_ref in zip(a_refs, wo_refs):
            h = h + jnp.dot(a_ref[...], wo_ref[...], preferred_element_type=F32)
        o_ref[...] = h
        xn = _rms(h, g_ref[...])
        xh, xl = _split_bf16(xn)
        xn_scr[...] = xh
        hi_terms = jnp.dot(xh, wr_ref[...], preferred_element_type=F32)
        logits = (hi_terms[:, :LANES] + (jnp.dot(xl, wr_ref[:, :LANES], preferred_element_type=F32)
                                         + hi_terms[:, LANES:])) + br_ref[...]
        is_g = lane < MOE_GROUPS
        gl = jnp.where(is_g, logits, neg)
        g_max = jnp.max(gl, axis=1, keepdims=True)
        g_w = 1.0 / jnp.sum(jnp.where(is_g, jnp.exp(gl - g_max), 0.0), axis=1, keepdims=True)
        g_top = jnp.min(jnp.where(gl == g_max, lane, float(LANES)), axis=1, keepdims=True)
        lo = MOE_GROUPS + MOE_EPG * g_top
        el = jnp.where(lane >= lo, jnp.where(lane < lo + MOE_EPG, logits, neg), neg)
        v1 = jnp.max(el, axis=1, keepdims=True)
        i1 = jnp.min(jnp.where(el == v1, lane, float(LANES)), axis=1, keepdims=True)
        el2 = jnp.where(lane == i1, neg, el)
        v2 = jnp.max(el2, axis=1, keepdims=True)
        i2 = jnp.min(jnp.where(el2 == v2, lane, float(LANES)), axis=1, keepdims=True)
        e2 = jnp.exp(v2 - v1)
        den = 1.0 + e2
        cw_scr[...] = (jnp.where(lane == i1, g_w / den, 0.0) + jnp.where(lane == i2, g_w * e2 / den, 0.0))

    xn = xn_scr[...]
    cw = cw_scr[...]
    for r in range(MOE_EPG):
        h1 = jnp.dot(xn, w1_ref[r], preferred_element_type=F32)
        h3 = jnp.dot(xn, w3_ref[r], preferred_element_type=F32)
        e_lane = (MOE_GROUPS + MOE_EPG * j + r).astype(F32)
        col = jnp.sum(jnp.where(lane == e_lane, cw, 0.0), axis=1, keepdims=True)
        hd_scr[:, r * MOE_HIDDEN:(r + 1) * MOE_HIDDEN] = ((h1 * _sigmoid(h1)) * h3 * col).astype(BF16)
    o_ref[...] += jnp.dot(hd_scr[...], w2_ref[...], preferred_element_type=F32)

    if final_norm:
        @pl.when(j == pl.num_programs(1) - 1)
        def _():
            o_ref[...] = _rms(o_ref[...], fg_ref[...])


def _proj_moe(res2d, acts, w_outs, gain, w_g, b_g, w_e, b_e, w1, w3, w2, final_gain, final_norm, tm=1024):
    t, d = res2d.shape
    n_act = len(acts)
    wr = jnp.zeros((d, LANES), F32).at[:, :MOE_GROUPS].set(w_g).at[:, MOE_GROUPS:MOE_GROUPS + MOE_EXPERTS].set(w_e)
    wr_cat = jnp.concatenate(_split_bf16(wr), axis=1)
    br = jnp.zeros((1, LANES), F32).at[0, :MOE_GROUPS].set(b_g).at[0, MOE_GROUPS:MOE_GROUPS + MOE_EXPERTS].set(b_e)
    tile = lambda i, j: (i, 0)
    const = lambda i, j: (0, 0)
    group_hidden = MOE_EPG * MOE_HIDDEN
    in_specs = [pl.BlockSpec((tm, d), tile)]
    in_specs += [pl.BlockSpec((tm, a.shape[1]), tile) for a in acts]
    in_specs += [pl.BlockSpec(w.shape, const) for w in w_outs]
    in_specs += [pl.BlockSpec((1, d), const),
                 pl.BlockSpec((d, 2 * LANES), const),
                 pl.BlockSpec((1, LANES), const),
                 pl.BlockSpec((MOE_EPG, d, MOE_HIDDEN), lambda i, j: (j, 0, 0)),
                 pl.BlockSpec((MOE_EPG, d, MOE_HIDDEN), lambda i, j: (j, 0, 0)),
                 pl.BlockSpec((group_hidden, d), lambda i, j: (j, 0)),
                 pl.BlockSpec((1, d), const)]
    scratch = [pltpu.VMEM((tm, d), BF16), pltpu.VMEM((tm, LANES), F32), pltpu.VMEM((tm, group_hidden), BF16)]
    return pl.pallas_call(
        functools.partial(_proj_moe_body, n_act=n_act, final_norm=final_norm),
        grid=(t // tm, MOE_GROUPS), in_specs=in_specs,
        out_specs=pl.BlockSpec((tm, d), tile),
        out_shape=jax.ShapeDtypeStruct((t, d), F32),
        scratch_shapes=scratch, compiler_params=_cparams(("parallel", "arbitrary")),
        name="out_proj_moe")(res2d, *acts, *w_outs, gain.reshape(1, d), wr_cat, br,
                             w1.astype(BF16), w3.astype(BF16), w2.astype(BF16).reshape(MOE_EXPERTS * MOE_HIDDEN, d),
                             final_gain.reshape(1, d))


def _rope_tables(s_len):
    half = HEAD_DIM // 2
    inv = ROPE_THETA ** (-(jnp.arange(half, dtype=F32) / half))
    ang = jnp.arange(s_len, dtype=F32)[:, None] * inv[None, :]
    cos, sin = jnp.cos(ang), jnp.sin(ang)
    return jnp.concatenate([cos, cos], axis=-1), jnp.concatenate([-sin, sin], axis=-1)


def _even_weights(w_in):
    n_if = 2 * MLSTM_HEADS
    a = 4 * MLSTM_W
    w_main = jnp.concatenate([w_in[:, :a], w_in[:, a + n_if:],
                              w_in[:, a:a + n_if], jnp.zeros((w_in.shape[0], LANES - n_if), w_in.dtype)], axis=1)
    w_if_t = jnp.concatenate([w_in[:, a:a + n_if].T, jnp.zeros((16 - n_if, w_in.shape[0]), w_in.dtype)], axis=0)
    return w_main.astype(BF16), w_if_t.astype(BF16)


def _odd_weights(w_in):
    a = NSA_W + 6 * NSA_KV_W
    per_group = NSA_REP * 3
    pad = jnp.zeros((w_in.shape[0], LANES - per_group), w_in.dtype)
    cols = [w_in[:, :a]]
    for g in range(NSA_GROUPS):
        cols += [w_in[:, a + g * per_group:a + (g + 1) * per_group], pad]
    return jnp.concatenate(cols, axis=1).astype(BF16)


def kernel(x, mix_norm_0, w_in_0, mlstm_conv_0, mlstm_gate_b_0, mlstm_head_norm_0, w_out_0, ffn_norm_0, router_group_0, router_group_b_0, router_expert_0, router_expert_b_0, moe_w1_0, moe_w3_0, moe_w2_0, mix_norm_1, w_in_1, nsa_cmp_pos_1, nsa_cmp_k1_1, nsa_cmp_k2_1, nsa_cmp_v1_1, nsa_cmp_v2_1, w_out_1, ffn_norm_1, router_group_1, router_group_b_1, router_expert_1, router_expert_b_1, moe_w1_1, moe_w3_1, moe_w2_1, final_norm):
    b, s_len, d = x.shape
    t = b * s_len
    cos2, sin2 = _rope_tables(s_len)
    x2d = x.reshape(t, d)

    w_main, w_if_t = _even_weights(w_in_0)
    z0, gt = _norm_matmul(x2d, mix_norm_0, w_main, w_if_t)
    z0 = z0.reshape(b, s_len, EVEN_N)
    h_m = _mlstm(z0, gt, mlstm_conv_0, mlstm_gate_b_0, mlstm_head_norm_0)
    o_b = _moba(z0, cos2, sin2)
    w_out = w_out_0.astype(BF16)
    h = _proj_moe(x2d, [h_m.reshape(t, MLSTM_W), o_b.reshape(t, MOBA_W)], [w_out[:MLSTM_W], w_out[MLSTM_W:]],
                  ffn_norm_0, router_group_0, router_group_b_0, router_expert_0, router_expert_b_0,
                  moe_w1_0, moe_w3_0, moe_w2_0, final_norm, False)

    z1 = _norm_matmul(h, mix_norm_1, _odd_weights(w_in_1)).reshape(b, s_len, ODD_N)
    k_cmp, v_cmp = _nsa_compress(z1, nsa_cmp_pos_1, nsa_cmp_k1_1.astype(BF16), nsa_cmp_k2_1.astype(BF16),
                                 nsa_cmp_v1_1.astype(BF16), nsa_cmp_v2_1.astype(BF16))
    o = _nsa_attention(z1, k_cmp, v_cmp, cos2, sin2)
    h = _proj_moe(h, [o.reshape(t, NSA_W)], [w_out_1.astype(BF16)],
                  ffn_norm_1, router_group_1, router_group_b_1, router_expert_1, router_expert_b_1,
                  moe_w1_1, moe_w3_1, moe_w2_1, final_norm, True)
    return h.reshape(b, s_len, d)
```

```python
import functools

import numpy as np
import jax
import jax.numpy as jnp
from jax import lax
from jax.experimental import pallas as pl
from jax.experimental.pallas import tpu as pltpu

F32 = jnp.float32
BF16 = jnp.bfloat16
HIGHEST = lax.Precision.HIGHEST

LANES = 128
BF16_SUBLANES = 16
D_MODEL = 1024
HEAD_DIM = 128
ROPE_THETA = 10000.0
NORM_EPS = 1e-6
NEG_INF = -1e30
FORCED_SCORE = 1e4

MLSTM_HEADS = 4
MLSTM_W = MLSTM_HEADS * HEAD_DIM
MLSTM_CHUNK = 64
MLSTM_CONV = 4
MLSTM_GATE_CAP = 15.0
MOBA_HEADS = 4
MOBA_W = MOBA_HEADS * HEAD_DIM
MOBA_BLOCK = 256
MOBA_TOPK = 3

NSA_HEADS = 8
NSA_GROUPS = 2
NSA_REP = NSA_HEADS // NSA_GROUPS
NSA_W = NSA_HEADS * HEAD_DIM
NSA_KV_W = NSA_GROUPS * HEAD_DIM
CMP_LEN = 32
CMP_STRIDE = 16
CMP_HIDDEN = 256
SEL_BLOCK = 64
SEL_TOPN = 8
WINDOW = 512
NSA_TQ = 256

MOE_GROUPS = 4
MOE_EPG = 4
MOE_EXPERTS = MOE_GROUPS * MOE_EPG
MOE_HIDDEN = D_MODEL // 4

EVEN_N = 4 * MLSTM_W + 3 * MOBA_W + LANES
EVEN_IF_BLK = (4 * MLSTM_W + 3 * MOBA_W) // LANES
ODD_GATE_BLK = (NSA_W + 6 * NSA_KV_W) // LANES
ODD_N = NSA_W + 6 * NSA_KV_W + NSA_GROUPS * LANES

VMEM_LIMIT = 56 * 1024 * 1024
VT_ROWS = HEAD_DIM + BF16_SUBLANES
LOG2_E = float(np.log2(np.e))

_NT = (((1,), (1,)), ((), ()))
_TN = (((0,), (0,)), ((), ()))


def _cparams(sem):
    return pltpu.CompilerParams(dimension_semantics=sem, vmem_limit_bytes=VMEM_LIMIT)


def _rms(x, g):
    return x * lax.rsqrt(jnp.mean(x * x, axis=-1, keepdims=True) + NORM_EPS) * g


def _sigmoid(x):
    return 0.5 * jnp.tanh(0.5 * x) + 0.5


def _log_sigmoid(x):
    return -(jnp.maximum(-x, 0.0) + jnp.log1p(jnp.exp(-jnp.abs(x))))


def _rotate(x, cos2, sin2):
    return x * cos2 + pltpu.roll(x, HEAD_DIM // 2, axis=1) * sin2


def _flash_steps(qs, k_ts, vt_ts, biases, carries):
    n = len(qs)
    ss = [lax.dot_general(k_ts[u], qs[u], _NT, preferred_element_type=F32) + biases[u] for u in range(n)]
    ms, es, alphas, pvs = [], [], [], []

    def value_matmul(u):
        pv = jnp.dot(vt_ts[u], es[u], preferred_element_type=F32)
        if carries is not None:
            pv = alphas[u] * carries[2 * u + 1] + pv
        pvs.append(pv)

    for u in range(n):
        m_new = jnp.max(ss[u], axis=0, keepdims=True)
        if carries is not None:
            m_new = jnp.maximum(carries[2 * u], m_new)
            alphas.append(jnp.exp2(carries[2 * u] - m_new))
        ms.append(m_new)
        es.append(jnp.exp2(ss[u] - m_new).astype(BF16))
        if u > 0:
            value_matmul(u - 1)
    value_matmul(n - 1)
    out = []
    for u in range(n):
        out += [ms[u], pvs[u]]
    return out


def _flash_output(acc):
    return acc[:HEAD_DIM] / jnp.maximum(acc[HEAD_DIM:HEAD_DIM + 1], 1e-30)


def _store_transposed(dst_ref, x):
    n_rows, n_cols = x.shape
    for r in range(0, n_rows, LANES):
        dst_ref[:n_cols, r:r + LANES] = x[r:r + LANES, :].T.astype(dst_ref.dtype)
    dst_ref[n_cols:, :] = jnp.ones((dst_ref.shape[0] - n_cols, n_rows), dst_ref.dtype)


def _rank_before(v, n_valid, axis):
    idx = lax.broadcasted_iota(jnp.int32, v.shape, axis)
    rank = jnp.zeros(v.shape, F32)
    for m in range(n_valid):
        vm = v[m:m + 1, :] if axis == 0 else v[:, m:m + 1]
        tie = jnp.where(idx > m, 1.0, 0.0)
        rank = rank + jnp.where(vm > v, 1.0, jnp.where(vm == v, tie, 0.0))
    return rank


def _norm_matmul_body(*refs, n_chunk, with_t):
    if with_t:
        x_ref, g_ref, w_ref, wt_ref, o_ref, ot_ref = refs
    else:
        x_ref, g_ref, w_ref, o_ref = refs
    yb = _rms(x_ref[...], g_ref[...]).astype(BF16)
    n = w_ref.shape[1]
    for c0 in range(0, n, n_chunk):
        c1 = min(n, c0 + n_chunk)
        o_ref[:, c0:c1] = jnp.dot(yb, w_ref[:, c0:c1], preferred_element_type=F32)
    if with_t:
        ot_ref[...] = lax.dot_general(wt_ref[...], yb, _NT, preferred_element_type=F32)


def _norm_matmul(x2d, gain, w, wt=None, tm=512):
    t, d = x2d.shape
    n = w.shape[1]
    with_t = wt is not None
    in_specs = [pl.BlockSpec((tm, d), lambda i: (i, 0)),
                pl.BlockSpec((1, d), lambda i: (0, 0)),
                pl.BlockSpec((d, n), lambda i: (0, 0))]
    out_specs = [pl.BlockSpec((tm, n), lambda i: (i, 0))]
    out_shape = [jax.ShapeDtypeStruct((t, n), F32)]
    args = [x2d, gain.reshape(1, d), w]
    if with_t:
        r = wt.shape[0]
        in_specs.append(pl.BlockSpec((r, d), lambda i: (0, 0)))
        out_specs.append(pl.BlockSpec((r, tm), lambda i: (0, i)))
        out_shape.append(jax.ShapeDtypeStruct((r, t), F32))
        args.append(wt)
    outs = pl.pallas_call(
        functools.partial(_norm_matmul_body, n_chunk=512, with_t=with_t),
        grid=(t // tm,), in_specs=in_specs, out_specs=out_specs, out_shape=out_shape,
        compiler_params=_cparams(("parallel",)), name="norm_in_proj")(*args)
    return outs if with_t else outs[0]


def _chunk_cumsum(x, axis):
    idx = lax.broadcasted_iota(jnp.int32, x.shape, axis) % MLSTM_CHUNK
    d = 1
    while d < MLSTM_CHUNK:
        x = x + jnp.where(idx >= d, pltpu.roll(x, d, axis=axis), 0.0)
        d *= 2
    return x


def _mlstm_body(q_ref, k_ref, v_ref, og_ref, if_ref, gt_ref, cwq_ref, cwk_ref, bcol_ref, brow_ref,
                gain_ref, out_ref, ks, qts, vts, hts, css, brs, lirs, gcol):
    h = pl.program_id(1)
    s_len = q_ref.shape[1]
    n_chunks = s_len // MLSTM_CHUNK
    L = MLSTM_CHUNK
    per_slab = LANES // L
    row = lax.broadcasted_iota(jnp.int32, (s_len, HEAD_DIM), 0)

    def conv_silu(x, w):
        acc = x * w[MLSTM_CONV - 1:MLSTM_CONV, :]
        for d in range(1, MLSTM_CONV):
            shifted = jnp.where(row >= d, pltpu.roll(x, d, axis=0), 0.0)
            acc = acc + shifted * w[MLSTM_CONV - 1 - d:MLSTM_CONV - d, :]
        return acc * _sigmoid(acc)

    def store_chunks_transposed(dst, x):
        for p in range(s_len // LANES):
            slab_t = x[p * LANES:(p + 1) * LANES, :].T
            for j in range(per_slab):
                dst[per_slab * p + j] = slab_t[:, j * L:(j + 1) * L].astype(dst.dtype)

    store_chunks_transposed(qts, conv_silu(q_ref[0], cwq_ref[...]) * (HEAD_DIM ** -0.5))
    store_chunks_transposed(vts, v_ref[0])
    ks[...] = conv_silu(k_ref[0], cwk_ref[...]).astype(BF16)

    lane = lax.broadcasted_iota(jnp.int32, gcol.shape, 1)

    @pl.when(h == 0)
    def _():
        pre = if_ref[0] + bcol_ref[...]
        pre = MLSTM_GATE_CAP * jnp.tanh(pre / MLSTM_GATE_CAP)
        gcol[...] = jnp.where(lane < MLSTM_HEADS, pre, _chunk_cumsum(_log_sigmoid(pre), 0))

    g_all = gcol[...]
    css[...] = jnp.sum(jnp.where(lane == h, g_all, jnp.where(lane == h + MLSTM_HEADS, -g_all, 0.0)),
                       axis=1, keepdims=True)

    pr = gt_ref[...] + brow_ref[...]
    pr = MLSTM_GATE_CAP * jnp.tanh(pr / MLSTM_GATE_CAP)
    sub = lax.broadcasted_iota(jnp.int32, pr.shape, 0)
    b_rows = _chunk_cumsum(_log_sigmoid(pr), 1)
    li_row = jnp.sum(jnp.where(sub == h, pr, 0.0), axis=0, keepdims=True)
    b_row = jnp.sum(jnp.where(sub == h + MLSTM_HEADS, b_rows, 0.0), axis=0, keepdims=True)
    for c in range(n_chunks):
        brs[c] = b_row[:, c * L:(c + 1) * L]
        lirs[c] = li_row[:, c * L:(c + 1) * L]

    tri = lax.broadcasted_iota(jnp.int32, (L, L), 0) <= lax.broadcasted_iota(jnp.int32, (L, L), 1)
    gain = jnp.broadcast_to(gain_ref[...], (HEAD_DIM, L))

    group_size = 8
    assert n_chunks % group_size == 0

    def group(gi, carry):
        c_state, n_state, m_state = carry
        cs = [gi * group_size + j for j in range(group_size)]
        r0s = [pl.multiple_of(c * L, L) for c in cs]
        ks_ = [ks[pl.ds(r0, L), :] for r0 in r0s]
        q_ts = [qts[c] for c in cs]
        v_ts = [vts[c] for c in cs]
        b_rs = [brs[c] for c in cs]
        kvs, ksums, kqs, g_maxs, b_lasts = [], [], [], [], []
        for j, c in enumerate(cs):
            b_last = b_rs[j][:, L - 1:L]
            g = b_last - b_rs[j] + lirs[c]
            g_max = jnp.max(g, axis=1, keepdims=True)
            w = jnp.exp(g - g_max)
            kvs.append(jnp.dot((v_ts[j] * w).astype(BF16), ks_[j], preferred_element_type=F32))
            ksums.append(jnp.dot(jnp.broadcast_to(w, (BF16_SUBLANES, L)).astype(BF16), ks_[j],
                                 preferred_element_type=F32)[:1])
            kqs.append(jnp.dot(ks_[j], q_ts[j], preferred_element_type=F32))
            g_maxs.append(g_max)
            b_lasts.append(b_last)
        c_ins, n_ins, m_ins = [], [], []
        for j in range(group_size):
            c_ins.append(c_state.astype(BF16))
            n_ins.append(jnp.broadcast_to(n_state, (BF16_SUBLANES, HEAD_DIM)).astype(BF16))
            m_ins.append(m_state)
            m_new = jnp.maximum(b_lasts[j] + m_state, g_maxs[j])
            a = jnp.exp(b_lasts[j] + m_state - m_new)
            cc = jnp.exp(g_maxs[j] - m_new)
            c_state, n_state, m_state = a * c_state + cc * kvs[j], a * n_state + cc * ksums[j], m_new
        c_qs = [jnp.dot(c_ins[j], q_ts[j], preferred_element_type=F32) for j in range(group_size)]
        n_qs = [jnp.dot(n_ins[j], q_ts[j], preferred_element_type=F32)[:1] for j in range(group_size)]
        ss, m_ts, w_inters = [], [], []
        for j in range(group_size):
            dmat = jnp.where(tri, b_rs[j] + css[pl.ds(r0s[j], L), :], NEG_INF)
            inter = b_rs[j] + m_ins[j]
            m_t = jnp.maximum(inter, jnp.max(dmat, axis=0, keepdims=True))
            ss.append(kqs[j] * jnp.exp(dmat - m_t))
            m_ts.append(m_t)
            w_inters.append(jnp.exp(inter - m_t))
        svs = [jnp.dot(v_ts[j].astype(BF16), ss[j].astype(BF16), preferred_element_type=F32)
               for j in range(group_size)]
        for j, c in enumerate(cs):
            num = svs[j] + w_inters[j] * c_qs[j]
            den = jnp.sum(ss[j], axis=0, keepdims=True) + w_inters[j] * n_qs[j]
            ht = num / jnp.maximum(jnp.abs(den), jnp.exp(-m_ts[j]))
            hts[c] = ht * lax.rsqrt(jnp.mean(ht * ht, axis=0, keepdims=True) + NORM_EPS) * gain
        return c_state, n_state, m_state

    init = (jnp.zeros((HEAD_DIM, HEAD_DIM), F32), jnp.zeros((1, HEAD_DIM), F32), jnp.zeros((1, 1), F32))
    lax.fori_loop(0, n_chunks // group_size, group, init)

    for p in range(s_len // LANES):
        slab_t = jnp.concatenate([hts[per_slab * p + j] for j in range(per_slab)], axis=1)
        rows = slice(p * LANES, (p + 1) * LANES)
        out_ref[0, rows, :] = (slab_t.T * _sigmoid(og_ref[0, rows, :])).astype(out_ref.dtype)


def _mlstm(z3, gt, conv_w, gate_b, head_gain):
    b, s_len, _ = z3.shape
    n_chunks = s_len // MLSTM_CHUNK
    gr = gt.shape[0]
    bcol = jnp.zeros((1, LANES), F32).at[0, :2 * MLSTM_HEADS].set(gate_b)
    brow = jnp.zeros((gr, 1), F32).at[:2 * MLSTM_HEADS, 0].set(gate_b)

    def col(off):
        return pl.BlockSpec((1, s_len, HEAD_DIM), lambda bi, hi: (bi, 0, off + hi))

    in_specs = [col(0), col(MLSTM_HEADS), col(2 * MLSTM_HEADS), col(3 * MLSTM_HEADS),
                pl.BlockSpec((1, s_len, LANES), lambda bi, hi: (bi, 0, EVEN_IF_BLK)),
                pl.BlockSpec((gr, s_len), lambda bi, hi: (0, bi)),
                pl.BlockSpec((MLSTM_CONV, HEAD_DIM), lambda bi, hi: (0, hi)),
                pl.BlockSpec((MLSTM_CONV, HEAD_DIM), lambda bi, hi: (0, MLSTM_HEADS + hi)),
                pl.BlockSpec((1, LANES), lambda bi, hi: (0, 0)),
                pl.BlockSpec((gr, 1), lambda bi, hi: (0, 0)),
                pl.BlockSpec((HEAD_DIM, 1), lambda bi, hi: (hi, 0))]
    chunk_t = (n_chunks, HEAD_DIM, MLSTM_CHUNK)
    scratch = [pltpu.VMEM((s_len, HEAD_DIM), BF16),
               pltpu.VMEM(chunk_t, BF16), pltpu.VMEM(chunk_t, F32), pltpu.VMEM(chunk_t, F32),
               pltpu.VMEM((s_len, 1), F32),
               pltpu.VMEM((n_chunks, 1, MLSTM_CHUNK), F32), pltpu.VMEM((n_chunks, 1, MLSTM_CHUNK), F32),
               pltpu.VMEM((s_len, LANES), F32)]
    return pl.pallas_call(
        _mlstm_body, grid=(b, MLSTM_HEADS), in_specs=in_specs,
        out_specs=pl.BlockSpec((1, s_len, HEAD_DIM), lambda bi, hi: (bi, 0, hi)),
        out_shape=jax.ShapeDtypeStruct((b, s_len, MLSTM_W), BF16),
        scratch_shapes=scratch, compiler_params=_cparams(("parallel", "arbitrary")),
        name="mlstm")(z3, z3, z3, z3, z3, gt, conv_w, conv_w, bcol, brow, head_gain.reshape(MLSTM_W, 1))


MOBA_ROWS = 2


def _moba_body(q_ref, k_ref, v_ref, cos_ref, sin_ref, o_ref, kr_scr, vt_scr, km_scr, sel_scr):
    i = pl.program_id(1)
    n_rows, s_len = k_ref.shape[0], k_ref.shape[1]
    bs = MOBA_BLOCK
    nb = s_len // bs
    heads = [slice(h * HEAD_DIM, (h + 1) * HEAD_DIM) for h in range(MOBA_HEADS)]
    units = [(b, h) for b in range(n_rows) for h in range(MOBA_HEADS)]

    @pl.when(i == 0)
    def _():
        for u, (b, h) in enumerate(units):
            _store_transposed(vt_scr.at[u], v_ref[b, :, heads[h]])
            kr = _rotate(k_ref[b, :, heads[h]], cos_ref[...], sin_ref[...])
            kr_scr[b, :, heads[h]] = kr.astype(BF16)
            rows = [jnp.sum(kr[n * bs:(n + 1) * bs, :], axis=0, keepdims=True) / float(bs) for n in range(nb)]
            rows.append(jnp.zeros((km_scr.shape[1] - nb, HEAD_DIM), F32))
            km_scr[u] = jnp.concatenate(rows, axis=0)

    t0 = pl.multiple_of(i * bs, bs)
    cos_q = cos_ref[pl.ds(t0, bs), :]
    sin_q = sin_ref[pl.ds(t0, bs), :]
    blk = lax.broadcasted_iota(jnp.int32, (km_scr.shape[1], bs), 0)
    causal_bias = jnp.where(lax.broadcasted_iota(jnp.int32, (bs, bs), 0)
                            <= lax.broadcasted_iota(jnp.int32, (bs, bs), 1), 0.0, NEG_INF)

    qbs = []
    for u, (b, h) in enumerate(units):
        qr = _rotate(q_ref[b, :, heads[h]], cos_q, sin_q)
        gate_t = lax.dot_general(km_scr[u], qr, _NT, precision=HIGHEST, preferred_element_type=F32)
        val = jnp.where(blk < i, gate_t, NEG_INF)
        picked = jnp.where(val > 0.5 * NEG_INF, _rank_before(val, nb, 0), float(nb)) < MOBA_TOPK
        sel_scr[u] = jnp.where(picked, 0.0, NEG_INF)
        qbs.append((qr * (HEAD_DIM ** -0.5 * LOG2_E)).astype(BF16))

    def key_tiles(k0, n_blocks):
        return ([kr_scr[b, pl.ds(k0, n_blocks * bs), heads[h]] for b, h in units],
                [vt_scr[u, :, pl.ds(k0, n_blocks * bs)] for u in range(len(units))])

    init = tuple(_flash_steps(qbs, *key_tiles(t0, 1), [causal_bias] * len(units), None))

    def past_blocks(j, n_blocks, carry):
        biases = [jnp.concatenate([jnp.broadcast_to(sel_scr[u, pl.ds(j + d, 1), :], (bs, bs))
                                   for d in range(n_blocks)], axis=0) for u in range(len(units))]
        k0 = j * bs if isinstance(j, int) else pl.multiple_of(j * bs, bs)
        return tuple(_flash_steps(qbs, *key_tiles(k0, n_blocks), biases, carry))

    odd = i % 2
    carry = lax.cond(odd == 1, lambda c: past_blocks(0, 1, c), lambda c: c, init)
    fin = lax.fori_loop(0, i // 2, lambda p, c: past_blocks(odd + 2 * p, 2, c), carry)
    for u, (b, h) in enumerate(units):
        o_ref[b, :, heads[h]] = _flash_output(fin[2 * u + 1]).T.astype(o_ref.dtype)


def _moba(z3, cos2, sin2):
    b, s_len, _ = z3.shape
    nb = s_len // MOBA_BLOCK
    q_off = 4 * MLSTM_W // MOBA_W
    rows = MOBA_ROWS if b % MOBA_ROWS == 0 else 1
    n_units = rows * MOBA_HEADS

    def kv(off):
        return pl.BlockSpec((rows, s_len, MOBA_W), lambda bi, i: (bi, 0, off))

    in_specs = [pl.BlockSpec((rows, MOBA_BLOCK, MOBA_W), lambda bi, i: (bi, i, q_off)),
                kv(q_off + 1), kv(q_off + 2),
                pl.BlockSpec((s_len, HEAD_DIM), lambda bi, i: (0, 0)),
                pl.BlockSpec((s_len, HEAD_DIM), lambda bi, i: (0, 0))]
    scratch = [pltpu.VMEM((rows, s_len, MOBA_W), BF16),
               pltpu.VMEM((n_units, VT_ROWS, s_len), BF16),
               pltpu.VMEM((n_units, BF16_SUBLANES, HEAD_DIM), F32),
               pltpu.VMEM((n_units, BF16_SUBLANES, MOBA_BLOCK), F32)]
    return pl.pallas_call(
        _moba_body, grid=(b // rows, nb), in_specs=in_specs,
        out_specs=pl.BlockSpec((rows, MOBA_BLOCK, MOBA_W), lambda bi, i: (bi, i, 0)),
        out_shape=jax.ShapeDtypeStruct((b, s_len, MOBA_W), BF16),
        scratch_shapes=scratch, compiler_params=_cparams(("parallel", "arbitrary")),
        name="moba")(z3, z3, z3, cos2, sin2)


def _gelu_tanh(x):
    return x * (0.5 * (1.0 + jnp.tanh(np.sqrt(2.0 / np.pi) * (x + 0.044715 * (x * x * x)))))


def _nsa_compress_body(kc_ref, vc_ref, pe_ref, wk1_ref, wk2_ref, wv1_ref, wv2_ref, ko_ref, vo_ref):
    n_rows = kc_ref.shape[1] // CMP_STRIDE
    halves = CMP_LEN // CMP_STRIDE
    assert halves == 2

    def compress(x_ref, pe, w1_ref, w2_ref):
        ya = jnp.zeros((n_rows, CMP_HIDDEN), F32)
        yb = jnp.zeros((n_rows, CMP_HIDDEN), F32)
        for l in range(CMP_STRIDE):
            r = x_ref[0, pl.ds(l, n_rows, stride=CMP_STRIDE), :]
            la, lb = l, CMP_STRIDE + l
            ya = ya + jnp.dot((r + pe[la:la + 1, :]).astype(BF16), w1_ref[la * HEAD_DIM:(la + 1) * HEAD_DIM, :],
                              preferred_element_type=F32)
            yb = yb + jnp.dot((r + pe[lb:lb + 1, :]).astype(BF16), w1_ref[lb * HEAD_DIM:(lb + 1) * HEAD_DIM, :],
                              preferred_element_type=F32)
        pre = ya + pltpu.roll(yb, n_rows - 1, axis=0)
        return jnp.dot(_gelu_tanh(pre).astype(BF16), w2_ref[...], preferred_element_type=F32)

    ko_ref[0, 0] = compress(kc_ref, pe_ref[0], wk1_ref, wk2_ref)
    vo_ref[0, 0] = compress(vc_ref, pe_ref[1], wv1_ref, wv2_ref)


def _nsa_compress(z3, cmp_pos, wk1, wk2, wv1, wv2):
    b, s_len, _ = z3.shape
    n_rows = s_len // CMP_STRIDE
    kc_off = NSA_W // LANES
    vc_off = kc_off + NSA_GROUPS
    const2 = lambda bi, gi: (0, 0)
    in_specs = [pl.BlockSpec((1, s_len, HEAD_DIM), lambda bi, gi: (bi, 0, kc_off + gi)),
                pl.BlockSpec((1, s_len, HEAD_DIM), lambda bi, gi: (bi, 0, vc_off + gi)),
                pl.BlockSpec(cmp_pos.shape, lambda bi, gi: (0, 0, 0)),
                pl.BlockSpec(wk1.shape, const2), pl.BlockSpec(wk2.shape, const2),
                pl.BlockSpec(wv1.shape, const2), pl.BlockSpec(wv2.shape, const2)]
    out_spec = pl.BlockSpec((1, 1, n_rows, HEAD_DIM), lambda bi, gi: (bi, gi, 0, 0))
    out_sds = jax.ShapeDtypeStruct((b, NSA_GROUPS, n_rows, HEAD_DIM), F32)
    return pl.pallas_call(
        _nsa_compress_body, grid=(b, NSA_GROUPS), in_specs=in_specs,
        out_specs=[out_spec, out_spec], out_shape=[out_sds, out_sds],
        compiler_params=_cparams(("parallel", "parallel")), name="nsa_compress")(
            z3, z3, cmp_pos, wk1, wk2, wv1, wv2)


def _nsa_body(q_ref, ks_ref, vs_ref, kw_ref, vw_ref, kc_ref, vc_ref, gate_ref, cos_ref, sin_ref, ov_ref,
              o_ref, krs, vst, krw, vwt, sel_scr):
    i = pl.program_id(1)
    tq = NSA_TQ
    tk = NSA_TQ
    scale = HEAD_DIM ** -0.5
    n_sel_blk = ks_ref.shape[1] // SEL_BLOCK
    n_cmp = (ks_ref.shape[1] - CMP_LEN) // CMP_STRIDE + 1
    shift = SEL_BLOCK.bit_length() - 1
    assert 1 << shift == SEL_BLOCK and n_sel_blk <= LANES
    blk_per_tile = tk // SEL_BLOCK
    groups = [slice(g * HEAD_DIM, (g + 1) * HEAD_DIM) for g in range(NSA_GROUPS)]

    @pl.when(i == 0)
    def _():
        for g, cols in enumerate(groups):
            krs[:, cols] = _rotate(ks_ref[0, :, cols], cos_ref[...], sin_ref[...]).astype(BF16)
            krw[:, cols] = _rotate(kw_ref[0, :, cols], cos_ref[...], sin_ref[...]).astype(BF16)
            _store_transposed(vst.at[g], vs_ref[0, :, cols])
            _store_transposed(vwt.at[g], vw_ref[0, :, cols])

    t0 = pl.multiple_of(i * tq, tq)
    pos_row = t0 + lax.broadcasted_iota(jnp.int32, (1, tq), 1)
    cos_q = cos_ref[pl.ds(t0, tq), :]
    sin_q = sin_ref[pl.ds(t0, tq), :]

    n_col = lax.broadcasted_iota(jnp.int32, (LANES, 1), 0)
    cmp_end = jnp.where(n_col < n_cmp, n_col * CMP_STRIDE + (CMP_LEN - 1), jnp.iinfo(jnp.int32).max)
    cmp_ok = cmp_end <= pos_row
    blk_i = lax.broadcasted_iota(jnp.int32, (n_sel_blk, tq), 0)
    behind = (pos_row >> shift) - blk_i
    units = [(g, r) for g in range(NSA_GROUPS) for r in range(NSA_REP)]
    qr_heads, o_cmp = [], []
    for g in range(NSA_GROUPS):
        kc = kc_ref[0, g].astype(BF16)
        vc = vc_ref[0, g].astype(BF16)
        p_sum = jnp.zeros((LANES, tq), F32)
        for r in range(NSA_REP):
            head = (g * NSA_REP + r) * HEAD_DIM
            q = q_ref[0, :, head:head + HEAD_DIM] * scale
            qr_heads.append(_rotate(q * LOG2_E, cos_q, sin_q).astype(BF16))
            s = lax.dot_general(kc, q.astype(BF16), _NT, preferred_element_type=F32)
            s = jnp.where(cmp_ok, s, NEG_INF)
            e = jnp.where(cmp_ok, jnp.exp(s - jnp.max(s, axis=0, keepdims=True)), 0.0)
            p = e / jnp.maximum(jnp.sum(e, axis=0, keepdims=True), 1e-30)
            o_cmp.append(lax.dot_general(p.astype(BF16), vc, _TN, preferred_element_type=F32))
            p_sum = p_sum + p
        imp = jnp.dot(ov_ref[...], p_sum, precision=HIGHEST, preferred_element_type=F32)
        val = jnp.where(behind == 0, FORCED_SCORE, jnp.where(behind == 1, FORCED_SCORE, imp))
        val = jnp.where(blk_i == 0, FORCED_SCORE, val)
        val = jnp.where(behind >= 0, val, NEG_INF)
        picked = jnp.where(val > 0.5 * NEG_INF, _rank_before(val, n_sel_blk, 0), float(n_sel_blk)) < SEL_TOPN
        sel_scr[g] = jnp.where(picked, 0.0, NEG_INF)

    def sel_bias(g, k0, n_tiles):
        blk0 = (k0 >> shift) if isinstance(k0, int) else pl.multiple_of(k0 >> shift, blk_per_tile)
        return jnp.concatenate([jnp.broadcast_to(sel_scr[g, pl.ds(blk0 + b, 1), :], (SEL_BLOCK, tq))
                                for b in range(n_tiles * blk_per_tile)], axis=0)

    def sel_tiles(k0, n_tiles, carry):
        biases = [sel_bias(g, k0, n_tiles) for g in range(NSA_GROUPS)]
        return tuple(_flash_steps(qr_heads, [krs[pl.ds(k0, n_tiles * tk), groups[g]] for g, _ in units],
                                  [vst[g, :, pl.ds(k0, n_tiles * tk)] for g, _ in units],
                                  [biases[g] for g, _ in units], carry))

    causal_bias = jnp.where(lax.broadcasted_iota(jnp.int32, (tk, tq), 0)
                            <= lax.broadcasted_iota(jnp.int32, (tk, tq), 1), 0.0, NEG_INF)
    span = WINDOW + tq
    start = pl.multiple_of(jnp.maximum(t0 - WINDOW, 0), tq)
    gap = ((t0 - start) + lax.broadcasted_iota(jnp.int32, (span, tq), 1)
           - lax.broadcasted_iota(jnp.int32, (span, tq), 0))
    win_bias = jnp.where(gap >= 0, jnp.where(gap < WINDOW, 0.0, NEG_INF), NEG_INF)
    diag_biases = [sel_bias(g, t0, 1) + causal_bias for g in range(NSA_GROUPS)]
    n = len(units)
    group = _flash_steps(qr_heads * 2,
                         [krs[pl.ds(t0, tk), groups[g]] for g, _ in units]
                         + [krw[pl.ds(start, span), groups[g]] for g, _ in units],
                         [vst[g, :, pl.ds(t0, tk)] for g, _ in units]
                         + [vwt[g, :, pl.ds(start, span)] for g, _ in units],
                         [diag_biases[g] for g, _ in units] + [win_bias] * n, None)
    win_fin = group[2 * n:]

    odd = i % 2
    carry = lax.cond(odd == 1, lambda c: sel_tiles(0, 1, c), lambda c: c, tuple(group[:2 * n]))
    sel_fin = lax.fori_loop(0, i // 2, lambda p, c: sel_tiles(pl.multiple_of((odd + 2 * p) * tk, tk), 2, c), carry)

    gates = _sigmoid(gate_ref[0])
    gates_t = [gates[:, cols].T for cols in groups]
    for u, (g, r) in enumerate(units):
        o_t = (gates_t[g][3 * r + 1:3 * r + 2, :] * _flash_output(sel_fin[2 * u + 1])
               + gates_t[g][3 * r + 2:3 * r + 3, :] * _flash_output(win_fin[2 * u + 1]))
        g_cmp = gates[:, g * LANES + 3 * r:g * LANES + 3 * r + 1]
        o_ref[0, :, u * HEAD_DIM:(u + 1) * HEAD_DIM] = (g_cmp * o_cmp[u] + o_t.T).astype(o_ref.dtype)


def _nsa_attention(z3, k_cmp, v_cmp, cos2, sin2):
    b, s_len, _ = z3.shape
    assert s_len >= WINDOW + NSA_TQ and s_len % NSA_TQ == 0
    n_cmp = (s_len - CMP_LEN) // CMP_STRIDE + 1
    n_blk = s_len // SEL_BLOCK
    n_rows = k_cmp.shape[2]
    assert n_rows == LANES
    cmp_start = np.arange(n_rows) * CMP_STRIDE
    cmp_end = cmp_start + CMP_LEN - 1
    blk_lo = np.arange(LANES) * SEL_BLOCK
    overlap = ((cmp_start[:, None] <= blk_lo[None, :] + SEL_BLOCK - 1) & (cmp_end[:, None] >= blk_lo[None, :])
               & (np.arange(n_rows)[:, None] < n_cmp)).astype(np.float32)
    overlap_t = np.ascontiguousarray(overlap.T[:n_blk])
    base = NSA_W // NSA_KV_W

    def kv(off):
        return pl.BlockSpec((1, s_len, NSA_KV_W), lambda bi, i: (bi, 0, base + off))

    cmp_spec = pl.BlockSpec((1, NSA_GROUPS, n_rows, HEAD_DIM), lambda bi, i: (bi, 0, 0, 0))
    const2 = lambda bi, i: (0, 0)
    gate_w = NSA_GROUPS * LANES
    in_specs = [pl.BlockSpec((1, NSA_TQ, NSA_W), lambda bi, i: (bi, i, 0)),
                kv(2), kv(3), kv(4), kv(5), cmp_spec, cmp_spec,
                pl.BlockSpec((1, NSA_TQ, gate_w), lambda bi, i: (bi, i, ODD_GATE_BLK * LANES // gate_w)),
                pl.BlockSpec((s_len, HEAD_DIM), const2), pl.BlockSpec((s_len, HEAD_DIM), const2),
                pl.BlockSpec((n_blk, n_rows), const2)]
    vt_shape = (NSA_GROUPS, VT_ROWS, s_len)
    scratch = [pltpu.VMEM((s_len, NSA_KV_W), BF16), pltpu.VMEM(vt_shape, BF16),
               pltpu.VMEM((s_len, NSA_KV_W), BF16), pltpu.VMEM(vt_shape, BF16),
               pltpu.VMEM((NSA_GROUPS, n_blk, NSA_TQ), F32)]
    assert (ODD_GATE_BLK * LANES) % gate_w == 0
    return pl.pallas_call(
        _nsa_body, grid=(b, s_len // NSA_TQ), in_specs=in_specs,
        out_specs=pl.BlockSpec((1, NSA_TQ, NSA_W), lambda bi, i: (bi, i, 0)),
        out_shape=jax.ShapeDtypeStruct((b, s_len, NSA_W), BF16),
        scratch_shapes=scratch, compiler_params=_cparams(("parallel", "arbitrary")),
        name="nsa_attention")(z3, z3, z3, z3, z3, k_cmp, v_cmp, z3, cos2, sin2, jnp.asarray(overlap_t))


def _split_bf16(x):
    hi = x.astype(BF16)
    return hi, (x - hi.astype(F32)).astype(BF16)


def _proj_moe_body(*refs, n_act, final_norm):
    res_ref = refs[0]
    a_refs = refs[1:1 + n_act]
    wo_refs = refs[1 + n_act:1 + 2 * n_act]
    (g_ref, wr_ref, br_ref, w1_ref, w3_ref, w2_ref, fg_ref, o_ref,
     xn_scr, cw_scr, hd_scr) = refs[1 + 2 * n_act:]
    j = pl.program_id(1)
    tm = res_ref.shape[0]
    lane = lax.broadcasted_iota(jnp.int32, (tm, LANES), 1).astype(F32)
    neg = float("-inf")

    @pl.when(j == 0)
    def _():
        h = res_ref[...]
        for a_ref, wo_ref in zip(a_refs, wo_refs):
            h = h + jnp.dot(a_ref[...], wo_ref[...], preferred_element_type=F32)
        o_ref[...] = h
        xn = _rms(h, g_ref[...])
        xh, xl = _split_bf16(xn)
        xn_scr[...] = xh
        hi_terms = jnp.dot(xh, wr_ref[...], preferred_element_type=F32)
        logits = (hi_terms[:, :LANES] + (jnp.dot(xl, wr_ref[:, :LANES], preferred_element_type=F32)
                                         + hi_terms[:, LANES:])) + br_ref[...]
        is_g = lane < MOE_GROUPS
        gl = jnp.where(is_g, logits, neg)
        g_max = jnp.max(gl, axis=1, keepdims=True)
        g_w = 1.0 / jnp.sum(jnp.where(is_g, jnp.exp(gl - g_max), 0.0), axis=1, keepdims=True)
        g_top = jnp.min(jnp.where(gl == g_max, lane, float(LANES)), axis=1, keepdims=True)
        lo = MOE_GROUPS + MOE_EPG * g_top
        el = jnp.where(lane >= lo, jnp.where(lane < lo + MOE_EPG, logits, neg), neg)
        v1 = jnp.max(el, axis=1, keepdims=True)
        i1 = jnp.min(jnp.where(el == v1, lane, float(LANES)), axis=1, keepdims=True)
        el2 = jnp.where(lane == i1, neg, el)
        v2 = jnp.max(el2, axis=1, keepdims=True)
        i2 = jnp.min(jnp.where(el2 == v2, lane, float(LANES)), axis=1, keepdims=True)
        e2 = jnp.exp(v2 - v1)
        den = 1.0 + e2
        cw_scr[...] = (jnp.where(lane == i1, g_w / den, 0.0) + jnp.where(lane == i2, g_w * e2 / den, 0.0))

    xn = xn_scr[...]
    cw = cw_scr[...]
    for r in range(MOE_EPG):
        h1 = jnp.dot(xn, w1_ref[r], preferred_element_type=F32)
        h3 = jnp.dot(xn, w3_ref[r], preferred_element_type=F32)
        e_lane = (MOE_GROUPS + MOE_EPG * j + r).astype(F32)
        col = jnp.sum(jnp.where(lane == e_lane, cw, 0.0), axis=1, keepdims=True)
        hd_scr[:, r * MOE_HIDDEN:(r + 1) * MOE_HIDDEN] = ((h1 * _sigmoid(h1)) * h3 * col).astype(BF16)
    o_ref[...] += jnp.dot(hd_scr[...], w2_ref[...], preferred_element_type=F32)

    if final_norm:
        @pl.when(j == pl.num_programs(1) - 1)
        def _():
            o_ref[...] = _rms(o_ref[...], fg_ref[...])


def _proj_moe(res2d, acts, w_outs, gain, w_g, b_g, w_e, b_e, w1, w3, w2, final_gain, final_norm, tm=1024):
    t, d = res2d.shape
    n_act = len(acts)
    wr = jnp.zeros((d, LANES), F32).at[:, :MOE_GROUPS].set(w_g).at[:, MOE_GROUPS:MOE_GROUPS + MOE_EXPERTS].set(w_e)
    wr_cat = jnp.concatenate(_split_bf16(wr), axis=1)
    br = jnp.zeros((1, LANES), F32).at[0, :MOE_GROUPS].set(b_g).at[0, MOE_GROUPS:MOE_GROUPS + MOE_EXPERTS].set(b_e)
    tile = lambda i, j: (i, 0)
    const = lambda i, j: (0, 0)
    group_hidden = MOE_EPG * MOE_HIDDEN
    in_specs = [pl.BlockSpec((tm, d), tile)]
    in_specs += [pl.BlockSpec((tm, a.shape[1]), tile) for a in acts]
    in_specs += [pl.BlockSpec(w.shape, const) for w in w_outs]
    in_specs += [pl.BlockSpec((1, d), const),
                 pl.BlockSpec((d, 2 * LANES), const),
                 pl.BlockSpec((1, LANES), const),
                 pl.BlockSpec((MOE_EPG, d, MOE_HIDDEN), lambda i, j: (j, 0, 0)),
                 pl.BlockSpec((MOE_EPG, d, MOE_HIDDEN), lambda i, j: (j, 0, 0)),
                 pl.BlockSpec((group_hidden, d), lambda i, j: (j, 0)),
                 pl.BlockSpec((1, d), const)]
    scratch = [pltpu.VMEM((tm, d), BF16), pltpu.VMEM((tm, LANES), F32), pltpu.VMEM((tm, group_hidden), BF16)]
    return pl.pallas_call(
        functools.partial(_proj_moe_body, n_act=n_act, final_norm=final_norm),
        grid=(t // tm, MOE_GROUPS), in_specs=in_specs,
        out_specs=pl.BlockSpec((tm, d), tile),
        out_shape=jax.ShapeDtypeStruct((t, d), F32),
        scratch_shapes=scratch, compiler_params=_cparams(("parallel", "arbitrary")),
        name="out_proj_moe")(res2d, *acts, *w_outs, gain.reshape(1, d), wr_cat, br,
                             w1.astype(BF16), w3.astype(BF16), w2.astype(BF16).reshape(MOE_EXPERTS * MOE_HIDDEN, d),
                             final_gain.reshape(1, d))


def _rope_tables(s_len):
    half = HEAD_DIM // 2
    inv = ROPE_THETA ** (-(jnp.arange(half, dtype=F32) / half))
    ang = jnp.arange(s_len, dtype=F32)[:, None] * inv[None, :]
    cos, sin = jnp.cos(ang), jnp.sin(ang)
    return jnp.concatenate([cos, cos], axis=-1), jnp.concatenate([-sin, sin], axis=-1)


def _even_weights(w_in):
    n_if = 2 * MLSTM_HEADS
    a = 4 * MLSTM_W
    w_main = jnp.concatenate([w_in[:, :a], w_in[:, a + n_if:],
                              w_in[:, a:a + n_if], jnp.zeros((w_in.shape[0], LANES - n_if), w_in.dtype)], axis=1)
    w_if_t = jnp.concatenate([w_in[:, a:a + n_if].T, jnp.zeros((16 - n_if, w_in.shape[0]), w_in.dtype)], axis=0)
    return w_main.astype(BF16), w_if_t.astype(BF16)


def _odd_weights(w_in):
    a = NSA_W + 6 * NSA_KV_W
    per_group = NSA_REP * 3
    pad = jnp.zeros((w_in.shape[0], LANES - per_group), w_in.dtype)
    cols = [w_in[:, :a]]
    for g in range(NSA_GROUPS):
        cols += [w_in[:, a + g * per_group:a + (g + 1) * per_group], pad]
    return jnp.concatenate(cols, axis=1).astype(BF16)


def kernel(x, mix_norm_0, w_in_0, mlstm_conv_0, mlstm_gate_b_0, mlstm_head_norm_0, w_out_0, ffn_norm_0, router_group_0, router_group_b_0, router_expert_0, router_expert_b_0, moe_w1_0, moe_w3_0, moe_w2_0, mix_norm_1, w_in_1, nsa_cmp_pos_1, nsa_cmp_k1_1, nsa_cmp_k2_1, nsa_cmp_v1_1, nsa_cmp_v2_1, w_out_1, ffn_norm_1, router_group_1, router_group_b_1, router_expert_1, router_expert_b_1, moe_w1_1, moe_w3_1, moe_w2_1, final_norm):
    b, s_len, d = x.shape
    t = b * s_len
    cos2, sin2 = _rope_tables(s_len)
    x2d = x.reshape(t, d)

    w_main, w_if_t = _even_weights(w_in_0)
    z0, gt = _norm_matmul(x2d, mix_norm_0, w_main, w_if_t)
    z0 = z0.reshape(b, s_len, EVEN_N)
    h_m = _mlstm(z0, gt, mlstm_conv_0, mlstm_gate_b_0, mlstm_head_norm_0)
    o_b = _moba(z0, cos2, sin2)
    w_out = w_out_0.astype(BF16)
    h = _proj_moe(x2d, [h_m.reshape(t, MLSTM_W), o_b.reshape(t, MOBA_W)], [w_out[:MLSTM_W], w_out[MLSTM_W:]],
                  ffn_norm_0, router_group_0, router_group_b_0, router_expert_0, router_expert_b_0,
                  moe_w1_0, moe_w3_0, moe_w2_0, final_norm, False)

    z1 = _norm_matmul(h, mix_norm_1, _odd_weights(w_in_1)).reshape(b, s_len, ODD_N)
    k_cmp, v_cmp = _nsa_compress(z1, nsa_cmp_pos_1, nsa_cmp_k1_1.astype(BF16), nsa_cmp_k2_1.astype(BF16),
                                 nsa_cmp_v1_1.astype(BF16), nsa_cmp_v2_1.astype(BF16))
    o = _nsa_attention(z1, k_cmp, v_cmp, cos2, sin2)
    h = _proj_moe(h, [o.reshape(t, NSA_W)], [w_out_1.astype(BF16)],
                  ffn_norm_1, router_group_1, router_group_b_1, router_expert_1, router_expert_b_1,
                  moe_w1_1, moe_w3_1, moe_w2_1, final_norm, True)
    return h.reshape(b, s_len, d)
```

```python
import functools

import numpy as np
import jax
import jax.numpy as jnp
from jax import lax
from jax.experimental import pallas as pl
from jax.experimental.pallas import tpu as pltpu

F32 = jnp.float32
BF16 = jnp.bfloat16
HIGHEST = lax.Precision.HIGHEST

LANES = 128
BF16_SUBLANES = 16
D_MODEL = 1024
HEAD_DIM = 128
ROPE_THETA = 10000.0
NORM_EPS = 1e-6
NEG_INF = -1e30
FORCED_SCORE = 1e4

MLSTM_HEADS = 4
MLSTM_W = MLSTM_HEADS * HEAD_DIM
MLSTM_CHUNK = 64
MLSTM_CONV = 4
MLSTM_GATE_CAP = 15.0
MOBA_HEADS = 4
MOBA_W = MOBA_HEADS * HEAD_DIM
MOBA_BLOCK = 256
MOBA_TOPK = 3

NSA_HEADS = 8
NSA_GROUPS = 2
NSA_REP = NSA_HEADS // NSA_GROUPS
NSA_W = NSA_HEADS * HEAD_DIM
NSA_KV_W = NSA_GROUPS * HEAD_DIM
CMP_LEN = 32
CMP_STRIDE = 16
CMP_HIDDEN = 256
SEL_BLOCK = 64
SEL_TOPN = 8
WINDOW = 512
NSA_TQ = 256

MOE_GROUPS = 4
MOE_EPG = 4
MOE_EXPERTS = MOE_GROUPS * MOE_EPG
MOE_HIDDEN = D_MODEL // 4

EVEN_N = 4 * MLSTM_W + 3 * MOBA_W + LANES
EVEN_IF_BLK = (4 * MLSTM_W + 3 * MOBA_W) // LANES
ODD_GATE_BLK = (NSA_W + 6 * NSA_KV_W) // LANES
ODD_N = NSA_W + 6 * NSA_KV_W + NSA_GROUPS * LANES

VMEM_LIMIT = 56 * 1024 * 1024
VT_ROWS = HEAD_DIM + BF16_SUBLANES
LOG2_E = float(np.log2(np.e))

_NT = (((1,), (1,)), ((), ()))
_TN = (((0,), (0,)), ((), ()))


def _cparams(sem):
    return pltpu.CompilerParams(dimension_semantics=sem, vmem_limit_bytes=VMEM_LIMIT)


def _rms(x, g):
    return x * lax.rsqrt(jnp.mean(x * x, axis=-1, keepdims=True) + NORM_EPS) * g


def _sigmoid(x):
    return 0.5 * jnp.tanh(0.5 * x) + 0.5


def _log_sigmoid(x):
    return -(jnp.maximum(-x, 0.0) + jnp.log1p(jnp.exp(-jnp.abs(x))))


def _rotate(x, cos2, sin2):
    return x * cos2 + pltpu.roll(x, HEAD_DIM // 2, axis=1) * sin2


def _flash_steps(qs, k_ts, vt_ts, biases, carries):
    n = len(qs)
    ss = [lax.dot_general(k_ts[u], qs[u], _NT, preferred_element_type=F32) + biases[u] for u in range(n)]
    ms, es, alphas, pvs = [], [], [], []

    def value_matmul(u):
        pv = jnp.dot(vt_ts[u], es[u], preferred_element_type=F32)
        if carries is not None:
            pv = alphas[u] * carries[2 * u + 1] + pv
        pvs.append(pv)

    for u in range(n):
        m_new = jnp.max(ss[u], axis=0, keepdims=True)
        if carries is not None:
            m_new = jnp.maximum(carries[2 * u], m_new)
            alphas.append(jnp.exp2(carries[2 * u] - m_new))
        ms.append(m_new)
        es.append(jnp.exp2(ss[u] - m_new).astype(BF16))
        if u > 0:
            value_matmul(u - 1)
    value_matmul(n - 1)
    out = []
    for u in range(n):
        out += [ms[u], pvs[u]]
    return out


def _flash_output(acc):
    return acc[:HEAD_DIM] / jnp.maximum(acc[HEAD_DIM:HEAD_DIM + 1], 1e-30)


def _store_transposed(dst_ref, x):
    n_rows, n_cols = x.shape
    for r in range(0, n_rows, LANES):
        dst_ref[:n_cols, r:r + LANES] = x[r:r + LANES, :].T.astype(dst_ref.dtype)
    dst_ref[n_cols:, :] = jnp.ones((dst_ref.shape[0] - n_cols, n_rows), dst_ref.dtype)


def _rank_before(v, n_valid, axis):
    idx = lax.broadcasted_iota(jnp.int32, v.shape, axis)
    rank = jnp.zeros(v.shape, F32)
    for m in range(n_valid):
        vm = v[m:m + 1, :] if axis == 0 else v[:, m:m + 1]
        tie = jnp.where(idx > m, 1.0, 0.0)
        rank = rank + jnp.where(vm > v, 1.0, jnp.where(vm == v, tie, 0.0))
    return rank


def _norm_matmul_body(*refs, n_chunk, with_t):
    if with_t:
        x_ref, g_ref, w_ref, wt_ref, o_ref, ot_ref = refs
    else:
        x_ref, g_ref, w_ref, o_ref = refs
    yb = _rms(x_ref[...], g_ref[...]).astype(BF16)
    n = w_ref.shape[1]
    for c0 in range(0, n, n_chunk):
        c1 = min(n, c0 + n_chunk)
        o_ref[:, c0:c1] = jnp.dot(yb, w_ref[:, c0:c1], preferred_element_type=F32)
    if with_t:
        ot_ref[...] = lax.dot_general(wt_ref[...], yb, _NT, preferred_element_type=F32)


def _norm_matmul(x2d, gain, w, wt=None, tm=512):
    t, d = x2d.shape
    n = w.shape[1]
    with_t = wt is not None
    in_specs = [pl.BlockSpec((tm, d), lambda i: (i, 0)),
                pl.BlockSpec((1, d), lambda i: (0, 0)),
                pl.BlockSpec((d, n), lambda i: (0, 0))]
    out_specs = [pl.BlockSpec((tm, n), lambda i: (i, 0))]
    out_shape = [jax.ShapeDtypeStruct((t, n), F32)]
    args = [x2d, gain.reshape(1, d), w]
    if with_t:
        r = wt.shape[0]
        in_specs.append(pl.BlockSpec((r, d), lambda i: (0, 0)))
        out_specs.append(pl.BlockSpec((r, tm), lambda i: (0, i)))
        out_shape.append(jax.ShapeDtypeStruct((r, t), F32))
        args.append(wt)
    outs = pl.pallas_call(
        functools.partial(_norm_matmul_body, n_chunk=512, with_t=with_t),
        grid=(t // tm,), in_specs=in_specs, out_specs=out_specs, out_shape=out_shape,
        compiler_params=_cparams(("parallel",)), name="norm_in_proj")(*args)
    return outs if with_t else outs[0]


def _chunk_cumsum(x, axis):
    idx = lax.broadcasted_iota(jnp.int32, x.shape, axis) % MLSTM_CHUNK
    d = 1
    while d < MLSTM_CHUNK:
        x = x + jnp.where(idx >= d, pltpu.roll(x, d, axis=axis), 0.0)
        d *= 2
    return x


MLSTM_HEADS_PER_STEP = 2


def _mlstm_body(q_ref, k_ref, v_ref, og_ref, if_ref, gt_ref, cwq_ref, cwk_ref, bcol_ref, brow_ref,
                gain_ref, out_ref, ks, qts, vts, hts, css, brs, lirs, gcol):
    hp = pl.program_id(1)
    s_len = q_ref.shape[1]
    n_local = q_ref.shape[2] // HEAD_DIM
    n_chunks = s_len // MLSTM_CHUNK
    L = MLSTM_CHUNK
    per_slab = LANES // L
    row = lax.broadcasted_iota(jnp.int32, (s_len, HEAD_DIM), 0)
    cols = [slice(a * HEAD_DIM, (a + 1) * HEAD_DIM) for a in range(n_local)]

    def conv_silu(x, w):
        acc = x * w[MLSTM_CONV - 1:MLSTM_CONV, :]
        for d in range(1, MLSTM_CONV):
            shifted = jnp.where(row >= d, pltpu.roll(x, d, axis=0), 0.0)
            acc = acc + shifted * w[MLSTM_CONV - 1 - d:MLSTM_CONV - d, :]
        return acc * _sigmoid(acc)

    def store_chunks_transposed(dst, x):
        for p in range(s_len // LANES):
            slab_t = x[p * LANES:(p + 1) * LANES, :].T
            for j in range(per_slab):
                dst[per_slab * p + j] = slab_t[:, j * L:(j + 1) * L].astype(dst.dtype)

    lane = lax.broadcasted_iota(jnp.int32, gcol.shape, 1)

    @pl.when(hp == 0)
    def _():
        pre = if_ref[0] + bcol_ref[...]
        pre = MLSTM_GATE_CAP * jnp.tanh(pre / MLSTM_GATE_CAP)
        gcol[...] = jnp.where(lane < MLSTM_HEADS, pre, _chunk_cumsum(_log_sigmoid(pre), 0))

    pr = gt_ref[...] + brow_ref[...]
    pr = MLSTM_GATE_CAP * jnp.tanh(pr / MLSTM_GATE_CAP)
    sub = lax.broadcasted_iota(jnp.int32, pr.shape, 0)
    b_rows = _chunk_cumsum(_log_sigmoid(pr), 1)
    g_all = gcol[...]

    gains = []
    for a in range(n_local):
        h = hp * n_local + a
        store_chunks_transposed(qts.at[a], conv_silu(q_ref[0, :, cols[a]], cwq_ref[:, cols[a]]) * (HEAD_DIM ** -0.5))
        store_chunks_transposed(vts.at[a], v_ref[0, :, cols[a]])
        ks[:, cols[a]] = conv_silu(k_ref[0, :, cols[a]], cwk_ref[:, cols[a]]).astype(BF16)
        css[a] = jnp.sum(jnp.where(lane == h, g_all, jnp.where(lane == h + MLSTM_HEADS, -g_all, 0.0)),
                         axis=1, keepdims=True)
        li_row = jnp.sum(jnp.where(sub == h, pr, 0.0), axis=0, keepdims=True)
        b_row = jnp.sum(jnp.where(sub == h + MLSTM_HEADS, b_rows, 0.0), axis=0, keepdims=True)
        for c in range(n_chunks):
            brs[a, c] = b_row[:, c * L:(c + 1) * L]
            lirs[a, c] = li_row[:, c * L:(c + 1) * L]
        gains.append(jnp.broadcast_to(gain_ref[a * HEAD_DIM:(a + 1) * HEAD_DIM, :], (HEAD_DIM, L)))

    tri = lax.broadcasted_iota(jnp.int32, (L, L), 0) <= lax.broadcasted_iota(jnp.int32, (L, L), 1)

    group_size = 8
    assert n_chunks % group_size == 0

    def group(gi, carry):
        states = [list(carry[3 * a:3 * a + 3]) for a in range(n_local)]
        cs = [gi * group_size + j for j in range(group_size)]
        r0s = [pl.multiple_of(c * L, L) for c in cs]
        units = [(a, j) for a in range(n_local) for j in range(group_size)]
        ks_ = {(a, j): ks[pl.ds(r0s[j], L), cols[a]] for a, j in units}
        q_ts = {(a, j): qts[a, cs[j]] for a, j in units}
        v_ts = {(a, j): vts[a, cs[j]] for a, j in units}
        b_rs = {(a, j): brs[a, cs[j]] for a, j in units}
        kvs, ksums, kqs, g_maxs, b_lasts = {}, {}, {}, {}, {}
        for u in units:
            a, j = u
            b_last = b_rs[u][:, L - 1:L]
            g = b_last - b_rs[u] + lirs[a, cs[j]]
            g_max = jnp.max(g, axis=1, keepdims=True)
            w = jnp.exp(g - g_max)
            kvs[u] = jnp.dot((v_ts[u] * w).astype(BF16), ks_[u], preferred_element_type=F32)
            ksums[u] = jnp.dot(jnp.broadcast_to(w, (BF16_SUBLANES, L)).astype(BF16), ks_[u],
                               preferred_element_type=F32)[:1]
            kqs[u] = jnp.dot(ks_[u], q_ts[u], preferred_element_type=F32)
            g_maxs[u] = g_max
            b_lasts[u] = b_last
        c_ins, n_ins, m_ins = {}, {}, {}
        for u in units:
            a, j = u
            c_state, n_state, m_state = states[a]
            c_ins[u] = c_state.astype(BF16)
            n_ins[u] = jnp.broadcast_to(n_state, (BF16_SUBLANES, HEAD_DIM)).astype(BF16)
            m_ins[u] = m_state
            m_new = jnp.maximum(b_lasts[u] + m_state, g_maxs[u])
            sa = jnp.exp(b_lasts[u] + m_state - m_new)
            cc = jnp.exp(g_maxs[u] - m_new)
            states[a] = [sa * c_state + cc * kvs[u], sa * n_state + cc * ksums[u], m_new]
        c_qs = {u: jnp.dot(c_ins[u], q_ts[u], preferred_element_type=F32) for u in units}
        n_qs = {u: jnp.dot(n_ins[u], q_ts[u], preferred_element_type=F32)[:1] for u in units}
        ss, m_ts, w_inters = {}, {}, {}
        for u in units:
            a, j = u
            dmat = jnp.where(tri, b_rs[u] + css[a, pl.ds(r0s[j], L), :], NEG_INF)
            inter = b_rs[u] + m_ins[u]
            m_t = jnp.maximum(inter, jnp.max(dmat, axis=0, keepdims=True))
            ss[u] = kqs[u] * jnp.exp(dmat - m_t)
            m_ts[u] = m_t
            w_inters[u] = jnp.exp(inter - m_t)
        svs = {u: jnp.dot(v_ts[u].astype(BF16), ss[u].astype(BF16), preferred_element_type=F32) for u in units}
        for u in units:
            a, j = u
            num = svs[u] + w_inters[u] * c_qs[u]
            den = jnp.sum(ss[u], axis=0, keepdims=True) + w_inters[u] * n_qs[u]
            ht = num / jnp.maximum(jnp.abs(den), jnp.exp(-m_ts[u]))
            hts[a, cs[j]] = ht * lax.rsqrt(jnp.mean(ht * ht, axis=0, keepdims=True) + NORM_EPS) * gains[a]
        return tuple(x for st in states for x in st)

    init = (jnp.zeros((HEAD_DIM, HEAD_DIM), F32), jnp.zeros((1, HEAD_DIM), F32), jnp.zeros((1, 1), F32)) * n_local
    lax.fori_loop(0, n_chunks // group_size, group, init)

    for a in range(n_local):
        for p in range(s_len // LANES):
            slab_t = jnp.concatenate([hts[a, per_slab * p + j] for j in range(per_slab)], axis=1)
            rows = slice(p * LANES, (p + 1) * LANES)
            out_ref[0, rows, cols[a]] = (slab_t.T * _sigmoid(og_ref[0, rows, cols[a]])).astype(out_ref.dtype)


def _mlstm(z3, gt, conv_w, gate_b, head_gain):
    b, s_len, _ = z3.shape
    n_chunks = s_len // MLSTM_CHUNK
    gr = gt.shape[0]
    bcol = jnp.zeros((1, LANES), F32).at[0, :2 * MLSTM_HEADS].set(gate_b)
    brow = jnp.zeros((gr, 1), F32).at[:2 * MLSTM_HEADS, 0].set(gate_b)
    n_local = MLSTM_HEADS_PER_STEP
    n_steps = MLSTM_HEADS // n_local
    width = n_local * HEAD_DIM

    def col(off):
        return pl.BlockSpec((1, s_len, width), lambda bi, hi: (bi, 0, off * n_steps + hi))

    in_specs = [col(0), col(1), col(2), col(3),
                pl.BlockSpec((1, s_len, LANES), lambda bi, hi: (bi, 0, EVEN_IF_BLK)),
                pl.BlockSpec((gr, s_len), lambda bi, hi: (0, bi)),
                pl.BlockSpec((MLSTM_CONV, width), lambda bi, hi: (0, hi)),
                pl.BlockSpec((MLSTM_CONV, width), lambda bi, hi: (0, n_steps + hi)),
                pl.BlockSpec((1, LANES), lambda bi, hi: (0, 0)),
                pl.BlockSpec((gr, 1), lambda bi, hi: (0, 0)),
                pl.BlockSpec((width, 1), lambda bi, hi: (hi, 0))]
    chunk_t = (n_local, n_chunks, HEAD_DIM, MLSTM_CHUNK)
    rows_t = (n_local, n_chunks, 1, MLSTM_CHUNK)
    scratch = [pltpu.VMEM((s_len, width), BF16),
               pltpu.VMEM(chunk_t, BF16), pltpu.VMEM(chunk_t, F32), pltpu.VMEM(chunk_t, F32),
               pltpu.VMEM((n_local, s_len, 1), F32),
               pltpu.VMEM(rows_t, F32), pltpu.VMEM(rows_t, F32),
               pltpu.VMEM((s_len, LANES), F32)]
    return pl.pallas_call(
        _mlstm_body, grid=(b, n_steps), in_specs=in_specs,
        out_specs=pl.BlockSpec((1, s_len, width), lambda bi, hi: (bi, 0, hi)),
        out_shape=jax.ShapeDtypeStruct((b, s_len, MLSTM_W), BF16),
        scratch_shapes=scratch, compiler_params=_cparams(("parallel", "arbitrary")),
        name="mlstm")(z3, z3, z3, z3, z3, gt, conv_w, conv_w, bcol, brow, head_gain.reshape(MLSTM_W, 1))


MOBA_ROWS = 2


def _moba_body(q_ref, k_ref, v_ref, cos_ref, sin_ref, o_ref, kr_scr, vt_scr, km_scr, sel_scr):
    i = pl.program_id(1)
    n_rows, s_len = k_ref.shape[0], k_ref.shape[1]
    bs = MOBA_BLOCK
    nb = s_len // bs
    heads = [slice(h * HEAD_DIM, (h + 1) * HEAD_DIM) for h in range(MOBA_HEADS)]
    units = [(b, h) for b in range(n_rows) for h in range(MOBA_HEADS)]

    @pl.when(i == 0)
    def _():
        for u, (b, h) in enumerate(units):
            _store_transposed(vt_scr.at[u], v_ref[b, :, heads[h]])
            kr = _rotate(k_ref[b, :, heads[h]], cos_ref[...], sin_ref[...])
            kr_scr[b, :, heads[h]] = kr.astype(BF16)
            rows = [jnp.sum(kr[n * bs:(n + 1) * bs, :], axis=0, keepdims=True) / float(bs) for n in range(nb)]
            rows.append(jnp.zeros((km_scr.shape[1] - nb, HEAD_DIM), F32))
            km_scr[u] = jnp.concatenate(rows, axis=0)

    t0 = pl.multiple_of(i * bs, bs)
    cos_q = cos_ref[pl.ds(t0, bs), :]
    sin_q = sin_ref[pl.ds(t0, bs), :]
    blk = lax.broadcasted_iota(jnp.int32, (km_scr.shape[1], bs), 0)
    causal_bias = jnp.where(lax.broadcasted_iota(jnp.int32, (bs, bs), 0)
                            <= lax.broadcasted_iota(jnp.int32, (bs, bs), 1), 0.0, NEG_INF)

    qbs = []
    for u, (b, h) in enumerate(units):
        qr = _rotate(q_ref[b, :, heads[h]], cos_q, sin_q)
        gate_t = lax.dot_general(km_scr[u], qr, _NT, precision=HIGHEST, preferred_element_type=F32)
        val = jnp.where(blk < i, gate_t, NEG_INF)
        picked = jnp.where(val > 0.5 * NEG_INF, _rank_before(val, nb, 0), float(nb)) < MOBA_TOPK
        sel_scr[u] = jnp.where(picked, 0.0, NEG_INF)
        qbs.append((qr * (HEAD_DIM ** -0.5 * LOG2_E)).astype(BF16))

    def key_tiles(k0, n_blocks):
        return ([kr_scr[b, pl.ds(k0, n_blocks * bs), heads[h]] for b, h in units],
                [vt_scr[u, :, pl.ds(k0, n_blocks * bs)] for u in range(len(units))])

    init = tuple(_flash_steps(qbs, *key_tiles(t0, 1), [causal_bias] * len(units), None))

    def past_blocks(j, n_blocks, carry):
        biases = [jnp.concatenate([jnp.broadcast_to(sel_scr[u, pl.ds(j + d, 1), :], (bs, bs))
                                   for d in range(n_blocks)], axis=0) for u in range(len(units))]
        k0 = j * bs if isinstance(j, int) else pl.multiple_of(j * bs, bs)
        return tuple(_flash_steps(qbs, *key_tiles(k0, n_blocks), biases, carry))

    odd = i % 2
    carry = lax.cond(odd == 1, lambda c: past_blocks(0, 1, c), lambda c: c, init)
    fin = lax.fori_loop(0, i // 2, lambda p, c: past_blocks(odd + 2 * p, 2, c), carry)
    for u, (b, h) in enumerate(units):
        o_ref[b, :, heads[h]] = _flash_output(fin[2 * u + 1]).T.astype(o_ref.dtype)


def _moba(z3, cos2, sin2):
    b, s_len, _ = z3.shape
    nb = s_len // MOBA_BLOCK
    q_off = 4 * MLSTM_W // MOBA_W
    rows = MOBA_ROWS if b % MOBA_ROWS == 0 else 1
    n_units = rows * MOBA_HEADS

    def kv(off):
        return pl.BlockSpec((rows, s_len, MOBA_W), lambda bi, i: (bi, 0, off))

    in_specs = [pl.BlockSpec((rows, MOBA_BLOCK, MOBA_W), lambda bi, i: (bi, i, q_off)),
                kv(q_off + 1), kv(q_off + 2),
                pl.BlockSpec((s_len, HEAD_DIM), lambda bi, i: (0, 0)),
                pl.BlockSpec((s_len, HEAD_DIM), lambda bi, i: (0, 0))]
    scratch = [pltpu.VMEM((rows, s_len, MOBA_W), BF16),
               pltpu.VMEM((n_units, VT_ROWS, s_len), BF16),
               pltpu.VMEM((n_units, BF16_SUBLANES, HEAD_DIM), F32),
               pltpu.VMEM((n_units, BF16_SUBLANES, MOBA_BLOCK), F32)]
    return pl.pallas_call(
        _moba_body, grid=(b // rows, nb), in_specs=in_specs,
        out_specs=pl.BlockSpec((rows, MOBA_BLOCK, MOBA_W), lambda bi, i: (bi, i, 0)),
        out_shape=jax.ShapeDtypeStruct((b, s_len, MOBA_W), BF16),
        scratch_shapes=scratch, compiler_params=_cparams(("parallel", "arbitrary")),
        name="moba")(z3, z3, z3, cos2, sin2)


def _gelu_tanh(x):
    return x * (0.5 * (1.0 + jnp.tanh(np.sqrt(2.0 / np.pi) * (x + 0.044715 * (x * x * x)))))


def _nsa_compress_body(kc_ref, vc_ref, pe_ref, wk1_ref, wk2_ref, wv1_ref, wv2_ref, ko_ref, vo_ref):
    n_rows = kc_ref.shape[1] // CMP_STRIDE
    halves = CMP_LEN // CMP_STRIDE
    assert halves == 2

    def compress(x_ref, pe, w1_ref, w2_ref):
        ya = jnp.zeros((n_rows, CMP_HIDDEN), F32)
        yb = jnp.zeros((n_rows, CMP_HIDDEN), F32)
        for l in range(CMP_STRIDE):
            r = x_ref[0, pl.ds(l, n_rows, stride=CMP_STRIDE), :]
            la, lb = l, CMP_STRIDE + l
            ya = ya + jnp.dot((r + pe[la:la + 1, :]).astype(BF16), w1_ref[la * HEAD_DIM:(la + 1) * HEAD_DIM, :],
                              preferred_element_type=F32)
            yb = yb + jnp.dot((r + pe[lb:lb + 1, :]).astype(BF16), w1_ref[lb * HEAD_DIM:(lb + 1) * HEAD_DIM, :],
                              preferred_element_type=F32)
        pre = ya + pltpu.roll(yb, n_rows - 1, axis=0)
        return jnp.dot(_gelu_tanh(pre).astype(BF16), w2_ref[...], preferred_element_type=F32)

    ko_ref[0, 0] = compress(kc_ref, pe_ref[0], wk1_ref, wk2_ref)
    vo_ref[0, 0] = compress(vc_ref, pe_ref[1], wv1_ref, wv2_ref)


def _nsa_compress(z3, cmp_pos, wk1, wk2, wv1, wv2):
    b, s_len, _ = z3.shape
    n_rows = s_len // CMP_STRIDE
    kc_off = NSA_W // LANES
    vc_off = kc_off + NSA_GROUPS
    const2 = lambda bi, gi: (0, 0)
    in_specs = [pl.BlockSpec((1, s_len, HEAD_DIM), lambda bi, gi: (bi, 0, kc_off + gi)),
                pl.BlockSpec((1, s_len, HEAD_DIM), lambda bi, gi: (bi, 0, vc_off + gi)),
                pl.BlockSpec(cmp_pos.shape, lambda bi, gi: (0, 0, 0)),
                pl.BlockSpec(wk1.shape, const2), pl.BlockSpec(wk2.shape, const2),
                pl.BlockSpec(wv1.shape, const2), pl.BlockSpec(wv2.shape, const2)]
    out_spec = pl.BlockSpec((1, 1, n_rows, HEAD_DIM), lambda bi, gi: (bi, gi, 0, 0))
    out_sds = jax.ShapeDtypeStruct((b, NSA_GROUPS, n_rows, HEAD_DIM), F32)
    return pl.pallas_call(
        _nsa_compress_body, grid=(b, NSA_GROUPS), in_specs=in_specs,
        out_specs=[out_spec, out_spec], out_shape=[out_sds, out_sds],
        compiler_params=_cparams(("parallel", "parallel")), name="nsa_compress")(
            z3, z3, cmp_pos, wk1, wk2, wv1, wv2)


def _nsa_body(q_ref, ks_ref, vs_ref, kw_ref, vw_ref, kc_ref, vc_ref, gate_ref, cos_ref, sin_ref, ov_ref,
              o_ref, krs, vst, krw, vwt, sel_scr):
    i = pl.program_id(1)
    tq = NSA_TQ
    tk = NSA_TQ
    scale = HEAD_DIM ** -0.5
    n_sel_blk = ks_ref.shape[1] // SEL_BLOCK
    n_cmp = (ks_ref.shape[1] - CMP_LEN) // CMP_STRIDE + 1
    shift = SEL_BLOCK.bit_length() - 1
    assert 1 << shift == SEL_BLOCK and n_sel_blk <= LANES
    blk_per_tile = tk // SEL_BLOCK
    groups = [slice(g * HEAD_DIM, (g + 1) * HEAD_DIM) for g in range(NSA_GROUPS)]

    @pl.when(i == 0)
    def _():
        for g, cols in enumerate(groups):
            krs[:, cols] = _rotate(ks_ref[0, :, cols], cos_ref[...], sin_ref[...]).astype(BF16)
            krw[:, cols] = _rotate(kw_ref[0, :, cols], cos_ref[...], sin_ref[...]).astype(BF16)
            _store_transposed(vst.at[g], vs_ref[0, :, cols])
            _store_transposed(vwt.at[g], vw_ref[0, :, cols])

    t0 = pl.multiple_of(i * tq, tq)
    pos_row = t0 + lax.broadcasted_iota(jnp.int32, (1, tq), 1)
    cos_q = cos_ref[pl.ds(t0, tq), :]
    sin_q = sin_ref[pl.ds(t0, tq), :]

    n_col = lax.broadcasted_iota(jnp.int32, (LANES, 1), 0)
    cmp_end = jnp.where(n_col < n_cmp, n_col * CMP_STRIDE + (CMP_LEN - 1), jnp.iinfo(jnp.int32).max)
    cmp_ok = cmp_end <= pos_row
    blk_i = lax.broadcasted_iota(jnp.int32, (n_sel_blk, tq), 0)
    behind = (pos_row >> shift) - blk_i
    units = [(g, r) for g in range(NSA_GROUPS) for r in range(NSA_REP)]
    qr_heads, o_cmp = [], []
    for g in range(NSA_GROUPS):
        kc = kc_ref[0, g].astype(BF16)
        vc = vc_ref[0, g].astype(BF16)
        p_sum = jnp.zeros((LANES, tq), F32)
        for r in range(NSA_REP):
            head = (g * NSA_REP + r) * HEAD_DIM
            q = q_ref[0, :, head:head + HEAD_DIM] * scale
            qr_heads.append(_rotate(q * LOG2_E, cos_q, sin_q).astype(BF16))
            s = lax.dot_general(kc, q.astype(BF16), _NT, preferred_element_type=F32)
            s = jnp.where(cmp_ok, s, NEG_INF)
            e = jnp.where(cmp_ok, jnp.exp(s - jnp.max(s, axis=0, keepdims=True)), 0.0)
            p = e / jnp.maximum(jnp.sum(e, axis=0, keepdims=True), 1e-30)
            o_cmp.append(lax.dot_general(p.astype(BF16), vc, _TN, preferred_element_type=F32))
            p_sum = p_sum + p
        imp = jnp.dot(ov_ref[...], p_sum, precision=HIGHEST, preferred_element_type=F32)
        val = jnp.where(behind == 0, FORCED_SCORE, jnp.where(behind == 1, FORCED_SCORE, imp))
        val = jnp.where(blk_i == 0, FORCED_SCORE, val)
        val = jnp.where(behind >= 0, val, NEG_INF)
        picked = jnp.where(val > 0.5 * NEG_INF, _rank_before(val, n_sel_blk, 0), float(n_sel_blk)) < SEL_TOPN
        sel_scr[g] = jnp.where(picked, 0.0, NEG_INF)

    def sel_bias(g, k0, n_tiles):
        blk0 = (k0 >> shift) if isinstance(k0, int) else pl.multiple_of(k0 >> shift, blk_per_tile)
        return jnp.concatenate([jnp.broadcast_to(sel_scr[g, pl.ds(blk0 + b, 1), :], (SEL_BLOCK, tq))
                                for b in range(n_tiles * blk_per_tile)], axis=0)

    def sel_tiles(k0, n_tiles, carry):
        biases = [sel_bias(g, k0, n_tiles) for g in range(NSA_GROUPS)]
        return tuple(_flash_steps(qr_heads, [krs[pl.ds(k0, n_tiles * tk), groups[g]] for g, _ in units],
                                  [vst[g, :, pl.ds(k0, n_tiles * tk)] for g, _ in units],
                                  [biases[g] for g, _ in units], carry))

    causal_bias = jnp.where(lax.broadcasted_iota(jnp.int32, (tk, tq), 0)
                            <= lax.broadcasted_iota(jnp.int32, (tk, tq), 1), 0.0, NEG_INF)
    span = WINDOW + tq
    start = pl.multiple_of(jnp.maximum(t0 - WINDOW, 0), tq)
    gap = ((t0 - start) + lax.broadcasted_iota(jnp.int32, (span, tq), 1)
           - lax.broadcasted_iota(jnp.int32, (span, tq), 0))
    win_bias = jnp.where(gap >= 0, jnp.where(gap < WINDOW, 0.0, NEG_INF), NEG_INF)
    diag_biases = [sel_bias(g, t0, 1) + causal_bias for g in range(NSA_GROUPS)]
    n = len(units)
    group = _flash_steps(qr_heads * 2,
                         [krs[pl.ds(t0, tk), groups[g]] for g, _ in units]
                         + [krw[pl.ds(start, span), groups[g]] for g, _ in units],
                         [vst[g, :, pl.ds(t0, tk)] for g, _ in units]
                         + [vwt[g, :, pl.ds(start, span)] for g, _ in units],
                         [diag_biases[g] for g, _ in units] + [win_bias] * n, None)
    win_fin = group[2 * n:]

    odd = i % 2
    carry = lax.cond(odd == 1, lambda c: sel_tiles(0, 1, c), lambda c: c, tuple(group[:2 * n]))
    sel_fin = lax.fori_loop(0, i // 2, lambda p, c: sel_tiles(pl.multiple_of((odd + 2 * p) * tk, tk), 2, c), carry)

    gates = _sigmoid(gate_ref[0])
    gates_t = [gates[:, cols].T for cols in groups]
    for u, (g, r) in enumerate(units):
        o_t = (gates_t[g][3 * r + 1:3 * r + 2, :] * _flash_output(sel_fin[2 * u + 1])
               + gates_t[g][3 * r + 2:3 * r + 3, :] * _flash_output(win_fin[2 * u + 1]))
        g_cmp = gates[:, g * LANES + 3 * r:g * LANES + 3 * r + 1]
        o_ref[0, :, u * HEAD_DIM:(u + 1) * HEAD_DIM] = (g_cmp * o_cmp[u] + o_t.T).astype(o_ref.dtype)


def _nsa_attention(z3, k_cmp, v_cmp, cos2, sin2):
    b, s_len, _ = z3.shape
    assert s_len >= WINDOW + NSA_TQ and s_len % NSA_TQ == 0
    n_cmp = (s_len - CMP_LEN) // CMP_STRIDE + 1
    n_blk = s_len // SEL_BLOCK
    n_rows = k_cmp.shape[2]
    assert n_rows == LANES
    cmp_start = np.arange(n_rows) * CMP_STRIDE
    cmp_end = cmp_start + CMP_LEN - 1
    blk_lo = np.arange(LANES) * SEL_BLOCK
    overlap = ((cmp_start[:, None] <= blk_lo[None, :] + SEL_BLOCK - 1) & (cmp_end[:, None] >= blk_lo[None, :])
               & (np.arange(n_rows)[:, None] < n_cmp)).astype(np.float32)
    overlap_t = np.ascontiguousarray(overlap.T[:n_blk])
    base = NSA_W // NSA_KV_W

    def kv(off):
        return pl.BlockSpec((1, s_len, NSA_KV_W), lambda bi, i: (bi, 0, base + off))

    cmp_spec = pl.BlockSpec((1, NSA_GROUPS, n_rows, HEAD_DIM), lambda bi, i: (bi, 0, 0, 0))
    const2 = lambda bi, i: (0, 0)
    gate_w = NSA_GROUPS * LANES
    in_specs = [pl.BlockSpec((1, NSA_TQ, NSA_W), lambda bi, i: (bi, i, 0)),
                kv(2), kv(3), kv(4), kv(5), cmp_spec, cmp_spec,
                pl.BlockSpec((1, NSA_TQ, gate_w), lambda bi, i: (bi, i, ODD_GATE_BLK * LANES // gate_w)),
                pl.BlockSpec((s_len, HEAD_DIM), const2), pl.BlockSpec((s_len, HEAD_DIM), const2),
                pl.BlockSpec((n_blk, n_rows), const2)]
    vt_shape = (NSA_GROUPS, VT_ROWS, s_len)
    scratch = [pltpu.VMEM((s_len, NSA_KV_W), BF16), pltpu.VMEM(vt_shape, BF16),
               pltpu.VMEM((s_len, NSA_KV_W), BF16), pltpu.VMEM(vt_shape, BF16),
               pltpu.VMEM((NSA_GROUPS, n_blk, NSA_TQ), F32)]
    assert (ODD_GATE_BLK * LANES) % gate_w == 0
    return pl.pallas_call(
        _nsa_body, grid=(b, s_len // NSA_TQ), in_specs=in_specs,
        out_specs=pl.BlockSpec((1, NSA_TQ, NSA_W), lambda bi, i: (bi, i, 0)),
        out_shape=jax.ShapeDtypeStruct((b, s_len, NSA_W), BF16),
        scratch_shapes=scratch, compiler_params=_cparams(("parallel", "arbitrary")),
        name="nsa_attention")(z3, z3, z3, z3, z3, k_cmp, v_cmp, z3, cos2, sin2, jnp.asarray(overlap_t))


def _split_bf16(x):
    hi = x.astype(BF16)
    return hi, (x - hi.astype(F32)).astype(BF16)


def _proj_moe_body(*refs, n_act, final_norm):
    res_ref = refs[0]
    a_refs = refs[1:1 + n_act]
    wo_refs = refs[1 + n_act:1 + 2 * n_act]
    (g_ref, wr_ref, br_ref, w1_ref, w3_ref, w2_ref, fg_ref, o_ref,
     xn_scr, cw_scr, hd_scr) = refs[1 + 2 * n_act:]
    j = pl.program_id(1)
    tm = res_ref.shape[0]
    lane = lax.broadcasted_iota(jnp.int32, (tm, LANES), 1).astype(F32)
    neg = float("-inf")

    @pl.when(j == 0)
    def _():
        h = res_ref[...]
        for a_ref, wo_ref in zip(a_refs, wo_refs):
            h = h + jnp.dot(a_ref[...], wo_ref[...], preferred_element_type=F32)
        o_ref[...] = h
        xn = _rms(h, g_ref[...])
        xh, xl = _split_bf16(xn)
        xn_scr[...] = xh
        hi_terms = jnp.dot(xh, wr_ref[...], preferred_element_type=F32)
        logits = (hi_terms[:, :LANES] + (jnp.dot(xl, wr_ref[:, :LANES], preferred_element_type=F32)
                                         + hi_terms[:, LANES:])) + br_ref[...]
        is_g = lane < MOE_GROUPS
        gl = jnp.where(is_g, logits, neg)
        g_max = jnp.max(gl, axis=1, keepdims=True)
        g_w = 1.0 / jnp.sum(jnp.where(is_g, jnp.exp(gl - g_max), 0.0), axis=1, keepdims=True)
        g_top = jnp.min(jnp.where(gl == g_max, lane, float(LANES)), axis=1, keepdims=True)
        lo = MOE_GROUPS + MOE_EPG * g_top
        el = jnp.where(lane >= lo, jnp.where(lane < lo + MOE_EPG, logits, neg), neg)
        v1 = jnp.max(el, axis=1, keepdims=True)
        i1 = jnp.min(jnp.where(el == v1, lane, float(LANES)), axis=1, keepdims=True)
        el2 = jnp.where(lane == i1, neg, el)
        v2 = jnp.max(el2, axis=1, keepdims=True)
        i2 = jnp.min(jnp.where(el2 == v2, lane, float(LANES)), axis=1, keepdims=True)
        e2 = jnp.exp(v2 - v1)
        den = 1.0 + e2
        cw_scr[...] = (jnp.where(lane == i1, g_w / den, 0.0) + jnp.where(lane == i2, g_w * e2 / den, 0.0))

    xn = xn_scr[...]
    cw = cw_scr[...]
    for r in range(MOE_EPG):
        h1 = jnp.dot(xn, w1_ref[r], preferred_element_type=F32)
        h3 = jnp.dot(xn, w3_ref[r], preferred_element_type=F32)
        e_lane = (MOE_GROUPS + MOE_EPG * j + r).astype(F32)
        col = jnp.sum(jnp.where(lane == e_lane, cw, 0.0), axis=1, keepdims=True)
        hd_scr[:, r * MOE_HIDDEN:(r + 1) * MOE_HIDDEN] = ((h1 * _sigmoid(h1)) * h3 * col).astype(BF16)
    o_ref[...] += jnp.dot(hd_scr[...], w2_ref[...], preferred_element_type=F32)

    if final_norm:
        @pl.when(j == pl.num_programs(1) - 1)
        def _():
            o_ref[...] = _rms(o_ref[...], fg_ref[...])


def _proj_moe(res2d, acts, w_outs, gain, w_g, b_g, w_e, b_e, w1, w3, w2, final_gain, final_norm, tm=1024):
    t, d = res2d.shape
    n_act = len(acts)
    wr = jnp.zeros((d, LANES), F32).at[:, :MOE_GROUPS].set(w_g).at[:, MOE_GROUPS:MOE_GROUPS + MOE_EXPERTS].set(w_e)
    wr_cat = jnp.concatenate(_split_bf16(wr), axis=1)
    br = jnp.zeros((1, LANES), F32).at[0, :MOE_GROUPS].set(b_g).at[0, MOE_GROUPS:MOE_GROUPS + MOE_EXPERTS].set(b_e)
    tile = lambda i, j: (i, 0)
    const = lambda i, j: (0, 0)
    group_hidden = MOE_EPG * MOE_HIDDEN
    in_specs = [pl.BlockSpec((tm, d), tile)]
    in_specs += [pl.BlockSpec((tm, a.shape[1]), tile) for a in acts]
    in_specs += [pl.BlockSpec(w.shape, const) for w in w_outs]
    in_specs += [pl.BlockSpec((1, d), const),
                 pl.BlockSpec((d, 2 * LANES), const),
                 pl.BlockSpec((1, LANES), const),
                 pl.BlockSpec((MOE_EPG, d, MOE_HIDDEN), lambda i, j: (j, 0, 0)),
                 pl.BlockSpec((MOE_EPG, d, MOE_HIDDEN), lambda i, j: (j, 0, 0)),
                 pl.BlockSpec((group_hidden, d), lambda i, j: (j, 0)),
                 pl.BlockSpec((1, d), const)]
    scratch = [pltpu.VMEM((tm, d), BF16), pltpu.VMEM((tm, LANES), F32), pltpu.VMEM((tm, group_hidden), BF16)]
    return pl.pallas_call(
        functools.partial(_proj_moe_body, n_act=n_act, final_norm=final_norm),
        grid=(t // tm, MOE_GROUPS), in_specs=in_specs,
        out_specs=pl.BlockSpec((tm, d), tile),
        out_shape=jax.ShapeDtypeStruct((t, d), F32),
        scratch_shapes=scratch, compiler_params=_cparams(("parallel", "arbitrary")),
        name="out_proj_moe")(res2d, *acts, *w_outs, gain.reshape(1, d), wr_cat, br,
                             w1.astype(BF16), w3.astype(BF16), w2.astype(BF16).reshape(MOE_EXPERTS * MOE_HIDDEN, d),
                             final_gain.reshape(1, d))


def _rope_tables(s_len):
    half = HEAD_DIM // 2
    inv = ROPE_THETA ** (-(jnp.arange(half, dtype=F32) / half))
    ang = jnp.arange(s_len, dtype=F32)[:, None] * inv[None, :]
    cos, sin = jnp.cos(ang), jnp.sin(ang)
    return jnp.concatenate([cos, cos], axis=-1), jnp.concatenate([-sin, sin], axis=-1)


def _even_weights(w_in):
    n_if = 2 * MLSTM_HEADS
    a = 4 * MLSTM_W
    w_main = jnp.concatenate([w_in[:, :a], w_in[:, a + n_if:],
                              w_in[:, a:a + n_if], jnp.zeros((w_in.shape[0], LANES - n_if), w_in.dtype)], axis=1)
    w_if_t = jnp.concatenate([w_in[:, a:a + n_if].T, jnp.zeros((16 - n_if, w_in.shape[0]), w_in.dtype)], axis=0)
    return w_main.astype(BF16), w_if_t.astype(BF16)


def _odd_weights(w_in):
    a = NSA_W + 6 * NSA_KV_W
    per_group = NSA_REP * 3
    pad = jnp.zeros((w_in.shape[0], LANES - per_group), w_in.dtype)
    cols = [w_in[:, :a]]
    for g in range(NSA_GROUPS):
        cols += [w_in[:, a + g * per_group:a + (g + 1) * per_group], pad]
    return jnp.concatenate(cols, axis=1).astype(BF16)


def kernel(x, mix_norm_0, w_in_0, mlstm_conv_0, mlstm_gate_b_0, mlstm_head_norm_0, w_out_0, ffn_norm_0, router_group_0, router_group_b_0, router_expert_0, router_expert_b_0, moe_w1_0, moe_w3_0, moe_w2_0, mix_norm_1, w_in_1, nsa_cmp_pos_1, nsa_cmp_k1_1, nsa_cmp_k2_1, nsa_cmp_v1_1, nsa_cmp_v2_1, w_out_1, ffn_norm_1, router_group_1, router_group_b_1, router_expert_1, router_expert_b_1, moe_w1_1, moe_w3_1, moe_w2_1, final_norm):
    b, s_len, d = x.shape
    t = b * s_len
    cos2, sin2 = _rope_tables(s_len)
    x2d = x.reshape(t, d)

    w_main, w_if_t = _even_weights(w_in_0)
    z0, gt = _norm_matmul(x2d, mix_norm_0, w_main, w_if_t)
    z0 = z0.reshape(b, s_len, EVEN_N)
    h_m = _mlstm(z0, gt, mlstm_conv_0, mlstm_gate_b_0, mlstm_head_norm_0)
    o_b = _moba(z0, cos2, sin2)
    w_out = w_out_0.astype(BF16)
    h = _proj_moe(x2d, [h_m.reshape(t, MLSTM_W), o_b.reshape(t, MOBA_W)], [w_out[:MLSTM_W], w_out[MLSTM_W:]],
                  ffn_norm_0, router_group_0, router_group_b_0, router_expert_0, router_expert_b_0,
                  moe_w1_0, moe_w3_0, moe_w2_0, final_norm, False)

    z1 = _norm_matmul(h, mix_norm_1, _odd_weights(w_in_1)).reshape(b, s_len, ODD_N)
    k_cmp, v_cmp = _nsa_compress(z1, nsa_cmp_pos_1, nsa_cmp_k1_1.astype(BF16), nsa_cmp_k2_1.astype(BF16),
                                 nsa_cmp_v1_1.astype(BF16), nsa_cmp_v2_1.astype(BF16))
    o = _nsa_attention(z1, k_cmp, v_cmp, cos2, sin2)
    h = _proj_moe(h, [o.reshape(t, NSA_W)], [w_out_1.astype(BF16)],
                  ffn_norm_1, router_group_1, router_group_b_1, router_expert_1, router_expert_b_1,
                  moe_w1_1, moe_w3_1, moe_w2_1, final_norm, True)
    return h.reshape(b, s_len, d)
```

```python
import functools

import numpy as np
import jax
import jax.numpy as jnp
from jax import lax
from jax.experimental import pallas as pl
from jax.experimental.pallas import tpu as pltpu

F32 = jnp.float32
BF16 = jnp.bfloat16
HIGHEST = lax.Precision.HIGHEST

LANES = 128
BF16_SUBLANES = 16
D_MODEL = 1024
HEAD_DIM = 128
ROPE_THETA = 10000.0
NORM_EPS = 1e-6
NEG_INF = -1e30
FORCED_SCORE = 1e4

MLSTM_HEADS = 4
MLSTM_W = MLSTM_HEADS * HEAD_DIM
MLSTM_CHUNK = 64
MLSTM_CONV = 4
MLSTM_GATE_CAP = 15.0
MOBA_HEADS = 4
MOBA_W = MOBA_HEADS * HEAD_DIM
MOBA_BLOCK = 256
MOBA_TOPK = 3

NSA_HEADS = 8
NSA_GROUPS = 2
NSA_REP = NSA_HEADS // NSA_GROUPS
NSA_W = NSA_HEADS * HEAD_DIM
NSA_KV_W = NSA_GROUPS * HEAD_DIM
CMP_LEN = 32
CMP_STRIDE = 16
CMP_HIDDEN = 256
SEL_BLOCK = 64
SEL_TOPN = 8
WINDOW = 512
NSA_TQ = 256

MOE_GROUPS = 4
MOE_EPG = 4
MOE_EXPERTS = MOE_GROUPS * MOE_EPG
MOE_HIDDEN = D_MODEL // 4

EVEN_N = 4 * MLSTM_W + 3 * MOBA_W + LANES
EVEN_IF_BLK = (4 * MLSTM_W + 3 * MOBA_W) // LANES
ODD_GATE_BLK = (NSA_W + 6 * NSA_KV_W) // LANES
ODD_N = NSA_W + 6 * NSA_KV_W + LANES
NSA_GATES = 3

VMEM_LIMIT = 56 * 1024 * 1024
VT_ROWS = HEAD_DIM + BF16_SUBLANES
LOG2_E = float(np.log2(np.e))

_NT = (((1,), (1,)), ((), ()))
_TN = (((0,), (0,)), ((), ()))


def _cparams(sem):
    return pltpu.CompilerParams(dimension_semantics=sem, vmem_limit_bytes=VMEM_LIMIT)


def _rms(x, g):
    return x * lax.rsqrt(jnp.mean(x * x, axis=-1, keepdims=True) + NORM_EPS) * g


def _sigmoid(x):
    return 0.5 * jnp.tanh(0.5 * x) + 0.5


def _log_sigmoid(x):
    return -(jnp.maximum(-x, 0.0) + jnp.log1p(jnp.exp(-jnp.abs(x))))


def _rotate(x, cos2, sin2):
    return x * cos2 + pltpu.roll(x, HEAD_DIM // 2, axis=1) * sin2


def _flash_steps(qs, k_ts, vt_ts, biases, carries):
    n = len(qs)
    ss = [lax.dot_general(k_ts[u], qs[u], _NT, preferred_element_type=F32) + biases[u] for u in range(n)]
    ms, es, alphas, pvs = [], [], [], []

    def value_matmul(u):
        pv = jnp.dot(vt_ts[u], es[u], preferred_element_type=F32)
        if carries is not None:
            pv = alphas[u] * carries[2 * u + 1] + pv
        pvs.append(pv)

    for u in range(n):
        m_new = jnp.max(ss[u], axis=0, keepdims=True)
        if carries is not None:
            m_new = jnp.maximum(carries[2 * u], m_new)
            alphas.append(jnp.exp2(carries[2 * u] - m_new))
        ms.append(m_new)
        es.append(jnp.exp2(ss[u] - m_new).astype(BF16))
        if u > 0:
            value_matmul(u - 1)
    value_matmul(n - 1)
    out = []
    for u in range(n):
        out += [ms[u], pvs[u]]
    return out


def _flash_output(acc):
    return acc[:HEAD_DIM] / jnp.maximum(acc[HEAD_DIM:HEAD_DIM + 1], 1e-30)


def _store_transposed(dst_ref, x):
    n_rows, n_cols = x.shape
    for r in range(0, n_rows, LANES):
        dst_ref[:n_cols, r:r + LANES] = x[r:r + LANES, :].T.astype(dst_ref.dtype)
    dst_ref[n_cols:, :] = jnp.ones((dst_ref.shape[0] - n_cols, n_rows), dst_ref.dtype)


def _rank_before(v, n_valid, axis):
    idx = lax.broadcasted_iota(jnp.int32, v.shape, axis)
    rank = jnp.zeros(v.shape, F32)
    for m in range(n_valid):
        vm = v[m:m + 1, :] if axis == 0 else v[:, m:m + 1]
        tie = jnp.where(idx > m, 1.0, 0.0)
        rank = rank + jnp.where(vm > v, 1.0, jnp.where(vm == v, tie, 0.0))
    return rank


def _norm_matmul_body(*refs, n_chunk, with_t):
    if with_t:
        x_ref, g_ref, w_ref, wt_ref, o_ref, ot_ref = refs
    else:
        x_ref, g_ref, w_ref, o_ref = refs
    yb = _rms(x_ref[...], g_ref[...]).astype(BF16)
    n = w_ref.shape[1]
    for c0 in range(0, n, n_chunk):
        c1 = min(n, c0 + n_chunk)
        o_ref[:, c0:c1] = jnp.dot(yb, w_ref[:, c0:c1], preferred_element_type=F32)
    if with_t:
        ot_ref[...] = lax.dot_general(wt_ref[...], yb, _NT, preferred_element_type=F32)


def _norm_matmul(x2d, gain, w, wt=None, tm=512):
    t, d = x2d.shape
    n = w.shape[1]
    with_t = wt is not None
    in_specs = [pl.BlockSpec((tm, d), lambda i: (i, 0)),
                pl.BlockSpec((1, d), lambda i: (0, 0)),
                pl.BlockSpec((d, n), lambda i: (0, 0))]
    out_specs = [pl.BlockSpec((tm, n), lambda i: (i, 0))]
    out_shape = [jax.ShapeDtypeStruct((t, n), F32)]
    args = [x2d, gain.reshape(1, d), w]
    if with_t:
        r = wt.shape[0]
        in_specs.append(pl.BlockSpec((r, d), lambda i: (0, 0)))
        out_specs.append(pl.BlockSpec((r, tm), lambda i: (0, i)))
        out_shape.append(jax.ShapeDtypeStruct((r, t), F32))
        args.append(wt)
    outs = pl.pallas_call(
        functools.partial(_norm_matmul_body, n_chunk=512, with_t=with_t),
        grid=(t // tm,), in_specs=in_specs, out_specs=out_specs, out_shape=out_shape,
        compiler_params=_cparams(("parallel",)), name="norm_in_proj")(*args)
    return outs if with_t else outs[0]


def _chunk_cumsum(x, axis):
    idx = lax.broadcasted_iota(jnp.int32, x.shape, axis) % MLSTM_CHUNK
    d = 1
    while d < MLSTM_CHUNK:
        x = x + jnp.where(idx >= d, pltpu.roll(x, d, axis=axis), 0.0)
        d *= 2
    return x


MLSTM_HEADS_PER_STEP = 2


def _mlstm_body(q_ref, k_ref, v_ref, og_ref, if_ref, gt_ref, cwq_ref, cwk_ref, bcol_ref, brow_ref,
                gain_ref, out_ref, ks, qts, vts, hts, css, brs, lirs, gcol):
    hp = pl.program_id(1)
    s_len = q_ref.shape[1]
    n_local = q_ref.shape[2] // HEAD_DIM
    n_chunks = s_len // MLSTM_CHUNK
    L = MLSTM_CHUNK
    per_slab = LANES // L
    row = lax.broadcasted_iota(jnp.int32, (s_len, HEAD_DIM), 0)
    cols = [slice(a * HEAD_DIM, (a + 1) * HEAD_DIM) for a in range(n_local)]

    def conv_silu(x, w):
        acc = x * w[MLSTM_CONV - 1:MLSTM_CONV, :]
        for d in range(1, MLSTM_CONV):
            shifted = jnp.where(row >= d, pltpu.roll(x, d, axis=0), 0.0)
            acc = acc + shifted * w[MLSTM_CONV - 1 - d:MLSTM_CONV - d, :]
        return acc * _sigmoid(acc)

    def store_chunks_transposed(dst, x):
        for p in range(s_len // LANES):
            slab_t = x[p * LANES:(p + 1) * LANES, :].T
            for j in range(per_slab):
                dst[per_slab * p + j] = slab_t[:, j * L:(j + 1) * L].astype(dst.dtype)

    lane = lax.broadcasted_iota(jnp.int32, gcol.shape, 1)

    @pl.when(hp == 0)
    def _():
        pre = if_ref[0] + bcol_ref[...]
        pre = MLSTM_GATE_CAP * jnp.tanh(pre / MLSTM_GATE_CAP)
        gcol[...] = jnp.where(lane < MLSTM_HEADS, pre, _chunk_cumsum(_log_sigmoid(pre), 0))

    pr = gt_ref[...] + brow_ref[...]
    pr = MLSTM_GATE_CAP * jnp.tanh(pr / MLSTM_GATE_CAP)
    sub = lax.broadcasted_iota(jnp.int32, pr.shape, 0)
    b_rows = _chunk_cumsum(_log_sigmoid(pr), 1)
    g_all = gcol[...]

    gains = []
    for a in range(n_local):
        h = hp * n_local + a
        store_chunks_transposed(qts.at[a], conv_silu(q_ref[0, :, cols[a]], cwq_ref[:, cols[a]]) * (HEAD_DIM ** -0.5))
        store_chunks_transposed(vts.at[a], v_ref[0, :, cols[a]])
        ks[:, cols[a]] = conv_silu(k_ref[0, :, cols[a]], cwk_ref[:, cols[a]]).astype(BF16)
        css[a] = jnp.sum(jnp.where(lane == h, g_all, jnp.where(lane == h + MLSTM_HEADS, -g_all, 0.0)),
                         axis=1, keepdims=True)
        li_row = jnp.sum(jnp.where(sub == h, pr, 0.0), axis=0, keepdims=True)
        b_row = jnp.sum(jnp.where(sub == h + MLSTM_HEADS, b_rows, 0.0), axis=0, keepdims=True)
        for c in range(n_chunks):
            brs[a, c] = b_row[:, c * L:(c + 1) * L]
            lirs[a, c] = li_row[:, c * L:(c + 1) * L]
        gains.append(jnp.broadcast_to(gain_ref[a * HEAD_DIM:(a + 1) * HEAD_DIM, :], (HEAD_DIM, L)))

    tri = lax.broadcasted_iota(jnp.int32, (L, L), 0) <= lax.broadcasted_iota(jnp.int32, (L, L), 1)

    group_size = 16
    assert n_chunks % group_size == 0

    def group(gi, carry):
        states = [list(carry[3 * a:3 * a + 3]) for a in range(n_local)]
        cs = [gi * group_size + j for j in range(group_size)]
        r0s = [pl.multiple_of(c * L, L) for c in cs]
        units = [(a, j) for a in range(n_local) for j in range(group_size)]
        ks_ = {(a, j): ks[pl.ds(r0s[j], L), cols[a]] for a, j in units}
        q_ts = {(a, j): qts[a, cs[j]] for a, j in units}
        v_ts = {(a, j): vts[a, cs[j]] for a, j in units}
        b_rs = {(a, j): brs[a, cs[j]] for a, j in units}
        kvs, ksums, kqs, g_maxs, b_lasts = {}, {}, {}, {}, {}
        for u in units:
            a, j = u
            b_last = b_rs[u][:, L - 1:L]
            g = b_last - b_rs[u] + lirs[a, cs[j]]
            g_max = jnp.max(g, axis=1, keepdims=True)
            w = jnp.exp(g - g_max)
            kvs[u] = jnp.dot((v_ts[u] * w).astype(BF16), ks_[u], preferred_element_type=F32)
            ksums[u] = jnp.dot(jnp.broadcast_to(w, (BF16_SUBLANES, L)).astype(BF16), ks_[u],
                               preferred_element_type=F32)[:1]
            kqs[u] = jnp.dot(ks_[u], q_ts[u], preferred_element_type=F32)
            g_maxs[u] = g_max
            b_lasts[u] = b_last
        c_ins, n_ins, m_ins = {}, {}, {}
        for u in units:
            a, j = u
            c_state, n_state, m_state = states[a]
            c_ins[u] = c_state.astype(BF16)
            n_ins[u] = jnp.broadcast_to(n_state, (BF16_SUBLANES, HEAD_DIM)).astype(BF16)
            m_ins[u] = m_state
            m_new = jnp.maximum(b_lasts[u] + m_state, g_maxs[u])
            sa = jnp.exp(b_lasts[u] + m_state - m_new)
            cc = jnp.exp(g_maxs[u] - m_new)
            states[a] = [sa * c_state + cc * kvs[u], sa * n_state + cc * ksums[u], m_new]
        c_qs = {u: jnp.dot(c_ins[u], q_ts[u], preferred_element_type=F32) for u in units}
        n_qs = {u: jnp.dot(n_ins[u], q_ts[u], preferred_element_type=F32)[:1] for u in units}
        ss, m_ts, w_inters = {}, {}, {}
        for u in units:
            a, j = u
            dmat = jnp.where(tri, b_rs[u] + css[a, pl.ds(r0s[j], L), :], NEG_INF)
            inter = b_rs[u] + m_ins[u]
            m_t = jnp.maximum(inter, jnp.max(dmat, axis=0, keepdims=True))
            ss[u] = kqs[u] * jnp.exp(dmat - m_t)
            m_ts[u] = m_t
            w_inters[u] = jnp.exp(inter - m_t)
        svs = {u: jnp.dot(v_ts[u].astype(BF16), ss[u].astype(BF16), preferred_element_type=F32) for u in units}
        for u in units:
            a, j = u
            num = svs[u] + w_inters[u] * c_qs[u]
            den = jnp.sum(ss[u], axis=0, keepdims=True) + w_inters[u] * n_qs[u]
            ht = num / jnp.maximum(jnp.abs(den), jnp.exp(-m_ts[u]))
            hts[a, cs[j]] = ht * lax.rsqrt(jnp.mean(ht * ht, axis=0, keepdims=True) + NORM_EPS) * gains[a]
        return tuple(x for st in states for x in st)

    init = (jnp.zeros((HEAD_DIM, HEAD_DIM), F32), jnp.zeros((1, HEAD_DIM), F32), jnp.zeros((1, 1), F32)) * n_local
    lax.fori_loop(0, n_chunks // group_size, group, init)

    for a in range(n_local):
        for p in range(s_len // LANES):
            slab_t = jnp.concatenate([hts[a, per_slab * p + j] for j in range(per_slab)], axis=1)
            rows = slice(p * LANES, (p + 1) * LANES)
            out_ref[0, rows, cols[a]] = (slab_t.T * _sigmoid(og_ref[0, rows, cols[a]])).astype(out_ref.dtype)


def _mlstm(z3, gt, conv_w, gate_b, head_gain):
    b, s_len, _ = z3.shape
    n_chunks = s_len // MLSTM_CHUNK
    gr = gt.shape[0]
    bcol = jnp.zeros((1, LANES), F32).at[0, :2 * MLSTM_HEADS].set(gate_b)
    brow = jnp.zeros((gr, 1), F32).at[:2 * MLSTM_HEADS, 0].set(gate_b)
    n_local = MLSTM_HEADS_PER_STEP
    n_steps = MLSTM_HEADS // n_local
    width = n_local * HEAD_DIM

    def col(off):
        return pl.BlockSpec((1, s_len, width), lambda bi, hi: (bi, 0, off * n_steps + hi))

    in_specs = [col(0), col(1), col(2), col(3),
                pl.BlockSpec((1, s_len, LANES), lambda bi, hi: (bi, 0, EVEN_IF_BLK)),
                pl.BlockSpec((gr, s_len), lambda bi, hi: (0, bi)),
                pl.BlockSpec((MLSTM_CONV, width), lambda bi, hi: (0, hi)),
                pl.BlockSpec((MLSTM_CONV, width), lambda bi, hi: (0, n_steps + hi)),
                pl.BlockSpec((1, LANES), lambda bi, hi: (0, 0)),
                pl.BlockSpec((gr, 1), lambda bi, hi: (0, 0)),
                pl.BlockSpec((width, 1), lambda bi, hi: (hi, 0))]
    chunk_t = (n_local, n_chunks, HEAD_DIM, MLSTM_CHUNK)
    rows_t = (n_local, n_chunks, 1, MLSTM_CHUNK)
    scratch = [pltpu.VMEM((s_len, width), BF16),
               pltpu.VMEM(chunk_t, BF16), pltpu.VMEM(chunk_t, F32), pltpu.VMEM(chunk_t, F32),
               pltpu.VMEM((n_local, s_len, 1), F32),
               pltpu.VMEM(rows_t, F32), pltpu.VMEM(rows_t, F32),
               pltpu.VMEM((s_len, LANES), F32)]
    return pl.pallas_call(
        _mlstm_body, grid=(b, n_steps), in_specs=in_specs,
        out_specs=pl.BlockSpec((1, s_len, width), lambda bi, hi: (bi, 0, hi)),
        out_shape=jax.ShapeDtypeStruct((b, s_len, MLSTM_W), BF16),
        scratch_shapes=scratch, compiler_params=_cparams(("parallel", "arbitrary")),
        name="mlstm")(z3, z3, z3, z3, z3, gt, conv_w, conv_w, bcol, brow, head_gain.reshape(MLSTM_W, 1))


MOBA_ROWS = 2


def _moba_body(q_ref, k_ref, v_ref, cos_ref, sin_ref, o_ref, kr_scr, vt_scr, km_scr, sel_scr):
    i = pl.program_id(1)
    n_rows, s_len = k_ref.shape[0], k_ref.shape[1]
    bs = MOBA_BLOCK
    nb = s_len // bs
    heads = [slice(h * HEAD_DIM, (h + 1) * HEAD_DIM) for h in range(MOBA_HEADS)]
    units = [(b, h) for b in range(n_rows) for h in range(MOBA_HEADS)]

    @pl.when(i == 0)
    def _():
        for u, (b, h) in enumerate(units):
            _store_transposed(vt_scr.at[u], v_ref[b, :, heads[h]])
            kr = _rotate(k_ref[b, :, heads[h]], cos_ref[...], sin_ref[...])
            kr_scr[b, :, heads[h]] = kr.astype(BF16)
            rows = [jnp.sum(kr[n * bs:(n + 1) * bs, :], axis=0, keepdims=True) / float(bs) for n in range(nb)]
            rows.append(jnp.zeros((km_scr.shape[1] - nb, HEAD_DIM), F32))
            km_scr[u] = jnp.concatenate(rows, axis=0)

    t0 = pl.multiple_of(i * bs, bs)
    cos_q = cos_ref[pl.ds(t0, bs), :]
    sin_q = sin_ref[pl.ds(t0, bs), :]
    blk = lax.broadcasted_iota(jnp.int32, (km_scr.shape[1], bs), 0)
    causal_bias = jnp.where(lax.broadcasted_iota(jnp.int32, (bs, bs), 0)
                            <= lax.broadcasted_iota(jnp.int32, (bs, bs), 1), 0.0, NEG_INF)

    qbs = []
    for u, (b, h) in enumerate(units):
        qr = _rotate(q_ref[b, :, heads[h]], cos_q, sin_q)
        gate_t = lax.dot_general(km_scr[u], qr, _NT, precision=HIGHEST, preferred_element_type=F32)
        val = jnp.where(blk < i, gate_t, NEG_INF)
        picked = jnp.where(val > 0.5 * NEG_INF, _rank_before(val, nb, 0), float(nb)) < MOBA_TOPK
        sel_scr[u] = jnp.where(picked, 0.0, NEG_INF)
        qbs.append((qr * (HEAD_DIM ** -0.5 * LOG2_E)).astype(BF16))

    def key_tiles(k0, n_blocks):
        return ([kr_scr[b, pl.ds(k0, n_blocks * bs), heads[h]] for b, h in units],
                [vt_scr[u, :, pl.ds(k0, n_blocks * bs)] for u in range(len(units))])

    init = tuple(_flash_steps(qbs, *key_tiles(t0, 1), [causal_bias] * len(units), None))

    def past_blocks(j, n_blocks, carry):
        biases = [jnp.concatenate([jnp.broadcast_to(sel_scr[u, pl.ds(j + d, 1), :], (bs, bs))
                                   for d in range(n_blocks)], axis=0) for u in range(len(units))]
        k0 = j * bs if isinstance(j, int) else pl.multiple_of(j * bs, bs)
        return tuple(_flash_steps(qbs, *key_tiles(k0, n_blocks), biases, carry))

    odd = i % 2
    carry = lax.cond(odd == 1, lambda c: past_blocks(0, 1, c), lambda c: c, init)
    fin = lax.fori_loop(0, i // 2, lambda p, c: past_blocks(odd + 2 * p, 2, c), carry)
    for u, (b, h) in enumerate(units):
        o_ref[b, :, heads[h]] = _flash_output(fin[2 * u + 1]).T.astype(o_ref.dtype)


def _moba(z3, cos2, sin2):
    b, s_len, _ = z3.shape
    nb = s_len // MOBA_BLOCK
    q_off = 4 * MLSTM_W // MOBA_W
    rows = MOBA_ROWS if b % MOBA_ROWS == 0 else 1
    n_units = rows * MOBA_HEADS

    def kv(off):
        return pl.BlockSpec((rows, s_len, MOBA_W), lambda bi, i: (bi, 0, off))

    in_specs = [pl.BlockSpec((rows, MOBA_BLOCK, MOBA_W), lambda bi, i: (bi, i, q_off)),
                kv(q_off + 1), kv(q_off + 2),
                pl.BlockSpec((s_len, HEAD_DIM), lambda bi, i: (0, 0)),
                pl.BlockSpec((s_len, HEAD_DIM), lambda bi, i: (0, 0))]
    scratch = [pltpu.VMEM((rows, s_len, MOBA_W), BF16),
               pltpu.VMEM((n_units, VT_ROWS, s_len), BF16),
               pltpu.VMEM((n_units, BF16_SUBLANES, HEAD_DIM), F32),
               pltpu.VMEM((n_units, BF16_SUBLANES, MOBA_BLOCK), F32)]
    return pl.pallas_call(
        _moba_body, grid=(b // rows, nb), in_specs=in_specs,
        out_specs=pl.BlockSpec((rows, MOBA_BLOCK, MOBA_W), lambda bi, i: (bi, i, 0)),
        out_shape=jax.ShapeDtypeStruct((b, s_len, MOBA_W), BF16),
        scratch_shapes=scratch, compiler_params=_cparams(("parallel", "arbitrary")),
        name="moba")(z3, z3, z3, cos2, sin2)


def _gelu_tanh(x):
    return x * (0.5 * (1.0 + jnp.tanh(np.sqrt(2.0 / np.pi) * (x + 0.044715 * (x * x * x)))))


def _nsa_compress_body(kc_ref, vc_ref, pe_ref, wk1_ref, wk2_ref, wv1_ref, wv2_ref, ko_ref, vo_ref):
    n_rows = kc_ref.shape[1] // CMP_STRIDE
    halves = CMP_LEN // CMP_STRIDE
    assert halves == 2

    def compress(x_ref, pe, w1_ref, w2_ref):
        ya = jnp.zeros((n_rows, CMP_HIDDEN), F32)
        yb = jnp.zeros((n_rows, CMP_HIDDEN), F32)
        for l in range(CMP_STRIDE):
            r = x_ref[0, pl.ds(l, n_rows, stride=CMP_STRIDE), :]
            la, lb = l, CMP_STRIDE + l
            ya = ya + jnp.dot((r + pe[la:la + 1, :]).astype(BF16), w1_ref[la * HEAD_DIM:(la + 1) * HEAD_DIM, :],
                              preferred_element_type=F32)
            yb = yb + jnp.dot((r + pe[lb:lb + 1, :]).astype(BF16), w1_ref[lb * HEAD_DIM:(lb + 1) * HEAD_DIM, :],
                              preferred_element_type=F32)
        pre = ya + pltpu.roll(yb, n_rows - 1, axis=0)
        return jnp.dot(_gelu_tanh(pre).astype(BF16), w2_ref[...], preferred_element_type=F32)

    ko_ref[0, 0] = compress(kc_ref, pe_ref[0], wk1_ref, wk2_ref)
    vo_ref[0, 0] = compress(vc_ref, pe_ref[1], wv1_ref, wv2_ref)


def _nsa_compress(z3, cmp_pos, wk1, wk2, wv1, wv2):
    b, s_len, _ = z3.shape
    n_rows = s_len // CMP_STRIDE
    kc_off = NSA_W // LANES
    vc_off = kc_off + NSA_GROUPS
    const2 = lambda bi, gi: (0, 0)
    in_specs = [pl.BlockSpec((1, s_len, HEAD_DIM), lambda bi, gi: (bi, 0, kc_off + gi)),
                pl.BlockSpec((1, s_len, HEAD_DIM), lambda bi, gi: (bi, 0, vc_off + gi)),
                pl.BlockSpec(cmp_pos.shape, lambda bi, gi: (0, 0, 0)),
                pl.BlockSpec(wk1.shape, const2), pl.BlockSpec(wk2.shape, const2),
                pl.BlockSpec(wv1.shape, const2), pl.BlockSpec(wv2.shape, const2)]
    out_spec = pl.BlockSpec((1, 1, n_rows, HEAD_DIM), lambda bi, gi: (bi, gi, 0, 0))
    out_sds = jax.ShapeDtypeStruct((b, NSA_GROUPS, n_rows, HEAD_DIM), F32)
    return pl.pallas_call(
        _nsa_compress_body, grid=(b, NSA_GROUPS), in_specs=in_specs,
        out_specs=[out_spec, out_spec], out_shape=[out_sds, out_sds],
        compiler_params=_cparams(("parallel", "parallel")), name="nsa_compress")(
            z3, z3, cmp_pos, wk1, wk2, wv1, wv2)


def _nsa_body(q_ref, ks_ref, vs_ref, kw_ref, vw_ref, kc_ref, vc_ref, gate_ref, cos_ref, sin_ref, ov_ref,
              o_ref, krs, vst, krw, vwt, sel_scr):
    i = pl.program_id(1)
    tq = NSA_TQ
    tk = NSA_TQ
    scale = HEAD_DIM ** -0.5
    n_sel_blk = ks_ref.shape[1] // SEL_BLOCK
    n_cmp = (ks_ref.shape[1] - CMP_LEN) // CMP_STRIDE + 1
    shift = SEL_BLOCK.bit_length() - 1
    assert 1 << shift == SEL_BLOCK and n_sel_blk <= LANES
    blk_per_tile = tk // SEL_BLOCK
    groups = [slice(g * HEAD_DIM, (g + 1) * HEAD_DIM) for g in range(NSA_GROUPS)]

    @pl.when(i == 0)
    def _():
        for g, cols in enumerate(groups):
            krs[:, cols] = _rotate(ks_ref[0, :, cols], cos_ref[...], sin_ref[...]).astype(BF16)
            krw[:, cols] = _rotate(kw_ref[0, :, cols], cos_ref[...], sin_ref[...]).astype(BF16)
            _store_transposed(vst.at[g], vs_ref[0, :, cols])
            _store_transposed(vwt.at[g], vw_ref[0, :, cols])

    t0 = pl.multiple_of(i * tq, tq)
    pos_row = t0 + lax.broadcasted_iota(jnp.int32, (1, tq), 1)
    cos_q = cos_ref[pl.ds(t0, tq), :]
    sin_q = sin_ref[pl.ds(t0, tq), :]

    n_col = lax.broadcasted_iota(jnp.int32, (LANES, 1), 0)
    cmp_end = jnp.where(n_col < n_cmp, n_col * CMP_STRIDE + (CMP_LEN - 1), jnp.iinfo(jnp.int32).max)
    cmp_ok = cmp_end <= pos_row
    blk_i = lax.broadcasted_iota(jnp.int32, (n_sel_blk, tq), 0)
    behind = (pos_row >> shift) - blk_i
    units = [(g, r) for g in range(NSA_GROUPS) for r in range(NSA_REP)]
    qr_heads, o_cmp = [], []
    for g in range(NSA_GROUPS):
        kc = kc_ref[0, g].astype(BF16)
        vc = vc_ref[0, g].astype(BF16)
        p_sum = jnp.zeros((LANES, tq), F32)
        for r in range(NSA_REP):
            head = (g * NSA_REP + r) * HEAD_DIM
            q = q_ref[0, :, head:head + HEAD_DIM] * scale
            qr_heads.append(_rotate(q * LOG2_E, cos_q, sin_q).astype(BF16))
            s = lax.dot_general(kc, q.astype(BF16), _NT, preferred_element_type=F32)
            s = jnp.where(cmp_ok, s, NEG_INF)
            e = jnp.where(cmp_ok, jnp.exp(s - jnp.max(s, axis=0, keepdims=True)), 0.0)
            p = e / jnp.maximum(jnp.sum(e, axis=0, keepdims=True), 1e-30)
            o_cmp.append(lax.dot_general(p.astype(BF16), vc, _TN, preferred_element_type=F32))
            p_sum = p_sum + p
        imp = jnp.dot(ov_ref[...], p_sum, precision=HIGHEST, preferred_element_type=F32)
        val = jnp.where(behind == 0, FORCED_SCORE, jnp.where(behind == 1, FORCED_SCORE, imp))
        val = jnp.where(blk_i == 0, FORCED_SCORE, val)
        val = jnp.where(behind >= 0, val, NEG_INF)
        picked = jnp.where(val > 0.5 * NEG_INF, _rank_before(val, n_sel_blk, 0), float(n_sel_blk)) < SEL_TOPN
        sel_scr[g] = jnp.where(picked, 0.0, NEG_INF)

    def sel_bias(g, k0, n_tiles):
        blk0 = (k0 >> shift) if isinstance(k0, int) else pl.multiple_of(k0 >> shift, blk_per_tile)
        return jnp.concatenate([jnp.broadcast_to(sel_scr[g, pl.ds(blk0 + b, 1), :], (SEL_BLOCK, tq))
                                for b in range(n_tiles * blk_per_tile)], axis=0)

    def sel_tiles(k0, n_tiles, carry):
        biases = [sel_bias(g, k0, n_tiles) for g in range(NSA_GROUPS)]
        return tuple(_flash_steps(qr_heads, [krs[pl.ds(k0, n_tiles * tk), groups[g]] for g, _ in units],
                                  [vst[g, :, pl.ds(k0, n_tiles * tk)] for g, _ in units],
                                  [biases[g] for g, _ in units], carry))

    causal_bias = jnp.where(lax.broadcasted_iota(jnp.int32, (tk, tq), 0)
                            <= lax.broadcasted_iota(jnp.int32, (tk, tq), 1), 0.0, NEG_INF)
    span = WINDOW + tq
    start = pl.multiple_of(jnp.maximum(t0 - WINDOW, 0), tq)
    gap = ((t0 - start) + lax.broadcasted_iota(jnp.int32, (span, tq), 1)
           - lax.broadcasted_iota(jnp.int32, (span, tq), 0))
    win_bias = jnp.where(gap >= 0, jnp.where(gap < WINDOW, 0.0, NEG_INF), NEG_INF)
    diag_biases = [sel_bias(g, t0, 1) + causal_bias for g in range(NSA_GROUPS)]
    n = len(units)
    group = _flash_steps(qr_heads * 2,
                         [krs[pl.ds(t0, tk), groups[g]] for g, _ in units]
                         + [krw[pl.ds(start, span), groups[g]] for g, _ in units],
                         [vst[g, :, pl.ds(t0, tk)] for g, _ in units]
                         + [vwt[g, :, pl.ds(start, span)] for g, _ in units],
                         [diag_biases[g] for g, _ in units] + [win_bias] * n, None)
    win_fin = group[2 * n:]

    odd = i % 2
    carry = lax.cond(odd == 1, lambda c: sel_tiles(0, 1, c), lambda c: c, tuple(group[:2 * n]))
    sel_fin = lax.fori_loop(0, i // 2, lambda p, c: sel_tiles(pl.multiple_of((odd + 2 * p) * tk, tk), 2, c), carry)

    gates = _sigmoid(gate_ref[0])
    gates_t = gates.T
    for u in range(n):
        c0 = NSA_GATES * u
        o_t = (gates_t[c0 + 1:c0 + 2, :] * _flash_output(sel_fin[2 * u + 1])
               + gates_t[c0 + 2:c0 + 3, :] * _flash_output(win_fin[2 * u + 1]))
        o_ref[0, :, u * HEAD_DIM:(u + 1) * HEAD_DIM] = (gates[:, c0:c0 + 1] * o_cmp[u] + o_t.T).astype(o_ref.dtype)


def _nsa_attention(z3, k_cmp, v_cmp, cos2, sin2):
    b, s_len, _ = z3.shape
    assert s_len >= WINDOW + NSA_TQ and s_len % NSA_TQ == 0
    n_cmp = (s_len - CMP_LEN) // CMP_STRIDE + 1
    n_blk = s_len // SEL_BLOCK
    n_rows = k_cmp.shape[2]
    assert n_rows == LANES
    cmp_start = np.arange(n_rows) * CMP_STRIDE
    cmp_end = cmp_start + CMP_LEN - 1
    blk_lo = np.arange(LANES) * SEL_BLOCK
    overlap = ((cmp_start[:, None] <= blk_lo[None, :] + SEL_BLOCK - 1) & (cmp_end[:, None] >= blk_lo[None, :])
               & (np.arange(n_rows)[:, None] < n_cmp)).astype(np.float32)
    overlap_t = np.ascontiguousarray(overlap.T[:n_blk])
    base = NSA_W // NSA_KV_W

    def kv(off):
        return pl.BlockSpec((1, s_len, NSA_KV_W), lambda bi, i: (bi, 0, base + off))

    cmp_spec = pl.BlockSpec((1, NSA_GROUPS, n_rows, HEAD_DIM), lambda bi, i: (bi, 0, 0, 0))
    const2 = lambda bi, i: (0, 0)
    in_specs = [pl.BlockSpec((1, NSA_TQ, NSA_W), lambda bi, i: (bi, i, 0)),
                kv(2), kv(3), kv(4), kv(5), cmp_spec, cmp_spec,
                pl.BlockSpec((1, NSA_TQ, LANES), lambda bi, i: (bi, i, ODD_GATE_BLK)),
                pl.BlockSpec((s_len, HEAD_DIM), const2), pl.BlockSpec((s_len, HEAD_DIM), const2),
                pl.BlockSpec((n_blk, n_rows), const2)]
    vt_shape = (NSA_GROUPS, VT_ROWS, s_len)
    scratch = [pltpu.VMEM((s_len, NSA_KV_W), BF16), pltpu.VMEM(vt_shape, BF16),
               pltpu.VMEM((s_len, NSA_KV_W), BF16), pltpu.VMEM(vt_shape, BF16),
               pltpu.VMEM((NSA_GROUPS, n_blk, NSA_TQ), F32)]
    return pl.pallas_call(
        _nsa_body, grid=(b, s_len // NSA_TQ), in_specs=in_specs,
        out_specs=pl.BlockSpec((1, NSA_TQ, NSA_W), lambda bi, i: (bi, i, 0)),
        out_shape=jax.ShapeDtypeStruct((b, s_len, NSA_W), BF16),
        scratch_shapes=scratch, compiler_params=_cparams(("parallel", "arbitrary")),
        name="nsa_attention")(z3, z3, z3, z3, z3, k_cmp, v_cmp, z3, cos2, sin2, jnp.asarray(overlap_t))


def _split_bf16(x):
    hi = x.astype(BF16)
    return hi, (x - hi.astype(F32)).astype(BF16)


def _proj_moe_body(*refs, n_act, final_norm):
    res_ref = refs[0]
    a_refs = refs[1:1 + n_act]
    wo_refs = refs[1 + n_act:1 + 2 * n_act]
    (g_ref, wr_ref, br_ref, w1_ref, w3_ref, w2_ref, fg_ref, o_ref,
     xn_scr, cw_scr, hd_scr) = refs[1 + 2 * n_act:]
    j = pl.program_id(1)
    tm = res_ref.shape[0]
    lane = lax.broadcasted_iota(jnp.int32, (tm, LANES), 1).astype(F32)
    neg = float("-inf")

    @pl.when(j == 0)
    def _():
        h = res_ref[...]
        for a_ref, wo_ref in zip(a_refs, wo_refs):
            h = h + jnp.dot(a_ref[...], wo_ref[...], preferred_element_type=F32)
        o_ref[...] = h
        xn = _rms(h, g_ref[...])
        xh, xl = _split_bf16(xn)
        xn_scr[...] = xh
        hi_terms = jnp.dot(xh, wr_ref[...], preferred_element_type=F32)
        logits = (hi_terms[:, :LANES] + (jnp.dot(xl, wr_ref[:, :LANES], preferred_element_type=F32)
                                         + hi_terms[:, LANES:])) + br_ref[...]
        is_g = lane < MOE_GROUPS
        gl = jnp.where(is_g, logits, neg)
        g_max = jnp.max(gl, axis=1, keepdims=True)
        g_w = 1.0 / jnp.sum(jnp.where(is_g, jnp.exp(gl - g_max), 0.0), axis=1, keepdims=True)
        g_top = jnp.min(jnp.where(gl == g_max, lane, float(LANES)), axis=1, keepdims=True)
        lo = MOE_GROUPS + MOE_EPG * g_top
        el = jnp.where(lane >= lo, jnp.where(lane < lo + MOE_EPG, logits, neg), neg)
        v1 = jnp.max(el, axis=1, keepdims=True)
        i1 = jnp.min(jnp.where(el == v1, lane, float(LANES)), axis=1, keepdims=True)
        el2 = jnp.where(lane == i1, neg, el)
        v2 = jnp.max(el2, axis=1, keepdims=True)
        i2 = jnp.min(jnp.where(el2 == v2, lane, float(LANES)), axis=1, keepdims=True)
        e2 = jnp.exp(v2 - v1)
        den = 1.0 + e2
        cw_scr[...] = (jnp.where(lane == i1, g_w / den, 0.0) + jnp.where(lane == i2, g_w * e2 / den, 0.0))

    xn = xn_scr[...]
    cw = cw_scr[...]
    for r in range(MOE_EPG):
        h1 = jnp.dot(xn, w1_ref[r].astype(BF16), preferred_element_type=F32)
        h3 = jnp.dot(xn, w3_ref[r], preferred_element_type=F32)
        e_lane = (MOE_GROUPS + MOE_EPG * j + r).astype(F32)
        col = jnp.sum(jnp.where(lane == e_lane, cw, 0.0), axis=1, keepdims=True)
        hd_scr[:, r * MOE_HIDDEN:(r + 1) * MOE_HIDDEN] = ((h1 * _sigmoid(h1)) * h3 * col).astype(BF16)
    o_ref[...] += jnp.dot(hd_scr[...], w2_ref[...].astype(BF16), preferred_element_type=F32)

    if final_norm:
        @pl.when(j == pl.num_programs(1) - 1)
        def _():
            o_ref[...] = _rms(o_ref[...], fg_ref[...])


def _proj_moe(res2d, acts, w_outs, gain, w_g, b_g, w_e, b_e, w1, w3, w2, final_gain, final_norm, tm=1024):
    t, d = res2d.shape
    n_act = len(acts)
    wr = jnp.zeros((d, LANES), F32).at[:, :MOE_GROUPS].set(w_g).at[:, MOE_GROUPS:MOE_GROUPS + MOE_EXPERTS].set(w_e)
    wr_cat = jnp.concatenate(_split_bf16(wr), axis=1)
    br = jnp.zeros((1, LANES), F32).at[0, :MOE_GROUPS].set(b_g).at[0, MOE_GROUPS:MOE_GROUPS + MOE_EXPERTS].set(b_e)
    tile = lambda i, j: (i, 0)
    const = lambda i, j: (0, 0)
    group_hidden = MOE_EPG * MOE_HIDDEN
    in_specs = [pl.BlockSpec((tm, d), tile)]
    in_specs += [pl.BlockSpec((tm, a.shape[1]), tile) for a in acts]
    in_specs += [pl.BlockSpec(w.shape, const) for w in w_outs]
    in_specs += [pl.BlockSpec((1, d), const),
                 pl.BlockSpec((d, 2 * LANES), const),
                 pl.BlockSpec((1, LANES), const),
                 pl.BlockSpec((MOE_EPG, d, MOE_HIDDEN), lambda i, j: (j, 0, 0)),
                 pl.BlockSpec((MOE_EPG, d, MOE_HIDDEN), lambda i, j: (j, 0, 0)),
                 pl.BlockSpec((group_hidden, d), lambda i, j: (j, 0)),
                 pl.BlockSpec((1, d), const)]
    scratch = [pltpu.VMEM((tm, d), BF16), pltpu.VMEM((tm, LANES), F32), pltpu.VMEM((tm, group_hidden), BF16)]
    return pl.pallas_call(
        functools.partial(_proj_moe_body, n_act=n_act, final_norm=final_norm),
        grid=(t // tm, MOE_GROUPS), in_specs=in_specs,
        out_specs=pl.BlockSpec((tm, d), tile),
        out_shape=jax.ShapeDtypeStruct((t, d), F32),
        scratch_shapes=scratch, compiler_params=_cparams(("parallel", "arbitrary")),
        name="out_proj_moe")(res2d, *acts, *w_outs, gain.reshape(1, d), wr_cat, br,
                             w1, w3.astype(BF16), w2.reshape(MOE_EXPERTS * MOE_HIDDEN, d),
                             final_gain.reshape(1, d))


def _rope_tables(s_len):
    half = HEAD_DIM // 2
    inv = ROPE_THETA ** (-(jnp.arange(half, dtype=F32) / half))
    ang = jnp.arange(s_len, dtype=F32)[:, None] * inv[None, :]
    cos, sin = jnp.cos(ang), jnp.sin(ang)
    return jnp.concatenate([cos, cos], axis=-1), jnp.concatenate([-sin, sin], axis=-1)


def _even_weights(w_in):
    n_if = 2 * MLSTM_HEADS
    a = 4 * MLSTM_W
    w_main = jnp.concatenate([w_in[:, :a], w_in[:, a + n_if:],
                              w_in[:, a:a + n_if], jnp.zeros((w_in.shape[0], LANES - n_if), w_in.dtype)], axis=1)
    w_if_t = jnp.concatenate([w_in[:, a:a + n_if].T, jnp.zeros((16 - n_if, w_in.shape[0]), w_in.dtype)], axis=0)
    return w_main.astype(BF16), w_if_t.astype(BF16)


def _odd_weights(w_in):
    a = NSA_W + 6 * NSA_KV_W
    n_gates = NSA_HEADS * NSA_GATES
    assert w_in.shape[1] == a + n_gates and n_gates <= LANES
    return jnp.pad(w_in, ((0, 0), (0, LANES - n_gates))).astype(BF16)


def kernel(x, mix_norm_0, w_in_0, mlstm_conv_0, mlstm_gate_b_0, mlstm_head_norm_0, w_out_0, ffn_norm_0, router_group_0, router_group_b_0, router_expert_0, router_expert_b_0, moe_w1_0, moe_w3_0, moe_w2_0, mix_norm_1, w_in_1, nsa_cmp_pos_1, nsa_cmp_k1_1, nsa_cmp_k2_1, nsa_cmp_v1_1, nsa_cmp_v2_1, w_out_1, ffn_norm_1, router_group_1, router_group_b_1, router_expert_1, router_expert_b_1, moe_w1_1, moe_w3_1, moe_w2_1, final_norm):
    b, s_len, d = x.shape
    t = b * s_len
    cos2, sin2 = _rope_tables(s_len)
    x2d = x.reshape(t, d)

    w_main, w_if_t = _even_weights(w_in_0)
    z0, gt = _norm_matmul(x2d, mix_norm_0, w_main, w_if_t)
    z0 = z0.reshape(b, s_len, EVEN_N)
    h_m = _mlstm(z0, gt, mlstm_conv_0, mlstm_gate_b_0, mlstm_head_norm_0)
    o_b = _moba(z0, cos2, sin2)
    w_out = w_out_0.astype(BF16)
    h = _proj_moe(x2d, [h_m.reshape(t, MLSTM_W), o_b.reshape(t, MOBA_W)], [w_out[:MLSTM_W], w_out[MLSTM_W:]],
                  ffn_norm_0, router_group_0, router_group_b_0, router_expert_0, router_expert_b_0,
                  moe_w1_0, moe_w3_0, moe_w2_0, final_norm, False)

    z1 = _norm_matmul(h, mix_norm_1, _odd_weights(w_in_1)).reshape(b, s_len, ODD_N)
    k_cmp, v_cmp = _nsa_compress(z1, nsa_cmp_pos_1, nsa_cmp_k1_1.astype(BF16), nsa_cmp_k2_1.astype(BF16),
                                 nsa_cmp_v1_1.astype(BF16), nsa_cmp_v2_1.astype(BF16))
    o = _nsa_attention(z1, k_cmp, v_cmp, cos2, sin2)
    h = _proj_moe(h, [o.reshape(t, NSA_W)], [w_out_1.astype(BF16)],
                  ffn_norm_1, router_group_1, router_group_b_1, router_expert_1, router_expert_b_1,
                  moe_w1_1, moe_w3_1, moe_w2_1, final_norm, True)
    return h.reshape(b, s_len, d)
```

```python
import functools

import numpy as np
import jax
import jax.numpy as jnp
from jax import lax
from jax.experimental import pallas as pl
from jax.experimental.pallas import tpu as pltpu

F32 = jnp.float32
BF16 = jnp.bfloat16
HIGHEST = lax.Precision.HIGHEST

LANES = 128
BF16_SUBLANES = 16
D_MODEL = 1024
HEAD_DIM = 128
ROPE_THETA = 10000.0
NORM_EPS = 1e-6
NEG_INF = -1e30
FORCED_SCORE = 1e4

MLSTM_HEADS = 4
MLSTM_W = MLSTM_HEADS * HEAD_DIM
MLSTM_CHUNK = 64
MLSTM_CONV = 4
MLSTM_GATE_CAP = 15.0
MOBA_HEADS = 4
MOBA_W = MOBA_HEADS * HEAD_DIM
MOBA_BLOCK = 256
MOBA_TOPK = 3

NSA_HEADS = 8
NSA_GROUPS = 2
NSA_REP = NSA_HEADS // NSA_GROUPS
NSA_W = NSA_HEADS * HEAD_DIM
NSA_KV_W = NSA_GROUPS * HEAD_DIM
CMP_LEN = 32
CMP_STRIDE = 16
CMP_HIDDEN = 256
SEL_BLOCK = 64
SEL_TOPN = 8
WINDOW = 512
NSA_TQ = 256

MOE_GROUPS = 4
MOE_EPG = 4
MOE_EXPERTS = MOE_GROUPS * MOE_EPG
MOE_HIDDEN = D_MODEL // 4
ROUTER_SLOTS = 32

EVEN_N = 4 * MLSTM_W + 3 * MOBA_W + LANES
EVEN_IF_BLK = (4 * MLSTM_W + 3 * MOBA_W) // LANES
ODD_GATE_BLK = (NSA_W + 6 * NSA_KV_W) // LANES
ODD_N = NSA_W + 6 * NSA_KV_W + LANES
NSA_GATES = 3

VMEM_LIMIT = 56 * 1024 * 1024
VT_ROWS = HEAD_DIM + BF16_SUBLANES
LOG2_E = float(np.log2(np.e))

_NT = (((1,), (1,)), ((), ()))
_TN = (((0,), (0,)), ((), ()))


def _cparams(sem):
    return pltpu.CompilerParams(dimension_semantics=sem, vmem_limit_bytes=VMEM_LIMIT)


def _rms(x, g):
    return x * lax.rsqrt(jnp.mean(x * x, axis=-1, keepdims=True) + NORM_EPS) * g


def _sigmoid(x):
    return 0.5 * jnp.tanh(0.5 * x) + 0.5


def _log_sigmoid(x):
    return -(jnp.maximum(-x, 0.0) + jnp.log1p(jnp.exp(-jnp.abs(x))))


def _rotate(x, cos2, sin2):
    return x * cos2 + pltpu.roll(x, HEAD_DIM // 2, axis=1) * sin2


def _flash_steps(qs, k_ts, vt_ts, biases, carries):
    n = len(qs)
    ss = [lax.dot_general(k_ts[u], qs[u], _NT, preferred_element_type=F32) + biases[u] for u in range(n)]
    ms, es, alphas, pvs = [], [], [], []

    def value_matmul(u):
        pv = jnp.dot(vt_ts[u], es[u], preferred_element_type=F32)
        if carries is not None:
            pv = alphas[u] * carries[2 * u + 1] + pv
        pvs.append(pv)

    for u in range(n):
        m_new = jnp.max(ss[u], axis=0, keepdims=True)
        if carries is not None:
            m_new = jnp.maximum(carries[2 * u], m_new)
            alphas.append(jnp.exp2(carries[2 * u] - m_new))
        ms.append(m_new)
        es.append(jnp.exp2(ss[u] - m_new).astype(BF16))
        if u > 0:
            value_matmul(u - 1)
    value_matmul(n - 1)
    out = []
    for u in range(n):
        out += [ms[u], pvs[u]]
    return out


def _flash_output(acc):
    return acc[:HEAD_DIM] / jnp.maximum(acc[HEAD_DIM:HEAD_DIM + 1], 1e-30)


def _store_transposed(dst_ref, x):
    n_rows, n_cols = x.shape
    for r in range(0, n_rows, LANES):
        dst_ref[:n_cols, r:r + LANES] = x[r:r + LANES, :].T.astype(dst_ref.dtype)
    dst_ref[n_cols:, :] = jnp.ones((dst_ref.shape[0] - n_cols, n_rows), dst_ref.dtype)


def _rank_before(v, n_valid, axis):
    idx = lax.broadcasted_iota(jnp.int32, v.shape, axis)
    rank = jnp.zeros(v.shape, F32)
    for m in range(n_valid):
        vm = v[m:m + 1, :] if axis == 0 else v[:, m:m + 1]
        tie = jnp.where(idx > m, 1.0, 0.0)
        rank = rank + jnp.where(vm > v, 1.0, jnp.where(vm == v, tie, 0.0))
    return rank


def _norm_matmul_body(*refs, n_chunk, with_t):
    if with_t:
        x_ref, g_ref, w_ref, wt_ref, o_ref, ot_ref = refs
    else:
        x_ref, g_ref, w_ref, o_ref = refs
    yb = _rms(x_ref[...], g_ref[...]).astype(BF16)
    n = w_ref.shape[1]
    for c0 in range(0, n, n_chunk):
        c1 = min(n, c0 + n_chunk)
        o_ref[:, c0:c1] = jnp.dot(yb, w_ref[:, c0:c1], preferred_element_type=F32)
    if with_t:
        ot_ref[...] = lax.dot_general(wt_ref[...], yb, _NT, preferred_element_type=F32)


def _norm_matmul(x2d, gain, w, wt=None, tm=512):
    t, d = x2d.shape
    n = w.shape[1]
    with_t = wt is not None
    in_specs = [pl.BlockSpec((tm, d), lambda i: (i, 0)),
                pl.BlockSpec((1, d), lambda i: (0, 0)),
                pl.BlockSpec((d, n), lambda i: (0, 0))]
    out_specs = [pl.BlockSpec((tm, n), lambda i: (i, 0))]
    out_shape = [jax.ShapeDtypeStruct((t, n), F32)]
    args = [x2d, gain.reshape(1, d), w]
    if with_t:
        r = wt.shape[0]
        in_specs.append(pl.BlockSpec((r, d), lambda i: (0, 0)))
        out_specs.append(pl.BlockSpec((r, tm), lambda i: (0, i)))
        out_shape.append(jax.ShapeDtypeStruct((r, t), F32))
        args.append(wt)
    outs = pl.pallas_call(
        functools.partial(_norm_matmul_body, n_chunk=512, with_t=with_t),
        grid=(t // tm,), in_specs=in_specs, out_specs=out_specs, out_shape=out_shape,
        compiler_params=_cparams(("parallel",)), name="norm_in_proj")(*args)
    return outs if with_t else outs[0]


def _chunk_cumsum(x, axis):
    idx = lax.broadcasted_iota(jnp.int32, x.shape, axis) % MLSTM_CHUNK
    d = 1
    while d < MLSTM_CHUNK:
        x = x + jnp.where(idx >= d, pltpu.roll(x, d, axis=axis), 0.0)
        d *= 2
    return x


MLSTM_HEADS_PER_STEP = 2


def _mlstm_body(q_ref, k_ref, v_ref, og_ref, if_ref, gt_ref, cwq_ref, cwk_ref, bcol_ref, brow_ref,
                gain_ref, out_ref, ks, qts, vts, hts, css, brs, lirs, gcol):
    hp = pl.program_id(1)
    s_len = q_ref.shape[1]
    n_local = q_ref.shape[2] // HEAD_DIM
    n_chunks = s_len // MLSTM_CHUNK
    L = MLSTM_CHUNK
    per_slab = LANES // L
    row = lax.broadcasted_iota(jnp.int32, (s_len, HEAD_DIM), 0)
    cols = [slice(a * HEAD_DIM, (a + 1) * HEAD_DIM) for a in range(n_local)]

    def conv_silu(x, w):
        acc = x * w[MLSTM_CONV - 1:MLSTM_CONV, :]
        for d in range(1, MLSTM_CONV):
            shifted = jnp.where(row >= d, pltpu.roll(x, d, axis=0), 0.0)
            acc = acc + shifted * w[MLSTM_CONV - 1 - d:MLSTM_CONV - d, :]
        return acc * _sigmoid(acc)

    def store_chunks_transposed(dst, x):
        for p in range(s_len // LANES):
            slab_t = x[p * LANES:(p + 1) * LANES, :].T
            for j in range(per_slab):
                dst[per_slab * p + j] = slab_t[:, j * L:(j + 1) * L].astype(dst.dtype)

    lane = lax.broadcasted_iota(jnp.int32, gcol.shape, 1)

    @pl.when(hp == 0)
    def _():
        pre = if_ref[0] + bcol_ref[...]
        pre = MLSTM_GATE_CAP * jnp.tanh(pre / MLSTM_GATE_CAP)
        gcol[...] = jnp.where(lane < MLSTM_HEADS, pre, _chunk_cumsum(_log_sigmoid(pre), 0))

    pr = gt_ref[...] + brow_ref[...]
    pr = MLSTM_GATE_CAP * jnp.tanh(pr / MLSTM_GATE_CAP)
    sub = lax.broadcasted_iota(jnp.int32, pr.shape, 0)
    b_rows = _chunk_cumsum(_log_sigmoid(pr), 1)
    g_all = gcol[...]

    gains = []
    for a in range(n_local):
        h = hp * n_local + a
        store_chunks_transposed(qts.at[a], conv_silu(q_ref[0, :, cols[a]], cwq_ref[:, cols[a]]) * (HEAD_DIM ** -0.5))
        store_chunks_transposed(vts.at[a], v_ref[0, :, cols[a]])
        ks[:, cols[a]] = conv_silu(k_ref[0, :, cols[a]], cwk_ref[:, cols[a]]).astype(BF16)
        css[a] = jnp.sum(jnp.where(lane == h, g_all, jnp.where(lane == h + MLSTM_HEADS, -g_all, 0.0)),
                         axis=1, keepdims=True)
        li_row = jnp.sum(jnp.where(sub == h, pr, 0.0), axis=0, keepdims=True)
        b_row = jnp.sum(jnp.where(sub == h + MLSTM_HEADS, b_rows, 0.0), axis=0, keepdims=True)
        for c in range(n_chunks):
            brs[a, c] = b_row[:, c * L:(c + 1) * L]
            lirs[a, c] = li_row[:, c * L:(c + 1) * L]
        gains.append(jnp.broadcast_to(gain_ref[a * HEAD_DIM:(a + 1) * HEAD_DIM, :], (HEAD_DIM, L)))

    tri = lax.broadcasted_iota(jnp.int32, (L, L), 0) <= lax.broadcasted_iota(jnp.int32, (L, L), 1)

    group_size = 16
    assert n_chunks % group_size == 0

    def group(gi, carry):
        states = [list(carry[3 * a:3 * a + 3]) for a in range(n_local)]
        cs = [gi * group_size + j for j in range(group_size)]
        r0s = [pl.multiple_of(c * L, L) for c in cs]
        units = [(a, j) for a in range(n_local) for j in range(group_size)]
        ks_ = {(a, j): ks[pl.ds(r0s[j], L), cols[a]] for a, j in units}
        q_ts = {(a, j): qts[a, cs[j]] for a, j in units}
        v_ts = {(a, j): vts[a, cs[j]] for a, j in units}
        b_rs = {(a, j): brs[a, cs[j]] for a, j in units}
        kvs, ksums, kqs, g_maxs, b_lasts = {}, {}, {}, {}, {}
        for u in units:
            a, j = u
            b_last = b_rs[u][:, L - 1:L]
            g = b_last - b_rs[u] + lirs[a, cs[j]]
            g_max = jnp.max(g, axis=1, keepdims=True)
            w = jnp.exp(g - g_max)
            kvs[u] = jnp.dot((v_ts[u] * w).astype(BF16), ks_[u], preferred_element_type=F32)
            ksums[u] = jnp.dot(jnp.broadcast_to(w, (BF16_SUBLANES, L)).astype(BF16), ks_[u],
                               preferred_element_type=F32)[:1]
            kqs[u] = jnp.dot(ks_[u], q_ts[u], preferred_element_type=F32)
            g_maxs[u] = g_max
            b_lasts[u] = b_last
        c_ins, n_ins, m_ins = {}, {}, {}
        for u in units:
            a, j = u
            c_state, n_state, m_state = states[a]
            c_ins[u] = c_state.astype(BF16)
            n_ins[u] = jnp.broadcast_to(n_state, (BF16_SUBLANES, HEAD_DIM)).astype(BF16)
            m_ins[u] = m_state
            m_new = jnp.maximum(b_lasts[u] + m_state, g_maxs[u])
            sa = jnp.exp(b_lasts[u] + m_state - m_new)
            cc = jnp.exp(g_maxs[u] - m_new)
            states[a] = [sa * c_state + cc * kvs[u], sa * n_state + cc * ksums[u], m_new]
        c_qs = {u: jnp.dot(c_ins[u], q_ts[u], preferred_element_type=F32) for u in units}
        n_qs = {u: jnp.dot(n_ins[u], q_ts[u], preferred_element_type=F32)[:1] for u in units}
        ss, m_ts, w_inters = {}, {}, {}
        for u in units:
            a, j = u
            dmat = jnp.where(tri, b_rs[u] + css[a, pl.ds(r0s[j], L), :], NEG_INF)
            inter = b_rs[u] + m_ins[u]
            m_t = jnp.maximum(inter, jnp.max(dmat, axis=0, keepdims=True))
            ss[u] = kqs[u] * jnp.exp(dmat - m_t)
            m_ts[u] = m_t
            w_inters[u] = jnp.exp(inter - m_t)
        svs = {u: jnp.dot(v_ts[u].astype(BF16), ss[u].astype(BF16), preferred_element_type=F32) for u in units}
        for u in units:
            a, j = u
            num = svs[u] + w_inters[u] * c_qs[u]
            den = jnp.sum(ss[u], axis=0, keepdims=True) + w_inters[u] * n_qs[u]
            ht = num / jnp.maximum(jnp.abs(den), jnp.exp(-m_ts[u]))
            hts[a, cs[j]] = ht * lax.rsqrt(jnp.mean(ht * ht, axis=0, keepdims=True) + NORM_EPS) * gains[a]
        return tuple(x for st in states for x in st)

    init = (jnp.zeros((HEAD_DIM, HEAD_DIM), F32), jnp.zeros((1, HEAD_DIM), F32), jnp.zeros((1, 1), F32)) * n_local
    lax.fori_loop(0, n_chunks // group_size, group, init)

    for a in range(n_local):
        for p in range(s_len // LANES):
            slab_t = jnp.concatenate([hts[a, per_slab * p + j] for j in range(per_slab)], axis=1)
            rows = slice(p * LANES, (p + 1) * LANES)
            out_ref[0, rows, cols[a]] = (slab_t.T * _sigmoid(og_ref[0, rows, cols[a]])).astype(out_ref.dtype)


def _mlstm(z3, gt, conv_w, gate_b, head_gain):
    b, s_len, _ = z3.shape
    n_chunks = s_len // MLSTM_CHUNK
    gr = gt.shape[0]
    bcol = jnp.zeros((1, LANES), F32).at[0, :2 * MLSTM_HEADS].set(gate_b)
    brow = jnp.zeros((gr, 1), F32).at[:2 * MLSTM_HEADS, 0].set(gate_b)
    n_local = MLSTM_HEADS_PER_STEP
    n_steps = MLSTM_HEADS // n_local
    width = n_local * HEAD_DIM

    def col(off):
        return pl.BlockSpec((1, s_len, width), lambda bi, hi: (bi, 0, off * n_steps + hi))

    in_specs = [col(0), col(1), col(2), col(3),
                pl.BlockSpec((1, s_len, LANES), lambda bi, hi: (bi, 0, EVEN_IF_BLK)),
                pl.BlockSpec((gr, s_len), lambda bi, hi: (0, bi)),
                pl.BlockSpec((MLSTM_CONV, width), lambda bi, hi: (0, hi)),
                pl.BlockSpec((MLSTM_CONV, width), lambda bi, hi: (0, n_steps + hi)),
                pl.BlockSpec((1, LANES), lambda bi, hi: (0, 0)),
                pl.BlockSpec((gr, 1), lambda bi, hi: (0, 0)),
                pl.BlockSpec((width, 1), lambda bi, hi: (hi, 0))]
    chunk_t = (n_local, n_chunks, HEAD_DIM, MLSTM_CHUNK)
    rows_t = (n_local, n_chunks, 1, MLSTM_CHUNK)
    scratch = [pltpu.VMEM((s_len, width), BF16),
               pltpu.VMEM(chunk_t, BF16), pltpu.VMEM(chunk_t, F32), pltpu.VMEM(chunk_t, F32),
               pltpu.VMEM((n_local, s_len, 1), F32),
               pltpu.VMEM(rows_t, F32), pltpu.VMEM(rows_t, F32),
               pltpu.VMEM((s_len, LANES), F32)]
    return pl.pallas_call(
        _mlstm_body, grid=(b, n_steps), in_specs=in_specs,
        out_specs=pl.BlockSpec((1, s_len, width), lambda bi, hi: (bi, 0, hi)),
        out_shape=jax.ShapeDtypeStruct((b, s_len, MLSTM_W), BF16),
        scratch_shapes=scratch, compiler_params=_cparams(("parallel", "arbitrary")),
        name="mlstm")(z3, z3, z3, z3, z3, gt, conv_w, conv_w, bcol, brow, head_gain.reshape(MLSTM_W, 1))


MOBA_ROWS = 2


def _moba_body(q_ref, k_ref, v_ref, cos_ref, sin_ref, o_ref, kr_scr, vt_scr, km_scr, sel_scr):
    i = pl.program_id(1)
    n_rows, s_len = k_ref.shape[0], k_ref.shape[1]
    bs = MOBA_BLOCK
    nb = s_len // bs
    heads = [slice(h * HEAD_DIM, (h + 1) * HEAD_DIM) for h in range(MOBA_HEADS)]
    units = [(b, h) for b in range(n_rows) for h in range(MOBA_HEADS)]

    @pl.when(i == 0)
    def _():
        for u, (b, h) in enumerate(units):
            _store_transposed(vt_scr.at[u], v_ref[b, :, heads[h]])
            kr = _rotate(k_ref[b, :, heads[h]], cos_ref[...], sin_ref[...])
            kr_scr[b, :, heads[h]] = kr.astype(BF16)
            rows = [jnp.sum(kr[n * bs:(n + 1) * bs, :], axis=0, keepdims=True) / float(bs) for n in range(nb)]
            rows.append(jnp.zeros((km_scr.shape[1] - nb, HEAD_DIM), F32))
            km_scr[u] = jnp.concatenate(rows, axis=0)

    t0 = pl.multiple_of(i * bs, bs)
    cos_q = cos_ref[pl.ds(t0, bs), :]
    sin_q = sin_ref[pl.ds(t0, bs), :]
    blk = lax.broadcasted_iota(jnp.int32, (km_scr.shape[1], bs), 0)
    causal_bias = jnp.where(lax.broadcasted_iota(jnp.int32, (bs, bs), 0)
                            <= lax.broadcasted_iota(jnp.int32, (bs, bs), 1), 0.0, NEG_INF)

    qbs = []
    for u, (b, h) in enumerate(units):
        qr = _rotate(q_ref[b, :, heads[h]], cos_q, sin_q)
        gate_t = lax.dot_general(km_scr[u], qr, _NT, precision=HIGHEST, preferred_element_type=F32)
        val = jnp.where(blk < i, gate_t, NEG_INF)
        picked = jnp.where(val > 0.5 * NEG_INF, _rank_before(val, nb, 0), float(nb)) < MOBA_TOPK
        sel_scr[u] = jnp.where(picked, 0.0, NEG_INF)
        qbs.append((qr * (HEAD_DIM ** -0.5 * LOG2_E)).astype(BF16))

    def key_tiles(k0, n_blocks):
        return ([kr_scr[b, pl.ds(k0, n_blocks * bs), heads[h]] for b, h in units],
                [vt_scr[u, :, pl.ds(k0, n_blocks * bs)] for u in range(len(units))])

    init = tuple(_flash_steps(qbs, *key_tiles(t0, 1), [causal_bias] * len(units), None))

    def past_blocks(j, n_blocks, carry):
        biases = [jnp.concatenate([jnp.broadcast_to(sel_scr[u, pl.ds(j + d, 1), :], (bs, bs))
                                   for d in range(n_blocks)], axis=0) for u in range(len(units))]
        k0 = j * bs if isinstance(j, int) else pl.multiple_of(j * bs, bs)
        return tuple(_flash_steps(qbs, *key_tiles(k0, n_blocks), biases, carry))

    odd = i % 2
    carry = lax.cond(odd == 1, lambda c: past_blocks(0, 1, c), lambda c: c, init)
    fin = lax.fori_loop(0, i // 2, lambda p, c: past_blocks(odd + 2 * p, 2, c), carry)
    for u, (b, h) in enumerate(units):
        o_ref[b, :, heads[h]] = _flash_output(fin[2 * u + 1]).T.astype(o_ref.dtype)


def _moba(z3, cos2, sin2):
    b, s_len, _ = z3.shape
    nb = s_len // MOBA_BLOCK
    q_off = 4 * MLSTM_W // MOBA_W
    rows = MOBA_ROWS if b % MOBA_ROWS == 0 else 1
    n_units = rows * MOBA_HEADS

    def kv(off):
        return pl.BlockSpec((rows, s_len, MOBA_W), lambda bi, i: (bi, 0, off))

    in_specs = [pl.BlockSpec((rows, MOBA_BLOCK, MOBA_W), lambda bi, i: (bi, i, q_off)),
                kv(q_off + 1), kv(q_off + 2),
                pl.BlockSpec((s_len, HEAD_DIM), lambda bi, i: (0, 0)),
                pl.BlockSpec((s_len, HEAD_DIM), lambda bi, i: (0, 0))]
    scratch = [pltpu.VMEM((rows, s_len, MOBA_W), BF16),
               pltpu.VMEM((n_units, VT_ROWS, s_len), BF16),
               pltpu.VMEM((n_units, BF16_SUBLANES, HEAD_DIM), F32),
               pltpu.VMEM((n_units, BF16_SUBLANES, MOBA_BLOCK), F32)]
    return pl.pallas_call(
        _moba_body, grid=(b // rows, nb), in_specs=in_specs,
        out_specs=pl.BlockSpec((rows, MOBA_BLOCK, MOBA_W), lambda bi, i: (bi, i, 0)),
        out_shape=jax.ShapeDtypeStruct((b, s_len, MOBA_W), BF16),
        scratch_shapes=scratch, compiler_params=_cparams(("parallel", "arbitrary")),
        name="moba")(z3, z3, z3, cos2, sin2)


def _gelu_tanh(x):
    return x * (0.5 * (1.0 + jnp.tanh(np.sqrt(2.0 / np.pi) * (x + 0.044715 * (x * x * x)))))


def _nsa_compress_body(kc_ref, vc_ref, pe_ref, wk1_ref, wk2_ref, wv1_ref, wv2_ref, ko_ref, vo_ref):
    n_rows = kc_ref.shape[1] // CMP_STRIDE
    halves = CMP_LEN // CMP_STRIDE
    assert halves == 2

    def compress(x_ref, pe, w1_ref, w2_ref):
        ya = jnp.zeros((n_rows, CMP_HIDDEN), F32)
        yb = jnp.zeros((n_rows, CMP_HIDDEN), F32)
        for l in range(CMP_STRIDE):
            r = x_ref[0, pl.ds(l, n_rows, stride=CMP_STRIDE), :]
            la, lb = l, CMP_STRIDE + l
            ya = ya + jnp.dot((r + pe[la:la + 1, :]).astype(BF16), w1_ref[la * HEAD_DIM:(la + 1) * HEAD_DIM, :],
                              preferred_element_type=F32)
            yb = yb + jnp.dot((r + pe[lb:lb + 1, :]).astype(BF16), w1_ref[lb * HEAD_DIM:(lb + 1) * HEAD_DIM, :],
                              preferred_element_type=F32)
        pre = ya + pltpu.roll(yb, n_rows - 1, axis=0)
        return jnp.dot(_gelu_tanh(pre).astype(BF16), w2_ref[...], preferred_element_type=F32)

    ko_ref[0, 0] = compress(kc_ref, pe_ref[0], wk1_ref, wk2_ref)
    vo_ref[0, 0] = compress(vc_ref, pe_ref[1], wv1_ref, wv2_ref)


def _nsa_compress(z3, cmp_pos, wk1, wk2, wv1, wv2):
    b, s_len, _ = z3.shape
    n_rows = s_len // CMP_STRIDE
    kc_off = NSA_W // LANES
    vc_off = kc_off + NSA_GROUPS
    const2 = lambda bi, gi: (0, 0)
    in_specs = [pl.BlockSpec((1, s_len, HEAD_DIM), lambda bi, gi: (bi, 0, kc_off + gi)),
                pl.BlockSpec((1, s_len, HEAD_DIM), lambda bi, gi: (bi, 0, vc_off + gi)),
                pl.BlockSpec(cmp_pos.shape, lambda bi, gi: (0, 0, 0)),
                pl.BlockSpec(wk1.shape, const2), pl.BlockSpec(wk2.shape, const2),
                pl.BlockSpec(wv1.shape, const2), pl.BlockSpec(wv2.shape, const2)]
    out_spec = pl.BlockSpec((1, 1, n_rows, HEAD_DIM), lambda bi, gi: (bi, gi, 0, 0))
    out_sds = jax.ShapeDtypeStruct((b, NSA_GROUPS, n_rows, HEAD_DIM), F32)
    return pl.pallas_call(
        _nsa_compress_body, grid=(b, NSA_GROUPS), in_specs=in_specs,
        out_specs=[out_spec, out_spec], out_shape=[out_sds, out_sds],
        compiler_params=_cparams(("parallel", "parallel")), name="nsa_compress")(
            z3, z3, cmp_pos, wk1, wk2, wv1, wv2)


def _nsa_body(q_ref, ks_ref, vs_ref, kw_ref, vw_ref, kc_ref, vc_ref, gate_ref, cos_ref, sin_ref, ov_ref,
              o_ref, krs, vst, krw, vwt, sel_scr):
    i = pl.program_id(1)
    tq = NSA_TQ
    tk = NSA_TQ
    scale = HEAD_DIM ** -0.5
    n_sel_blk = ks_ref.shape[1] // SEL_BLOCK
    n_cmp = (ks_ref.shape[1] - CMP_LEN) // CMP_STRIDE + 1
    shift = SEL_BLOCK.bit_length() - 1
    assert 1 << shift == SEL_BLOCK and n_sel_blk <= LANES
    blk_per_tile = tk // SEL_BLOCK
    groups = [slice(g * HEAD_DIM, (g + 1) * HEAD_DIM) for g in range(NSA_GROUPS)]

    @pl.when(i == 0)
    def _():
        for g, cols in enumerate(groups):
            krs[:, cols] = _rotate(ks_ref[0, :, cols], cos_ref[...], sin_ref[...]).astype(BF16)
            krw[:, cols] = _rotate(kw_ref[0, :, cols], cos_ref[...], sin_ref[...]).astype(BF16)
            _store_transposed(vst.at[g], vs_ref[0, :, cols])
            _store_transposed(vwt.at[g], vw_ref[0, :, cols])

    t0 = pl.multiple_of(i * tq, tq)
    pos_row = t0 + lax.broadcasted_iota(jnp.int32, (1, tq), 1)
    cos_q = cos_ref[pl.ds(t0, tq), :]
    sin_q = sin_ref[pl.ds(t0, tq), :]

    n_col = lax.broadcasted_iota(jnp.int32, (LANES, 1), 0)
    cmp_end = jnp.where(n_col < n_cmp, n_col * CMP_STRIDE + (CMP_LEN - 1), jnp.iinfo(jnp.int32).max)
    cmp_ok = cmp_end <= pos_row
    blk_i = lax.broadcasted_iota(jnp.int32, (n_sel_blk, tq), 0)
    behind = (pos_row >> shift) - blk_i
    units = [(g, r) for g in range(NSA_GROUPS) for r in range(NSA_REP)]
    qr_heads, o_cmp = [], []
    for g in range(NSA_GROUPS):
        kc = kc_ref[0, g].astype(BF16)
        vc = vc_ref[0, g].astype(BF16)
        p_sum = jnp.zeros((LANES, tq), F32)
        for r in range(NSA_REP):
            head = (g * NSA_REP + r) * HEAD_DIM
            q = q_ref[0, :, head:head + HEAD_DIM] * scale
            qr_heads.append(_rotate(q * LOG2_E, cos_q, sin_q).astype(BF16))
            s = lax.dot_general(kc, q.astype(BF16), _NT, preferred_element_type=F32)
            s = jnp.where(cmp_ok, s, NEG_INF)
            e = jnp.where(cmp_ok, jnp.exp(s - jnp.max(s, axis=0, keepdims=True)), 0.0)
            p = e / jnp.maximum(jnp.sum(e, axis=0, keepdims=True), 1e-30)
            o_cmp.append(lax.dot_general(p.astype(BF16), vc, _TN, preferred_element_type=F32))
            p_sum = p_sum + p
        imp = jnp.dot(ov_ref[...], p_sum, precision=HIGHEST, preferred_element_type=F32)
        val = jnp.where(behind == 0, FORCED_SCORE, jnp.where(behind == 1, FORCED_SCORE, imp))
        val = jnp.where(blk_i == 0, FORCED_SCORE, val)
        val = jnp.where(behind >= 0, val, NEG_INF)
        picked = jnp.where(val > 0.5 * NEG_INF, _rank_before(val, n_sel_blk, 0), float(n_sel_blk)) < SEL_TOPN
        sel_scr[g] = jnp.where(picked, 0.0, NEG_INF)

    def sel_bias(g, k0, n_tiles):
        blk0 = (k0 >> shift) if isinstance(k0, int) else pl.multiple_of(k0 >> shift, blk_per_tile)
        return jnp.concatenate([jnp.broadcast_to(sel_scr[g, pl.ds(blk0 + b, 1), :], (SEL_BLOCK, tq))
                                for b in range(n_tiles * blk_per_tile)], axis=0)

    def sel_tiles(k0, n_tiles, carry):
        biases = [sel_bias(g, k0, n_tiles) for g in range(NSA_GROUPS)]
        return tuple(_flash_steps(qr_heads, [krs[pl.ds(k0, n_tiles * tk), groups[g]] for g, _ in units],
                                  [vst[g, :, pl.ds(k0, n_tiles * tk)] for g, _ in units],
                                  [biases[g] for g, _ in units], carry))

    causal_bias = jnp.where(lax.broadcasted_iota(jnp.int32, (tk, tq), 0)
                            <= lax.broadcasted_iota(jnp.int32, (tk, tq), 1), 0.0, NEG_INF)
    span = WINDOW + tq
    start = pl.multiple_of(jnp.maximum(t0 - WINDOW, 0), tq)
    gap = ((t0 - start) + lax.broadcasted_iota(jnp.int32, (span, tq), 1)
           - lax.broadcasted_iota(jnp.int32, (span, tq), 0))
    win_bias = jnp.where(gap >= 0, jnp.where(gap < WINDOW, 0.0, NEG_INF), NEG_INF)
    diag_biases = [sel_bias(g, t0, 1) + causal_bias for g in range(NSA_GROUPS)]
    n = len(units)
    group = _flash_steps(qr_heads * 2,
                         [krs[pl.ds(t0, tk), groups[g]] for g, _ in units]
                         + [krw[pl.ds(start, span), groups[g]] for g, _ in units],
                         [vst[g, :, pl.ds(t0, tk)] for g, _ in units]
                         + [vwt[g, :, pl.ds(start, span)] for g, _ in units],
                         [diag_biases[g] for g, _ in units] + [win_bias] * n, None)
    win_fin = group[2 * n:]

    odd = i % 2
    carry = lax.cond(odd == 1, lambda c: sel_tiles(0, 1, c), lambda c: c, tuple(group[:2 * n]))
    sel_fin = lax.fori_loop(0, i // 2, lambda p, c: sel_tiles(pl.multiple_of((odd + 2 * p) * tk, tk), 2, c), carry)

    gates = _sigmoid(gate_ref[0])
    gates_t = gates.T
    for u in range(n):
        c0 = NSA_GATES * u
        o_t = (gates_t[c0 + 1:c0 + 2, :] * _flash_output(sel_fin[2 * u + 1])
               + gates_t[c0 + 2:c0 + 3, :] * _flash_output(win_fin[2 * u + 1]))
        o_ref[0, :, u * HEAD_DIM:(u + 1) * HEAD_DIM] = (gates[:, c0:c0 + 1] * o_cmp[u] + o_t.T).astype(o_ref.dtype)


def _nsa_attention(z3, k_cmp, v_cmp, cos2, sin2):
    b, s_len, _ = z3.shape
    assert s_len >= WINDOW + NSA_TQ and s_len % NSA_TQ == 0
    n_cmp = (s_len - CMP_LEN) // CMP_STRIDE + 1
    n_blk = s_len // SEL_BLOCK
    n_rows = k_cmp.shape[2]
    assert n_rows == LANES
    cmp_start = np.arange(n_rows) * CMP_STRIDE
    cmp_end = cmp_start + CMP_LEN - 1
    blk_lo = np.arange(LANES) * SEL_BLOCK
    overlap = ((cmp_start[:, None] <= blk_lo[None, :] + SEL_BLOCK - 1) & (cmp_end[:, None] >= blk_lo[None, :])
               & (np.arange(n_rows)[:, None] < n_cmp)).astype(np.float32)
    overlap_t = np.ascontiguousarray(overlap.T[:n_blk])
    base = NSA_W // NSA_KV_W

    def kv(off):
        return pl.BlockSpec((1, s_len, NSA_KV_W), lambda bi, i: (bi, 0, base + off))

    cmp_spec = pl.BlockSpec((1, NSA_GROUPS, n_rows, HEAD_DIM), lambda bi, i: (bi, 0, 0, 0))
    const2 = lambda bi, i: (0, 0)
    in_specs = [pl.BlockSpec((1, NSA_TQ, NSA_W), lambda bi, i: (bi, i, 0)),
                kv(2), kv(3), kv(4), kv(5), cmp_spec, cmp_spec,
                pl.BlockSpec((1, NSA_TQ, LANES), lambda bi, i: (bi, i, ODD_GATE_BLK)),
                pl.BlockSpec((s_len, HEAD_DIM), const2), pl.BlockSpec((s_len, HEAD_DIM), const2),
                pl.BlockSpec((n_blk, n_rows), const2)]
    vt_shape = (NSA_GROUPS, VT_ROWS, s_len)
    scratch = [pltpu.VMEM((s_len, NSA_KV_W), BF16), pltpu.VMEM(vt_shape, BF16),
               pltpu.VMEM((s_len, NSA_KV_W), BF16), pltpu.VMEM(vt_shape, BF16),
               pltpu.VMEM((NSA_GROUPS, n_blk, NSA_TQ), F32)]
    return pl.pallas_call(
        _nsa_body, grid=(b, s_len // NSA_TQ), in_specs=in_specs,
        out_specs=pl.BlockSpec((1, NSA_TQ, NSA_W), lambda bi, i: (bi, i, 0)),
        out_shape=jax.ShapeDtypeStruct((b, s_len, NSA_W), BF16),
        scratch_shapes=scratch, compiler_params=_cparams(("parallel", "arbitrary")),
        name="nsa_attention")(z3, z3, z3, z3, z3, k_cmp, v_cmp, z3, cos2, sin2, jnp.asarray(overlap_t))


def _split_bf16(x):
    hi = x.astype(BF16)
    return hi, (x - hi.astype(F32)).astype(BF16)


def _proj_moe_body(*refs, n_act, final_norm):
    res_ref = refs[0]
    a_refs = refs[1:1 + n_act]
    wo_refs = refs[1 + n_act:1 + 2 * n_act]
    (g_ref, wr_ref, br_ref, w1_ref, w3_ref, w2_ref, fg_ref, o_ref,
     xn_scr, cw_scr, hd_scr) = refs[1 + 2 * n_act:]
    j = pl.program_id(1)
    tm = res_ref.shape[0]
    lane = lax.broadcasted_iota(jnp.int32, (tm, LANES), 1).astype(F32)
    neg = float("-inf")

    @pl.when(j == 0)
    def _():
        h = res_ref[...]
        for a_ref, wo_ref in zip(a_refs, wo_refs):
            h = h + jnp.dot(a_ref[...], wo_ref[...], preferred_element_type=F32)
        o_ref[...] = h
        xn = _rms(h, g_ref[...])
        xh, xl = _split_bf16(xn)
        xn_scr[...] = xh
        hi_terms = jnp.dot(xh, wr_ref[...], preferred_element_type=F32)
        logits = (hi_terms[:, :LANES] + (jnp.dot(xl, wr_ref[:, :LANES], preferred_element_type=F32)
                                         + hi_terms[:, LANES:])) + br_ref[...]
        lt = logits.T[:ROUTER_SLOTS]
        slot = lax.broadcasted_iota(jnp.int32, lt.shape, 0).astype(F32)
        is_g = slot < MOE_GROUPS
        gl = jnp.where(is_g, lt, neg)
        g_max = jnp.max(gl, axis=0, keepdims=True)
        g_w = 1.0 / jnp.sum(jnp.where(is_g, jnp.exp(gl - g_max), 0.0), axis=0, keepdims=True)
        g_top = jnp.min(jnp.where(gl == g_max, slot, float(LANES)), axis=0, keepdims=True)
        lo = MOE_GROUPS + MOE_EPG * g_top
        el = jnp.where(slot >= lo, jnp.where(slot < lo + MOE_EPG, lt, neg), neg)
        v1 = jnp.max(el, axis=0, keepdims=True)
        i1 = jnp.min(jnp.where(el == v1, slot, float(LANES)), axis=0, keepdims=True)
        el2 = jnp.where(slot == i1, neg, el)
        v2 = jnp.max(el2, axis=0, keepdims=True)
        i2 = jnp.min(jnp.where(el2 == v2, slot, float(LANES)), axis=0, keepdims=True)
        e2 = jnp.exp(v2 - v1)
        den = 1.0 + e2
        cw_t = jnp.where(slot == i1, g_w / den, 0.0) + jnp.where(slot == i2, g_w * e2 / den, 0.0)
        cw_scr[...] = jnp.concatenate([cw_t, jnp.zeros((LANES - ROUTER_SLOTS, tm), F32)], axis=0).T

    xn = xn_scr[...]
    cw = cw_scr[...]
    for r in range(MOE_EPG):
        h1 = jnp.dot(xn, w1_ref[r].astype(BF16), preferred_element_type=F32)
        h3 = jnp.dot(xn, w3_ref[r], preferred_element_type=F32)
        e_lane = (MOE_GROUPS + MOE_EPG * j + r).astype(F32)
        col = jnp.sum(jnp.where(lane == e_lane, cw, 0.0), axis=1, keepdims=True)
        hd_scr[:, r * MOE_HIDDEN:(r + 1) * MOE_HIDDEN] = ((h1 * _sigmoid(h1)) * h3 * col).astype(BF16)
    o_ref[...] += jnp.dot(hd_scr[...], w2_ref[...].astype(BF16), preferred_element_type=F32)

    if final_norm:
        @pl.when(j == pl.num_programs(1) - 1)
        def _():
            o_ref[...] = _rms(o_ref[...], fg_ref[...])


def _proj_moe(res2d, acts, w_outs, gain, w_g, b_g, w_e, b_e, w1, w3, w2, final_gain, final_norm, tm=1024):
    t, d = res2d.shape
    n_act = len(acts)
    wr = jnp.zeros((d, LANES), F32).at[:, :MOE_GROUPS].set(w_g).at[:, MOE_GROUPS:MOE_GROUPS + MOE_EXPERTS].set(w_e)
    wr_cat = jnp.concatenate(_split_bf16(wr), axis=1)
    br = jnp.zeros((1, LANES), F32).at[0, :MOE_GROUPS].set(b_g).at[0, MOE_GROUPS:MOE_GROUPS + MOE_EXPERTS].set(b_e)
    tile = lambda i, j: (i, 0)
    const = lambda i, j: (0, 0)
    group_hidden = MOE_EPG * MOE_HIDDEN
    in_specs = [pl.BlockSpec((tm, d), tile)]
    in_specs += [pl.BlockSpec((tm, a.shape[1]), tile) for a in acts]
    in_specs += [pl.BlockSpec(w.shape, const) for w in w_outs]
    in_specs += [pl.BlockSpec((1, d), const),
                 pl.BlockSpec((d, 2 * LANES), const),
                 pl.BlockSpec((1, LANES), const),
                 pl.BlockSpec((MOE_EPG, d, MOE_HIDDEN), lambda i, j: (j, 0, 0)),
                 pl.BlockSpec((MOE_EPG, d, MOE_HIDDEN), lambda i, j: (j, 0, 0)),
                 pl.BlockSpec((group_hidden, d), lambda i, j: (j, 0)),
                 pl.BlockSpec((1, d), const)]
    scratch = [pltpu.VMEM((tm, d), BF16), pltpu.VMEM((tm, LANES), F32), pltpu.VMEM((tm, group_hidden), BF16)]
    return pl.pallas_call(
        functools.partial(_proj_moe_body, n_act=n_act, final_norm=final_norm),
        grid=(t // tm, MOE_GROUPS), in_specs=in_specs,
        out_specs=pl.BlockSpec((tm, d), tile),
        out_shape=jax.ShapeDtypeStruct((t, d), F32),
        scratch_shapes=scratch, compiler_params=_cparams(("parallel", "arbitrary")),
        name="out_proj_moe")(res2d, *acts, *w_outs, gain.reshape(1, d), wr_cat, br,
                             w1, w3.astype(BF16), w2.reshape(MOE_EXPERTS * MOE_HIDDEN, d),
                             final_gain.reshape(1, d))


def _rope_tables(s_len):
    half = HEAD_DIM // 2
    inv = ROPE_THETA ** (-(jnp.arange(half, dtype=F32) / half))
    ang = jnp.arange(s_len, dtype=F32)[:, None] * inv[None, :]
    cos, sin = jnp.cos(ang), jnp.sin(ang)
    return jnp.concatenate([cos, cos], axis=-1), jnp.concatenate([-sin, sin], axis=-1)


def _even_weights(w_in):
    n_if = 2 * MLSTM_HEADS
    a = 4 * MLSTM_W
    w_main = jnp.concatenate([w_in[:, :a], w_in[:, a + n_if:],
                              w_in[:, a:a + n_if], jnp.zeros((w_in.shape[0], LANES - n_if), w_in.dtype)], axis=1)
    w_if_t = jnp.concatenate([w_in[:, a:a + n_if].T, jnp.zeros((16 - n_if, w_in.shape[0]), w_in.dtype)], axis=0)
    return w_main.astype(BF16), w_if_t.astype(BF16)


def _odd_weights(w_in):
    a = NSA_W + 6 * NSA_KV_W
    n_gates = NSA_HEADS * NSA_GATES
    assert w_in.shape[1] == a + n_gates and n_gates <= LANES
    return jnp.pad(w_in, ((0, 0), (0, LANES - n_gates))).astype(BF16)


def kernel(x, mix_norm_0, w_in_0, mlstm_conv_0, mlstm_gate_b_0, mlstm_head_norm_0, w_out_0, ffn_norm_0, router_group_0, router_group_b_0, router_expert_0, router_expert_b_0, moe_w1_0, moe_w3_0, moe_w2_0, mix_norm_1, w_in_1, nsa_cmp_pos_1, nsa_cmp_k1_1, nsa_cmp_k2_1, nsa_cmp_v1_1, nsa_cmp_v2_1, w_out_1, ffn_norm_1, router_group_1, router_group_b_1, router_expert_1, router_expert_b_1, moe_w1_1, moe_w3_1, moe_w2_1, final_norm):
    b, s_len, d = x.shape
    t = b * s_len
    cos2, sin2 = _rope_tables(s_len)
    x2d = x.reshape(t, d)

    w_main, w_if_t = _even_weights(w_in_0)
    z0, gt = _norm_matmul(x2d, mix_norm_0, w_main, w_if_t)
    z0 = z0.reshape(b, s_len, EVEN_N)
    h_m = _mlstm(z0, gt, mlstm_conv_0, mlstm_gate_b_0, mlstm_head_norm_0)
    o_b = _moba(z0, cos2, sin2)
    w_out = w_out_0.astype(BF16)
    h = _proj_moe(x2d, [h_m.reshape(t, MLSTM_W), o_b.reshape(t, MOBA_W)], [w_out[:MLSTM_W], w_out[MLSTM_W:]],
                  ffn_norm_0, router_group_0, router_group_b_0, router_expert_0, router_expert_b_0,
                  moe_w1_0, moe_w3_0, moe_w2_0, final_norm, False)

    z1 = _norm_matmul(h, mix_norm_1, _odd_weights(w_in_1)).reshape(b, s_len, ODD_N)
    k_cmp, v_cmp = _nsa_compress(z1, nsa_cmp_pos_1, nsa_cmp_k1_1.astype(BF16), nsa_cmp_k2_1.astype(BF16),
                                 nsa_cmp_v1_1.astype(BF16), nsa_cmp_v2_1.astype(BF16))
    o = _nsa_attention(z1, k_cmp, v_cmp, cos2, sin2)
    h = _proj_moe(h, [o.reshape(t, NSA_W)], [w_out_1.astype(BF16)],
                  ffn_norm_1, router_group_1, router_group_b_1, router_expert_1, router_expert_b_1,
                  moe_w1_1, moe_w3_1, moe_w2_1, final_norm, True)
    return h.reshape(b, s_len, d)
```

```python
import functools

import numpy as np
import jax
import jax.numpy as jnp
from jax import lax
from jax.experimental import pallas as pl
from jax.experimental.pallas import tpu as pltpu

F32 = jnp.float32
BF16 = jnp.bfloat16
HIGHEST = lax.Precision.HIGHEST

LANES = 128
BF16_SUBLANES = 16
D_MODEL = 1024
HEAD_DIM = 128
ROPE_THETA = 10000.0
NORM_EPS = 1e-6
NEG_INF = -1e30
FORCED_SCORE = 1e4

MLSTM_HEADS = 4
MLSTM_W = MLSTM_HEADS * HEAD_DIM
MLSTM_CHUNK = 64
MLSTM_CONV = 4
MLSTM_GATE_CAP = 15.0
MOBA_HEADS = 4
MOBA_W = MOBA_HEADS * HEAD_DIM
MOBA_BLOCK = 256
MOBA_TOPK = 3

NSA_HEADS = 8
NSA_GROUPS = 2
NSA_REP = NSA_HEADS // NSA_GROUPS
NSA_W = NSA_HEADS * HEAD_DIM
NSA_KV_W = NSA_GROUPS * HEAD_DIM
CMP_LEN = 32
CMP_STRIDE = 16
CMP_HIDDEN = 256
SEL_BLOCK = 64
SEL_TOPN = 8
WINDOW = 512
NSA_TQ = 256

MOE_GROUPS = 4
MOE_EPG = 4
MOE_EXPERTS = MOE_GROUPS * MOE_EPG
MOE_HIDDEN = D_MODEL // 4
ROUTER_SLOTS = 32

EVEN_N = 4 * MLSTM_W + 3 * MOBA_W + LANES
EVEN_IF_BLK = (4 * MLSTM_W + 3 * MOBA_W) // LANES
ODD_GATE_BLK = (NSA_W + 6 * NSA_KV_W) // LANES
ODD_N = NSA_W + 6 * NSA_KV_W + LANES
NSA_GATES = 3

VMEM_LIMIT = 56 * 1024 * 1024
VT_ROWS = HEAD_DIM + BF16_SUBLANES
LOG2_E = float(np.log2(np.e))

_NT = (((1,), (1,)), ((), ()))
_TN = (((0,), (0,)), ((), ()))


def _cparams(sem):
    return pltpu.CompilerParams(dimension_semantics=sem, vmem_limit_bytes=VMEM_LIMIT)


def _rms(x, g):
    return x * lax.rsqrt(jnp.mean(x * x, axis=-1, keepdims=True) + NORM_EPS) * g


def _sigmoid(x):
    return 0.5 * jnp.tanh(0.5 * x) + 0.5


def _log_sigmoid(x):
    return -(jnp.maximum(-x, 0.0) + jnp.log1p(jnp.exp(-jnp.abs(x))))


def _rotate(x, cos2, sin2):
    return x * cos2 + pltpu.roll(x, HEAD_DIM // 2, axis=1) * sin2


def _flash_steps(qs, k_ts, vt_ts, biases, carries):
    n = len(qs)
    ss = [lax.dot_general(k_ts[u], qs[u], _NT, preferred_element_type=F32) + biases[u] for u in range(n)]
    ms, es, alphas, pvs = [], [], [], []

    def value_matmul(u):
        pv = jnp.dot(vt_ts[u], es[u], preferred_element_type=F32)
        if carries is not None:
            pv = alphas[u] * carries[2 * u + 1] + pv
        pvs.append(pv)

    for u in range(n):
        m_new = jnp.max(ss[u], axis=0, keepdims=True)
        if carries is not None:
            m_new = jnp.maximum(carries[2 * u], m_new)
            alphas.append(jnp.exp2(carries[2 * u] - m_new))
        ms.append(m_new)
        es.append(jnp.exp2(ss[u] - m_new).astype(BF16))
        if u > 0:
            value_matmul(u - 1)
    value_matmul(n - 1)
    out = []
    for u in range(n):
        out += [ms[u], pvs[u]]
    return out


def _flash_output(acc):
    return acc[:HEAD_DIM] / jnp.maximum(acc[HEAD_DIM:HEAD_DIM + 1], 1e-30)


def _store_transposed(dst_ref, x):
    n_rows, n_cols = x.shape
    for r in range(0, n_rows, LANES):
        dst_ref[:n_cols, r:r + LANES] = x[r:r + LANES, :].T.astype(dst_ref.dtype)
    dst_ref[n_cols:, :] = jnp.ones((dst_ref.shape[0] - n_cols, n_rows), dst_ref.dtype)


def _rank_before(v, n_valid, axis):
    idx = lax.broadcasted_iota(jnp.int32, v.shape, axis)
    rank = jnp.zeros(v.shape, F32)
    for m in range(n_valid):
        vm = v[m:m + 1, :] if axis == 0 else v[:, m:m + 1]
        tie = jnp.where(idx > m, 1.0, 0.0)
        rank = rank + jnp.where(vm > v, 1.0, jnp.where(vm == v, tie, 0.0))
    return rank


def _norm_matmul_body(*refs, n_chunk, with_t):
    if with_t:
        x_ref, g_ref, w_ref, wt_ref, o_ref, ot_ref = refs
    else:
        x_ref, g_ref, w_ref, o_ref = refs
    yb = _rms(x_ref[...], g_ref[...]).astype(BF16)
    n = w_ref.shape[1]
    for c0 in range(0, n, n_chunk):
        c1 = min(n, c0 + n_chunk)
        o_ref[:, c0:c1] = jnp.dot(yb, w_ref[:, c0:c1], preferred_element_type=F32)
    if with_t:
        ot_ref[...] = lax.dot_general(wt_ref[...], yb, _NT, preferred_element_type=F32)


def _norm_matmul(x2d, gain, w, wt=None, tm=512):
    t, d = x2d.shape
    n = w.shape[1]
    with_t = wt is not None
    in_specs = [pl.BlockSpec((tm, d), lambda i: (i, 0)),
                pl.BlockSpec((1, d), lambda i: (0, 0)),
                pl.BlockSpec((d, n), lambda i: (0, 0))]
    out_specs = [pl.BlockSpec((tm, n), lambda i: (i, 0))]
    out_shape = [jax.ShapeDtypeStruct((t, n), F32)]
    args = [x2d, gain.reshape(1, d), w]
    if with_t:
        r = wt.shape[0]
        in_specs.append(pl.BlockSpec((r, d), lambda i: (0, 0)))
        out_specs.append(pl.BlockSpec((r, tm), lambda i: (0, i)))
        out_shape.append(jax.ShapeDtypeStruct((r, t), F32))
        args.append(wt)
    outs = pl.pallas_call(
        functools.partial(_norm_matmul_body, n_chunk=512, with_t=with_t),
        grid=(t // tm,), in_specs=in_specs, out_specs=out_specs, out_shape=out_shape,
        compiler_params=_cparams(("parallel",)), name="norm_in_proj")(*args)
    return outs if with_t else outs[0]


def _chunk_cumsum(x, axis):
    idx = lax.broadcasted_iota(jnp.int32, x.shape, axis) % MLSTM_CHUNK
    d = 1
    while d < MLSTM_CHUNK:
        x = x + jnp.where(idx >= d, pltpu.roll(x, d, axis=axis), 0.0)
        d *= 2
    return x


MLSTM_HEADS_PER_STEP = 2


def _mlstm_body(q_ref, k_ref, v_ref, og_ref, if_ref, gt_ref, cwq_ref, cwk_ref, bcol_ref, brow_ref,
                gain_ref, out_ref, ks, qts, vts, hts, css, brs, lirs, gcol):
    hp = pl.program_id(1)
    s_len = q_ref.shape[1]
    n_local = q_ref.shape[2] // HEAD_DIM
    n_chunks = s_len // MLSTM_CHUNK
    L = MLSTM_CHUNK
    per_slab = LANES // L
    row = lax.broadcasted_iota(jnp.int32, (s_len, HEAD_DIM), 0)
    cols = [slice(a * HEAD_DIM, (a + 1) * HEAD_DIM) for a in range(n_local)]

    def conv_silu(x, w):
        acc = x * w[MLSTM_CONV - 1:MLSTM_CONV, :]
        for d in range(1, MLSTM_CONV):
            shifted = jnp.where(row >= d, pltpu.roll(x, d, axis=0), 0.0)
            acc = acc + shifted * w[MLSTM_CONV - 1 - d:MLSTM_CONV - d, :]
        return acc * _sigmoid(acc)

    def store_chunks_transposed(dst, x):
        for p in range(s_len // LANES):
            slab_t = x[p * LANES:(p + 1) * LANES, :].T
            for j in range(per_slab):
                dst[per_slab * p + j] = slab_t[:, j * L:(j + 1) * L].astype(dst.dtype)

    lane = lax.broadcasted_iota(jnp.int32, gcol.shape, 1)

    @pl.when(hp == 0)
    def _():
        pre = if_ref[0] + bcol_ref[...]
        pre = MLSTM_GATE_CAP * jnp.tanh(pre / MLSTM_GATE_CAP)
        gcol[...] = jnp.where(lane < MLSTM_HEADS, pre, _chunk_cumsum(_log_sigmoid(pre), 0))

    pr = gt_ref[...] + brow_ref[...]
    pr = MLSTM_GATE_CAP * jnp.tanh(pr / MLSTM_GATE_CAP)
    sub = lax.broadcasted_iota(jnp.int32, pr.shape, 0)
    b_rows = _chunk_cumsum(_log_sigmoid(pr), 1)
    g_all = gcol[...]

    gains = []
    for a in range(n_local):
        h = hp * n_local + a
        store_chunks_transposed(qts.at[a], conv_silu(q_ref[0, :, cols[a]], cwq_ref[:, cols[a]]) * (HEAD_DIM ** -0.5))
        store_chunks_transposed(vts.at[a], v_ref[0, :, cols[a]])
        ks[:, cols[a]] = conv_silu(k_ref[0, :, cols[a]], cwk_ref[:, cols[a]]).astype(BF16)
        css[a] = jnp.sum(jnp.where(lane == h, g_all, jnp.where(lane == h + MLSTM_HEADS, -g_all, 0.0)),
                         axis=1, keepdims=True)
        li_row = jnp.sum(jnp.where(sub == h, pr, 0.0), axis=0, keepdims=True)
        b_row = jnp.sum(jnp.where(sub == h + MLSTM_HEADS, b_rows, 0.0), axis=0, keepdims=True)
        for c in range(n_chunks):
            brs[a, c] = b_row[:, c * L:(c + 1) * L]
            lirs[a, c] = li_row[:, c * L:(c + 1) * L]
        gains.append(jnp.broadcast_to(gain_ref[a * HEAD_DIM:(a + 1) * HEAD_DIM, :], (HEAD_DIM, L)))

    tri = lax.broadcasted_iota(jnp.int32, (L, L), 0) <= lax.broadcasted_iota(jnp.int32, (L, L), 1)

    group_size = 16
    assert n_chunks % group_size == 0

    def group(gi, carry):
        states = [list(carry[3 * a:3 * a + 3]) for a in range(n_local)]
        cs = [gi * group_size + j for j in range(group_size)]
        r0s = [pl.multiple_of(c * L, L) for c in cs]
        units = [(a, j) for a in range(n_local) for j in range(group_size)]
        ks_ = {(a, j): ks[pl.ds(r0s[j], L), cols[a]] for a, j in units}
        q_ts = {(a, j): qts[a, cs[j]] for a, j in units}
        v_ts = {(a, j): vts[a, cs[j]] for a, j in units}
        b_rs = {(a, j): brs[a, cs[j]] for a, j in units}
        kvs, ksums, kqs, g_maxs, b_lasts = {}, {}, {}, {}, {}
        for u in units:
            a, j = u
            b_last = b_rs[u][:, L - 1:L]
            g = b_last - b_rs[u] + lirs[a, cs[j]]
            g_max = jnp.max(g, axis=1, keepdims=True)
            w = jnp.exp(g - g_max)
            kvs[u] = jnp.dot((v_ts[u] * w).astype(BF16), ks_[u], preferred_element_type=F32)
            ksums[u] = jnp.dot(jnp.broadcast_to(w, (BF16_SUBLANES, L)).astype(BF16), ks_[u],
                               preferred_element_type=F32)[:1]
            kqs[u] = jnp.dot(ks_[u], q_ts[u], preferred_element_type=F32)
            g_maxs[u] = g_max
            b_lasts[u] = b_last
        c_ins, n_ins, m_ins = {}, {}, {}
        for u in units:
            a, j = u
            c_state, n_state, m_state = states[a]
            c_ins[u] = c_state.astype(BF16)
            n_ins[u] = jnp.broadcast_to(n_state, (BF16_SUBLANES, HEAD_DIM)).astype(BF16)
            m_ins[u] = m_state
            m_new = jnp.maximum(b_lasts[u] + m_state, g_maxs[u])
            sa = jnp.exp(b_lasts[u] + m_state - m_new)
            cc = jnp.exp(g_maxs[u] - m_new)
            states[a] = [sa * c_state + cc * kvs[u], sa * n_state + cc * ksums[u], m_new]
        c_qs = {u: jnp.dot(c_ins[u], q_ts[u], preferred_element_type=F32) for u in units}
        n_qs = {u: jnp.dot(n_ins[u], q_ts[u], preferred_element_type=F32)[:1] for u in units}
        ss, m_ts, w_inters = {}, {}, {}
        for u in units:
            a, j = u
            dmat = jnp.where(tri, b_rs[u] + css[a, pl.ds(r0s[j], L), :], NEG_INF)
            inter = b_rs[u] + m_ins[u]
            m_t = jnp.maximum(inter, jnp.max(dmat, axis=0, keepdims=True))
            ss[u] = kqs[u] * jnp.exp(dmat - m_t)
            m_ts[u] = m_t
            w_inters[u] = jnp.exp(inter - m_t)
        svs = {u: jnp.dot(v_ts[u].astype(BF16), ss[u].astype(BF16), preferred_element_type=F32) for u in units}
        for u in units:
            a, j = u
            num = svs[u] + w_inters[u] * c_qs[u]
            den = jnp.sum(ss[u], axis=0, keepdims=True) + w_inters[u] * n_qs[u]
            ht = num / jnp.maximum(jnp.abs(den), jnp.exp(-m_ts[u]))
            hts[a, cs[j]] = ht * lax.rsqrt(jnp.mean(ht * ht, axis=0, keepdims=True) + NORM_EPS) * gains[a]
        return tuple(x for st in states for x in st)

    init = (jnp.zeros((HEAD_DIM, HEAD_DIM), F32), jnp.zeros((1, HEAD_DIM), F32), jnp.zeros((1, 1), F32)) * n_local
    lax.fori_loop(0, n_chunks // group_size, group, init)

    for a in range(n_local):
        for p in range(s_len // LANES):
            slab_t = jnp.concatenate([hts[a, per_slab * p + j] for j in range(per_slab)], axis=1)
            rows = slice(p * LANES, (p + 1) * LANES)
            out_ref[0, rows, cols[a]] = (slab_t.T * _sigmoid(og_ref[0, rows, cols[a]])).astype(out_ref.dtype)


def _mlstm(z3, gt, conv_w, gate_b, head_gain):
    b, s_len, _ = z3.shape
    n_chunks = s_len // MLSTM_CHUNK
    gr = gt.shape[0]
    bcol = jnp.pad(gate_b, (0, LANES - 2 * MLSTM_HEADS)).reshape(1, LANES)
    brow = jnp.pad(gate_b, (0, gr - 2 * MLSTM_HEADS)).reshape(gr, 1)
    n_local = MLSTM_HEADS_PER_STEP
    n_steps = MLSTM_HEADS // n_local
    width = n_local * HEAD_DIM

    def col(off):
        return pl.BlockSpec((1, s_len, width), lambda bi, hi: (bi, 0, off * n_steps + hi))

    in_specs = [col(0), col(1), col(2), col(3),
                pl.BlockSpec((1, s_len, LANES), lambda bi, hi: (bi, 0, EVEN_IF_BLK)),
                pl.BlockSpec((gr, s_len), lambda bi, hi: (0, bi)),
                pl.BlockSpec((MLSTM_CONV, width), lambda bi, hi: (0, hi)),
                pl.BlockSpec((MLSTM_CONV, width), lambda bi, hi: (0, n_steps + hi)),
                pl.BlockSpec((1, LANES), lambda bi, hi: (0, 0)),
                pl.BlockSpec((gr, 1), lambda bi, hi: (0, 0)),
                pl.BlockSpec((width, 1), lambda bi, hi: (hi, 0))]
    chunk_t = (n_local, n_chunks, HEAD_DIM, MLSTM_CHUNK)
    rows_t = (n_local, n_chunks, 1, MLSTM_CHUNK)
    scratch = [pltpu.VMEM((s_len, width), BF16),
               pltpu.VMEM(chunk_t, BF16), pltpu.VMEM(chunk_t, F32), pltpu.VMEM(chunk_t, F32),
               pltpu.VMEM((n_local, s_len, 1), F32),
               pltpu.VMEM(rows_t, F32), pltpu.VMEM(rows_t, F32),
               pltpu.VMEM((s_len, LANES), F32)]
    return pl.pallas_call(
        _mlstm_body, grid=(b, n_steps), in_specs=in_specs,
        out_specs=pl.BlockSpec((1, s_len, width), lambda bi, hi: (bi, 0, hi)),
        out_shape=jax.ShapeDtypeStruct((b, s_len, MLSTM_W), BF16),
        scratch_shapes=scratch, compiler_params=_cparams(("parallel", "arbitrary")),
        name="mlstm")(z3, z3, z3, z3, z3, gt, conv_w, conv_w, bcol, brow, head_gain.reshape(MLSTM_W, 1))


MOBA_ROWS = 2


def _moba_body(q_ref, k_ref, v_ref, cos_ref, sin_ref, o_ref, kr_scr, vt_scr, km_scr, sel_scr):
    i = pl.program_id(1)
    n_rows, s_len = k_ref.shape[0], k_ref.shape[1]
    bs = MOBA_BLOCK
    nb = s_len // bs
    heads = [slice(h * HEAD_DIM, (h + 1) * HEAD_DIM) for h in range(MOBA_HEADS)]
    units = [(b, h) for b in range(n_rows) for h in range(MOBA_HEADS)]

    @pl.when(i == 0)
    def _():
        for u, (b, h) in enumerate(units):
            _store_transposed(vt_scr.at[u], v_ref[b, :, heads[h]])
            kr = _rotate(k_ref[b, :, heads[h]], cos_ref[...], sin_ref[...])
            kr_scr[b, :, heads[h]] = kr.astype(BF16)
            rows = [jnp.sum(kr[n * bs:(n + 1) * bs, :], axis=0, keepdims=True) / float(bs) for n in range(nb)]
            rows.append(jnp.zeros((km_scr.shape[1] - nb, HEAD_DIM), F32))
            km_scr[u] = jnp.concatenate(rows, axis=0)

    t0 = pl.multiple_of(i * bs, bs)
    cos_q = cos_ref[pl.ds(t0, bs), :]
    sin_q = sin_ref[pl.ds(t0, bs), :]
    blk = lax.broadcasted_iota(jnp.int32, (km_scr.shape[1], bs), 0)
    causal_bias = jnp.where(lax.broadcasted_iota(jnp.int32, (bs, bs), 0)
                            <= lax.broadcasted_iota(jnp.int32, (bs, bs), 1), 0.0, NEG_INF)

    qbs = []
    for u, (b, h) in enumerate(units):
        qr = _rotate(q_ref[b, :, heads[h]], cos_q, sin_q)
        gate_t = lax.dot_general(km_scr[u], qr, _NT, precision=HIGHEST, preferred_element_type=F32)
        val = jnp.where(blk < i, gate_t, NEG_INF)
        picked = jnp.where(val > 0.5 * NEG_INF, _rank_before(val, nb, 0), float(nb)) < MOBA_TOPK
        sel_scr[u] = jnp.where(picked, 0.0, NEG_INF)
        qbs.append((qr * (HEAD_DIM ** -0.5 * LOG2_E)).astype(BF16))

    def key_tiles(k0, n_blocks):
        return ([kr_scr[b, pl.ds(k0, n_blocks * bs), heads[h]] for b, h in units],
                [vt_scr[u, :, pl.ds(k0, n_blocks * bs)] for u in range(len(units))])

    init = tuple(_flash_steps(qbs, *key_tiles(t0, 1), [causal_bias] * len(units), None))

    def past_blocks(j, n_blocks, carry):
        biases = [jnp.concatenate([jnp.broadcast_to(sel_scr[u, pl.ds(j + d, 1), :], (bs, bs))
                                   for d in range(n_blocks)], axis=0) for u in range(len(units))]
        k0 = j * bs if isinstance(j, int) else pl.multiple_of(j * bs, bs)
        return tuple(_flash_steps(qbs, *key_tiles(k0, n_blocks), biases, carry))

    odd = i % 2
    carry = lax.cond(odd == 1, lambda c: past_blocks(0, 1, c), lambda c: c, init)
    fin = lax.fori_loop(0, i // 2, lambda p, c: past_blocks(odd + 2 * p, 2, c), carry)
    for u, (b, h) in enumerate(units):
        o_ref[b, :, heads[h]] = _flash_output(fin[2 * u + 1]).T.astype(o_ref.dtype)


def _moba(z3, cos2, sin2):
    b, s_len, _ = z3.shape
    nb = s_len // MOBA_BLOCK
    q_off = 4 * MLSTM_W // MOBA_W
    rows = MOBA_ROWS if b % MOBA_ROWS == 0 else 1
    n_units = rows * MOBA_HEADS

    def kv(off):
        return pl.BlockSpec((rows, s_len, MOBA_W), lambda bi, i: (bi, 0, off))

    in_specs = [pl.BlockSpec((rows, MOBA_BLOCK, MOBA_W), lambda bi, i: (bi, i, q_off)),
                kv(q_off + 1), kv(q_off + 2),
                pl.BlockSpec((s_len, HEAD_DIM), lambda bi, i: (0, 0)),
                pl.BlockSpec((s_len, HEAD_DIM), lambda bi, i: (0, 0))]
    scratch = [pltpu.VMEM((rows, s_len, MOBA_W), BF16),
               pltpu.VMEM((n_units, VT_ROWS, s_len), BF16),
               pltpu.VMEM((n_units, BF16_SUBLANES, HEAD_DIM), F32),
               pltpu.VMEM((n_units, BF16_SUBLANES, MOBA_BLOCK), F32)]
    return pl.pallas_call(
        _moba_body, grid=(b // rows, nb), in_specs=in_specs,
        out_specs=pl.BlockSpec((rows, MOBA_BLOCK, MOBA_W), lambda bi, i: (bi, i, 0)),
        out_shape=jax.ShapeDtypeStruct((b, s_len, MOBA_W), BF16),
        scratch_shapes=scratch, compiler_params=_cparams(("parallel", "arbitrary")),
        name="moba")(z3, z3, z3, cos2, sin2)


def _gelu_tanh(x):
    return x * (0.5 * (1.0 + jnp.tanh(np.sqrt(2.0 / np.pi) * (x + 0.044715 * (x * x * x)))))


def _nsa_compress_body(kc_ref, vc_ref, pe_ref, wk1_ref, wk2_ref, wv1_ref, wv2_ref, ko_ref, vo_ref):
    n_rows = kc_ref.shape[1] // CMP_STRIDE
    halves = CMP_LEN // CMP_STRIDE
    assert halves == 2

    def compress(x_ref, pe, w1_ref, w2_ref):
        rows = [x_ref[0, pl.ds(l, n_rows, stride=CMP_STRIDE), :] for l in range(CMP_STRIDE)]
        half = CMP_STRIDE * HEAD_DIM
        first = jnp.concatenate([(r + pe[l:l + 1, :]).astype(BF16) for l, r in enumerate(rows)], axis=1)
        second = jnp.concatenate([(r + pe[CMP_STRIDE + l:CMP_STRIDE + l + 1, :]).astype(BF16)
                                  for l, r in enumerate(rows)], axis=1)
        ya = jnp.dot(first, w1_ref[:half, :], preferred_element_type=F32)
        yb = jnp.dot(second, w1_ref[half:, :], preferred_element_type=F32)
        pre = ya + pltpu.roll(yb, n_rows - 1, axis=0)
        return jnp.dot(_gelu_tanh(pre).astype(BF16), w2_ref[...], preferred_element_type=F32)

    ko_ref[0, 0] = compress(kc_ref, pe_ref[0], wk1_ref, wk2_ref)
    vo_ref[0, 0] = compress(vc_ref, pe_ref[1], wv1_ref, wv2_ref)


def _nsa_compress(z3, cmp_pos, wk1, wk2, wv1, wv2):
    b, s_len, _ = z3.shape
    n_rows = s_len // CMP_STRIDE
    kc_off = NSA_W // LANES
    vc_off = kc_off + NSA_GROUPS
    const2 = lambda bi, gi: (0, 0)
    in_specs = [pl.BlockSpec((1, s_len, HEAD_DIM), lambda bi, gi: (bi, 0, kc_off + gi)),
                pl.BlockSpec((1, s_len, HEAD_DIM), lambda bi, gi: (bi, 0, vc_off + gi)),
                pl.BlockSpec(cmp_pos.shape, lambda bi, gi: (0, 0, 0)),
                pl.BlockSpec(wk1.shape, const2), pl.BlockSpec(wk2.shape, const2),
                pl.BlockSpec(wv1.shape, const2), pl.BlockSpec(wv2.shape, const2)]
    out_spec = pl.BlockSpec((1, 1, n_rows, HEAD_DIM), lambda bi, gi: (bi, gi, 0, 0))
    out_sds = jax.ShapeDtypeStruct((b, NSA_GROUPS, n_rows, HEAD_DIM), F32)
    return pl.pallas_call(
        _nsa_compress_body, grid=(b, NSA_GROUPS), in_specs=in_specs,
        out_specs=[out_spec, out_spec], out_shape=[out_sds, out_sds],
        compiler_params=_cparams(("parallel", "parallel")), name="nsa_compress")(
            z3, z3, cmp_pos, wk1, wk2, wv1, wv2)


def _nsa_body(q_ref, ks_ref, vs_ref, kw_ref, vw_ref, kc_ref, vc_ref, gate_ref, cos_ref, sin_ref, ov_ref,
              o_ref, krs, vst, krw, vwt, sel_scr):
    i = pl.program_id(1)
    tq = NSA_TQ
    tk = NSA_TQ
    scale = HEAD_DIM ** -0.5
    n_sel_blk = ks_ref.shape[1] // SEL_BLOCK
    n_cmp = (ks_ref.shape[1] - CMP_LEN) // CMP_STRIDE + 1
    shift = SEL_BLOCK.bit_length() - 1
    assert 1 << shift == SEL_BLOCK and n_sel_blk <= LANES
    blk_per_tile = tk // SEL_BLOCK
    groups = [slice(g * HEAD_DIM, (g + 1) * HEAD_DIM) for g in range(NSA_GROUPS)]

    @pl.when(i == 0)
    def _():
        for g, cols in enumerate(groups):
            krs[:, cols] = _rotate(ks_ref[0, :, cols], cos_ref[...], sin_ref[...]).astype(BF16)
            krw[:, cols] = _rotate(kw_ref[0, :, cols], cos_ref[...], sin_ref[...]).astype(BF16)
            _store_transposed(vst.at[g], vs_ref[0, :, cols])
            _store_transposed(vwt.at[g], vw_ref[0, :, cols])

    t0 = pl.multiple_of(i * tq, tq)
    pos_row = t0 + lax.broadcasted_iota(jnp.int32, (1, tq), 1)
    cos_q = cos_ref[pl.ds(t0, tq), :]
    sin_q = sin_ref[pl.ds(t0, tq), :]

    n_col = lax.broadcasted_iota(jnp.int32, (LANES, 1), 0)
    cmp_end = jnp.where(n_col < n_cmp, n_col * CMP_STRIDE + (CMP_LEN - 1), jnp.iinfo(jnp.int32).max)
    cmp_ok = cmp_end <= pos_row
    blk_i = lax.broadcasted_iota(jnp.int32, (n_sel_blk, tq), 0)
    behind = (pos_row >> shift) - blk_i
    units = [(g, r) for g in range(NSA_GROUPS) for r in range(NSA_REP)]
    qr_heads, o_cmp = [], []
    for g in range(NSA_GROUPS):
        kc = kc_ref[0, g].astype(BF16)
        vc = vc_ref[0, g].astype(BF16)
        p_sum = jnp.zeros((LANES, tq), F32)
        for r in range(NSA_REP):
            head = (g * NSA_REP + r) * HEAD_DIM
            q = q_ref[0, :, head:head + HEAD_DIM] * scale
            qr_heads.append(_rotate(q * LOG2_E, cos_q, sin_q).astype(BF16))
            s = lax.dot_general(kc, q.astype(BF16), _NT, preferred_element_type=F32)
            s = jnp.where(cmp_ok, s, NEG_INF)
            e = jnp.where(cmp_ok, jnp.exp(s - jnp.max(s, axis=0, keepdims=True)), 0.0)
            p = e / jnp.maximum(jnp.sum(e, axis=0, keepdims=True), 1e-30)
            o_cmp.append(lax.dot_general(p.astype(BF16), vc, _TN, preferred_element_type=F32))
            p_sum = p_sum + p
        imp = jnp.dot(ov_ref[...], p_sum, precision=HIGHEST, preferred_element_type=F32)
        val = jnp.where(behind == 0, FORCED_SCORE, jnp.where(behind == 1, FORCED_SCORE, imp))
        val = jnp.where(blk_i == 0, FORCED_SCORE, val)
        val = jnp.where(behind >= 0, val, NEG_INF)
        picked = jnp.where(val > 0.5 * NEG_INF, _rank_before(val, n_sel_blk, 0), float(n_sel_blk)) < SEL_TOPN
        sel_scr[g] = jnp.where(picked, 0.0, NEG_INF)

    def sel_bias(g, k0, n_tiles):
        blk0 = (k0 >> shift) if isinstance(k0, int) else pl.multiple_of(k0 >> shift, blk_per_tile)
        return jnp.concatenate([jnp.broadcast_to(sel_scr[g, pl.ds(blk0 + b, 1), :], (SEL_BLOCK, tq))
                                for b in range(n_tiles * blk_per_tile)], axis=0)

    def sel_tiles(k0, n_tiles, carry):
        biases = [sel_bias(g, k0, n_tiles) for g in range(NSA_GROUPS)]
        return tuple(_flash_steps(qr_heads, [krs[pl.ds(k0, n_tiles * tk), groups[g]] for g, _ in units],
                                  [vst[g, :, pl.ds(k0, n_tiles * tk)] for g, _ in units],
                                  [biases[g] for g, _ in units], carry))

    causal_bias = jnp.where(lax.broadcasted_iota(jnp.int32, (tk, tq), 0)
                            <= lax.broadcasted_iota(jnp.int32, (tk, tq), 1), 0.0, NEG_INF)
    span = WINDOW + tq
    start = pl.multiple_of(jnp.maximum(t0 - WINDOW, 0), tq)
    gap = ((t0 - start) + lax.broadcasted_iota(jnp.int32, (span, tq), 1)
           - lax.broadcasted_iota(jnp.int32, (span, tq), 0))
    win_bias = jnp.where(gap >= 0, jnp.where(gap < WINDOW, 0.0, NEG_INF), NEG_INF)
    diag_biases = [sel_bias(g, t0, 1) + causal_bias for g in range(NSA_GROUPS)]
    n = len(units)
    group = _flash_steps(qr_heads * 2,
                         [krs[pl.ds(t0, tk), groups[g]] for g, _ in units]
                         + [krw[pl.ds(start, span), groups[g]] for g, _ in units],
                         [vst[g, :, pl.ds(t0, tk)] for g, _ in units]
                         + [vwt[g, :, pl.ds(start, span)] for g, _ in units],
                         [diag_biases[g] for g, _ in units] + [win_bias] * n, None)
    win_fin = group[2 * n:]

    odd = i % 2
    carry = lax.cond(odd == 1, lambda c: sel_tiles(0, 1, c), lambda c: c, tuple(group[:2 * n]))
    sel_fin = lax.fori_loop(0, i // 2, lambda p, c: sel_tiles(pl.multiple_of((odd + 2 * p) * tk, tk), 2, c), carry)

    gates = _sigmoid(gate_ref[0])
    gates_t = gates.T
    for u in range(n):
        c0 = NSA_GATES * u
        o_t = (gates_t[c0 + 1:c0 + 2, :] * _flash_output(sel_fin[2 * u + 1])
               + gates_t[c0 + 2:c0 + 3, :] * _flash_output(win_fin[2 * u + 1]))
        o_ref[0, :, u * HEAD_DIM:(u + 1) * HEAD_DIM] = (gates[:, c0:c0 + 1] * o_cmp[u] + o_t.T).astype(o_ref.dtype)


def _nsa_attention(z3, k_cmp, v_cmp, cos2, sin2):
    b, s_len, _ = z3.shape
    assert s_len >= WINDOW + NSA_TQ and s_len % NSA_TQ == 0
    n_cmp = (s_len - CMP_LEN) // CMP_STRIDE + 1
    n_blk = s_len // SEL_BLOCK
    n_rows = k_cmp.shape[2]
    assert n_rows == LANES
    cmp_start = np.arange(n_rows) * CMP_STRIDE
    cmp_end = cmp_start + CMP_LEN - 1
    blk_lo = np.arange(LANES) * SEL_BLOCK
    overlap = ((cmp_start[:, None] <= blk_lo[None, :] + SEL_BLOCK - 1) & (cmp_end[:, None] >= blk_lo[None, :])
               & (np.arange(n_rows)[:, None] < n_cmp)).astype(np.float32)
    overlap_t = np.ascontiguousarray(overlap.T[:n_blk])
    base = NSA_W // NSA_KV_W

    def kv(off):
        return pl.BlockSpec((1, s_len, NSA_KV_W), lambda bi, i: (bi, 0, base + off))

    cmp_spec = pl.BlockSpec((1, NSA_GROUPS, n_rows, HEAD_DIM), lambda bi, i: (bi, 0, 0, 0))
    const2 = lambda bi, i: (0, 0)
    in_specs = [pl.BlockSpec((1, NSA_TQ, NSA_W), lambda bi, i: (bi, i, 0)),
                kv(2), kv(3), kv(4), kv(5), cmp_spec, cmp_spec,
                pl.BlockSpec((1, NSA_TQ, LANES), lambda bi, i: (bi, i, ODD_GATE_BLK)),
                pl.BlockSpec((s_len, HEAD_DIM), const2), pl.BlockSpec((s_len, HEAD_DIM), const2),
                pl.BlockSpec((n_blk, n_rows), const2)]
    vt_shape = (NSA_GROUPS, VT_ROWS, s_len)
    scratch = [pltpu.VMEM((s_len, NSA_KV_W), BF16), pltpu.VMEM(vt_shape, BF16),
               pltpu.VMEM((s_len, NSA_KV_W), BF16), pltpu.VMEM(vt_shape, BF16),
               pltpu.VMEM((NSA_GROUPS, n_blk, NSA_TQ), F32)]
    return pl.pallas_call(
        _nsa_body, grid=(b, s_len // NSA_TQ), in_specs=in_specs,
        out_specs=pl.BlockSpec((1, NSA_TQ, NSA_W), lambda bi, i: (bi, i, 0)),
        out_shape=jax.ShapeDtypeStruct((b, s_len, NSA_W), BF16),
        scratch_shapes=scratch, compiler_params=_cparams(("parallel", "arbitrary")),
        name="nsa_attention")(z3, z3, z3, z3, z3, k_cmp, v_cmp, z3, cos2, sin2, jnp.asarray(overlap_t))


def _split_bf16(x):
    hi = x.astype(BF16)
    return hi, (x - hi.astype(F32)).astype(BF16)


def _proj_moe_body(*refs, n_act, final_norm):
    res_ref = refs[0]
    a_refs = refs[1:1 + n_act]
    wo_refs = refs[1 + n_act:1 + 2 * n_act]
    (g_ref, wr_ref, br_ref, w1_ref, w3_ref, w2_ref, fg_ref, o_ref,
     xn_scr, cw_scr, hd_scr) = refs[1 + 2 * n_act:]
    j = pl.program_id(1)
    tm = res_ref.shape[0]
    lane = lax.broadcasted_iota(jnp.int32, (tm, LANES), 1).astype(F32)
    neg = float("-inf")

    @pl.when(j == 0)
    def _():
        h = res_ref[...]
        for a_ref, wo_ref in zip(a_refs, wo_refs):
            h = h + jnp.dot(a_ref[...], wo_ref[...], preferred_element_type=F32)
        o_ref[...] = h
        xn = _rms(h, g_ref[...])
        xh, xl = _split_bf16(xn)
        xn_scr[...] = xh
        hi_terms = jnp.dot(xh, wr_ref[...], preferred_element_type=F32)
        logits = (hi_terms[:, :LANES] + (jnp.dot(xl, wr_ref[:, :LANES], preferred_element_type=F32)
                                         + hi_terms[:, LANES:])) + br_ref[...]
        lt = logits.T[:ROUTER_SLOTS]
        slot = lax.broadcasted_iota(jnp.int32, lt.shape, 0).astype(F32)
        is_g = slot < MOE_GROUPS
        gl = jnp.where(is_g, lt, neg)
        g_max = jnp.max(gl, axis=0, keepdims=True)
        g_w = 1.0 / jnp.sum(jnp.where(is_g, jnp.exp(gl - g_max), 0.0), axis=0, keepdims=True)
        g_top = jnp.min(jnp.where(gl == g_max, slot, float(LANES)), axis=0, keepdims=True)
        lo = MOE_GROUPS + MOE_EPG * g_top
        el = jnp.where(slot >= lo, jnp.where(slot < lo + MOE_EPG, lt, neg), neg)
        v1 = jnp.max(el, axis=0, keepdims=True)
        i1 = jnp.min(jnp.where(el == v1, slot, float(LANES)), axis=0, keepdims=True)
        el2 = jnp.where(slot == i1, neg, el)
        v2 = jnp.max(el2, axis=0, keepdims=True)
        i2 = jnp.min(jnp.where(el2 == v2, slot, float(LANES)), axis=0, keepdims=True)
        e2 = jnp.exp(v2 - v1)
        den = 1.0 + e2
        cw_t = jnp.where(slot == i1, g_w / den, 0.0) + jnp.where(slot == i2, g_w * e2 / den, 0.0)
        cw_scr[...] = jnp.concatenate([cw_t, jnp.zeros((LANES - ROUTER_SLOTS, tm), F32)], axis=0).T

    xn = xn_scr[...]
    cw = cw_scr[...]
    for r in range(MOE_EPG):
        h1 = jnp.dot(xn, w1_ref[r].astype(BF16), preferred_element_type=F32)
        h3 = jnp.dot(xn, w3_ref[r], preferred_element_type=F32)
        e_lane = (MOE_GROUPS + MOE_EPG * j + r).astype(F32)
        col = jnp.sum(jnp.where(lane == e_lane, cw, 0.0), axis=1, keepdims=True)
        hd_scr[:, r * MOE_HIDDEN:(r + 1) * MOE_HIDDEN] = ((h1 * _sigmoid(h1)) * h3 * col).astype(BF16)
    o_ref[...] += jnp.dot(hd_scr[...], w2_ref[...].astype(BF16), preferred_element_type=F32)

    if final_norm:
        @pl.when(j == pl.num_programs(1) - 1)
        def _():
            o_ref[...] = _rms(o_ref[...], fg_ref[...])


def _proj_moe(res2d, acts, w_out, gain, w_g, b_g, w_e, b_e, w1, w3, w2, final_gain, final_norm, tm=1024):
    t, d = res2d.shape
    n_act = len(acts)
    n_slots = MOE_GROUPS + MOE_EXPERTS
    wr = jnp.pad(jnp.concatenate([w_g, w_e], axis=1), ((0, 0), (0, LANES - n_slots)))
    wr_cat = jnp.concatenate(_split_bf16(wr), axis=1)
    br = jnp.pad(jnp.concatenate([b_g, b_e]), (0, LANES - n_slots)).reshape(1, LANES)
    tile = lambda i, j: (i, 0)
    const = lambda i, j: (0, 0)
    group_hidden = MOE_EPG * MOE_HIDDEN
    in_specs = [pl.BlockSpec((tm, d), tile)]
    in_specs += [pl.BlockSpec((tm, a.shape[1]), tile) for a in acts]
    row0 = np.cumsum([0] + [a.shape[1] for a in acts])
    assert all(r % a.shape[1] == 0 for r, a in zip(row0, acts)) and row0[-1] == w_out.shape[0]
    in_specs += [pl.BlockSpec((a.shape[1], d), functools.partial(lambda i, j, blk: (blk, 0), blk=int(r // a.shape[1])))
                 for r, a in zip(row0, acts)]
    in_specs += [pl.BlockSpec((1, d), const),
                 pl.BlockSpec((d, 2 * LANES), const),
                 pl.BlockSpec((1, LANES), const),
                 pl.BlockSpec((MOE_EPG, d, MOE_HIDDEN), lambda i, j: (j, 0, 0)),
                 pl.BlockSpec((MOE_EPG, d, MOE_HIDDEN), lambda i, j: (j, 0, 0)),
                 pl.BlockSpec((group_hidden, d), lambda i, j: (j, 0)),
                 pl.BlockSpec((1, d), const)]
    scratch = [pltpu.VMEM((tm, d), BF16), pltpu.VMEM((tm, LANES), F32), pltpu.VMEM((tm, group_hidden), BF16)]
    return pl.pallas_call(
        functools.partial(_proj_moe_body, n_act=n_act, final_norm=final_norm),
        grid=(t // tm, MOE_GROUPS), in_specs=in_specs,
        out_specs=pl.BlockSpec((tm, d), tile),
        out_shape=jax.ShapeDtypeStruct((t, d), F32),
        scratch_shapes=scratch, compiler_params=_cparams(("parallel", "arbitrary")),
        name="out_proj_moe")(res2d, *acts, *([w_out] * n_act), gain.reshape(1, d), wr_cat, br,
                             w1, w3.astype(BF16), w2.reshape(MOE_EXPERTS * MOE_HIDDEN, d),
                             final_gain.reshape(1, d))


def _rope_tables(s_len):
    half = HEAD_DIM // 2
    inv = ROPE_THETA ** (-(jnp.arange(half, dtype=F32) / half))
    ang = jnp.arange(s_len, dtype=F32)[:, None] * inv[None, :]
    cos, sin = jnp.cos(ang), jnp.sin(ang)
    return jnp.concatenate([cos, cos], axis=-1), jnp.concatenate([-sin, sin], axis=-1)


def _even_weights(w_in):
    n_if = 2 * MLSTM_HEADS
    a = 4 * MLSTM_W
    w_in = w_in.astype(BF16)
    w_main = jnp.concatenate([w_in[:, :a], w_in[:, a + n_if:],
                              w_in[:, a:a + n_if], jnp.zeros((w_in.shape[0], LANES - n_if), w_in.dtype)], axis=1)
    w_if_t = jnp.concatenate([w_in[:, a:a + n_if].T,
                              jnp.zeros((BF16_SUBLANES - n_if, w_in.shape[0]), w_in.dtype)], axis=0)
    return w_main, w_if_t


def _odd_weights(w_in):
    a = NSA_W + 6 * NSA_KV_W
    n_gates = NSA_HEADS * NSA_GATES
    assert w_in.shape[1] == a + n_gates and n_gates <= LANES
    return jnp.pad(w_in.astype(BF16), ((0, 0), (0, LANES - n_gates)))


def kernel(x, mix_norm_0, w_in_0, mlstm_conv_0, mlstm_gate_b_0, mlstm_head_norm_0, w_out_0, ffn_norm_0, router_group_0, router_group_b_0, router_expert_0, router_expert_b_0, moe_w1_0, moe_w3_0, moe_w2_0, mix_norm_1, w_in_1, nsa_cmp_pos_1, nsa_cmp_k1_1, nsa_cmp_k2_1, nsa_cmp_v1_1, nsa_cmp_v2_1, w_out_1, ffn_norm_1, router_group_1, router_group_b_1, router_expert_1, router_expert_b_1, moe_w1_1, moe_w3_1, moe_w2_1, final_norm):
    b, s_len, d = x.shape
    t = b * s_len
    cos2, sin2 = _rope_tables(s_len)
    x2d = x.reshape(t, d)

    w_main, w_if_t = _even_weights(w_in_0)
    z0, gt = _norm_matmul(x2d, mix_norm_0, w_main, w_if_t)
    z0 = z0.reshape(b, s_len, EVEN_N)
    h_m = _mlstm(z0, gt, mlstm_conv_0, mlstm_gate_b_0, mlstm_head_norm_0)
    o_b = _moba(z0, cos2, sin2)
    h = _proj_moe(x2d, [h_m.reshape(t, MLSTM_W), o_b.reshape(t, MOBA_W)], w_out_0.astype(BF16),
                  ffn_norm_0, router_group_0, router_group_b_0, router_expert_0, router_expert_b_0,
                  moe_w1_0, moe_w3_0, moe_w2_0, final_norm, False)

    z1 = _norm_matmul(h, mix_norm_1, _odd_weights(w_in_1)).reshape(b, s_len, ODD_N)
    k_cmp, v_cmp = _nsa_compress(z1, nsa_cmp_pos_1, nsa_cmp_k1_1.astype(BF16), nsa_cmp_k2_1.astype(BF16),
                                 nsa_cmp_v1_1.astype(BF16), nsa_cmp_v2_1.astype(BF16))
    o = _nsa_attention(z1, k_cmp, v_cmp, cos2, sin2)
    h = _proj_moe(h, [o.reshape(t, NSA_W)], w_out_1.astype(BF16),
                  ffn_norm_1, router_group_1, router_group_b_1, router_expert_1, router_expert_b_1,
                  moe_w1_1, moe_w3_1, moe_w2_1, final_norm, True)
    return h.reshape(b, s_len, d)
```

```python
import functools

import numpy as np
import jax
import jax.numpy as jnp
from jax import lax
from jax.experimental import pallas as pl
from jax.experimental.pallas import tpu as pltpu

F32 = jnp.float32
BF16 = jnp.bfloat16
HIGHEST = lax.Precision.HIGHEST

LANES = 128
BF16_SUBLANES = 16
D_MODEL = 1024
HEAD_DIM = 128
ROPE_THETA = 10000.0
NORM_EPS = 1e-6
NEG_INF = -1e30
FORCED_SCORE = 1e4

MLSTM_HEADS = 4
MLSTM_W = MLSTM_HEADS * HEAD_DIM
MLSTM_CHUNK = 64
MLSTM_CONV = 4
MLSTM_GATE_CAP = 15.0
MOBA_HEADS = 4
MOBA_W = MOBA_HEADS * HEAD_DIM
MOBA_BLOCK = 256
MOBA_TOPK = 3

NSA_HEADS = 8
NSA_GROUPS = 2
NSA_REP = NSA_HEADS // NSA_GROUPS
NSA_W = NSA_HEADS * HEAD_DIM
NSA_KV_W = NSA_GROUPS * HEAD_DIM
CMP_LEN = 32
CMP_STRIDE = 16
CMP_HIDDEN = 256
SEL_BLOCK = 64
SEL_TOPN = 8
WINDOW = 512
NSA_TQ = 256

MOE_GROUPS = 4
MOE_EPG = 4
MOE_EXPERTS = MOE_GROUPS * MOE_EPG
MOE_HIDDEN = D_MODEL // 4
ROUTER_SLOTS = 32

EVEN_N = 4 * MLSTM_W + 3 * MOBA_W + LANES
EVEN_IF_BLK = (4 * MLSTM_W + 3 * MOBA_W) // LANES
ODD_GATE_BLK = (NSA_W + 6 * NSA_KV_W) // LANES
ODD_N = NSA_W + 6 * NSA_KV_W + LANES
NSA_GATES = 3

VMEM_LIMIT = 56 * 1024 * 1024
VT_ROWS = HEAD_DIM + BF16_SUBLANES
LOG2_E = float(np.log2(np.e))

_NT = (((1,), (1,)), ((), ()))
_TN = (((0,), (0,)), ((), ()))


def _cparams(sem):
    return pltpu.CompilerParams(dimension_semantics=sem, vmem_limit_bytes=VMEM_LIMIT)


def _rms(x, g):
    return x * lax.rsqrt(jnp.mean(x * x, axis=-1, keepdims=True) + NORM_EPS) * g


def _sigmoid(x):
    return 0.5 * jnp.tanh(0.5 * x) + 0.5


def _log_sigmoid(x):
    return -(jnp.maximum(-x, 0.0) + jnp.log1p(jnp.exp(-jnp.abs(x))))


def _rotate(x, cos2, sin2):
    return x * cos2 + pltpu.roll(x, HEAD_DIM // 2, axis=1) * sin2


def _flash_steps(qs, k_ts, vt_ts, biases, carries):
    n = len(qs)
    ss = [lax.dot_general(k_ts[u], qs[u], _NT, preferred_element_type=F32) + biases[u] for u in range(n)]
    ms, es, alphas, pvs = [], [], [], []

    def value_matmul(u):
        pv = jnp.dot(vt_ts[u], es[u], preferred_element_type=F32)
        if carries is not None:
            pv = alphas[u] * carries[2 * u + 1] + pv
        pvs.append(pv)

    for u in range(n):
        m_new = jnp.max(ss[u], axis=0, keepdims=True)
        if carries is not None:
            m_new = jnp.maximum(carries[2 * u], m_new)
            alphas.append(jnp.exp2(carries[2 * u] - m_new))
        ms.append(m_new)
        es.append(jnp.exp2(ss[u] - m_new).astype(BF16))
        if u > 0:
            value_matmul(u - 1)
    value_matmul(n - 1)
    out = []
    for u in range(n):
        out += [ms[u], pvs[u]]
    return out


def _flash_output(acc):
    return acc[:HEAD_DIM] / jnp.maximum(acc[HEAD_DIM:HEAD_DIM + 1], 1e-30)


def _store_transposed(dst_ref, x):
    n_rows, n_cols = x.shape
    for r in range(0, n_rows, LANES):
        dst_ref[:n_cols, r:r + LANES] = x[r:r + LANES, :].T.astype(dst_ref.dtype)
    dst_ref[n_cols:, :] = jnp.ones((dst_ref.shape[0] - n_cols, n_rows), dst_ref.dtype)


def _rank_before(v, n_valid, axis):
    idx = lax.broadcasted_iota(jnp.int32, v.shape, axis)
    rank = jnp.zeros(v.shape, F32)
    for m in range(n_valid):
        vm = v[m:m + 1, :] if axis == 0 else v[:, m:m + 1]
        tie = jnp.where(idx > m, 1.0, 0.0)
        rank = rank + jnp.where(vm > v, 1.0, jnp.where(vm == v, tie, 0.0))
    return rank


def _norm_matmul_body(*refs, n_chunk, with_t):
    if with_t:
        x_ref, g_ref, w_ref, wt_ref, o_ref, ot_ref = refs
    else:
        x_ref, g_ref, w_ref, o_ref = refs
    yb = _rms(x_ref[...], g_ref[...]).astype(BF16)
    n = w_ref.shape[1]
    for c0 in range(0, n, n_chunk):
        c1 = min(n, c0 + n_chunk)
        o_ref[:, c0:c1] = jnp.dot(yb, w_ref[:, c0:c1], preferred_element_type=F32)
    if o_ref.shape[1] > n:
        o_ref[:, n:] = jnp.zeros((o_ref.shape[0], o_ref.shape[1] - n), F32)
    if with_t:
        ot_ref[...] = lax.dot_general(wt_ref[...], yb, _NT, preferred_element_type=F32)


def _norm_matmul(x2d, gain, w, wt=None, tm=512):
    t, d = x2d.shape
    n = -(-w.shape[1] // LANES) * LANES
    with_t = wt is not None
    in_specs = [pl.BlockSpec((tm, d), lambda i: (i, 0)),
                pl.BlockSpec((1, d), lambda i: (0, 0)),
                pl.BlockSpec(w.shape, lambda i: (0, 0))]
    out_specs = [pl.BlockSpec((tm, n), lambda i: (i, 0))]
    out_shape = [jax.ShapeDtypeStruct((t, n), F32)]
    args = [x2d, gain.reshape(1, d), w]
    if with_t:
        r = wt.shape[0]
        in_specs.append(pl.BlockSpec((r, d), lambda i: (0, 0)))
        out_specs.append(pl.BlockSpec((r, tm), lambda i: (0, i)))
        out_shape.append(jax.ShapeDtypeStruct((r, t), F32))
        args.append(wt)
    outs = pl.pallas_call(
        functools.partial(_norm_matmul_body, n_chunk=512, with_t=with_t),
        grid=(t // tm,), in_specs=in_specs, out_specs=out_specs, out_shape=out_shape,
        compiler_params=_cparams(("parallel",)), name="norm_in_proj")(*args)
    return outs if with_t else outs[0]


def _chunk_cumsum(x, axis):
    idx = lax.broadcasted_iota(jnp.int32, x.shape, axis) % MLSTM_CHUNK
    d = 1
    while d < MLSTM_CHUNK:
        x = x + jnp.where(idx >= d, pltpu.roll(x, d, axis=axis), 0.0)
        d *= 2
    return x


MLSTM_HEADS_PER_STEP = 2


def _mlstm_body(q_ref, k_ref, v_ref, og_ref, if_ref, gt_ref, cwq_ref, cwk_ref, bcol_ref, brow_ref,
                gain_ref, out_ref, ks, qts, vts, hts, css, brs, lirs, gcol):
    hp = pl.program_id(1)
    s_len = q_ref.shape[1]
    n_local = q_ref.shape[2] // HEAD_DIM
    n_chunks = s_len // MLSTM_CHUNK
    L = MLSTM_CHUNK
    per_slab = LANES // L
    row = lax.broadcasted_iota(jnp.int32, (s_len, HEAD_DIM), 0)
    cols = [slice(a * HEAD_DIM, (a + 1) * HEAD_DIM) for a in range(n_local)]

    def conv_silu(x, w):
        acc = x * w[MLSTM_CONV - 1:MLSTM_CONV, :]
        for d in range(1, MLSTM_CONV):
            shifted = jnp.where(row >= d, pltpu.roll(x, d, axis=0), 0.0)
            acc = acc + shifted * w[MLSTM_CONV - 1 - d:MLSTM_CONV - d, :]
        return acc * _sigmoid(acc)

    def store_chunks_transposed(dst, x):
        for p in range(s_len // LANES):
            slab_t = x[p * LANES:(p + 1) * LANES, :].T
            for j in range(per_slab):
                dst[per_slab * p + j] = slab_t[:, j * L:(j + 1) * L].astype(dst.dtype)

    lane = lax.broadcasted_iota(jnp.int32, gcol.shape, 1)

    @pl.when(hp == 0)
    def _():
        pre = if_ref[0] + bcol_ref[...]
        pre = MLSTM_GATE_CAP * jnp.tanh(pre / MLSTM_GATE_CAP)
        gcol[...] = jnp.where(lane < MLSTM_HEADS, pre, _chunk_cumsum(_log_sigmoid(pre), 0))

    pr = gt_ref[...] + brow_ref[...]
    pr = MLSTM_GATE_CAP * jnp.tanh(pr / MLSTM_GATE_CAP)
    sub = lax.broadcasted_iota(jnp.int32, pr.shape, 0)
    b_rows = _chunk_cumsum(_log_sigmoid(pr), 1)
    g_all = gcol[...]

    gains = []
    for a in range(n_local):
        h = hp * n_local + a
        store_chunks_transposed(qts.at[a], conv_silu(q_ref[0, :, cols[a]], cwq_ref[:, cols[a]]) * (HEAD_DIM ** -0.5))
        store_chunks_transposed(vts.at[a], v_ref[0, :, cols[a]])
        ks[:, cols[a]] = conv_silu(k_ref[0, :, cols[a]], cwk_ref[:, cols[a]]).astype(BF16)
        css[a] = jnp.sum(jnp.where(lane == h, g_all, jnp.where(lane == h + MLSTM_HEADS, -g_all, 0.0)),
                         axis=1, keepdims=True)
        li_row = jnp.sum(jnp.where(sub == h, pr, 0.0), axis=0, keepdims=True)
        b_row = jnp.sum(jnp.where(sub == h + MLSTM_HEADS, b_rows, 0.0), axis=0, keepdims=True)
        for c in range(n_chunks):
            brs[a, c] = b_row[:, c * L:(c + 1) * L]
            lirs[a, c] = li_row[:, c * L:(c + 1) * L]
        gains.append(jnp.broadcast_to(gain_ref[a * HEAD_DIM:(a + 1) * HEAD_DIM, :], (HEAD_DIM, L)))

    tri = lax.broadcasted_iota(jnp.int32, (L, L), 0) <= lax.broadcasted_iota(jnp.int32, (L, L), 1)

    group_size = 16
    assert n_chunks % group_size == 0

    def group(gi, carry):
        states = [list(carry[3 * a:3 * a + 3]) for a in range(n_local)]
        cs = [gi * group_size + j for j in range(group_size)]
        r0s = [pl.multiple_of(c * L, L) for c in cs]
        units = [(a, j) for a in range(n_local) for j in range(group_size)]
        ks_ = {(a, j): ks[pl.ds(r0s[j], L), cols[a]] for a, j in units}
        q_ts = {(a, j): qts[a, cs[j]] for a, j in units}
        v_ts = {(a, j): vts[a, cs[j]] for a, j in units}
        b_rs = {(a, j): brs[a, cs[j]] for a, j in units}
        kvs, ksums, kqs, g_maxs, b_lasts = {}, {}, {}, {}, {}
        for u in units:
            a, j = u
            b_last = b_rs[u][:, L - 1:L]
            g = b_last - b_rs[u] + lirs[a, cs[j]]
            g_max = jnp.max(g, axis=1, keepdims=True)
            w = jnp.exp(g - g_max)
            kvs[u] = jnp.dot((v_ts[u] * w).astype(BF16), ks_[u], preferred_element_type=F32)
            ksums[u] = jnp.dot(jnp.broadcast_to(w, (BF16_SUBLANES, L)).astype(BF16), ks_[u],
                               preferred_element_type=F32)[:1]
            kqs[u] = jnp.dot(ks_[u], q_ts[u], preferred_element_type=F32)
            g_maxs[u] = g_max
            b_lasts[u] = b_last
        c_ins, n_ins, m_ins = {}, {}, {}
        for u in units:
            a, j = u
            c_state, n_state, m_state = states[a]
            c_ins[u] = c_state.astype(BF16)
            n_ins[u] = jnp.broadcast_to(n_state, (BF16_SUBLANES, HEAD_DIM)).astype(BF16)
            m_ins[u] = m_state
            m_new = jnp.maximum(b_lasts[u] + m_state, g_maxs[u])
            sa = jnp.exp(b_lasts[u] + m_state - m_new)
            cc = jnp.exp(g_maxs[u] - m_new)
            states[a] = [sa * c_state + cc * kvs[u], sa * n_state + cc * ksums[u], m_new]
        c_qs = {u: jnp.dot(c_ins[u], q_ts[u], preferred_element_type=F32) for u in units}
        n_qs = {u: jnp.dot(n_ins[u], q_ts[u], preferred_element_type=F32)[:1] for u in units}
        ss, m_ts, w_inters = {}, {}, {}
        for u in units:
            a, j = u
            dmat = jnp.where(tri, b_rs[u] + css[a, pl.ds(r0s[j], L), :], NEG_INF)
            inter = b_rs[u] + m_ins[u]
            m_t = jnp.maximum(inter, jnp.max(dmat, axis=0, keepdims=True))
            ss[u] = kqs[u] * jnp.exp(dmat - m_t)
            m_ts[u] = m_t
            w_inters[u] = jnp.exp(inter - m_t)
        svs = {u: jnp.dot(v_ts[u].astype(BF16), ss[u].astype(BF16), preferred_element_type=F32) for u in units}
        for u in units:
            a, j = u
            num = svs[u] + w_inters[u] * c_qs[u]
            den = jnp.sum(ss[u], axis=0, keepdims=True) + w_inters[u] * n_qs[u]
            ht = num / jnp.maximum(jnp.abs(den), jnp.exp(-m_ts[u]))
            hts[a, cs[j]] = ht * lax.rsqrt(jnp.mean(ht * ht, axis=0, keepdims=True) + NORM_EPS) * gains[a]
        return tuple(x for st in states for x in st)

    init = (jnp.zeros((HEAD_DIM, HEAD_DIM), F32), jnp.zeros((1, HEAD_DIM), F32), jnp.zeros((1, 1), F32)) * n_local
    lax.fori_loop(0, n_chunks // group_size, group, init)

    for a in range(n_local):
        for p in range(s_len // LANES):
            slab_t = jnp.concatenate([hts[a, per_slab * p + j] for j in range(per_slab)], axis=1)
            rows = slice(p * LANES, (p + 1) * LANES)
            out_ref[0, rows, cols[a]] = (slab_t.T * _sigmoid(og_ref[0, rows, cols[a]])).astype(out_ref.dtype)


def _mlstm(z3, gt, conv_w, gate_b, head_gain):
    b, s_len, _ = z3.shape
    n_chunks = s_len // MLSTM_CHUNK
    gr = gt.shape[0]
    bcol = jnp.pad(gate_b, (0, LANES - 2 * MLSTM_HEADS)).reshape(1, LANES)
    brow = jnp.pad(gate_b, (0, gr - 2 * MLSTM_HEADS)).reshape(gr, 1)
    n_local = MLSTM_HEADS_PER_STEP
    n_steps = MLSTM_HEADS // n_local
    width = n_local * HEAD_DIM

    def col(off):
        return pl.BlockSpec((1, s_len, width), lambda bi, hi: (bi, 0, off * n_steps + hi))

    in_specs = [col(0), col(1), col(2), col(3),
                pl.BlockSpec((1, s_len, LANES), lambda bi, hi: (bi, 0, EVEN_IF_BLK)),
                pl.BlockSpec((gr, s_len), lambda bi, hi: (0, bi)),
                pl.BlockSpec((MLSTM_CONV, width), lambda bi, hi: (0, hi)),
                pl.BlockSpec((MLSTM_CONV, width), lambda bi, hi: (0, n_steps + hi)),
                pl.BlockSpec((1, LANES), lambda bi, hi: (0, 0)),
                pl.BlockSpec((gr, 1), lambda bi, hi: (0, 0)),
                pl.BlockSpec((width, 1), lambda bi, hi: (hi, 0))]
    chunk_t = (n_local, n_chunks, HEAD_DIM, MLSTM_CHUNK)
    rows_t = (n_local, n_chunks, 1, MLSTM_CHUNK)
    scratch = [pltpu.VMEM((s_len, width), BF16),
               pltpu.VMEM(chunk_t, BF16), pltpu.VMEM(chunk_t, F32), pltpu.VMEM(chunk_t, F32),
               pltpu.VMEM((n_local, s_len, 1), F32),
               pltpu.VMEM(rows_t, F32), pltpu.VMEM(rows_t, F32),
               pltpu.VMEM((s_len, LANES), F32)]
    return pl.pallas_call(
        _mlstm_body, grid=(b, n_steps), in_specs=in_specs,
        out_specs=pl.BlockSpec((1, s_len, width), lambda bi, hi: (bi, 0, hi)),
        out_shape=jax.ShapeDtypeStruct((b, s_len, MLSTM_W), BF16),
        scratch_shapes=scratch, compiler_params=_cparams(("parallel", "arbitrary")),
        name="mlstm")(z3, z3, z3, z3, z3, gt, conv_w, conv_w, bcol, brow, head_gain.reshape(MLSTM_W, 1))


MOBA_ROWS = 2


def _moba_body(q_ref, k_ref, v_ref, cos_ref, sin_ref, o_ref, kr_scr, vt_scr, km_scr, sel_scr):
    i = pl.program_id(1)
    n_rows, s_len = k_ref.shape[0], k_ref.shape[1]
    bs = MOBA_BLOCK
    nb = s_len // bs
    heads = [slice(h * HEAD_DIM, (h + 1) * HEAD_DIM) for h in range(MOBA_HEADS)]
    units = [(b, h) for b in range(n_rows) for h in range(MOBA_HEADS)]

    @pl.when(i == 0)
    def _():
        for u, (b, h) in enumerate(units):
            _store_transposed(vt_scr.at[u], v_ref[b, :, heads[h]])
            kr = _rotate(k_ref[b, :, heads[h]], cos_ref[...], sin_ref[...])
            kr_scr[b, :, heads[h]] = kr.astype(BF16)
            rows = [jnp.sum(kr[n * bs:(n + 1) * bs, :], axis=0, keepdims=True) / float(bs) for n in range(nb)]
            rows.append(jnp.zeros((km_scr.shape[1] - nb, HEAD_DIM), F32))
            km_scr[u] = jnp.concatenate(rows, axis=0)

    t0 = pl.multiple_of(i * bs, bs)
    cos_q = cos_ref[pl.ds(t0, bs), :]
    sin_q = sin_ref[pl.ds(t0, bs), :]
    blk = lax.broadcasted_iota(jnp.int32, (km_scr.shape[1], bs), 0)
    causal_bias = jnp.where(lax.broadcasted_iota(jnp.int32, (bs, bs), 0)
                            <= lax.broadcasted_iota(jnp.int32, (bs, bs), 1), 0.0, NEG_INF)

    qbs = []
    for u, (b, h) in enumerate(units):
        qr = _rotate(q_ref[b, :, heads[h]], cos_q, sin_q)
        gate_t = lax.dot_general(km_scr[u], qr, _NT, precision=HIGHEST, preferred_element_type=F32)
        val = jnp.where(blk < i, gate_t, NEG_INF)
        picked = jnp.where(val > 0.5 * NEG_INF, _rank_before(val, nb, 0), float(nb)) < MOBA_TOPK
        sel_scr[u] = jnp.where(picked, 0.0, NEG_INF)
        qbs.append((qr * (HEAD_DIM ** -0.5 * LOG2_E)).astype(BF16))

    def key_tiles(k0, n_blocks):
        return ([kr_scr[b, pl.ds(k0, n_blocks * bs), heads[h]] for b, h in units],
                [vt_scr[u, :, pl.ds(k0, n_blocks * bs)] for u in range(len(units))])

    init = tuple(_flash_steps(qbs, *key_tiles(t0, 1), [causal_bias] * len(units), None))

    def past_blocks(j, n_blocks, carry):
        biases = [jnp.concatenate([jnp.broadcast_to(sel_scr[u, pl.ds(j + d, 1), :], (bs, bs))
                                   for d in range(n_blocks)], axis=0) for u in range(len(units))]
        k0 = j * bs if isinstance(j, int) else pl.multiple_of(j * bs, bs)
        return tuple(_flash_steps(qbs, *key_tiles(k0, n_blocks), biases, carry))

    odd = i % 2
    carry = lax.cond(odd == 1, lambda c: past_blocks(0, 1, c), lambda c: c, init)
    fin = lax.fori_loop(0, i // 2, lambda p, c: past_blocks(odd + 2 * p, 2, c), carry)
    for u, (b, h) in enumerate(units):
        o_ref[b, :, heads[h]] = _flash_output(fin[2 * u + 1]).T.astype(o_ref.dtype)


def _moba(z3, cos2, sin2):
    b, s_len, _ = z3.shape
    nb = s_len // MOBA_BLOCK
    q_off = 4 * MLSTM_W // MOBA_W
    rows = MOBA_ROWS if b % MOBA_ROWS == 0 else 1
    n_units = rows * MOBA_HEADS

    def kv(off):
        return pl.BlockSpec((rows, s_len, MOBA_W), lambda bi, i: (bi, 0, off))

    in_specs = [pl.BlockSpec((rows, MOBA_BLOCK, MOBA_W), lambda bi, i: (bi, i, q_off)),
                kv(q_off + 1), kv(q_off + 2),
                pl.BlockSpec((s_len, HEAD_DIM), lambda bi, i: (0, 0)),
                pl.BlockSpec((s_len, HEAD_DIM), lambda bi, i: (0, 0))]
    scratch = [pltpu.VMEM((rows, s_len, MOBA_W), BF16),
               pltpu.VMEM((n_units, VT_ROWS, s_len), BF16),
               pltpu.VMEM((n_units, BF16_SUBLANES, HEAD_DIM), F32),
               pltpu.VMEM((n_units, BF16_SUBLANES, MOBA_BLOCK), F32)]
    return pl.pallas_call(
        _moba_body, grid=(b // rows, nb), in_specs=in_specs,
        out_specs=pl.BlockSpec((rows, MOBA_BLOCK, MOBA_W), lambda bi, i: (bi, i, 0)),
        out_shape=jax.ShapeDtypeStruct((b, s_len, MOBA_W), BF16),
        scratch_shapes=scratch, compiler_params=_cparams(("parallel", "arbitrary")),
        name="moba")(z3, z3, z3, cos2, sin2)


def _gelu_tanh(x):
    return x * (0.5 * (1.0 + jnp.tanh(np.sqrt(2.0 / np.pi) * (x + 0.044715 * (x * x * x)))))


def _nsa_compress_body(kc_ref, vc_ref, pe_ref, wk1_ref, wk2_ref, wv1_ref, wv2_ref, ko_ref, vo_ref):
    n_rows = kc_ref.shape[1] // CMP_STRIDE
    halves = CMP_LEN // CMP_STRIDE
    assert halves == 2

    def compress(x_ref, pe, w1_ref, w2_ref):
        rows = [x_ref[0, pl.ds(l, n_rows, stride=CMP_STRIDE), :] for l in range(CMP_STRIDE)]
        half = CMP_STRIDE * HEAD_DIM
        first = jnp.concatenate([(r + pe[l:l + 1, :]).astype(BF16) for l, r in enumerate(rows)], axis=1)
        second = jnp.concatenate([(r + pe[CMP_STRIDE + l:CMP_STRIDE + l + 1, :]).astype(BF16)
                                  for l, r in enumerate(rows)], axis=1)
        ya = jnp.dot(first, w1_ref[:half, :], preferred_element_type=F32)
        yb = jnp.dot(second, w1_ref[half:, :], preferred_element_type=F32)
        pre = ya + pltpu.roll(yb, n_rows - 1, axis=0)
        return jnp.dot(_gelu_tanh(pre).astype(BF16), w2_ref[...], preferred_element_type=F32)

    ko_ref[0, 0] = compress(kc_ref, pe_ref[0], wk1_ref, wk2_ref)
    vo_ref[0, 0] = compress(vc_ref, pe_ref[1], wv1_ref, wv2_ref)


def _nsa_compress(z3, cmp_pos, wk1, wk2, wv1, wv2):
    b, s_len, _ = z3.shape
    n_rows = s_len // CMP_STRIDE
    kc_off = NSA_W // LANES
    vc_off = kc_off + NSA_GROUPS
    const2 = lambda bi, gi: (0, 0)
    in_specs = [pl.BlockSpec((1, s_len, HEAD_DIM), lambda bi, gi: (bi, 0, kc_off + gi)),
                pl.BlockSpec((1, s_len, HEAD_DIM), lambda bi, gi: (bi, 0, vc_off + gi)),
                pl.BlockSpec(cmp_pos.shape, lambda bi, gi: (0, 0, 0)),
                pl.BlockSpec(wk1.shape, const2), pl.BlockSpec(wk2.shape, const2),
                pl.BlockSpec(wv1.shape, const2), pl.BlockSpec(wv2.shape, const2)]
    out_spec = pl.BlockSpec((1, 1, n_rows, HEAD_DIM), lambda bi, gi: (bi, gi, 0, 0))
    out_sds = jax.ShapeDtypeStruct((b, NSA_GROUPS, n_rows, HEAD_DIM), F32)
    return pl.pallas_call(
        _nsa_compress_body, grid=(b, NSA_GROUPS), in_specs=in_specs,
        out_specs=[out_spec, out_spec], out_shape=[out_sds, out_sds],
        compiler_params=_cparams(("parallel", "parallel")), name="nsa_compress")(
            z3, z3, cmp_pos, wk1, wk2, wv1, wv2)


def _nsa_body(q_ref, ks_ref, vs_ref, kw_ref, vw_ref, kc_ref, vc_ref, gate_ref, cos_ref, sin_ref, ov_ref,
              o_ref, krs, vst, krw, vwt, sel_scr):
    i = pl.program_id(1)
    tq = NSA_TQ
    tk = NSA_TQ
    scale = HEAD_DIM ** -0.5
    n_sel_blk = ks_ref.shape[1] // SEL_BLOCK
    n_cmp = (ks_ref.shape[1] - CMP_LEN) // CMP_STRIDE + 1
    shift = SEL_BLOCK.bit_length() - 1
    assert 1 << shift == SEL_BLOCK and n_sel_blk <= LANES
    blk_per_tile = tk // SEL_BLOCK
    groups = [slice(g * HEAD_DIM, (g + 1) * HEAD_DIM) for g in range(NSA_GROUPS)]

    @pl.when(i == 0)
    def _():
        for g, cols in enumerate(groups):
            krs[:, cols] = _rotate(ks_ref[0, :, cols], cos_ref[...], sin_ref[...]).astype(BF16)
            krw[:, cols] = _rotate(kw_ref[0, :, cols], cos_ref[...], sin_ref[...]).astype(BF16)
            _store_transposed(vst.at[g], vs_ref[0, :, cols])
            _store_transposed(vwt.at[g], vw_ref[0, :, cols])

    t0 = pl.multiple_of(i * tq, tq)
    pos_row = t0 + lax.broadcasted_iota(jnp.int32, (1, tq), 1)
    cos_q = cos_ref[pl.ds(t0, tq), :]
    sin_q = sin_ref[pl.ds(t0, tq), :]

    n_col = lax.broadcasted_iota(jnp.int32, (LANES, 1), 0)
    cmp_end = jnp.where(n_col < n_cmp, n_col * CMP_STRIDE + (CMP_LEN - 1), jnp.iinfo(jnp.int32).max)
    cmp_ok = cmp_end <= pos_row
    blk_i = lax.broadcasted_iota(jnp.int32, (n_sel_blk, tq), 0)
    behind = (pos_row >> shift) - blk_i
    units = [(g, r) for g in range(NSA_GROUPS) for r in range(NSA_REP)]
    qr_heads, o_cmp = [], []
    for g in range(NSA_GROUPS):
        kc = kc_ref[0, g].astype(BF16)
        vc = vc_ref[0, g].astype(BF16)
        p_sum = jnp.zeros((LANES, tq), F32)
        for r in range(NSA_REP):
            head = (g * NSA_REP + r) * HEAD_DIM
            q = q_ref[0, :, head:head + HEAD_DIM] * scale
            qr_heads.append(_rotate(q * LOG2_E, cos_q, sin_q).astype(BF16))
            s = lax.dot_general(kc, q.astype(BF16), _NT, preferred_element_type=F32)
            s = jnp.where(cmp_ok, s, NEG_INF)
            e = jnp.where(cmp_ok, jnp.exp(s - jnp.max(s, axis=0, keepdims=True)), 0.0)
            p = e / jnp.maximum(jnp.sum(e, axis=0, keepdims=True), 1e-30)
            o_cmp.append(lax.dot_general(p.astype(BF16), vc, _TN, preferred_element_type=F32))
            p_sum = p_sum + p
        imp = jnp.dot(ov_ref[...], p_sum, precision=HIGHEST, preferred_element_type=F32)
        val = jnp.where(behind == 0, FORCED_SCORE, jnp.where(behind == 1, FORCED_SCORE, imp))
        val = jnp.where(blk_i == 0, FORCED_SCORE, val)
        val = jnp.where(behind >= 0, val, NEG_INF)
        picked = jnp.where(val > 0.5 * NEG_INF, _rank_before(val, n_sel_blk, 0), float(n_sel_blk)) < SEL_TOPN
        sel_scr[g] = jnp.where(picked, 0.0, NEG_INF)

    def sel_bias(g, k0, n_tiles):
        blk0 = (k0 >> shift) if isinstance(k0, int) else pl.multiple_of(k0 >> shift, blk_per_tile)
        return jnp.concatenate([jnp.broadcast_to(sel_scr[g, pl.ds(blk0 + b, 1), :], (SEL_BLOCK, tq))
                                for b in range(n_tiles * blk_per_tile)], axis=0)

    def sel_tiles(k0, n_tiles, carry):
        biases = [sel_bias(g, k0, n_tiles) for g in range(NSA_GROUPS)]
        return tuple(_flash_steps(qr_heads, [krs[pl.ds(k0, n_tiles * tk), groups[g]] for g, _ in units],
                                  [vst[g, :, pl.ds(k0, n_tiles * tk)] for g, _ in units],
                                  [biases[g] for g, _ in units], carry))

    causal_bias = jnp.where(lax.broadcasted_iota(jnp.int32, (tk, tq), 0)
                            <= lax.broadcasted_iota(jnp.int32, (tk, tq), 1), 0.0, NEG_INF)
    span = WINDOW + tq
    start = pl.multiple_of(jnp.maximum(t0 - WINDOW, 0), tq)
    gap = ((t0 - start) + lax.broadcasted_iota(jnp.int32, (span, tq), 1)
           - lax.broadcasted_iota(jnp.int32, (span, tq), 0))
    win_bias = jnp.where(gap >= 0, jnp.where(gap < WINDOW, 0.0, NEG_INF), NEG_INF)
    diag_biases = [sel_bias(g, t0, 1) + causal_bias for g in range(NSA_GROUPS)]
    n = len(units)
    group = _flash_steps(qr_heads * 2,
                         [krs[pl.ds(t0, tk), groups[g]] for g, _ in units]
                         + [krw[pl.ds(start, span), groups[g]] for g, _ in units],
                         [vst[g, :, pl.ds(t0, tk)] for g, _ in units]
                         + [vwt[g, :, pl.ds(start, span)] for g, _ in units],
                         [diag_biases[g] for g, _ in units] + [win_bias] * n, None)
    win_fin = group[2 * n:]

    odd = i % 2
    carry = lax.cond(odd == 1, lambda c: sel_tiles(0, 1, c), lambda c: c, tuple(group[:2 * n]))
    sel_fin = lax.fori_loop(0, i // 2, lambda p, c: sel_tiles(pl.multiple_of((odd + 2 * p) * tk, tk), 2, c), carry)

    gates = _sigmoid(gate_ref[0])
    gates_t = gates.T
    for u in range(n):
        c0 = NSA_GATES * u
        o_t = (gates_t[c0 + 1:c0 + 2, :] * _flash_output(sel_fin[2 * u + 1])
               + gates_t[c0 + 2:c0 + 3, :] * _flash_output(win_fin[2 * u + 1]))
        o_ref[0, :, u * HEAD_DIM:(u + 1) * HEAD_DIM] = (gates[:, c0:c0 + 1] * o_cmp[u] + o_t.T).astype(o_ref.dtype)


def _nsa_attention(z3, k_cmp, v_cmp, cos2, sin2):
    b, s_len, _ = z3.shape
    assert s_len >= WINDOW + NSA_TQ and s_len % NSA_TQ == 0
    n_cmp = (s_len - CMP_LEN) // CMP_STRIDE + 1
    n_blk = s_len // SEL_BLOCK
    n_rows = k_cmp.shape[2]
    assert n_rows == LANES
    cmp_start = np.arange(n_rows) * CMP_STRIDE
    cmp_end = cmp_start + CMP_LEN - 1
    blk_lo = np.arange(LANES) * SEL_BLOCK
    overlap = ((cmp_start[:, None] <= blk_lo[None, :] + SEL_BLOCK - 1) & (cmp_end[:, None] >= blk_lo[None, :])
               & (np.arange(n_rows)[:, None] < n_cmp)).astype(np.float32)
    overlap_t = np.ascontiguousarray(overlap.T[:n_blk])
    base = NSA_W // NSA_KV_W

    def kv(off):
        return pl.BlockSpec((1, s_len, NSA_KV_W), lambda bi, i: (bi, 0, base + off))

    cmp_spec = pl.BlockSpec((1, NSA_GROUPS, n_rows, HEAD_DIM), lambda bi, i: (bi, 0, 0, 0))
    const2 = lambda bi, i: (0, 0)
    in_specs = [pl.BlockSpec((1, NSA_TQ, NSA_W), lambda bi, i: (bi, i, 0)),
                kv(2), kv(3), kv(4), kv(5), cmp_spec, cmp_spec,
                pl.BlockSpec((1, NSA_TQ, LANES), lambda bi, i: (bi, i, ODD_GATE_BLK)),
                pl.BlockSpec((s_len, HEAD_DIM), const2), pl.BlockSpec((s_len, HEAD_DIM), const2),
                pl.BlockSpec((n_blk, n_rows), const2)]
    vt_shape = (NSA_GROUPS, VT_ROWS, s_len)
    scratch = [pltpu.VMEM((s_len, NSA_KV_W), BF16), pltpu.VMEM(vt_shape, BF16),
               pltpu.VMEM((s_len, NSA_KV_W), BF16), pltpu.VMEM(vt_shape, BF16),
               pltpu.VMEM((NSA_GROUPS, n_blk, NSA_TQ), F32)]
    return pl.pallas_call(
        _nsa_body, grid=(b, s_len // NSA_TQ), in_specs=in_specs,
        out_specs=pl.BlockSpec((1, NSA_TQ, NSA_W), lambda bi, i: (bi, i, 0)),
        out_shape=jax.ShapeDtypeStruct((b, s_len, NSA_W), BF16),
        scratch_shapes=scratch, compiler_params=_cparams(("parallel", "arbitrary")),
        name="nsa_attention")(z3, z3, z3, z3, z3, k_cmp, v_cmp, z3, cos2, sin2, jnp.asarray(overlap_t))


def _split_bf16(x):
    hi = x.astype(BF16)
    return hi, (x - hi.astype(F32)).astype(BF16)


def _proj_moe_body(*refs, n_act, final_norm):
    res_ref = refs[0]
    a_refs = refs[1:1 + n_act]
    wo_refs = refs[1 + n_act:1 + 2 * n_act]
    (g_ref, wr_ref, br_ref, w1_ref, w3_ref, w2_ref, fg_ref, o_ref,
     xn_scr, cw_scr, hd_scr) = refs[1 + 2 * n_act:]
    j = pl.program_id(1)
    tm = res_ref.shape[0]
    lane = lax.broadcasted_iota(jnp.int32, (tm, LANES), 1).astype(F32)
    neg = float("-inf")

    @pl.when(j == 0)
    def _():
        h = res_ref[...]
        for a_ref, wo_ref in zip(a_refs, wo_refs):
            h = h + jnp.dot(a_ref[...], wo_ref[...], preferred_element_type=F32)
        o_ref[...] = h
        xn = _rms(h, g_ref[...])
        xh, xl = _split_bf16(xn)
        xn_scr[...] = xh
        hi_terms = jnp.dot(xh, wr_ref[...], preferred_element_type=F32)
        logits = (hi_terms[:, :LANES] + (jnp.dot(xl, wr_ref[:, :LANES], preferred_element_type=F32)
                                         + hi_terms[:, LANES:])) + br_ref[...]
        lt = logits.T[:ROUTER_SLOTS]
        slot = lax.broadcasted_iota(jnp.int32, lt.shape, 0).astype(F32)
        is_g = slot < MOE_GROUPS
        gl = jnp.where(is_g, lt, neg)
        g_max = jnp.max(gl, axis=0, keepdims=True)
        g_w = 1.0 / jnp.sum(jnp.where(is_g, jnp.exp(gl - g_max), 0.0), axis=0, keepdims=True)
        g_top = jnp.min(jnp.where(gl == g_max, slot, float(LANES)), axis=0, keepdims=True)
        lo = MOE_GROUPS + MOE_EPG * g_top
        el = jnp.where(slot >= lo, jnp.where(slot < lo + MOE_EPG, lt, neg), neg)
        v1 = jnp.max(el, axis=0, keepdims=True)
        i1 = jnp.min(jnp.where(el == v1, slot, float(LANES)), axis=0, keepdims=True)
        el2 = jnp.where(slot == i1, neg, el)
        v2 = jnp.max(el2, axis=0, keepdims=True)
        i2 = jnp.min(jnp.where(el2 == v2, slot, float(LANES)), axis=0, keepdims=True)
        e2 = jnp.exp(v2 - v1)
        den = 1.0 + e2
        cw_t = jnp.where(slot == i1, g_w / den, 0.0) + jnp.where(slot == i2, g_w * e2 / den, 0.0)
        cw_scr[...] = jnp.concatenate([cw_t, jnp.zeros((LANES - ROUTER_SLOTS, tm), F32)], axis=0).T

    xn = xn_scr[...]
    cw = cw_scr[...]
    for r in range(MOE_EPG):
        h1 = jnp.dot(xn, w1_ref[r].astype(BF16), preferred_element_type=F32)
        h3 = jnp.dot(xn, w3_ref[r], preferred_element_type=F32)
        e_lane = (MOE_GROUPS + MOE_EPG * j + r).astype(F32)
        col = jnp.sum(jnp.where(lane == e_lane, cw, 0.0), axis=1, keepdims=True)
        hd_scr[:, r * MOE_HIDDEN:(r + 1) * MOE_HIDDEN] = ((h1 * _sigmoid(h1)) * h3 * col).astype(BF16)
    o_ref[...] += jnp.dot(hd_scr[...], w2_ref[...].astype(BF16), preferred_element_type=F32)

    if final_norm:
        @pl.when(j == pl.num_programs(1) - 1)
        def _():
            o_ref[...] = _rms(o_ref[...], fg_ref[...])


def _proj_moe(res2d, acts, w_out, gain, w_g, b_g, w_e, b_e, w1, w3, w2, final_gain, final_norm, tm=1024):
    t, d = res2d.shape
    n_act = len(acts)
    n_slots = MOE_GROUPS + MOE_EXPERTS
    wr = jnp.pad(jnp.concatenate([w_g, w_e], axis=1), ((0, 0), (0, LANES - n_slots)))
    wr_cat = jnp.concatenate(_split_bf16(wr), axis=1)
    br = jnp.pad(jnp.concatenate([b_g, b_e]), (0, LANES - n_slots)).reshape(1, LANES)
    tile = lambda i, j: (i, 0)
    const = lambda i, j: (0, 0)
    group_hidden = MOE_EPG * MOE_HIDDEN
    in_specs = [pl.BlockSpec((tm, d), tile)]
    in_specs += [pl.BlockSpec((tm, a.shape[1]), tile) for a in acts]
    row0 = np.cumsum([0] + [a.shape[1] for a in acts])
    assert all(r % a.shape[1] == 0 for r, a in zip(row0, acts)) and row0[-1] == w_out.shape[0]
    in_specs += [pl.BlockSpec((a.shape[1], d), functools.partial(lambda i, j, blk: (blk, 0), blk=int(r // a.shape[1])))
                 for r, a in zip(row0, acts)]
    in_specs += [pl.BlockSpec((1, d), const),
                 pl.BlockSpec((d, 2 * LANES), const),
                 pl.BlockSpec((1, LANES), const),
                 pl.BlockSpec((MOE_EPG, d, MOE_HIDDEN), lambda i, j: (j, 0, 0)),
                 pl.BlockSpec((MOE_EPG, d, MOE_HIDDEN), lambda i, j: (j, 0, 0)),
                 pl.BlockSpec((group_hidden, d), lambda i, j: (j, 0)),
                 pl.BlockSpec((1, d), const)]
    scratch = [pltpu.VMEM((tm, d), BF16), pltpu.VMEM((tm, LANES), F32), pltpu.VMEM((tm, group_hidden), BF16)]
    return pl.pallas_call(
        functools.partial(_proj_moe_body, n_act=n_act, final_norm=final_norm),
        grid=(t // tm, MOE_GROUPS), in_specs=in_specs,
        out_specs=pl.BlockSpec((tm, d), tile),
        out_shape=jax.ShapeDtypeStruct((t, d), F32),
        scratch_shapes=scratch, compiler_params=_cparams(("parallel", "arbitrary")),
        name="out_proj_moe")(res2d, *acts, *([w_out] * n_act), gain.reshape(1, d), wr_cat, br,
                             w1, w3.astype(BF16), w2.reshape(MOE_EXPERTS * MOE_HIDDEN, d),
                             final_gain.reshape(1, d))


def _rope_tables(s_len):
    half = HEAD_DIM // 2
    inv = ROPE_THETA ** (-(jnp.arange(half, dtype=F32) / half))
    ang = jnp.arange(s_len, dtype=F32)[:, None] * inv[None, :]
    cos, sin = jnp.cos(ang), jnp.sin(ang)
    return jnp.concatenate([cos, cos], axis=-1), jnp.concatenate([-sin, sin], axis=-1)


def _even_weights(w_in):
    n_if = 2 * MLSTM_HEADS
    a = 4 * MLSTM_W
    w_in = w_in.astype(BF16)
    w_main = jnp.concatenate([w_in[:, :a], w_in[:, a + n_if:],
                              w_in[:, a:a + n_if], jnp.zeros((w_in.shape[0], LANES - n_if), w_in.dtype)], axis=1)
    w_if_t = jnp.concatenate([w_in[:, a:a + n_if].T,
                              jnp.zeros((BF16_SUBLANES - n_if, w_in.shape[0]), w_in.dtype)], axis=0)
    return w_main, w_if_t


def _odd_weights(w_in):
    a = NSA_W + 6 * NSA_KV_W
    n_gates = NSA_HEADS * NSA_GATES
    assert w_in.shape[1] == a + n_gates and n_gates <= LANES
    return w_in.astype(BF16)


def kernel(x, mix_norm_0, w_in_0, mlstm_conv_0, mlstm_gate_b_0, mlstm_head_norm_0, w_out_0, ffn_norm_0, router_group_0, router_group_b_0, router_expert_0, router_expert_b_0, moe_w1_0, moe_w3_0, moe_w2_0, mix_norm_1, w_in_1, nsa_cmp_pos_1, nsa_cmp_k1_1, nsa_cmp_k2_1, nsa_cmp_v1_1, nsa_cmp_v2_1, w_out_1, ffn_norm_1, router_group_1, router_group_b_1, router_expert_1, router_expert_b_1, moe_w1_1, moe_w3_1, moe_w2_1, final_norm):
    b, s_len, d = x.shape
    t = b * s_len
    cos2, sin2 = _rope_tables(s_len)
    x2d = x.reshape(t, d)

    w_main, w_if_t = _even_weights(w_in_0)
    z0, gt = _norm_matmul(x2d, mix_norm_0, w_main, w_if_t)
    z0 = z0.reshape(b, s_len, EVEN_N)
    h_m = _mlstm(z0, gt, mlstm_conv_0, mlstm_gate_b_0, mlstm_head_norm_0)
    o_b = _moba(z0, cos2, sin2)
    h = _proj_moe(x2d, [h_m.reshape(t, MLSTM_W), o_b.reshape(t, MOBA_W)], w_out_0.astype(BF16),
                  ffn_norm_0, router_group_0, router_group_b_0, router_expert_0, router_expert_b_0,
                  moe_w1_0, moe_w3_0, moe_w2_0, final_norm, False)

    z1 = _norm_matmul(h, mix_norm_1, _odd_weights(w_in_1)).reshape(b, s_len, ODD_N)
    k_cmp, v_cmp = _nsa_compress(z1, nsa_cmp_pos_1, nsa_cmp_k1_1.astype(BF16), nsa_cmp_k2_1.astype(BF16),
                                 nsa_cmp_v1_1.astype(BF16), nsa_cmp_v2_1.astype(BF16))
    o = _nsa_attention(z1, k_cmp, v_cmp, cos2, sin2)
    h = _proj_moe(h, [o.reshape(t, NSA_W)], w_out_1.astype(BF16),
                  ffn_norm_1, router_group_1, router_group_b_1, router_expert_1, router_expert_b_1,
                  moe_w1_1, moe_w3_1, moe_w2_1, final_norm, True)
    return h.reshape(b, s_len, d)
```

```python
import functools

import numpy as np
import jax
import jax.numpy as jnp
from jax import lax
from jax.experimental import pallas as pl
from jax.experimental.pallas import tpu as pltpu

F32 = jnp.float32
BF16 = jnp.bfloat16
HIGHEST = lax.Precision.HIGHEST

LANES = 128
BF16_SUBLANES = 16
D_MODEL = 1024
HEAD_DIM = 128
ROPE_THETA = 10000.0
NORM_EPS = 1e-6
NEG_INF = -1e30
FORCED_SCORE = 1e4

MLSTM_HEADS = 4
MLSTM_W = MLSTM_HEADS * HEAD_DIM
MLSTM_CHUNK = 64
MLSTM_CONV = 4
MLSTM_GATE_CAP = 15.0
MOBA_HEADS = 4
MOBA_W = MOBA_HEADS * HEAD_DIM
MOBA_BLOCK = 256
MOBA_TOPK = 3

NSA_HEADS = 8
NSA_GROUPS = 2
NSA_REP = NSA_HEADS // NSA_GROUPS
NSA_W = NSA_HEADS * HEAD_DIM
NSA_KV_W = NSA_GROUPS * HEAD_DIM
CMP_LEN = 32
CMP_STRIDE = 16
CMP_HIDDEN = 256
SEL_BLOCK = 64
SEL_TOPN = 8
WINDOW = 512
NSA_TQ = 256

MOE_GROUPS = 4
MOE_EPG = 4
MOE_EXPERTS = MOE_GROUPS * MOE_EPG
MOE_HIDDEN = D_MODEL // 4
ROUTER_SLOTS = 32

EVEN_N = 4 * MLSTM_W + 3 * MOBA_W + LANES
EVEN_IF_BLK = (4 * MLSTM_W + 3 * MOBA_W) // LANES
ODD_GATE_BLK = (NSA_W + 6 * NSA_KV_W) // LANES
ODD_N = NSA_W + 6 * NSA_KV_W + LANES
NSA_GATES = 3

VMEM_LIMIT = 56 * 1024 * 1024
VT_ROWS = HEAD_DIM + BF16_SUBLANES
LOG2_E = float(np.log2(np.e))

_NT = (((1,), (1,)), ((), ()))
_TN = (((0,), (0,)), ((), ()))


def _cparams(sem):
    return pltpu.CompilerParams(dimension_semantics=sem, vmem_limit_bytes=VMEM_LIMIT)


def _rms(x, g):
    return x * lax.rsqrt(jnp.mean(x * x, axis=-1, keepdims=True) + NORM_EPS) * g


def _sigmoid(x):
    return 0.5 * jnp.tanh(0.5 * x) + 0.5


def _log_sigmoid(x):
    return -(jnp.maximum(-x, 0.0) + jnp.log1p(jnp.exp(-jnp.abs(x))))


def _rotate(x, cos2, sin2):
    return x * cos2 + pltpu.roll(x, HEAD_DIM // 2, axis=1) * sin2


def _flash_steps(qs, k_ts, vt_ts, biases, carries):
    n = len(qs)
    ss = [lax.dot_general(k_ts[u], qs[u], _NT, preferred_element_type=F32) + biases[u] for u in range(n)]
    ms, es, alphas, pvs = [], [], [], []

    def value_matmul(u):
        pv = jnp.dot(vt_ts[u], es[u], preferred_element_type=F32)
        if carries is not None:
            pv = alphas[u] * carries[2 * u + 1] + pv
        pvs.append(pv)

    for u in range(n):
        m_new = jnp.max(ss[u], axis=0, keepdims=True)
        if carries is not None:
            m_new = jnp.maximum(carries[2 * u], m_new)
            alphas.append(jnp.exp2(carries[2 * u] - m_new))
        ms.append(m_new)
        es.append(jnp.exp2(ss[u] - m_new).astype(BF16))
        if u > 0:
            value_matmul(u - 1)
    value_matmul(n - 1)
    out = []
    for u in range(n):
        out += [ms[u], pvs[u]]
    return out


def _flash_output(acc):
    return acc[:HEAD_DIM] / jnp.maximum(acc[HEAD_DIM:HEAD_DIM + 1], 1e-30)


def _store_transposed(dst_ref, x):
    n_rows, n_cols = x.shape
    for r in range(0, n_rows, LANES):
        dst_ref[:n_cols, r:r + LANES] = x[r:r + LANES, :].T.astype(dst_ref.dtype)
    dst_ref[n_cols:, :] = jnp.ones((dst_ref.shape[0] - n_cols, n_rows), dst_ref.dtype)


def _rank_before(v, n_valid, axis):
    idx = lax.broadcasted_iota(jnp.int32, v.shape, axis)
    rank = jnp.zeros(v.shape, F32)
    for m in range(n_valid):
        vm = v[m:m + 1, :] if axis == 0 else v[:, m:m + 1]
        tie = jnp.where(idx > m, 1.0, 0.0)
        rank = rank + jnp.where(vm > v, 1.0, jnp.where(vm == v, tie, 0.0))
    return rank


def _norm_matmul_body(*refs, n_chunk, with_t, pieces):
    if with_t:
        x_ref, g_ref, w_ref, wt_ref, o_ref, ot_ref = refs
    else:
        x_ref, g_ref, w_ref, o_ref = refs
    yb = _rms(x_ref[...], g_ref[...]).astype(BF16)
    n_done = 0
    for src0, src1, dst0 in pieces:
        assert dst0 == n_done
        for c0 in range(src0, src1, n_chunk):
            c1 = min(src1, c0 + n_chunk)
            o_ref[:, n_done:n_done + c1 - c0] = jnp.dot(yb, w_ref[:, c0:c1], preferred_element_type=F32)
            n_done += c1 - c0
    if o_ref.shape[1] > n_done:
        o_ref[:, n_done:] = jnp.zeros((o_ref.shape[0], o_ref.shape[1] - n_done), F32)
    if with_t:
        ot_ref[...] = lax.dot_general(wt_ref[...], yb, _NT, preferred_element_type=F32)


def _norm_matmul(x2d, gain, w, wt=None, pieces=None, tm=512):
    t, d = x2d.shape
    n = -(-w.shape[1] // LANES) * LANES
    pieces = pieces or ((0, w.shape[1], 0),)
    with_t = wt is not None
    in_specs = [pl.BlockSpec((tm, d), lambda i: (i, 0)),
                pl.BlockSpec((1, d), lambda i: (0, 0)),
                pl.BlockSpec(w.shape, lambda i: (0, 0))]
    out_specs = [pl.BlockSpec((tm, n), lambda i: (i, 0))]
    out_shape = [jax.ShapeDtypeStruct((t, n), F32)]
    args = [x2d, gain.reshape(1, d), w]
    if with_t:
        r = wt.shape[0]
        in_specs.append(pl.BlockSpec((r, d), lambda i: (0, 0)))
        out_specs.append(pl.BlockSpec((r, tm), lambda i: (0, i)))
        out_shape.append(jax.ShapeDtypeStruct((r, t), F32))
        args.append(wt)
    outs = pl.pallas_call(
        functools.partial(_norm_matmul_body, n_chunk=512, with_t=with_t, pieces=pieces),
        grid=(t // tm,), in_specs=in_specs, out_specs=out_specs, out_shape=out_shape,
        compiler_params=_cparams(("parallel",)), name="norm_in_proj")(*args)
    return outs if with_t else outs[0]


def _chunk_cumsum(x, axis):
    idx = lax.broadcasted_iota(jnp.int32, x.shape, axis) % MLSTM_CHUNK
    d = 1
    while d < MLSTM_CHUNK:
        x = x + jnp.where(idx >= d, pltpu.roll(x, d, axis=axis), 0.0)
        d *= 2
    return x


MLSTM_HEADS_PER_STEP = 2


def _mlstm_body(q_ref, k_ref, v_ref, og_ref, if_ref, gt_ref, cwq_ref, cwk_ref, bcol_ref, brow_ref,
                gain_ref, out_ref, ks, qts, vts, hts, css, brs, lirs, gcol):
    hp = pl.program_id(1)
    s_len = q_ref.shape[1]
    n_local = q_ref.shape[2] // HEAD_DIM
    n_chunks = s_len // MLSTM_CHUNK
    L = MLSTM_CHUNK
    per_slab = LANES // L
    row = lax.broadcasted_iota(jnp.int32, (s_len, HEAD_DIM), 0)
    cols = [slice(a * HEAD_DIM, (a + 1) * HEAD_DIM) for a in range(n_local)]

    def conv_silu(x, w):
        acc = x * w[MLSTM_CONV - 1:MLSTM_CONV, :]
        for d in range(1, MLSTM_CONV):
            shifted = jnp.where(row >= d, pltpu.roll(x, d, axis=0), 0.0)
            acc = acc + shifted * w[MLSTM_CONV - 1 - d:MLSTM_CONV - d, :]
        return acc * _sigmoid(acc)

    def store_chunks_transposed(dst, x):
        for p in range(s_len // LANES):
            slab_t = x[p * LANES:(p + 1) * LANES, :].T
            for j in range(per_slab):
                dst[per_slab * p + j] = slab_t[:, j * L:(j + 1) * L].astype(dst.dtype)

    lane = lax.broadcasted_iota(jnp.int32, gcol.shape, 1)

    @pl.when(hp == 0)
    def _():
        pre = if_ref[0] + bcol_ref[...]
        pre = MLSTM_GATE_CAP * jnp.tanh(pre / MLSTM_GATE_CAP)
        gcol[...] = jnp.where(lane < MLSTM_HEADS, pre, _chunk_cumsum(_log_sigmoid(pre), 0))

    pr = gt_ref[...] + brow_ref[...]
    pr = MLSTM_GATE_CAP * jnp.tanh(pr / MLSTM_GATE_CAP)
    sub = lax.broadcasted_iota(jnp.int32, pr.shape, 0)
    b_rows = _chunk_cumsum(_log_sigmoid(pr), 1)
    g_all = gcol[...]

    gains = []
    for a in range(n_local):
        h = hp * n_local + a
        store_chunks_transposed(qts.at[a], conv_silu(q_ref[0, :, cols[a]], cwq_ref[:, cols[a]]) * (HEAD_DIM ** -0.5))
        store_chunks_transposed(vts.at[a], v_ref[0, :, cols[a]])
        ks[:, cols[a]] = conv_silu(k_ref[0, :, cols[a]], cwk_ref[:, cols[a]]).astype(BF16)
        css[a] = jnp.sum(jnp.where(lane == h, g_all, jnp.where(lane == h + MLSTM_HEADS, -g_all, 0.0)),
                         axis=1, keepdims=True)
        li_row = jnp.sum(jnp.where(sub == h, pr, 0.0), axis=0, keepdims=True)
        b_row = jnp.sum(jnp.where(sub == h + MLSTM_HEADS, b_rows, 0.0), axis=0, keepdims=True)
        for c in range(n_chunks):
            brs[a, c] = b_row[:, c * L:(c + 1) * L]
            lirs[a, c] = li_row[:, c * L:(c + 1) * L]
        gains.append(jnp.broadcast_to(gain_ref[a * HEAD_DIM:(a + 1) * HEAD_DIM, :], (HEAD_DIM, L)))

    tri = lax.broadcasted_iota(jnp.int32, (L, L), 0) <= lax.broadcasted_iota(jnp.int32, (L, L), 1)

    group_size = 16
    assert n_chunks % group_size == 0

    def group(gi, carry):
        states = [list(carry[3 * a:3 * a + 3]) for a in range(n_local)]
        cs = [gi * group_size + j for j in range(group_size)]
        r0s = [pl.multiple_of(c * L, L) for c in cs]
        units = [(a, j) for a in range(n_local) for j in range(group_size)]
        ks_ = {(a, j): ks[pl.ds(r0s[j], L), cols[a]] for a, j in units}
        q_ts = {(a, j): qts[a, cs[j]] for a, j in units}
        v_ts = {(a, j): vts[a, cs[j]] for a, j in units}
        b_rs = {(a, j): brs[a, cs[j]] for a, j in units}
        kvs, ksums, kqs, g_maxs, b_lasts = {}, {}, {}, {}, {}
        for u in units:
            a, j = u
            b_last = b_rs[u][:, L - 1:L]
            g = b_last - b_rs[u] + lirs[a, cs[j]]
            g_max = jnp.max(g, axis=1, keepdims=True)
            w = jnp.exp(g - g_max)
            kvs[u] = jnp.dot((v_ts[u] * w).astype(BF16), ks_[u], preferred_element_type=F32)
            ksums[u] = jnp.dot(jnp.broadcast_to(w, (BF16_SUBLANES, L)).astype(BF16), ks_[u],
                               preferred_element_type=F32)[:1]
            kqs[u] = jnp.dot(ks_[u], q_ts[u], preferred_element_type=F32)
            g_maxs[u] = g_max
            b_lasts[u] = b_last
        c_ins, n_ins, m_ins = {}, {}, {}
        for u in units:
            a, j = u
            c_state, n_state, m_state = states[a]
            c_ins[u] = c_state.astype(BF16)
            n_ins[u] = jnp.broadcast_to(n_state, (BF16_SUBLANES, HEAD_DIM)).astype(BF16)
            m_ins[u] = m_state
            m_new = jnp.maximum(b_lasts[u] + m_state, g_maxs[u])
            sa = jnp.exp(b_lasts[u] + m_state - m_new)
            cc = jnp.exp(g_maxs[u] - m_new)
            states[a] = [sa * c_state + cc * kvs[u], sa * n_state + cc * ksums[u], m_new]
        c_qs = {u: jnp.dot(c_ins[u], q_ts[u], preferred_element_type=F32) for u in units}
        n_qs = {u: jnp.dot(n_ins[u], q_ts[u], preferred_element_type=F32)[:1] for u in units}
        ss, m_ts, w_inters = {}, {}, {}
        for u in units:
            a, j = u
            dmat = jnp.where(tri, b_rs[u] + css[a, pl.ds(r0s[j], L), :], NEG_INF)
            inter = b_rs[u] + m_ins[u]
            m_t = jnp.maximum(inter, jnp.max(dmat, axis=0, keepdims=True))
            ss[u] = kqs[u] * jnp.exp(dmat - m_t)
            m_ts[u] = m_t
            w_inters[u] = jnp.exp(inter - m_t)
        svs = {u: jnp.dot(v_ts[u].astype(BF16), ss[u].astype(BF16), preferred_element_type=F32) for u in units}
        for u in units:
            a, j = u
            num = svs[u] + w_inters[u] * c_qs[u]
            den = jnp.sum(ss[u], axis=0, keepdims=True) + w_inters[u] * n_qs[u]
            ht = num / jnp.maximum(jnp.abs(den), jnp.exp(-m_ts[u]))
            hts[a, cs[j]] = ht * lax.rsqrt(jnp.mean(ht * ht, axis=0, keepdims=True) + NORM_EPS) * gains[a]
        return tuple(x for st in states for x in st)

    init = (jnp.zeros((HEAD_DIM, HEAD_DIM), F32), jnp.zeros((1, HEAD_DIM), F32), jnp.zeros((1, 1), F32)) * n_local
    lax.fori_loop(0, n_chunks // group_size, group, init)

    for a in range(n_local):
        for p in range(s_len // LANES):
            slab_t = jnp.concatenate([hts[a, per_slab * p + j] for j in range(per_slab)], axis=1)
            rows = slice(p * LANES, (p + 1) * LANES)
            out_ref[0, rows, cols[a]] = (slab_t.T * _sigmoid(og_ref[0, rows, cols[a]])).astype(out_ref.dtype)


def _mlstm(z3, gt, conv_w, gate_b, head_gain):
    b, s_len, _ = z3.shape
    n_chunks = s_len // MLSTM_CHUNK
    gr = gt.shape[0]
    bcol = jnp.pad(gate_b, (0, LANES - 2 * MLSTM_HEADS)).reshape(1, LANES)
    brow = jnp.pad(gate_b, (0, gr - 2 * MLSTM_HEADS)).reshape(gr, 1)
    n_local = MLSTM_HEADS_PER_STEP
    n_steps = MLSTM_HEADS // n_local
    width = n_local * HEAD_DIM

    def col(off):
        return pl.BlockSpec((1, s_len, width), lambda bi, hi: (bi, 0, off * n_steps + hi))

    in_specs = [col(0), col(1), col(2), col(3),
                pl.BlockSpec((1, s_len, LANES), lambda bi, hi: (bi, 0, EVEN_IF_BLK)),
                pl.BlockSpec((gr, s_len), lambda bi, hi: (0, bi)),
                pl.BlockSpec((MLSTM_CONV, width), lambda bi, hi: (0, hi)),
                pl.BlockSpec((MLSTM_CONV, width), lambda bi, hi: (0, n_steps + hi)),
                pl.BlockSpec((1, LANES), lambda bi, hi: (0, 0)),
                pl.BlockSpec((gr, 1), lambda bi, hi: (0, 0)),
                pl.BlockSpec((width, 1), lambda bi, hi: (hi, 0))]
    chunk_t = (n_local, n_chunks, HEAD_DIM, MLSTM_CHUNK)
    rows_t = (n_local, n_chunks, 1, MLSTM_CHUNK)
    scratch = [pltpu.VMEM((s_len, width), BF16),
               pltpu.VMEM(chunk_t, BF16), pltpu.VMEM(chunk_t, F32), pltpu.VMEM(chunk_t, F32),
               pltpu.VMEM((n_local, s_len, 1), F32),
               pltpu.VMEM(rows_t, F32), pltpu.VMEM(rows_t, F32),
               pltpu.VMEM((s_len, LANES), F32)]
    return pl.pallas_call(
        _mlstm_body, grid=(b, n_steps), in_specs=in_specs,
        out_specs=pl.BlockSpec((1, s_len, width), lambda bi, hi: (bi, 0, hi)),
        out_shape=jax.ShapeDtypeStruct((b, s_len, MLSTM_W), BF16),
        scratch_shapes=scratch, compiler_params=_cparams(("parallel", "arbitrary")),
        name="mlstm")(z3, z3, z3, z3, z3, gt, conv_w, conv_w, bcol, brow, head_gain.reshape(MLSTM_W, 1))


MOBA_ROWS = 2


def _moba_body(q_ref, k_ref, v_ref, cos_ref, sin_ref, o_ref, kr_scr, vt_scr, km_scr, sel_scr):
    i = pl.program_id(1)
    n_rows, s_len = k_ref.shape[0], k_ref.shape[1]
    bs = MOBA_BLOCK
    nb = s_len // bs
    heads = [slice(h * HEAD_DIM, (h + 1) * HEAD_DIM) for h in range(MOBA_HEADS)]
    units = [(b, h) for b in range(n_rows) for h in range(MOBA_HEADS)]

    @pl.when(i == 0)
    def _():
        for u, (b, h) in enumerate(units):
            _store_transposed(vt_scr.at[u], v_ref[b, :, heads[h]])
            kr = _rotate(k_ref[b, :, heads[h]], cos_ref[...], sin_ref[...])
            kr_scr[b, :, heads[h]] = kr.astype(BF16)
            rows = [jnp.sum(kr[n * bs:(n + 1) * bs, :], axis=0, keepdims=True) / float(bs) for n in range(nb)]
            rows.append(jnp.zeros((km_scr.shape[1] - nb, HEAD_DIM), F32))
            km_scr[u] = jnp.concatenate(rows, axis=0)

    t0 = pl.multiple_of(i * bs, bs)
    cos_q = cos_ref[pl.ds(t0, bs), :]
    sin_q = sin_ref[pl.ds(t0, bs), :]
    blk = lax.broadcasted_iota(jnp.int32, (km_scr.shape[1], bs), 0)
    causal_bias = jnp.where(lax.broadcasted_iota(jnp.int32, (bs, bs), 0)
                            <= lax.broadcasted_iota(jnp.int32, (bs, bs), 1), 0.0, NEG_INF)

    qbs = []
    for u, (b, h) in enumerate(units):
        qr = _rotate(q_ref[b, :, heads[h]], cos_q, sin_q)
        gate_t = lax.dot_general(km_scr[u], qr, _NT, precision=HIGHEST, preferred_element_type=F32)
        val = jnp.where(blk < i, gate_t, NEG_INF)
        picked = jnp.where(val > 0.5 * NEG_INF, _rank_before(val, nb, 0), float(nb)) < MOBA_TOPK
        sel_scr[u] = jnp.where(picked, 0.0, NEG_INF)
        qbs.append((qr * (HEAD_DIM ** -0.5 * LOG2_E)).astype(BF16))

    def key_tiles(k0, n_blocks):
        return ([kr_scr[b, pl.ds(k0, n_blocks * bs), heads[h]] for b, h in units],
                [vt_scr[u, :, pl.ds(k0, n_blocks * bs)] for u in range(len(units))])

    init = tuple(_flash_steps(qbs, *key_tiles(t0, 1), [causal_bias] * len(units), None))

    def past_blocks(j, n_blocks, carry):
        biases = [jnp.concatenate([jnp.broadcast_to(sel_scr[u, pl.ds(j + d, 1), :], (bs, bs))
                                   for d in range(n_blocks)], axis=0) for u in range(len(units))]
        k0 = j * bs if isinstance(j, int) else pl.multiple_of(j * bs, bs)
        return tuple(_flash_steps(qbs, *key_tiles(k0, n_blocks), biases, carry))

    odd = i % 2
    carry = lax.cond(odd == 1, lambda c: past_blocks(0, 1, c), lambda c: c, init)
    fin = lax.fori_loop(0, i // 2, lambda p, c: past_blocks(odd + 2 * p, 2, c), carry)
    for u, (b, h) in enumerate(units):
        o_ref[b, :, heads[h]] = _flash_output(fin[2 * u + 1]).T.astype(o_ref.dtype)


def _moba(z3, cos2, sin2):
    b, s_len, _ = z3.shape
    nb = s_len // MOBA_BLOCK
    q_off = 4 * MLSTM_W // MOBA_W
    rows = MOBA_ROWS if b % MOBA_ROWS == 0 else 1
    n_units = rows * MOBA_HEADS

    def kv(off):
        return pl.BlockSpec((rows, s_len, MOBA_W), lambda bi, i: (bi, 0, off))

    in_specs = [pl.BlockSpec((rows, MOBA_BLOCK, MOBA_W), lambda bi, i: (bi, i, q_off)),
                kv(q_off + 1), kv(q_off + 2),
                pl.BlockSpec((s_len, HEAD_DIM), lambda bi, i: (0, 0)),
                pl.BlockSpec((s_len, HEAD_DIM), lambda bi, i: (0, 0))]
    scratch = [pltpu.VMEM((rows, s_len, MOBA_W), BF16),
               pltpu.VMEM((n_units, VT_ROWS, s_len), BF16),
               pltpu.VMEM((n_units, BF16_SUBLANES, HEAD_DIM), F32),
               pltpu.VMEM((n_units, BF16_SUBLANES, MOBA_BLOCK), F32)]
    return pl.pallas_call(
        _moba_body, grid=(b // rows, nb), in_specs=in_specs,
        out_specs=pl.BlockSpec((rows, MOBA_BLOCK, MOBA_W), lambda bi, i: (bi, i, 0)),
        out_shape=jax.ShapeDtypeStruct((b, s_len, MOBA_W), BF16),
        scratch_shapes=scratch, compiler_params=_cparams(("parallel", "arbitrary")),
        name="moba")(z3, z3, z3, cos2, sin2)


def _gelu_tanh(x):
    return x * (0.5 * (1.0 + jnp.tanh(np.sqrt(2.0 / np.pi) * (x + 0.044715 * (x * x * x)))))


def _nsa_compress_body(kc_ref, vc_ref, pe_ref, wk1_ref, wk2_ref, wv1_ref, wv2_ref, ko_ref, vo_ref):
    n_rows = kc_ref.shape[1] // CMP_STRIDE
    halves = CMP_LEN // CMP_STRIDE
    assert halves == 2

    def compress(x_ref, pe, w1_ref, w2_ref):
        rows = [x_ref[0, pl.ds(l, n_rows, stride=CMP_STRIDE), :] for l in range(CMP_STRIDE)]
        half = CMP_STRIDE * HEAD_DIM
        first = jnp.concatenate([(r + pe[l:l + 1, :]).astype(BF16) for l, r in enumerate(rows)], axis=1)
        second = jnp.concatenate([(r + pe[CMP_STRIDE + l:CMP_STRIDE + l + 1, :]).astype(BF16)
                                  for l, r in enumerate(rows)], axis=1)
        ya = jnp.dot(first, w1_ref[:half, :], preferred_element_type=F32)
        yb = jnp.dot(second, w1_ref[half:, :], preferred_element_type=F32)
        pre = ya + pltpu.roll(yb, n_rows - 1, axis=0)
        return jnp.dot(_gelu_tanh(pre).astype(BF16), w2_ref[...], preferred_element_type=F32)

    ko_ref[0, 0] = compress(kc_ref, pe_ref[0], wk1_ref, wk2_ref)
    vo_ref[0, 0] = compress(vc_ref, pe_ref[1], wv1_ref, wv2_ref)


def _nsa_compress(z3, cmp_pos, wk1, wk2, wv1, wv2):
    b, s_len, _ = z3.shape
    n_rows = s_len // CMP_STRIDE
    kc_off = NSA_W // LANES
    vc_off = kc_off + NSA_GROUPS
    const2 = lambda bi, gi: (0, 0)
    in_specs = [pl.BlockSpec((1, s_len, HEAD_DIM), lambda bi, gi: (bi, 0, kc_off + gi)),
                pl.BlockSpec((1, s_len, HEAD_DIM), lambda bi, gi: (bi, 0, vc_off + gi)),
                pl.BlockSpec(cmp_pos.shape, lambda bi, gi: (0, 0, 0)),
                pl.BlockSpec(wk1.shape, const2), pl.BlockSpec(wk2.shape, const2),
                pl.BlockSpec(wv1.shape, const2), pl.BlockSpec(wv2.shape, const2)]
    out_spec = pl.BlockSpec((1, 1, n_rows, HEAD_DIM), lambda bi, gi: (bi, gi, 0, 0))
    out_sds = jax.ShapeDtypeStruct((b, NSA_GROUPS, n_rows, HEAD_DIM), F32)
    return pl.pallas_call(
        _nsa_compress_body, grid=(b, NSA_GROUPS), in_specs=in_specs,
        out_specs=[out_spec, out_spec], out_shape=[out_sds, out_sds],
        compiler_params=_cparams(("parallel", "parallel")), name="nsa_compress")(
            z3, z3, cmp_pos, wk1, wk2, wv1, wv2)


def _nsa_body(q_ref, ks_ref, vs_ref, kw_ref, vw_ref, kc_ref, vc_ref, gate_ref, cos_ref, sin_ref, ov_ref,
              o_ref, krs, vst, krw, vwt, sel_scr):
    i = pl.program_id(1)
    tq = NSA_TQ
    tk = NSA_TQ
    scale = HEAD_DIM ** -0.5
    n_sel_blk = ks_ref.shape[1] // SEL_BLOCK
    n_cmp = (ks_ref.shape[1] - CMP_LEN) // CMP_STRIDE + 1
    shift = SEL_BLOCK.bit_length() - 1
    assert 1 << shift == SEL_BLOCK and n_sel_blk <= LANES
    blk_per_tile = tk // SEL_BLOCK
    groups = [slice(g * HEAD_DIM, (g + 1) * HEAD_DIM) for g in range(NSA_GROUPS)]

    @pl.when(i == 0)
    def _():
        for g, cols in enumerate(groups):
            krs[:, cols] = _rotate(ks_ref[0, :, cols], cos_ref[...], sin_ref[...]).astype(BF16)
            krw[:, cols] = _rotate(kw_ref[0, :, cols], cos_ref[...], sin_ref[...]).astype(BF16)
            _store_transposed(vst.at[g], vs_ref[0, :, cols])
            _store_transposed(vwt.at[g], vw_ref[0, :, cols])

    t0 = pl.multiple_of(i * tq, tq)
    pos_row = t0 + lax.broadcasted_iota(jnp.int32, (1, tq), 1)
    cos_q = cos_ref[pl.ds(t0, tq), :]
    sin_q = sin_ref[pl.ds(t0, tq), :]

    n_col = lax.broadcasted_iota(jnp.int32, (LANES, 1), 0)
    cmp_end = jnp.where(n_col < n_cmp, n_col * CMP_STRIDE + (CMP_LEN - 1), jnp.iinfo(jnp.int32).max)
    cmp_ok = cmp_end <= pos_row
    blk_i = lax.broadcasted_iota(jnp.int32, (n_sel_blk, tq), 0)
    behind = (pos_row >> shift) - blk_i
    units = [(g, r) for g in range(NSA_GROUPS) for r in range(NSA_REP)]
    qr_heads, o_cmp = [], []
    for g in range(NSA_GROUPS):
        kc = kc_ref[0, g].astype(BF16)
        vc = vc_ref[0, g].astype(BF16)
        p_sum = jnp.zeros((LANES, tq), F32)
        for r in range(NSA_REP):
            head = (g * NSA_REP + r) * HEAD_DIM
            q = q_ref[0, :, head:head + HEAD_DIM] * scale
            qr_heads.append(_rotate(q * LOG2_E, cos_q, sin_q).astype(BF16))
            s = lax.dot_general(kc, q.astype(BF16), _NT, preferred_element_type=F32)
            s = jnp.where(cmp_ok, s, NEG_INF)
            e = jnp.where(cmp_ok, jnp.exp(s - jnp.max(s, axis=0, keepdims=True)), 0.0)
            p = e / jnp.maximum(jnp.sum(e, axis=0, keepdims=True), 1e-30)
            o_cmp.append(lax.dot_general(p.astype(BF16), vc, _TN, preferred_element_type=F32))
            p_sum = p_sum + p
        imp = jnp.dot(ov_ref[...], p_sum, precision=HIGHEST, preferred_element_type=F32)
        val = jnp.where(behind == 0, FORCED_SCORE, jnp.where(behind == 1, FORCED_SCORE, imp))
        val = jnp.where(blk_i == 0, FORCED_SCORE, val)
        val = jnp.where(behind >= 0, val, NEG_INF)
        picked = jnp.where(val > 0.5 * NEG_INF, _rank_before(val, n_sel_blk, 0), float(n_sel_blk)) < SEL_TOPN
        sel_scr[g] = jnp.where(picked, 0.0, NEG_INF)

    def sel_bias(g, k0, n_tiles):
        blk0 = (k0 >> shift) if isinstance(k0, int) else pl.multiple_of(k0 >> shift, blk_per_tile)
        return jnp.concatenate([jnp.broadcast_to(sel_scr[g, pl.ds(blk0 + b, 1), :], (SEL_BLOCK, tq))
                                for b in range(n_tiles * blk_per_tile)], axis=0)

    def sel_tiles(k0, n_tiles, carry):
        biases = [sel_bias(g, k0, n_tiles) for g in range(NSA_GROUPS)]
        return tuple(_flash_steps(qr_heads, [krs[pl.ds(k0, n_tiles * tk), groups[g]] for g, _ in units],
                                  [vst[g, :, pl.ds(k0, n_tiles * tk)] for g, _ in units],
                                  [biases[g] for g, _ in units], carry))

    causal_bias = jnp.where(lax.broadcasted_iota(jnp.int32, (tk, tq), 0)
                            <= lax.broadcasted_iota(jnp.int32, (tk, tq), 1), 0.0, NEG_INF)
    span = WINDOW + tq
    start = pl.multiple_of(jnp.maximum(t0 - WINDOW, 0), tq)
    gap = ((t0 - start) + lax.broadcasted_iota(jnp.int32, (span, tq), 1)
           - lax.broadcasted_iota(jnp.int32, (span, tq), 0))
    win_bias = jnp.where(gap >= 0, jnp.where(gap < WINDOW, 0.0, NEG_INF), NEG_INF)
    diag_biases = [sel_bias(g, t0, 1) + causal_bias for g in range(NSA_GROUPS)]
    n = len(units)
    group = _flash_steps(qr_heads * 2,
                         [krs[pl.ds(t0, tk), groups[g]] for g, _ in units]
                         + [krw[pl.ds(start, span), groups[g]] for g, _ in units],
                         [vst[g, :, pl.ds(t0, tk)] for g, _ in units]
                         + [vwt[g, :, pl.ds(start, span)] for g, _ in units],
                         [diag_biases[g] for g, _ in units] + [win_bias] * n, None)
    win_fin = group[2 * n:]

    odd = i % 2
    carry = lax.cond(odd == 1, lambda c: sel_tiles(0, 1, c), lambda c: c, tuple(group[:2 * n]))
    sel_fin = lax.fori_loop(0, i // 2, lambda p, c: sel_tiles(pl.multiple_of((odd + 2 * p) * tk, tk), 2, c), carry)

    gates = _sigmoid(gate_ref[0])
    gates_t = gates.T
    for u in range(n):
        c0 = NSA_GATES * u
        o_t = (gates_t[c0 + 1:c0 + 2, :] * _flash_output(sel_fin[2 * u + 1])
               + gates_t[c0 + 2:c0 + 3, :] * _flash_output(win_fin[2 * u + 1]))
        o_ref[0, :, u * HEAD_DIM:(u + 1) * HEAD_DIM] = (gates[:, c0:c0 + 1] * o_cmp[u] + o_t.T).astype(o_ref.dtype)


def _nsa_attention(z3, k_cmp, v_cmp, cos2, sin2):
    b, s_len, _ = z3.shape
    assert s_len >= WINDOW + NSA_TQ and s_len % NSA_TQ == 0
    n_cmp = (s_len - CMP_LEN) // CMP_STRIDE + 1
    n_blk = s_len // SEL_BLOCK
    n_rows = k_cmp.shape[2]
    assert n_rows == LANES
    cmp_start = np.arange(n_rows) * CMP_STRIDE
    cmp_end = cmp_start + CMP_LEN - 1
    blk_lo = np.arange(LANES) * SEL_BLOCK
    overlap = ((cmp_start[:, None] <= blk_lo[None, :] + SEL_BLOCK - 1) & (cmp_end[:, None] >= blk_lo[None, :])
               & (np.arange(n_rows)[:, None] < n_cmp)).astype(np.float32)
    overlap_t = np.ascontiguousarray(overlap.T[:n_blk])
    base = NSA_W // NSA_KV_W

    def kv(off):
        return pl.BlockSpec((1, s_len, NSA_KV_W), lambda bi, i: (bi, 0, base + off))

    cmp_spec = pl.BlockSpec((1, NSA_GROUPS, n_rows, HEAD_DIM), lambda bi, i: (bi, 0, 0, 0))
    const2 = lambda bi, i: (0, 0)
    in_specs = [pl.BlockSpec((1, NSA_TQ, NSA_W), lambda bi, i: (bi, i, 0)),
                kv(2), kv(3), kv(4), kv(5), cmp_spec, cmp_spec,
                pl.BlockSpec((1, NSA_TQ, LANES), lambda bi, i: (bi, i, ODD_GATE_BLK)),
                pl.BlockSpec((s_len, HEAD_DIM), const2), pl.BlockSpec((s_len, HEAD_DIM), const2),
                pl.BlockSpec((n_blk, n_rows), const2)]
    vt_shape = (NSA_GROUPS, VT_ROWS, s_len)
    scratch = [pltpu.VMEM((s_len, NSA_KV_W), BF16), pltpu.VMEM(vt_shape, BF16),
               pltpu.VMEM((s_len, NSA_KV_W), BF16), pltpu.VMEM(vt_shape, BF16),
               pltpu.VMEM((NSA_GROUPS, n_blk, NSA_TQ), F32)]
    return pl.pallas_call(
        _nsa_body, grid=(b, s_len // NSA_TQ), in_specs=in_specs,
        out_specs=pl.BlockSpec((1, NSA_TQ, NSA_W), lambda bi, i: (bi, i, 0)),
        out_shape=jax.ShapeDtypeStruct((b, s_len, NSA_W), BF16),
        scratch_shapes=scratch, compiler_params=_cparams(("parallel", "arbitrary")),
        name="nsa_attention")(z3, z3, z3, z3, z3, k_cmp, v_cmp, z3, cos2, sin2, jnp.asarray(overlap_t))


def _split_bf16(x):
    hi = x.astype(BF16)
    return hi, (x - hi.astype(F32)).astype(BF16)


def _proj_moe_body(*refs, n_act, final_norm):
    res_ref = refs[0]
    a_refs = refs[1:1 + n_act]
    wo_refs = refs[1 + n_act:1 + 2 * n_act]
    (g_ref, wr_ref, br_ref, w1_ref, w3_ref, w2_ref, fg_ref, o_ref,
     xn_scr, cw_scr, hd_scr) = refs[1 + 2 * n_act:]
    j = pl.program_id(1)
    tm = res_ref.shape[0]
    lane = lax.broadcasted_iota(jnp.int32, (tm, LANES), 1).astype(F32)
    neg = float("-inf")

    @pl.when(j == 0)
    def _():
        h = res_ref[...]
        for a_ref, wo_ref in zip(a_refs, wo_refs):
            h = h + jnp.dot(a_ref[...], wo_ref[...], preferred_element_type=F32)
        o_ref[...] = h
        xn = _rms(h, g_ref[...])
        xh, xl = _split_bf16(xn)
        xn_scr[...] = xh
        hi_terms = jnp.dot(xh, wr_ref[...], preferred_element_type=F32)
        logits = (hi_terms[:, :LANES] + (jnp.dot(xl, wr_ref[:, :LANES], preferred_element_type=F32)
                                         + hi_terms[:, LANES:])) + br_ref[...]
        lt = logits.T[:ROUTER_SLOTS]
        slot = lax.broadcasted_iota(jnp.int32, lt.shape, 0).astype(F32)
        is_g = slot < MOE_GROUPS
        gl = jnp.where(is_g, lt, neg)
        g_max = jnp.max(gl, axis=0, keepdims=True)
        g_w = 1.0 / jnp.sum(jnp.where(is_g, jnp.exp(gl - g_max), 0.0), axis=0, keepdims=True)
        g_top = jnp.min(jnp.where(gl == g_max, slot, float(LANES)), axis=0, keepdims=True)
        lo = MOE_GROUPS + MOE_EPG * g_top
        el = jnp.where(slot >= lo, jnp.where(slot < lo + MOE_EPG, lt, neg), neg)
        v1 = jnp.max(el, axis=0, keepdims=True)
        i1 = jnp.min(jnp.where(el == v1, slot, float(LANES)), axis=0, keepdims=True)
        el2 = jnp.where(slot == i1, neg, el)
        v2 = jnp.max(el2, axis=0, keepdims=True)
        i2 = jnp.min(jnp.where(el2 == v2, slot, float(LANES)), axis=0, keepdims=True)
        e2 = jnp.exp(v2 - v1)
        den = 1.0 + e2
        cw_t = jnp.where(slot == i1, g_w / den, 0.0) + jnp.where(slot == i2, g_w * e2 / den, 0.0)
        cw_scr[...] = jnp.concatenate([cw_t, jnp.zeros((LANES - ROUTER_SLOTS, tm), F32)], axis=0).T

    xn = xn_scr[...]
    cw = cw_scr[...]
    for r in range(MOE_EPG):
        h1 = jnp.dot(xn, w1_ref[r].astype(BF16), preferred_element_type=F32)
        h3 = jnp.dot(xn, w3_ref[r], preferred_element_type=F32)
        e_lane = (MOE_GROUPS + MOE_EPG * j + r).astype(F32)
        col = jnp.sum(jnp.where(lane == e_lane, cw, 0.0), axis=1, keepdims=True)
        hd_scr[:, r * MOE_HIDDEN:(r + 1) * MOE_HIDDEN] = ((h1 * _sigmoid(h1)) * h3 * col).astype(BF16)
    w2 = w2_ref[...].astype(BF16).reshape(MOE_EPG * MOE_HIDDEN, w2_ref.shape[2])
    o_ref[...] += jnp.dot(hd_scr[...], w2, preferred_element_type=F32)

    if final_norm:
        @pl.when(j == pl.num_programs(1) - 1)
        def _():
            o_ref[...] = _rms(o_ref[...], fg_ref[...])


def _proj_moe(res2d, acts, w_out, gain, w_g, b_g, w_e, b_e, w1, w3, w2, final_gain, final_norm, tm=1024):
    t, d = res2d.shape
    n_act = len(acts)
    n_slots = MOE_GROUPS + MOE_EXPERTS
    wr = jnp.pad(jnp.concatenate([w_g, w_e], axis=1), ((0, 0), (0, LANES - n_slots)))
    wr_cat = jnp.concatenate(_split_bf16(wr), axis=1)
    br = jnp.pad(jnp.concatenate([b_g, b_e]), (0, LANES - n_slots)).reshape(1, LANES)
    tile = lambda i, j: (i, 0)
    const = lambda i, j: (0, 0)
    group_hidden = MOE_EPG * MOE_HIDDEN
    in_specs = [pl.BlockSpec((tm, d), tile)]
    in_specs += [pl.BlockSpec((tm, a.shape[1]), tile) for a in acts]
    row0 = np.cumsum([0] + [a.shape[1] for a in acts])
    assert all(r % a.shape[1] == 0 for r, a in zip(row0, acts)) and row0[-1] == w_out.shape[0]
    in_specs += [pl.BlockSpec((a.shape[1], d), functools.partial(lambda i, j, blk: (blk, 0), blk=int(r // a.shape[1])))
                 for r, a in zip(row0, acts)]
    in_specs += [pl.BlockSpec((1, d), const),
                 pl.BlockSpec((d, 2 * LANES), const),
                 pl.BlockSpec((1, LANES), const),
                 pl.BlockSpec((MOE_EPG, d, MOE_HIDDEN), lambda i, j: (j, 0, 0)),
                 pl.BlockSpec((MOE_EPG, d, MOE_HIDDEN), lambda i, j: (j, 0, 0)),
                 pl.BlockSpec((MOE_EPG, MOE_HIDDEN, d), lambda i, j: (j, 0, 0)),
                 pl.BlockSpec((1, d), const)]
    scratch = [pltpu.VMEM((tm, d), BF16), pltpu.VMEM((tm, LANES), F32), pltpu.VMEM((tm, group_hidden), BF16)]
    return pl.pallas_call(
        functools.partial(_proj_moe_body, n_act=n_act, final_norm=final_norm),
        grid=(t // tm, MOE_GROUPS), in_specs=in_specs,
        out_specs=pl.BlockSpec((tm, d), tile),
        out_shape=jax.ShapeDtypeStruct((t, d), F32),
        scratch_shapes=scratch, compiler_params=_cparams(("parallel", "arbitrary")),
        name="out_proj_moe")(res2d, *acts, *([w_out] * n_act), gain.reshape(1, d), wr_cat, br,
                             w1, w3.astype(BF16), w2,
                             final_gain.reshape(1, d))


def _rope_tables(s_len):
    half = HEAD_DIM // 2
    inv = ROPE_THETA ** (-(jnp.arange(half, dtype=F32) / half))
    ang = jnp.arange(s_len, dtype=F32)[:, None] * inv[None, :]
    cos, sin = jnp.cos(ang), jnp.sin(ang)
    return jnp.concatenate([cos, cos], axis=-1), jnp.concatenate([-sin, sin], axis=-1)


def _even_weights(w_in):
    n_if = 2 * MLSTM_HEADS
    a = 4 * MLSTM_W
    w_in = w_in.astype(BF16)
    pieces = ((0, a, 0), (a + n_if, w_in.shape[1], a), (a, a + n_if, w_in.shape[1] - n_if))
    w_if_t = jnp.concatenate([w_in[:, a:a + n_if].T,
                              jnp.zeros((BF16_SUBLANES - n_if, w_in.shape[0]), w_in.dtype)], axis=0)
    return w_in, w_if_t, pieces


def _odd_weights(w_in):
    a = NSA_W + 6 * NSA_KV_W
    n_gates = NSA_HEADS * NSA_GATES
    assert w_in.shape[1] == a + n_gates and n_gates <= LANES
    return w_in.astype(BF16)


def kernel(x, mix_norm_0, w_in_0, mlstm_conv_0, mlstm_gate_b_0, mlstm_head_norm_0, w_out_0, ffn_norm_0, router_group_0, router_group_b_0, router_expert_0, router_expert_b_0, moe_w1_0, moe_w3_0, moe_w2_0, mix_norm_1, w_in_1, nsa_cmp_pos_1, nsa_cmp_k1_1, nsa_cmp_k2_1, nsa_cmp_v1_1, nsa_cmp_v2_1, w_out_1, ffn_norm_1, router_group_1, router_group_b_1, router_expert_1, router_expert_b_1, moe_w1_1, moe_w3_1, moe_w2_1, final_norm):
    b, s_len, d = x.shape
    t = b * s_len
    cos2, sin2 = _rope_tables(s_len)
    x2d = x.reshape(t, d)

    w_main, w_if_t, pieces = _even_weights(w_in_0)
    z0, gt = _norm_matmul(x2d, mix_norm_0, w_main, w_if_t, pieces)
    z0 = z0.reshape(b, s_len, EVEN_N)
    h_m = _mlstm(z0, gt, mlstm_conv_0, mlstm_gate_b_0, mlstm_head_norm_0)
    o_b = _moba(z0, cos2, sin2)
    h = _proj_moe(x2d, [h_m.reshape(t, MLSTM_W), o_b.reshape(t, MOBA_W)], w_out_0.astype(BF16),
                  ffn_norm_0, router_group_0, router_group_b_0, router_expert_0, router_expert_b_0,
                  moe_w1_0, moe_w3_0, moe_w2_0, final_norm, False)

    z1 = _norm_matmul(h, mix_norm_1, _odd_weights(w_in_1)).reshape(b, s_len, ODD_N)
    k_cmp, v_cmp = _nsa_compress(z1, nsa_cmp_pos_1, nsa_cmp_k1_1.astype(BF16), nsa_cmp_k2_1.astype(BF16),
                                 nsa_cmp_v1_1.astype(BF16), nsa_cmp_v2_1.astype(BF16))
    o = _nsa_attention(z1, k_cmp, v_cmp, cos2, sin2)
    h = _proj_moe(h, [o.reshape(t, NSA_W)], w_out_1.astype(BF16),
                  ffn_norm_1, router_group_1, router_group_b_1, router_expert_1, router_expert_b_1,
                  moe_w1_1, moe_w3_1, moe_w2_1, final_norm, True)
    return h.reshape(b, s_len, d)
```

```python
import functools

import numpy as np
import jax
import jax.numpy as jnp
from jax import lax
from jax.experimental import pallas as pl
from jax.experimental.pallas import tpu as pltpu

F32 = jnp.float32
BF16 = jnp.bfloat16
HIGHEST = lax.Precision.HIGHEST

LANES = 128
BF16_SUBLANES = 16
D_MODEL = 1024
HEAD_DIM = 128
ROPE_THETA = 10000.0
NORM_EPS = 1e-6
NEG_INF = -1e30
FORCED_SCORE = 1e4

MLSTM_HEADS = 4
MLSTM_W = MLSTM_HEADS * HEAD_DIM
MLSTM_CHUNK = 64
MLSTM_CONV = 4
MLSTM_GATE_CAP = 15.0
MOBA_HEADS = 4
MOBA_W = MOBA_HEADS * HEAD_DIM
MOBA_BLOCK = 256
MOBA_TOPK = 3

NSA_HEADS = 8
NSA_GROUPS = 2
NSA_REP = NSA_HEADS // NSA_GROUPS
NSA_W = NSA_HEADS * HEAD_DIM
NSA_KV_W = NSA_GROUPS * HEAD_DIM
CMP_LEN = 32
CMP_STRIDE = 16
SEL_BLOCK = 64
SEL_TOPN = 8
WINDOW = 512
NSA_TQ = 256

MOE_GROUPS = 4
MOE_EPG = 4
MOE_EXPERTS = MOE_GROUPS * MOE_EPG
MOE_HIDDEN = D_MODEL // 4
ROUTER_SLOTS = 32

EVEN_N = 4 * MLSTM_W + 3 * MOBA_W + LANES
EVEN_IF_BLK = (4 * MLSTM_W + 3 * MOBA_W) // LANES
ODD_GATE_BLK = (NSA_W + 6 * NSA_KV_W) // LANES
ODD_N = NSA_W + 6 * NSA_KV_W + LANES
NSA_GATES = 3

VMEM_LIMIT = 56 * 1024 * 1024
VT_ROWS = HEAD_DIM + BF16_SUBLANES
LOG2_E = float(np.log2(np.e))

_NT = (((1,), (1,)), ((), ()))
_TN = (((0,), (0,)), ((), ()))


def _cparams(sem):
    return pltpu.CompilerParams(dimension_semantics=sem, vmem_limit_bytes=VMEM_LIMIT)


def _rms(x, g):
    return x * lax.rsqrt(jnp.mean(x * x, axis=-1, keepdims=True) + NORM_EPS) * g


def _sigmoid(x):
    return 0.5 * jnp.tanh(0.5 * x) + 0.5


def _log_sigmoid(x):
    return -(jnp.maximum(-x, 0.0) + jnp.log1p(jnp.exp(-jnp.abs(x))))


def _rotate(x, cos2, sin2):
    return x * cos2 + pltpu.roll(x, HEAD_DIM // 2, axis=1) * sin2


def _flash_steps(qs, k_ts, vt_ts, biases, carries):
    n = len(qs)
    ss = [lax.dot_general(k_ts[u], qs[u], _NT, preferred_element_type=F32) + biases[u] for u in range(n)]
    ms, es, alphas, pvs = [], [], [], []

    def value_matmul(u):
        pv = jnp.dot(vt_ts[u], es[u], preferred_element_type=F32)
        if carries is not None:
            pv = alphas[u] * carries[2 * u + 1] + pv
        pvs.append(pv)

    for u in range(n):
        m_new = jnp.max(ss[u], axis=0, keepdims=True)
        if carries is not None:
            m_new = jnp.maximum(carries[2 * u], m_new)
            alphas.append(jnp.exp2(carries[2 * u] - m_new))
        ms.append(m_new)
        es.append(jnp.exp2(ss[u] - m_new).astype(BF16))
        if u > 0:
            value_matmul(u - 1)
    value_matmul(n - 1)
    out = []
    for u in range(n):
        out += [ms[u], pvs[u]]
    return out


def _flash_output(acc):
    return acc[:HEAD_DIM] / jnp.maximum(acc[HEAD_DIM:HEAD_DIM + 1], 1e-30)


def _store_transposed(dst_ref, x):
    n_rows, n_cols = x.shape
    for r in range(0, n_rows, LANES):
        dst_ref[:n_cols, r:r + LANES] = x[r:r + LANES, :].T.astype(dst_ref.dtype)
    dst_ref[n_cols:, :] = jnp.ones((dst_ref.shape[0] - n_cols, n_rows), dst_ref.dtype)


def _rank_before(v, n_valid, axis):
    idx = lax.broadcasted_iota(jnp.int32, v.shape, axis)
    rank = jnp.zeros(v.shape, F32)
    for m in range(n_valid):
        vm = v[m:m + 1, :] if axis == 0 else v[:, m:m + 1]
        tie = jnp.where(idx > m, 1.0, 0.0)
        rank = rank + jnp.where(vm > v, 1.0, jnp.where(vm == v, tie, 0.0))
    return rank


IN_PROJ_ROWS = 512
IN_PROJ_COLS = 512
GATE_ROWS = 2 * MLSTM_HEADS


def _norm_matmul_body(*refs, pieces, t_col):
    if t_col is None:
        x_ref, g_ref, w_ref, o_ref = refs
    else:
        x_ref, g_ref, w_ref, o_ref, ot_ref = refs
    yb = _rms(x_ref[...], g_ref[...]).astype(BF16)
    n_done = 0
    for src0, src1, dst0 in pieces:
        assert dst0 == n_done
        for c0 in range(src0, src1, IN_PROJ_COLS):
            c1 = min(src1, c0 + IN_PROJ_COLS)
            o_ref[:, n_done:n_done + c1 - c0] = jnp.dot(yb, w_ref[:, c0:c1], preferred_element_type=F32)
            n_done += c1 - c0
    if o_ref.shape[1] > n_done:
        o_ref[:, n_done:] = jnp.zeros((o_ref.shape[0], o_ref.shape[1] - n_done), F32)
    if t_col is not None:
        ot_ref[...] = o_ref[:, t_col:t_col + LANES].T[:ot_ref.shape[0]]


def _norm_matmul(x2d, gain, w, pieces=None, t_col=None):
    t, d = x2d.shape
    tm = IN_PROJ_ROWS
    n = -(-w.shape[1] // LANES) * LANES
    pieces = pieces or ((0, w.shape[1], 0),)
    in_specs = [pl.BlockSpec((tm, d), lambda i: (i, 0)),
                pl.BlockSpec((1, d), lambda i: (0, 0)),
                pl.BlockSpec(w.shape, lambda i: (0, 0))]
    out_specs = [pl.BlockSpec((tm, n), lambda i: (i, 0))]
    out_shape = [jax.ShapeDtypeStruct((t, n), F32)]
    if t_col is not None:
        out_specs.append(pl.BlockSpec((BF16_SUBLANES, tm), lambda i: (0, i)))
        out_shape.append(jax.ShapeDtypeStruct((BF16_SUBLANES, t), F32))
    outs = pl.pallas_call(
        functools.partial(_norm_matmul_body, pieces=pieces, t_col=t_col),
        grid=(t // tm,), in_specs=in_specs, out_specs=out_specs, out_shape=out_shape,
        compiler_params=_cparams(("parallel",)), name="norm_in_proj")(x2d, gain.reshape(1, d), w)
    return outs if t_col is not None else outs[0]


def _chunk_cumsum(x, axis):
    idx = lax.broadcasted_iota(jnp.int32, x.shape, axis) % MLSTM_CHUNK
    d = 1
    while d < MLSTM_CHUNK:
        x = x + jnp.where(idx >= d, pltpu.roll(x, d, axis=axis), 0.0)
        d *= 2
    return x


MLSTM_HEADS_PER_STEP = 2
MLSTM_GROUP = 16


def _mlstm_body(q_ref, k_ref, v_ref, og_ref, if_ref, gt_ref, cwq_ref, cwk_ref, bcol_ref, brow_ref,
                gain_ref, out_ref, ks, qts, vts, hts, css, brs, lirs, gcol):
    hp = pl.program_id(1)
    s_len = q_ref.shape[1]
    n_local = q_ref.shape[2] // HEAD_DIM
    n_chunks = s_len // MLSTM_CHUNK
    L = MLSTM_CHUNK
    per_slab = LANES // L
    row = lax.broadcasted_iota(jnp.int32, (s_len, HEAD_DIM), 0)
    cols = [slice(a * HEAD_DIM, (a + 1) * HEAD_DIM) for a in range(n_local)]

    def conv_silu(x, w):
        acc = x * w[MLSTM_CONV - 1:MLSTM_CONV, :]
        for d in range(1, MLSTM_CONV):
            shifted = jnp.where(row >= d, pltpu.roll(x, d, axis=0), 0.0)
            acc = acc + shifted * w[MLSTM_CONV - 1 - d:MLSTM_CONV - d, :]
        return acc * _sigmoid(acc)

    def store_chunks_transposed(dst, x):
        for p in range(s_len // LANES):
            slab_t = x[p * LANES:(p + 1) * LANES, :].T
            for j in range(per_slab):
                dst[per_slab * p + j] = slab_t[:, j * L:(j + 1) * L].astype(dst.dtype)

    lane = lax.broadcasted_iota(jnp.int32, gcol.shape, 1)

    @pl.when(hp == 0)
    def _():
        pre = if_ref[0] + bcol_ref[...]
        pre = MLSTM_GATE_CAP * jnp.tanh(pre / MLSTM_GATE_CAP)
        gcol[...] = jnp.where(lane < MLSTM_HEADS, pre, _chunk_cumsum(_log_sigmoid(pre), 0))

    pr = gt_ref[...] + brow_ref[...]
    pr = MLSTM_GATE_CAP * jnp.tanh(pr / MLSTM_GATE_CAP)
    sub = lax.broadcasted_iota(jnp.int32, pr.shape, 0)
    b_rows = _chunk_cumsum(_log_sigmoid(pr), 1)
    g_all = gcol[...]

    gains = []
    for a in range(n_local):
        h = hp * n_local + a
        store_chunks_transposed(qts.at[a], conv_silu(q_ref[0, :, cols[a]], cwq_ref[:, cols[a]]) * (HEAD_DIM ** -0.5))
        store_chunks_transposed(vts.at[a], v_ref[0, :, cols[a]])
        ks[:, cols[a]] = conv_silu(k_ref[0, :, cols[a]], cwk_ref[:, cols[a]]).astype(BF16)
        css[a] = jnp.sum(jnp.where(lane == h, g_all, jnp.where(lane == h + MLSTM_HEADS, -g_all, 0.0)),
                         axis=1, keepdims=True)
        li_row = jnp.sum(jnp.where(sub == h, pr, 0.0), axis=0, keepdims=True)
        b_row = jnp.sum(jnp.where(sub == h + MLSTM_HEADS, b_rows, 0.0), axis=0, keepdims=True)
        for c in range(n_chunks):
            brs[a, c] = b_row[:, c * L:(c + 1) * L]
            lirs[a, c] = li_row[:, c * L:(c + 1) * L]
        gains.append(jnp.broadcast_to(gain_ref[a * HEAD_DIM:(a + 1) * HEAD_DIM, :], (HEAD_DIM, L)))

    tri = lax.broadcasted_iota(jnp.int32, (L, L), 0) <= lax.broadcasted_iota(jnp.int32, (L, L), 1)

    group_size = MLSTM_GROUP
    assert n_chunks % group_size == 0

    def group(gi, carry):
        states = [list(carry[3 * a:3 * a + 3]) for a in range(n_local)]
        cs = [gi * group_size + j for j in range(group_size)]
        r0s = [pl.multiple_of(c * L, L) for c in cs]
        units = [(a, j) for a in range(n_local) for j in range(group_size)]
        ks_ = {(a, j): ks[pl.ds(r0s[j], L), cols[a]] for a, j in units}
        q_ts = {(a, j): qts[a, cs[j]] for a, j in units}
        v_ts = {(a, j): vts[a, cs[j]] for a, j in units}
        b_rs = {(a, j): brs[a, cs[j]] for a, j in units}
        kvs, ksums, kqs, g_maxs, b_lasts = {}, {}, {}, {}, {}
        for u in units:
            a, j = u
            b_last = b_rs[u][:, L - 1:L]
            g = b_last - b_rs[u] + lirs[a, cs[j]]
            g_max = jnp.max(g, axis=1, keepdims=True)
            w = jnp.exp(g - g_max)
            kvs[u] = jnp.dot((v_ts[u] * w).astype(BF16), ks_[u], preferred_element_type=F32)
            ksums[u] = jnp.dot(jnp.broadcast_to(w, (BF16_SUBLANES, L)).astype(BF16), ks_[u],
                               preferred_element_type=F32)[:1]
            kqs[u] = jnp.dot(ks_[u], q_ts[u], preferred_element_type=F32)
            g_maxs[u] = g_max
            b_lasts[u] = b_last
        c_ins, n_ins, m_ins = {}, {}, {}
        for u in units:
            a, j = u
            c_state, n_state, m_state = states[a]
            c_ins[u] = c_state.astype(BF16)
            n_ins[u] = jnp.broadcast_to(n_state, (BF16_SUBLANES, HEAD_DIM)).astype(BF16)
            m_ins[u] = m_state
            m_new = jnp.maximum(b_lasts[u] + m_state, g_maxs[u])
            sa = jnp.exp(b_lasts[u] + m_state - m_new)
            cc = jnp.exp(g_maxs[u] - m_new)
            states[a] = [sa * c_state + cc * kvs[u], sa * n_state + cc * ksums[u], m_new]
        c_qs = {u: jnp.dot(c_ins[u], q_ts[u], preferred_element_type=F32) for u in units}
        n_qs = {u: jnp.dot(n_ins[u], q_ts[u], preferred_element_type=F32)[:1] for u in units}
        ss, m_ts, w_inters = {}, {}, {}
        for u in units:
            a, j = u
            dmat = jnp.where(tri, b_rs[u] + css[a, pl.ds(r0s[j], L), :], NEG_INF)
            inter = b_rs[u] + m_ins[u]
            m_t = jnp.maximum(inter, jnp.max(dmat, axis=0, keepdims=True))
            ss[u] = kqs[u] * jnp.exp(dmat - m_t)
            m_ts[u] = m_t
            w_inters[u] = jnp.exp(inter - m_t)
        svs = {u: jnp.dot(v_ts[u].astype(BF16), ss[u].astype(BF16), preferred_element_type=F32) for u in units}
        for u in units:
            a, j = u
            num = svs[u] + w_inters[u] * c_qs[u]
            den = jnp.sum(ss[u], axis=0, keepdims=True) + w_inters[u] * n_qs[u]
            ht = num / jnp.maximum(jnp.abs(den), jnp.exp(-m_ts[u]))
            hts[a, cs[j]] = ht * lax.rsqrt(jnp.mean(ht * ht, axis=0, keepdims=True) + NORM_EPS) * gains[a]
        return tuple(x for st in states for x in st)

    init = (jnp.zeros((HEAD_DIM, HEAD_DIM), F32), jnp.zeros((1, HEAD_DIM), F32), jnp.zeros((1, 1), F32)) * n_local
    lax.fori_loop(0, n_chunks // group_size, group, init)

    for a in range(n_local):
        for p in range(s_len // LANES):
            slab_t = jnp.concatenate([hts[a, per_slab * p + j] for j in range(per_slab)], axis=1)
            rows = slice(p * LANES, (p + 1) * LANES)
            out_ref[0, rows, cols[a]] = (slab_t.T * _sigmoid(og_ref[0, rows, cols[a]])).astype(out_ref.dtype)


def _mlstm(z3, gt, conv_w, gate_b, head_gain):
    b, s_len, _ = z3.shape
    n_chunks = s_len // MLSTM_CHUNK
    gr = gt.shape[0]
    bcol = jnp.pad(gate_b, (0, LANES - 2 * MLSTM_HEADS)).reshape(1, LANES)
    brow = jnp.pad(gate_b, (0, gr - 2 * MLSTM_HEADS)).reshape(gr, 1)
    n_local = MLSTM_HEADS_PER_STEP
    n_steps = MLSTM_HEADS // n_local
    width = n_local * HEAD_DIM

    def col(off):
        return pl.BlockSpec((1, s_len, width), lambda bi, hi: (bi, 0, off * n_steps + hi))

    in_specs = [col(0), col(1), col(2), col(3),
                pl.BlockSpec((1, s_len, LANES), lambda bi, hi: (bi, 0, EVEN_IF_BLK)),
                pl.BlockSpec((gr, s_len), lambda bi, hi: (0, bi)),
                pl.BlockSpec((MLSTM_CONV, width), lambda bi, hi: (0, hi)),
                pl.BlockSpec((MLSTM_CONV, width), lambda bi, hi: (0, n_steps + hi)),
                pl.BlockSpec((1, LANES), lambda bi, hi: (0, 0)),
                pl.BlockSpec((gr, 1), lambda bi, hi: (0, 0)),
                pl.BlockSpec((width, 1), lambda bi, hi: (hi, 0))]
    chunk_t = (n_local, n_chunks, HEAD_DIM, MLSTM_CHUNK)
    rows_t = (n_local, n_chunks, 1, MLSTM_CHUNK)
    scratch = [pltpu.VMEM((s_len, width), BF16),
               pltpu.VMEM(chunk_t, BF16), pltpu.VMEM(chunk_t, F32), pltpu.VMEM(chunk_t, F32),
               pltpu.VMEM((n_local, s_len, 1), F32),
               pltpu.VMEM(rows_t, F32), pltpu.VMEM(rows_t, F32),
               pltpu.VMEM((s_len, LANES), F32)]
    return pl.pallas_call(
        _mlstm_body, grid=(b, n_steps), in_specs=in_specs,
        out_specs=pl.BlockSpec((1, s_len, width), lambda bi, hi: (bi, 0, hi)),
        out_shape=jax.ShapeDtypeStruct((b, s_len, MLSTM_W), BF16),
        scratch_shapes=scratch, compiler_params=_cparams(("parallel", "arbitrary")),
        name="mlstm")(z3, z3, z3, z3, z3, gt, conv_w, conv_w, bcol, brow, head_gain.reshape(MLSTM_W, 1))


MOBA_ROWS = 2


def _moba_body(q_ref, k_ref, v_ref, cos_ref, sin_ref, o_ref, kr_scr, vt_scr, km_scr, sel_scr):
    i = pl.program_id(1)
    n_rows, s_len = k_ref.shape[0], k_ref.shape[1]
    bs = MOBA_BLOCK
    nb = s_len // bs
    heads = [slice(h * HEAD_DIM, (h + 1) * HEAD_DIM) for h in range(MOBA_HEADS)]
    units = [(b, h) for b in range(n_rows) for h in range(MOBA_HEADS)]

    @pl.when(i == 0)
    def _():
        for u, (b, h) in enumerate(units):
            _store_transposed(vt_scr.at[u], v_ref[b, :, heads[h]])
            kr = _rotate(k_ref[b, :, heads[h]], cos_ref[...], sin_ref[...])
            kr_scr[b, :, heads[h]] = kr.astype(BF16)
            rows = [jnp.sum(kr[n * bs:(n + 1) * bs, :], axis=0, keepdims=True) / float(bs) for n in range(nb)]
            rows.append(jnp.zeros((km_scr.shape[1] - nb, HEAD_DIM), F32))
            km_scr[u] = jnp.concatenate(rows, axis=0)

    t0 = pl.multiple_of(i * bs, bs)
    cos_q = cos_ref[pl.ds(t0, bs), :]
    sin_q = sin_ref[pl.ds(t0, bs), :]
    blk = lax.broadcasted_iota(jnp.int32, (km_scr.shape[1], bs), 0)
    causal_bias = jnp.where(lax.broadcasted_iota(jnp.int32, (bs, bs), 0)
                            <= lax.broadcasted_iota(jnp.int32, (bs, bs), 1), 0.0, NEG_INF)

    qbs = []
    for u, (b, h) in enumerate(units):
        qr = _rotate(q_ref[b, :, heads[h]], cos_q, sin_q)
        gate_t = lax.dot_general(km_scr[u], qr, _NT, precision=HIGHEST, preferred_element_type=F32)
        val = jnp.where(blk < i, gate_t, NEG_INF)
        picked = jnp.where(val > 0.5 * NEG_INF, _rank_before(val, nb, 0), float(nb)) < MOBA_TOPK
        sel_scr[u] = jnp.where(picked, 0.0, NEG_INF)
        qbs.append((qr * (HEAD_DIM ** -0.5 * LOG2_E)).astype(BF16))

    def key_tiles(k0, n_blocks):
        return ([kr_scr[b, pl.ds(k0, n_blocks * bs), heads[h]] for b, h in units],
                [vt_scr[u, :, pl.ds(k0, n_blocks * bs)] for u in range(len(units))])

    init = tuple(_flash_steps(qbs, *key_tiles(t0, 1), [causal_bias] * len(units), None))

    def past_blocks(j, n_blocks, carry):
        biases = [jnp.concatenate([jnp.broadcast_to(sel_scr[u, pl.ds(j + d, 1), :], (bs, bs))
                                   for d in range(n_blocks)], axis=0) for u in range(len(units))]
        k0 = j * bs if isinstance(j, int) else pl.multiple_of(j * bs, bs)
        return tuple(_flash_steps(qbs, *key_tiles(k0, n_blocks), biases, carry))

    odd = i % 2
    carry = lax.cond(odd == 1, lambda c: past_blocks(0, 1, c), lambda c: c, init)
    fin = lax.fori_loop(0, i // 2, lambda p, c: past_blocks(odd + 2 * p, 2, c), carry)
    for u, (b, h) in enumerate(units):
        o_ref[b, :, heads[h]] = _flash_output(fin[2 * u + 1]).T.astype(o_ref.dtype)


def _moba(z3, cos2, sin2):
    b, s_len, _ = z3.shape
    nb = s_len // MOBA_BLOCK
    q_off = 4 * MLSTM_W // MOBA_W
    rows = MOBA_ROWS if b % MOBA_ROWS == 0 else 1
    n_units = rows * MOBA_HEADS

    def kv(off):
        return pl.BlockSpec((rows, s_len, MOBA_W), lambda bi, i: (bi, 0, off))

    in_specs = [pl.BlockSpec((rows, MOBA_BLOCK, MOBA_W), lambda bi, i: (bi, i, q_off)),
                kv(q_off + 1), kv(q_off + 2),
                pl.BlockSpec((s_len, HEAD_DIM), lambda bi, i: (0, 0)),
                pl.BlockSpec((s_len, HEAD_DIM), lambda bi, i: (0, 0))]
    scratch = [pltpu.VMEM((rows, s_len, MOBA_W), BF16),
               pltpu.VMEM((n_units, VT_ROWS, s_len), BF16),
               pltpu.VMEM((n_units, BF16_SUBLANES, HEAD_DIM), F32),
               pltpu.VMEM((n_units, BF16_SUBLANES, MOBA_BLOCK), F32)]
    return pl.pallas_call(
        _moba_body, grid=(b // rows, nb), in_specs=in_specs,
        out_specs=pl.BlockSpec((rows, MOBA_BLOCK, MOBA_W), lambda bi, i: (bi, i, 0)),
        out_shape=jax.ShapeDtypeStruct((b, s_len, MOBA_W), BF16),
        scratch_shapes=scratch, compiler_params=_cparams(("parallel", "arbitrary")),
        name="moba")(z3, z3, z3, cos2, sin2)


def _gelu_tanh(x):
    return x * (0.5 * (1.0 + jnp.tanh(np.sqrt(2.0 / np.pi) * (x + 0.044715 * (x * x * x)))))


def _nsa_compress_body(kc_ref, vc_ref, pe_ref, wk1_ref, wk2_ref, wv1_ref, wv2_ref, ko_ref, vo_ref):
    n_rows = kc_ref.shape[1] // CMP_STRIDE
    halves = CMP_LEN // CMP_STRIDE
    assert halves == 2

    def compress(x_ref, pe, w1_ref, w2_ref):
        rows = [x_ref[0, pl.ds(l, n_rows, stride=CMP_STRIDE), :] for l in range(CMP_STRIDE)]
        half = CMP_STRIDE * HEAD_DIM
        first = jnp.concatenate([(r + pe[l:l + 1, :]).astype(BF16) for l, r in enumerate(rows)], axis=1)
        second = jnp.concatenate([(r + pe[CMP_STRIDE + l:CMP_STRIDE + l + 1, :]).astype(BF16)
                                  for l, r in enumerate(rows)], axis=1)
        ya = jnp.dot(first, w1_ref[:half, :], preferred_element_type=F32)
        yb = jnp.dot(second, w1_ref[half:, :], preferred_element_type=F32)
        pre = ya + pltpu.roll(yb, n_rows - 1, axis=0)
        return jnp.dot(_gelu_tanh(pre).astype(BF16), w2_ref[...], preferred_element_type=F32)

    ko_ref[0, 0] = compress(kc_ref, pe_ref[0], wk1_ref, wk2_ref)
    vo_ref[0, 0] = compress(vc_ref, pe_ref[1], wv1_ref, wv2_ref)


def _nsa_compress(z3, cmp_pos, wk1, wk2, wv1, wv2):
    b, s_len, _ = z3.shape
    n_rows = s_len // CMP_STRIDE
    kc_off = NSA_W // LANES
    vc_off = kc_off + NSA_GROUPS
    const2 = lambda bi, gi: (0, 0)
    in_specs = [pl.BlockSpec((1, s_len, HEAD_DIM), lambda bi, gi: (bi, 0, kc_off + gi)),
                pl.BlockSpec((1, s_len, HEAD_DIM), lambda bi, gi: (bi, 0, vc_off + gi)),
                pl.BlockSpec(cmp_pos.shape, lambda bi, gi: (0, 0, 0)),
                pl.BlockSpec(wk1.shape, const2), pl.BlockSpec(wk2.shape, const2),
                pl.BlockSpec(wv1.shape, const2), pl.BlockSpec(wv2.shape, const2)]
    out_spec = pl.BlockSpec((1, 1, n_rows, HEAD_DIM), lambda bi, gi: (bi, gi, 0, 0))
    out_sds = jax.ShapeDtypeStruct((b, NSA_GROUPS, n_rows, HEAD_DIM), F32)
    return pl.pallas_call(
        _nsa_compress_body, grid=(b, NSA_GROUPS), in_specs=in_specs,
        out_specs=[out_spec, out_spec], out_shape=[out_sds, out_sds],
        compiler_params=_cparams(("parallel", "parallel")), name="nsa_compress")(
            z3, z3, cmp_pos, wk1, wk2, wv1, wv2)


def _nsa_body(q_ref, ks_ref, vs_ref, kw_ref, vw_ref, kc_ref, vc_ref, gate_ref, cos_ref, sin_ref, ov_ref,
              o_ref, krs, vst, krw, vwt, sel_scr):
    i = pl.program_id(1)
    tq = NSA_TQ
    tk = NSA_TQ
    scale = HEAD_DIM ** -0.5
    n_sel_blk = ks_ref.shape[1] // SEL_BLOCK
    n_cmp = (ks_ref.shape[1] - CMP_LEN) // CMP_STRIDE + 1
    shift = SEL_BLOCK.bit_length() - 1
    assert 1 << shift == SEL_BLOCK and n_sel_blk <= LANES
    blk_per_tile = tk // SEL_BLOCK
    groups = [slice(g * HEAD_DIM, (g + 1) * HEAD_DIM) for g in range(NSA_GROUPS)]

    @pl.when(i == 0)
    def _():
        for g, cols in enumerate(groups):
            krs[:, cols] = _rotate(ks_ref[0, :, cols], cos_ref[...], sin_ref[...]).astype(BF16)
            krw[:, cols] = _rotate(kw_ref[0, :, cols], cos_ref[...], sin_ref[...]).astype(BF16)
            _store_transposed(vst.at[g], vs_ref[0, :, cols])
            _store_transposed(vwt.at[g], vw_ref[0, :, cols])

    t0 = pl.multiple_of(i * tq, tq)
    pos_row = t0 + lax.broadcasted_iota(jnp.int32, (1, tq), 1)
    cos_q = cos_ref[pl.ds(t0, tq), :]
    sin_q = sin_ref[pl.ds(t0, tq), :]

    n_col = lax.broadcasted_iota(jnp.int32, (LANES, 1), 0)
    cmp_end = jnp.where(n_col < n_cmp, n_col * CMP_STRIDE + (CMP_LEN - 1), jnp.iinfo(jnp.int32).max)
    cmp_ok = cmp_end <= pos_row
    blk_i = lax.broadcasted_iota(jnp.int32, (n_sel_blk, tq), 0)
    behind = (pos_row >> shift) - blk_i
    units = [(g, r) for g in range(NSA_GROUPS) for r in range(NSA_REP)]
    qr_heads, o_cmp = [], []
    for g in range(NSA_GROUPS):
        kc = kc_ref[0, g].astype(BF16)
        vc = vc_ref[0, g].astype(BF16)
        p_sum = jnp.zeros((LANES, tq), F32)
        for r in range(NSA_REP):
            head = (g * NSA_REP + r) * HEAD_DIM
            q = q_ref[0, :, head:head + HEAD_DIM] * scale
            qr_heads.append(_rotate(q * LOG2_E, cos_q, sin_q).astype(BF16))
            s = lax.dot_general(kc, q.astype(BF16), _NT, preferred_element_type=F32)
            s = jnp.where(cmp_ok, s, NEG_INF)
            e = jnp.where(cmp_ok, jnp.exp(s - jnp.max(s, axis=0, keepdims=True)), 0.0)
            p = e / jnp.maximum(jnp.sum(e, axis=0, keepdims=True), 1e-30)
            o_cmp.append(lax.dot_general(p.astype(BF16), vc, _TN, preferred_element_type=F32))
            p_sum = p_sum + p
        imp = jnp.dot(ov_ref[...], p_sum, precision=HIGHEST, preferred_element_type=F32)
        val = jnp.where(behind == 0, FORCED_SCORE, jnp.where(behind == 1, FORCED_SCORE, imp))
        val = jnp.where(blk_i == 0, FORCED_SCORE, val)
        val = jnp.where(behind >= 0, val, NEG_INF)
        picked = jnp.where(val > 0.5 * NEG_INF, _rank_before(val, n_sel_blk, 0), float(n_sel_blk)) < SEL_TOPN
        sel_scr[g] = jnp.where(picked, 0.0, NEG_INF)

    def sel_bias(g, k0, n_tiles):
        blk0 = (k0 >> shift) if isinstance(k0, int) else pl.multiple_of(k0 >> shift, blk_per_tile)
        return jnp.concatenate([jnp.broadcast_to(sel_scr[g, pl.ds(blk0 + b, 1), :], (SEL_BLOCK, tq))
                                for b in range(n_tiles * blk_per_tile)], axis=0)

    def sel_tiles(k0, n_tiles, carry):
        biases = [sel_bias(g, k0, n_tiles) for g in range(NSA_GROUPS)]
        return tuple(_flash_steps(qr_heads, [krs[pl.ds(k0, n_tiles * tk), groups[g]] for g, _ in units],
                                  [vst[g, :, pl.ds(k0, n_tiles * tk)] for g, _ in units],
                                  [biases[g] for g, _ in units], carry))

    causal_bias = jnp.where(lax.broadcasted_iota(jnp.int32, (tk, tq), 0)
                            <= lax.broadcasted_iota(jnp.int32, (tk, tq), 1), 0.0, NEG_INF)
    span = WINDOW + tq
    start = pl.multiple_of(jnp.maximum(t0 - WINDOW, 0), tq)
    gap = ((t0 - start) + lax.broadcasted_iota(jnp.int32, (span, tq), 1)
           - lax.broadcasted_iota(jnp.int32, (span, tq), 0))
    win_bias = jnp.where(gap >= 0, jnp.where(gap < WINDOW, 0.0, NEG_INF), NEG_INF)
    diag_biases = [sel_bias(g, t0, 1) + causal_bias for g in range(NSA_GROUPS)]
    n = len(units)
    group = _flash_steps(qr_heads * 2,
                         [krs[pl.ds(t0, tk), groups[g]] for g, _ in units]
                         + [krw[pl.ds(start, span), groups[g]] for g, _ in units],
                         [vst[g, :, pl.ds(t0, tk)] for g, _ in units]
                         + [vwt[g, :, pl.ds(start, span)] for g, _ in units],
                         [diag_biases[g] for g, _ in units] + [win_bias] * n, None)
    win_fin = group[2 * n:]

    odd = i % 2
    carry = lax.cond(odd == 1, lambda c: sel_tiles(0, 1, c), lambda c: c, tuple(group[:2 * n]))
    sel_fin = lax.fori_loop(0, i // 2, lambda p, c: sel_tiles(pl.multiple_of((odd + 2 * p) * tk, tk), 2, c), carry)

    gates = _sigmoid(gate_ref[0])
    gates_t = gates.T
    for u in range(n):
        c0 = NSA_GATES * u
        o_t = (gates_t[c0 + 1:c0 + 2, :] * _flash_output(sel_fin[2 * u + 1])
               + gates_t[c0 + 2:c0 + 3, :] * _flash_output(win_fin[2 * u + 1]))
        o_ref[0, :, u * HEAD_DIM:(u + 1) * HEAD_DIM] = (gates[:, c0:c0 + 1] * o_cmp[u] + o_t.T).astype(o_ref.dtype)


def _nsa_attention(z3, k_cmp, v_cmp, cos2, sin2):
    b, s_len, _ = z3.shape
    assert s_len >= WINDOW + NSA_TQ and s_len % NSA_TQ == 0
    n_cmp = (s_len - CMP_LEN) // CMP_STRIDE + 1
    n_blk = s_len // SEL_BLOCK
    n_rows = k_cmp.shape[2]
    assert n_rows == LANES
    cmp_start = np.arange(n_rows) * CMP_STRIDE
    cmp_end = cmp_start + CMP_LEN - 1
    blk_lo = np.arange(LANES) * SEL_BLOCK
    overlap = ((cmp_start[:, None] <= blk_lo[None, :] + SEL_BLOCK - 1) & (cmp_end[:, None] >= blk_lo[None, :])
               & (np.arange(n_rows)[:, None] < n_cmp)).astype(np.float32)
    overlap_t = np.ascontiguousarray(overlap.T[:n_blk])
    base = NSA_W // NSA_KV_W

    def kv(off):
        return pl.BlockSpec((1, s_len, NSA_KV_W), lambda bi, i: (bi, 0, base + off))

    cmp_spec = pl.BlockSpec((1, NSA_GROUPS, n_rows, HEAD_DIM), lambda bi, i: (bi, 0, 0, 0))
    const2 = lambda bi, i: (0, 0)
    in_specs = [pl.BlockSpec((1, NSA_TQ, NSA_W), lambda bi, i: (bi, i, 0)),
                kv(2), kv(3), kv(4), kv(5), cmp_spec, cmp_spec,
                pl.BlockSpec((1, NSA_TQ, LANES), lambda bi, i: (bi, i, ODD_GATE_BLK)),
                pl.BlockSpec((s_len, HEAD_DIM), const2), pl.BlockSpec((s_len, HEAD_DIM), const2),
                pl.BlockSpec((n_blk, n_rows), const2)]
    vt_shape = (NSA_GROUPS, VT_ROWS, s_len)
    scratch = [pltpu.VMEM((s_len, NSA_KV_W), BF16), pltpu.VMEM(vt_shape, BF16),
               pltpu.VMEM((s_len, NSA_KV_W), BF16), pltpu.VMEM(vt_shape, BF16),
               pltpu.VMEM((NSA_GROUPS, n_blk, NSA_TQ), F32)]
    return pl.pallas_call(
        _nsa_body, grid=(b, s_len // NSA_TQ), in_specs=in_specs,
        out_specs=pl.BlockSpec((1, NSA_TQ, NSA_W), lambda bi, i: (bi, i, 0)),
        out_shape=jax.ShapeDtypeStruct((b, s_len, NSA_W), BF16),
        scratch_shapes=scratch, compiler_params=_cparams(("parallel", "arbitrary")),
        name="nsa_attention")(z3, z3, z3, z3, z3, k_cmp, v_cmp, z3, cos2, sin2, jnp.asarray(overlap_t))


MOE_ROWS = 1024


def _split_bf16(x):
    hi = x.astype(BF16)
    return hi, (x - hi.astype(F32)).astype(BF16)


def _proj_moe_body(*refs, n_act, final_norm):
    res_ref = refs[0]
    a_refs = refs[1:1 + n_act]
    wo_refs = refs[1 + n_act:1 + 2 * n_act]
    (g_ref, wr_ref, br_ref, w1_ref, w3_ref, w2_ref, fg_ref, o_ref,
     xn_scr, cw_scr, hd_scr) = refs[1 + 2 * n_act:]
    j = pl.program_id(1)
    tm = res_ref.shape[0]
    lane = lax.broadcasted_iota(jnp.int32, (tm, LANES), 1).astype(F32)
    neg = float("-inf")

    @pl.when(j == 0)
    def _():
        h = res_ref[...]
        for a_ref, wo_ref in zip(a_refs, wo_refs):
            h = h + jnp.dot(a_ref[...], wo_ref[...], preferred_element_type=F32)
        o_ref[...] = h
        xn = _rms(h, g_ref[...])
        xh, xl = _split_bf16(xn)
        xn_scr[...] = xh
        hi_terms = jnp.dot(xh, wr_ref[...], preferred_element_type=F32)
        logits = (hi_terms[:, :LANES] + (jnp.dot(xl, wr_ref[:, :LANES], preferred_element_type=F32)
                                         + hi_terms[:, LANES:])) + br_ref[...]
        lt = logits.T[:ROUTER_SLOTS]
        slot = lax.broadcasted_iota(jnp.int32, lt.shape, 0).astype(F32)
        is_g = slot < MOE_GROUPS
        gl = jnp.where(is_g, lt, neg)
        g_max = jnp.max(gl, axis=0, keepdims=True)
        g_w = 1.0 / jnp.sum(jnp.where(is_g, jnp.exp(gl - g_max), 0.0), axis=0, keepdims=True)
        g_top = jnp.min(jnp.where(gl == g_max, slot, float(LANES)), axis=0, keepdims=True)
        lo = MOE_GROUPS + MOE_EPG * g_top
        el = jnp.where(slot >= lo, jnp.where(slot < lo + MOE_EPG, lt, neg), neg)
        v1 = jnp.max(el, axis=0, keepdims=True)
        i1 = jnp.min(jnp.where(el == v1, slot, float(LANES)), axis=0, keepdims=True)
        el2 = jnp.where(slot == i1, neg, el)
        v2 = jnp.max(el2, axis=0, keepdims=True)
        i2 = jnp.min(jnp.where(el2 == v2, slot, float(LANES)), axis=0, keepdims=True)
        e2 = jnp.exp(v2 - v1)
        den = 1.0 + e2
        cw_t = jnp.where(slot == i1, g_w / den, 0.0) + jnp.where(slot == i2, g_w * e2 / den, 0.0)
        cw_scr[...] = jnp.concatenate([cw_t, jnp.zeros((LANES - ROUTER_SLOTS, tm), F32)], axis=0).T

    xn = xn_scr[...]
    cw = cw_scr[...]
    for r in range(MOE_EPG):
        h1 = jnp.dot(xn, w1_ref[r].astype(BF16), preferred_element_type=F32)
        h3 = jnp.dot(xn, w3_ref[r], preferred_element_type=F32)
        e_lane = (MOE_GROUPS + MOE_EPG * j + r).astype(F32)
        col = jnp.sum(jnp.where(lane == e_lane, cw, 0.0), axis=1, keepdims=True)
        hd_scr[:, r * MOE_HIDDEN:(r + 1) * MOE_HIDDEN] = ((h1 * _sigmoid(h1)) * h3 * col).astype(BF16)
    o_ref[...] += jnp.dot(hd_scr[...], w2_ref[...].astype(BF16), preferred_element_type=F32)

    if final_norm:
        @pl.when(j == pl.num_programs(1) - 1)
        def _():
            o_ref[...] = _rms(o_ref[...], fg_ref[...])


def _proj_moe(res2d, acts, w_out, gain, w_g, b_g, w_e, b_e, w1, w3, w2, final_gain, final_norm):
    t, d = res2d.shape
    tm = MOE_ROWS
    n_act = len(acts)
    n_slots = MOE_GROUPS + MOE_EXPERTS
    wr = jnp.pad(jnp.concatenate([w_g, w_e], axis=1), ((0, 0), (0, LANES - n_slots)))
    wr_cat = jnp.concatenate(_split_bf16(wr), axis=1)
    br = jnp.pad(jnp.concatenate([b_g, b_e]), (0, LANES - n_slots)).reshape(1, LANES)
    tile = lambda i, j: (i, 0)
    const = lambda i, j: (0, 0)
    group_hidden = MOE_EPG * MOE_HIDDEN
    in_specs = [pl.BlockSpec((tm, d), tile)]
    in_specs += [pl.BlockSpec((tm, a.shape[1]), tile) for a in acts]
    row0 = np.cumsum([0] + [a.shape[1] for a in acts])
    assert all(r % a.shape[1] == 0 for r, a in zip(row0, acts)) and row0[-1] == w_out.shape[0]
    in_specs += [pl.BlockSpec((a.shape[1], d), functools.partial(lambda i, j, blk: (blk, 0), blk=int(r // a.shape[1])))
                 for r, a in zip(row0, acts)]
    in_specs += [pl.BlockSpec((1, d), const),
                 pl.BlockSpec((d, 2 * LANES), const),
                 pl.BlockSpec((1, LANES), const),
                 pl.BlockSpec((MOE_EPG, d, MOE_HIDDEN), lambda i, j: (j, 0, 0)),
                 pl.BlockSpec((MOE_EPG, d, MOE_HIDDEN), lambda i, j: (j, 0, 0)),
                 pl.BlockSpec((group_hidden, d), lambda i, j: (j, 0)),
                 pl.BlockSpec((1, d), const)]
    scratch = [pltpu.VMEM((tm, d), BF16), pltpu.VMEM((tm, LANES), F32), pltpu.VMEM((tm, group_hidden), BF16)]
    return pl.pallas_call(
        functools.partial(_proj_moe_body, n_act=n_act, final_norm=final_norm),
        grid=(t // tm, MOE_GROUPS), in_specs=in_specs,
        out_specs=pl.BlockSpec((tm, d), tile),
        out_shape=jax.ShapeDtypeStruct((t, d), F32),
        scratch_shapes=scratch, compiler_params=_cparams(("parallel", "arbitrary")),
        name="out_proj_moe")(res2d, *acts, *([w_out] * n_act), gain.reshape(1, d), wr_cat, br,
                             w1, w3.astype(BF16), w2.reshape(MOE_EXPERTS * MOE_HIDDEN, d),
                             final_gain.reshape(1, d))


def _rope_tables(s_len):
    half = HEAD_DIM // 2
    inv = ROPE_THETA ** (-(jnp.arange(half, dtype=F32) / half))
    ang = jnp.arange(s_len, dtype=F32)[:, None] * inv[None, :]
    cos, sin = jnp.cos(ang), jnp.sin(ang)
    return jnp.concatenate([cos, cos], axis=-1), jnp.concatenate([-sin, sin], axis=-1)


def _even_weights(w_in):
    a = 4 * MLSTM_W
    pieces = ((0, a, 0), (a + GATE_ROWS, w_in.shape[1], a), (a, a + GATE_ROWS, w_in.shape[1] - GATE_ROWS))
    return w_in.astype(BF16), pieces


def _odd_weights(w_in):
    a = NSA_W + 6 * NSA_KV_W
    n_gates = NSA_HEADS * NSA_GATES
    assert w_in.shape[1] == a + n_gates and n_gates <= LANES
    return w_in.astype(BF16)


def kernel(x, mix_norm_0, w_in_0, mlstm_conv_0, mlstm_gate_b_0, mlstm_head_norm_0, w_out_0, ffn_norm_0, router_group_0, router_group_b_0, router_expert_0, router_expert_b_0, moe_w1_0, moe_w3_0, moe_w2_0, mix_norm_1, w_in_1, nsa_cmp_pos_1, nsa_cmp_k1_1, nsa_cmp_k2_1, nsa_cmp_v1_1, nsa_cmp_v2_1, w_out_1, ffn_norm_1, router_group_1, router_group_b_1, router_expert_1, router_expert_b_1, moe_w1_1, moe_w3_1, moe_w2_1, final_norm):
    b, s_len, d = x.shape
    t = b * s_len
    cos2, sin2 = _rope_tables(s_len)
    x2d = x.reshape(t, d)

    w_main, pieces = _even_weights(w_in_0)
    z0, gt = _norm_matmul(x2d, mix_norm_0, w_main, pieces, t_col=EVEN_IF_BLK * LANES)
    z0 = z0.reshape(b, s_len, EVEN_N)
    h_m = _mlstm(z0, gt, mlstm_conv_0, mlstm_gate_b_0, mlstm_head_norm_0)
    o_b = _moba(z0, cos2, sin2)
    h = _proj_moe(x2d, [h_m.reshape(t, MLSTM_W), o_b.reshape(t, MOBA_W)], w_out_0.astype(BF16),
                  ffn_norm_0, router_group_0, router_group_b_0, router_expert_0, router_expert_b_0,
                  moe_w1_0, moe_w3_0, moe_w2_0, final_norm, False)

    z1 = _norm_matmul(h, mix_norm_1, _odd_weights(w_in_1)).reshape(b, s_len, ODD_N)
    k_cmp, v_cmp = _nsa_compress(z1, nsa_cmp_pos_1, nsa_cmp_k1_1.astype(BF16), nsa_cmp_k2_1.astype(BF16),
                                 nsa_cmp_v1_1.astype(BF16), nsa_cmp_v2_1.astype(BF16))
    o = _nsa_attention(z1, k_cmp, v_cmp, cos2, sin2)
    h = _proj_moe(h, [o.reshape(t, NSA_W)], w_out_1.astype(BF16),
                  ffn_norm_1, router_group_1, router_group_b_1, router_expert_1, router_expert_b_1,
                  moe_w1_1, moe_w3_1, moe_w2_1, final_norm, True)
    return h.reshape(b, s_len, d)
```

```python
import functools

import numpy as np
import jax
import jax.numpy as jnp
from jax import lax
from jax.experimental import pallas as pl
from jax.experimental.pallas import tpu as pltpu

F32 = jnp.float32
BF16 = jnp.bfloat16
HIGHEST = lax.Precision.HIGHEST

LANES = 128
BF16_SUBLANES = 16
D_MODEL = 1024
HEAD_DIM = 128
ROPE_THETA = 10000.0
NORM_EPS = 1e-6
NEG_INF = -1e30
FORCED_SCORE = 1e4

MLSTM_HEADS = 4
MLSTM_W = MLSTM_HEADS * HEAD_DIM
MLSTM_CHUNK = 64
MLSTM_CONV = 4
MLSTM_GATE_CAP = 15.0
MOBA_HEADS = 4
MOBA_W = MOBA_HEADS * HEAD_DIM
MOBA_BLOCK = 256
MOBA_TOPK = 3

NSA_HEADS = 8
NSA_GROUPS = 2
NSA_REP = NSA_HEADS // NSA_GROUPS
NSA_W = NSA_HEADS * HEAD_DIM
NSA_KV_W = NSA_GROUPS * HEAD_DIM
CMP_LEN = 32
CMP_STRIDE = 16
SEL_BLOCK = 64
SEL_TOPN = 8
WINDOW = 512
NSA_TQ = 256

MOE_GROUPS = 4
MOE_EPG = 4
MOE_EXPERTS = MOE_GROUPS * MOE_EPG
MOE_HIDDEN = D_MODEL // 4
ROUTER_SLOTS = 32

EVEN_N = 4 * MLSTM_W + 3 * MOBA_W + LANES
EVEN_IF_BLK = (4 * MLSTM_W + 3 * MOBA_W) // LANES
ODD_GATE_BLK = (NSA_W + 6 * NSA_KV_W) // LANES
ODD_N = NSA_W + 6 * NSA_KV_W + LANES
NSA_GATES = 3

VMEM_LIMIT = 56 * 1024 * 1024
VT_ROWS = HEAD_DIM + BF16_SUBLANES
LOG2_E = float(np.log2(np.e))

_NT = (((1,), (1,)), ((), ()))
_TN = (((0,), (0,)), ((), ()))


def _cparams(sem):
    return pltpu.CompilerParams(dimension_semantics=sem, vmem_limit_bytes=VMEM_LIMIT)


def _rms(x, g):
    return x * lax.rsqrt(jnp.mean(x * x, axis=-1, keepdims=True) + NORM_EPS) * g


def _sigmoid(x):
    return 0.5 * jnp.tanh(0.5 * x) + 0.5


def _log_sigmoid(x):
    return -(jnp.maximum(-x, 0.0) + jnp.log1p(jnp.exp(-jnp.abs(x))))


def _rotate(x, cos2, sin2):
    return x * cos2 + pltpu.roll(x, HEAD_DIM // 2, axis=1) * sin2


def _flash_steps(qs, k_ts, vt_ts, biases, carries):
    n = len(qs)
    ss = [lax.dot_general(k_ts[u], qs[u], _NT, preferred_element_type=F32) + biases[u] for u in range(n)]
    ms, es, alphas, pvs = [], [], [], []

    def value_matmul(u):
        pv = jnp.dot(vt_ts[u], es[u], preferred_element_type=F32)
        if carries is not None:
            pv = alphas[u] * carries[2 * u + 1] + pv
        pvs.append(pv)

    for u in range(n):
        m_new = jnp.max(ss[u], axis=0, keepdims=True)
        if carries is not None:
            m_new = jnp.maximum(carries[2 * u], m_new)
            alphas.append(jnp.exp2(carries[2 * u] - m_new))
        ms.append(m_new)
        es.append(jnp.exp2(ss[u] - m_new).astype(BF16))
        if u > 0:
            value_matmul(u - 1)
    value_matmul(n - 1)
    out = []
    for u in range(n):
        out += [ms[u], pvs[u]]
    return out


def _flash_output(acc):
    return acc[:HEAD_DIM] / jnp.maximum(acc[HEAD_DIM:HEAD_DIM + 1], 1e-30)


def _store_transposed(dst_ref, x):
    n_rows, n_cols = x.shape
    for r in range(0, n_rows, LANES):
        dst_ref[:n_cols, r:r + LANES] = x[r:r + LANES, :].T.astype(dst_ref.dtype)
    dst_ref[n_cols:, :] = jnp.ones((dst_ref.shape[0] - n_cols, n_rows), dst_ref.dtype)


def _rank_before(v, n_valid, axis):
    idx = lax.broadcasted_iota(jnp.int32, v.shape, axis)
    rank = jnp.zeros(v.shape, F32)
    for m in range(n_valid):
        vm = v[m:m + 1, :] if axis == 0 else v[:, m:m + 1]
        tie = jnp.where(idx > m, 1.0, 0.0)
        rank = rank + jnp.where(vm > v, 1.0, jnp.where(vm == v, tie, 0.0))
    return rank


IN_PROJ_ROWS = 512
IN_PROJ_COLS = 512
GATE_ROWS = 2 * MLSTM_HEADS


def _norm_matmul_body(*refs, pieces, t_col):
    if t_col is None:
        x_ref, g_ref, w_ref, o_ref = refs
    else:
        x_ref, g_ref, w_ref, o_ref, ot_ref = refs
    yb = _rms(x_ref[...], g_ref[...]).astype(BF16)
    n_done = 0
    for src0, src1, dst0 in pieces:
        assert dst0 == n_done
        for c0 in range(src0, src1, IN_PROJ_COLS):
            c1 = min(src1, c0 + IN_PROJ_COLS)
            o_ref[:, n_done:n_done + c1 - c0] = jnp.dot(yb, w_ref[:, c0:c1], preferred_element_type=F32)
            n_done += c1 - c0
    if o_ref.shape[1] > n_done:
        o_ref[:, n_done:] = jnp.zeros((o_ref.shape[0], o_ref.shape[1] - n_done), F32)
    if t_col is not None:
        ot_ref[...] = o_ref[:, t_col:t_col + LANES].T[:ot_ref.shape[0]]


def _norm_matmul(x2d, gain, w, pieces=None, t_col=None):
    t, d = x2d.shape
    tm = IN_PROJ_ROWS
    n = -(-w.shape[1] // LANES) * LANES
    pieces = pieces or ((0, w.shape[1], 0),)
    in_specs = [pl.BlockSpec((tm, d), lambda i: (i, 0)),
                pl.BlockSpec((1, d), lambda i: (0, 0)),
                pl.BlockSpec(w.shape, lambda i: (0, 0))]
    out_specs = [pl.BlockSpec((tm, n), lambda i: (i, 0))]
    out_shape = [jax.ShapeDtypeStruct((t, n), F32)]
    if t_col is not None:
        out_specs.append(pl.BlockSpec((BF16_SUBLANES, tm), lambda i: (0, i)))
        out_shape.append(jax.ShapeDtypeStruct((BF16_SUBLANES, t), F32))
    outs = pl.pallas_call(
        functools.partial(_norm_matmul_body, pieces=pieces, t_col=t_col),
        grid=(t // tm,), in_specs=in_specs, out_specs=out_specs, out_shape=out_shape,
        compiler_params=_cparams(("parallel",)), name="norm_in_proj")(x2d, gain.reshape(1, d), w)
    return outs if t_col is not None else outs[0]


def _chunk_cumsum(x, axis):
    idx = lax.broadcasted_iota(jnp.int32, x.shape, axis) % MLSTM_CHUNK
    d = 1
    while d < MLSTM_CHUNK:
        x = x + jnp.where(idx >= d, pltpu.roll(x, d, axis=axis), 0.0)
        d *= 2
    return x


MLSTM_HEADS_PER_STEP = 2
MLSTM_GROUP = 16


def _mlstm_body(q_ref, k_ref, v_ref, og_ref, gt_ref, cwq_ref, cwk_ref, brow_ref,
                gain_ref, out_ref, ks, qts, vts, hts, css, brs, lirs):
    hp = pl.program_id(1)
    s_len = q_ref.shape[1]
    n_local = q_ref.shape[2] // HEAD_DIM
    n_chunks = s_len // MLSTM_CHUNK
    L = MLSTM_CHUNK
    per_slab = LANES // L
    row = lax.broadcasted_iota(jnp.int32, (s_len, HEAD_DIM), 0)
    cols = [slice(a * HEAD_DIM, (a + 1) * HEAD_DIM) for a in range(n_local)]

    def conv_silu(x, w):
        acc = x * w[MLSTM_CONV - 1:MLSTM_CONV, :]
        for d in range(1, MLSTM_CONV):
            shifted = jnp.where(row >= d, pltpu.roll(x, d, axis=0), 0.0)
            acc = acc + shifted * w[MLSTM_CONV - 1 - d:MLSTM_CONV - d, :]
        return acc * _sigmoid(acc)

    def store_chunks_transposed(dst, x):
        for p in range(s_len // LANES):
            slab_t = x[p * LANES:(p + 1) * LANES, :].T
            for j in range(per_slab):
                dst[per_slab * p + j] = slab_t[:, j * L:(j + 1) * L].astype(dst.dtype)

    pr = gt_ref[...] + brow_ref[...]
    pr = MLSTM_GATE_CAP * jnp.tanh(pr / MLSTM_GATE_CAP)
    sub = lax.broadcasted_iota(jnp.int32, pr.shape, 0)
    b_rows = _chunk_cumsum(_log_sigmoid(pr), 1)

    gains, key_rows = [], []
    for a in range(n_local):
        h = hp * n_local + a
        store_chunks_transposed(qts.at[a], conv_silu(q_ref[0, :, cols[a]], cwq_ref[:, cols[a]]) * (HEAD_DIM ** -0.5))
        store_chunks_transposed(vts.at[a], v_ref[0, :, cols[a]])
        ks[:, cols[a]] = conv_silu(k_ref[0, :, cols[a]], cwk_ref[:, cols[a]]).astype(BF16)
        li_row = jnp.sum(jnp.where(sub == h, pr, 0.0), axis=0, keepdims=True)
        b_row = jnp.sum(jnp.where(sub == h + MLSTM_HEADS, b_rows, 0.0), axis=0, keepdims=True)
        for c in range(n_chunks):
            brs[a, c] = b_row[:, c * L:(c + 1) * L]
            lirs[a, c] = li_row[:, c * L:(c + 1) * L]
        key_rows.append(li_row - b_row)
        gains.append(jnp.broadcast_to(gain_ref[a * HEAD_DIM:(a + 1) * HEAD_DIM, :], (HEAD_DIM, L)))
    key_rows = jnp.concatenate(key_rows + [jnp.zeros((LANES - n_local, s_len), F32)], axis=0)
    for p in range(s_len // LANES):
        slab_t = key_rows[:, p * LANES:(p + 1) * LANES].T
        for a in range(n_local):
            css[a, p * LANES:(p + 1) * LANES, :] = slab_t[:, a:a + 1]

    tri = lax.broadcasted_iota(jnp.int32, (L, L), 0) <= lax.broadcasted_iota(jnp.int32, (L, L), 1)

    group_size = MLSTM_GROUP
    assert n_chunks % group_size == 0

    def group(gi, carry):
        states = [list(carry[3 * a:3 * a + 3]) for a in range(n_local)]
        cs = [gi * group_size + j for j in range(group_size)]
        r0s = [pl.multiple_of(c * L, L) for c in cs]
        units = [(a, j) for a in range(n_local) for j in range(group_size)]
        ks_ = {(a, j): ks[pl.ds(r0s[j], L), cols[a]] for a, j in units}
        q_ts = {(a, j): qts[a, cs[j]] for a, j in units}
        v_ts = {(a, j): vts[a, cs[j]] for a, j in units}
        b_rs = {(a, j): brs[a, cs[j]] for a, j in units}
        kvs, ksums, kqs, g_maxs, b_lasts = {}, {}, {}, {}, {}
        for u in units:
            a, j = u
            b_last = b_rs[u][:, L - 1:L]
            g = b_last - b_rs[u] + lirs[a, cs[j]]
            g_max = jnp.max(g, axis=1, keepdims=True)
            w = jnp.exp(g - g_max)
            kvs[u] = jnp.dot((v_ts[u] * w).astype(BF16), ks_[u], preferred_element_type=F32)
            ksums[u] = jnp.dot(jnp.broadcast_to(w, (BF16_SUBLANES, L)).astype(BF16), ks_[u],
                               preferred_element_type=F32)[:1]
            kqs[u] = jnp.dot(ks_[u], q_ts[u], preferred_element_type=F32)
            g_maxs[u] = g_max
            b_lasts[u] = b_last
        c_ins, n_ins, m_ins = {}, {}, {}
        for u in units:
            a, j = u
            c_state, n_state, m_state = states[a]
            c_ins[u] = c_state.astype(BF16)
            n_ins[u] = jnp.broadcast_to(n_state, (BF16_SUBLANES, HEAD_DIM)).astype(BF16)
            m_ins[u] = m_state
            m_new = jnp.maximum(b_lasts[u] + m_state, g_maxs[u])
            sa = jnp.exp(b_lasts[u] + m_state - m_new)
            cc = jnp.exp(g_maxs[u] - m_new)
            states[a] = [sa * c_state + cc * kvs[u], sa * n_state + cc * ksums[u], m_new]
        c_qs = {u: jnp.dot(c_ins[u], q_ts[u], preferred_element_type=F32) for u in units}
        n_qs = {u: jnp.dot(n_ins[u], q_ts[u], preferred_element_type=F32)[:1] for u in units}
        ss, m_ts, w_inters = {}, {}, {}
        for u in units:
            a, j = u
            dmat = jnp.where(tri, b_rs[u] + css[a, pl.ds(r0s[j], L), :], NEG_INF)
            inter = b_rs[u] + m_ins[u]
            m_t = jnp.maximum(inter, jnp.max(dmat, axis=0, keepdims=True))
            ss[u] = kqs[u] * jnp.exp(dmat - m_t)
            m_ts[u] = m_t
            w_inters[u] = jnp.exp(inter - m_t)
        svs = {u: jnp.dot(v_ts[u].astype(BF16), ss[u].astype(BF16), preferred_element_type=F32) for u in units}
        for u in units:
            a, j = u
            num = svs[u] + w_inters[u] * c_qs[u]
            den = jnp.sum(ss[u], axis=0, keepdims=True) + w_inters[u] * n_qs[u]
            ht = num / jnp.maximum(jnp.abs(den), jnp.exp(-m_ts[u]))
            hts[a, cs[j]] = ht * lax.rsqrt(jnp.mean(ht * ht, axis=0, keepdims=True) + NORM_EPS) * gains[a]
        return tuple(x for st in states for x in st)

    init = (jnp.zeros((HEAD_DIM, HEAD_DIM), F32), jnp.zeros((1, HEAD_DIM), F32), jnp.zeros((1, 1), F32)) * n_local
    lax.fori_loop(0, n_chunks // group_size, group, init)

    for a in range(n_local):
        for p in range(s_len // LANES):
            slab_t = jnp.concatenate([hts[a, per_slab * p + j] for j in range(per_slab)], axis=1)
            rows = slice(p * LANES, (p + 1) * LANES)
            out_ref[0, rows, cols[a]] = (slab_t.T * _sigmoid(og_ref[0, rows, cols[a]])).astype(out_ref.dtype)


def _mlstm(z3, gt, conv_w, gate_b, head_gain):
    b, s_len, _ = z3.shape
    n_chunks = s_len // MLSTM_CHUNK
    gr = gt.shape[0]
    brow = jnp.pad(gate_b, (0, gr - 2 * MLSTM_HEADS)).reshape(gr, 1)
    n_local = MLSTM_HEADS_PER_STEP
    n_steps = MLSTM_HEADS // n_local
    width = n_local * HEAD_DIM

    def col(off):
        return pl.BlockSpec((1, s_len, width), lambda bi, hi: (bi, 0, off * n_steps + hi))

    in_specs = [col(0), col(1), col(2), col(3),
                pl.BlockSpec((gr, s_len), lambda bi, hi: (0, bi)),
                pl.BlockSpec((MLSTM_CONV, width), lambda bi, hi: (0, hi)),
                pl.BlockSpec((MLSTM_CONV, width), lambda bi, hi: (0, n_steps + hi)),
                pl.BlockSpec((gr, 1), lambda bi, hi: (0, 0)),
                pl.BlockSpec((width, 1), lambda bi, hi: (hi, 0))]
    chunk_t = (n_local, n_chunks, HEAD_DIM, MLSTM_CHUNK)
    rows_t = (n_local, n_chunks, 1, MLSTM_CHUNK)
    scratch = [pltpu.VMEM((s_len, width), BF16),
               pltpu.VMEM(chunk_t, BF16), pltpu.VMEM(chunk_t, F32), pltpu.VMEM(chunk_t, F32),
               pltpu.VMEM((n_local, s_len, 1), F32),
               pltpu.VMEM(rows_t, F32), pltpu.VMEM(rows_t, F32)]
    return pl.pallas_call(
        _mlstm_body, grid=(b, n_steps), in_specs=in_specs,
        out_specs=pl.BlockSpec((1, s_len, width), lambda bi, hi: (bi, 0, hi)),
        out_shape=jax.ShapeDtypeStruct((b, s_len, MLSTM_W), BF16),
        scratch_shapes=scratch, compiler_params=_cparams(("parallel", "parallel")),
        name="mlstm")(z3, z3, z3, z3, gt, conv_w, conv_w, brow, head_gain.reshape(MLSTM_W, 1))


MOBA_ROWS = 2


def _moba_body(q_ref, k_ref, v_ref, cos_ref, sin_ref, o_ref, kr_scr, vt_scr, km_scr, sel_scr):
    i = pl.program_id(1)
    n_rows, s_len = k_ref.shape[0], k_ref.shape[1]
    bs = MOBA_BLOCK
    nb = s_len // bs
    heads = [slice(h * HEAD_DIM, (h + 1) * HEAD_DIM) for h in range(MOBA_HEADS)]
    units = [(b, h) for b in range(n_rows) for h in range(MOBA_HEADS)]

    @pl.when(i == 0)
    def _():
        for u, (b, h) in enumerate(units):
            _store_transposed(vt_scr.at[u], v_ref[b, :, heads[h]])
            kr = _rotate(k_ref[b, :, heads[h]], cos_ref[...], sin_ref[...])
            kr_scr[b, :, heads[h]] = kr.astype(BF16)
            rows = [jnp.sum(kr[n * bs:(n + 1) * bs, :], axis=0, keepdims=True) / float(bs) for n in range(nb)]
            rows.append(jnp.zeros((km_scr.shape[1] - nb, HEAD_DIM), F32))
            km_scr[u] = jnp.concatenate(rows, axis=0)

    t0 = pl.multiple_of(i * bs, bs)
    cos_q = cos_ref[pl.ds(t0, bs), :]
    sin_q = sin_ref[pl.ds(t0, bs), :]
    blk = lax.broadcasted_iota(jnp.int32, (km_scr.shape[1], bs), 0)
    causal_bias = jnp.where(lax.broadcasted_iota(jnp.int32, (bs, bs), 0)
                            <= lax.broadcasted_iota(jnp.int32, (bs, bs), 1), 0.0, NEG_INF)

    qbs = []
    for u, (b, h) in enumerate(units):
        qr = _rotate(q_ref[b, :, heads[h]], cos_q, sin_q)
        gate_t = lax.dot_general(km_scr[u], qr, _NT, precision=HIGHEST, preferred_element_type=F32)
        val = jnp.where(blk < i, gate_t, NEG_INF)
        picked = jnp.where(val > 0.5 * NEG_INF, _rank_before(val, nb, 0), float(nb)) < MOBA_TOPK
        sel_scr[u] = jnp.where(picked, 0.0, NEG_INF)
        qbs.append((qr * (HEAD_DIM ** -0.5 * LOG2_E)).astype(BF16))

    def key_tiles(k0, n_blocks):
        return ([kr_scr[b, pl.ds(k0, n_blocks * bs), heads[h]] for b, h in units],
                [vt_scr[u, :, pl.ds(k0, n_blocks * bs)] for u in range(len(units))])

    init = tuple(_flash_steps(qbs, *key_tiles(t0, 1), [causal_bias] * len(units), None))

    def past_blocks(j, n_blocks, carry):
        biases = [jnp.concatenate([jnp.broadcast_to(sel_scr[u, pl.ds(j + d, 1), :], (bs, bs))
                                   for d in range(n_blocks)], axis=0) for u in range(len(units))]
        k0 = j * bs if isinstance(j, int) else pl.multiple_of(j * bs, bs)
        return tuple(_flash_steps(qbs, *key_tiles(k0, n_blocks), biases, carry))

    odd = i % 2
    carry = lax.cond(odd == 1, lambda c: past_blocks(0, 1, c), lambda c: c, init)
    fin = lax.fori_loop(0, i // 2, lambda p, c: past_blocks(odd + 2 * p, 2, c), carry)
    for u, (b, h) in enumerate(units):
        o_ref[b, :, heads[h]] = _flash_output(fin[2 * u + 1]).T.astype(o_ref.dtype)


def _moba(z3, cos2, sin2):
    b, s_len, _ = z3.shape
    nb = s_len // MOBA_BLOCK
    q_off = 4 * MLSTM_W // MOBA_W
    rows = MOBA_ROWS if b % MOBA_ROWS == 0 else 1
    n_units = rows * MOBA_HEADS

    def kv(off):
        return pl.BlockSpec((rows, s_len, MOBA_W), lambda bi, i: (bi, 0, off))

    in_specs = [pl.BlockSpec((rows, MOBA_BLOCK, MOBA_W), lambda bi, i: (bi, i, q_off)),
                kv(q_off + 1), kv(q_off + 2),
                pl.BlockSpec((s_len, HEAD_DIM), lambda bi, i: (0, 0)),
                pl.BlockSpec((s_len, HEAD_DIM), lambda bi, i: (0, 0))]
    scratch = [pltpu.VMEM((rows, s_len, MOBA_W), BF16),
               pltpu.VMEM((n_units, VT_ROWS, s_len), BF16),
               pltpu.VMEM((n_units, BF16_SUBLANES, HEAD_DIM), F32),
               pltpu.VMEM((n_units, BF16_SUBLANES, MOBA_BLOCK), F32)]
    return pl.pallas_call(
        _moba_body, grid=(b // rows, nb), in_specs=in_specs,
        out_specs=pl.BlockSpec((rows, MOBA_BLOCK, MOBA_W), lambda bi, i: (bi, i, 0)),
        out_shape=jax.ShapeDtypeStruct((b, s_len, MOBA_W), BF16),
        scratch_shapes=scratch, compiler_params=_cparams(("parallel", "arbitrary")),
        name="moba")(z3, z3, z3, cos2, sin2)


def _gelu_tanh(x):
    return x * (0.5 * (1.0 + jnp.tanh(np.sqrt(2.0 / np.pi) * (x + 0.044715 * (x * x * x)))))


def _nsa_compress_body(kc_ref, vc_ref, pe_ref, wk1_ref, wk2_ref, wv1_ref, wv2_ref, ko_ref, vo_ref):
    n_rows = kc_ref.shape[1] // CMP_STRIDE
    halves = CMP_LEN // CMP_STRIDE
    assert halves == 2

    def compress(x_ref, pe, w1_ref, w2_ref):
        rows = [x_ref[0, pl.ds(l, n_rows, stride=CMP_STRIDE), :] for l in range(CMP_STRIDE)]
        half = CMP_STRIDE * HEAD_DIM
        first = jnp.concatenate([(r + pe[l:l + 1, :]).astype(BF16) for l, r in enumerate(rows)], axis=1)
        second = jnp.concatenate([(r + pe[CMP_STRIDE + l:CMP_STRIDE + l + 1, :]).astype(BF16)
                                  for l, r in enumerate(rows)], axis=1)
        ya = jnp.dot(first, w1_ref[:half, :], preferred_element_type=F32)
        yb = jnp.dot(second, w1_ref[half:, :], preferred_element_type=F32)
        pre = ya + pltpu.roll(yb, n_rows - 1, axis=0)
        return jnp.dot(_gelu_tanh(pre).astype(BF16), w2_ref[...], preferred_element_type=F32)

    ko_ref[0, 0] = compress(kc_ref, pe_ref[0], wk1_ref, wk2_ref)
    vo_ref[0, 0] = compress(vc_ref, pe_ref[1], wv1_ref, wv2_ref)


def _nsa_compress(z3, cmp_pos, wk1, wk2, wv1, wv2):
    b, s_len, _ = z3.shape
    n_rows = s_len // CMP_STRIDE
    kc_off = NSA_W // LANES
    vc_off = kc_off + NSA_GROUPS
    const2 = lambda bi, gi: (0, 0)
    in_specs = [pl.BlockSpec((1, s_len, HEAD_DIM), lambda bi, gi: (bi, 0, kc_off + gi)),
                pl.BlockSpec((1, s_len, HEAD_DIM), lambda bi, gi: (bi, 0, vc_off + gi)),
                pl.BlockSpec(cmp_pos.shape, lambda bi, gi: (0, 0, 0)),
                pl.BlockSpec(wk1.shape, const2), pl.BlockSpec(wk2.shape, const2),
                pl.BlockSpec(wv1.shape, const2), pl.BlockSpec(wv2.shape, const2)]
    out_spec = pl.BlockSpec((1, 1, n_rows, HEAD_DIM), lambda bi, gi: (bi, gi, 0, 0))
    out_sds = jax.ShapeDtypeStruct((b, NSA_GROUPS, n_rows, HEAD_DIM), F32)
    return pl.pallas_call(
        _nsa_compress_body, grid=(b, NSA_GROUPS), in_specs=in_specs,
        out_specs=[out_spec, out_spec], out_shape=[out_sds, out_sds],
        compiler_params=_cparams(("parallel", "parallel")), name="nsa_compress")(
            z3, z3, cmp_pos, wk1, wk2, wv1, wv2)


def _nsa_body(q_ref, ks_ref, vs_ref, kw_ref, vw_ref, kc_ref, vc_ref, gate_ref, cos_ref, sin_ref, ov_ref,
              o_ref, krs, vst, krw, vwt, sel_scr):
    i = pl.program_id(1)
    tq = NSA_TQ
    tk = NSA_TQ
    scale = HEAD_DIM ** -0.5
    n_sel_blk = ks_ref.shape[1] // SEL_BLOCK
    n_cmp = (ks_ref.shape[1] - CMP_LEN) // CMP_STRIDE + 1
    shift = SEL_BLOCK.bit_length() - 1
    assert 1 << shift == SEL_BLOCK and n_sel_blk <= LANES
    blk_per_tile = tk // SEL_BLOCK
    groups = [slice(g * HEAD_DIM, (g + 1) * HEAD_DIM) for g in range(NSA_GROUPS)]

    @pl.when(i == 0)
    def _():
        for g, cols in enumerate(groups):
            krs[:, cols] = _rotate(ks_ref[0, :, cols], cos_ref[...], sin_ref[...]).astype(BF16)
            krw[:, cols] = _rotate(kw_ref[0, :, cols], cos_ref[...], sin_ref[...]).astype(BF16)
            _store_transposed(vst.at[g], vs_ref[0, :, cols])
            _store_transposed(vwt.at[g], vw_ref[0, :, cols])

    t0 = pl.multiple_of(i * tq, tq)
    pos_row = t0 + lax.broadcasted_iota(jnp.int32, (1, tq), 1)
    cos_q = cos_ref[pl.ds(t0, tq), :]
    sin_q = sin_ref[pl.ds(t0, tq), :]

    n_col = lax.broadcasted_iota(jnp.int32, (LANES, 1), 0)
    cmp_end = jnp.where(n_col < n_cmp, n_col * CMP_STRIDE + (CMP_LEN - 1), jnp.iinfo(jnp.int32).max)
    cmp_ok = cmp_end <= pos_row
    blk_i = lax.broadcasted_iota(jnp.int32, (n_sel_blk, tq), 0)
    behind = (pos_row >> shift) - blk_i
    units = [(g, r) for g in range(NSA_GROUPS) for r in range(NSA_REP)]
    qr_heads, o_cmp = [], []
    for g in range(NSA_GROUPS):
        kc = kc_ref[0, g].astype(BF16)
        vc = vc_ref[0, g].astype(BF16)
        p_sum = jnp.zeros((LANES, tq), F32)
        for r in range(NSA_REP):
            head = (g * NSA_REP + r) * HEAD_DIM
            q = q_ref[0, :, head:head + HEAD_DIM] * scale
            qr_heads.append(_rotate(q * LOG2_E, cos_q, sin_q).astype(BF16))
            s = lax.dot_general(kc, q.astype(BF16), _NT, preferred_element_type=F32)
            s = jnp.where(cmp_ok, s, NEG_INF)
            e = jnp.where(cmp_ok, jnp.exp(s - jnp.max(s, axis=0, keepdims=True)), 0.0)
            p = e / jnp.maximum(jnp.sum(e, axis=0, keepdims=True), 1e-30)
            o_cmp.append(lax.dot_general(p.astype(BF16), vc, _TN, preferred_element_type=F32))
            p_sum = p_sum + p
        imp = jnp.dot(ov_ref[...], p_sum, precision=HIGHEST, preferred_element_type=F32)
        val = jnp.where(behind == 0, FORCED_SCORE, jnp.where(behind == 1, FORCED_SCORE, imp))
        val = jnp.where(blk_i == 0, FORCED_SCORE, val)
        val = jnp.where(behind >= 0, val, NEG_INF)
        picked = jnp.where(val > 0.5 * NEG_INF, _rank_before(val, n_sel_blk, 0), float(n_sel_blk)) < SEL_TOPN
        sel_scr[g] = jnp.where(picked, 0.0, NEG_INF)

    def sel_bias(g, k0, n_tiles):
        blk0 = (k0 >> shift) if isinstance(k0, int) else pl.multiple_of(k0 >> shift, blk_per_tile)
        return jnp.concatenate([jnp.broadcast_to(sel_scr[g, pl.ds(blk0 + b, 1), :], (SEL_BLOCK, tq))
                                for b in range(n_tiles * blk_per_tile)], axis=0)

    def sel_tiles(k0, n_tiles, carry):
        biases = [sel_bias(g, k0, n_tiles) for g in range(NSA_GROUPS)]
        return tuple(_flash_steps(qr_heads, [krs[pl.ds(k0, n_tiles * tk), groups[g]] for g, _ in units],
                                  [vst[g, :, pl.ds(k0, n_tiles * tk)] for g, _ in units],
                                  [biases[g] for g, _ in units], carry))

    causal_bias = jnp.where(lax.broadcasted_iota(jnp.int32, (tk, tq), 0)
                            <= lax.broadcasted_iota(jnp.int32, (tk, tq), 1), 0.0, NEG_INF)
    span = WINDOW + tq
    start = pl.multiple_of(jnp.maximum(t0 - WINDOW, 0), tq)
    gap = ((t0 - start) + lax.broadcasted_iota(jnp.int32, (span, tq), 1)
           - lax.broadcasted_iota(jnp.int32, (span, tq), 0))
    win_bias = jnp.where(gap >= 0, jnp.where(gap < WINDOW, 0.0, NEG_INF), NEG_INF)
    diag_biases = [sel_bias(g, t0, 1) + causal_bias for g in range(NSA_GROUPS)]
    n = len(units)
    group = _flash_steps(qr_heads * 2,
                         [krs[pl.ds(t0, tk), groups[g]] for g, _ in units]
                         + [krw[pl.ds(start, span), groups[g]] for g, _ in units],
                         [vst[g, :, pl.ds(t0, tk)] for g, _ in units]
                         + [vwt[g, :, pl.ds(start, span)] for g, _ in units],
                         [diag_biases[g] for g, _ in units] + [win_bias] * n, None)
    win_fin = group[2 * n:]

    odd = i % 2
    carry = lax.cond(odd == 1, lambda c: sel_tiles(0, 1, c), lambda c: c, tuple(group[:2 * n]))
    sel_fin = lax.fori_loop(0, i // 2, lambda p, c: sel_tiles(pl.multiple_of((odd + 2 * p) * tk, tk), 2, c), carry)

    gates = _sigmoid(gate_ref[0])
    gates_t = gates.T
    for u in range(n):
        c0 = NSA_GATES * u
        o_t = (gates_t[c0 + 1:c0 + 2, :] * _flash_output(sel_fin[2 * u + 1])
               + gates_t[c0 + 2:c0 + 3, :] * _flash_output(win_fin[2 * u + 1]))
        o_ref[0, :, u * HEAD_DIM:(u + 1) * HEAD_DIM] = (gates[:, c0:c0 + 1] * o_cmp[u] + o_t.T).astype(o_ref.dtype)


def _nsa_attention(z3, k_cmp, v_cmp, cos2, sin2):
    b, s_len, _ = z3.shape
    assert s_len >= WINDOW + NSA_TQ and s_len % NSA_TQ == 0
    n_cmp = (s_len - CMP_LEN) // CMP_STRIDE + 1
    n_blk = s_len // SEL_BLOCK
    n_rows = k_cmp.shape[2]
    assert n_rows == LANES
    cmp_start = np.arange(n_rows) * CMP_STRIDE
    cmp_end = cmp_start + CMP_LEN - 1
    blk_lo = np.arange(LANES) * SEL_BLOCK
    overlap = ((cmp_start[:, None] <= blk_lo[None, :] + SEL_BLOCK - 1) & (cmp_end[:, None] >= blk_lo[None, :])
               & (np.arange(n_rows)[:, None] < n_cmp)).astype(np.float32)
    overlap_t = np.ascontiguousarray(overlap.T[:n_blk])
    base = NSA_W // NSA_KV_W

    def kv(off):
        return pl.BlockSpec((1, s_len, NSA_KV_W), lambda bi, i: (bi, 0, base + off))

    cmp_spec = pl.BlockSpec((1, NSA_GROUPS, n_rows, HEAD_DIM), lambda bi, i: (bi, 0, 0, 0))
    const2 = lambda bi, i: (0, 0)
    in_specs = [pl.BlockSpec((1, NSA_TQ, NSA_W), lambda bi, i: (bi, i, 0)),
                kv(2), kv(3), kv(4), kv(5), cmp_spec, cmp_spec,
                pl.BlockSpec((1, NSA_TQ, LANES), lambda bi, i: (bi, i, ODD_GATE_BLK)),
                pl.BlockSpec((s_len, HEAD_DIM), const2), pl.BlockSpec((s_len, HEAD_DIM), const2),
                pl.BlockSpec((n_blk, n_rows), const2)]
    vt_shape = (NSA_GROUPS, VT_ROWS, s_len)
    scratch = [pltpu.VMEM((s_len, NSA_KV_W), BF16), pltpu.VMEM(vt_shape, BF16),
               pltpu.VMEM((s_len, NSA_KV_W), BF16), pltpu.VMEM(vt_shape, BF16),
               pltpu.VMEM((NSA_GROUPS, n_blk, NSA_TQ), F32)]
    return pl.pallas_call(
        _nsa_body, grid=(b, s_len // NSA_TQ), in_specs=in_specs,
        out_specs=pl.BlockSpec((1, NSA_TQ, NSA_W), lambda bi, i: (bi, i, 0)),
        out_shape=jax.ShapeDtypeStruct((b, s_len, NSA_W), BF16),
        scratch_shapes=scratch, compiler_params=_cparams(("parallel", "arbitrary")),
        name="nsa_attention")(z3, z3, z3, z3, z3, k_cmp, v_cmp, z3, cos2, sin2, jnp.asarray(overlap_t))


MOE_ROWS = 1024


def _split_bf16(x):
    hi = x.astype(BF16)
    return hi, (x - hi.astype(F32)).astype(BF16)


def _proj_moe_body(*refs, n_act, final_norm):
    res_ref = refs[0]
    a_refs = refs[1:1 + n_act]
    wo_refs = refs[1 + n_act:1 + 2 * n_act]
    (g_ref, wr_ref, br_ref, w1_ref, w3_ref, w2_ref, fg_ref, o_ref,
     xn_scr, cw_scr, hd_scr) = refs[1 + 2 * n_act:]
    j = pl.program_id(1)
    tm = res_ref.shape[0]
    lane = lax.broadcasted_iota(jnp.int32, (tm, LANES), 1).astype(F32)
    neg = float("-inf")

    @pl.when(j == 0)
    def _():
        h = res_ref[...]
        for a_ref, wo_ref in zip(a_refs, wo_refs):
            h = h + jnp.dot(a_ref[...], wo_ref[...], preferred_element_type=F32)
        o_ref[...] = h
        xn = _rms(h, g_ref[...])
        xh, xl = _split_bf16(xn)
        xn_scr[...] = xh
        hi_terms = jnp.dot(xh, wr_ref[...], preferred_element_type=F32)
        logits = (hi_terms[:, :LANES] + (jnp.dot(xl, wr_ref[:, :LANES], preferred_element_type=F32)
                                         + hi_terms[:, LANES:])) + br_ref[...]
        lt = logits.T[:ROUTER_SLOTS]
        slot = lax.broadcasted_iota(jnp.int32, lt.shape, 0).astype(F32)
        is_g = slot < MOE_GROUPS
        gl = jnp.where(is_g, lt, neg)
        g_max = jnp.max(gl, axis=0, keepdims=True)
        g_w = 1.0 / jnp.sum(jnp.where(is_g, jnp.exp(gl - g_max), 0.0), axis=0, keepdims=True)
        g_top = jnp.min(jnp.where(gl == g_max, slot, float(LANES)), axis=0, keepdims=True)
        lo = MOE_GROUPS + MOE_EPG * g_top
        el = jnp.where(slot >= lo, jnp.where(slot < lo + MOE_EPG, lt, neg), neg)
        v1 = jnp.max(el, axis=0, keepdims=True)
        i1 = jnp.min(jnp.where(el == v1, slot, float(LANES)), axis=0, keepdims=True)
        el2 = jnp.where(slot == i1, neg, el)
        v2 = jnp.max(el2, axis=0, keepdims=True)
        i2 = jnp.min(jnp.where(el2 == v2, slot, float(LANES)), axis=0, keepdims=True)
        e2 = jnp.exp(v2 - v1)
        den = 1.0 + e2
        cw_t = jnp.where(slot == i1, g_w / den, 0.0) + jnp.where(slot == i2, g_w * e2 / den, 0.0)
        cw_scr[...] = jnp.concatenate([cw_t, jnp.zeros((LANES - ROUTER_SLOTS, tm), F32)], axis=0).T

    xn = xn_scr[...]
    cw = cw_scr[...]
    for r in range(MOE_EPG):
        h1 = jnp.dot(xn, w1_ref[r].astype(BF16), preferred_element_type=F32)
        h3 = jnp.dot(xn, w3_ref[r], preferred_element_type=F32)
        e_lane = (MOE_GROUPS + MOE_EPG * j + r).astype(F32)
        col = jnp.sum(jnp.where(lane == e_lane, cw, 0.0), axis=1, keepdims=True)
        hd_scr[:, r * MOE_HIDDEN:(r + 1) * MOE_HIDDEN] = ((h1 * _sigmoid(h1)) * h3 * col).astype(BF16)
    o_ref[...] += jnp.dot(hd_scr[...], w2_ref[...].astype(BF16), preferred_element_type=F32)

    if final_norm:
        @pl.when(j == pl.num_programs(1) - 1)
        def _():
            o_ref[...] = _rms(o_ref[...], fg_ref[...])


def _proj_moe(res2d, acts, w_out, gain, w_g, b_g, w_e, b_e, w1, w3, w2, final_gain, final_norm):
    t, d = res2d.shape
    tm = MOE_ROWS
    n_act = len(acts)
    n_slots = MOE_GROUPS + MOE_EXPERTS
    wr = jnp.pad(jnp.concatenate([w_g, w_e], axis=1), ((0, 0), (0, LANES - n_slots)))
    wr_cat = jnp.concatenate(_split_bf16(wr), axis=1)
    br = jnp.pad(jnp.concatenate([b_g, b_e]), (0, LANES - n_slots)).reshape(1, LANES)
    tile = lambda i, j: (i, 0)
    const = lambda i, j: (0, 0)
    group_hidden = MOE_EPG * MOE_HIDDEN
    in_specs = [pl.BlockSpec((tm, d), tile)]
    in_specs += [pl.BlockSpec((tm, a.shape[1]), tile) for a in acts]
    row0 = np.cumsum([0] + [a.shape[1] for a in acts])
    assert all(r % a.shape[1] == 0 for r, a in zip(row0, acts)) and row0[-1] == w_out.shape[0]
    in_specs += [pl.BlockSpec((a.shape[1], d), functools.partial(lambda i, j, blk: (blk, 0), blk=int(r // a.shape[1])))
                 for r, a in zip(row0, acts)]
    in_specs += [pl.BlockSpec((1, d), const),
                 pl.BlockSpec((d, 2 * LANES), const),
                 pl.BlockSpec((1, LANES), const),
                 pl.BlockSpec((MOE_EPG, d, MOE_HIDDEN), lambda i, j: (j, 0, 0)),
                 pl.BlockSpec((MOE_EPG, d, MOE_HIDDEN), lambda i, j: (j, 0, 0)),
                 pl.BlockSpec((group_hidden, d), lambda i, j: (j, 0)),
                 pl.BlockSpec((1, d), const)]
    scratch = [pltpu.VMEM((tm, d), BF16), pltpu.VMEM((tm, LANES), F32), pltpu.VMEM((tm, group_hidden), BF16)]
    return pl.pallas_call(
        functools.partial(_proj_moe_body, n_act=n_act, final_norm=final_norm),
        grid=(t // tm, MOE_GROUPS), in_specs=in_specs,
        out_specs=pl.BlockSpec((tm, d), tile),
        out_shape=jax.ShapeDtypeStruct((t, d), F32),
        scratch_shapes=scratch, compiler_params=_cparams(("parallel", "arbitrary")),
        name="out_proj_moe")(res2d, *acts, *([w_out] * n_act), gain.reshape(1, d), wr_cat, br,
                             w1, w3.astype(BF16), w2.reshape(MOE_EXPERTS * MOE_HIDDEN, d),
                             final_gain.reshape(1, d))


def _rope_tables(s_len):
    half = HEAD_DIM // 2
    inv = ROPE_THETA ** (-(jnp.arange(half, dtype=F32) / half))
    ang = jnp.arange(s_len, dtype=F32)[:, None] * inv[None, :]
    cos, sin = jnp.cos(ang), jnp.sin(ang)
    return jnp.concatenate([cos, cos], axis=-1), jnp.concatenate([-sin, sin], axis=-1)


def _even_weights(w_in):
    a = 4 * MLSTM_W
    pieces = ((0, a, 0), (a + GATE_ROWS, w_in.shape[1], a), (a, a + GATE_ROWS, w_in.shape[1] - GATE_ROWS))
    return w_in.astype(BF16), pieces


def _odd_weights(w_in):
    a = NSA_W + 6 * NSA_KV_W
    n_gates = NSA_HEADS * NSA_GATES
    assert w_in.shape[1] == a + n_gates and n_gates <= LANES
    return w_in.astype(BF16)


def kernel(x, mix_norm_0, w_in_0, mlstm_conv_0, mlstm_gate_b_0, mlstm_head_norm_0, w_out_0, ffn_norm_0, router_group_0, router_group_b_0, router_expert_0, router_expert_b_0, moe_w1_0, moe_w3_0, moe_w2_0, mix_norm_1, w_in_1, nsa_cmp_pos_1, nsa_cmp_k1_1, nsa_cmp_k2_1, nsa_cmp_v1_1, nsa_cmp_v2_1, w_out_1, ffn_norm_1, router_group_1, router_group_b_1, router_expert_1, router_expert_b_1, moe_w1_1, moe_w3_1, moe_w2_1, final_norm):
    b, s_len, d = x.shape
    t = b * s_len
    cos2, sin2 = _rope_tables(s_len)
    x2d = x.reshape(t, d)

    w_main, pieces = _even_weights(w_in_0)
    z0, gt = _norm_matmul(x2d, mix_norm_0, w_main, pieces, t_col=EVEN_IF_BLK * LANES)
    z0 = z0.reshape(b, s_len, EVEN_N)
    h_m = _mlstm(z0, gt, mlstm_conv_0, mlstm_gate_b_0, mlstm_head_norm_0)
    o_b = _moba(z0, cos2, sin2)
    h = _proj_moe(x2d, [h_m.reshape(t, MLSTM_W), o_b.reshape(t, MOBA_W)], w_out_0.astype(BF16),
                  ffn_norm_0, router_group_0, router_group_b_0, router_expert_0, router_expert_b_0,
                  moe_w1_0, moe_w3_0, moe_w2_0, final_norm, False)

    z1 = _norm_matmul(h, mix_norm_1, _odd_weights(w_in_1)).reshape(b, s_len, ODD_N)
    k_cmp, v_cmp = _nsa_compress(z1, nsa_cmp_pos_1, nsa_cmp_k1_1.astype(BF16), nsa_cmp_k2_1.astype(BF16),
                                 nsa_cmp_v1_1.astype(BF16), nsa_cmp_v2_1.astype(BF16))
    o = _nsa_attention(z1, k_cmp, v_cmp, cos2, sin2)
    h = _proj_moe(h, [o.reshape(t, NSA_W)], w_out_1.astype(BF16),
                  ffn_norm_1, router_group_1, router_group_b_1, router_expert_1, router_expert_b_1,
                  moe_w1_1, moe_w3_1, moe_w2_1, final_norm, True)
    return h.reshape(b, s_len, d)
```

```python
import functools

import numpy as np
import jax
import jax.numpy as jnp
from jax import lax
from jax.experimental import pallas as pl
from jax.experimental.pallas import tpu as pltpu

F32 = jnp.float32
BF16 = jnp.bfloat16
HIGHEST = lax.Precision.HIGHEST

LANES = 128
BF16_SUBLANES = 16
D_MODEL = 1024
HEAD_DIM = 128
ROPE_THETA = 10000.0
NORM_EPS = 1e-6
NEG_INF = -1e30
FORCED_SCORE = 1e4

MLSTM_HEADS = 4
MLSTM_W = MLSTM_HEADS * HEAD_DIM
MLSTM_CHUNK = 64
MLSTM_CONV = 4
MLSTM_GATE_CAP = 15.0
MOBA_HEADS = 4
MOBA_W = MOBA_HEADS * HEAD_DIM
MOBA_BLOCK = 256
MOBA_TOPK = 3

NSA_HEADS = 8
NSA_GROUPS = 2
NSA_REP = NSA_HEADS // NSA_GROUPS
NSA_W = NSA_HEADS * HEAD_DIM
NSA_KV_W = NSA_GROUPS * HEAD_DIM
CMP_LEN = 32
CMP_STRIDE = 16
SEL_BLOCK = 64
SEL_TOPN = 8
WINDOW = 512
NSA_TQ = 256

MOE_GROUPS = 4
MOE_EPG = 4
MOE_EXPERTS = MOE_GROUPS * MOE_EPG
MOE_HIDDEN = D_MODEL // 4
ROUTER_SLOTS = 32

EVEN_N = 4 * MLSTM_W + 3 * MOBA_W + LANES
EVEN_IF_BLK = (4 * MLSTM_W + 3 * MOBA_W) // LANES
ODD_GATE_BLK = (NSA_W + 6 * NSA_KV_W) // LANES
ODD_N = NSA_W + 6 * NSA_KV_W + LANES
NSA_GATES = 3

VMEM_LIMIT = 56 * 1024 * 1024
VT_ROWS = HEAD_DIM + BF16_SUBLANES
LOG2_E = float(np.log2(np.e))

_NT = (((1,), (1,)), ((), ()))
_TN = (((0,), (0,)), ((), ()))


def _cparams(sem):
    return pltpu.CompilerParams(dimension_semantics=sem, vmem_limit_bytes=VMEM_LIMIT)


def _rms(x, g):
    return x * lax.rsqrt(jnp.mean(x * x, axis=-1, keepdims=True) + NORM_EPS) * g


def _sigmoid(x):
    return 0.5 * jnp.tanh(0.5 * x) + 0.5


def _log_sigmoid(x):
    return -(jnp.maximum(-x, 0.0) + jnp.log1p(jnp.exp(-jnp.abs(x))))


def _rotate(x, cos2, sin2):
    return x * cos2 + pltpu.roll(x, HEAD_DIM // 2, axis=1) * sin2


def _flash_steps(qs, k_ts, vt_ts, biases, carries):
    n = len(qs)
    ss = [lax.dot_general(k_ts[u], qs[u], _NT, preferred_element_type=F32) + biases[u] for u in range(n)]
    ms, es, alphas, pvs = [], [], [], []

    def value_matmul(u):
        pv = jnp.dot(vt_ts[u], es[u], preferred_element_type=F32)
        if carries is not None:
            pv = alphas[u] * carries[2 * u + 1] + pv
        pvs.append(pv)

    for u in range(n):
        m_new = jnp.max(ss[u], axis=0, keepdims=True)
        if carries is not None:
            m_new = jnp.maximum(carries[2 * u], m_new)
            alphas.append(jnp.exp2(carries[2 * u] - m_new))
        ms.append(m_new)
        es.append(jnp.exp2(ss[u] - m_new).astype(BF16))
        if u > 0:
            value_matmul(u - 1)
    value_matmul(n - 1)
    out = []
    for u in range(n):
        out += [ms[u], pvs[u]]
    return out


def _flash_output(acc):
    return acc[:HEAD_DIM] / jnp.maximum(acc[HEAD_DIM:HEAD_DIM + 1], 1e-30)


def _store_transposed(dst_ref, x):
    n_rows, n_cols = x.shape
    for r in range(0, n_rows, LANES):
        dst_ref[:n_cols, r:r + LANES] = x[r:r + LANES, :].T.astype(dst_ref.dtype)
    dst_ref[n_cols:, :] = jnp.ones((dst_ref.shape[0] - n_cols, n_rows), dst_ref.dtype)


def _rank_before(v, n_valid, axis):
    idx = lax.broadcasted_iota(jnp.int32, v.shape, axis)
    rank = jnp.zeros(v.shape, F32)
    for m in range(n_valid):
        vm = v[m:m + 1, :] if axis == 0 else v[:, m:m + 1]
        tie = jnp.where(idx > m, 1.0, 0.0)
        rank = rank + jnp.where(vm > v, 1.0, jnp.where(vm == v, tie, 0.0))
    return rank


IN_PROJ_ROWS = 512
IN_PROJ_COLS = 512
GATE_ROWS = 2 * MLSTM_HEADS


def _norm_matmul_body(*refs, pieces, t_col):
    if t_col is None:
        x_ref, g_ref, w_ref, o_ref = refs
    else:
        x_ref, g_ref, w_ref, o_ref, ot_ref = refs
    yb = _rms(x_ref[...], g_ref[...]).astype(BF16)
    n_done = 0
    for src0, src1, dst0 in pieces:
        assert dst0 == n_done
        for c0 in range(src0, src1, IN_PROJ_COLS):
            c1 = min(src1, c0 + IN_PROJ_COLS)
            o_ref[:, n_done:n_done + c1 - c0] = jnp.dot(yb, w_ref[:, c0:c1], preferred_element_type=F32)
            n_done += c1 - c0
    if o_ref.shape[1] > n_done:
        o_ref[:, n_done:] = jnp.zeros((o_ref.shape[0], o_ref.shape[1] - n_done), F32)
    if t_col is not None:
        ot_ref[...] = o_ref[:, t_col:t_col + LANES].T[:ot_ref.shape[0]]


def _norm_matmul(x2d, gain, w, pieces=None, t_col=None):
    t, d = x2d.shape
    tm = IN_PROJ_ROWS
    n = -(-w.shape[1] // LANES) * LANES
    pieces = pieces or ((0, w.shape[1], 0),)
    in_specs = [pl.BlockSpec((tm, d), lambda i: (i, 0)),
                pl.BlockSpec((1, d), lambda i: (0, 0)),
                pl.BlockSpec(w.shape, lambda i: (0, 0))]
    out_specs = [pl.BlockSpec((tm, n), lambda i: (i, 0))]
    out_shape = [jax.ShapeDtypeStruct((t, n), F32)]
    if t_col is not None:
        out_specs.append(pl.BlockSpec((BF16_SUBLANES, tm), lambda i: (0, i)))
        out_shape.append(jax.ShapeDtypeStruct((BF16_SUBLANES, t), F32))
    outs = pl.pallas_call(
        functools.partial(_norm_matmul_body, pieces=pieces, t_col=t_col),
        grid=(t // tm,), in_specs=in_specs, out_specs=out_specs, out_shape=out_shape,
        compiler_params=_cparams(("parallel",)), name="norm_in_proj")(x2d, gain.reshape(1, d), w)
    return outs if t_col is not None else outs[0]


def _chunk_cumsum(x, axis):
    idx = lax.broadcasted_iota(jnp.int32, x.shape, axis) % MLSTM_CHUNK
    d = 1
    while d < MLSTM_CHUNK:
        x = x + jnp.where(idx >= d, pltpu.roll(x, d, axis=axis), 0.0)
        d *= 2
    return x


MLSTM_HEADS_PER_STEP = 2
MLSTM_GROUP = 32


def _mlstm_body(q_ref, k_ref, v_ref, og_ref, gt_ref, cwq_ref, cwk_ref, brow_ref,
                gain_ref, out_ref, ks, qts, vts, hts, css, brs, lirs):
    hp = pl.program_id(1)
    s_len = q_ref.shape[1]
    n_local = q_ref.shape[2] // HEAD_DIM
    n_chunks = s_len // MLSTM_CHUNK
    L = MLSTM_CHUNK
    per_slab = LANES // L
    row = lax.broadcasted_iota(jnp.int32, (s_len, HEAD_DIM), 0)
    cols = [slice(a * HEAD_DIM, (a + 1) * HEAD_DIM) for a in range(n_local)]

    def conv_silu(x, w):
        acc = x * w[MLSTM_CONV - 1:MLSTM_CONV, :]
        for d in range(1, MLSTM_CONV):
            shifted = jnp.where(row >= d, pltpu.roll(x, d, axis=0), 0.0)
            acc = acc + shifted * w[MLSTM_CONV - 1 - d:MLSTM_CONV - d, :]
        return acc * _sigmoid(acc)

    def store_chunks_transposed(dst, x):
        for p in range(s_len // LANES):
            slab_t = x[p * LANES:(p + 1) * LANES, :].T
            for j in range(per_slab):
                dst[per_slab * p + j] = slab_t[:, j * L:(j + 1) * L].astype(dst.dtype)

    pr = gt_ref[...] + brow_ref[...]
    pr = MLSTM_GATE_CAP * jnp.tanh(pr / MLSTM_GATE_CAP)
    sub = lax.broadcasted_iota(jnp.int32, pr.shape, 0)
    b_rows = _chunk_cumsum(_log_sigmoid(pr), 1)

    gains, key_rows = [], []
    for a in range(n_local):
        h = hp * n_local + a
        store_chunks_transposed(qts.at[a], conv_silu(q_ref[0, :, cols[a]], cwq_ref[:, cols[a]]) * (HEAD_DIM ** -0.5))
        store_chunks_transposed(vts.at[a], v_ref[0, :, cols[a]])
        ks[:, cols[a]] = conv_silu(k_ref[0, :, cols[a]], cwk_ref[:, cols[a]]).astype(BF16)
        li_row = jnp.sum(jnp.where(sub == h, pr, 0.0), axis=0, keepdims=True)
        b_row = jnp.sum(jnp.where(sub == h + MLSTM_HEADS, b_rows, 0.0), axis=0, keepdims=True)
        for c in range(n_chunks):
            brs[a, c] = b_row[:, c * L:(c + 1) * L]
            lirs[a, c] = li_row[:, c * L:(c + 1) * L]
        key_rows.append(li_row - b_row)
        gains.append(jnp.broadcast_to(gain_ref[a * HEAD_DIM:(a + 1) * HEAD_DIM, :], (HEAD_DIM, L)))
    key_rows = jnp.concatenate(key_rows + [jnp.zeros((LANES - n_local, s_len), F32)], axis=0)
    for p in range(s_len // LANES):
        slab_t = key_rows[:, p * LANES:(p + 1) * LANES].T
        for a in range(n_local):
            css[a, p * LANES:(p + 1) * LANES, :] = slab_t[:, a:a + 1]

    tri = lax.broadcasted_iota(jnp.int32, (L, L), 0) <= lax.broadcasted_iota(jnp.int32, (L, L), 1)

    group_size = MLSTM_GROUP
    assert n_chunks % group_size == 0

    def group(gi, carry):
        states = [list(carry[3 * a:3 * a + 3]) for a in range(n_local)]
        cs = [gi * group_size + j for j in range(group_size)]
        r0s = [pl.multiple_of(c * L, L) for c in cs]
        units = [(a, j) for a in range(n_local) for j in range(group_size)]
        ks_ = {(a, j): ks[pl.ds(r0s[j], L), cols[a]] for a, j in units}
        q_ts = {(a, j): qts[a, cs[j]] for a, j in units}
        v_ts = {(a, j): vts[a, cs[j]] for a, j in units}
        b_rs = {(a, j): brs[a, cs[j]] for a, j in units}
        kvs, ksums, kqs, g_maxs, b_lasts = {}, {}, {}, {}, {}
        for u in units:
            a, j = u
            b_last = b_rs[u][:, L - 1:L]
            g = b_last - b_rs[u] + lirs[a, cs[j]]
            g_max = jnp.max(g, axis=1, keepdims=True)
            w = jnp.exp(g - g_max)
            kvs[u] = jnp.dot((v_ts[u] * w).astype(BF16), ks_[u], preferred_element_type=F32)
            ksums[u] = jnp.dot(jnp.broadcast_to(w, (BF16_SUBLANES, L)).astype(BF16), ks_[u],
                               preferred_element_type=F32)[:1]
            kqs[u] = jnp.dot(ks_[u], q_ts[u], preferred_element_type=F32)
            g_maxs[u] = g_max
            b_lasts[u] = b_last
        c_ins, n_ins, m_ins = {}, {}, {}
        for u in units:
            a, j = u
            c_state, n_state, m_state = states[a]
            c_ins[u] = c_state.astype(BF16)
            n_ins[u] = jnp.broadcast_to(n_state, (BF16_SUBLANES, HEAD_DIM)).astype(BF16)
            m_ins[u] = m_state
            m_new = jnp.maximum(b_lasts[u] + m_state, g_maxs[u])
            sa = jnp.exp(b_lasts[u] + m_state - m_new)
            cc = jnp.exp(g_maxs[u] - m_new)
            states[a] = [sa * c_state + cc * kvs[u], sa * n_state + cc * ksums[u], m_new]
        c_qs = {u: jnp.dot(c_ins[u], q_ts[u], preferred_element_type=F32) for u in units}
        n_qs = {u: jnp.dot(n_ins[u], q_ts[u], preferred_element_type=F32)[:1] for u in units}
        ss, m_ts, w_inters = {}, {}, {}
        for u in units:
            a, j = u
            dmat = jnp.where(tri, b_rs[u] + css[a, pl.ds(r0s[j], L), :], NEG_INF)
            inter = b_rs[u] + m_ins[u]
            m_t = jnp.maximum(inter, jnp.max(dmat, axis=0, keepdims=True))
            ss[u] = kqs[u] * jnp.exp(dmat - m_t)
            m_ts[u] = m_t
            w_inters[u] = jnp.exp(inter - m_t)
        svs = {u: jnp.dot(v_ts[u].astype(BF16), ss[u].astype(BF16), preferred_element_type=F32) for u in units}
        for u in units:
            a, j = u
            num = svs[u] + w_inters[u] * c_qs[u]
            den = jnp.sum(ss[u], axis=0, keepdims=True) + w_inters[u] * n_qs[u]
            ht = num / jnp.maximum(jnp.abs(den), jnp.exp(-m_ts[u]))
            hts[a, cs[j]] = ht * lax.rsqrt(jnp.mean(ht * ht, axis=0, keepdims=True) + NORM_EPS) * gains[a]
        return tuple(x for st in states for x in st)

    init = (jnp.zeros((HEAD_DIM, HEAD_DIM), F32), jnp.zeros((1, HEAD_DIM), F32), jnp.zeros((1, 1), F32)) * n_local
    lax.fori_loop(0, n_chunks // group_size, group, init)

    for a in range(n_local):
        for p in range(s_len // LANES):
            slab_t = jnp.concatenate([hts[a, per_slab * p + j] for j in range(per_slab)], axis=1)
            rows = slice(p * LANES, (p + 1) * LANES)
            out_ref[0, rows, cols[a]] = (slab_t.T * _sigmoid(og_ref[0, rows, cols[a]])).astype(out_ref.dtype)


def _mlstm(z3, gt, conv_w, gate_b, head_gain):
    b, s_len, _ = z3.shape
    n_chunks = s_len // MLSTM_CHUNK
    gr = gt.shape[0]
    brow = jnp.pad(gate_b, (0, gr - 2 * MLSTM_HEADS)).reshape(gr, 1)
    n_local = MLSTM_HEADS_PER_STEP
    n_steps = MLSTM_HEADS // n_local
    width = n_local * HEAD_DIM

    def col(off):
        return pl.BlockSpec((1, s_len, width), lambda bi, hi: (bi, 0, off * n_steps + hi))

    in_specs = [col(0), col(1), col(2), col(3),
                pl.BlockSpec((gr, s_len), lambda bi, hi: (0, bi)),
                pl.BlockSpec((MLSTM_CONV, width), lambda bi, hi: (0, hi)),
                pl.BlockSpec((MLSTM_CONV, width), lambda bi, hi: (0, n_steps + hi)),
                pl.BlockSpec((gr, 1), lambda bi, hi: (0, 0)),
                pl.BlockSpec((width, 1), lambda bi, hi: (hi, 0))]
    chunk_t = (n_local, n_chunks, HEAD_DIM, MLSTM_CHUNK)
    rows_t = (n_local, n_chunks, 1, MLSTM_CHUNK)
    scratch = [pltpu.VMEM((s_len, width), BF16),
               pltpu.VMEM(chunk_t, BF16), pltpu.VMEM(chunk_t, F32), pltpu.VMEM(chunk_t, F32),
               pltpu.VMEM((n_local, s_len, 1), F32),
               pltpu.VMEM(rows_t, F32), pltpu.VMEM(rows_t, F32)]
    return pl.pallas_call(
        _mlstm_body, grid=(b, n_steps), in_specs=in_specs,
        out_specs=pl.BlockSpec((1, s_len, width), lambda bi, hi: (bi, 0, hi)),
        out_shape=jax.ShapeDtypeStruct((b, s_len, MLSTM_W), BF16),
        scratch_shapes=scratch, compiler_params=_cparams(("parallel", "parallel")),
        name="mlstm")(z3, z3, z3, z3, gt, conv_w, conv_w, brow, head_gain.reshape(MLSTM_W, 1))


MOBA_ROWS = 2


def _moba_body(q_ref, k_ref, v_ref, cos_ref, sin_ref, o_ref, kr_scr, vt_scr, km_scr, sel_scr):
    i = pl.program_id(1)
    n_rows, s_len = k_ref.shape[0], k_ref.shape[1]
    bs = MOBA_BLOCK
    nb = s_len // bs
    heads = [slice(h * HEAD_DIM, (h + 1) * HEAD_DIM) for h in range(MOBA_HEADS)]
    units = [(b, h) for b in range(n_rows) for h in range(MOBA_HEADS)]

    @pl.when(i == 0)
    def _():
        for u, (b, h) in enumerate(units):
            _store_transposed(vt_scr.at[u], v_ref[b, :, heads[h]])
            kr = _rotate(k_ref[b, :, heads[h]], cos_ref[...], sin_ref[...])
            kr_scr[b, :, heads[h]] = kr.astype(BF16)
            rows = [jnp.sum(kr[n * bs:(n + 1) * bs, :], axis=0, keepdims=True) / float(bs) for n in range(nb)]
            rows.append(jnp.zeros((km_scr.shape[1] - nb, HEAD_DIM), F32))
            km_scr[u] = jnp.concatenate(rows, axis=0)

    t0 = pl.multiple_of(i * bs, bs)
    cos_q = cos_ref[pl.ds(t0, bs), :]
    sin_q = sin_ref[pl.ds(t0, bs), :]
    blk = lax.broadcasted_iota(jnp.int32, (km_scr.shape[1], bs), 0)
    causal_bias = jnp.where(lax.broadcasted_iota(jnp.int32, (bs, bs), 0)
                            <= lax.broadcasted_iota(jnp.int32, (bs, bs), 1), 0.0, NEG_INF)

    qbs = []
    for u, (b, h) in enumerate(units):
        qr = _rotate(q_ref[b, :, heads[h]], cos_q, sin_q)
        gate_t = lax.dot_general(km_scr[u], qr, _NT, precision=HIGHEST, preferred_element_type=F32)
        val = jnp.where(blk < i, gate_t, NEG_INF)
        picked = jnp.where(val > 0.5 * NEG_INF, _rank_before(val, nb, 0), float(nb)) < MOBA_TOPK
        sel_scr[u] = jnp.where(picked, 0.0, NEG_INF)
        qbs.append((qr * (HEAD_DIM ** -0.5 * LOG2_E)).astype(BF16))

    def key_tiles(k0, n_blocks):
        return ([kr_scr[b, pl.ds(k0, n_blocks * bs), heads[h]] for b, h in units],
                [vt_scr[u, :, pl.ds(k0, n_blocks * bs)] for u in range(len(units))])

    init = tuple(_flash_steps(qbs, *key_tiles(t0, 1), [causal_bias] * len(units), None))

    def past_blocks(j, n_blocks, carry):
        biases = [jnp.concatenate([jnp.broadcast_to(sel_scr[u, pl.ds(j + d, 1), :], (bs, bs))
                                   for d in range(n_blocks)], axis=0) for u in range(len(units))]
        k0 = j * bs if isinstance(j, int) else pl.multiple_of(j * bs, bs)
        return tuple(_flash_steps(qbs, *key_tiles(k0, n_blocks), biases, carry))

    odd = i % 2
    carry = lax.cond(odd == 1, lambda c: past_blocks(0, 1, c), lambda c: c, init)
    fin = lax.fori_loop(0, i // 2, lambda p, c: past_blocks(odd + 2 * p, 2, c), carry)
    for u, (b, h) in enumerate(units):
        o_ref[b, :, heads[h]] = _flash_output(fin[2 * u + 1]).T.astype(o_ref.dtype)


def _moba(z3, cos2, sin2):
    b, s_len, _ = z3.shape
    nb = s_len // MOBA_BLOCK
    q_off = 4 * MLSTM_W // MOBA_W
    rows = MOBA_ROWS if b % MOBA_ROWS == 0 else 1
    n_units = rows * MOBA_HEADS

    def kv(off):
        return pl.BlockSpec((rows, s_len, MOBA_W), lambda bi, i: (bi, 0, off))

    in_specs = [pl.BlockSpec((rows, MOBA_BLOCK, MOBA_W), lambda bi, i: (bi, i, q_off)),
                kv(q_off + 1), kv(q_off + 2),
                pl.BlockSpec((s_len, HEAD_DIM), lambda bi, i: (0, 0)),
                pl.BlockSpec((s_len, HEAD_DIM), lambda bi, i: (0, 0))]
    scratch = [pltpu.VMEM((rows, s_len, MOBA_W), BF16),
               pltpu.VMEM((n_units, VT_ROWS, s_len), BF16),
               pltpu.VMEM((n_units, BF16_SUBLANES, HEAD_DIM), F32),
               pltpu.VMEM((n_units, BF16_SUBLANES, MOBA_BLOCK), F32)]
    return pl.pallas_call(
        _moba_body, grid=(b // rows, nb), in_specs=in_specs,
        out_specs=pl.BlockSpec((rows, MOBA_BLOCK, MOBA_W), lambda bi, i: (bi, i, 0)),
        out_shape=jax.ShapeDtypeStruct((b, s_len, MOBA_W), BF16),
        scratch_shapes=scratch, compiler_params=_cparams(("parallel", "arbitrary")),
        name="moba")(z3, z3, z3, cos2, sin2)


def _gelu_tanh(x):
    return x * (0.5 * (1.0 + jnp.tanh(np.sqrt(2.0 / np.pi) * (x + 0.044715 * (x * x * x)))))


def _nsa_compress_body(kc_ref, vc_ref, pe_ref, wk1_ref, wk2_ref, wv1_ref, wv2_ref, ko_ref, vo_ref):
    n_rows = kc_ref.shape[1] // CMP_STRIDE
    halves = CMP_LEN // CMP_STRIDE
    assert halves == 2

    def compress(x_ref, pe, w1_ref, w2_ref):
        rows = [x_ref[0, pl.ds(l, n_rows, stride=CMP_STRIDE), :] for l in range(CMP_STRIDE)]
        half = CMP_STRIDE * HEAD_DIM
        first = jnp.concatenate([(r + pe[l:l + 1, :]).astype(BF16) for l, r in enumerate(rows)], axis=1)
        second = jnp.concatenate([(r + pe[CMP_STRIDE + l:CMP_STRIDE + l + 1, :]).astype(BF16)
                                  for l, r in enumerate(rows)], axis=1)
        ya = jnp.dot(first, w1_ref[:half, :], preferred_element_type=F32)
        yb = jnp.dot(second, w1_ref[half:, :], preferred_element_type=F32)
        pre = ya + pltpu.roll(yb, n_rows - 1, axis=0)
        return jnp.dot(_gelu_tanh(pre).astype(BF16), w2_ref[...], preferred_element_type=F32)

    ko_ref[0, 0] = compress(kc_ref, pe_ref[0], wk1_ref, wk2_ref)
    vo_ref[0, 0] = compress(vc_ref, pe_ref[1], wv1_ref, wv2_ref)


def _nsa_compress(z3, cmp_pos, wk1, wk2, wv1, wv2):
    b, s_len, _ = z3.shape
    n_rows = s_len // CMP_STRIDE
    kc_off = NSA_W // LANES
    vc_off = kc_off + NSA_GROUPS
    const2 = lambda bi, gi: (0, 0)
    in_specs = [pl.BlockSpec((1, s_len, HEAD_DIM), lambda bi, gi: (bi, 0, kc_off + gi)),
                pl.BlockSpec((1, s_len, HEAD_DIM), lambda bi, gi: (bi, 0, vc_off + gi)),
                pl.BlockSpec(cmp_pos.shape, lambda bi, gi: (0, 0, 0)),
                pl.BlockSpec(wk1.shape, const2), pl.BlockSpec(wk2.shape, const2),
                pl.BlockSpec(wv1.shape, const2), pl.BlockSpec(wv2.shape, const2)]
    out_spec = pl.BlockSpec((1, 1, n_rows, HEAD_DIM), lambda bi, gi: (bi, gi, 0, 0))
    out_sds = jax.ShapeDtypeStruct((b, NSA_GROUPS, n_rows, HEAD_DIM), F32)
    return pl.pallas_call(
        _nsa_compress_body, grid=(b, NSA_GROUPS), in_specs=in_specs,
        out_specs=[out_spec, out_spec], out_shape=[out_sds, out_sds],
        compiler_params=_cparams(("parallel", "parallel")), name="nsa_compress")(
            z3, z3, cmp_pos, wk1, wk2, wv1, wv2)


def _nsa_body(q_ref, ks_ref, vs_ref, kw_ref, vw_ref, kc_ref, vc_ref, gate_ref, cos_ref, sin_ref, ov_ref,
              o_ref, krs, vst, krw, vwt, sel_scr):
    i = pl.program_id(1)
    tq = NSA_TQ
    tk = NSA_TQ
    scale = HEAD_DIM ** -0.5
    n_sel_blk = ks_ref.shape[1] // SEL_BLOCK
    n_cmp = (ks_ref.shape[1] - CMP_LEN) // CMP_STRIDE + 1
    shift = SEL_BLOCK.bit_length() - 1
    assert 1 << shift == SEL_BLOCK and n_sel_blk <= LANES
    blk_per_tile = tk // SEL_BLOCK
    groups = [slice(g * HEAD_DIM, (g + 1) * HEAD_DIM) for g in range(NSA_GROUPS)]

    @pl.when(i == 0)
    def _():
        for g, cols in enumerate(groups):
            krs[:, cols] = _rotate(ks_ref[0, :, cols], cos_ref[...], sin_ref[...]).astype(BF16)
            krw[:, cols] = _rotate(kw_ref[0, :, cols], cos_ref[...], sin_ref[...]).astype(BF16)
            _store_transposed(vst.at[g], vs_ref[0, :, cols])
            _store_transposed(vwt.at[g], vw_ref[0, :, cols])

    t0 = pl.multiple_of(i * tq, tq)
    pos_row = t0 + lax.broadcasted_iota(jnp.int32, (1, tq), 1)
    cos_q = cos_ref[pl.ds(t0, tq), :]
    sin_q = sin_ref[pl.ds(t0, tq), :]

    n_col = lax.broadcasted_iota(jnp.int32, (LANES, 1), 0)
    cmp_end = jnp.where(n_col < n_cmp, n_col * CMP_STRIDE + (CMP_LEN - 1), jnp.iinfo(jnp.int32).max)
    cmp_ok = cmp_end <= pos_row
    blk_i = lax.broadcasted_iota(jnp.int32, (n_sel_blk, tq), 0)
    behind = (pos_row >> shift) - blk_i
    units = [(g, r) for g in range(NSA_GROUPS) for r in range(NSA_REP)]
    qr_heads, o_cmp = [], []
    for g in range(NSA_GROUPS):
        kc = kc_ref[0, g].astype(BF16)
        vc = vc_ref[0, g].astype(BF16)
        p_sum = jnp.zeros((LANES, tq), F32)
        for r in range(NSA_REP):
            head = (g * NSA_REP + r) * HEAD_DIM
            q = q_ref[0, :, head:head + HEAD_DIM] * scale
            qr_heads.append(_rotate(q * LOG2_E, cos_q, sin_q).astype(BF16))
            s = lax.dot_general(kc, q.astype(BF16), _NT, preferred_element_type=F32)
            s = jnp.where(cmp_ok, s, NEG_INF)
            e = jnp.where(cmp_ok, jnp.exp(s - jnp.max(s, axis=0, keepdims=True)), 0.0)
            p = e / jnp.maximum(jnp.sum(e, axis=0, keepdims=True), 1e-30)
            o_cmp.append(lax.dot_general(p.astype(BF16), vc, _TN, preferred_element_type=F32))
            p_sum = p_sum + p
        imp = jnp.dot(ov_ref[...], p_sum, precision=HIGHEST, preferred_element_type=F32)
        val = jnp.where(behind == 0, FORCED_SCORE, jnp.where(behind == 1, FORCED_SCORE, imp))
        val = jnp.where(blk_i == 0, FORCED_SCORE, val)
        val = jnp.where(behind >= 0, val, NEG_INF)
        picked = jnp.where(val > 0.5 * NEG_INF, _rank_before(val, n_sel_blk, 0), float(n_sel_blk)) < SEL_TOPN
        sel_scr[g] = jnp.where(picked, 0.0, NEG_INF)

    def sel_bias(g, k0, n_tiles):
        blk0 = (k0 >> shift) if isinstance(k0, int) else pl.multiple_of(k0 >> shift, blk_per_tile)
        return jnp.concatenate([jnp.broadcast_to(sel_scr[g, pl.ds(blk0 + b, 1), :], (SEL_BLOCK, tq))
                                for b in range(n_tiles * blk_per_tile)], axis=0)

    def sel_tiles(k0, n_tiles, carry):
        biases = [sel_bias(g, k0, n_tiles) for g in range(NSA_GROUPS)]
        return tuple(_flash_steps(qr_heads, [krs[pl.ds(k0, n_tiles * tk), groups[g]] for g, _ in units],
                                  [vst[g, :, pl.ds(k0, n_tiles * tk)] for g, _ in units],
                                  [biases[g] for g, _ in units], carry))

    causal_bias = jnp.where(lax.broadcasted_iota(jnp.int32, (tk, tq), 0)
                            <= lax.broadcasted_iota(jnp.int32, (tk, tq), 1), 0.0, NEG_INF)
    span = WINDOW + tq
    start = pl.multiple_of(jnp.maximum(t0 - WINDOW, 0), tq)
    gap = ((t0 - start) + lax.broadcasted_iota(jnp.int32, (span, tq), 1)
           - lax.broadcasted_iota(jnp.int32, (span, tq), 0))
    win_bias = jnp.where(gap >= 0, jnp.where(gap < WINDOW, 0.0, NEG_INF), NEG_INF)
    diag_biases = [sel_bias(g, t0, 1) + causal_bias for g in range(NSA_GROUPS)]
    n = len(units)
    group = _flash_steps(qr_heads * 2,
                         [krs[pl.ds(t0, tk), groups[g]] for g, _ in units]
                         + [krw[pl.ds(start, span), groups[g]] for g, _ in units],
                         [vst[g, :, pl.ds(t0, tk)] for g, _ in units]
                         + [vwt[g, :, pl.ds(start, span)] for g, _ in units],
                         [diag_biases[g] for g, _ in units] + [win_bias] * n, None)
    win_fin = group[2 * n:]

    odd = i % 2
    carry = lax.cond(odd == 1, lambda c: sel_tiles(0, 1, c), lambda c: c, tuple(group[:2 * n]))
    sel_fin = lax.fori_loop(0, i // 2, lambda p, c: sel_tiles(pl.multiple_of((odd + 2 * p) * tk, tk), 2, c), carry)

    gates = _sigmoid(gate_ref[0])
    gates_t = gates.T
    for u in range(n):
        c0 = NSA_GATES * u
        o_t = (gates_t[c0 + 1:c0 + 2, :] * _flash_output(sel_fin[2 * u + 1])
               + gates_t[c0 + 2:c0 + 3, :] * _flash_output(win_fin[2 * u + 1]))
        o_ref[0, :, u * HEAD_DIM:(u + 1) * HEAD_DIM] = (gates[:, c0:c0 + 1] * o_cmp[u] + o_t.T).astype(o_ref.dtype)


def _nsa_attention(z3, k_cmp, v_cmp, cos2, sin2):
    b, s_len, _ = z3.shape
    assert s_len >= WINDOW + NSA_TQ and s_len % NSA_TQ == 0
    n_cmp = (s_len - CMP_LEN) // CMP_STRIDE + 1
    n_blk = s_len // SEL_BLOCK
    n_rows = k_cmp.shape[2]
    assert n_rows == LANES
    cmp_start = np.arange(n_rows) * CMP_STRIDE
    cmp_end = cmp_start + CMP_LEN - 1
    blk_lo = np.arange(LANES) * SEL_BLOCK
    overlap = ((cmp_start[:, None] <= blk_lo[None, :] + SEL_BLOCK - 1) & (cmp_end[:, None] >= blk_lo[None, :])
               & (np.arange(n_rows)[:, None] < n_cmp)).astype(np.float32)
    overlap_t = np.ascontiguousarray(overlap.T[:n_blk])
    base = NSA_W // NSA_KV_W

    def kv(off):
        return pl.BlockSpec((1, s_len, NSA_KV_W), lambda bi, i: (bi, 0, base + off))

    cmp_spec = pl.BlockSpec((1, NSA_GROUPS, n_rows, HEAD_DIM), lambda bi, i: (bi, 0, 0, 0))
    const2 = lambda bi, i: (0, 0)
    in_specs = [pl.BlockSpec((1, NSA_TQ, NSA_W), lambda bi, i: (bi, i, 0)),
                kv(2), kv(3), kv(4), kv(5), cmp_spec, cmp_spec,
                pl.BlockSpec((1, NSA_TQ, LANES), lambda bi, i: (bi, i, ODD_GATE_BLK)),
                pl.BlockSpec((s_len, HEAD_DIM), const2), pl.BlockSpec((s_len, HEAD_DIM), const2),
                pl.BlockSpec((n_blk, n_rows), const2)]
    vt_shape = (NSA_GROUPS, VT_ROWS, s_len)
    scratch = [pltpu.VMEM((s_len, NSA_KV_W), BF16), pltpu.VMEM(vt_shape, BF16),
               pltpu.VMEM((s_len, NSA_KV_W), BF16), pltpu.VMEM(vt_shape, BF16),
               pltpu.VMEM((NSA_GROUPS, n_blk, NSA_TQ), F32)]
    return pl.pallas_call(
        _nsa_body, grid=(b, s_len // NSA_TQ), in_specs=in_specs,
        out_specs=pl.BlockSpec((1, NSA_TQ, NSA_W), lambda bi, i: (bi, i, 0)),
        out_shape=jax.ShapeDtypeStruct((b, s_len, NSA_W), BF16),
        scratch_shapes=scratch, compiler_params=_cparams(("parallel", "arbitrary")),
        name="nsa_attention")(z3, z3, z3, z3, z3, k_cmp, v_cmp, z3, cos2, sin2, jnp.asarray(overlap_t))


MOE_ROWS = 1024


def _split_bf16(x):
    hi = x.astype(BF16)
    return hi, (x - hi.astype(F32)).astype(BF16)


def _proj_moe_body(*refs, n_act, final_norm):
    res_ref = refs[0]
    a_refs = refs[1:1 + n_act]
    wo_refs = refs[1 + n_act:1 + 2 * n_act]
    (g_ref, wr_ref, br_ref, w1_ref, w3_ref, w2_ref, fg_ref, o_ref,
     xn_scr, cw_scr, hd_scr) = refs[1 + 2 * n_act:]
    j = pl.program_id(1)
    tm = res_ref.shape[0]
    lane = lax.broadcasted_iota(jnp.int32, (tm, LANES), 1).astype(F32)
    neg = float("-inf")

    @pl.when(j == 0)
    def _():
        h = res_ref[...]
        for a_ref, wo_ref in zip(a_refs, wo_refs):
            h = h + jnp.dot(a_ref[...], wo_ref[...], preferred_element_type=F32)
        o_ref[...] = h
        xn = _rms(h, g_ref[...])
        xh, xl = _split_bf16(xn)
        xn_scr[...] = xh
        hi_terms = jnp.dot(xh, wr_ref[...], preferred_element_type=F32)
        logits = (hi_terms[:, :LANES] + (jnp.dot(xl, wr_ref[:, :LANES], preferred_element_type=F32)
                                         + hi_terms[:, LANES:])) + br_ref[...]
        lt = logits.T[:ROUTER_SLOTS]
        slot = lax.broadcasted_iota(jnp.int32, lt.shape, 0).astype(F32)
        is_g = slot < MOE_GROUPS
        gl = jnp.where(is_g, lt, neg)
        g_max = jnp.max(gl, axis=0, keepdims=True)
        g_w = 1.0 / jnp.sum(jnp.where(is_g, jnp.exp(gl - g_max), 0.0), axis=0, keepdims=True)
        g_top = jnp.min(jnp.where(gl == g_max, slot, float(LANES)), axis=0, keepdims=True)
        lo = MOE_GROUPS + MOE_EPG * g_top
        el = jnp.where(slot >= lo, jnp.where(slot < lo + MOE_EPG, lt, neg), neg)
        v1 = jnp.max(el, axis=0, keepdims=True)
        i1 = jnp.min(jnp.where(el == v1, slot, float(LANES)), axis=0, keepdims=True)
        el2 = jnp.where(slot == i1, neg, el)
        v2 = jnp.max(el2, axis=0, keepdims=True)
        i2 = jnp.min(jnp.where(el2 == v2, slot, float(LANES)), axis=0, keepdims=True)
        e2 = jnp.exp(v2 - v1)
        den = 1.0 + e2
        cw_t = jnp.where(slot == i1, g_w / den, 0.0) + jnp.where(slot == i2, g_w * e2 / den, 0.0)
        cw_scr[...] = jnp.concatenate([cw_t, jnp.zeros((LANES - ROUTER_SLOTS, tm), F32)], axis=0).T

    xn = xn_scr[...]
    cw = cw_scr[...]
    for r in range(MOE_EPG):
        h1 = jnp.dot(xn, w1_ref[r].astype(BF16), preferred_element_type=F32)
        h3 = jnp.dot(xn, w3_ref[r], preferred_element_type=F32)
        e_lane = (MOE_GROUPS + MOE_EPG * j + r).astype(F32)
        col = jnp.sum(jnp.where(lane == e_lane, cw, 0.0), axis=1, keepdims=True)
        hd_scr[:, r * MOE_HIDDEN:(r + 1) * MOE_HIDDEN] = ((h1 * _sigmoid(h1)) * h3 * col).astype(BF16)
    o_ref[...] += jnp.dot(hd_scr[...], w2_ref[...].astype(BF16), preferred_element_type=F32)

    if final_norm:
        @pl.when(j == pl.num_programs(1) - 1)
        def _():
            o_ref[...] = _rms(o_ref[...], fg_ref[...])


def _proj_moe(res2d, acts, w_out, gain, w_g, b_g, w_e, b_e, w1, w3, w2, final_gain, final_norm):
    t, d = res2d.shape
    tm = MOE_ROWS
    n_act = len(acts)
    n_slots = MOE_GROUPS + MOE_EXPERTS
    wr = jnp.pad(jnp.concatenate([w_g, w_e], axis=1), ((0, 0), (0, LANES - n_slots)))
    wr_cat = jnp.concatenate(_split_bf16(wr), axis=1)
    br = jnp.pad(jnp.concatenate([b_g, b_e]), (0, LANES - n_slots)).reshape(1, LANES)
    tile = lambda i, j: (i, 0)
    const = lambda i, j: (0, 0)
    group_hidden = MOE_EPG * MOE_HIDDEN
    in_specs = [pl.BlockSpec((tm, d), tile)]
    in_specs += [pl.BlockSpec((tm, a.shape[1]), tile) for a in acts]
    row0 = np.cumsum([0] + [a.shape[1] for a in acts])
    assert all(r % a.shape[1] == 0 for r, a in zip(row0, acts)) and row0[-1] == w_out.shape[0]
    in_specs += [pl.BlockSpec((a.shape[1], d), functools.partial(lambda i, j, blk: (blk, 0), blk=int(r // a.shape[1])))
                 for r, a in zip(row0, acts)]
    in_specs += [pl.BlockSpec((1, d), const),
                 pl.BlockSpec((d, 2 * LANES), const),
                 pl.BlockSpec((1, LANES), const),
                 pl.BlockSpec((MOE_EPG, d, MOE_HIDDEN), lambda i, j: (j, 0, 0)),
                 pl.BlockSpec((MOE_EPG, d, MOE_HIDDEN), lambda i, j: (j, 0, 0)),
                 pl.BlockSpec((group_hidden, d), lambda i, j: (j, 0)),
                 pl.BlockSpec((1, d), const)]
    scratch = [pltpu.VMEM((tm, d), BF16), pltpu.VMEM((tm, LANES), F32), pltpu.VMEM((tm, group_hidden), BF16)]
    return pl.pallas_call(
        functools.partial(_proj_moe_body, n_act=n_act, final_norm=final_norm),
        grid=(t // tm, MOE_GROUPS), in_specs=in_specs,
        out_specs=pl.BlockSpec((tm, d), tile),
        out_shape=jax.ShapeDtypeStruct((t, d), F32),
        scratch_shapes=scratch, compiler_params=_cparams(("parallel", "arbitrary")),
        name="out_proj_moe")(res2d, *acts, *([w_out] * n_act), gain.reshape(1, d), wr_cat, br,
                             w1, w3.astype(BF16), w2.reshape(MOE_EXPERTS * MOE_HIDDEN, d),
                             final_gain.reshape(1, d))


def _rope_tables(s_len):
    half = HEAD_DIM // 2
    inv = ROPE_THETA ** (-(jnp.arange(half, dtype=F32) / half))
    ang = jnp.arange(s_len, dtype=F32)[:, None] * inv[None, :]
    cos, sin = jnp.cos(ang), jnp.sin(ang)
    return jnp.concatenate([cos, cos], axis=-1), jnp.concatenate([-sin, sin], axis=-1)


def _even_weights(w_in):
    a = 4 * MLSTM_W
    pieces = ((0, a, 0), (a + GATE_ROWS, w_in.shape[1], a), (a, a + GATE_ROWS, w_in.shape[1] - GATE_ROWS))
    return w_in.astype(BF16), pieces


def _odd_weights(w_in):
    a = NSA_W + 6 * NSA_KV_W
    n_gates = NSA_HEADS * NSA_GATES
    assert w_in.shape[1] == a + n_gates and n_gates <= LANES
    return w_in.astype(BF16)


def kernel(x, mix_norm_0, w_in_0, mlstm_conv_0, mlstm_gate_b_0, mlstm_head_norm_0, w_out_0, ffn_norm_0, router_group_0, router_group_b_0, router_expert_0, router_expert_b_0, moe_w1_0, moe_w3_0, moe_w2_0, mix_norm_1, w_in_1, nsa_cmp_pos_1, nsa_cmp_k1_1, nsa_cmp_k2_1, nsa_cmp_v1_1, nsa_cmp_v2_1, w_out_1, ffn_norm_1, router_group_1, router_group_b_1, router_expert_1, router_expert_b_1, moe_w1_1, moe_w3_1, moe_w2_1, final_norm):
    b, s_len, d = x.shape
    t = b * s_len
    cos2, sin2 = _rope_tables(s_len)
    x2d = x.reshape(t, d)

    w_main, pieces = _even_weights(w_in_0)
    z0, gt = _norm_matmul(x2d, mix_norm_0, w_main, pieces, t_col=EVEN_IF_BLK * LANES)
    z0 = z0.reshape(b, s_len, EVEN_N)
    h_m = _mlstm(z0, gt, mlstm_conv_0, mlstm_gate_b_0, mlstm_head_norm_0)
    o_b = _moba(z0, cos2, sin2)
    h = _proj_moe(x2d, [h_m.reshape(t, MLSTM_W), o_b.reshape(t, MOBA_W)], w_out_0.astype(BF16),
                  ffn_norm_0, router_group_0, router_group_b_0, router_expert_0, router_expert_b_0,
                  moe_w1_0, moe_w3_0, moe_w2_0, final_norm, False)

    z1 = _norm_matmul(h, mix_norm_1, _odd_weights(w_in_1)).reshape(b, s_len, ODD_N)
    k_cmp, v_cmp = _nsa_compress(z1, nsa_cmp_pos_1, nsa_cmp_k1_1.astype(BF16), nsa_cmp_k2_1.astype(BF16),
                                 nsa_cmp_v1_1.astype(BF16), nsa_cmp_v2_1.astype(BF16))
    o = _nsa_attention(z1, k_cmp, v_cmp, cos2, sin2)
    h = _proj_moe(h, [o.reshape(t, NSA_W)], w_out_1.astype(BF16),
                  ffn_norm_1, router_group_1, router_group_b_1, router_expert_1, router_expert_b_1,
                  moe_w1_1, moe_w3_1, moe_w2_1, final_norm, True)
    return h.reshape(b, s_len, d)
```

```python
import functools

import numpy as np
import jax
import jax.numpy as jnp
from jax import lax
from jax.experimental import pallas as pl
from jax.experimental.pallas import tpu as pltpu

F32 = jnp.float32
BF16 = jnp.bfloat16
HIGHEST = lax.Precision.HIGHEST

LANES = 128
BF16_SUBLANES = 16
D_MODEL = 1024
HEAD_DIM = 128
ROPE_THETA = 10000.0
NORM_EPS = 1e-6
NEG_INF = -1e30
FORCED_SCORE = 1e4

MLSTM_HEADS = 4
MLSTM_W = MLSTM_HEADS * HEAD_DIM
MLSTM_CHUNK = 64
MLSTM_CONV = 4
MLSTM_GATE_CAP = 15.0
MOBA_HEADS = 4
MOBA_W = MOBA_HEADS * HEAD_DIM
MOBA_BLOCK = 256
MOBA_TOPK = 3

NSA_HEADS = 8
NSA_GROUPS = 2
NSA_REP = NSA_HEADS // NSA_GROUPS
NSA_W = NSA_HEADS * HEAD_DIM
NSA_KV_W = NSA_GROUPS * HEAD_DIM
CMP_LEN = 32
CMP_STRIDE = 16
SEL_BLOCK = 64
SEL_TOPN = 8
WINDOW = 512
NSA_TQ = 256

MOE_GROUPS = 4
MOE_EPG = 4
MOE_EXPERTS = MOE_GROUPS * MOE_EPG
MOE_HIDDEN = D_MODEL // 4
ROUTER_SLOTS = 32

EVEN_N = 4 * MLSTM_W + 3 * MOBA_W + LANES
EVEN_IF_BLK = (4 * MLSTM_W + 3 * MOBA_W) // LANES
ODD_GATE_BLK = (NSA_W + 6 * NSA_KV_W) // LANES
ODD_N = NSA_W + 6 * NSA_KV_W + LANES
NSA_GATES = 3

VMEM_LIMIT = 56 * 1024 * 1024
VT_ROWS = HEAD_DIM + BF16_SUBLANES
LOG2_E = float(np.log2(np.e))

_NT = (((1,), (1,)), ((), ()))
_TN = (((0,), (0,)), ((), ()))


def _cparams(sem):
    return pltpu.CompilerParams(dimension_semantics=sem, vmem_limit_bytes=VMEM_LIMIT)


def _rms(x, g):
    return x * lax.rsqrt(jnp.mean(x * x, axis=-1, keepdims=True) + NORM_EPS) * g


def _sigmoid(x):
    return 0.5 * jnp.tanh(0.5 * x) + 0.5


def _log_sigmoid(x):
    return -(jnp.maximum(-x, 0.0) + jnp.log1p(jnp.exp(-jnp.abs(x))))


def _rotate(x, cos2, sin2):
    return x * cos2 + pltpu.roll(x, HEAD_DIM // 2, axis=1) * sin2


def _flash_steps(qs, k_ts, vt_ts, biases, carries):
    n = len(qs)
    ss = [lax.dot_general(k_ts[u], qs[u], _NT, preferred_element_type=F32) + biases[u] for u in range(n)]
    ms, es, alphas, pvs = [], [], [], []

    def value_matmul(u):
        pv = jnp.dot(vt_ts[u], es[u], preferred_element_type=F32)
        if carries is not None:
            pv = alphas[u] * carries[2 * u + 1] + pv
        pvs.append(pv)

    for u in range(n):
        m_new = jnp.max(ss[u], axis=0, keepdims=True)
        if carries is not None:
            m_new = jnp.maximum(carries[2 * u], m_new)
            alphas.append(jnp.exp2(carries[2 * u] - m_new))
        ms.append(m_new)
        es.append(jnp.exp2(ss[u] - m_new).astype(BF16))
        if u > 0:
            value_matmul(u - 1)
    value_matmul(n - 1)
    out = []
    for u in range(n):
        out += [ms[u], pvs[u]]
    return out


def _flash_output(acc):
    return acc[:HEAD_DIM] / jnp.maximum(acc[HEAD_DIM:HEAD_DIM + 1], 1e-30)


def _store_transposed(dst_ref, x):
    n_rows, n_cols = x.shape
    for r in range(0, n_rows, LANES):
        dst_ref[:n_cols, r:r + LANES] = x[r:r + LANES, :].T.astype(dst_ref.dtype)
    dst_ref[n_cols:, :] = jnp.ones((dst_ref.shape[0] - n_cols, n_rows), dst_ref.dtype)


def _rank_before(v, n_valid, axis):
    idx = lax.broadcasted_iota(jnp.int32, v.shape, axis)
    rank = jnp.zeros(v.shape, F32)
    for m in range(n_valid):
        vm = v[m:m + 1, :] if axis == 0 else v[:, m:m + 1]
        tie = jnp.where(idx > m, 1.0, 0.0)
        rank = rank + jnp.where(vm > v, 1.0, jnp.where(vm == v, tie, 0.0))
    return rank


IN_PROJ_ROWS = 512
IN_PROJ_COLS = 512
GATE_ROWS = 2 * MLSTM_HEADS


def _norm_matmul_body(*refs, pieces, t_col):
    if t_col is None:
        x_ref, g_ref, w_ref, o_ref = refs
    else:
        x_ref, g_ref, w_ref, o_ref, ot_ref = refs
    yb = _rms(x_ref[...], g_ref[...]).astype(BF16)
    n_done = 0
    for src0, src1, dst0 in pieces:
        assert dst0 == n_done
        for c0 in range(src0, src1, IN_PROJ_COLS):
            c1 = min(src1, c0 + IN_PROJ_COLS)
            o_ref[:, n_done:n_done + c1 - c0] = jnp.dot(yb, w_ref[:, c0:c1], preferred_element_type=F32)
            n_done += c1 - c0
    if o_ref.shape[1] > n_done:
        o_ref[:, n_done:] = jnp.zeros((o_ref.shape[0], o_ref.shape[1] - n_done), F32)
    if t_col is not None:
        ot_ref[...] = o_ref[:, t_col:t_col + LANES].T[:ot_ref.shape[0]]


def _norm_matmul(x2d, gain, w, pieces=None, t_col=None):
    t, d = x2d.shape
    tm = IN_PROJ_ROWS
    n = -(-w.shape[1] // LANES) * LANES
    pieces = pieces or ((0, w.shape[1], 0),)
    in_specs = [pl.BlockSpec((tm, d), lambda i: (i, 0)),
                pl.BlockSpec((1, d), lambda i: (0, 0)),
                pl.BlockSpec(w.shape, lambda i: (0, 0))]
    out_specs = [pl.BlockSpec((tm, n), lambda i: (i, 0))]
    out_shape = [jax.ShapeDtypeStruct((t, n), F32)]
    if t_col is not None:
        out_specs.append(pl.BlockSpec((BF16_SUBLANES, tm), lambda i: (0, i)))
        out_shape.append(jax.ShapeDtypeStruct((BF16_SUBLANES, t), F32))
    outs = pl.pallas_call(
        functools.partial(_norm_matmul_body, pieces=pieces, t_col=t_col),
        grid=(t // tm,), in_specs=in_specs, out_specs=out_specs, out_shape=out_shape,
        compiler_params=_cparams(("parallel",)), name="norm_in_proj")(x2d, gain.reshape(1, d), w)
    return outs if t_col is not None else outs[0]


def _chunk_cumsum(x, axis):
    idx = lax.broadcasted_iota(jnp.int32, x.shape, axis) % MLSTM_CHUNK
    d = 1
    while d < MLSTM_CHUNK:
        x = x + jnp.where(idx >= d, pltpu.roll(x, d, axis=axis), 0.0)
        d *= 2
    return x


MLSTM_HEADS_PER_STEP = 2
MLSTM_GROUP = 32


def _mlstm_body(q_ref, k_ref, v_ref, og_ref, gt_ref, cwq_ref, cwk_ref, brow_ref,
                gain_ref, out_ref, ks, qts, vts, hts, css, brs, lirs):
    hp = pl.program_id(1)
    s_len = q_ref.shape[1]
    n_local = q_ref.shape[2] // HEAD_DIM
    n_chunks = s_len // MLSTM_CHUNK
    L = MLSTM_CHUNK
    per_slab = LANES // L
    row = lax.broadcasted_iota(jnp.int32, (s_len, HEAD_DIM), 0)
    cols = [slice(a * HEAD_DIM, (a + 1) * HEAD_DIM) for a in range(n_local)]

    def conv_silu(x, w):
        acc = x * w[MLSTM_CONV - 1:MLSTM_CONV, :]
        for d in range(1, MLSTM_CONV):
            shifted = jnp.where(row >= d, pltpu.roll(x, d, axis=0), 0.0)
            acc = acc + shifted * w[MLSTM_CONV - 1 - d:MLSTM_CONV - d, :]
        return acc * _sigmoid(acc)

    def store_chunks_transposed(dst, x):
        for p in range(s_len // LANES):
            slab_t = x[p * LANES:(p + 1) * LANES, :].T
            for j in range(per_slab):
                dst[per_slab * p + j] = slab_t[:, j * L:(j + 1) * L].astype(dst.dtype)

    pr = gt_ref[...] + brow_ref[...]
    pr = MLSTM_GATE_CAP * jnp.tanh(pr / MLSTM_GATE_CAP)
    sub = lax.broadcasted_iota(jnp.int32, pr.shape, 0)
    b_rows = _chunk_cumsum(_log_sigmoid(pr), 1)

    gains, key_rows = [], []
    for a in range(n_local):
        h = hp * n_local + a
        store_chunks_transposed(qts.at[a], conv_silu(q_ref[0, :, cols[a]], cwq_ref[:, cols[a]]) * (HEAD_DIM ** -0.5))
        store_chunks_transposed(vts.at[a], v_ref[0, :, cols[a]])
        ks[:, cols[a]] = conv_silu(k_ref[0, :, cols[a]], cwk_ref[:, cols[a]]).astype(BF16)
        li_row = jnp.sum(jnp.where(sub == h, pr, 0.0), axis=0, keepdims=True)
        b_row = jnp.sum(jnp.where(sub == h + MLSTM_HEADS, b_rows, 0.0), axis=0, keepdims=True)
        for c in range(n_chunks):
            brs[a, c] = b_row[:, c * L:(c + 1) * L]
            lirs[a, c] = li_row[:, c * L:(c + 1) * L]
        key_rows.append(li_row - b_row)
        gains.append(jnp.broadcast_to(gain_ref[a * HEAD_DIM:(a + 1) * HEAD_DIM, :], (HEAD_DIM, L)))
    key_rows = jnp.concatenate(key_rows + [jnp.zeros((LANES - n_local, s_len), F32)], axis=0)
    for p in range(s_len // LANES):
        slab_t = key_rows[:, p * LANES:(p + 1) * LANES].T
        for a in range(n_local):
            css[a, p * LANES:(p + 1) * LANES, :] = slab_t[:, a:a + 1]

    tri = lax.broadcasted_iota(jnp.int32, (L, L), 0) <= lax.broadcasted_iota(jnp.int32, (L, L), 1)

    group_size = MLSTM_GROUP
    assert n_chunks % group_size == 0

    def group(gi, carry):
        states = [list(carry[3 * a:3 * a + 3]) for a in range(n_local)]
        cs = [gi * group_size + j for j in range(group_size)]
        r0s = [pl.multiple_of(c * L, L) for c in cs]
        units = [(a, j) for a in range(n_local) for j in range(group_size)]
        ks_ = {(a, j): ks[pl.ds(r0s[j], L), cols[a]] for a, j in units}
        q_ts = {(a, j): qts[a, cs[j]] for a, j in units}
        v_ts = {(a, j): vts[a, cs[j]] for a, j in units}
        b_rs = {(a, j): brs[a, cs[j]] for a, j in units}
        kvs, ksums, kqs, g_maxs, b_lasts = {}, {}, {}, {}, {}
        for u in units:
            a, j = u
            b_last = b_rs[u][:, L - 1:L]
            g = b_last - b_rs[u] + lirs[a, cs[j]]
            g_max = jnp.max(g, axis=1, keepdims=True)
            w = jnp.exp(g - g_max)
            kvs[u] = jnp.dot((v_ts[u] * w).astype(BF16), ks_[u], preferred_element_type=F32)
            ksums[u] = jnp.dot(jnp.broadcast_to(w, (BF16_SUBLANES, L)).astype(BF16), ks_[u],
                               preferred_element_type=F32)[:1]
            kqs[u] = jnp.dot(ks_[u], q_ts[u], preferred_element_type=F32)
            g_maxs[u] = g_max
            b_lasts[u] = b_last
        c_ins, n_ins, m_ins = {}, {}, {}
        for u in units:
            a, j = u
            c_state, n_state, m_state = states[a]
            c_ins[u] = c_state.astype(BF16)
            n_ins[u] = jnp.broadcast_to(n_state, (BF16_SUBLANES, HEAD_DIM)).astype(BF16)
            m_ins[u] = m_state
            m_new = jnp.maximum(b_lasts[u] + m_state, g_maxs[u])
            sa = jnp.exp(b_lasts[u] + m_state - m_new)
            cc = jnp.exp(g_maxs[u] - m_new)
            states[a] = [sa * c_state + cc * kvs[u], sa * n_state + cc * ksums[u], m_new]
        c_qs = {u: jnp.dot(c_ins[u], q_ts[u], preferred_element_type=F32) for u in units}
        n_qs = {u: jnp.dot(n_ins[u], q_ts[u], preferred_element_type=F32)[:1] for u in units}
        ss, m_ts, w_inters = {}, {}, {}
        for u in units:
            a, j = u
            dmat = jnp.where(tri, b_rs[u] + css[a, pl.ds(r0s[j], L), :], NEG_INF)
            inter = b_rs[u] + m_ins[u]
            m_t = jnp.maximum(inter, jnp.max(dmat, axis=0, keepdims=True))
            ss[u] = kqs[u] * jnp.exp(dmat - m_t)
            m_ts[u] = m_t
            w_inters[u] = jnp.exp(inter - m_t)
        svs = {u: jnp.dot(v_ts[u].astype(BF16), ss[u].astype(BF16), preferred_element_type=F32) for u in units}
        for u in units:
            a, j = u
            num = svs[u] + w_inters[u] * c_qs[u]
            den = jnp.sum(ss[u], axis=0, keepdims=True) + w_inters[u] * n_qs[u]
            ht = num / jnp.maximum(jnp.abs(den), jnp.exp(-m_ts[u]))
            hts[a, cs[j]] = ht * lax.rsqrt(jnp.mean(ht * ht, axis=0, keepdims=True) + NORM_EPS) * gains[a]
        return tuple(x for st in states for x in st)

    init = (jnp.zeros((HEAD_DIM, HEAD_DIM), F32), jnp.zeros((1, HEAD_DIM), F32), jnp.zeros((1, 1), F32)) * n_local
    lax.fori_loop(0, n_chunks // group_size, group, init)

    for a in range(n_local):
        for p in range(s_len // LANES):
            slab_t = jnp.concatenate([hts[a, per_slab * p + j] for j in range(per_slab)], axis=1)
            rows = slice(p * LANES, (p + 1) * LANES)
            out_ref[0, rows, cols[a]] = (slab_t.T * _sigmoid(og_ref[0, rows, cols[a]])).astype(out_ref.dtype)


def _mlstm(z3, gt, conv_w, gate_b, head_gain):
    b, s_len, _ = z3.shape
    n_chunks = s_len // MLSTM_CHUNK
    gr = gt.shape[0]
    brow = jnp.pad(gate_b, (0, gr - 2 * MLSTM_HEADS)).reshape(gr, 1)
    n_local = MLSTM_HEADS_PER_STEP
    n_steps = MLSTM_HEADS // n_local
    width = n_local * HEAD_DIM

    def col(off):
        return pl.BlockSpec((1, s_len, width), lambda bi, hi: (bi, 0, off * n_steps + hi))

    in_specs = [col(0), col(1), col(2), col(3),
                pl.BlockSpec((gr, s_len), lambda bi, hi: (0, bi)),
                pl.BlockSpec((MLSTM_CONV, width), lambda bi, hi: (0, hi)),
                pl.BlockSpec((MLSTM_CONV, width), lambda bi, hi: (0, n_steps + hi)),
                pl.BlockSpec((gr, 1), lambda bi, hi: (0, 0)),
                pl.BlockSpec((width, 1), lambda bi, hi: (hi, 0))]
    chunk_t = (n_local, n_chunks, HEAD_DIM, MLSTM_CHUNK)
    rows_t = (n_local, n_chunks, 1, MLSTM_CHUNK)
    scratch = [pltpu.VMEM((s_len, width), BF16),
               pltpu.VMEM(chunk_t, BF16), pltpu.VMEM(chunk_t, F32), pltpu.VMEM(chunk_t, F32),
               pltpu.VMEM((n_local, s_len, 1), F32),
               pltpu.VMEM(rows_t, F32), pltpu.VMEM(rows_t, F32)]
    return pl.pallas_call(
        _mlstm_body, grid=(b, n_steps), in_specs=in_specs,
        out_specs=pl.BlockSpec((1, s_len, width), lambda bi, hi: (bi, 0, hi)),
        out_shape=jax.ShapeDtypeStruct((b, s_len, MLSTM_W), BF16),
        scratch_shapes=scratch, compiler_params=_cparams(("parallel", "parallel")),
        name="mlstm")(z3, z3, z3, z3, gt, conv_w, conv_w, brow, head_gain.reshape(MLSTM_W, 1))


MOBA_ROWS = 2


def _moba_body(q_ref, k_ref, v_ref, cos_ref, sin_ref, o_ref, kr_scr, vt_scr, km_scr, sel_scr):
    i = pl.program_id(1)
    n_rows, s_len = k_ref.shape[0], k_ref.shape[1]
    bs = MOBA_BLOCK
    nb = s_len // bs
    heads = [slice(h * HEAD_DIM, (h + 1) * HEAD_DIM) for h in range(MOBA_HEADS)]
    units = [(b, h) for b in range(n_rows) for h in range(MOBA_HEADS)]

    @pl.when(i == 0)
    def _():
        for u, (b, h) in enumerate(units):
            _store_transposed(vt_scr.at[u], v_ref[b, :, heads[h]])
            kr = _rotate(k_ref[b, :, heads[h]], cos_ref[...], sin_ref[...])
            kr_scr[b, :, heads[h]] = kr.astype(BF16)
            rows = [jnp.sum(kr[n * bs:(n + 1) * bs, :], axis=0, keepdims=True) / float(bs) for n in range(nb)]
            rows.append(jnp.zeros((km_scr.shape[1] - nb, HEAD_DIM), F32))
            km_scr[u] = jnp.concatenate(rows, axis=0)

    t0 = pl.multiple_of(i * bs, bs)
    cos_q = cos_ref[pl.ds(t0, bs), :]
    sin_q = sin_ref[pl.ds(t0, bs), :]
    blk = lax.broadcasted_iota(jnp.int32, (km_scr.shape[1], bs), 0)
    causal_bias = jnp.where(lax.broadcasted_iota(jnp.int32, (bs, bs), 0)
                            <= lax.broadcasted_iota(jnp.int32, (bs, bs), 1), 0.0, NEG_INF)

    qbs = []
    for u, (b, h) in enumerate(units):
        qr = _rotate(q_ref[b, :, heads[h]], cos_q, sin_q)
        gate_t = lax.dot_general(km_scr[u], qr, _NT, precision=HIGHEST, preferred_element_type=F32)
        val = jnp.where(blk < i, gate_t, NEG_INF)
        picked = jnp.where(val > 0.5 * NEG_INF, _rank_before(val, nb, 0), float(nb)) < MOBA_TOPK
        sel_scr[u] = jnp.where(picked, 0.0, NEG_INF)
        qbs.append((qr * (HEAD_DIM ** -0.5 * LOG2_E)).astype(BF16))

    def key_tiles(k0, n_blocks):
        return ([kr_scr[b, pl.ds(k0, n_blocks * bs), heads[h]] for b, h in units],
                [vt_scr[u, :, pl.ds(k0, n_blocks * bs)] for u in range(len(units))])

    init = tuple(_flash_steps(qbs, *key_tiles(t0, 1), [causal_bias] * len(units), None))

    def past_blocks(j, n_blocks, carry):
        biases = [jnp.concatenate([jnp.broadcast_to(sel_scr[u, pl.ds(j + d, 1), :], (bs, bs))
                                   for d in range(n_blocks)], axis=0) for u in range(len(units))]
        k0 = j * bs if isinstance(j, int) else pl.multiple_of(j * bs, bs)
        return tuple(_flash_steps(qbs, *key_tiles(k0, n_blocks), biases, carry))

    odd = i % 2
    carry = lax.cond(odd == 1, lambda c: past_blocks(0, 1, c), lambda c: c, init)
    fin = lax.fori_loop(0, i // 2, lambda p, c: past_blocks(odd + 2 * p, 2, c), carry)
    for u, (b, h) in enumerate(units):
        o_ref[b, :, heads[h]] = _flash_output(fin[2 * u + 1]).T.astype(o_ref.dtype)


def _moba(z3, cos2, sin2):
    b, s_len, _ = z3.shape
    nb = s_len // MOBA_BLOCK
    q_off = 4 * MLSTM_W // MOBA_W
    rows = MOBA_ROWS if b % MOBA_ROWS == 0 else 1
    n_units = rows * MOBA_HEADS

    def kv(off):
        return pl.BlockSpec((rows, s_len, MOBA_W), lambda bi, i: (bi, 0, off))

    in_specs = [pl.BlockSpec((rows, MOBA_BLOCK, MOBA_W), lambda bi, i: (bi, i, q_off)),
                kv(q_off + 1), kv(q_off + 2),
                pl.BlockSpec((s_len, HEAD_DIM), lambda bi, i: (0, 0)),
                pl.BlockSpec((s_len, HEAD_DIM), lambda bi, i: (0, 0))]
    scratch = [pltpu.VMEM((rows, s_len, MOBA_W), BF16),
               pltpu.VMEM((n_units, VT_ROWS, s_len), BF16),
               pltpu.VMEM((n_units, BF16_SUBLANES, HEAD_DIM), F32),
               pltpu.VMEM((n_units, BF16_SUBLANES, MOBA_BLOCK), F32)]
    return pl.pallas_call(
        _moba_body, grid=(b // rows, nb), in_specs=in_specs,
        out_specs=pl.BlockSpec((rows, MOBA_BLOCK, MOBA_W), lambda bi, i: (bi, i, 0)),
        out_shape=jax.ShapeDtypeStruct((b, s_len, MOBA_W), BF16),
        scratch_shapes=scratch, compiler_params=_cparams(("parallel", "arbitrary")),
        name="moba")(z3, z3, z3, cos2, sin2)


def _gelu_tanh(x):
    return x * (0.5 * (1.0 + jnp.tanh(np.sqrt(2.0 / np.pi) * (x + 0.044715 * (x * x * x)))))


def _nsa_compress_body(kc_ref, vc_ref, pe_ref, wk1_ref, wk2_ref, wv1_ref, wv2_ref, ko_ref, vo_ref):
    n_rows = kc_ref.shape[1] // CMP_STRIDE
    halves = CMP_LEN // CMP_STRIDE
    assert halves == 2

    def compress(x_ref, pe, w1_ref, w2_ref):
        rows = [x_ref[0, pl.ds(l, n_rows, stride=CMP_STRIDE), :] for l in range(CMP_STRIDE)]
        half = CMP_STRIDE * HEAD_DIM
        first = jnp.concatenate([(r + pe[l:l + 1, :]).astype(BF16) for l, r in enumerate(rows)], axis=1)
        second = jnp.concatenate([(r + pe[CMP_STRIDE + l:CMP_STRIDE + l + 1, :]).astype(BF16)
                                  for l, r in enumerate(rows)], axis=1)
        ya = jnp.dot(first, w1_ref[:half, :], preferred_element_type=F32)
        yb = jnp.dot(second, w1_ref[half:, :], preferred_element_type=F32)
        pre = ya + pltpu.roll(yb, n_rows - 1, axis=0)
        return jnp.dot(_gelu_tanh(pre).astype(BF16), w2_ref[...], preferred_element_type=F32)

    ko_ref[0, 0] = compress(kc_ref, pe_ref[0], wk1_ref, wk2_ref)
    vo_ref[0, 0] = compress(vc_ref, pe_ref[1], wv1_ref, wv2_ref)


def _nsa_compress(z3, cmp_pos, wk1, wk2, wv1, wv2):
    b, s_len, _ = z3.shape
    n_rows = s_len // CMP_STRIDE
    kc_off = NSA_W // LANES
    vc_off = kc_off + NSA_GROUPS
    const2 = lambda bi, gi: (0, 0)
    in_specs = [pl.BlockSpec((1, s_len, HEAD_DIM), lambda bi, gi: (bi, 0, kc_off + gi)),
                pl.BlockSpec((1, s_len, HEAD_DIM), lambda bi, gi: (bi, 0, vc_off + gi)),
                pl.BlockSpec(cmp_pos.shape, lambda bi, gi: (0, 0, 0)),
                pl.BlockSpec(wk1.shape, const2), pl.BlockSpec(wk2.shape, const2),
                pl.BlockSpec(wv1.shape, const2), pl.BlockSpec(wv2.shape, const2)]
    out_spec = pl.BlockSpec((1, 1, n_rows, HEAD_DIM), lambda bi, gi: (bi, gi, 0, 0))
    out_sds = jax.ShapeDtypeStruct((b, NSA_GROUPS, n_rows, HEAD_DIM), F32)
    return pl.pallas_call(
        _nsa_compress_body, grid=(b, NSA_GROUPS), in_specs=in_specs,
        out_specs=[out_spec, out_spec], out_shape=[out_sds, out_sds],
        compiler_params=_cparams(("parallel", "parallel")), name="nsa_compress")(
            z3, z3, cmp_pos, wk1, wk2, wv1, wv2)


def _nsa_body(q_ref, ks_ref, vs_ref, kw_ref, vw_ref, kc_ref, vc_ref, gate_ref, cos_ref, sin_ref, ov_ref,
              o_ref, krs, vst, krw, vwt, sel_scr):
    i = pl.program_id(1)
    tq = NSA_TQ
    tk = NSA_TQ
    scale = HEAD_DIM ** -0.5
    n_sel_blk = ks_ref.shape[1] // SEL_BLOCK
    n_cmp = (ks_ref.shape[1] - CMP_LEN) // CMP_STRIDE + 1
    shift = SEL_BLOCK.bit_length() - 1
    assert 1 << shift == SEL_BLOCK and n_sel_blk <= LANES
    blk_per_tile = tk // SEL_BLOCK
    groups = [slice(g * HEAD_DIM, (g + 1) * HEAD_DIM) for g in range(NSA_GROUPS)]

    @pl.when(i == 0)
    def _():
        for g, cols in enumerate(groups):
            krs[:, cols] = _rotate(ks_ref[0, :, cols], cos_ref[...], sin_ref[...]).astype(BF16)
            krw[:, cols] = _rotate(kw_ref[0, :, cols], cos_ref[...], sin_ref[...]).astype(BF16)
            _store_transposed(vst.at[g], vs_ref[0, :, cols])
            _store_transposed(vwt.at[g], vw_ref[0, :, cols])

    t0 = pl.multiple_of(i * tq, tq)
    pos_row = t0 + lax.broadcasted_iota(jnp.int32, (1, tq), 1)
    cos_q = cos_ref[pl.ds(t0, tq), :]
    sin_q = sin_ref[pl.ds(t0, tq), :]

    n_col = lax.broadcasted_iota(jnp.int32, (LANES, 1), 0)
    cmp_end = jnp.where(n_col < n_cmp, n_col * CMP_STRIDE + (CMP_LEN - 1), jnp.iinfo(jnp.int32).max)
    cmp_ok = cmp_end <= pos_row
    blk_i = lax.broadcasted_iota(jnp.int32, (n_sel_blk, tq), 0)
    behind = (pos_row >> shift) - blk_i
    units = [(g, r) for g in range(NSA_GROUPS) for r in range(NSA_REP)]
    qr_heads, o_cmp = [], []
    for g in range(NSA_GROUPS):
        kc = kc_ref[0, g].astype(BF16)
        vc = vc_ref[0, g].astype(BF16)
        p_sum = jnp.zeros((LANES, tq), F32)
        for r in range(NSA_REP):
            head = (g * NSA_REP + r) * HEAD_DIM
            q = q_ref[0, :, head:head + HEAD_DIM] * scale
            qr_heads.append(_rotate(q * LOG2_E, cos_q, sin_q).astype(BF16))
            s = lax.dot_general(kc, q.astype(BF16), _NT, preferred_element_type=F32)
            s = jnp.where(cmp_ok, s, NEG_INF)
            e = jnp.where(cmp_ok, jnp.exp(s - jnp.max(s, axis=0, keepdims=True)), 0.0)
            p = e / jnp.maximum(jnp.sum(e, axis=0, keepdims=True), 1e-30)
            o_cmp.append(lax.dot_general(p.astype(BF16), vc, _TN, preferred_element_type=F32))
            p_sum = p_sum + p
        imp = jnp.dot(ov_ref[...], p_sum, precision=HIGHEST, preferred_element_type=F32)
        val = jnp.where(behind == 0, FORCED_SCORE, jnp.where(behind == 1, FORCED_SCORE, imp))
        val = jnp.where(blk_i == 0, FORCED_SCORE, val)
        val = jnp.where(behind >= 0, val, NEG_INF)
        picked = jnp.where(val > 0.5 * NEG_INF, _rank_before(val, n_sel_blk, 0), float(n_sel_blk)) < SEL_TOPN
        sel_scr[g] = jnp.where(picked, 0.0, NEG_INF)

    def sel_bias(g, k0, n_tiles):
        blk0 = (k0 >> shift) if isinstance(k0, int) else pl.multiple_of(k0 >> shift, blk_per_tile)
        return jnp.concatenate([jnp.broadcast_to(sel_scr[g, pl.ds(blk0 + b, 1), :], (SEL_BLOCK, tq))
                                for b in range(n_tiles * blk_per_tile)], axis=0)

    def sel_tiles(k0, n_tiles, carry):
        biases = [sel_bias(g, k0, n_tiles) for g in range(NSA_GROUPS)]
        return tuple(_flash_steps(qr_heads, [krs[pl.ds(k0, n_tiles * tk), groups[g]] for g, _ in units],
                                  [vst[g, :, pl.ds(k0, n_tiles * tk)] for g, _ in units],
                                  [biases[g] for g, _ in units], carry))

    causal_bias = jnp.where(lax.broadcasted_iota(jnp.int32, (tk, tq), 0)
                            <= lax.broadcasted_iota(jnp.int32, (tk, tq), 1), 0.0, NEG_INF)
    span = WINDOW + tq
    start = pl.multiple_of(jnp.maximum(t0 - WINDOW, 0), tq)
    gap = ((t0 - start) + lax.broadcasted_iota(jnp.int32, (span, tq), 1)
           - lax.broadcasted_iota(jnp.int32, (span, tq), 0))
    win_bias = jnp.where(gap >= 0, jnp.where(gap < WINDOW, 0.0, NEG_INF), NEG_INF)
    diag_biases = [sel_bias(g, t0, 1) + causal_bias for g in range(NSA_GROUPS)]
    n = len(units)
    group = _flash_steps(qr_heads * 2,
                         [krs[pl.ds(t0, tk), groups[g]] for g, _ in units]
                         + [krw[pl.ds(start, span), groups[g]] for g, _ in units],
                         [vst[g, :, pl.ds(t0, tk)] for g, _ in units]
                         + [vwt[g, :, pl.ds(start, span)] for g, _ in units],
                         [diag_biases[g] for g, _ in units] + [win_bias] * n, None)
    win_fin = group[2 * n:]

    def past_tiles(n_past):
        def run(c):
            k0 = 0
            if n_past % 2:
                c = sel_tiles(0, 1, c)
                k0 = tk
            for p in range(n_past // 2):
                c = sel_tiles(k0 + 2 * p * tk, 2, c)
            return c
        return run

    sel_fin = lax.switch(i, [past_tiles(t) for t in range(ks_ref.shape[1] // tq)], tuple(group[:2 * n]))

    gates = _sigmoid(gate_ref[0])
    gates_t = gates.T
    for u in range(n):
        c0 = NSA_GATES * u
        o_t = (gates_t[c0 + 1:c0 + 2, :] * _flash_output(sel_fin[2 * u + 1])
               + gates_t[c0 + 2:c0 + 3, :] * _flash_output(win_fin[2 * u + 1]))
        o_ref[0, :, u * HEAD_DIM:(u + 1) * HEAD_DIM] = (gates[:, c0:c0 + 1] * o_cmp[u] + o_t.T).astype(o_ref.dtype)


def _nsa_attention(z3, k_cmp, v_cmp, cos2, sin2):
    b, s_len, _ = z3.shape
    assert s_len >= WINDOW + NSA_TQ and s_len % NSA_TQ == 0
    n_cmp = (s_len - CMP_LEN) // CMP_STRIDE + 1
    n_blk = s_len // SEL_BLOCK
    n_rows = k_cmp.shape[2]
    assert n_rows == LANES
    cmp_start = np.arange(n_rows) * CMP_STRIDE
    cmp_end = cmp_start + CMP_LEN - 1
    blk_lo = np.arange(LANES) * SEL_BLOCK
    overlap = ((cmp_start[:, None] <= blk_lo[None, :] + SEL_BLOCK - 1) & (cmp_end[:, None] >= blk_lo[None, :])
               & (np.arange(n_rows)[:, None] < n_cmp)).astype(np.float32)
    overlap_t = np.ascontiguousarray(overlap.T[:n_blk])
    base = NSA_W // NSA_KV_W

    def kv(off):
        return pl.BlockSpec((1, s_len, NSA_KV_W), lambda bi, i: (bi, 0, base + off))

    cmp_spec = pl.BlockSpec((1, NSA_GROUPS, n_rows, HEAD_DIM), lambda bi, i: (bi, 0, 0, 0))
    const2 = lambda bi, i: (0, 0)
    in_specs = [pl.BlockSpec((1, NSA_TQ, NSA_W), lambda bi, i: (bi, i, 0)),
                kv(2), kv(3), kv(4), kv(5), cmp_spec, cmp_spec,
                pl.BlockSpec((1, NSA_TQ, LANES), lambda bi, i: (bi, i, ODD_GATE_BLK)),
                pl.BlockSpec((s_len, HEAD_DIM), const2), pl.BlockSpec((s_len, HEAD_DIM), const2),
                pl.BlockSpec((n_blk, n_rows), const2)]
    vt_shape = (NSA_GROUPS, VT_ROWS, s_len)
    scratch = [pltpu.VMEM((s_len, NSA_KV_W), BF16), pltpu.VMEM(vt_shape, BF16),
               pltpu.VMEM((s_len, NSA_KV_W), BF16), pltpu.VMEM(vt_shape, BF16),
               pltpu.VMEM((NSA_GROUPS, n_blk, NSA_TQ), F32)]
    return pl.pallas_call(
        _nsa_body, grid=(b, s_len // NSA_TQ), in_specs=in_specs,
        out_specs=pl.BlockSpec((1, NSA_TQ, NSA_W), lambda bi, i: (bi, i, 0)),
        out_shape=jax.ShapeDtypeStruct((b, s_len, NSA_W), BF16),
        scratch_shapes=scratch, compiler_params=_cparams(("parallel", "arbitrary")),
        name="nsa_attention")(z3, z3, z3, z3, z3, k_cmp, v_cmp, z3, cos2, sin2, jnp.asarray(overlap_t))


MOE_ROWS = 1024


def _split_bf16(x):
    hi = x.astype(BF16)
    return hi, (x - hi.astype(F32)).astype(BF16)


def _proj_moe_body(*refs, n_act, final_norm):
    res_ref = refs[0]
    a_refs = refs[1:1 + n_act]
    wo_refs = refs[1 + n_act:1 + 2 * n_act]
    (g_ref, wr_ref, br_ref, w1_ref, w3_ref, w2_ref, fg_ref, o_ref,
     xn_scr, cw_scr, hd_scr) = refs[1 + 2 * n_act:]
    j = pl.program_id(1)
    tm = res_ref.shape[0]
    lane = lax.broadcasted_iota(jnp.int32, (tm, LANES), 1).astype(F32)
    neg = float("-inf")

    @pl.when(j == 0)
    def _():
        h = res_ref[...]
        for a_ref, wo_ref in zip(a_refs, wo_refs):
            h = h + jnp.dot(a_ref[...], wo_ref[...], preferred_element_type=F32)
        o_ref[...] = h
        xn = _rms(h, g_ref[...])
        xh, xl = _split_bf16(xn)
        xn_scr[...] = xh
        hi_terms = jnp.dot(xh, wr_ref[...], preferred_element_type=F32)
        logits = (hi_terms[:, :LANES] + (jnp.dot(xl, wr_ref[:, :LANES], preferred_element_type=F32)
                                         + hi_terms[:, LANES:])) + br_ref[...]
        lt = logits.T[:ROUTER_SLOTS]
        slot = lax.broadcasted_iota(jnp.int32, lt.shape, 0).astype(F32)
        is_g = slot < MOE_GROUPS
        gl = jnp.where(is_g, lt, neg)
        g_max = jnp.max(gl, axis=0, keepdims=True)
        g_w = 1.0 / jnp.sum(jnp.where(is_g, jnp.exp(gl - g_max), 0.0), axis=0, keepdims=True)
        g_top = jnp.min(jnp.where(gl == g_max, slot, float(LANES)), axis=0, keepdims=True)
        lo = MOE_GROUPS + MOE_EPG * g_top
        el = jnp.where(slot >= lo, jnp.where(slot < lo + MOE_EPG, lt, neg), neg)
        v1 = jnp.max(el, axis=0, keepdims=True)
        i1 = jnp.min(jnp.where(el == v1, slot, float(LANES)), axis=0, keepdims=True)
        el2 = jnp.where(slot == i1, neg, el)
        v2 = jnp.max(el2, axis=0, keepdims=True)
        i2 = jnp.min(jnp.where(el2 == v2, slot, float(LANES)), axis=0, keepdims=True)
        e2 = jnp.exp(v2 - v1)
        den = 1.0 + e2
        cw_t = jnp.where(slot == i1, g_w / den, 0.0) + jnp.where(slot == i2, g_w * e2 / den, 0.0)
        cw_scr[...] = jnp.concatenate([cw_t, jnp.zeros((LANES - ROUTER_SLOTS, tm), F32)], axis=0).T

    xn = xn_scr[...]
    cw = cw_scr[...]
    for r in range(MOE_EPG):
        h1 = jnp.dot(xn, w1_ref[r].astype(BF16), preferred_element_type=F32)
        h3 = jnp.dot(xn, w3_ref[r], preferred_element_type=F32)
        e_lane = (MOE_GROUPS + MOE_EPG * j + r).astype(F32)
        col = jnp.sum(jnp.where(lane == e_lane, cw, 0.0), axis=1, keepdims=True)
        hd_scr[:, r * MOE_HIDDEN:(r + 1) * MOE_HIDDEN] = ((h1 * _sigmoid(h1)) * h3 * col).astype(BF16)
    o_ref[...] += jnp.dot(hd_scr[...], w2_ref[...].astype(BF16), preferred_element_type=F32)

    if final_norm:
        @pl.when(j == pl.num_programs(1) - 1)
        def _():
            o_ref[...] = _rms(o_ref[...], fg_ref[...])


def _proj_moe(res2d, acts, w_out, gain, w_g, b_g, w_e, b_e, w1, w3, w2, final_gain, final_norm):
    t, d = res2d.shape
    tm = MOE_ROWS
    n_act = len(acts)
    n_slots = MOE_GROUPS + MOE_EXPERTS
    wr = jnp.pad(jnp.concatenate([w_g, w_e], axis=1), ((0, 0), (0, LANES - n_slots)))
    wr_cat = jnp.concatenate(_split_bf16(wr), axis=1)
    br = jnp.pad(jnp.concatenate([b_g, b_e]), (0, LANES - n_slots)).reshape(1, LANES)
    tile = lambda i, j: (i, 0)
    const = lambda i, j: (0, 0)
    group_hidden = MOE_EPG * MOE_HIDDEN
    in_specs = [pl.BlockSpec((tm, d), tile)]
    in_specs += [pl.BlockSpec((tm, a.shape[1]), tile) for a in acts]
    row0 = np.cumsum([0] + [a.shape[1] for a in acts])
    assert all(r % a.shape[1] == 0 for r, a in zip(row0, acts)) and row0[-1] == w_out.shape[0]
    in_specs += [pl.BlockSpec((a.shape[1], d), functools.partial(lambda i, j, blk: (blk, 0), blk=int(r // a.shape[1])))
                 for r, a in zip(row0, acts)]
    in_specs += [pl.BlockSpec((1, d), const),
                 pl.BlockSpec((d, 2 * LANES), const),
                 pl.BlockSpec((1, LANES), const),
                 pl.BlockSpec((MOE_EPG, d, MOE_HIDDEN), lambda i, j: (j, 0, 0)),
                 pl.BlockSpec((MOE_EPG, d, MOE_HIDDEN), lambda i, j: (j, 0, 0)),
                 pl.BlockSpec((group_hidden, d), lambda i, j: (j, 0)),
                 pl.BlockSpec((1, d), const)]
    scratch = [pltpu.VMEM((tm, d), BF16), pltpu.VMEM((tm, LANES), F32), pltpu.VMEM((tm, group_hidden), BF16)]
    return pl.pallas_call(
        functools.partial(_proj_moe_body, n_act=n_act, final_norm=final_norm),
        grid=(t // tm, MOE_GROUPS), in_specs=in_specs,
        out_specs=pl.BlockSpec((tm, d), tile),
        out_shape=jax.ShapeDtypeStruct((t, d), F32),
        scratch_shapes=scratch, compiler_params=_cparams(("parallel", "arbitrary")),
        name="out_proj_moe")(res2d, *acts, *([w_out] * n_act), gain.reshape(1, d), wr_cat, br,
                             w1, w3.astype(BF16), w2.reshape(MOE_EXPERTS * MOE_HIDDEN, d),
                             final_gain.reshape(1, d))


def _rope_tables(s_len):
    half = HEAD_DIM // 2
    inv = ROPE_THETA ** (-(jnp.arange(half, dtype=F32) / half))
    ang = jnp.arange(s_len, dtype=F32)[:, None] * inv[None, :]
    cos, sin = jnp.cos(ang), jnp.sin(ang)
    return jnp.concatenate([cos, cos], axis=-1), jnp.concatenate([-sin, sin], axis=-1)


def _even_weights(w_in):
    a = 4 * MLSTM_W
    pieces = ((0, a, 0), (a + GATE_ROWS, w_in.shape[1], a), (a, a + GATE_ROWS, w_in.shape[1] - GATE_ROWS))
    return w_in.astype(BF16), pieces


def _odd_weights(w_in):
    a = NSA_W + 6 * NSA_KV_W
    n_gates = NSA_HEADS * NSA_GATES
    assert w_in.shape[1] == a + n_gates and n_gates <= LANES
    return w_in.astype(BF16)


def kernel(x, mix_norm_0, w_in_0, mlstm_conv_0, mlstm_gate_b_0, mlstm_head_norm_0, w_out_0, ffn_norm_0, router_group_0, router_group_b_0, router_expert_0, router_expert_b_0, moe_w1_0, moe_w3_0, moe_w2_0, mix_norm_1, w_in_1, nsa_cmp_pos_1, nsa_cmp_k1_1, nsa_cmp_k2_1, nsa_cmp_v1_1, nsa_cmp_v2_1, w_out_1, ffn_norm_1, router_group_1, router_group_b_1, router_expert_1, router_expert_b_1, moe_w1_1, moe_w3_1, moe_w2_1, final_norm):
    b, s_len, d = x.shape
    t = b * s_len
    cos2, sin2 = _rope_tables(s_len)
    x2d = x.reshape(t, d)

    w_main, pieces = _even_weights(w_in_0)
    z0, gt = _norm_matmul(x2d, mix_norm_0, w_main, pieces, t_col=EVEN_IF_BLK * LANES)
    z0 = z0.reshape(b, s_len, EVEN_N)
    h_m = _mlstm(z0, gt, mlstm_conv_0, mlstm_gate_b_0, mlstm_head_norm_0)
    o_b = _moba(z0, cos2, sin2)
    h = _proj_moe(x2d, [h_m.reshape(t, MLSTM_W), o_b.reshape(t, MOBA_W)], w_out_0.astype(BF16),
                  ffn_norm_0, router_group_0, router_group_b_0, router_expert_0, router_expert_b_0,
                  moe_w1_0, moe_w3_0, moe_w2_0, final_norm, False)

    z1 = _norm_matmul(h, mix_norm_1, _odd_weights(w_in_1)).reshape(b, s_len, ODD_N)
    k_cmp, v_cmp = _nsa_compress(z1, nsa_cmp_pos_1, nsa_cmp_k1_1.astype(BF16), nsa_cmp_k2_1.astype(BF16),
                                 nsa_cmp_v1_1.astype(BF16), nsa_cmp_v2_1.astype(BF16))
    o = _nsa_attention(z1, k_cmp, v_cmp, cos2, sin2)
    h = _proj_moe(h, [o.reshape(t, NSA_W)], w_out_1.astype(BF16),
                  ffn_norm_1, router_group_1, router_group_b_1, router_expert_1, router_expert_b_1,
                  moe_w1_1, moe_w3_1, moe_w2_1, final_norm, True)
    return h.reshape(b, s_len, d)
```

```python
import functools

import numpy as np
import jax
import jax.numpy as jnp
from jax import lax
from jax.experimental import pallas as pl
from jax.experimental.pallas import tpu as pltpu

F32 = jnp.float32
BF16 = jnp.bfloat16
HIGHEST = lax.Precision.HIGHEST

LANES = 128
BF16_SUBLANES = 16
D_MODEL = 1024
HEAD_DIM = 128
ROPE_THETA = 10000.0
NORM_EPS = 1e-6
NEG_INF = -1e30
FORCED_SCORE = 1e4

MLSTM_HEADS = 4
MLSTM_W = MLSTM_HEADS * HEAD_DIM
MLSTM_CHUNK = 64
MLSTM_CONV = 4
MLSTM_GATE_CAP = 15.0
MOBA_HEADS = 4
MOBA_W = MOBA_HEADS * HEAD_DIM
MOBA_BLOCK = 256
MOBA_TOPK = 3

NSA_HEADS = 8
NSA_GROUPS = 2
NSA_REP = NSA_HEADS // NSA_GROUPS
NSA_W = NSA_HEADS * HEAD_DIM
NSA_KV_W = NSA_GROUPS * HEAD_DIM
CMP_LEN = 32
CMP_STRIDE = 16
SEL_BLOCK = 64
SEL_TOPN = 8
WINDOW = 512
NSA_TQ = 256

MOE_GROUPS = 4
MOE_EPG = 4
MOE_EXPERTS = MOE_GROUPS * MOE_EPG
MOE_HIDDEN = D_MODEL // 4
ROUTER_SLOTS = 32

EVEN_N = 4 * MLSTM_W + 3 * MOBA_W + LANES
EVEN_IF_BLK = (4 * MLSTM_W + 3 * MOBA_W) // LANES
ODD_GATE_BLK = (NSA_W + 6 * NSA_KV_W) // LANES
ODD_N = NSA_W + 6 * NSA_KV_W + LANES
NSA_GATES = 3

VMEM_LIMIT = 56 * 1024 * 1024
VT_ROWS = HEAD_DIM + BF16_SUBLANES
LOG2_E = float(np.log2(np.e))

_NT = (((1,), (1,)), ((), ()))
_TN = (((0,), (0,)), ((), ()))


def _cparams(sem):
    return pltpu.CompilerParams(dimension_semantics=sem, vmem_limit_bytes=VMEM_LIMIT)


def _rms(x, g):
    return x * lax.rsqrt(jnp.mean(x * x, axis=-1, keepdims=True) + NORM_EPS) * g


def _sigmoid(x):
    return 0.5 * jnp.tanh(0.5 * x) + 0.5


def _log_sigmoid(x):
    return -(jnp.maximum(-x, 0.0) + jnp.log1p(jnp.exp(-jnp.abs(x))))


def _rotate(x, cos2, sin2):
    return x * cos2 + pltpu.roll(x, HEAD_DIM // 2, axis=1) * sin2


def _flash_steps(qs, k_ts, vt_ts, biases, carries):
    n = len(qs)
    ss = [lax.dot_general(k_ts[u], qs[u], _NT, preferred_element_type=F32) + biases[u] for u in range(n)]
    ms, es, alphas, pvs = [], [], [], []

    def value_matmul(u):
        pv = jnp.dot(vt_ts[u], es[u], preferred_element_type=F32)
        if carries is not None:
            pv = alphas[u] * carries[2 * u + 1] + pv
        pvs.append(pv)

    for u in range(n):
        m_new = jnp.max(ss[u], axis=0, keepdims=True)
        if carries is not None:
            m_new = jnp.maximum(carries[2 * u], m_new)
            alphas.append(jnp.exp2(carries[2 * u] - m_new))
        ms.append(m_new)
        es.append(jnp.exp2(ss[u] - m_new).astype(BF16))
        if u > 0:
            value_matmul(u - 1)
    value_matmul(n - 1)
    out = []
    for u in range(n):
        out += [ms[u], pvs[u]]
    return out


def _flash_output(acc):
    return acc[:HEAD_DIM] / jnp.maximum(acc[HEAD_DIM:HEAD_DIM + 1], 1e-30)


def _store_transposed(dst_ref, x):
    n_rows, n_cols = x.shape
    for r in range(0, n_rows, LANES):
        dst_ref[:n_cols, r:r + LANES] = x[r:r + LANES, :].T.astype(dst_ref.dtype)
    dst_ref[n_cols:, :] = jnp.ones((dst_ref.shape[0] - n_cols, n_rows), dst_ref.dtype)


def _rank_before(v, n_valid, axis):
    idx = lax.broadcasted_iota(jnp.int32, v.shape, axis)
    rank = jnp.zeros(v.shape, F32)
    for m in range(n_valid):
        vm = v[m:m + 1, :] if axis == 0 else v[:, m:m + 1]
        tie = jnp.where(idx > m, 1.0, 0.0)
        rank = rank + jnp.where(vm > v, 1.0, jnp.where(vm == v, tie, 0.0))
    return rank


IN_PROJ_ROWS = 512
IN_PROJ_COLS = 512
GATE_ROWS = 2 * MLSTM_HEADS


def _norm_matmul_body(*refs, pieces, t_col):
    if t_col is None:
        x_ref, g_ref, w_ref, o_ref = refs
    else:
        x_ref, g_ref, w_ref, o_ref, ot_ref = refs
    yb = _rms(x_ref[...], g_ref[...]).astype(BF16)
    n_done = 0
    for src0, src1, dst0 in pieces:
        assert dst0 == n_done
        for c0 in range(src0, src1, IN_PROJ_COLS):
            c1 = min(src1, c0 + IN_PROJ_COLS)
            o_ref[:, n_done:n_done + c1 - c0] = jnp.dot(yb, w_ref[:, c0:c1], preferred_element_type=F32)
            n_done += c1 - c0
    if o_ref.shape[1] > n_done:
        o_ref[:, n_done:] = jnp.zeros((o_ref.shape[0], o_ref.shape[1] - n_done), F32)
    if t_col is not None:
        ot_ref[...] = o_ref[:, t_col:t_col + LANES].T[:ot_ref.shape[0]]


def _norm_matmul(x2d, gain, w, pieces=None, t_col=None):
    t, d = x2d.shape
    tm = IN_PROJ_ROWS
    n = -(-w.shape[1] // LANES) * LANES
    pieces = pieces or ((0, w.shape[1], 0),)
    in_specs = [pl.BlockSpec((tm, d), lambda i: (i, 0)),
                pl.BlockSpec((1, d), lambda i: (0, 0)),
                pl.BlockSpec(w.shape, lambda i: (0, 0))]
    out_specs = [pl.BlockSpec((tm, n), lambda i: (i, 0))]
    out_shape = [jax.ShapeDtypeStruct((t, n), F32)]
    if t_col is not None:
        out_specs.append(pl.BlockSpec((BF16_SUBLANES, tm), lambda i: (0, i)))
        out_shape.append(jax.ShapeDtypeStruct((BF16_SUBLANES, t), F32))
    outs = pl.pallas_call(
        functools.partial(_norm_matmul_body, pieces=pieces, t_col=t_col),
        grid=(t // tm,), in_specs=in_specs, out_specs=out_specs, out_shape=out_shape,
        compiler_params=_cparams(("parallel",)), name="norm_in_proj")(x2d, gain.reshape(1, d), w)
    return outs if t_col is not None else outs[0]


def _chunk_cumsum(x, axis):
    idx = lax.broadcasted_iota(jnp.int32, x.shape, axis) % MLSTM_CHUNK
    d = 1
    while d < MLSTM_CHUNK:
        x = x + jnp.where(idx >= d, pltpu.roll(x, d, axis=axis), 0.0)
        d *= 2
    return x


MLSTM_HEADS_PER_STEP = 2
MLSTM_GROUP = 32


def _mlstm_body(q_ref, k_ref, v_ref, og_ref, gt_ref, cwq_ref, cwk_ref, brow_ref,
                gain_ref, out_ref, ks, qts, vts, hts, css, brs, lirs):
    hp = pl.program_id(1)
    s_len = q_ref.shape[1]
    n_local = q_ref.shape[2] // HEAD_DIM
    n_chunks = s_len // MLSTM_CHUNK
    L = MLSTM_CHUNK
    per_slab = LANES // L
    row = lax.broadcasted_iota(jnp.int32, (s_len, HEAD_DIM), 0)
    cols = [slice(a * HEAD_DIM, (a + 1) * HEAD_DIM) for a in range(n_local)]

    def conv_silu(x, w):
        acc = x * w[MLSTM_CONV - 1:MLSTM_CONV, :]
        for d in range(1, MLSTM_CONV):
            shifted = jnp.where(row >= d, pltpu.roll(x, d, axis=0), 0.0)
            acc = acc + shifted * w[MLSTM_CONV - 1 - d:MLSTM_CONV - d, :]
        return acc * _sigmoid(acc)

    def store_chunks_transposed(dst, x):
        for p in range(s_len // LANES):
            slab_t = x[p * LANES:(p + 1) * LANES, :].T
            for j in range(per_slab):
                dst[per_slab * p + j] = slab_t[:, j * L:(j + 1) * L].astype(dst.dtype)

    pr = gt_ref[...] + brow_ref[...]
    pr = MLSTM_GATE_CAP * jnp.tanh(pr / MLSTM_GATE_CAP)
    sub = lax.broadcasted_iota(jnp.int32, pr.shape, 0)
    b_rows = _chunk_cumsum(_log_sigmoid(pr), 1)

    gains, key_rows = [], []
    for a in range(n_local):
        h = hp * n_local + a
        store_chunks_transposed(qts.at[a], conv_silu(q_ref[0, :, cols[a]], cwq_ref[:, cols[a]]) * (HEAD_DIM ** -0.5))
        store_chunks_transposed(vts.at[a], v_ref[0, :, cols[a]])
        ks[:, cols[a]] = conv_silu(k_ref[0, :, cols[a]], cwk_ref[:, cols[a]]).astype(BF16)
        li_row = jnp.sum(jnp.where(sub == h, pr, 0.0), axis=0, keepdims=True)
        b_row = jnp.sum(jnp.where(sub == h + MLSTM_HEADS, b_rows, 0.0), axis=0, keepdims=True)
        for c in range(n_chunks):
            brs[a, c] = b_row[:, c * L:(c + 1) * L]
            lirs[a, c] = li_row[:, c * L:(c + 1) * L]
        key_rows.append(li_row - b_row)
        gains.append(jnp.broadcast_to(gain_ref[a * HEAD_DIM:(a + 1) * HEAD_DIM, :], (HEAD_DIM, L)))
    key_rows = jnp.concatenate(key_rows + [jnp.zeros((LANES - n_local, s_len), F32)], axis=0)
    for p in range(s_len // LANES):
        slab_t = key_rows[:, p * LANES:(p + 1) * LANES].T
        for a in range(n_local):
            css[a, p * LANES:(p + 1) * LANES, :] = slab_t[:, a:a + 1]

    tri = lax.broadcasted_iota(jnp.int32, (L, L), 0) <= lax.broadcasted_iota(jnp.int32, (L, L), 1)

    group_size = MLSTM_GROUP
    assert n_chunks % group_size == 0

    def group(gi, carry):
        states = [list(carry[3 * a:3 * a + 3]) for a in range(n_local)]
        cs = [gi * group_size + j for j in range(group_size)]
        r0s = [pl.multiple_of(c * L, L) for c in cs]
        units = [(a, j) for a in range(n_local) for j in range(group_size)]
        ks_ = {(a, j): ks[pl.ds(r0s[j], L), cols[a]] for a, j in units}
        q_ts = {(a, j): qts[a, cs[j]] for a, j in units}
        v_ts = {(a, j): vts[a, cs[j]] for a, j in units}
        b_rs = {(a, j): brs[a, cs[j]] for a, j in units}
        kvs, ksums, kqs, g_maxs, b_lasts = {}, {}, {}, {}, {}
        for u in units:
            a, j = u
            b_last = b_rs[u][:, L - 1:L]
            g = b_last - b_rs[u] + lirs[a, cs[j]]
            g_max = jnp.max(g, axis=1, keepdims=True)
            w = jnp.exp(g - g_max)
            kvs[u] = jnp.dot((v_ts[u] * w).astype(BF16), ks_[u], preferred_element_type=F32)
            ksums[u] = jnp.dot(jnp.broadcast_to(w, (BF16_SUBLANES, L)).astype(BF16), ks_[u],
                               preferred_element_type=F32)[:1]
            kqs[u] = jnp.dot(ks_[u], q_ts[u], preferred_element_type=F32)
            g_maxs[u] = g_max
            b_lasts[u] = b_last
        c_ins, n_ins, m_ins = {}, {}, {}
        for u in units:
            a, j = u
            c_state, n_state, m_state = states[a]
            c_ins[u] = c_state.astype(BF16)
            n_ins[u] = jnp.broadcast_to(n_state, (BF16_SUBLANES, HEAD_DIM)).astype(BF16)
            m_ins[u] = m_state
            m_new = jnp.maximum(b_lasts[u] + m_state, g_maxs[u])
            sa = jnp.exp(b_lasts[u] + m_state - m_new)
            cc = jnp.exp(g_maxs[u] - m_new)
            states[a] = [sa * c_state + cc * kvs[u], sa * n_state + cc * ksums[u], m_new]
        c_qs = {u: jnp.dot(c_ins[u], q_ts[u], preferred_element_type=F32) for u in units}
        n_qs = {u: jnp.dot(n_ins[u], q_ts[u], preferred_element_type=F32)[:1] for u in units}
        ss, m_ts, w_inters = {}, {}, {}
        for u in units:
            a, j = u
            dmat = jnp.where(tri, b_rs[u] + css[a, pl.ds(r0s[j], L), :], NEG_INF)
            inter = b_rs[u] + m_ins[u]
            m_t = jnp.maximum(inter, jnp.max(dmat, axis=0, keepdims=True))
            ss[u] = kqs[u] * jnp.exp(dmat - m_t)
            m_ts[u] = m_t
            w_inters[u] = jnp.exp(inter - m_t)
        svs = {u: jnp.dot(v_ts[u].astype(BF16), ss[u].astype(BF16), preferred_element_type=F32) for u in units}
        for u in units:
            a, j = u
            num = svs[u] + w_inters[u] * c_qs[u]
            den = jnp.sum(ss[u], axis=0, keepdims=True) + w_inters[u] * n_qs[u]
            ht = num / jnp.maximum(jnp.abs(den), jnp.exp(-m_ts[u]))
            hts[a, cs[j]] = ht * lax.rsqrt(jnp.mean(ht * ht, axis=0, keepdims=True) + NORM_EPS) * gains[a]
        return tuple(x for st in states for x in st)

    init = (jnp.zeros((HEAD_DIM, HEAD_DIM), F32), jnp.zeros((1, HEAD_DIM), F32), jnp.zeros((1, 1), F32)) * n_local
    lax.fori_loop(0, n_chunks // group_size, group, init)

    for a in range(n_local):
        for p in range(s_len // LANES):
            slab_t = jnp.concatenate([hts[a, per_slab * p + j] for j in range(per_slab)], axis=1)
            rows = slice(p * LANES, (p + 1) * LANES)
            out_ref[0, rows, cols[a]] = (slab_t.T * _sigmoid(og_ref[0, rows, cols[a]])).astype(out_ref.dtype)


def _mlstm(z3, gt, conv_w, gate_b, head_gain):
    b, s_len, _ = z3.shape
    n_chunks = s_len // MLSTM_CHUNK
    gr = gt.shape[0]
    brow = jnp.pad(gate_b, (0, gr - 2 * MLSTM_HEADS)).reshape(gr, 1)
    n_local = MLSTM_HEADS_PER_STEP
    n_steps = MLSTM_HEADS // n_local
    width = n_local * HEAD_DIM

    def col(off):
        return pl.BlockSpec((1, s_len, width), lambda bi, hi: (bi, 0, off * n_steps + hi))

    in_specs = [col(0), col(1), col(2), col(3),
                pl.BlockSpec((gr, s_len), lambda bi, hi: (0, bi)),
                pl.BlockSpec((MLSTM_CONV, width), lambda bi, hi: (0, hi)),
                pl.BlockSpec((MLSTM_CONV, width), lambda bi, hi: (0, n_steps + hi)),
                pl.BlockSpec((gr, 1), lambda bi, hi: (0, 0)),
                pl.BlockSpec((width, 1), lambda bi, hi: (hi, 0))]
    chunk_t = (n_local, n_chunks, HEAD_DIM, MLSTM_CHUNK)
    rows_t = (n_local, n_chunks, 1, MLSTM_CHUNK)
    scratch = [pltpu.VMEM((s_len, width), BF16),
               pltpu.VMEM(chunk_t, BF16), pltpu.VMEM(chunk_t, F32), pltpu.VMEM(chunk_t, F32),
               pltpu.VMEM((n_local, s_len, 1), F32),
               pltpu.VMEM(rows_t, F32), pltpu.VMEM(rows_t, F32)]
    return pl.pallas_call(
        _mlstm_body, grid=(b, n_steps), in_specs=in_specs,
        out_specs=pl.BlockSpec((1, s_len, width), lambda bi, hi: (bi, 0, hi)),
        out_shape=jax.ShapeDtypeStruct((b, s_len, MLSTM_W), BF16),
        scratch_shapes=scratch, compiler_params=_cparams(("parallel", "parallel")),
        name="mlstm")(z3, z3, z3, z3, gt, conv_w, conv_w, brow, head_gain.reshape(MLSTM_W, 1))


MOBA_ROWS = 2


def _moba_body(q_ref, k_ref, v_ref, cos_ref, sin_ref, o_ref, kr_scr, vt_scr, km_scr, sel_scr):
    i = pl.program_id(1)
    n_rows, s_len = k_ref.shape[0], k_ref.shape[1]
    bs = MOBA_BLOCK
    nb = s_len // bs
    heads = [slice(h * HEAD_DIM, (h + 1) * HEAD_DIM) for h in range(MOBA_HEADS)]
    units = [(b, h) for b in range(n_rows) for h in range(MOBA_HEADS)]

    @pl.when(i == 0)
    def _():
        for u, (b, h) in enumerate(units):
            _store_transposed(vt_scr.at[u], v_ref[b, :, heads[h]])
            kr = _rotate(k_ref[b, :, heads[h]], cos_ref[...], sin_ref[...])
            kr_scr[b, :, heads[h]] = kr.astype(BF16)
            rows = [jnp.sum(kr[n * bs:(n + 1) * bs, :], axis=0, keepdims=True) / float(bs) for n in range(nb)]
            rows.append(jnp.zeros((km_scr.shape[1] - nb, HEAD_DIM), F32))
            km_scr[u] = jnp.concatenate(rows, axis=0)

    t0 = pl.multiple_of(i * bs, bs)
    cos_q = cos_ref[pl.ds(t0, bs), :]
    sin_q = sin_ref[pl.ds(t0, bs), :]
    blk = lax.broadcasted_iota(jnp.int32, (km_scr.shape[1], bs), 0)
    causal_bias = jnp.where(lax.broadcasted_iota(jnp.int32, (bs, bs), 0)
                            <= lax.broadcasted_iota(jnp.int32, (bs, bs), 1), 0.0, NEG_INF)

    qbs = []
    for u, (b, h) in enumerate(units):
        qr = _rotate(q_ref[b, :, heads[h]], cos_q, sin_q)
        gate_t = lax.dot_general(km_scr[u], qr, _NT, precision=HIGHEST, preferred_element_type=F32)
        val = jnp.where(blk < i, gate_t, NEG_INF)
        picked = jnp.where(val > 0.5 * NEG_INF, _rank_before(val, nb, 0), float(nb)) < MOBA_TOPK
        sel_scr[u] = jnp.where(picked, 0.0, NEG_INF)
        qbs.append((qr * (HEAD_DIM ** -0.5 * LOG2_E)).astype(BF16))

    def key_tiles(k0, n_blocks):
        return ([kr_scr[b, pl.ds(k0, n_blocks * bs), heads[h]] for b, h in units],
                [vt_scr[u, :, pl.ds(k0, n_blocks * bs)] for u in range(len(units))])

    init = tuple(_flash_steps(qbs, *key_tiles(t0, 1), [causal_bias] * len(units), None))

    def past_blocks(j, n_blocks, carry):
        biases = [jnp.concatenate([jnp.broadcast_to(sel_scr[u, pl.ds(j + d, 1), :], (bs, bs))
                                   for d in range(n_blocks)], axis=0) for u in range(len(units))]
        k0 = j * bs if isinstance(j, int) else pl.multiple_of(j * bs, bs)
        return tuple(_flash_steps(qbs, *key_tiles(k0, n_blocks), biases, carry))

    odd = i % 2
    carry = lax.cond(odd == 1, lambda c: past_blocks(0, 1, c), lambda c: c, init)
    fin = lax.fori_loop(0, i // 2, lambda p, c: past_blocks(odd + 2 * p, 2, c), carry)
    for u, (b, h) in enumerate(units):
        o_ref[b, :, heads[h]] = _flash_output(fin[2 * u + 1]).T.astype(o_ref.dtype)


def _moba(z3, cos2, sin2):
    b, s_len, _ = z3.shape
    nb = s_len // MOBA_BLOCK
    q_off = 4 * MLSTM_W // MOBA_W
    rows = MOBA_ROWS if b % MOBA_ROWS == 0 else 1
    n_units = rows * MOBA_HEADS

    def kv(off):
        return pl.BlockSpec((rows, s_len, MOBA_W), lambda bi, i: (bi, 0, off))

    in_specs = [pl.BlockSpec((rows, MOBA_BLOCK, MOBA_W), lambda bi, i: (bi, i, q_off)),
                kv(q_off + 1), kv(q_off + 2),
                pl.BlockSpec((s_len, HEAD_DIM), lambda bi, i: (0, 0)),
                pl.BlockSpec((s_len, HEAD_DIM), lambda bi, i: (0, 0))]
    scratch = [pltpu.VMEM((rows, s_len, MOBA_W), BF16),
               pltpu.VMEM((n_units, VT_ROWS, s_len), BF16),
               pltpu.VMEM((n_units, BF16_SUBLANES, HEAD_DIM), F32),
               pltpu.VMEM((n_units, BF16_SUBLANES, MOBA_BLOCK), F32)]
    return pl.pallas_call(
        _moba_body, grid=(b // rows, nb), in_specs=in_specs,
        out_specs=pl.BlockSpec((rows, MOBA_BLOCK, MOBA_W), lambda bi, i: (bi, i, 0)),
        out_shape=jax.ShapeDtypeStruct((b, s_len, MOBA_W), BF16),
        scratch_shapes=scratch, compiler_params=_cparams(("parallel", "arbitrary")),
        name="moba")(z3, z3, z3, cos2, sin2)


def _gelu_tanh(x):
    return x * (0.5 * (1.0 + jnp.tanh(np.sqrt(2.0 / np.pi) * (x + 0.044715 * (x * x * x)))))


def _nsa_compress_body(kc_ref, vc_ref, pe_ref, wk1_ref, wk2_ref, wv1_ref, wv2_ref, ko_ref, vo_ref):
    n_rows = kc_ref.shape[1] // CMP_STRIDE
    halves = CMP_LEN // CMP_STRIDE
    assert halves == 2

    def compress(x_ref, pe, w1_ref, w2_ref):
        rows = [x_ref[0, pl.ds(l, n_rows, stride=CMP_STRIDE), :] for l in range(CMP_STRIDE)]
        half = CMP_STRIDE * HEAD_DIM
        first = jnp.concatenate([(r + pe[l:l + 1, :]).astype(BF16) for l, r in enumerate(rows)], axis=1)
        second = jnp.concatenate([(r + pe[CMP_STRIDE + l:CMP_STRIDE + l + 1, :]).astype(BF16)
                                  for l, r in enumerate(rows)], axis=1)
        ya = jnp.dot(first, w1_ref[:half, :], preferred_element_type=F32)
        yb = jnp.dot(second, w1_ref[half:, :], preferred_element_type=F32)
        pre = ya + pltpu.roll(yb, n_rows - 1, axis=0)
        return jnp.dot(_gelu_tanh(pre).astype(BF16), w2_ref[...], preferred_element_type=F32)

    ko_ref[0, 0] = compress(kc_ref, pe_ref[0], wk1_ref, wk2_ref)
    vo_ref[0, 0] = compress(vc_ref, pe_ref[1], wv1_ref, wv2_ref)


def _nsa_compress(z3, cmp_pos, wk1, wk2, wv1, wv2):
    b, s_len, _ = z3.shape
    n_rows = s_len // CMP_STRIDE
    kc_off = NSA_W // LANES
    vc_off = kc_off + NSA_GROUPS
    const2 = lambda bi, gi: (0, 0)
    in_specs = [pl.BlockSpec((1, s_len, HEAD_DIM), lambda bi, gi: (bi, 0, kc_off + gi)),
                pl.BlockSpec((1, s_len, HEAD_DIM), lambda bi, gi: (bi, 0, vc_off + gi)),
                pl.BlockSpec(cmp_pos.shape, lambda bi, gi: (0, 0, 0)),
                pl.BlockSpec(wk1.shape, const2), pl.BlockSpec(wk2.shape, const2),
                pl.BlockSpec(wv1.shape, const2), pl.BlockSpec(wv2.shape, const2)]
    out_spec = pl.BlockSpec((1, 1, n_rows, HEAD_DIM), lambda bi, gi: (bi, gi, 0, 0))
    out_sds = jax.ShapeDtypeStruct((b, NSA_GROUPS, n_rows, HEAD_DIM), F32)
    return pl.pallas_call(
        _nsa_compress_body, grid=(b, NSA_GROUPS), in_specs=in_specs,
        out_specs=[out_spec, out_spec], out_shape=[out_sds, out_sds],
        compiler_params=_cparams(("parallel", "parallel")), name="nsa_compress")(
            z3, z3, cmp_pos, wk1, wk2, wv1, wv2)


def _nsa_body(q_ref, ks_ref, vs_ref, kw_ref, vw_ref, kc_ref, vc_ref, gate_ref, cos_ref, sin_ref, ov_ref,
              o_ref, krs, vst, krw, vwt, sel_scr):
    i = pl.program_id(1)
    tq = NSA_TQ
    tk = NSA_TQ
    scale = HEAD_DIM ** -0.5
    n_sel_blk = ks_ref.shape[1] // SEL_BLOCK
    n_cmp = (ks_ref.shape[1] - CMP_LEN) // CMP_STRIDE + 1
    shift = SEL_BLOCK.bit_length() - 1
    assert 1 << shift == SEL_BLOCK and n_sel_blk <= LANES
    blk_per_tile = tk // SEL_BLOCK
    groups = [slice(g * HEAD_DIM, (g + 1) * HEAD_DIM) for g in range(NSA_GROUPS)]

    @pl.when(i == 0)
    def _():
        for g, cols in enumerate(groups):
            krs[:, cols] = _rotate(ks_ref[0, :, cols], cos_ref[...], sin_ref[...]).astype(BF16)
            krw[:, cols] = _rotate(kw_ref[0, :, cols], cos_ref[...], sin_ref[...]).astype(BF16)
            _store_transposed(vst.at[g], vs_ref[0, :, cols])
            _store_transposed(vwt.at[g], vw_ref[0, :, cols])

    t0 = pl.multiple_of(i * tq, tq)
    pos_row = t0 + lax.broadcasted_iota(jnp.int32, (1, tq), 1)
    cos_q = cos_ref[pl.ds(t0, tq), :]
    sin_q = sin_ref[pl.ds(t0, tq), :]

    n_col = lax.broadcasted_iota(jnp.int32, (LANES, 1), 0)
    cmp_end = jnp.where(n_col < n_cmp, n_col * CMP_STRIDE + (CMP_LEN - 1), jnp.iinfo(jnp.int32).max)
    cmp_ok = cmp_end <= pos_row
    blk_i = lax.broadcasted_iota(jnp.int32, (n_sel_blk, tq), 0)
    behind = (pos_row >> shift) - blk_i
    units = [(g, r) for g in range(NSA_GROUPS) for r in range(NSA_REP)]
    qr_heads, o_cmp = [], []
    for g in range(NSA_GROUPS):
        kc = kc_ref[0, g].astype(BF16)
        vc = vc_ref[0, g].astype(BF16)
        qs = []
        for r in range(NSA_REP):
            head = (g * NSA_REP + r) * HEAD_DIM
            q = q_ref[0, :, head:head + HEAD_DIM] * scale
            qr_heads.append(_rotate(q * LOG2_E, cos_q, sin_q).astype(BF16))
            qs.append(q.astype(BF16))
        ok = jnp.concatenate([cmp_ok] * NSA_REP, axis=1)
        s = lax.dot_general(kc, jnp.concatenate(qs, axis=0), _NT, preferred_element_type=F32)
        s = jnp.where(ok, s, NEG_INF)
        e = jnp.where(ok, jnp.exp(s - jnp.max(s, axis=0, keepdims=True)), 0.0)
        p = e / jnp.maximum(jnp.sum(e, axis=0, keepdims=True), 1e-30)
        o_all = lax.dot_general(p.astype(BF16), vc, _TN, preferred_element_type=F32)
        o_cmp += [o_all[r * tq:(r + 1) * tq] for r in range(NSA_REP)]
        p_sum = sum(p[:, r * tq:(r + 1) * tq] for r in range(NSA_REP))
        imp = jnp.dot(ov_ref[...], p_sum, precision=HIGHEST, preferred_element_type=F32)
        val = jnp.where(behind == 0, FORCED_SCORE, jnp.where(behind == 1, FORCED_SCORE, imp))
        val = jnp.where(blk_i == 0, FORCED_SCORE, val)
        val = jnp.where(behind >= 0, val, NEG_INF)
        picked = jnp.where(val > 0.5 * NEG_INF, _rank_before(val, n_sel_blk, 0), float(n_sel_blk)) < SEL_TOPN
        sel_scr[g] = jnp.where(picked, 0.0, NEG_INF)

    def sel_bias(g, k0, n_tiles):
        blk0 = (k0 >> shift) if isinstance(k0, int) else pl.multiple_of(k0 >> shift, blk_per_tile)
        return jnp.concatenate([jnp.broadcast_to(sel_scr[g, pl.ds(blk0 + b, 1), :], (SEL_BLOCK, tq))
                                for b in range(n_tiles * blk_per_tile)], axis=0)

    def sel_tiles(k0, n_tiles, carry):
        biases = [sel_bias(g, k0, n_tiles) for g in range(NSA_GROUPS)]
        return tuple(_flash_steps(qr_heads, [krs[pl.ds(k0, n_tiles * tk), groups[g]] for g, _ in units],
                                  [vst[g, :, pl.ds(k0, n_tiles * tk)] for g, _ in units],
                                  [biases[g] for g, _ in units], carry))

    causal_bias = jnp.where(lax.broadcasted_iota(jnp.int32, (tk, tq), 0)
                            <= lax.broadcasted_iota(jnp.int32, (tk, tq), 1), 0.0, NEG_INF)
    span = WINDOW + tq
    start = pl.multiple_of(jnp.maximum(t0 - WINDOW, 0), tq)
    gap = ((t0 - start) + lax.broadcasted_iota(jnp.int32, (span, tq), 1)
           - lax.broadcasted_iota(jnp.int32, (span, tq), 0))
    win_bias = jnp.where(gap >= 0, jnp.where(gap < WINDOW, 0.0, NEG_INF), NEG_INF)
    diag_biases = [sel_bias(g, t0, 1) + causal_bias for g in range(NSA_GROUPS)]
    n = len(units)
    group = _flash_steps(qr_heads * 2,
                         [krs[pl.ds(t0, tk), groups[g]] for g, _ in units]
                         + [krw[pl.ds(start, span), groups[g]] for g, _ in units],
                         [vst[g, :, pl.ds(t0, tk)] for g, _ in units]
                         + [vwt[g, :, pl.ds(start, span)] for g, _ in units],
                         [diag_biases[g] for g, _ in units] + [win_bias] * n, None)
    win_fin = group[2 * n:]

    odd = i % 2
    carry = lax.cond(odd == 1, lambda c: sel_tiles(0, 1, c), lambda c: c, tuple(group[:2 * n]))
    sel_fin = lax.fori_loop(0, i // 2, lambda p, c: sel_tiles(pl.multiple_of((odd + 2 * p) * tk, tk), 2, c), carry)

    gates = _sigmoid(gate_ref[0])
    gates_t = gates.T
    for u in range(n):
        c0 = NSA_GATES * u
        o_t = (gates_t[c0 + 1:c0 + 2, :] * _flash_output(sel_fin[2 * u + 1])
               + gates_t[c0 + 2:c0 + 3, :] * _flash_output(win_fin[2 * u + 1]))
        o_ref[0, :, u * HEAD_DIM:(u + 1) * HEAD_DIM] = (gates[:, c0:c0 + 1] * o_cmp[u] + o_t.T).astype(o_ref.dtype)


def _nsa_attention(z3, k_cmp, v_cmp, cos2, sin2):
    b, s_len, _ = z3.shape
    assert s_len >= WINDOW + NSA_TQ and s_len % NSA_TQ == 0
    n_cmp = (s_len - CMP_LEN) // CMP_STRIDE + 1
    n_blk = s_len // SEL_BLOCK
    n_rows = k_cmp.shape[2]
    assert n_rows == LANES
    cmp_start = np.arange(n_rows) * CMP_STRIDE
    cmp_end = cmp_start + CMP_LEN - 1
    blk_lo = np.arange(LANES) * SEL_BLOCK
    overlap = ((cmp_start[:, None] <= blk_lo[None, :] + SEL_BLOCK - 1) & (cmp_end[:, None] >= blk_lo[None, :])
               & (np.arange(n_rows)[:, None] < n_cmp)).astype(np.float32)
    overlap_t = np.ascontiguousarray(overlap.T[:n_blk])
    base = NSA_W // NSA_KV_W

    def kv(off):
        return pl.BlockSpec((1, s_len, NSA_KV_W), lambda bi, i: (bi, 0, base + off))

    cmp_spec = pl.BlockSpec((1, NSA_GROUPS, n_rows, HEAD_DIM), lambda bi, i: (bi, 0, 0, 0))
    const2 = lambda bi, i: (0, 0)
    in_specs = [pl.BlockSpec((1, NSA_TQ, NSA_W), lambda bi, i: (bi, i, 0)),
                kv(2), kv(3), kv(4), kv(5), cmp_spec, cmp_spec,
                pl.BlockSpec((1, NSA_TQ, LANES), lambda bi, i: (bi, i, ODD_GATE_BLK)),
                pl.BlockSpec((s_len, HEAD_DIM), const2), pl.BlockSpec((s_len, HEAD_DIM), const2),
                pl.BlockSpec((n_blk, n_rows), const2)]
    vt_shape = (NSA_GROUPS, VT_ROWS, s_len)
    scratch = [pltpu.VMEM((s_len, NSA_KV_W), BF16), pltpu.VMEM(vt_shape, BF16),
               pltpu.VMEM((s_len, NSA_KV_W), BF16), pltpu.VMEM(vt_shape, BF16),
               pltpu.VMEM((NSA_GROUPS, n_blk, NSA_TQ), F32)]
    return pl.pallas_call(
        _nsa_body, grid=(b, s_len // NSA_TQ), in_specs=in_specs,
        out_specs=pl.BlockSpec((1, NSA_TQ, NSA_W), lambda bi, i: (bi, i, 0)),
        out_shape=jax.ShapeDtypeStruct((b, s_len, NSA_W), BF16),
        scratch_shapes=scratch, compiler_params=_cparams(("parallel", "arbitrary")),
        name="nsa_attention")(z3, z3, z3, z3, z3, k_cmp, v_cmp, z3, cos2, sin2, jnp.asarray(overlap_t))


MOE_ROWS = 1024


def _split_bf16(x):
    hi = x.astype(BF16)
    return hi, (x - hi.astype(F32)).astype(BF16)


def _proj_moe_body(*refs, n_act, final_norm):
    res_ref = refs[0]
    a_refs = refs[1:1 + n_act]
    wo_refs = refs[1 + n_act:1 + 2 * n_act]
    (g_ref, wr_ref, br_ref, w1_ref, w3_ref, w2_ref, fg_ref, o_ref,
     xn_scr, cw_scr, hd_scr) = refs[1 + 2 * n_act:]
    j = pl.program_id(1)
    tm = res_ref.shape[0]
    lane = lax.broadcasted_iota(jnp.int32, (tm, LANES), 1).astype(F32)
    neg = float("-inf")

    @pl.when(j == 0)
    def _():
        h = res_ref[...]
        for a_ref, wo_ref in zip(a_refs, wo_refs):
            h = h + jnp.dot(a_ref[...], wo_ref[...], preferred_element_type=F32)
        o_ref[...] = h
        xn = _rms(h, g_ref[...])
        xh, xl = _split_bf16(xn)
        xn_scr[...] = xh
        hi_terms = jnp.dot(xh, wr_ref[...], preferred_element_type=F32)
        logits = (hi_terms[:, :LANES] + (jnp.dot(xl, wr_ref[:, :LANES], preferred_element_type=F32)
                                         + hi_terms[:, LANES:])) + br_ref[...]
        lt = logits.T[:ROUTER_SLOTS]
        slot = lax.broadcasted_iota(jnp.int32, lt.shape, 0).astype(F32)
        is_g = slot < MOE_GROUPS
        gl = jnp.where(is_g, lt, neg)
        g_max = jnp.max(gl, axis=0, keepdims=True)
        g_w = 1.0 / jnp.sum(jnp.where(is_g, jnp.exp(gl - g_max), 0.0), axis=0, keepdims=True)
        g_top = jnp.min(jnp.where(gl == g_max, slot, float(LANES)), axis=0, keepdims=True)
        lo = MOE_GROUPS + MOE_EPG * g_top
        el = jnp.where(slot >= lo, jnp.where(slot < lo + MOE_EPG, lt, neg), neg)
        v1 = jnp.max(el, axis=0, keepdims=True)
        i1 = jnp.min(jnp.where(el == v1, slot, float(LANES)), axis=0, keepdims=True)
        el2 = jnp.where(slot == i1, neg, el)
        v2 = jnp.max(el2, axis=0, keepdims=True)
        i2 = jnp.min(jnp.where(el2 == v2, slot, float(LANES)), axis=0, keepdims=True)
        e2 = jnp.exp(v2 - v1)
        den = 1.0 + e2
        cw_t = jnp.where(slot == i1, g_w / den, 0.0) + jnp.where(slot == i2, g_w * e2 / den, 0.0)
        cw_scr[...] = jnp.concatenate([cw_t, jnp.zeros((LANES - ROUTER_SLOTS, tm), F32)], axis=0).T

    xn = xn_scr[...]
    cw = cw_scr[...]
    for r in range(MOE_EPG):
        h1 = jnp.dot(xn, w1_ref[r].astype(BF16), preferred_element_type=F32)
        h3 = jnp.dot(xn, w3_ref[r], preferred_element_type=F32)
        e_lane = (MOE_GROUPS + MOE_EPG * j + r).astype(F32)
        col = jnp.sum(jnp.where(lane == e_lane, cw, 0.0), axis=1, keepdims=True)
        hd_scr[:, r * MOE_HIDDEN:(r + 1) * MOE_HIDDEN] = ((h1 * _sigmoid(h1)) * h3 * col).astype(BF16)
    o_ref[...] += jnp.dot(hd_scr[...], w2_ref[...].astype(BF16), preferred_element_type=F32)

    if final_norm:
        @pl.when(j == pl.num_programs(1) - 1)
        def _():
            o_ref[...] = _rms(o_ref[...], fg_ref[...])


def _proj_moe(res2d, acts, w_out, gain, w_g, b_g, w_e, b_e, w1, w3, w2, final_gain, final_norm):
    t, d = res2d.shape
    tm = MOE_ROWS
    n_act = len(acts)
    n_slots = MOE_GROUPS + MOE_EXPERTS
    wr = jnp.pad(jnp.concatenate([w_g, w_e], axis=1), ((0, 0), (0, LANES - n_slots)))
    wr_cat = jnp.concatenate(_split_bf16(wr), axis=1)
    br = jnp.pad(jnp.concatenate([b_g, b_e]), (0, LANES - n_slots)).reshape(1, LANES)
    tile = lambda i, j: (i, 0)
    const = lambda i, j: (0, 0)
    group_hidden = MOE_EPG * MOE_HIDDEN
    in_specs = [pl.BlockSpec((tm, d), tile)]
    in_specs += [pl.BlockSpec((tm, a.shape[1]), tile) for a in acts]
    row0 = np.cumsum([0] + [a.shape[1] for a in acts])
    assert all(r % a.shape[1] == 0 for r, a in zip(row0, acts)) and row0[-1] == w_out.shape[0]
    in_specs += [pl.BlockSpec((a.shape[1], d), functools.partial(lambda i, j, blk: (blk, 0), blk=int(r // a.shape[1])))
                 for r, a in zip(row0, acts)]
    in_specs += [pl.BlockSpec((1, d), const),
                 pl.BlockSpec((d, 2 * LANES), const),
                 pl.BlockSpec((1, LANES), const),
                 pl.BlockSpec((MOE_EPG, d, MOE_HIDDEN), lambda i, j: (j, 0, 0)),
                 pl.BlockSpec((MOE_EPG, d, MOE_HIDDEN), lambda i, j: (j, 0, 0)),
                 pl.BlockSpec((group_hidden, d), lambda i, j: (j, 0)),
                 pl.BlockSpec((1, d), const)]
    scratch = [pltpu.VMEM((tm, d), BF16), pltpu.VMEM((tm, LANES), F32), pltpu.VMEM((tm, group_hidden), BF16)]
    return pl.pallas_call(
        functools.partial(_proj_moe_body, n_act=n_act, final_norm=final_norm),
        grid=(t // tm, MOE_GROUPS), in_specs=in_specs,
        out_specs=pl.BlockSpec((tm, d), tile),
        out_shape=jax.ShapeDtypeStruct((t, d), F32),
        scratch_shapes=scratch, compiler_params=_cparams(("parallel", "arbitrary")),
        name="out_proj_moe")(res2d, *acts, *([w_out] * n_act), gain.reshape(1, d), wr_cat, br,
                             w1, w3.astype(BF16), w2.reshape(MOE_EXPERTS * MOE_HIDDEN, d),
                             final_gain.reshape(1, d))


def _rope_tables(s_len):
    half = HEAD_DIM // 2
    inv = ROPE_THETA ** (-(jnp.arange(half, dtype=F32) / half))
    ang = jnp.arange(s_len, dtype=F32)[:, None] * inv[None, :]
    cos, sin = jnp.cos(ang), jnp.sin(ang)
    return jnp.concatenate([cos, cos], axis=-1), jnp.concatenate([-sin, sin], axis=-1)


def _even_weights(w_in):
    a = 4 * MLSTM_W
    pieces = ((0, a, 0), (a + GATE_ROWS, w_in.shape[1], a), (a, a + GATE_ROWS, w_in.shape[1] - GATE_ROWS))
    return w_in.astype(BF16), pieces


def _odd_weights(w_in):
    a = NSA_W + 6 * NSA_KV_W
    n_gates = NSA_HEADS * NSA_GATES
    assert w_in.shape[1] == a + n_gates and n_gates <= LANES
    return w_in.astype(BF16)


def kernel(x, mix_norm_0, w_in_0, mlstm_conv_0, mlstm_gate_b_0, mlstm_head_norm_0, w_out_0, ffn_norm_0, router_group_0, router_group_b_0, router_expert_0, router_expert_b_0, moe_w1_0, moe_w3_0, moe_w2_0, mix_norm_1, w_in_1, nsa_cmp_pos_1, nsa_cmp_k1_1, nsa_cmp_k2_1, nsa_cmp_v1_1, nsa_cmp_v2_1, w_out_1, ffn_norm_1, router_group_1, router_group_b_1, router_expert_1, router_expert_b_1, moe_w1_1, moe_w3_1, moe_w2_1, final_norm):
    b, s_len, d = x.shape
    t = b * s_len
    cos2, sin2 = _rope_tables(s_len)
    x2d = x.reshape(t, d)

    w_main, pieces = _even_weights(w_in_0)
    z0, gt = _norm_matmul(x2d, mix_norm_0, w_main, pieces, t_col=EVEN_IF_BLK * LANES)
    z0 = z0.reshape(b, s_len, EVEN_N)
    h_m = _mlstm(z0, gt, mlstm_conv_0, mlstm_gate_b_0, mlstm_head_norm_0)
    o_b = _moba(z0, cos2, sin2)
    h = _proj_moe(x2d, [h_m.reshape(t, MLSTM_W), o_b.reshape(t, MOBA_W)], w_out_0.astype(BF16),
                  ffn_norm_0, router_group_0, router_group_b_0, router_expert_0, router_expert_b_0,
                  moe_w1_0, moe_w3_0, moe_w2_0, final_norm, False)

    z1 = _norm_matmul(h, mix_norm_1, _odd_weights(w_in_1)).reshape(b, s_len, ODD_N)
    k_cmp, v_cmp = _nsa_compress(z1, nsa_cmp_pos_1, nsa_cmp_k1_1.astype(BF16), nsa_cmp_k2_1.astype(BF16),
                                 nsa_cmp_v1_1.astype(BF16), nsa_cmp_v2_1.astype(BF16))
    o = _nsa_attention(z1, k_cmp, v_cmp, cos2, sin2)
    h = _proj_moe(h, [o.reshape(t, NSA_W)], w_out_1.astype(BF16),
                  ffn_norm_1, router_group_1, router_group_b_1, router_expert_1, router_expert_b_1,
                  moe_w1_1, moe_w3_1, moe_w2_1, final_norm, True)
    return h.reshape(b, s_len, d)
```
